```python
import math
import jax, jax.numpy as jnp
from jax import lax
import numpy as np

D_MODEL = 1024
BATCH = 8
SEQ = 8192
DEPTH = 2

MEM_LEN = 256
EPS = 1e-6
MLA_HEADS = 8
Q_LORA = 384
KV_LORA = 256
D_NOPE = 64
D_ROPE = 32
D_QK = D_NOPE + D_ROPE
D_V = 64
MLA_WIDTH = MLA_HEADS * D_V
ROPE_THETA = 10000.0
Q_BLOCK = 128
SSM_GROUPS = 32
SSM_GROUP_CH = 16
SSM_WIDTH = SSM_GROUPS * SSM_GROUP_CH
SSM_STATE = 64
DT_MIN = 1e-3
DT_MAX = 1e-1
X_HEADS = 4
X_HEAD_DIM = 128
X_WIDTH = X_HEADS * X_HEAD_DIM
N_BRANCH = 3
D_FF = 2816
CONV_WIDTH = 3
IN_WIDTH = Q_LORA + KV_LORA + D_ROPE + SSM_WIDTH + X_WIDTH + N_BRANCH * D_MODEL

kernel_name = "hybrid_mla_s5_memxattn_convffn"


def rmsnorm(x, g):
    xf = x.astype(jnp.float32)
    y = xf * lax.rsqrt(jnp.mean(xf * xf, axis=-1, keepdims=True) + EPS)
    return (y * g.astype(jnp.float32)).astype(x.dtype)


def rope_tables(positions):
    inv_freq = ROPE_THETA ** (-jnp.arange(0, D_ROPE, 2, dtype=jnp.float32) / D_ROPE)
    ang = positions.astype(jnp.float32)[..., None] * inv_freq
    return jnp.cos(ang), jnp.sin(ang)


def apply_rope(x, cos, sin):
    xf = x.astype(jnp.float32)
    x1, x2 = jnp.split(xf, 2, axis=-1)
    return jnp.concatenate([x1 * cos - x2 * sin, x1 * sin + x2 * cos], axis=-1).astype(x.dtype)


def split_combined(proj):
    sizes = (Q_LORA, KV_LORA, D_ROPE, SSM_WIDTH, X_WIDTH)
    idx = [int(v) for v in np.cumsum(sizes)]
    return jnp.split(proj, idx, axis=-1)


def causal_block_attention(q, k, v):
    b, l, h, dq = q.shape
    nb = l // Q_BLOCK
    scale = dq ** -0.5
    qb = jnp.moveaxis(q.reshape(b, nb, Q_BLOCK, h, dq), 1, 0)
    kpos = jnp.arange(l)

    def one_block(args):
        i, qi = args
        s = jnp.einsum('bqhd,bkhd->bhqk', qi, k, preferred_element_type=jnp.float32) * scale
        qpos = i * Q_BLOCK + jnp.arange(Q_BLOCK)
        s = jnp.where(kpos[None, :] <= qpos[:, None], s, -jnp.inf)
        p = jax.nn.softmax(s, axis=-1)
        return jnp.einsum('bhqk,bkhd->bqhd', p.astype(v.dtype), v)

    out = lax.map(one_block, (jnp.arange(nb), qb))
    return jnp.moveaxis(out, 0, 1).reshape(b, l, h * v.shape[-1])


def mla_branch(c_q, c_kv, k_r, cos, sin, q_a_norm_g, w_q_b, kv_a_norm_g, w_kv_b, q_norm_g, k_norm_g):
    b, l, _ = c_q.shape
    q = (rmsnorm(c_q, q_a_norm_g) @ w_q_b).reshape(b, l, MLA_HEADS, D_QK)
    kv = (rmsnorm(c_kv, kv_a_norm_g) @ w_kv_b).reshape(b, l, MLA_HEADS, D_NOPE + D_V)
    k_nope, v = kv[..., :D_NOPE], kv[..., D_NOPE:]
    k_rope = jnp.broadcast_to(k_r[:, :, None, :], (b, l, MLA_HEADS, D_ROPE))
    k = jnp.concatenate([k_nope, k_rope], axis=-1)
    q = rmsnorm(q, q_norm_g)
    k = rmsnorm(k, k_norm_g)
    c4, s4 = cos[:, :, None, :], sin[:, :, None, :]
    q = jnp.concatenate([q[..., :D_NOPE], apply_rope(q[..., D_NOPE:], c4, s4)], axis=-1)
    k = jnp.concatenate([k[..., :D_NOPE], apply_rope(k[..., D_NOPE:], c4, s4)], axis=-1)
    return causal_block_attention(q, k, v)


def _ssm_combine(left, right):
    a1r, a1i, b1r, b1i = left
    a2r, a2i, b2r, b2i = right
    return (a2r * a1r - a2i * a1i,
            a2r * a1i + a2i * a1r,
            a2r * b1r - a2i * b1i + b2r,
            a2r * b1i + a2i * b1r + b2i)


def s5_branch(u, lam_re, lam_im, log_dt, b_re, b_im, c_re, c_im, d_skip, w_glu, b_glu):
    f32 = jnp.float32
    bsz, l, _ = u.shape
    uf = u.astype(f32).reshape(bsz, l, SSM_GROUPS, SSM_GROUP_CH)
    dt = jnp.exp(log_dt.astype(f32))[:, None]
    lr, li = lam_re.astype(f32), lam_im.astype(f32)
    mag = jnp.exp(lr * dt)
    a_re, a_im = mag * jnp.cos(li * dt), mag * jnp.sin(li * dt)
    den = lr * lr + li * li
    e_re, e_im = a_re - 1.0, a_im
    f_re = ((e_re * lr + e_im * li) / den)[..., None]
    f_im = ((e_im * lr - e_re * li) / den)[..., None]
    br, bi = b_re.astype(f32), b_im.astype(f32)
    bb_re = f_re * br - f_im * bi
    bb_im = f_re * bi + f_im * br
    bu_re = jnp.einsum('blgc,gnc->blgn', uf, bb_re)
    bu_im = jnp.einsum('blgc,gnc->blgn', uf, bb_im)
    a_re_t = jnp.broadcast_to(a_re, bu_re.shape)
    a_im_t = jnp.broadcast_to(a_im, bu_im.shape)
    _, _, s_re, s_im = lax.associative_scan(_ssm_combine, (a_re_t, a_im_t, bu_re, bu_im), axis=1)
    y = (jnp.einsum('blgn,gcn->blgc', s_re, c_re.astype(f32))
         - jnp.einsum('blgn,gcn->blgc', s_im, c_im.astype(f32))
         + d_skip.astype(f32) * uf)
    y = jax.nn.gelu(y.reshape(bsz, l, SSM_WIDTH)).astype(u.dtype)
    return y * jax.nn.sigmoid(y @ w_glu + b_glu)


def cross_branch(x_q, mem, mem_norm_g, w_mem_kv, xq_norm_g, xk_norm_g):
    b, l, _ = x_q.shape
    kv = rmsnorm(mem, mem_norm_g) @ w_mem_kv
    k = kv[..., :X_WIDTH].reshape(b, MEM_LEN, X_HEADS, X_HEAD_DIM)
    v = kv[..., X_WIDTH:].reshape(b, MEM_LEN, X_HEADS, X_HEAD_DIM)
    q = rmsnorm(x_q.reshape(b, l, X_HEADS, X_HEAD_DIM), xq_norm_g)
    k = rmsnorm(k, xk_norm_g)
    s = jnp.einsum('blhd,bmhd->bhlm', q, k, preferred_element_type=jnp.float32) * (X_HEAD_DIM ** -0.5)
    p = jax.nn.softmax(s, axis=-1)
    return jnp.einsum('bhlm,bmhd->blhd', p.astype(v.dtype), v).reshape(b, l, X_WIDTH)


def causal_dwconv(x, w, bias):
    c = x.shape[-1]
    y = lax.conv_general_dilated(x, w[:, None, :].astype(x.dtype), window_strides=(1,),
                                 padding=((CONV_WIDTH - 1, 0),),
                                 dimension_numbers=('NWC', 'WIO', 'NWC'),
                                 feature_group_count=c)
    return y + bias


def _fwd_setup_inputs(seed: int = 0) -> dict:
    key = jax.random.key(seed)
    ks = iter(jax.random.split(key, 48))
    f32 = jnp.float32
    L = DEPTH

    def nrm(shape, fan_in):
        return jax.random.normal(next(ks), shape, f32) * (fan_in ** -0.5)

    def gain(shape):
        return 1.0 + 0.02 * jax.random.normal(next(ks), shape, f32)

    def small(shape):
        return 0.01 * jax.random.normal(next(ks), shape, f32)

    x = jax.random.normal(next(ks), (BATCH, SEQ, D_MODEL), f32)
    mem = jax.random.normal(next(ks), (BATCH, MEM_LEN, D_MODEL), f32)
    offset = jax.random.randint(next(ks), (BATCH, 1), 0, 1024, dtype=jnp.int32)
    positions = offset + jnp.arange(SEQ, dtype=jnp.int32)[None, :]
    n_idx = jnp.arange(SSM_STATE, dtype=f32)
    lam_re = -0.5 + small((L, SSM_GROUPS, SSM_STATE))
    lam_im = math.pi * n_idx + small((L, SSM_GROUPS, SSM_STATE))
    log_dt = jax.random.uniform(next(ks), (L, SSM_GROUPS), f32, math.log(DT_MIN), math.log(DT_MAX))
    return {
        "x": x,
        "mem": mem,
        "positions": positions,
        "norm_mix_g": gain((L, D_MODEL)),
        "w_in": nrm((L, D_MODEL, IN_WIDTH), D_MODEL),
        "q_a_norm_g": gain((L, Q_LORA)),
        "w_q_b": nrm((L, Q_LORA, MLA_HEADS * D_QK), Q_LORA),
        "kv_a_norm_g": gain((L, KV_LORA)),
        "w_kv_b": nrm((L, KV_LORA, MLA_HEADS * (D_NOPE + D_V)), KV_LORA),
        "q_norm_g": gain((L, D_QK)),
        "k_norm_g": gain((L, D_QK)),
        "w_o_mla": nrm((L, MLA_WIDTH, D_MODEL), MLA_WIDTH),
        "ssm_lambda_re": lam_re,
        "ssm_lambda_im": lam_im,
        "ssm_log_dt": log_dt,
        "ssm_b_re": nrm((L, SSM_GROUPS, SSM_STATE, SSM_GROUP_CH), 2 * SSM_GROUP_CH),
        "ssm_b_im": nrm((L, SSM_GROUPS, SSM_STATE, SSM_GROUP_CH), 2 * SSM_GROUP_CH),
        "ssm_c_re": nrm((L, SSM_GROUPS, SSM_GROUP_CH, SSM_STATE), SSM_STATE),
        "ssm_c_im": nrm((L, SSM_GROUPS, SSM_GROUP_CH, SSM_STATE), SSM_STATE),
        "ssm_d": jax.random.normal(next(ks), (L, SSM_GROUPS, SSM_GROUP_CH), f32),
        "w_glu": nrm((L, SSM_WIDTH, SSM_WIDTH), SSM_WIDTH),
        "b_glu": small((L, SSM_WIDTH)),
        "w_o_ssm": nrm((L, SSM_WIDTH, D_MODEL), SSM_WIDTH),
        "mem_norm_g": gain((L, D_MODEL)),
        "w_mem_kv": nrm((L, D_MODEL, 2 * X_WIDTH), D_MODEL),
        "xq_norm_g": gain((L, X_HEAD_DIM)),
        "xk_norm_g": gain((L, X_HEAD_DIM)),
        "w_o_cross": nrm((L, X_WIDTH, D_MODEL), X_WIDTH),
        "b_gate": small((L, N_BRANCH * D_MODEL)),
        "w_out": nrm((L, D_MODEL, D_MODEL), D_MODEL),
        "norm_ffn_g": gain((L, D_MODEL)),
        "w_up": nrm((L, D_MODEL, 2 * D_FF), D_MODEL),
        "conv_w": nrm((L, CONV_WIDTH, 2 * D_FF), CONV_WIDTH),
        "conv_b": small((L, 2 * D_FF)),
        "w_down": nrm((L, D_FF, D_MODEL), D_FF),
    }


def _fwd_reference(x, mem, positions, norm_mix_g, w_in, q_a_norm_g, w_q_b, kv_a_norm_g, w_kv_b,
              q_norm_g, k_norm_g, w_o_mla, ssm_lambda_re, ssm_lambda_im, ssm_log_dt,
              ssm_b_re, ssm_b_im, ssm_c_re, ssm_c_im, ssm_d, w_glu, b_glu, w_o_ssm,
              mem_norm_g, w_mem_kv, xq_norm_g, xk_norm_g, w_o_cross, b_gate, w_out,
              norm_ffn_g, w_up, conv_w, conv_b, w_down):
    bsz, l, _ = x.shape
    cos, sin = rope_tables(positions)
    for i in range(DEPTH):
        h = rmsnorm(x, norm_mix_g[i])
        c_q, c_kv, k_r, u_ssm, x_q, gate_logits = split_combined(h @ w_in[i])
        y_a = mla_branch(c_q, c_kv, k_r, cos, sin, q_a_norm_g[i], w_q_b[i], kv_a_norm_g[i],
                         w_kv_b[i], q_norm_g[i], k_norm_g[i]) @ w_o_mla[i]
        y_b = s5_branch(u_ssm, ssm_lambda_re[i], ssm_lambda_im[i], ssm_log_dt[i], ssm_b_re[i],
                        ssm_b_im[i], ssm_c_re[i], ssm_c_im[i], ssm_d[i], w_glu[i], b_glu[i]) @ w_o_ssm[i]
        y_c = cross_branch(x_q, mem, mem_norm_g[i], w_mem_kv[i], xq_norm_g[i], xk_norm_g[i]) @ w_o_cross[i]
        gates = jax.nn.sigmoid(gate_logits + b_gate[i]).reshape(bsz, l, N_BRANCH, D_MODEL)
        merged = gates[:, :, 0] * y_a + gates[:, :, 1] * y_b + gates[:, :, 2] * y_c
        x = x + merged @ w_out[i]
        h2 = rmsnorm(x, norm_ffn_g[i])
        up = causal_dwconv(h2 @ w_up[i], conv_w[i], conv_b[i])
        g_ff, v_ff = up[..., :D_FF], up[..., D_FF:]
        x = x + (jax.nn.silu(g_ff) * v_ff) @ w_down[i]
    return x


import jax as _jax
import jax.numpy as _jnp

TWIN_FORMAT = 'train_step'
FWD_PARAMS = ['x', 'mem', 'positions', 'norm_mix_g', 'w_in', 'q_a_norm_g', 'w_q_b', 'kv_a_norm_g', 'w_kv_b', 'q_norm_g', 'k_norm_g', 'w_o_mla', 'ssm_lambda_re', 'ssm_lambda_im', 'ssm_log_dt', 'ssm_b_re', 'ssm_b_im', 'ssm_c_re', 'ssm_c_im', 'ssm_d', 'w_glu', 'b_glu', 'w_o_ssm', 'mem_norm_g', 'w_mem_kv', 'xq_norm_g', 'xk_norm_g', 'w_o_cross', 'b_gate', 'w_out', 'norm_ffn_g', 'w_up', 'conv_w', 'conv_b', 'w_down']
TWIN_WEIGHTS = ['norm_mix_g', 'w_in', 'q_a_norm_g', 'w_q_b', 'kv_a_norm_g', 'w_kv_b', 'q_norm_g', 'k_norm_g', 'w_o_mla', 'ssm_lambda_re', 'ssm_lambda_im', 'ssm_log_dt', 'ssm_b_re', 'ssm_b_im', 'ssm_c_re', 'ssm_c_im', 'ssm_d', 'w_glu', 'b_glu', 'w_o_ssm', 'mem_norm_g', 'w_mem_kv', 'xq_norm_g', 'xk_norm_g', 'w_o_cross', 'b_gate', 'w_out', 'norm_ffn_g', 'w_up', 'conv_w', 'conv_b', 'w_down']
TWIN_DIFF_INPUT = 'x'
TWIN_INPUTS = ['x', 'mem', 'positions', 'norm_mix_g', 'w_in', 'q_a_norm_g', 'w_q_b', 'kv_a_norm_g', 'w_kv_b', 'q_norm_g', 'k_norm_g', 'w_o_mla', 'ssm_lambda_re', 'ssm_lambda_im', 'ssm_log_dt', 'ssm_b_re', 'ssm_b_im', 'ssm_c_re', 'ssm_c_im', 'ssm_d', 'w_glu', 'b_glu', 'w_o_ssm', 'mem_norm_g', 'w_mem_kv', 'xq_norm_g', 'xk_norm_g', 'w_o_cross', 'b_gate', 'w_out', 'norm_ffn_g', 'w_up', 'conv_w', 'conv_b', 'w_down', 'loss_target', 'm_norm_mix_g', 'm_w_in', 'm_q_a_norm_g', 'm_w_q_b', 'm_kv_a_norm_g', 'm_w_kv_b', 'm_q_norm_g', 'm_k_norm_g', 'm_w_o_mla', 'm_ssm_lambda_re', 'm_ssm_lambda_im', 'm_ssm_log_dt', 'm_ssm_b_re', 'm_ssm_b_im', 'm_ssm_c_re', 'm_ssm_c_im', 'm_ssm_d', 'm_w_glu', 'm_b_glu', 'm_w_o_ssm', 'm_mem_norm_g', 'm_w_mem_kv', 'm_xq_norm_g', 'm_xk_norm_g', 'm_w_o_cross', 'm_b_gate', 'm_w_out', 'm_norm_ffn_g', 'm_w_up', 'm_conv_w', 'm_conv_b', 'm_w_down', 'v_norm_mix_g', 'v_w_in', 'v_q_a_norm_g', 'v_w_q_b', 'v_kv_a_norm_g', 'v_w_kv_b', 'v_q_norm_g', 'v_k_norm_g', 'v_w_o_mla', 'v_ssm_lambda_re', 'v_ssm_lambda_im', 'v_ssm_log_dt', 'v_ssm_b_re', 'v_ssm_b_im', 'v_ssm_c_re', 'v_ssm_c_im', 'v_ssm_d', 'v_w_glu', 'v_b_glu', 'v_w_o_ssm', 'v_mem_norm_g', 'v_w_mem_kv', 'v_xq_norm_g', 'v_xk_norm_g', 'v_w_o_cross', 'v_b_gate', 'v_w_out', 'v_norm_ffn_g', 'v_w_up', 'v_conv_w', 'v_conv_b', 'v_w_down']
TWIN_OUTPUTS = ['loss', 'grad_x', 'grad_norm_mix_g', 'grad_w_in', 'grad_q_a_norm_g', 'grad_w_q_b', 'grad_kv_a_norm_g', 'grad_w_kv_b', 'grad_q_norm_g', 'grad_k_norm_g', 'grad_w_o_mla', 'grad_ssm_lambda_re', 'grad_ssm_lambda_im', 'grad_ssm_log_dt', 'grad_ssm_b_re', 'grad_ssm_b_im', 'grad_ssm_c_re', 'grad_ssm_c_im', 'grad_ssm_d', 'grad_w_glu', 'grad_b_glu', 'grad_w_o_ssm', 'grad_mem_norm_g', 'grad_w_mem_kv', 'grad_xq_norm_g', 'grad_xk_norm_g', 'grad_w_o_cross', 'grad_b_gate', 'grad_w_out', 'grad_norm_ffn_g', 'grad_w_up', 'grad_conv_w', 'grad_conv_b', 'grad_w_down', 'delta_norm_mix_g', 'delta_w_in', 'delta_q_a_norm_g', 'delta_w_q_b', 'delta_kv_a_norm_g', 'delta_w_kv_b', 'delta_q_norm_g', 'delta_k_norm_g', 'delta_w_o_mla', 'delta_ssm_lambda_re', 'delta_ssm_lambda_im', 'delta_ssm_log_dt', 'delta_ssm_b_re', 'delta_ssm_b_im', 'delta_ssm_c_re', 'delta_ssm_c_im', 'delta_ssm_d', 'delta_w_glu', 'delta_b_glu', 'delta_w_o_ssm', 'delta_mem_norm_g', 'delta_w_mem_kv', 'delta_xq_norm_g', 'delta_xk_norm_g', 'delta_w_o_cross', 'delta_b_gate', 'delta_w_out', 'delta_norm_ffn_g', 'delta_w_up', 'delta_conv_w', 'delta_conv_b', 'delta_w_down', 'new_m_norm_mix_g', 'new_m_w_in', 'new_m_q_a_norm_g', 'new_m_w_q_b', 'new_m_kv_a_norm_g', 'new_m_w_kv_b', 'new_m_q_norm_g', 'new_m_k_norm_g', 'new_m_w_o_mla', 'new_m_ssm_lambda_re', 'new_m_ssm_lambda_im', 'new_m_ssm_log_dt', 'new_m_ssm_b_re', 'new_m_ssm_b_im', 'new_m_ssm_c_re', 'new_m_ssm_c_im', 'new_m_ssm_d', 'new_m_w_glu', 'new_m_b_glu', 'new_m_w_o_ssm', 'new_m_mem_norm_g', 'new_m_w_mem_kv', 'new_m_xq_norm_g', 'new_m_xk_norm_g', 'new_m_w_o_cross', 'new_m_b_gate', 'new_m_w_out', 'new_m_norm_ffn_g', 'new_m_w_up', 'new_m_conv_w', 'new_m_conv_b', 'new_m_w_down', 'new_v_norm_mix_g', 'new_v_w_in', 'new_v_q_a_norm_g', 'new_v_w_q_b', 'new_v_kv_a_norm_g', 'new_v_w_kv_b', 'new_v_q_norm_g', 'new_v_k_norm_g', 'new_v_w_o_mla', 'new_v_ssm_lambda_re', 'new_v_ssm_lambda_im', 'new_v_ssm_log_dt', 'new_v_ssm_b_re', 'new_v_ssm_b_im', 'new_v_ssm_c_re', 'new_v_ssm_c_im', 'new_v_ssm_d', 'new_v_w_glu', 'new_v_b_glu', 'new_v_w_o_ssm', 'new_v_mem_norm_g', 'new_v_w_mem_kv', 'new_v_xq_norm_g', 'new_v_xk_norm_g', 'new_v_w_o_cross', 'new_v_b_gate', 'new_v_w_out', 'new_v_norm_ffn_g', 'new_v_w_up', 'new_v_conv_w', 'new_v_conv_b', 'new_v_w_down']
TWIN_LEAF_KINDS = {'loss': 'loss', 'grad_x': 'grad_x', 'grad_norm_mix_g': 'grad_w', 'grad_w_in': 'grad_w', 'grad_q_a_norm_g': 'grad_w', 'grad_w_q_b': 'grad_w', 'grad_kv_a_norm_g': 'grad_w', 'grad_w_kv_b': 'grad_w', 'grad_q_norm_g': 'grad_w', 'grad_k_norm_g': 'grad_w', 'grad_w_o_mla': 'grad_w', 'grad_ssm_lambda_re': 'grad_w', 'grad_ssm_lambda_im': 'grad_w', 'grad_ssm_log_dt': 'grad_w', 'grad_ssm_b_re': 'grad_w', 'grad_ssm_b_im': 'grad_w', 'grad_ssm_c_re': 'grad_w', 'grad_ssm_c_im': 'grad_w', 'grad_ssm_d': 'grad_w', 'grad_w_glu': 'grad_w', 'grad_b_glu': 'grad_w', 'grad_w_o_ssm': 'grad_w', 'grad_mem_norm_g': 'grad_w', 'grad_w_mem_kv': 'grad_w', 'grad_xq_norm_g': 'grad_w', 'grad_xk_norm_g': 'grad_w', 'grad_w_o_cross': 'grad_w', 'grad_b_gate': 'grad_w', 'grad_w_out': 'grad_w', 'grad_norm_ffn_g': 'grad_w', 'grad_w_up': 'grad_w', 'grad_conv_w': 'grad_w', 'grad_conv_b': 'grad_w', 'grad_w_down': 'grad_w', 'delta_norm_mix_g': 'delta_w', 'delta_w_in': 'delta_w', 'delta_q_a_norm_g': 'delta_w', 'delta_w_q_b': 'delta_w', 'delta_kv_a_norm_g': 'delta_w', 'delta_w_kv_b': 'delta_w', 'delta_q_norm_g': 'delta_w', 'delta_k_norm_g': 'delta_w', 'delta_w_o_mla': 'delta_w', 'delta_ssm_lambda_re': 'delta_w', 'delta_ssm_lambda_im': 'delta_w', 'delta_ssm_log_dt': 'delta_w', 'delta_ssm_b_re': 'delta_w', 'delta_ssm_b_im': 'delta_w', 'delta_ssm_c_re': 'delta_w', 'delta_ssm_c_im': 'delta_w', 'delta_ssm_d': 'delta_w', 'delta_w_glu': 'delta_w', 'delta_b_glu': 'delta_w', 'delta_w_o_ssm': 'delta_w', 'delta_mem_norm_g': 'delta_w', 'delta_w_mem_kv': 'delta_w', 'delta_xq_norm_g': 'delta_w', 'delta_xk_norm_g': 'delta_w', 'delta_w_o_cross': 'delta_w', 'delta_b_gate': 'delta_w', 'delta_w_out': 'delta_w', 'delta_norm_ffn_g': 'delta_w', 'delta_w_up': 'delta_w', 'delta_conv_w': 'delta_w', 'delta_conv_b': 'delta_w', 'delta_w_down': 'delta_w', 'new_m_norm_mix_g': 'new_m', 'new_m_w_in': 'new_m', 'new_m_q_a_norm_g': 'new_m', 'new_m_w_q_b': 'new_m', 'new_m_kv_a_norm_g': 'new_m', 'new_m_w_kv_b': 'new_m', 'new_m_q_norm_g': 'new_m', 'new_m_k_norm_g': 'new_m', 'new_m_w_o_mla': 'new_m', 'new_m_ssm_lambda_re': 'new_m', 'new_m_ssm_lambda_im': 'new_m', 'new_m_ssm_log_dt': 'new_m', 'new_m_ssm_b_re': 'new_m', 'new_m_ssm_b_im': 'new_m', 'new_m_ssm_c_re': 'new_m', 'new_m_ssm_c_im': 'new_m', 'new_m_ssm_d': 'new_m', 'new_m_w_glu': 'new_m', 'new_m_b_glu': 'new_m', 'new_m_w_o_ssm': 'new_m', 'new_m_mem_norm_g': 'new_m', 'new_m_w_mem_kv': 'new_m', 'new_m_xq_norm_g': 'new_m', 'new_m_xk_norm_g': 'new_m', 'new_m_w_o_cross': 'new_m', 'new_m_b_gate': 'new_m', 'new_m_w_out': 'new_m', 'new_m_norm_ffn_g': 'new_m', 'new_m_w_up': 'new_m', 'new_m_conv_w': 'new_m', 'new_m_conv_b': 'new_m', 'new_m_w_down': 'new_m', 'new_v_norm_mix_g': 'new_v', 'new_v_w_in': 'new_v', 'new_v_q_a_norm_g': 'new_v', 'new_v_w_q_b': 'new_v', 'new_v_kv_a_norm_g': 'new_v', 'new_v_w_kv_b': 'new_v', 'new_v_q_norm_g': 'new_v', 'new_v_k_norm_g': 'new_v', 'new_v_w_o_mla': 'new_v', 'new_v_ssm_lambda_re': 'new_v', 'new_v_ssm_lambda_im': 'new_v', 'new_v_ssm_log_dt': 'new_v', 'new_v_ssm_b_re': 'new_v', 'new_v_ssm_b_im': 'new_v', 'new_v_ssm_c_re': 'new_v', 'new_v_ssm_c_im': 'new_v', 'new_v_ssm_d': 'new_v', 'new_v_w_glu': 'new_v', 'new_v_b_glu': 'new_v', 'new_v_w_o_ssm': 'new_v', 'new_v_mem_norm_g': 'new_v', 'new_v_w_mem_kv': 'new_v', 'new_v_xq_norm_g': 'new_v', 'new_v_xk_norm_g': 'new_v', 'new_v_w_o_cross': 'new_v', 'new_v_b_gate': 'new_v', 'new_v_w_out': 'new_v', 'new_v_norm_ffn_g': 'new_v', 'new_v_w_up': 'new_v', 'new_v_conv_w': 'new_v', 'new_v_conv_b': 'new_v', 'new_v_w_down': 'new_v'}


def _forward(args):
    return _fwd_reference(*[args[k] for k in FWD_PARAMS])


def _output_shape():
    def fwd():
        inp = _fwd_setup_inputs(0)
        return _fwd_reference(*[inp[k] for k in FWD_PARAMS])
    out = _jax.eval_shape(fwd)
    return out.shape, out.dtype

N_MICROBATCH = 1
ADAM_LR = 0.001
ADAM_B1 = 0.9
ADAM_B2 = 0.999
ADAM_EPS = 1e-08
ADAM_WD = 0.01
ADAM_STEP = 10
PER_EXAMPLE_BATCH_AXIS = {'x': 0, 'mem': 0, 'positions': 0, 'loss_target': 0}
SHARED_INPUTS = []
_WEIGHT_DTYPES = {'norm_mix_g': _jnp.float32, 'w_in': _jnp.float32, 'q_a_norm_g': _jnp.float32, 'w_q_b': _jnp.float32, 'kv_a_norm_g': _jnp.float32, 'w_kv_b': _jnp.float32, 'q_norm_g': _jnp.float32, 'k_norm_g': _jnp.float32, 'w_o_mla': _jnp.float32, 'ssm_lambda_re': _jnp.float32, 'ssm_lambda_im': _jnp.float32, 'ssm_log_dt': _jnp.float32, 'ssm_b_re': _jnp.float32, 'ssm_b_im': _jnp.float32, 'ssm_c_re': _jnp.float32, 'ssm_c_im': _jnp.float32, 'ssm_d': _jnp.float32, 'w_glu': _jnp.float32, 'b_glu': _jnp.float32, 'w_o_ssm': _jnp.float32, 'mem_norm_g': _jnp.float32, 'w_mem_kv': _jnp.float32, 'xq_norm_g': _jnp.float32, 'xk_norm_g': _jnp.float32, 'w_o_cross': _jnp.float32, 'b_gate': _jnp.float32, 'w_out': _jnp.float32, 'norm_ffn_g': _jnp.float32, 'w_up': _jnp.float32, 'conv_w': _jnp.float32, 'conv_b': _jnp.float32, 'w_down': _jnp.float32}
MOMENT_SCALE = {'norm_mix_g': 2.624274e+00, 'w_in': 1.988901e-01, 'q_a_norm_g': 1.385353e-01, 'w_q_b': 9.533762e-02, 'kv_a_norm_g': 8.555604e-01, 'w_kv_b': 2.644195e-01, 'q_norm_g': 1.008650e+00, 'k_norm_g': 1.012443e+00, 'w_o_mla': 2.814830e-01, 'ssm_lambda_re': 4.879994e-02, 'ssm_lambda_im': 6.095727e-02, 'ssm_log_dt': 1.204136e+01, 'ssm_b_re': 3.108281e-02, 'ssm_b_im': 2.432357e-02, 'ssm_c_re': 4.682979e-02, 'ssm_c_im': 3.588971e-02, 'ssm_d': 6.254452e+00, 'w_glu': 1.319021e+00, 'b_glu': 3.738410e+00, 'w_o_ssm': 1.997831e+00, 'mem_norm_g': 2.813056e-01, 'w_mem_kv': 2.295358e-01, 'xq_norm_g': 1.367367e+00, 'xk_norm_g': 1.368426e+00, 'w_o_cross': 2.334657e-01, 'b_gate': 6.731707e-01, 'w_out': 1.957891e+00, 'norm_ffn_g': 5.185700e+01, 'w_up': 7.278815e-01, 'conv_w': 7.218749e+00, 'conv_b': 6.600159e+00, 'w_down': 6.971967e-01}


def _to_microbatches(a, axis):
    t = _jnp.moveaxis(a, axis, 0)
    t = t.reshape((N_MICROBATCH, t.shape[0] // N_MICROBATCH) + t.shape[1:])
    return _jnp.moveaxis(t, 1, axis + 1)


def setup_inputs(seed: int = 0) -> dict:
    inp = _fwd_setup_inputs(seed)
    key = _jax.random.fold_in(_jax.random.key(seed), 7919)
    shape, _ = _output_shape()
    out = dict(inp)
    out["loss_target"] = _jax.random.normal(_jax.random.fold_in(key, 0), shape, _jnp.float32)
    for i, name in enumerate(TWIN_WEIGHTS):
        w = inp[name].astype(_jnp.float32)
        if MOMENT_SCALE is None:
            s = _jnp.sqrt(_jnp.mean(_jnp.square(w)) + 1e-30)
        else:
            s = MOMENT_SCALE[name]
        km, kv = _jax.random.split(_jax.random.fold_in(key, i + 1))
        out[name] = w
        out["m_" + name] = s * _jax.random.normal(km, w.shape, _jnp.float32)
        out["v_" + name] = (s * s) * _jax.random.uniform(kv, w.shape, _jnp.float32, 0.5, 1.5)
    if N_MICROBATCH > 1:
        for name, axis in PER_EXAMPLE_BATCH_AXIS.items():
            out[name] = _to_microbatches(out[name], axis)
    return {'x': out['x'], 'mem': out['mem'], 'positions': out['positions'], 'norm_mix_g': out['norm_mix_g'], 'w_in': out['w_in'], 'q_a_norm_g': out['q_a_norm_g'], 'w_q_b': out['w_q_b'], 'kv_a_norm_g': out['kv_a_norm_g'], 'w_kv_b': out['w_kv_b'], 'q_norm_g': out['q_norm_g'], 'k_norm_g': out['k_norm_g'], 'w_o_mla': out['w_o_mla'], 'ssm_lambda_re': out['ssm_lambda_re'], 'ssm_lambda_im': out['ssm_lambda_im'], 'ssm_log_dt': out['ssm_log_dt'], 'ssm_b_re': out['ssm_b_re'], 'ssm_b_im': out['ssm_b_im'], 'ssm_c_re': out['ssm_c_re'], 'ssm_c_im': out['ssm_c_im'], 'ssm_d': out['ssm_d'], 'w_glu': out['w_glu'], 'b_glu': out['b_glu'], 'w_o_ssm': out['w_o_ssm'], 'mem_norm_g': out['mem_norm_g'], 'w_mem_kv': out['w_mem_kv'], 'xq_norm_g': out['xq_norm_g'], 'xk_norm_g': out['xk_norm_g'], 'w_o_cross': out['w_o_cross'], 'b_gate': out['b_gate'], 'w_out': out['w_out'], 'norm_ffn_g': out['norm_ffn_g'], 'w_up': out['w_up'], 'conv_w': out['conv_w'], 'conv_b': out['conv_b'], 'w_down': out['w_down'], 'loss_target': out['loss_target'], 'm_norm_mix_g': out['m_norm_mix_g'], 'm_w_in': out['m_w_in'], 'm_q_a_norm_g': out['m_q_a_norm_g'], 'm_w_q_b': out['m_w_q_b'], 'm_kv_a_norm_g': out['m_kv_a_norm_g'], 'm_w_kv_b': out['m_w_kv_b'], 'm_q_norm_g': out['m_q_norm_g'], 'm_k_norm_g': out['m_k_norm_g'], 'm_w_o_mla': out['m_w_o_mla'], 'm_ssm_lambda_re': out['m_ssm_lambda_re'], 'm_ssm_lambda_im': out['m_ssm_lambda_im'], 'm_ssm_log_dt': out['m_ssm_log_dt'], 'm_ssm_b_re': out['m_ssm_b_re'], 'm_ssm_b_im': out['m_ssm_b_im'], 'm_ssm_c_re': out['m_ssm_c_re'], 'm_ssm_c_im': out['m_ssm_c_im'], 'm_ssm_d': out['m_ssm_d'], 'm_w_glu': out['m_w_glu'], 'm_b_glu': out['m_b_glu'], 'm_w_o_ssm': out['m_w_o_ssm'], 'm_mem_norm_g': out['m_mem_norm_g'], 'm_w_mem_kv': out['m_w_mem_kv'], 'm_xq_norm_g': out['m_xq_norm_g'], 'm_xk_norm_g': out['m_xk_norm_g'], 'm_w_o_cross': out['m_w_o_cross'], 'm_b_gate': out['m_b_gate'], 'm_w_out': out['m_w_out'], 'm_norm_ffn_g': out['m_norm_ffn_g'], 'm_w_up': out['m_w_up'], 'm_conv_w': out['m_conv_w'], 'm_conv_b': out['m_conv_b'], 'm_w_down': out['m_w_down'], 'v_norm_mix_g': out['v_norm_mix_g'], 'v_w_in': out['v_w_in'], 'v_q_a_norm_g': out['v_q_a_norm_g'], 'v_w_q_b': out['v_w_q_b'], 'v_kv_a_norm_g': out['v_kv_a_norm_g'], 'v_w_kv_b': out['v_w_kv_b'], 'v_q_norm_g': out['v_q_norm_g'], 'v_k_norm_g': out['v_k_norm_g'], 'v_w_o_mla': out['v_w_o_mla'], 'v_ssm_lambda_re': out['v_ssm_lambda_re'], 'v_ssm_lambda_im': out['v_ssm_lambda_im'], 'v_ssm_log_dt': out['v_ssm_log_dt'], 'v_ssm_b_re': out['v_ssm_b_re'], 'v_ssm_b_im': out['v_ssm_b_im'], 'v_ssm_c_re': out['v_ssm_c_re'], 'v_ssm_c_im': out['v_ssm_c_im'], 'v_ssm_d': out['v_ssm_d'], 'v_w_glu': out['v_w_glu'], 'v_b_glu': out['v_b_glu'], 'v_w_o_ssm': out['v_w_o_ssm'], 'v_mem_norm_g': out['v_mem_norm_g'], 'v_w_mem_kv': out['v_w_mem_kv'], 'v_xq_norm_g': out['v_xq_norm_g'], 'v_xk_norm_g': out['v_xk_norm_g'], 'v_w_o_cross': out['v_w_o_cross'], 'v_b_gate': out['v_b_gate'], 'v_w_out': out['v_w_out'], 'v_norm_ffn_g': out['v_norm_ffn_g'], 'v_w_up': out['v_w_up'], 'v_conv_w': out['v_conv_w'], 'v_conv_b': out['v_conv_b'], 'v_w_down': out['v_w_down']}


def _loss(weights, diff, rest, loss_target):
    with _jax.named_scope("forward"):
        args = {**rest, TWIN_DIFF_INPUT: diff, **{k: w.astype(_WEIGHT_DTYPES[k]) for k, w in weights.items()}}
        y = _forward(args)
    with _jax.named_scope("loss_head"):
        err = _jnp.square(y.astype(_jnp.float32) - loss_target)
        return 0.5 * _jnp.sum(_jnp.mean(err, axis=-1)) if err.ndim else 0.5 * err


def _adamw(w, g, m, v):
    m = ADAM_B1 * m + (1.0 - ADAM_B1) * g
    v = ADAM_B2 * v + (1.0 - ADAM_B2) * _jnp.square(g)
    m_hat = m / (1.0 - ADAM_B1 ** ADAM_STEP)
    v_hat = v / (1.0 - ADAM_B2 ** ADAM_STEP)
    delta = -ADAM_LR * (m_hat / (_jnp.sqrt(v_hat) + ADAM_EPS) + ADAM_WD * w)
    return delta, m, v


def reference(x, mem, positions, norm_mix_g, w_in, q_a_norm_g, w_q_b, kv_a_norm_g, w_kv_b, q_norm_g, k_norm_g, w_o_mla, ssm_lambda_re, ssm_lambda_im, ssm_log_dt, ssm_b_re, ssm_b_im, ssm_c_re, ssm_c_im, ssm_d, w_glu, b_glu, w_o_ssm, mem_norm_g, w_mem_kv, xq_norm_g, xk_norm_g, w_o_cross, b_gate, w_out, norm_ffn_g, w_up, conv_w, conv_b, w_down, loss_target, m_norm_mix_g, m_w_in, m_q_a_norm_g, m_w_q_b, m_kv_a_norm_g, m_w_kv_b, m_q_norm_g, m_k_norm_g, m_w_o_mla, m_ssm_lambda_re, m_ssm_lambda_im, m_ssm_log_dt, m_ssm_b_re, m_ssm_b_im, m_ssm_c_re, m_ssm_c_im, m_ssm_d, m_w_glu, m_b_glu, m_w_o_ssm, m_mem_norm_g, m_w_mem_kv, m_xq_norm_g, m_xk_norm_g, m_w_o_cross, m_b_gate, m_w_out, m_norm_ffn_g, m_w_up, m_conv_w, m_conv_b, m_w_down, v_norm_mix_g, v_w_in, v_q_a_norm_g, v_w_q_b, v_kv_a_norm_g, v_w_kv_b, v_q_norm_g, v_k_norm_g, v_w_o_mla, v_ssm_lambda_re, v_ssm_lambda_im, v_ssm_log_dt, v_ssm_b_re, v_ssm_b_im, v_ssm_c_re, v_ssm_c_im, v_ssm_d, v_w_glu, v_b_glu, v_w_o_ssm, v_mem_norm_g, v_w_mem_kv, v_xq_norm_g, v_xk_norm_g, v_w_o_cross, v_b_gate, v_w_out, v_norm_ffn_g, v_w_up, v_conv_w, v_conv_b, v_w_down):
    given = dict(x=x, mem=mem, positions=positions, norm_mix_g=norm_mix_g, w_in=w_in, q_a_norm_g=q_a_norm_g, w_q_b=w_q_b, kv_a_norm_g=kv_a_norm_g, w_kv_b=w_kv_b, q_norm_g=q_norm_g, k_norm_g=k_norm_g, w_o_mla=w_o_mla, ssm_lambda_re=ssm_lambda_re, ssm_lambda_im=ssm_lambda_im, ssm_log_dt=ssm_log_dt, ssm_b_re=ssm_b_re, ssm_b_im=ssm_b_im, ssm_c_re=ssm_c_re, ssm_c_im=ssm_c_im, ssm_d=ssm_d, w_glu=w_glu, b_glu=b_glu, w_o_ssm=w_o_ssm, mem_norm_g=mem_norm_g, w_mem_kv=w_mem_kv, xq_norm_g=xq_norm_g, xk_norm_g=xk_norm_g, w_o_cross=w_o_cross, b_gate=b_gate, w_out=w_out, norm_ffn_g=norm_ffn_g, w_up=w_up, conv_w=conv_w, conv_b=conv_b, w_down=w_down, loss_target=loss_target, m_norm_mix_g=m_norm_mix_g, m_w_in=m_w_in, m_q_a_norm_g=m_q_a_norm_g, m_w_q_b=m_w_q_b, m_kv_a_norm_g=m_kv_a_norm_g, m_w_kv_b=m_w_kv_b, m_q_norm_g=m_q_norm_g, m_k_norm_g=m_k_norm_g, m_w_o_mla=m_w_o_mla, m_ssm_lambda_re=m_ssm_lambda_re, m_ssm_lambda_im=m_ssm_lambda_im, m_ssm_log_dt=m_ssm_log_dt, m_ssm_b_re=m_ssm_b_re, m_ssm_b_im=m_ssm_b_im, m_ssm_c_re=m_ssm_c_re, m_ssm_c_im=m_ssm_c_im, m_ssm_d=m_ssm_d, m_w_glu=m_w_glu, m_b_glu=m_b_glu, m_w_o_ssm=m_w_o_ssm, m_mem_norm_g=m_mem_norm_g, m_w_mem_kv=m_w_mem_kv, m_xq_norm_g=m_xq_norm_g, m_xk_norm_g=m_xk_norm_g, m_w_o_cross=m_w_o_cross, m_b_gate=m_b_gate, m_w_out=m_w_out, m_norm_ffn_g=m_norm_ffn_g, m_w_up=m_w_up, m_conv_w=m_conv_w, m_conv_b=m_conv_b, m_w_down=m_w_down, v_norm_mix_g=v_norm_mix_g, v_w_in=v_w_in, v_q_a_norm_g=v_q_a_norm_g, v_w_q_b=v_w_q_b, v_kv_a_norm_g=v_kv_a_norm_g, v_w_kv_b=v_w_kv_b, v_q_norm_g=v_q_norm_g, v_k_norm_g=v_k_norm_g, v_w_o_mla=v_w_o_mla, v_ssm_lambda_re=v_ssm_lambda_re, v_ssm_lambda_im=v_ssm_lambda_im, v_ssm_log_dt=v_ssm_log_dt, v_ssm_b_re=v_ssm_b_re, v_ssm_b_im=v_ssm_b_im, v_ssm_c_re=v_ssm_c_re, v_ssm_c_im=v_ssm_c_im, v_ssm_d=v_ssm_d, v_w_glu=v_w_glu, v_b_glu=v_b_glu, v_w_o_ssm=v_w_o_ssm, v_mem_norm_g=v_mem_norm_g, v_w_mem_kv=v_w_mem_kv, v_xq_norm_g=v_xq_norm_g, v_xk_norm_g=v_xk_norm_g, v_w_o_cross=v_w_o_cross, v_b_gate=v_b_gate, v_w_out=v_w_out, v_norm_ffn_g=v_norm_ffn_g, v_w_up=v_w_up, v_conv_w=v_conv_w, v_conv_b=v_conv_b, v_w_down=v_w_down)
    weights = {n: given[n] for n in TWIN_WEIGHTS}
    shared = {n: given[n] for n in SHARED_INPUTS}
    per_example = {n: given[n] for n in ['x', 'mem', 'positions']}
    grad_fn = _jax.value_and_grad(_loss, argnums=(0, 1))

    def one_microbatch(ex, loss_target):
        ex = dict(ex)
        diff = ex.pop(TWIN_DIFF_INPUT)
        return grad_fn(weights, diff, {**shared, **ex}, loss_target)

    if N_MICROBATCH == 1:
        loss, (grad_w, grad_x) = one_microbatch(per_example, given["loss_target"])
    else:
        def body(carry, xs):
            loss_sum, grad_sum = carry
            l_k, (gw_k, gx_k) = one_microbatch(xs[0], xs[1])
            with _jax.named_scope("update"):
                return (loss_sum + l_k, _jax.tree.map(_jnp.add, grad_sum, gw_k)), gx_k

        init = (_jnp.zeros((), _jnp.float32), _jax.tree.map(_jnp.zeros_like, weights))
        (loss, grad_w), grad_x = _jax.lax.scan(body, init, (per_example, given["loss_target"]))
    with _jax.named_scope("update"):
        delta_w, new_m, new_v = {}, {}, {}
        for n in TWIN_WEIGHTS:
            delta_w[n], new_m[n], new_v[n] = _adamw(weights[n], grad_w[n], given["m_" + n], given["v_" + n])
    return (loss, grad_x, *[grad_w[n] for n in TWIN_WEIGHTS], *[delta_w[n] for n in TWIN_WEIGHTS],
            *[new_m[n] for n in TWIN_WEIGHTS], *[new_v[n] for n in TWIN_WEIGHTS])
```

```python
import math

import jax
import jax.numpy as jnp
from jax import lax
from jax.experimental import pallas as pl
from jax.experimental.pallas import tpu as pltpu

F32 = jnp.float32
BF16 = jnp.bfloat16

N_DEV = 8
LANES = 128
VMEM_LIMIT_BYTES = 56 * 1024 * 1024

D_MODEL = 1024
EPS = 1e-6
MLA_HEADS = 8
Q_LORA = 384
KV_LORA = 256
D_NOPE = 64
D_ROPE = 32
D_QK = D_NOPE + D_ROPE
D_V = 64
ROPE_THETA = 10000.0
SSM_GROUPS = 32
SSM_GROUP_CH = 16
SSM_WIDTH = 512
SSM_STATE = 64
SSM_N = SSM_GROUPS * SSM_STATE
X_HEADS = 4
X_HEAD_DIM = 128
X_WIDTH = 512
D_FF = 2816
IN_WIDTH = Q_LORA + KV_LORA + D_ROPE + SSM_WIDTH + X_WIDTH + 3 * D_MODEL
QKV_W = Q_LORA + KV_LORA + LANES
KR_LO = D_NOPE

ADAM_LR = 0.001
ADAM_B1 = 0.9
ADAM_B2 = 0.999
ADAM_EPS = 1e-08
ADAM_WD = 0.01
ADAM_STEP = 10

SHARDED = (
    ("w_in", (D_MODEL, IN_WIDTH), 1),
    ("w_q_b", (Q_LORA, MLA_HEADS * D_QK), 1),
    ("w_kv_b", (KV_LORA, MLA_HEADS * (D_NOPE + D_V)), 1),
    ("w_o_mla", (MLA_HEADS * D_V, D_MODEL), 1),
    ("w_glu", (SSM_WIDTH, SSM_WIDTH), 0),
    ("w_o_ssm", (SSM_WIDTH, D_MODEL), 1),
    ("w_mem_kv", (D_MODEL, 2 * X_WIDTH), 0),
    ("w_o_cross", (X_WIDTH, D_MODEL), 1),
    ("w_out", (D_MODEL, D_MODEL), 0),
    ("w_up", (D_MODEL, 2 * D_FF), 1),
    ("conv_w", (3, 2 * D_FF), 1),
    ("w_down", (D_FF, D_MODEL), 0),
)
REPLICATED = (
    "norm_mix_g", "q_a_norm_g", "kv_a_norm_g", "q_norm_g", "k_norm_g", "ssm_lambda_re", "ssm_lambda_im",
    "ssm_log_dt", "ssm_b_re", "ssm_b_im", "ssm_c_re", "ssm_c_im", "ssm_d", "b_glu", "mem_norm_g",
    "xq_norm_g", "xk_norm_g", "b_gate", "norm_ffn_g", "conv_b",
)
WEIGHT_ORDER = (
    "norm_mix_g", "w_in", "q_a_norm_g", "w_q_b", "kv_a_norm_g", "w_kv_b", "q_norm_g", "k_norm_g", "w_o_mla",
    "ssm_lambda_re", "ssm_lambda_im", "ssm_log_dt", "ssm_b_re", "ssm_b_im", "ssm_c_re", "ssm_c_im", "ssm_d",
    "w_glu", "b_glu", "w_o_ssm", "mem_norm_g", "w_mem_kv", "xq_norm_g", "xk_norm_g", "w_o_cross", "b_gate",
    "w_out", "norm_ffn_g", "w_up", "conv_w", "conv_b", "w_down",
)


def _params(**kw):
    return pltpu.CompilerParams(vmem_limit_bytes=VMEM_LIMIT_BYTES, **kw)


def _pick(n, cap):
    if n <= cap:
        return n
    best = None
    for m in range(LANES, cap + 1, LANES):
        if n % m == 0:
            best = m
    assert best is not None, n
    return best


_NN = (((1,), (0,)), ((), ()))
_NT = (((1,), (1,)), ((), ()))
_TN = (((0,), (0,)), ((), ()))


def _dot(a, b, dn):
    return lax.dot_general(a.astype(BF16), b.astype(BF16), dn, preferred_element_type=F32)


def _mm(name, pairs, *, trans_b=False, add=None, out_dtype=F32, bm=512, bn_cap=512):
    m = pairs[0][0].shape[0]
    n = pairs[0][1].shape[0 if trans_b else 1]
    bm = min(bm, m)
    bn = _pick(n, bn_cap)
    npair = len(pairs)

    def body(*refs):
        o_ref = refs[-1]
        acc = None
        for p in range(npair):
            d = _dot(refs[2 * p][...], refs[2 * p + 1][...], _NT if trans_b else _NN)
            acc = d if acc is None else acc + d
        if add is not None:
            acc = acc + refs[2 * npair][...]
        o_ref[...] = acc.astype(out_dtype)

    in_specs, args = [], []
    for a, b in pairs:
        k = a.shape[1]
        in_specs.append(pl.BlockSpec((bm, k), lambda i, j: (i, 0)))
        if trans_b:
            in_specs.append(pl.BlockSpec((bn, k), lambda i, j: (j, 0)))
        else:
            in_specs.append(pl.BlockSpec((k, bn), lambda i, j: (0, j)))
        args += [a, b]
    if add is not None:
        in_specs.append(pl.BlockSpec((bm, bn), lambda i, j: (i, j)))
        args.append(add)
    return pl.pallas_call(
        body, name=name, grid=(m // bm, n // bn), in_specs=in_specs,
        out_specs=pl.BlockSpec((bm, bn), lambda i, j: (i, j)),
        out_shape=jax.ShapeDtypeStruct((m, n), out_dtype), compiler_params=_params(),
    )(*args)


def _mm_tn(name, a, b, *, bm_cap=512, bn_cap=1024, bk=1024):
    l, m = a.shape
    n = b.shape[1]
    bm, bn, bk = _pick(m, bm_cap), _pick(n, bn_cap), min(bk, l)

    def body(a_ref, b_ref, o_ref):
        @pl.when(pl.program_id(2) == 0)
        def _():
            o_ref[...] = jnp.zeros_like(o_ref)

        o_ref[...] += _dot(a_ref[...], b_ref[...], _TN)

    return pl.pallas_call(
        body, name=name, grid=(m // bm, n // bn, l // bk),
        in_specs=[pl.BlockSpec((bk, bm), lambda i, j, k: (k, i)), pl.BlockSpec((bk, bn), lambda i, j, k: (k, j))],
        out_specs=pl.BlockSpec((bm, bn), lambda i, j, k: (i, j)),
        out_shape=jax.ShapeDtypeStruct((m, n), F32), compiler_params=_params(),
    )(a, b)


def _rowwise(name, fn, nrows, bm, row_ins, consts, row_outs, acc_outs=()):
    bm = min(bm, nrows)
    nblk = nrows // bm
    sub = bm // 8
    nin, nc, nro = len(row_ins), len(consts), len(row_outs)

    def body(*refs):
        i = pl.program_id(0)
        outs = fn(i, refs[:nin], refs[nin:nin + nc])
        o_refs = refs[nin + nc:nin + nc + nro]
        a_refs = refs[nin + nc + nro:]
        for r, v in zip(o_refs, outs[:nro]):
            r[...] = v.astype(r.dtype)
        if a_refs:
            @pl.when(i == 0)
            def _():
                for r in a_refs:
                    r[...] = jnp.zeros_like(r)

            for r, v in zip(a_refs, outs[nro:]):
                r[...] += v

    in_specs, args = [], []
    for arr, w, cb, kind in row_ins:
        if kind == "row":
            in_specs.append(pl.BlockSpec((bm, w), lambda i, cb=cb: (i, cb)))
        elif kind == "prev":
            in_specs.append(pl.BlockSpec((8, w), lambda i, cb=cb: (jnp.maximum(i * sub - 1, 0), cb)))
        else:
            in_specs.append(pl.BlockSpec((8, w), lambda i, cb=cb: (jnp.minimum((i + 1) * sub, nrows // 8 - 1), cb)))
        args.append(arr)
    for c in consts:
        in_specs.append(pl.BlockSpec(c.shape, lambda i: (0, 0)))
        args.append(c)
    out_specs = [pl.BlockSpec((bm, w), lambda i: (i, 0)) for w, _ in row_outs]
    out_specs += [pl.BlockSpec(s, lambda i: (0, 0)) for s in acc_outs]
    out_shape = [jax.ShapeDtypeStruct((nrows, w), dt) for w, dt in row_outs]
    out_shape += [jax.ShapeDtypeStruct(s, F32) for s in acc_outs]
    res = pl.pallas_call(
        body, name=name, grid=(nblk,), in_specs=in_specs, out_specs=out_specs, out_shape=out_shape,
        compiler_params=_params(),
    )(*args)
    return res


def _rms_f(x, g, n):
    r = lax.rsqrt(jnp.sum(x * x, axis=-1, keepdims=True) * (1.0 / n) + EPS)
    return x * r * g


def _rms_b(x, g, dy, n):
    r = lax.rsqrt(jnp.sum(x * x, axis=-1, keepdims=True) * (1.0 / n) + EPS)
    gx = dy * g
    dx = r * gx - x * (r * r * r * (jnp.sum(x * gx, axis=-1, keepdims=True) * (1.0 / n)))
    dg = jnp.sum(dy * (x * r), axis=0, keepdims=True)
    return dx, dg


def _rope_f(x, c, sa, sb):
    return x * c + pltpu.roll(x, LANES - 16, 1) * sa + pltpu.roll(x, 16, 1) * sb


def _rope_b(g, c, sa, sb):
    return g * c + pltpu.roll(g * sa, 16, 1) + pltpu.roll(g * sb, LANES - 16, 1)


def _gelu(x):
    c = math.sqrt(2.0 / math.pi)
    return 0.5 * x * (1.0 + jnp.tanh(c * (x + 0.044715 * (x * x * x))))


def _gelu_grad(x):
    c = math.sqrt(2.0 / math.pi)
    th = jnp.tanh(c * (x + 0.044715 * (x * x * x)))
    return 0.5 * (1.0 + th) + 0.5 * x * (1.0 - th * th) * (c * (1.0 + 3.0 * 0.044715 * (x * x)))


def _row_ids(bm):
    return lax.broadcasted_iota(jnp.int32, (bm, 1), 0)


def _shift_down(x, halo_ref, i, k):
    bm = x.shape[0]
    row = _row_ids(bm)
    live = (i > 0).astype(F32)
    out = pltpu.roll(x, k, 0)
    for r in range(k):
        e = (row == r).astype(F32)
        out = out * (1.0 - e) + e * (halo_ref[8 - k + r:8 - k + r + 1, :] * live)
    return out


def _shift_up(x, halo_ref, i, nblk, k):
    bm = x.shape[0]
    row = _row_ids(bm)
    live = (i < nblk - 1).astype(F32)
    out = pltpu.roll(x, bm - k, 0)
    for r in range(k):
        e = (row == bm - k + r).astype(F32)
        out = out * (1.0 - e) + e * (halo_ref[r:r + 1, :] * live)
    return out


def _attn_fwd(name, qa, ka, va, *, qoff, koff, voff, heads, causal, scale, bq, bk):
    lq, lk = qa.shape[0], ka.shape[0]
    bq, bk = min(bq, lq), min(bk, lk)
    nq, nk = lq // bq, lk // bk

    def kv_map(off):
        def f(h, i, j):
            jj = jnp.minimum(j, (i * bq + bq - 1) // bk) if causal else j
            return (jj, off + h)
        return f

    def body(q_ref, k_ref, v_ref, o_ref, lse_ref, m_s, l_s, acc_s):
        i, j = pl.program_id(1), pl.program_id(2)

        @pl.when(j == 0)
        def _():
            m_s[...] = jnp.full_like(m_s, -1e30)
            l_s[...] = jnp.zeros_like(l_s)
            acc_s[...] = jnp.zeros_like(acc_s)

        def step(masked):
            s = _dot(q_ref[...], k_ref[...], _NT) * scale
            if masked:
                row = i * bq + lax.broadcasted_iota(jnp.int32, (bq, bk), 0)
                col = j * bk + lax.broadcasted_iota(jnp.int32, (bq, bk), 1)
                s = jnp.where(col <= row, s, -1e30)
            m_prev = m_s[...]
            m_new = jnp.maximum(m_prev, jnp.max(s, axis=-1, keepdims=True))
            alpha = jnp.exp(m_prev - m_new)
            p = jnp.exp(s - m_new)
            l_s[...] = alpha * l_s[...] + jnp.sum(p, axis=-1, keepdims=True)
            acc_s[...] = alpha * acc_s[...] + _dot(p, v_ref[...], _NN)
            m_s[...] = m_new

        if causal:
            full = j * bk + bk - 1 <= i * bq
            live = j * bk <= i * bq + bq - 1
            pl.when(full)(lambda: step(False))
            pl.when(jnp.logical_and(live, jnp.logical_not(full)))(lambda: step(True))
        else:
            step(False)

        @pl.when(j == nk - 1)
        def _():
            l = l_s[...]
            o_ref[...] = (acc_s[...] / l).astype(o_ref.dtype)
            lse_ref[...] = jnp.broadcast_to(m_s[...] + jnp.log(l), lse_ref.shape)

    return pl.pallas_call(
        body, name=name, grid=(heads, nq, nk),
        in_specs=[pl.BlockSpec((bq, LANES), lambda h, i, j: (i, qoff + h)),
                  pl.BlockSpec((bk, LANES), kv_map(koff)), pl.BlockSpec((bk, LANES), kv_map(voff))],
        out_specs=[pl.BlockSpec((bq, LANES), lambda h, i, j: (i, h)), pl.BlockSpec((bq, LANES), lambda h, i, j: (i, h))],
        out_shape=[jax.ShapeDtypeStruct((lq, heads * LANES), BF16), jax.ShapeDtypeStruct((lq, heads * LANES), F32)],
        scratch_shapes=[pltpu.VMEM((bq, 1), F32), pltpu.VMEM((bq, 1), F32), pltpu.VMEM((bq, LANES), F32)],
        compiler_params=_params(),
    )(qa, ka, va)


def _attn_bwd(name, qa, ka, va, oa, doa, lsea, *, qoff, koff, voff, heads, causal, scale, bq, bk):
    lq, lk = qa.shape[0], ka.shape[0]
    bq, bk = min(bq, lq), min(bk, lk)
    nq, nk = lq // bq, lk // bk

    def q_map(off):
        def f(h, j, i):
            ii = jnp.maximum(i, (j * bk) // bq) if causal else i
            return (ii, off + h)
        return f

    def body(q_ref, k_ref, v_ref, o_ref, do_ref, lse_ref, dq_ref, dk_ref, dv_ref):
        j, i = pl.program_id(1), pl.program_id(2)

        @pl.when(jnp.logical_and(j == 0, i == 0))
        def _():
            dq_ref[...] = jnp.zeros_like(dq_ref)

        @pl.when(i == 0)
        def _():
            dk_ref[...] = jnp.zeros_like(dk_ref)
            dv_ref[...] = jnp.zeros_like(dv_ref)

        def step(masked):
            q, k, v, do = q_ref[...], k_ref[...], v_ref[...], do_ref[...]
            s = _dot(q, k, _NT) * scale
            if masked:
                row = i * bq + lax.broadcasted_iota(jnp.int32, (bq, bk), 0)
                col = j * bk + lax.broadcasted_iota(jnp.int32, (bq, bk), 1)
                s = jnp.where(col <= row, s, -1e30)
            lse = jnp.max(lse_ref[...], axis=-1, keepdims=True)
            p = jnp.exp(s - lse)
            dv_ref[...] += _dot(p, do, _TN)
            dp = _dot(do, v, _NT)
            delta = jnp.sum(do.astype(F32) * o_ref[...].astype(F32), axis=-1, keepdims=True)
            ds = (p * (dp - delta) * scale).astype(BF16)
            dk_ref[...] += _dot(ds, q, _TN)
            rows = pl.ds(pl.multiple_of(i * bq, bq), bq)
            dq_ref[rows, :] += _dot(ds, k, _NN)

        if causal:
            full = j * bk + bk - 1 <= i * bq
            live = j * bk <= i * bq + bq - 1
            pl.when(full)(lambda: step(False))
            pl.when(jnp.logical_and(live, jnp.logical_not(full)))(lambda: step(True))
        else:
            step(False)

    kv_spec = lambda off: pl.BlockSpec((bk, LANES), lambda h, j, i: (j, off + h))
    return pl.pallas_call(
        body, name=name, grid=(heads, nk, nq),
        in_specs=[pl.BlockSpec((bq, LANES), q_map(qoff)), kv_spec(koff), kv_spec(voff),
                  pl.BlockSpec((bq, LANES), q_map(0)), pl.BlockSpec((bq, LANES), q_map(0)),
                  pl.BlockSpec((bq, LANES), q_map(0))],
        out_specs=[pl.BlockSpec((lq, LANES), lambda h, j, i: (0, h)), pl.BlockSpec((bk, LANES), lambda h, j, i: (j, h)),
                   pl.BlockSpec((bk, LANES), lambda h, j, i: (j, h))],
        out_shape=[jax.ShapeDtypeStruct((lq, heads * LANES), F32), jax.ShapeDtypeStruct((lk, heads * LANES), F32),
                   jax.ShapeDtypeStruct((lk, heads * LANES), F32)],
        compiler_params=_params(),
    )(qa, ka, va, oa, doa, lsea)


def _scan(name, bre, bim, are, aim, *, reverse, chunk=512, strip=512):
    l, n = bre.shape
    t = min(chunk, l)
    nc, ns = l // t, n // strip
    steps = int(math.log2(t))
    assert 1 << steps == t

    def cmap(w, c):
        return ((nc - 1 - c) if reverse else c, w)

    def body(bre_ref, bim_ref, are_ref, aim_ref, sre_ref, sim_ref, pre_s, pim_s, cre_s, cim_s):
        c = pl.program_id(1)
        ar, ai = are_ref[...], aim_ref[...]
        row = _row_ids(t)

        def chunk_scan(xr, xi):
            pr, pi = ar, ai
            for k in range(steps):
                d = 1 << k
                if reverse:
                    live = (row < t - d).astype(F32)
                    shr, shi = pltpu.roll(xr, t - d, 0) * live, pltpu.roll(xi, t - d, 0) * live
                else:
                    live = (row >= d).astype(F32)
                    shr, shi = pltpu.roll(xr, d, 0) * live, pltpu.roll(xi, d, 0) * live
                xr, xi = xr + pr * shr - pi * shi, xi + pr * shi + pi * shr
                pr, pi = pr * pr - pi * pi, 2.0 * pr * pi
            return xr, xi

        @pl.when(c == 0)
        def _():
            e = (row == (t - 1 if reverse else 0)).astype(F32)
            tr, ti = chunk_scan(e * ar, e * ai)
            pre_s[...] = tr
            pim_s[...] = ti
            cre_s[...] = jnp.zeros_like(cre_s)
            cim_s[...] = jnp.zeros_like(cim_s)

        xr, xi = chunk_scan(bre_ref[...], bim_ref[...])
        cr, ci = cre_s[...], cim_s[...]
        p_r, p_i = pre_s[...], pim_s[...]
        sre_ref[...] = xr + p_r * cr - p_i * ci
        sim_ref[...] = xi + p_r * ci + p_i * cr
        last = 0 if reverse else t - 1
        cre_s[...] = sre_ref[last:last + 1, :]
        cim_s[...] = sim_ref[last:last + 1, :]

    blk = pl.BlockSpec((t, strip), cmap)
    a_blk = pl.BlockSpec((1, strip), lambda w, c: (0, w))
    return pl.pallas_call(
        body, name=name, grid=(ns, nc), in_specs=[blk, blk, a_blk, a_blk], out_specs=[blk, blk],
        out_shape=[jax.ShapeDtypeStruct((l, n), F32), jax.ShapeDtypeStruct((l, n), F32)],
        scratch_shapes=[pltpu.VMEM((t, strip), F32), pltpu.VMEM((t, strip), F32),
                        pltpu.VMEM((1, strip), F32), pltpu.VMEM((1, strip), F32)],
        compiler_params=_params(),
    )(bre, bim, are, aim)


def _disc_math(lr, li, ldt, br, bi):
    dt = jnp.exp(ldt)
    mag = jnp.exp(lr * dt)
    a_re, a_im = mag * jnp.cos(li * dt), mag * jnp.sin(li * dt)
    den = lr * lr + li * li
    e_re, e_im = a_re - 1.0, a_im
    f_re = (e_re * lr + e_im * li) / den
    f_im = (e_im * lr - e_re * li) / den
    return a_re, a_im, f_re * br - f_im * bi, f_re * bi + f_im * br


def _disc_fwd(lr, li, ldt, br, bi):
    def body(lr_ref, li_ref, ldt_ref, br_ref, bi_ref, are_ref, aim_ref, bbr_ref, bbi_ref):
        a_re, a_im, bb_re, bb_im = _disc_math(lr_ref[...], li_ref[...], ldt_ref[...], br_ref[...], bi_ref[...])
        are_ref[...] = a_re
        aim_ref[...] = a_im
        bbr_ref[...] = bb_re
        bbi_ref[...] = bb_im

    col = jax.ShapeDtypeStruct(lr.shape, F32)
    mat = jax.ShapeDtypeStruct(br.shape, F32)
    return pl.pallas_call(body, name="s5_disc_fwd", out_shape=[col, col, mat, mat], compiler_params=_params())(
        lr, li, ldt, br, bi)


def _disc_bwd(lr, li, ldt, br, bi, da_re, da_im, dbb_re, dbb_im):
    def body(lr_ref, li_ref, ldt_ref, br_ref, bi_ref, g0, g1, g2, g3, o0, o1, o2, o3, o4):
        _, vjp = jax.vjp(_disc_math, lr_ref[...], li_ref[...], ldt_ref[...], br_ref[...], bi_ref[...])
        grads = vjp((g0[...], g1[...], g2[...], g3[...]))
        for o, g in zip((o0, o1, o2, o3, o4), grads):
            o[...] = g

    col = jax.ShapeDtypeStruct(lr.shape, F32)
    mat = jax.ShapeDtypeStruct(br.shape, F32)
    return pl.pallas_call(body, name="s5_disc_bwd", out_shape=[col, col, col, mat, mat], compiler_params=_params())(
        lr, li, ldt, br, bi, da_re, da_im, dbb_re, dbb_im)


def _peers():
    x, y, c = lax.axis_index("x"), lax.axis_index("y"), lax.axis_index("c")
    me = 4 * x + 2 * y + c
    out = []
    for k in range(1, N_DEV):
        px = (1 - x) if (k >> 2) & 1 else x
        py = (1 - y) if (k >> 1) & 1 else y
        pc = (1 - c) if k & 1 else c
        out.append(((px, py, pc), 4 * px + 2 * py + pc))
    return me, out


def _exchange(name, src, gather):
    shape = (N_DEV,) + tuple(src.shape[-2:])

    def body(s_ref, o_ref, send_sems, recv_sems, loc_sem):
        me, peers = _peers()
        mine = pltpu.make_async_copy(s_ref if gather else s_ref.at[me], o_ref.at[me], loc_sem)
        mine.start()
        copies = []
        for k, (dev, lin) in enumerate(peers):
            cp = pltpu.make_async_remote_copy(
                src_ref=s_ref if gather else s_ref.at[lin], dst_ref=o_ref.at[me], send_sem=send_sems.at[k],
                recv_sem=recv_sems.at[k], device_id=dev, device_id_type=pl.DeviceIdType.MESH)
            cp.start()
            copies.append(cp)
        for k, (dev, lin) in enumerate(peers):
            pltpu.make_async_remote_copy(
                src_ref=s_ref if gather else s_ref.at[lin], dst_ref=o_ref.at[lin], send_sem=send_sems.at[k],
                recv_sem=recv_sems.at[k], device_id=dev, device_id_type=pl.DeviceIdType.MESH).wait_recv()
        for cp in copies:
            cp.wait_send()
        mine.wait()

    return pl.pallas_call(
        body, name=name, in_specs=[pl.BlockSpec(memory_space=pltpu.HBM)],
        out_specs=pl.BlockSpec(memory_space=pltpu.HBM), out_shape=jax.ShapeDtypeStruct(shape, src.dtype),
        scratch_shapes=[pltpu.SemaphoreType.DMA((N_DEV - 1,)), pltpu.SemaphoreType.DMA((N_DEV - 1,)),
                        pltpu.SemaphoreType.DMA],
        compiler_params=pltpu.CompilerParams(has_side_effects=True),
    )(src)


def _adamw(rcv, w, m, v, bm=256):
    r, c = w.shape

    def body(rcv_ref, w_ref, m_ref, v_ref, g_ref, d_ref, m2_ref, v2_ref):
        g = rcv_ref[0]
        for s in range(1, N_DEV):
            g = g + rcv_ref[s]
        m2 = ADAM_B1 * m_ref[...] + (1.0 - ADAM_B1) * g
        v2 = ADAM_B2 * v_ref[...] + (1.0 - ADAM_B2) * (g * g)
        m_hat = m2 / (1.0 - ADAM_B1 ** ADAM_STEP)
        v_hat = v2 / (1.0 - ADAM_B2 ** ADAM_STEP)
        g_ref[...] = g
        d_ref[...] = -ADAM_LR * (m_hat / (jnp.sqrt(v_hat) + ADAM_EPS) + ADAM_WD * w_ref[...])
        m2_ref[...] = m2
        v2_ref[...] = v2

    blk = pl.BlockSpec((bm, c), lambda i: (i, 0))
    out = jax.ShapeDtypeStruct((r, c), F32)
    return pl.pallas_call(
        body, name="adamw", grid=(r // bm,),
        in_specs=[pl.BlockSpec((N_DEV, bm, c), lambda i: (0, i, 0)), blk, blk, blk], out_specs=[blk] * 4,
        out_shape=[out] * 4, compiler_params=_params(),
    )(rcv, w, m, v)


PACK_W = 1024


def _pad_rows(flat, mult):
    n = flat.shape[-1]
    per = PACK_W * mult
    tot = -(-n // per) * per
    flat = jnp.pad(flat, [(0, 0)] * (flat.ndim - 1) + [(0, tot - n)])
    return flat.reshape(flat.shape[:-1] + (tot // PACK_W, PACK_W))


def _to_shards(name_axis, full):
    ax = name_axis
    two, a, b = full.shape
    if ax == 0:
        t = full.reshape(two, N_DEV, a // N_DEV, b).transpose(1, 0, 2, 3)
    else:
        t = full.reshape(two, a, N_DEV, b // N_DEV).transpose(2, 0, 1, 3)
    return t.reshape(N_DEV, -1)


def _from_shards(ax, blocks, local_shape):
    two, a, b = local_shape
    t = blocks.reshape(N_DEV, two, a, b)
    if ax == 0:
        return t.transpose(1, 0, 2, 3).reshape(two, N_DEV * a, b)
    return t.transpose(1, 2, 0, 3).reshape(two, a, N_DEV * b)


def _block_diag_in(bb):
    b3 = bb.reshape(SSM_GROUPS, SSM_STATE, SSM_GROUP_CH).transpose(0, 2, 1)
    eye = jnp.eye(SSM_GROUPS, dtype=bb.dtype)
    return (b3[:, :, None, :] * eye[:, None, :, None]).reshape(SSM_WIDTH, SSM_N)


def _block_diag_out(cc):
    c3 = cc.transpose(0, 2, 1)
    eye = jnp.eye(SSM_GROUPS, dtype=cc.dtype)
    return (c3[:, :, None, :] * eye[:, None, :, None]).reshape(SSM_N, SSM_WIDTH)


def _diag_blocks(mat, rows_per, cols_per):
    m4 = mat.reshape(SSM_GROUPS, rows_per, SSM_GROUPS, cols_per)
    idx = jnp.arange(SSM_GROUPS)
    return m4[idx, :, idx, :]


def _layer_weights(wf, li):
    w_in = wf["w_in"][li]
    z = lambda r, c: jnp.zeros((r, c), BF16)
    kr = w_in[:, Q_LORA + KV_LORA:Q_LORA + KV_LORA + D_ROPE]
    o = {}
    o["wqkv"] = jnp.concatenate([w_in[:, :Q_LORA + KV_LORA], z(D_MODEL, KR_LO), kr,
                                 z(D_MODEL, LANES - KR_LO - D_ROPE)], axis=1)
    base = Q_LORA + KV_LORA + D_ROPE
    o["w_u"] = w_in[:, base:base + SSM_WIDTH]
    o["w_xq"] = w_in[:, base + SSM_WIDTH:base + SSM_WIDTH + X_WIDTH]
    o["w_g"] = w_in[:, base + SSM_WIDTH + X_WIDTH:]
    wqb = wf["w_q_b"][li].reshape(Q_LORA, MLA_HEADS, D_QK)
    o["wqb"] = jnp.pad(wqb, ((0, 0), (0, 0), (0, LANES - D_QK))).reshape(Q_LORA, MLA_HEADS * LANES)
    wkv = wf["w_kv_b"][li].reshape(KV_LORA, MLA_HEADS, D_NOPE + D_V)
    o["wk"] = jnp.pad(wkv[:, :, :D_NOPE], ((0, 0), (0, 0), (0, LANES - D_NOPE))).reshape(KV_LORA, MLA_HEADS * LANES)
    o["wv"] = jnp.pad(wkv[:, :, D_NOPE:], ((0, 0), (0, 0), (0, LANES - D_V))).reshape(KV_LORA, MLA_HEADS * LANES)
    wo = wf["w_o_mla"][li].reshape(MLA_HEADS, D_V, D_MODEL)
    o["wo_mla"] = jnp.pad(wo, ((0, 0), (0, LANES - D_V), (0, 0))).reshape(MLA_HEADS * LANES, D_MODEL)
    for n in ("w_glu", "w_o_ssm", "w_mem_kv", "w_o_cross", "w_out", "w_up", "w_down"):
        o[n] = wf[n][li]
    o["w_up_g"], o["w_up_v"] = o["w_up"][:, :D_FF], o["w_up"][:, D_FF:]
    return o


def _row(v):
    return v.reshape(1, -1).astype(F32)


def _pad_lanes(v, n=LANES):
    return jnp.pad(v, (0, n - v.shape[0])).reshape(1, n).astype(F32)


def _layer_fwd(x, mem, tabs, w, p):
    l = x.shape[0]
    rope_c, rope_sa, rope_sb = tabs
    s = {"x": x}
    g_mix, g_qa, g_kva = _row(p["norm_mix_g"]), _row(p["q_a_norm_g"]), _row(p["kv_a_norm_g"])
    g_q, g_k = _pad_lanes(p["q_norm_g"]), _pad_lanes(p["k_norm_g"])

    (h,) = _rowwise("rms_mix", lambda i, r, c: (_rms_f(r[0][...], c[0][...], D_MODEL),), l, 512,
                    [(x, D_MODEL, 0, "row")], [g_mix], [(D_MODEL, BF16)])
    pqkv = _mm("proj_qkv", [(h, w["wqkv"])])
    u = _mm("proj_u", [(h, w["w_u"])])
    xq = _mm("proj_xq", [(h, w["w_xq"])])
    gl = _mm("proj_gate", [(h, w["w_g"])], bn_cap=1024)
    s.update(h=h, pqkv=pqkv, u=u, xq=xq, gl=gl)

    def prep_a(i, r, c):
        return (_rms_f(r[0][:, :Q_LORA], c[0][...], Q_LORA),
                _rms_f(r[0][:, Q_LORA:Q_LORA + KV_LORA], c[1][...], KV_LORA))

    nq, nkv = _rowwise("mla_prep_a", prep_a, l, 512, [(pqkv, QKV_W, 0, "row")], [g_qa, g_kva],
                       [(Q_LORA, BF16), (KV_LORA, BF16)])
    q_raw = _mm("mla_q_b", [(nq, w["wqb"])], bn_cap=1024)
    k_raw = _mm("mla_k_b", [(nkv, w["wk"])], bn_cap=1024)
    v_mla = _mm("mla_v_b", [(nkv, w["wv"])], out_dtype=BF16, bn_cap=1024)

    def prep_b(i, r, c):
        q_ref, k_ref, kr_ref, c_ref, sa_ref, sb_ref = r
        rc, sa, sb, kr = c_ref[...], sa_ref[...], sb_ref[...], kr_ref[...]
        qs, ks = [], []
        for hd in range(MLA_HEADS):
            cols = slice(hd * LANES, (hd + 1) * LANES)
            qs.append(_rope_f(_rms_f(q_ref[:, cols], c[0][...], D_QK), rc, sa, sb))
            ks.append(_rope_f(_rms_f(k_ref[:, cols] + kr, c[1][...], D_QK), rc, sa, sb))
        return jnp.concatenate(qs, axis=1), jnp.concatenate(ks, axis=1)

    hw = MLA_HEADS * LANES
    kr_blk = (Q_LORA + KV_LORA) // LANES
    tab_ins = [(rope_c, LANES, 0, "row"), (rope_sa, LANES, 0, "row"), (rope_sb, LANES, 0, "row")]
    q, k = _rowwise("mla_prep_b", prep_b, l, 256,
                    [(q_raw, hw, 0, "row"), (k_raw, hw, 0, "row"), (pqkv, LANES, kr_blk, "row")] + tab_ins,
                    [g_q, g_k], [(hw, BF16), (hw, BF16)])
    o_a, lse_a = _attn_fwd("mla_attn_fwd", q, k, v_mla, qoff=0, koff=0, voff=0, heads=MLA_HEADS, causal=True,
                           scale=D_QK ** -0.5, bq=512, bk=512)
    ya = _mm("mla_o", [(o_a, w["wo_mla"])], bn_cap=1024)
    s.update(nq=nq, nkv=nkv, q_raw=q_raw, k_raw=k_raw, v_mla=v_mla, q=q, k=k, o_a=o_a, lse_a=lse_a, ya=ya)

    lr = p["ssm_lambda_re"].reshape(SSM_N, 1)
    li = p["ssm_lambda_im"].reshape(SSM_N, 1)
    ldt = jnp.repeat(p["ssm_log_dt"], SSM_STATE).reshape(SSM_N, 1)
    br = p["ssm_b_re"].reshape(SSM_N, SSM_GROUP_CH)
    bi = p["ssm_b_im"].reshape(SSM_N, SSM_GROUP_CH)
    a_re, a_im, bb_re, bb_im = _disc_fwd(lr, li, ldt, br, bi)
    bb_re_d, bb_im_d = _block_diag_in(bb_re).astype(BF16), _block_diag_in(bb_im).astype(BF16)
    cc_re_d = _block_diag_out(p["ssm_c_re"]).astype(BF16)
    cc_imn_d = _block_diag_out(-p["ssm_c_im"]).astype(BF16)
    a_re_row, a_im_row = a_re.reshape(1, SSM_N), a_im.reshape(1, SSM_N)
    d_row = _row(p["ssm_d"])
    b_glu = _row(p["b_glu"])
    bu_re = _mm("s5_bu_re", [(u, bb_re_d)], bn_cap=1024)
    bu_im = _mm("s5_bu_im", [(u, bb_im_d)], bn_cap=1024)
    s_re, s_im = _scan("s5_scan_fwd", bu_re, bu_im, a_re_row, a_im_row, reverse=False)
    ypre = _mm("s5_c", [(s_re, cc_re_d), (s_im, cc_imn_d)])

    def ssm_y(i, r, c):
        return (_gelu(r[0][...] + c[0][...] * r[1][...]),)

    (y_b,) = _rowwise("s5_gelu", ssm_y, l, 512, [(ypre, SSM_WIDTH, 0, "row"), (u, SSM_WIDTH, 0, "row")], [d_row],
                      [(SSM_WIDTH, BF16)])
    z = _mm("s5_glu", [(y_b, w["w_glu"])])

    def ssm_out(i, r, c):
        y = _gelu(r[0][...] + c[0][...] * r[1][...])
        return (y * jax.nn.sigmoid(r[2][...] + c[1][...]),)

    (out_b,) = _rowwise("s5_glu_out", ssm_out, l, 512,
                        [(ypre, SSM_WIDTH, 0, "row"), (u, SSM_WIDTH, 0, "row"), (z, SSM_WIDTH, 0, "row")],
                        [d_row, b_glu], [(SSM_WIDTH, BF16)])
    yb = _mm("s5_o", [(out_b, w["w_o_ssm"])], bn_cap=1024)
    s.update(disc=(lr, li, ldt, br, bi), a_rows=(a_re_row, a_im_row), bb_d=(bb_re_d, bb_im_d),
             cc_d=(cc_re_d, cc_imn_d), s_re=s_re, s_im=s_im, ypre=ypre, y_b=y_b, z=z, out_b=out_b, yb=yb)

    g_mem, g_xq, g_xk = _row(p["mem_norm_g"]), _row(p["xq_norm_g"]), _row(p["xk_norm_g"])
    ml = mem.shape[0]
    (memn,) = _rowwise("rms_mem", lambda i, r, c: (_rms_f(r[0][...], c[0][...], D_MODEL),), ml, 256,
                       [(mem, D_MODEL, 0, "row")], [g_mem], [(D_MODEL, BF16)])
    kvm = _mm("cross_kv", [(memn, w["w_mem_kv"])], bn_cap=1024)

    def headnorm(i, r, c):
        return (jnp.concatenate([_rms_f(r[0][:, hd * LANES:(hd + 1) * LANES], c[0][...], X_HEAD_DIM)
                                 for hd in range(X_HEADS)], axis=1),)

    (xk,) = _rowwise("cross_k_norm", headnorm, ml, 256, [(kvm, X_WIDTH, 0, "row")], [g_xk], [(X_WIDTH, BF16)])
    (xqn,) = _rowwise("cross_q_norm", headnorm, l, 512, [(xq, X_WIDTH, 0, "row")], [g_xq], [(X_WIDTH, BF16)])
    o_c, lse_c = _attn_fwd("cross_attn_fwd", xqn, xk, kvm, qoff=0, koff=0, voff=X_HEADS, heads=X_HEADS,
                           causal=False, scale=X_HEAD_DIM ** -0.5, bq=512, bk=256)
    yc = _mm("cross_o", [(o_c, w["w_o_cross"])], bn_cap=1024)
    s.update(memn=memn, kvm=kvm, xk=xk, xqn=xqn, o_c=o_c, lse_c=lse_c, yc=yc)

    b_gate = _row(p["b_gate"])

    def merge(i, r, c):
        acc = None
        for br_ in range(3):
            g = jax.nn.sigmoid(r[br_][...] + c[0][:, br_ * D_MODEL:(br_ + 1) * D_MODEL])
            t = g * r[3 + br_][...]
            acc = t if acc is None else acc + t
        return (acc,)

    gate_ins = [(gl, D_MODEL, b_, "row") for b_ in range(3)]
    (merged,) = _rowwise("merge", merge, l, 256,
                         gate_ins + [(ya, D_MODEL, 0, "row"), (yb, D_MODEL, 0, "row"), (yc, D_MODEL, 0, "row")],
                         [b_gate], [(D_MODEL, BF16)])
    x1 = _mm("mix_out", [(merged, w["w_out"])], add=x, bn_cap=1024)
    s.update(merged=merged, x1=x1)

    g_ffn = _row(p["norm_ffn_g"])
    (h2,) = _rowwise("rms_ffn", lambda i, r, c: (_rms_f(r[0][...], c[0][...], D_MODEL),), l, 512,
                     [(x1, D_MODEL, 0, "row")], [g_ffn], [(D_MODEL, BF16)])
    up = _mm("ffn_up", [(h2, w["w_up"])])
    conv_w = p["conv_w_full"].astype(F32)
    conv_b = _row(p["conv_b"])

    def conv_glu(i, r, c):
        cg = _conv(r[0], r[2], i, c[0], c[1], 0)
        cv = _conv(r[1], r[3], i, c[0], c[1], D_FF)
        return (cg * jax.nn.sigmoid(cg) * cv,)

    up_ins = [(up, D_FF, 0, "row"), (up, D_FF, 1, "row"), (up, D_FF, 0, "prev"), (up, D_FF, 1, "prev")]
    (act,) = _rowwise("ffn_conv_glu", conv_glu, l, 256, up_ins, [conv_w, conv_b], [(D_FF, BF16)])
    x2 = _mm("ffn_down", [(act, w["w_down"])], add=x1, bn_cap=1024)
    s.update(h2=h2, up=up, act=act, conv_w=conv_w, conv_b=conv_b)
    return x2, s


def _conv(x_ref, halo_ref, i, w_ref, b_ref, col0):
    x = x_ref[...]
    cols = slice(col0, col0 + D_FF)
    return (w_ref[0:1, cols] * _shift_down(x, halo_ref, i, 2) + w_ref[1:2, cols] * _shift_down(x, halo_ref, i, 1)
            + w_ref[2:3, cols] * x + b_ref[:, cols])


def _layer_bwd(dx2, s, mem, tabs, w, p):
    l = dx2.shape[0]
    rope_c, rope_sa, rope_sb = tabs
    x, x1 = s["x"], s["x1"]
    g = {}

    dact = _mm("ffn_down_dx", [(dx2, w["w_down"])], trans_b=True)
    g["w_down"] = _mm_tn("ffn_down_dw", s["act"], dx2)
    up = s["up"]
    nblk_c = l // min(256, l)

    def conv_bwd_a(i, r, c):
        outs, accs = [], []
        for half in range(2):
            x_ref, halo_ref = r[1 + half], r[3 + half]
            col0 = half * D_FF
            cols = slice(col0, col0 + D_FF)
            xv = x_ref[...]
            xm1, xm2 = _shift_down(xv, halo_ref, i, 1), _shift_down(xv, halo_ref, i, 2)
            cv = c[0][0:1, cols] * xm2 + c[0][1:2, cols] * xm1 + c[0][2:3, cols] * xv + c[1][:, cols]
            outs.append((cv, xv, xm1, xm2))
        (cg, xg, xg1, xg2), (cvv, xv, xv1, xv2) = outs
        sig = jax.nn.sigmoid(cg)
        da = r[0][...]
        dcv = da * (cg * sig)
        dcg = da * cvv * (sig * (1.0 + cg * (1.0 - sig)))
        for d, (x0, xa, xb) in ((dcg, (xg, xg1, xg2)), (dcv, (xv, xv1, xv2))):
            accs += [jnp.sum(d, axis=0, keepdims=True), jnp.sum(d * xb, axis=0, keepdims=True),
                     jnp.sum(d * xa, axis=0, keepdims=True), jnp.sum(d * x0, axis=0, keepdims=True)]
        return (dcg, dcv, *accs)

    up_ins = [(up, D_FF, 0, "row"), (up, D_FF, 1, "row"), (up, D_FF, 0, "prev"), (up, D_FF, 1, "prev")]
    res = _rowwise("ffn_conv_glu_bwd", conv_bwd_a, l, 256, [(dact, D_FF, 0, "row")] + up_ins,
                   [s["conv_w"], s["conv_b"]], [(D_FF, F32), (D_FF, F32)], [(1, D_FF)] * 8)
    dcg, dcv = res[0], res[1]
    db_g, dw0_g, dw1_g, dw2_g, db_v, dw0_v, dw1_v, dw2_v = res[2:]
    g["conv_b"] = jnp.concatenate([db_g, db_v], axis=1)[0]
    g["conv_w"] = jnp.concatenate([jnp.concatenate([dw0_g, dw0_v], axis=1), jnp.concatenate([dw1_g, dw1_v], axis=1),
                                   jnp.concatenate([dw2_g, dw2_v], axis=1)], axis=0)

    def conv_bwd_b(i, r, c):
        outs = []
        for half in range(2):
            d = r[half][...]
            cols = slice(half * D_FF, (half + 1) * D_FF)
            outs.append(c[0][2:3, cols] * d + c[0][1:2, cols] * _shift_up(d, r[2 + half], i, nblk_c, 1)
                        + c[0][0:1, cols] * _shift_up(d, r[2 + half], i, nblk_c, 2))
        return tuple(outs)

    dup_g, dup_v = _rowwise("ffn_conv_bwd_x", conv_bwd_b, l, 256,
                            [(dcg, D_FF, 0, "row"), (dcv, D_FF, 0, "row"), (dcg, D_FF, 0, "next"),
                             (dcv, D_FF, 0, "next")], [s["conv_w"]], [(D_FF, BF16), (D_FF, BF16)])
    dh2 = _mm("ffn_up_dx", [(dup_g, w["w_up_g"]), (dup_v, w["w_up_v"])], trans_b=True, bn_cap=1024)
    g["w_up"] = jnp.concatenate([_mm_tn("ffn_up_dw_g", s["h2"], dup_g), _mm_tn("ffn_up_dw_v", s["h2"], dup_v)], axis=1)

    def rms_bwd_res(i, r, c):
        dx, dg = _rms_b(r[0][...], c[0][...], r[1][...], D_MODEL)
        return (dx + r[2][...], dg)

    dx1, g["norm_ffn_g"] = _rowwise(
        "rms_ffn_bwd", rms_bwd_res, l, 512, [(x1, D_MODEL, 0, "row"), (dh2, D_MODEL, 0, "row"), (dx2, D_MODEL, 0, "row")],
        [_row(p["norm_ffn_g"])], [(D_MODEL, F32)], [(1, D_MODEL)])

    dmerged = _mm("mix_out_dx", [(dx1, w["w_out"])], trans_b=True, bn_cap=1024)
    g["w_out"] = _mm_tn("mix_out_dw", s["merged"], dx1)
    gl, ya, yb, yc = s["gl"], s["ya"], s["yb"], s["yc"]

    def merge_bwd(i, r, c):
        dm = r[0][...]
        dys, dgs = [], []
        for b_ in range(3):
            gate = jax.nn.sigmoid(r[1 + b_][...] + c[0][:, b_ * D_MODEL:(b_ + 1) * D_MODEL])
            dys.append(dm * gate)
            dgs.append(dm * r[4 + b_][...] * (gate * (1.0 - gate)))
        dgl = jnp.concatenate(dgs, axis=1)
        return (*dys, dgl, jnp.sum(dgl, axis=0, keepdims=True))

    gate_ins = [(gl, D_MODEL, b_, "row") for b_ in range(3)]
    dya, dyb, dyc, dgl, db_gate = _rowwise(
        "merge_bwd", merge_bwd, l, 256,
        [(dmerged, D_MODEL, 0, "row")] + gate_ins + [(ya, D_MODEL, 0, "row"), (yb, D_MODEL, 0, "row"),
                                                     (yc, D_MODEL, 0, "row")],
        [_row(p["b_gate"])], [(D_MODEL, BF16)] * 3 + [(3 * D_MODEL, BF16)], [(1, 3 * D_MODEL)])
    g["b_gate"] = db_gate[0]

    do_c = _mm("cross_o_dx", [(dyc, w["w_o_cross"])], trans_b=True, out_dtype=BF16)
    g["w_o_cross"] = _mm_tn("cross_o_dw", s["o_c"], dyc)
    kvm = s["kvm"]
    dxqn, dxk, dxv = _attn_bwd("cross_attn_bwd", s["xqn"], s["xk"], kvm, s["o_c"], do_c, s["lse_c"], qoff=0, koff=0,
                               voff=X_HEADS, heads=X_HEADS, causal=False, scale=X_HEAD_DIM ** -0.5, bq=512, bk=256)
    ml = mem.shape[0]

    def headnorm_bwd(i, r, c):
        dxs, dg = [], None
        for hd in range(X_HEADS):
            cols = slice(hd * LANES, (hd + 1) * LANES)
            dx_h, dg_h = _rms_b(r[0][:, cols], c[0][...], r[1][:, cols], X_HEAD_DIM)
            dxs.append(dx_h)
            dg = dg_h if dg is None else dg + dg_h
        return (jnp.concatenate(dxs, axis=1), dg)

    dxq, dg_xq = _rowwise("cross_q_norm_bwd", headnorm_bwd, l, 512,
                          [(s["xq"], X_WIDTH, 0, "row"), (dxqn, X_WIDTH, 0, "row")], [_row(p["xq_norm_g"])],
                          [(X_WIDTH, BF16)], [(1, X_HEAD_DIM)])
    dkvm_k, dg_xk = _rowwise("cross_k_norm_bwd", headnorm_bwd, ml, 256,
                             [(kvm, X_WIDTH, 0, "row"), (dxk, X_WIDTH, 0, "row")], [_row(p["xk_norm_g"])],
                             [(X_WIDTH, F32)], [(1, X_HEAD_DIM)])
    g["xq_norm_g"], g["xk_norm_g"] = dg_xq[0], dg_xk[0]
    dkvm = jnp.concatenate([dkvm_k, dxv], axis=1)
    g["w_mem_kv"] = _mm_tn("cross_kv_dw", s["memn"], dkvm)
    dmemn = _mm("cross_kv_dx", [(dkvm, w["w_mem_kv"])], trans_b=True, bn_cap=1024)

    def rms_bwd_gain_only(i, r, c):
        return (_rms_b(r[0][...], c[0][...], r[1][...], D_MODEL)[1],)

    (dg_mem,) = _rowwise("rms_mem_bwd", rms_bwd_gain_only, ml, 256,
                         [(mem, D_MODEL, 0, "row"), (dmemn, D_MODEL, 0, "row")], [_row(p["mem_norm_g"])], [],
                         [(1, D_MODEL)])
    g["mem_norm_g"] = dg_mem[0]

    dout_b = _mm("s5_o_dx", [(dyb, w["w_o_ssm"])], trans_b=True)
    g["w_o_ssm"] = _mm_tn("s5_o_dw", s["out_b"], dyb)
    ypre, u, z = s["ypre"], s["u"], s["z"]
    d_row, b_glu = _row(p["ssm_d"]), _row(p["b_glu"])
    yuz = [(ypre, SSM_WIDTH, 0, "row"), (u, SSM_WIDTH, 0, "row"), (z, SSM_WIDTH, 0, "row")]

    def glu_bwd_z(i, r, c):
        y = _gelu(r[1][...] + c[0][...] * r[2][...])
        sg = jax.nn.sigmoid(r[3][...] + c[1][...])
        dz = r[0][...] * y * (sg * (1.0 - sg))
        return (dz, jnp.sum(dz, axis=0, keepdims=True))

    dz, db_glu = _rowwise("s5_glu_bwd_z", glu_bwd_z, l, 512, [(dout_b, SSM_WIDTH, 0, "row")] + yuz, [d_row, b_glu],
                          [(SSM_WIDTH, BF16)], [(1, SSM_WIDTH)])
    g["b_glu"] = db_glu[0]
    g["w_glu"] = _mm_tn("s5_glu_dw", s["y_b"], dz)
    dy2 = _mm("s5_glu_dx", [(dz, w["w_glu"])], trans_b=True)

    def gelu_bwd(i, r, c):
        t = r[2][...] + c[0][...] * r[3][...]
        sg = jax.nn.sigmoid(r[4][...] + c[1][...])
        dt = (r[0][...] * sg + r[1][...]) * _gelu_grad(t)
        return (dt, c[0][...] * dt, jnp.sum(dt * r[3][...], axis=0, keepdims=True))

    dypre, du_skip, dd = _rowwise(
        "s5_gelu_bwd", gelu_bwd, l, 512, [(dout_b, SSM_WIDTH, 0, "row"), (dy2, SSM_WIDTH, 0, "row")] + yuz,
        [d_row, b_glu], [(SSM_WIDTH, BF16), (SSM_WIDTH, F32)], [(1, SSM_WIDTH)])
    g["ssm_d"] = dd.reshape(SSM_GROUPS, SSM_GROUP_CH)
    cc_re_d, cc_imn_d = s["cc_d"]
    bb_re_d, bb_im_d = s["bb_d"]
    a_re_row, a_im_row = s["a_rows"]
    s_re, s_im = s["s_re"], s["s_im"]
    g_re = _mm("s5_c_dx_re", [(dypre, cc_re_d)], trans_b=True, bn_cap=1024)
    g_im = _mm("s5_c_dx_im", [(dypre, cc_imn_d)], trans_b=True, bn_cap=1024)
    dcc_re = _mm_tn("s5_c_dw_re", s_re, dypre)
    dcc_imn = _mm_tn("s5_c_dw_im", s_im, dypre)
    g["ssm_c_re"] = _diag_blocks(dcc_re, SSM_STATE, SSM_GROUP_CH).transpose(0, 2, 1)
    g["ssm_c_im"] = -_diag_blocks(dcc_imn, SSM_STATE, SSM_GROUP_CH).transpose(0, 2, 1)
    lam_re, lam_im = _scan("s5_scan_bwd", g_re, g_im, a_re_row, -a_im_row, reverse=True)

    def da_sum(i, r, c):
        lr_, li_ = r[0][...], r[1][...]
        pr, pi = _shift_down(r[2][...], r[4], i, 1), _shift_down(r[3][...], r[5], i, 1)
        return (jnp.sum(lr_ * pr + li_ * pi, axis=0, keepdims=True), jnp.sum(li_ * pr - lr_ * pi, axis=0, keepdims=True))

    da_re, da_im = _rowwise(
        "s5_da", da_sum, l, 256,
        [(lam_re, SSM_N, 0, "row"), (lam_im, SSM_N, 0, "row"), (s_re, SSM_N, 0, "row"), (s_im, SSM_N, 0, "row"),
         (s_re, SSM_N, 0, "prev"), (s_im, SSM_N, 0, "prev")], [], [], [(1, SSM_N), (1, SSM_N)])
    dbb_re_d = _mm_tn("s5_b_dw_re", u, lam_re)
    dbb_im_d = _mm_tn("s5_b_dw_im", u, lam_im)
    dbb_re = _diag_blocks(dbb_re_d, SSM_GROUP_CH, SSM_STATE).transpose(0, 2, 1).reshape(SSM_N, SSM_GROUP_CH)
    dbb_im = _diag_blocks(dbb_im_d, SSM_GROUP_CH, SSM_STATE).transpose(0, 2, 1).reshape(SSM_N, SSM_GROUP_CH)
    dlr, dli, dldt, dbr, dbi = _disc_bwd(*s["disc"], da_re.reshape(SSM_N, 1), da_im.reshape(SSM_N, 1), dbb_re, dbb_im)
    g["ssm_lambda_re"] = dlr.reshape(SSM_GROUPS, SSM_STATE)
    g["ssm_lambda_im"] = dli.reshape(SSM_GROUPS, SSM_STATE)
    g["ssm_log_dt"] = dldt.reshape(SSM_GROUPS, SSM_STATE).sum(axis=1)
    g["ssm_b_re"] = dbr.reshape(SSM_GROUPS, SSM_STATE, SSM_GROUP_CH)
    g["ssm_b_im"] = dbi.reshape(SSM_GROUPS, SSM_STATE, SSM_GROUP_CH)
    du = _mm("s5_b_dx", [(lam_re, bb_re_d), (lam_im, bb_im_d)], trans_b=True, add=du_skip, out_dtype=BF16)

    do_a = _mm("mla_o_dx", [(dya, w["wo_mla"])], trans_b=True, out_dtype=BF16, bn_cap=1024)
    dwo = _mm_tn("mla_o_dw", s["o_a"], dya)
    g["w_o_mla"] = dwo.reshape(MLA_HEADS, LANES, D_MODEL)[:, :D_V].reshape(MLA_HEADS * D_V, D_MODEL)
    dq, dk, dv = _attn_bwd("mla_attn_bwd", s["q"], s["k"], s["v_mla"], s["o_a"], do_a, s["lse_a"], qoff=0, koff=0,
                           voff=0, heads=MLA_HEADS, causal=True, scale=D_QK ** -0.5, bq=512, bk=512)
    hw = MLA_HEADS * LANES
    kr_blk = (Q_LORA + KV_LORA) // LANES
    pqkv = s["pqkv"]
    g_q, g_k = _pad_lanes(p["q_norm_g"]), _pad_lanes(p["k_norm_g"])
    lane = lax.broadcasted_iota(jnp.int32, (1, LANES), 1)
    kr_mask = jnp.logical_and(lane >= KR_LO, lane < KR_LO + D_ROPE).astype(F32)

    def prep_b_bwd(i, r, c):
        dq_ref, dk_ref, q_ref, k_ref, kr_ref, c_ref, sa_ref, sb_ref = r
        rc, sa, sb, kr = c_ref[...], sa_ref[...], sb_ref[...], kr_ref[...]
        dqs, dks, dkr, dgq, dgk = [], [], None, None, None
        for hd in range(MLA_HEADS):
            cols = slice(hd * LANES, (hd + 1) * LANES)
            dxq, dgq_h = _rms_b(q_ref[:, cols], c[0][...], _rope_b(dq_ref[:, cols], rc, sa, sb), D_QK)
            dxk, dgk_h = _rms_b(k_ref[:, cols] + kr, c[1][...], _rope_b(dk_ref[:, cols], rc, sa, sb), D_QK)
            dqs.append(dxq)
            dks.append(dxk)
            dkr = dxk if dkr is None else dkr + dxk
            dgq = dgq_h if dgq is None else dgq + dgq_h
            dgk = dgk_h if dgk is None else dgk + dgk_h
        return (jnp.concatenate(dqs, axis=1), jnp.concatenate(dks, axis=1), dkr * c[2][...], dgq, dgk)

    tab_ins = [(rope_c, LANES, 0, "row"), (rope_sa, LANES, 0, "row"), (rope_sb, LANES, 0, "row")]
    dq_raw, dk_raw, dkr, dg_q, dg_k = _rowwise(
        "mla_prep_b_bwd", prep_b_bwd, l, 256,
        [(dq, hw, 0, "row"), (dk, hw, 0, "row"), (s["q_raw"], hw, 0, "row"), (s["k_raw"], hw, 0, "row"),
         (pqkv, LANES, kr_blk, "row")] + tab_ins, [g_q, g_k, kr_mask],
        [(hw, BF16), (hw, BF16), (LANES, F32)], [(1, LANES), (1, LANES)])
    g["q_norm_g"], g["k_norm_g"] = dg_q[0, :D_QK], dg_k[0, :D_QK]
    dnq = _mm("mla_q_b_dx", [(dq_raw, w["wqb"])], trans_b=True)
    dnkv = _mm("mla_kv_b_dx", [(dk_raw, w["wk"]), (dv, w["wv"])], trans_b=True)
    dwqb = _mm_tn("mla_q_b_dw", s["nq"], dq_raw)
    g["w_q_b"] = dwqb.reshape(Q_LORA, MLA_HEADS, LANES)[:, :, :D_QK].reshape(Q_LORA, MLA_HEADS * D_QK)
    dwk = _mm_tn("mla_k_b_dw", s["nkv"], dk_raw).reshape(KV_LORA, MLA_HEADS, LANES)[:, :, :D_NOPE]
    dwv = _mm_tn("mla_v_b_dw", s["nkv"], dv).reshape(KV_LORA, MLA_HEADS, LANES)[:, :, :D_V]
    g["w_kv_b"] = jnp.concatenate([dwk, dwv], axis=2).reshape(KV_LORA, MLA_HEADS * (D_NOPE + D_V))

    def prep_a_bwd(i, r, c):
        dcq, dgqa = _rms_b(r[0][:, :Q_LORA], c[0][...], r[1][...], Q_LORA)
        dckv, dgkva = _rms_b(r[0][:, Q_LORA:Q_LORA + KV_LORA], c[1][...], r[2][...], KV_LORA)
        return (jnp.concatenate([dcq, dckv, r[3][...]], axis=1), dgqa, dgkva)

    dpqkv, dg_qa, dg_kva = _rowwise(
        "mla_prep_a_bwd", prep_a_bwd, l, 512,
        [(pqkv, QKV_W, 0, "row"), (dnq, Q_LORA, 0, "row"), (dnkv, KV_LORA, 0, "row"), (dkr, LANES, 0, "row")],
        [_row(p["q_a_norm_g"]), _row(p["kv_a_norm_g"])], [(QKV_W, BF16)], [(1, Q_LORA), (1, KV_LORA)])
    g["q_a_norm_g"], g["kv_a_norm_g"] = dg_qa[0], dg_kva[0]

    h = s["h"]
    dh = _mm("proj_dx", [(dpqkv, w["wqkv"]), (du, w["w_u"]), (dxq, w["w_xq"]), (dgl, w["w_g"])], trans_b=True,
             bn_cap=1024)
    dwqkv = _mm_tn("proj_qkv_dw", h, dpqkv)
    kr0 = Q_LORA + KV_LORA + KR_LO
    g["w_in"] = jnp.concatenate([dwqkv[:, :Q_LORA + KV_LORA], dwqkv[:, kr0:kr0 + D_ROPE], _mm_tn("proj_u_dw", h, du),
                                 _mm_tn("proj_xq_dw", h, dxq), _mm_tn("proj_gate_dw", h, dgl)], axis=1)
    dx, dg_mix = _rowwise(
        "rms_mix_bwd", rms_bwd_res, l, 512, [(x, D_MODEL, 0, "row"), (dh, D_MODEL, 0, "row"), (dx1, D_MODEL, 0, "row")],
        [_row(p["norm_mix_g"])], [(D_MODEL, F32)], [(1, D_MODEL)])
    g["norm_mix_g"] = dg_mix[0]
    g["norm_ffn_g"] = g["norm_ffn_g"][0]
    return dx, g


def _rope_tables(positions):
    inv_freq = ROPE_THETA ** (-jnp.arange(0, D_ROPE, 2, dtype=F32) / D_ROPE)
    ang = positions.astype(F32)[:, None] * inv_freq
    cos, sin = jnp.cos(ang), jnp.sin(ang)
    l = positions.shape[0]
    one, zero = jnp.ones((l, D_NOPE), F32), lambda n: jnp.zeros((l, n), F32)
    pad = LANES - D_QK
    rope_c = jnp.concatenate([one, cos, cos, zero(pad)], axis=1)
    rope_sa = jnp.concatenate([zero(D_NOPE), -sin, zero(16), zero(pad)], axis=1)
    rope_sb = jnp.concatenate([zero(D_NOPE + 16), sin, zero(pad)], axis=1)
    return rope_c, rope_sa, rope_sb


def kernel(x, mem, positions, norm_mix_g, w_in, q_a_norm_g, w_q_b, kv_a_norm_g, w_kv_b, q_norm_g, k_norm_g, w_o_mla, ssm_lambda_re, ssm_lambda_im, ssm_log_dt, ssm_b_re, ssm_b_im, ssm_c_re, ssm_c_im, ssm_d, w_glu, b_glu, w_o_ssm, mem_norm_g, w_mem_kv, xq_norm_g, xk_norm_g, w_o_cross, b_gate, w_out, norm_ffn_g, w_up, conv_w, conv_b, w_down, loss_target, m_norm_mix_g, m_w_in, m_q_a_norm_g, m_w_q_b, m_kv_a_norm_g, m_w_kv_b, m_q_norm_g, m_k_norm_g, m_w_o_mla, m_ssm_lambda_re, m_ssm_lambda_im, m_ssm_log_dt, m_ssm_b_re, m_ssm_b_im, m_ssm_c_re, m_ssm_c_im, m_ssm_d, m_w_glu, m_b_glu, m_w_o_ssm, m_mem_norm_g, m_w_mem_kv, m_xq_norm_g, m_xk_norm_g, m_w_o_cross, m_b_gate, m_w_out, m_norm_ffn_g, m_w_up, m_conv_w, m_conv_b, m_w_down, v_norm_mix_g, v_w_in, v_q_a_norm_g, v_w_q_b, v_kv_a_norm_g, v_w_kv_b, v_q_norm_g, v_k_norm_g, v_w_o_mla, v_ssm_lambda_re, v_ssm_lambda_im, v_ssm_log_dt, v_ssm_b_re, v_ssm_b_im, v_ssm_c_re, v_ssm_c_im, v_ssm_d, v_w_glu, v_b_glu, v_w_o_ssm, v_mem_norm_g, v_w_mem_kv, v_xq_norm_g, v_xk_norm_g, v_w_o_cross, v_b_gate, v_w_out, v_norm_ffn_g, v_w_up, v_conv_w, v_conv_b, v_w_down):
    a = dict(locals())
    wts = {n: a[n] for n in WEIGHT_ORDER}
    m_in = {n: a["m_" + n] for n in WEIGHT_ORDER}
    v_in = {n: a["v_" + n] for n in WEIGHT_ORDER}
    depth = norm_mix_g.shape[0]
    x0, mem0, pos0, tgt = x[0], mem[0], positions[0], loss_target[0]
    l = x0.shape[0]

    def wire(n):
        if n == "conv_w":
            return lax.bitcast_convert_type(wts[n], BF16).reshape(-1)
        return wts[n].astype(BF16).reshape(-1)

    local_flat = jnp.concatenate([wire(n) for n, _, _ in SHARDED])
    gathered = _exchange("gather_weights", _pad_rows(local_flat, 16), gather=True)
    gathered = gathered.reshape(N_DEV, -1)
    wf, off = {}, 0
    for n, _, ax in SHARDED:
        cnt = wts[n].size * (2 if n == "conv_w" else 1)
        blocks = gathered[:, off:off + cnt]
        if n == "conv_w":
            blocks = lax.bitcast_convert_type(blocks.reshape(N_DEV, -1, 2), F32)
        wf[n] = _from_shards(ax, blocks, wts[n].shape)
        off += cnt

    tabs = _rope_tables(pos0)
    layer_w = [_layer_weights(wf, i) for i in range(depth)]
    layer_p = []
    for i in range(depth):
        p = {n: wts[n][i] for n in REPLICATED}
        p["conv_w_full"] = wf["conv_w"][i]
        layer_p.append(p)

    saved = []
    xc = x0
    for i in range(depth):
        xc, s = _layer_fwd(xc, mem0, tabs, layer_w[i], layer_p[i])
        saved.append(s)

    def loss_fn(i, r, c):
        d = r[0][...] - r[1][...]
        return (d * (1.0 / D_MODEL), jnp.sum(d * d, axis=0, keepdims=True))

    dy, sq = _rowwise("loss", loss_fn, l, 512, [(xc, D_MODEL, 0, "row"), (tgt, D_MODEL, 0, "row")], [],
                      [(D_MODEL, F32)], [(1, D_MODEL)])
    loss = lax.psum(0.5 * jnp.sum(sq) / D_MODEL, ("x", "y", "c"))

    grads = [None] * depth
    dxc = dy
    for i in reversed(range(depth)):
        dxc, grads[i] = _layer_bwd(dxc, saved[i], mem0, tabs, layer_w[i], layer_p[i])
    grad_x = dxc[None]
    gfull = {n: jnp.stack([grads[i][n] for i in range(depth)]) for n in WEIGHT_ORDER}

    send = jnp.concatenate(
        [_to_shards(ax, gfull[n]) for n, _, ax in SHARDED]
        + [jnp.broadcast_to(gfull[n].reshape(1, -1), (N_DEV, gfull[n].size)) for n in REPLICATED], axis=1)
    rcv = _exchange("exchange_grads", _pad_rows(send, 256), gather=False)
    order = [n for n, _, _ in SHARDED] + list(REPLICATED)

    def flat_local(d):
        return _pad_rows(jnp.concatenate([d[n].astype(F32).reshape(-1) for n in order]), 256)

    g_sum, delta, m_new, v_new = _adamw(rcv, flat_local(wts), flat_local(m_in), flat_local(v_in))
    outs = {}
    for tag, arr in (("grad", g_sum), ("delta", delta), ("m", m_new), ("v", v_new)):
        flat, off = arr.reshape(-1), 0
        for n in order:
            cnt = wts[n].size
            outs[(tag, n)] = flat[off:off + cnt].reshape(wts[n].shape)
            off += cnt
    result = [loss, grad_x]
    for tag in ("grad", "delta", "m", "v"):
        result += [outs[(tag, n)] for n in WEIGHT_ORDER]
    return tuple(result)
```

```python
import math

import jax
import jax.numpy as jnp
from jax import lax
from jax.experimental import pallas as pl
from jax.experimental.pallas import tpu as pltpu

F32 = jnp.float32
BF16 = jnp.bfloat16

N_DEV = 8
LANES = 128
LOG2E = math.log2(math.e)
VMEM_LIMIT_BYTES = 56 * 1024 * 1024

D_MODEL = 1024
EPS = 1e-6
MLA_HEADS = 8
Q_LORA = 384
KV_LORA = 256
D_NOPE = 64
D_ROPE = 32
D_QK = D_NOPE + D_ROPE
D_V = 64
ROPE_THETA = 10000.0
SSM_GROUPS = 32
SSM_GROUP_CH = 16
SSM_WIDTH = 512
SSM_STATE = 64
SSM_N = SSM_GROUPS * SSM_STATE
X_HEADS = 4
X_HEAD_DIM = 128
X_WIDTH = 512
D_FF = 2816
IN_WIDTH = Q_LORA + KV_LORA + D_ROPE + SSM_WIDTH + X_WIDTH + 3 * D_MODEL
QKV_W = Q_LORA + KV_LORA + LANES
KR_LO = D_NOPE

ADAM_LR = 0.001
ADAM_B1 = 0.9
ADAM_B2 = 0.999
ADAM_EPS = 1e-08
ADAM_WD = 0.01
ADAM_STEP = 10

SHARDED = (
    ("w_in", (D_MODEL, IN_WIDTH), 1),
    ("w_q_b", (Q_LORA, MLA_HEADS * D_QK), 1),
    ("w_kv_b", (KV_LORA, MLA_HEADS * (D_NOPE + D_V)), 1),
    ("w_o_mla", (MLA_HEADS * D_V, D_MODEL), 1),
    ("w_glu", (SSM_WIDTH, SSM_WIDTH), 0),
    ("w_o_ssm", (SSM_WIDTH, D_MODEL), 1),
    ("w_mem_kv", (D_MODEL, 2 * X_WIDTH), 0),
    ("w_o_cross", (X_WIDTH, D_MODEL), 1),
    ("w_out", (D_MODEL, D_MODEL), 0),
    ("w_up", (D_MODEL, 2 * D_FF), 1),
    ("conv_w", (3, 2 * D_FF), 1),
    ("w_down", (D_FF, D_MODEL), 0),
)
REPLICATED = (
    "norm_mix_g", "q_a_norm_g", "kv_a_norm_g", "q_norm_g", "k_norm_g", "ssm_lambda_re", "ssm_lambda_im",
    "ssm_log_dt", "ssm_b_re", "ssm_b_im", "ssm_c_re", "ssm_c_im", "ssm_d", "b_glu", "mem_norm_g",
    "xq_norm_g", "xk_norm_g", "b_gate", "norm_ffn_g", "conv_b",
)
WEIGHT_ORDER = (
    "norm_mix_g", "w_in", "q_a_norm_g", "w_q_b", "kv_a_norm_g", "w_kv_b", "q_norm_g", "k_norm_g", "w_o_mla",
    "ssm_lambda_re", "ssm_lambda_im", "ssm_log_dt", "ssm_b_re", "ssm_b_im", "ssm_c_re", "ssm_c_im", "ssm_d",
    "w_glu", "b_glu", "w_o_ssm", "mem_norm_g", "w_mem_kv", "xq_norm_g", "xk_norm_g", "w_o_cross", "b_gate",
    "w_out", "norm_ffn_g", "w_up", "conv_w", "conv_b", "w_down",
)


def _params(**kw):
    return pltpu.CompilerParams(vmem_limit_bytes=VMEM_LIMIT_BYTES, **kw)


def _pick(n, cap):
    if n <= cap:
        return n
    best = None
    for m in range(LANES, cap + 1, LANES):
        if n % m == 0:
            best = m
    assert best is not None, n
    return best


_NN = (((1,), (0,)), ((), ()))
_NT = (((1,), (1,)), ((), ()))
_TN = (((0,), (0,)), ((), ()))


def _dot(a, b, dn):
    return lax.dot_general(a.astype(BF16), b.astype(BF16), dn, preferred_element_type=F32)


def _mm(name, pairs, *, trans_b=False, add=None, out_dtype=F32, bm=512, bn_cap=512):
    m = pairs[0][0].shape[0]
    n = pairs[0][1].shape[0 if trans_b else 1]
    bm = min(bm, m)
    bn = _pick(n, bn_cap)
    npair = len(pairs)

    def body(*refs):
        o_ref = refs[-1]
        acc = None
        for p in range(npair):
            d = _dot(refs[2 * p][...], refs[2 * p + 1][...], _NT if trans_b else _NN)
            acc = d if acc is None else acc + d
        if add is not None:
            acc = acc + refs[2 * npair][...]
        o_ref[...] = acc.astype(out_dtype)

    in_specs, args = [], []
    for a, b in pairs:
        k = a.shape[1]
        in_specs.append(pl.BlockSpec((bm, k), lambda i, j: (i, 0)))
        if trans_b:
            in_specs.append(pl.BlockSpec((bn, k), lambda i, j: (j, 0)))
        else:
            in_specs.append(pl.BlockSpec((k, bn), lambda i, j: (0, j)))
        args += [a, b]
    if add is not None:
        in_specs.append(pl.BlockSpec((bm, bn), lambda i, j: (i, j)))
        args.append(add)
    return pl.pallas_call(
        body, name=name, grid=(m // bm, n // bn), in_specs=in_specs,
        out_specs=pl.BlockSpec((bm, bn), lambda i, j: (i, j)),
        out_shape=jax.ShapeDtypeStruct((m, n), out_dtype), compiler_params=_params(),
    )(*args)


def _mm_tn(name, a, b, *, bm_cap=512, bn_cap=1536, bk=1024):
    l, m = a.shape
    n = b.shape[1]
    bm, bn, bk = _pick(m, bm_cap), _pick(n, bn_cap), min(bk, l)

    def body(a_ref, b_ref, o_ref):
        @pl.when(pl.program_id(2) == 0)
        def _():
            o_ref[...] = jnp.zeros_like(o_ref)

        o_ref[...] += _dot(a_ref[...], b_ref[...], _TN)

    return pl.pallas_call(
        body, name=name, grid=(m // bm, n // bn, l // bk),
        in_specs=[pl.BlockSpec((bk, bm), lambda i, j, k: (k, i)), pl.BlockSpec((bk, bn), lambda i, j, k: (k, j))],
        out_specs=pl.BlockSpec((bm, bn), lambda i, j, k: (i, j)),
        out_shape=jax.ShapeDtypeStruct((m, n), F32), compiler_params=_params(),
    )(a, b)


def _rowwise(name, fn, nrows, bm, row_ins, consts, row_outs, acc_outs=()):
    bm = min(bm, nrows)
    nblk = nrows // bm
    sub = bm // 8
    nin, nc, nro = len(row_ins), len(consts), len(row_outs)

    def body(*refs):
        i = pl.program_id(0)
        outs = fn(i, refs[:nin], refs[nin:nin + nc])
        o_refs = refs[nin + nc:nin + nc + nro]
        a_refs = refs[nin + nc + nro:]
        for r, v in zip(o_refs, outs[:nro]):
            r[...] = v.astype(r.dtype)
        if a_refs:
            @pl.when(i == 0)
            def _():
                for r in a_refs:
                    r[...] = jnp.zeros_like(r)

            for r, v in zip(a_refs, outs[nro:]):
                r[...] += v

    in_specs, args = [], []
    for arr, w, cb, kind in row_ins:
        if kind == "row":
            in_specs.append(pl.BlockSpec((bm, w), lambda i, cb=cb: (i, cb)))
        elif kind == "prev":
            in_specs.append(pl.BlockSpec((8, w), lambda i, cb=cb: (jnp.maximum(i * sub - 1, 0), cb)))
        else:
            in_specs.append(pl.BlockSpec((8, w), lambda i, cb=cb: (jnp.minimum((i + 1) * sub, nrows // 8 - 1), cb)))
        args.append(arr)
    for c in consts:
        in_specs.append(pl.BlockSpec(c.shape, lambda i: (0, 0)))
        args.append(c)
    out_specs = [pl.BlockSpec((bm, w), lambda i: (i, 0)) for w, _ in row_outs]
    out_specs += [pl.BlockSpec(s, lambda i: (0, 0)) for s in acc_outs]
    out_shape = [jax.ShapeDtypeStruct((nrows, w), dt) for w, dt in row_outs]
    out_shape += [jax.ShapeDtypeStruct(s, F32) for s in acc_outs]
    res = pl.pallas_call(
        body, name=name, grid=(nblk,), in_specs=in_specs, out_specs=out_specs, out_shape=out_shape,
        compiler_params=_params(),
    )(*args)
    return res


def _rms_f(x, g, n):
    r = lax.rsqrt(jnp.sum(x * x, axis=-1, keepdims=True) * (1.0 / n) + EPS)
    return x * r * g


def _rms_b(x, g, dy, n):
    r = lax.rsqrt(jnp.sum(x * x, axis=-1, keepdims=True) * (1.0 / n) + EPS)
    gx = dy * g
    dx = r * gx - x * (r * r * r * (jnp.sum(x * gx, axis=-1, keepdims=True) * (1.0 / n)))
    dg = jnp.sum(dy * (x * r), axis=0, keepdims=True)
    return dx, dg


def _rope_f(x, c, sa, sb):
    return x * c + pltpu.roll(x, LANES - 16, 1) * sa + pltpu.roll(x, 16, 1) * sb


def _rope_b(g, c, sa, sb):
    return g * c + pltpu.roll(g * sa, 16, 1) + pltpu.roll(g * sb, LANES - 16, 1)


def _gelu(x):
    c = math.sqrt(2.0 / math.pi)
    return 0.5 * x * (1.0 + jnp.tanh(c * (x + 0.044715 * (x * x * x))))


def _gelu_grad(x):
    c = math.sqrt(2.0 / math.pi)
    th = jnp.tanh(c * (x + 0.044715 * (x * x * x)))
    return 0.5 * (1.0 + th) + 0.5 * x * (1.0 - th * th) * (c * (1.0 + 3.0 * 0.044715 * (x * x)))


def _row_ids(bm):
    return lax.broadcasted_iota(jnp.int32, (bm, 1), 0)


def _shift_down(x, halo_ref, i, k):
    bm = x.shape[0]
    row = _row_ids(bm)
    live = (i > 0).astype(F32)
    out = pltpu.roll(x, k, 0)
    for r in range(k):
        e = (row == r).astype(F32)
        out = out * (1.0 - e) + e * (halo_ref[8 - k + r:8 - k + r + 1, :] * live)
    return out


def _shift_up(x, halo_ref, i, nblk, k):
    bm = x.shape[0]
    row = _row_ids(bm)
    live = (i < nblk - 1).astype(F32)
    out = pltpu.roll(x, bm - k, 0)
    for r in range(k):
        e = (row == bm - k + r).astype(F32)
        out = out * (1.0 - e) + e * (halo_ref[r:r + 1, :] * live)
    return out


def _live_pairs(nq, nk, bq, bk, causal, key_major):
    pairs = [(i, j) for i in range(nq) for j in range(nk) if not causal or j * bk <= i * bq + bq - 1]
    if key_major:
        pairs.sort(key=lambda ij: (ij[1], ij[0]))
    return (jnp.asarray([p[0] for p in pairs], jnp.int32), jnp.asarray([p[1] for p in pairs], jnp.int32))


def _attn_fwd(name, qa, ka, vta, *, qoff, koff, voff, heads, causal, scale, bq, bk):
    lq, lk = qa.shape[0], ka.shape[0]
    bq, bk = min(bq, lq), min(bk, lk)
    nq, nk = lq // bq, lk // bk
    c2 = scale * LOG2E
    tab_i, tab_j = _live_pairs(nq, nk, bq, bk, causal, key_major=False)

    def body(ti, tj, q_ref, k_ref, vt_ref, o_ref, lse_ref, m_s, l_s, acc_s):
        t = pl.program_id(1)
        i, j = ti[t], tj[t]
        j_last = jnp.minimum(nk - 1, (i * bq + bq - 1) // bk) if causal else nk - 1

        @pl.when(j == 0)
        def _():
            m_s[...] = jnp.full_like(m_s, -1e30)
            l_s[...] = jnp.zeros_like(l_s)
            acc_s[...] = jnp.zeros_like(acc_s)

        def step(masked):
            st = _dot(k_ref[...], q_ref[...], _NT) * c2
            if masked:
                key = j * bk + lax.broadcasted_iota(jnp.int32, (bk, bq), 0)
                qry = i * bq + lax.broadcasted_iota(jnp.int32, (bk, bq), 1)
                st = jnp.where(key <= qry, st, -1e30)
            m_prev = m_s[...]
            m_new = jnp.maximum(m_prev, jnp.max(st, axis=0, keepdims=True))
            alpha = jnp.exp2(m_prev - m_new)
            pt = jnp.exp2(st - m_new)
            l_s[...] = alpha * l_s[...] + jnp.sum(pt, axis=0, keepdims=True)
            acc_s[...] = alpha * acc_s[...] + _dot(vt_ref[...], pt, _NN)
            m_s[...] = m_new

        if causal:
            full = j * bk + bk - 1 <= i * bq
            pl.when(full)(lambda: step(False))
            pl.when(jnp.logical_not(full))(lambda: step(True))
        else:
            step(False)

        @pl.when(j == j_last)
        def _():
            l = l_s[...]
            o_ref[...] = (acc_s[...] / l).T.astype(o_ref.dtype)
            lse_ref[0] = m_s[...] + jnp.log2(l)

    grid_spec = pltpu.PrefetchScalarGridSpec(
        num_scalar_prefetch=2, grid=(heads, int(tab_i.shape[0])),
        in_specs=[pl.BlockSpec((bq, LANES), lambda h, t, ti, tj: (ti[t], qoff + h)),
                  pl.BlockSpec((bk, LANES), lambda h, t, ti, tj: (tj[t], koff + h)),
                  pl.BlockSpec((LANES, bk), lambda h, t, ti, tj: (voff + h, tj[t]))],
        out_specs=[pl.BlockSpec((bq, LANES), lambda h, t, ti, tj: (ti[t], h)),
                   pl.BlockSpec((1, 1, bq), lambda h, t, ti, tj: (h, 0, ti[t]))],
        scratch_shapes=[pltpu.VMEM((1, bq), F32), pltpu.VMEM((1, bq), F32), pltpu.VMEM((LANES, bq), F32)])
    return pl.pallas_call(
        body, name=name, grid_spec=grid_spec,
        out_shape=[jax.ShapeDtypeStruct((lq, heads * LANES), BF16), jax.ShapeDtypeStruct((heads, 1, lq), F32)],
        compiler_params=_params(),
    )(tab_i, tab_j, qa, ka, vta)


def _attn_delta(name, oa, doa, *, heads, bq):
    lq = oa.shape[0]
    bq = min(bq, lq)

    def body(o_ref, do_ref, d_ref):
        d_ref[0] = jnp.sum((o_ref[...].astype(F32) * do_ref[...].astype(F32)).T, axis=0, keepdims=True)

    blk = pl.BlockSpec((bq, LANES), lambda h, i: (i, h))
    return pl.pallas_call(
        body, name=name, grid=(heads, lq // bq), in_specs=[blk, blk],
        out_specs=pl.BlockSpec((1, 1, bq), lambda h, i: (h, 0, i)),
        out_shape=jax.ShapeDtypeStruct((heads, 1, lq), F32), compiler_params=_params(),
    )(oa, doa)


def _attn_bwd(name, qa, ka, kta, va, doa, lsea, deltaa, *, qoff, koff, voff, heads, causal, scale, bq, bk):
    lq, lk = qa.shape[0], ka.shape[0]
    bq, bk = min(bq, lq), min(bk, lk)
    nq, nk = lq // bq, lk // bk
    c2 = scale * LOG2E
    tab_i, tab_j = _live_pairs(nq, nk, bq, bk, causal, key_major=True)

    def body(ti, tj, q_ref, k_ref, kt_ref, v_ref, do_ref, lse_ref, delta_ref, dqt_ref, dk_ref, dv_ref):
        t = pl.program_id(1)
        i, j = ti[t], tj[t]
        i_first = (j * bk) // bq if causal else 0

        @pl.when(t == 0)
        def _():
            dqt_ref[...] = jnp.zeros_like(dqt_ref)

        @pl.when(i == i_first)
        def _():
            dk_ref[...] = jnp.zeros_like(dk_ref)
            dv_ref[...] = jnp.zeros_like(dv_ref)

        def step(masked):
            q, k, v, do = q_ref[...], k_ref[...], v_ref[...], do_ref[...]
            st = _dot(k, q, _NT) * c2
            if masked:
                key = j * bk + lax.broadcasted_iota(jnp.int32, (bk, bq), 0)
                qry = i * bq + lax.broadcasted_iota(jnp.int32, (bk, bq), 1)
                st = jnp.where(key <= qry, st, -1e30)
            pt = jnp.exp2(st - lse_ref[0])
            dv_ref[...] += _dot(pt, do, _NN)
            dpt = _dot(v, do, _NT)
            dst = (pt * (dpt - delta_ref[0]) * scale).astype(BF16)
            dk_ref[...] += _dot(dst, q, _NN)
            dqt_ref[0, i] += _dot(kt_ref[...], dst, _NN)

        if causal:
            full = j * bk + bk - 1 <= i * bq
            pl.when(full)(lambda: step(False))
            pl.when(jnp.logical_not(full))(lambda: step(True))
        else:
            step(False)

    q_spec = lambda off: pl.BlockSpec((bq, LANES), lambda h, t, ti, tj: (ti[t], off + h))
    kv_spec = lambda off: pl.BlockSpec((bk, LANES), lambda h, t, ti, tj: (tj[t], off + h))
    row_spec = pl.BlockSpec((1, 1, bq), lambda h, t, ti, tj: (h, 0, ti[t]))
    grid_spec = pltpu.PrefetchScalarGridSpec(
        num_scalar_prefetch=2, grid=(heads, int(tab_i.shape[0])),
        in_specs=[q_spec(qoff), kv_spec(koff), pl.BlockSpec((LANES, bk), lambda h, t, ti, tj: (koff + h, tj[t])),
                  kv_spec(voff), q_spec(0), row_spec, row_spec],
        out_specs=[pl.BlockSpec((1, nq, LANES, bq), lambda h, t, ti, tj: (h, 0, 0, 0)),
                   pl.BlockSpec((bk, LANES), lambda h, t, ti, tj: (tj[t], h)),
                   pl.BlockSpec((bk, LANES), lambda h, t, ti, tj: (tj[t], h))])
    dqt, dk, dv = pl.pallas_call(
        body, name=name, grid_spec=grid_spec,
        out_shape=[jax.ShapeDtypeStruct((heads, nq, LANES, bq), F32), jax.ShapeDtypeStruct((lk, heads * LANES), F32),
                   jax.ShapeDtypeStruct((lk, heads * LANES), F32)],
        compiler_params=_params(),
    )(tab_i, tab_j, qa, ka, kta, va, doa, lsea, deltaa)
    return dqt.transpose(1, 3, 0, 2).reshape(lq, heads * LANES), dk, dv


def _scan(name, bre, bim, are, aim, *, reverse, chunk=512, strip=512):
    l, n = bre.shape
    t = min(chunk, l)
    nc, ns = l // t, n // strip
    steps = int(math.log2(t))
    assert 1 << steps == t

    def cmap(w, c):
        return ((nc - 1 - c) if reverse else c, w)

    def body(bre_ref, bim_ref, are_ref, aim_ref, sre_ref, sim_ref, pre_s, pim_s, cre_s, cim_s):
        c = pl.program_id(1)
        ar, ai = are_ref[...], aim_ref[...]
        row = _row_ids(t)

        def chunk_scan(xr, xi):
            pr, pi = ar, ai
            for k in range(steps):
                d = 1 << k
                if reverse:
                    live = (row < t - d).astype(F32)
                    shr, shi = pltpu.roll(xr, t - d, 0) * live, pltpu.roll(xi, t - d, 0) * live
                else:
                    live = (row >= d).astype(F32)
                    shr, shi = pltpu.roll(xr, d, 0) * live, pltpu.roll(xi, d, 0) * live
                xr, xi = xr + pr * shr - pi * shi, xi + pr * shi + pi * shr
                pr, pi = pr * pr - pi * pi, 2.0 * pr * pi
            return xr, xi

        @pl.when(c == 0)
        def _():
            e = (row == (t - 1 if reverse else 0)).astype(F32)
            tr, ti = chunk_scan(e * ar, e * ai)
            pre_s[...] = tr
            pim_s[...] = ti
            cre_s[...] = jnp.zeros_like(cre_s)
            cim_s[...] = jnp.zeros_like(cim_s)

        xr, xi = chunk_scan(bre_ref[...], bim_ref[...])
        cr, ci = cre_s[...], cim_s[...]
        p_r, p_i = pre_s[...], pim_s[...]
        sre_ref[...] = xr + p_r * cr - p_i * ci
        sim_ref[...] = xi + p_r * ci + p_i * cr
        last = 0 if reverse else t - 1
        cre_s[...] = sre_ref[last:last + 1, :]
        cim_s[...] = sim_ref[last:last + 1, :]

    blk = pl.BlockSpec((t, strip), cmap)
    a_blk = pl.BlockSpec((1, strip), lambda w, c: (0, w))
    return pl.pallas_call(
        body, name=name, grid=(ns, nc), in_specs=[blk, blk, a_blk, a_blk], out_specs=[blk, blk],
        out_shape=[jax.ShapeDtypeStruct((l, n), F32), jax.ShapeDtypeStruct((l, n), F32)],
        scratch_shapes=[pltpu.VMEM((t, strip), F32), pltpu.VMEM((t, strip), F32),
                        pltpu.VMEM((1, strip), F32), pltpu.VMEM((1, strip), F32)],
        compiler_params=_params(),
    )(bre, bim, are, aim)


def _disc_math(lr, li, ldt, br, bi):
    dt = jnp.exp(ldt)
    mag = jnp.exp(lr * dt)
    a_re, a_im = mag * jnp.cos(li * dt), mag * jnp.sin(li * dt)
    den = lr * lr + li * li
    e_re, e_im = a_re - 1.0, a_im
    f_re = (e_re * lr + e_im * li) / den
    f_im = (e_im * lr - e_re * li) / den
    return a_re, a_im, f_re * br - f_im * bi, f_re * bi + f_im * br


def _disc_fwd(lr, li, ldt, br, bi):
    def body(lr_ref, li_ref, ldt_ref, br_ref, bi_ref, are_ref, aim_ref, bbr_ref, bbi_ref):
        a_re, a_im, bb_re, bb_im = _disc_math(lr_ref[...], li_ref[...], ldt_ref[...], br_ref[...], bi_ref[...])
        are_ref[...] = a_re
        aim_ref[...] = a_im
        bbr_ref[...] = bb_re
        bbi_ref[...] = bb_im

    col = jax.ShapeDtypeStruct(lr.shape, F32)
    mat = jax.ShapeDtypeStruct(br.shape, F32)
    return pl.pallas_call(body, name="s5_disc_fwd", out_shape=[col, col, mat, mat], compiler_params=_params())(
        lr, li, ldt, br, bi)


def _disc_bwd(lr, li, ldt, br, bi, da_re, da_im, dbb_re, dbb_im):
    def body(lr_ref, li_ref, ldt_ref, br_ref, bi_ref, g0, g1, g2, g3, o0, o1, o2, o3, o4):
        _, vjp = jax.vjp(_disc_math, lr_ref[...], li_ref[...], ldt_ref[...], br_ref[...], bi_ref[...])
        grads = vjp((g0[...], g1[...], g2[...], g3[...]))
        for o, g in zip((o0, o1, o2, o3, o4), grads):
            o[...] = g

    col = jax.ShapeDtypeStruct(lr.shape, F32)
    mat = jax.ShapeDtypeStruct(br.shape, F32)
    return pl.pallas_call(body, name="s5_disc_bwd", out_shape=[col, col, col, mat, mat], compiler_params=_params())(
        lr, li, ldt, br, bi, da_re, da_im, dbb_re, dbb_im)


N_CHIP = 4


def _place():
    x, y, c = lax.axis_index("x"), lax.axis_index("y"), lax.axis_index("c")
    return (x, y, c), (x, y, 1 - c), [(1 - x, y), (x, 1 - y), (1 - x, 1 - y)]


def _lin(px, py, pc):
    return 4 * px + 2 * py + pc


def _remote(src, dst, sems, k, dev):
    return pltpu.make_async_remote_copy(src_ref=src, dst_ref=dst, send_sem=sems[0].at[k], recv_sem=sems[1].at[k],
                                        device_id=dev, device_id_type=pl.DeviceIdType.MESH)


def _hbm_call(body, name, srcs, out_shape, nsem):
    hbm = pl.BlockSpec(memory_space=pltpu.HBM)
    return pl.pallas_call(
        body, name=name, in_specs=[hbm] * len(srcs), out_specs=hbm, out_shape=out_shape,
        scratch_shapes=[pltpu.SemaphoreType.DMA((nsem,)), pltpu.SemaphoreType.DMA((nsem,)), pltpu.SemaphoreType.DMA],
        compiler_params=pltpu.CompilerParams(has_side_effects=True),
    )(*srcs)


def _gather_all(name, src):
    def body(s_ref, o_ref, send_sems, recv_sems, loc_sem):
        me, sib, chips = _place()
        c = me[2]
        sems = (send_sems, recv_sems)
        slot = lambda dev: o_ref.at[_lin(*dev)]
        mine = pltpu.make_async_copy(s_ref, slot(me), loc_sem)
        mine.start()
        first = [_remote(s_ref, slot(me), sems, 0, sib)]
        first += [_remote(s_ref, slot(me), sems, 1 + j, (*chip, c)) for j, chip in enumerate(chips)]
        for cp in first:
            cp.start()
        passed = [_remote(slot((*chip, c)), slot((*chip, c)), sems, 4 + j, sib) for j, chip in enumerate(chips)]
        for j, chip in enumerate(chips):
            _remote(s_ref, slot((*chip, c)), sems, 1 + j, me).wait_recv()
            passed[j].start()
        _remote(s_ref, slot(sib), sems, 0, me).wait_recv()
        for j, chip in enumerate(chips):
            _remote(s_ref, slot((*chip, 1 - c)), sems, 4 + j, me).wait_recv()
        for cp in first + passed:
            cp.wait_send()
        mine.wait()

    return _hbm_call(body, name, [src], jax.ShapeDtypeStruct((N_DEV,) + src.shape, src.dtype), 7)


def _pair_exchange(name, send):
    def body(s_ref, o_ref, send_sems, recv_sems, loc_sem):
        me, sib, _ = _place()
        c = me[2]
        sems = (send_sems, recv_sems)
        copies = [_remote(s_ref.at[2 * q + (1 - c)], o_ref.at[q], sems, q, sib) for q in range(N_CHIP)]
        for cp in copies:
            cp.start()
        for cp in copies:
            cp.wait()

    return _hbm_call(body, name, [send], jax.ShapeDtypeStruct((N_CHIP,) + send.shape[1:], send.dtype), N_CHIP)


def _pair_sum(send, got, bm=256):
    _, r, cols = send.shape
    core = lax.axis_index("c").astype(jnp.int32).reshape(1)

    def body(core_ref, a_ref, b_ref, o_ref):
        o_ref[...] = a_ref[...] + b_ref[...]

    grid_spec = pltpu.PrefetchScalarGridSpec(
        num_scalar_prefetch=1, grid=(N_CHIP, r // bm),
        in_specs=[pl.BlockSpec((1, bm, cols), lambda q, i, cr: (2 * q + cr[0], i, 0)),
                  pl.BlockSpec((1, bm, cols), lambda q, i, cr: (q, i, 0))],
        out_specs=pl.BlockSpec((1, bm, cols), lambda q, i, cr: (q, i, 0)))
    return pl.pallas_call(body, name="grad_pair_sum", grid_spec=grid_spec,
                          out_shape=jax.ShapeDtypeStruct((N_CHIP, r, cols), F32), compiler_params=_params())(
        core, send, got)


def _chip_exchange(name, part):
    def body(p_ref, o_ref, send_sems, recv_sems, loc_sem):
        me, _, chips = _place()
        c = me[2]
        sems = (send_sems, recv_sems)
        chip_id = lambda chip: 2 * chip[0] + chip[1]
        my_chip = chip_id(me)
        mine = pltpu.make_async_copy(p_ref.at[my_chip], o_ref.at[my_chip], loc_sem)
        mine.start()
        copies = [_remote(p_ref.at[chip_id(chip)], o_ref.at[my_chip], sems, j, (*chip, c)) for j, chip in enumerate(chips)]
        for cp in copies:
            cp.start()
        for j, chip in enumerate(chips):
            _remote(p_ref.at[my_chip], o_ref.at[chip_id(chip)], sems, j, me).wait_recv()
        for cp in copies:
            cp.wait_send()
        mine.wait()

    return _hbm_call(body, name, [part], jax.ShapeDtypeStruct(part.shape, part.dtype), 3)


def _adamw(rcv, w, m, v, bm=256):
    r, c = w.shape
    nslot = rcv.shape[0]

    def body(rcv_ref, w_ref, m_ref, v_ref, g_ref, d_ref, m2_ref, v2_ref):
        g = rcv_ref[0]
        for s in range(1, nslot):
            g = g + rcv_ref[s]
        m2 = ADAM_B1 * m_ref[...] + (1.0 - ADAM_B1) * g
        v2 = ADAM_B2 * v_ref[...] + (1.0 - ADAM_B2) * (g * g)
        m_hat = m2 / (1.0 - ADAM_B1 ** ADAM_STEP)
        v_hat = v2 / (1.0 - ADAM_B2 ** ADAM_STEP)
        g_ref[...] = g
        d_ref[...] = -ADAM_LR * (m_hat / (jnp.sqrt(v_hat) + ADAM_EPS) + ADAM_WD * w_ref[...])
        m2_ref[...] = m2
        v2_ref[...] = v2

    blk = pl.BlockSpec((bm, c), lambda i: (i, 0))
    out = jax.ShapeDtypeStruct((r, c), F32)
    return pl.pallas_call(
        body, name="adamw", grid=(r // bm,),
        in_specs=[pl.BlockSpec((nslot, bm, c), lambda i: (0, i, 0)), blk, blk, blk], out_specs=[blk] * 4,
        out_shape=[out] * 4, compiler_params=_params(),
    )(rcv, w, m, v)


PACK_W = 1024


def _pad_rows(flat, mult):
    n = flat.shape[-1]
    per = PACK_W * mult
    tot = -(-n // per) * per
    flat = jnp.pad(flat, [(0, 0)] * (flat.ndim - 1) + [(0, tot - n)])
    return flat.reshape(flat.shape[:-1] + (tot // PACK_W, PACK_W))


def _to_shards(name_axis, full):
    ax = name_axis
    two, a, b = full.shape
    if ax == 0:
        t = full.reshape(two, N_DEV, a // N_DEV, b).transpose(1, 0, 2, 3)
    else:
        t = full.reshape(two, a, N_DEV, b // N_DEV).transpose(2, 0, 1, 3)
    return t.reshape(N_DEV, -1)


def _from_shards(ax, blocks, local_shape):
    two, a, b = local_shape
    t = blocks.reshape(N_DEV, two, a, b)
    if ax == 0:
        return t.transpose(1, 0, 2, 3).reshape(two, N_DEV * a, b)
    return t.transpose(1, 2, 0, 3).reshape(two, a, N_DEV * b)


def _block_diag_in(bb):
    b3 = bb.reshape(SSM_GROUPS, SSM_STATE, SSM_GROUP_CH).transpose(0, 2, 1)
    eye = jnp.eye(SSM_GROUPS, dtype=bb.dtype)
    return (b3[:, :, None, :] * eye[:, None, :, None]).reshape(SSM_WIDTH, SSM_N)


def _block_diag_out(cc):
    c3 = cc.transpose(0, 2, 1)
    eye = jnp.eye(SSM_GROUPS, dtype=cc.dtype)
    return (c3[:, :, None, :] * eye[:, None, :, None]).reshape(SSM_N, SSM_WIDTH)


def _diag_blocks(mat, rows_per, cols_per):
    m4 = mat.reshape(SSM_GROUPS, rows_per, SSM_GROUPS, cols_per)
    eye = jnp.eye(SSM_GROUPS, dtype=mat.dtype)
    return jnp.sum(m4 * eye[:, None, :, None], axis=2)


def _layer_weights(wf, li):
    w_in = wf["w_in"][li]
    z = lambda r, c: jnp.zeros((r, c), BF16)
    kr = w_in[:, Q_LORA + KV_LORA:Q_LORA + KV_LORA + D_ROPE]
    o = {}
    o["wqkv"] = jnp.concatenate([w_in[:, :Q_LORA + KV_LORA], z(D_MODEL, KR_LO), kr,
                                 z(D_MODEL, LANES - KR_LO - D_ROPE)], axis=1)
    base = Q_LORA + KV_LORA + D_ROPE
    o["w_u"] = w_in[:, base:base + SSM_WIDTH]
    o["w_xq"] = w_in[:, base + SSM_WIDTH:base + SSM_WIDTH + X_WIDTH]
    o["w_g"] = w_in[:, base + SSM_WIDTH + X_WIDTH:]
    wqb = wf["w_q_b"][li].reshape(Q_LORA, MLA_HEADS, D_QK)
    o["wqb"] = jnp.pad(wqb, ((0, 0), (0, 0), (0, LANES - D_QK))).reshape(Q_LORA, MLA_HEADS * LANES)
    wkv = wf["w_kv_b"][li].reshape(KV_LORA, MLA_HEADS, D_NOPE + D_V)
    o["wk"] = jnp.pad(wkv[:, :, :D_NOPE], ((0, 0), (0, 0), (0, LANES - D_NOPE))).reshape(KV_LORA, MLA_HEADS * LANES)
    o["wv"] = jnp.pad(wkv[:, :, D_NOPE:], ((0, 0), (0, 0), (0, LANES - D_V))).reshape(KV_LORA, MLA_HEADS * LANES)
    o["wv_t"] = o["wv"].T
    wo = wf["w_o_mla"][li].reshape(MLA_HEADS, D_V, D_MODEL)
    o["wo_mla"] = jnp.pad(wo, ((0, 0), (0, LANES - D_V), (0, 0))).reshape(MLA_HEADS * LANES, D_MODEL)
    for n in ("w_glu", "w_o_ssm", "w_mem_kv", "w_o_cross", "w_out", "w_up", "w_down"):
        o[n] = wf[n][li]
    o["w_up_g"], o["w_up_v"] = o["w_up"][:, :D_FF], o["w_up"][:, D_FF:]
    return o


def _row(v):
    return v.reshape(1, -1).astype(F32)


def _pad_lanes(v, n=LANES):
    return jnp.pad(v, (0, n - v.shape[0])).reshape(1, n).astype(F32)


def _layer_fwd(x, mem, tabs, w, p):
    l = x.shape[0]
    rope_c, rope_sa, rope_sb = tabs
    s = {"x": x}
    g_mix, g_qa, g_kva = _row(p["norm_mix_g"]), _row(p["q_a_norm_g"]), _row(p["kv_a_norm_g"])
    g_q, g_k = _pad_lanes(p["q_norm_g"]), _pad_lanes(p["k_norm_g"])

    (h,) = _rowwise("rms_mix", lambda i, r, c: (_rms_f(r[0][...], c[0][...], D_MODEL),), l, 512,
                    [(x, D_MODEL, 0, "row")], [g_mix], [(D_MODEL, BF16)])
    pqkv = _mm("proj_qkv", [(h, w["wqkv"])])
    u = _mm("proj_u", [(h, w["w_u"])])
    xq = _mm("proj_xq", [(h, w["w_xq"])])
    gl = _mm("proj_gate", [(h, w["w_g"])], bm=1024, bn_cap=1024)
    s.update(h=h, pqkv=pqkv, u=u, xq=xq, gl=gl)

    def prep_a(i, r, c):
        return (_rms_f(r[0][:, :Q_LORA], c[0][...], Q_LORA),
                _rms_f(r[0][:, Q_LORA:Q_LORA + KV_LORA], c[1][...], KV_LORA))

    nq, nkv = _rowwise("mla_prep_a", prep_a, l, 512, [(pqkv, QKV_W, 0, "row")], [g_qa, g_kva],
                       [(Q_LORA, BF16), (KV_LORA, BF16)])
    q_raw = _mm("mla_q_b", [(nq, w["wqb"])], bn_cap=1024)
    k_raw = _mm("mla_k_b", [(nkv, w["wk"])], bn_cap=1024)
    v_mla = _mm("mla_v_b", [(nkv, w["wv"])], out_dtype=BF16, bn_cap=1024)
    vt_mla = _mm("mla_vt_b", [(w["wv_t"], nkv)], trans_b=True, out_dtype=BF16, bn_cap=1024)

    def prep_b(i, r, c):
        q_ref, k_ref, kr_ref, c_ref, sa_ref, sb_ref = r
        rc, sa, sb, kr = c_ref[...], sa_ref[...], sb_ref[...], kr_ref[...]
        qs, ks = [], []
        for hd in range(MLA_HEADS):
            cols = slice(hd * LANES, (hd + 1) * LANES)
            qs.append(_rope_f(_rms_f(q_ref[:, cols], c[0][...], D_QK), rc, sa, sb))
            ks.append(_rope_f(_rms_f(k_ref[:, cols] + kr, c[1][...], D_QK), rc, sa, sb))
        return jnp.concatenate(qs, axis=1), jnp.concatenate(ks, axis=1)

    hw = MLA_HEADS * LANES
    kr_blk = (Q_LORA + KV_LORA) // LANES
    tab_ins = [(rope_c, LANES, 0, "row"), (rope_sa, LANES, 0, "row"), (rope_sb, LANES, 0, "row")]
    q, k = _rowwise("mla_prep_b", prep_b, l, 256,
                    [(q_raw, hw, 0, "row"), (k_raw, hw, 0, "row"), (pqkv, LANES, kr_blk, "row")] + tab_ins,
                    [g_q, g_k], [(hw, BF16), (hw, BF16)])
    o_a, lse_a = _attn_fwd("mla_attn_fwd", q, k, vt_mla, qoff=0, koff=0, voff=0, heads=MLA_HEADS, causal=True,
                           scale=D_QK ** -0.5, bq=512, bk=512)
    ya = _mm("mla_o", [(o_a, w["wo_mla"])], bn_cap=1024)
    s.update(nq=nq, nkv=nkv, q_raw=q_raw, k_raw=k_raw, v_mla=v_mla, q=q, k=k, o_a=o_a, lse_a=lse_a, ya=ya)

    lr = p["ssm_lambda_re"].reshape(SSM_N, 1)
    li = p["ssm_lambda_im"].reshape(SSM_N, 1)
    ldt = jnp.repeat(p["ssm_log_dt"], SSM_STATE).reshape(SSM_N, 1)
    br = p["ssm_b_re"].reshape(SSM_N, SSM_GROUP_CH)
    bi = p["ssm_b_im"].reshape(SSM_N, SSM_GROUP_CH)
    a_re, a_im, bb_re, bb_im = _disc_fwd(lr, li, ldt, br, bi)
    bb_re_d, bb_im_d = _block_diag_in(bb_re).astype(BF16), _block_diag_in(bb_im).astype(BF16)
    cc_re_d = _block_diag_out(p["ssm_c_re"]).astype(BF16)
    cc_imn_d = _block_diag_out(-p["ssm_c_im"]).astype(BF16)
    a_re_row, a_im_row = a_re.reshape(1, SSM_N), a_im.reshape(1, SSM_N)
    d_row = _row(p["ssm_d"])
    b_glu = _row(p["b_glu"])
    bu_re = _mm("s5_bu_re", [(u, bb_re_d)], bn_cap=1024)
    bu_im = _mm("s5_bu_im", [(u, bb_im_d)], bn_cap=1024)
    s_re, s_im = _scan("s5_scan_fwd", bu_re, bu_im, a_re_row, a_im_row, reverse=False)
    ypre = _mm("s5_c", [(s_re, cc_re_d), (s_im, cc_imn_d)])

    def ssm_y(i, r, c):
        return (_gelu(r[0][...] + c[0][...] * r[1][...]),)

    (y_b,) = _rowwise("s5_gelu", ssm_y, l, 512, [(ypre, SSM_WIDTH, 0, "row"), (u, SSM_WIDTH, 0, "row")], [d_row],
                      [(SSM_WIDTH, BF16)])
    z = _mm("s5_glu", [(y_b, w["w_glu"])])

    def ssm_out(i, r, c):
        y = _gelu(r[0][...] + c[0][...] * r[1][...])
        return (y * jax.nn.sigmoid(r[2][...] + c[1][...]),)

    (out_b,) = _rowwise("s5_glu_out", ssm_out, l, 512,
                        [(ypre, SSM_WIDTH, 0, "row"), (u, SSM_WIDTH, 0, "row"), (z, SSM_WIDTH, 0, "row")],
                        [d_row, b_glu], [(SSM_WIDTH, BF16)])
    yb = _mm("s5_o", [(out_b, w["w_o_ssm"])], bn_cap=1024)
    s.update(disc=(lr, li, ldt, br, bi), a_rows=(a_re_row, a_im_row), bb_d=(bb_re_d, bb_im_d),
             cc_d=(cc_re_d, cc_imn_d), s_re=s_re, s_im=s_im, ypre=ypre, y_b=y_b, z=z, out_b=out_b, yb=yb)

    g_mem, g_xq, g_xk = _row(p["mem_norm_g"]), _row(p["xq_norm_g"]), _row(p["xk_norm_g"])
    ml = mem.shape[0]
    (memn,) = _rowwise("rms_mem", lambda i, r, c: (_rms_f(r[0][...], c[0][...], D_MODEL),), ml, 256,
                       [(mem, D_MODEL, 0, "row")], [g_mem], [(D_MODEL, BF16)])
    kvm = _mm("cross_kv", [(memn, w["w_mem_kv"])], bn_cap=1024)

    def headnorm(i, r, c):
        return (jnp.concatenate([_rms_f(r[0][:, hd * LANES:(hd + 1) * LANES], c[0][...], X_HEAD_DIM)
                                 for hd in range(X_HEADS)], axis=1),)

    (xk,) = _rowwise("cross_k_norm", headnorm, ml, 256, [(kvm, X_WIDTH, 0, "row")], [g_xk], [(X_WIDTH, BF16)])
    (xqn,) = _rowwise("cross_q_norm", headnorm, l, 512, [(xq, X_WIDTH, 0, "row")], [g_xq], [(X_WIDTH, BF16)])
    xvt = kvm[:, X_WIDTH:].T.astype(BF16)
    o_c, lse_c = _attn_fwd("cross_attn_fwd", xqn, xk, xvt, qoff=0, koff=0, voff=0, heads=X_HEADS,
                           causal=False, scale=X_HEAD_DIM ** -0.5, bq=512, bk=256)
    yc = _mm("cross_o", [(o_c, w["w_o_cross"])], bn_cap=1024)
    s.update(memn=memn, kvm=kvm, xk=xk, xqn=xqn, o_c=o_c, lse_c=lse_c, yc=yc)

    b_gate = _row(p["b_gate"])

    def merge(i, r, c):
        acc = None
        for br_ in range(3):
            g = jax.nn.sigmoid(r[br_][...] + c[0][:, br_ * D_MODEL:(br_ + 1) * D_MODEL])
            t = g * r[3 + br_][...]
            acc = t if acc is None else acc + t
        return (acc,)

    gate_ins = [(gl, D_MODEL, b_, "row") for b_ in range(3)]
    (merged,) = _rowwise("merge", merge, l, 256,
                         gate_ins + [(ya, D_MODEL, 0, "row"), (yb, D_MODEL, 0, "row"), (yc, D_MODEL, 0, "row")],
                         [b_gate], [(D_MODEL, BF16)])
    x1 = _mm("mix_out", [(merged, w["w_out"])], add=x, bn_cap=1024)
    s.update(merged=merged, x1=x1)

    g_ffn = _row(p["norm_ffn_g"])
    (h2,) = _rowwise("rms_ffn", lambda i, r, c: (_rms_f(r[0][...], c[0][...], D_MODEL),), l, 512,
                     [(x1, D_MODEL, 0, "row")], [g_ffn], [(D_MODEL, BF16)])
    up = _mm("ffn_up", [(h2, w["w_up"])], bm=1024, bn_cap=1408)
    conv_w = p["conv_w_full"].astype(F32)
    conv_b = _row(p["conv_b"])

    def conv_glu(i, r, c):
        cg = _conv(r[0], r[2], i, c[0], c[1], 0)
        cv = _conv(r[1], r[3], i, c[0], c[1], D_FF)
        return (cg * jax.nn.sigmoid(cg) * cv,)

    up_ins = [(up, D_FF, 0, "row"), (up, D_FF, 1, "row"), (up, D_FF, 0, "prev"), (up, D_FF, 1, "prev")]
    (act,) = _rowwise("ffn_conv_glu", conv_glu, l, 256, up_ins, [conv_w, conv_b], [(D_FF, BF16)])
    x2 = _mm("ffn_down", [(act, w["w_down"])], add=x1, bm=1024, bn_cap=1024)
    s.update(h2=h2, up=up, act=act, conv_w=conv_w, conv_b=conv_b)
    return x2, s


def _conv(x_ref, halo_ref, i, w_ref, b_ref, col0):
    x = x_ref[...]
    cols = slice(col0, col0 + D_FF)
    return (w_ref[0:1, cols] * _shift_down(x, halo_ref, i, 2) + w_ref[1:2, cols] * _shift_down(x, halo_ref, i, 1)
            + w_ref[2:3, cols] * x + b_ref[:, cols])


def _layer_bwd(dx2, s, mem, tabs, w, p):
    l = dx2.shape[0]
    rope_c, rope_sa, rope_sb = tabs
    x, x1 = s["x"], s["x1"]
    g = {}

    dact = _mm("ffn_down_dx", [(dx2, w["w_down"])], trans_b=True, bn_cap=1408)
    g["w_down"] = _mm_tn("ffn_down_dw", s["act"], dx2)
    up = s["up"]
    nblk_c = l // min(256, l)

    def conv_bwd_a(i, r, c):
        outs, accs = [], []
        for half in range(2):
            x_ref, halo_ref = r[1 + half], r[3 + half]
            col0 = half * D_FF
            cols = slice(col0, col0 + D_FF)
            xv = x_ref[...]
            xm1, xm2 = _shift_down(xv, halo_ref, i, 1), _shift_down(xv, halo_ref, i, 2)
            cv = c[0][0:1, cols] * xm2 + c[0][1:2, cols] * xm1 + c[0][2:3, cols] * xv + c[1][:, cols]
            outs.append((cv, xv, xm1, xm2))
        (cg, xg, xg1, xg2), (cvv, xv, xv1, xv2) = outs
        sig = jax.nn.sigmoid(cg)
        da = r[0][...]
        dcv = da * (cg * sig)
        dcg = da * cvv * (sig * (1.0 + cg * (1.0 - sig)))
        for d, (x0, xa, xb) in ((dcg, (xg, xg1, xg2)), (dcv, (xv, xv1, xv2))):
            accs += [jnp.sum(d, axis=0, keepdims=True), jnp.sum(d * xb, axis=0, keepdims=True),
                     jnp.sum(d * xa, axis=0, keepdims=True), jnp.sum(d * x0, axis=0, keepdims=True)]
        return (dcg, dcv, *accs)

    up_ins = [(up, D_FF, 0, "row"), (up, D_FF, 1, "row"), (up, D_FF, 0, "prev"), (up, D_FF, 1, "prev")]
    res = _rowwise("ffn_conv_glu_bwd", conv_bwd_a, l, 256, [(dact, D_FF, 0, "row")] + up_ins,
                   [s["conv_w"], s["conv_b"]], [(D_FF, F32), (D_FF, F32)], [(1, D_FF)] * 8)
    dcg, dcv = res[0], res[1]
    db_g, dw0_g, dw1_g, dw2_g, db_v, dw0_v, dw1_v, dw2_v = res[2:]
    g["conv_b"] = jnp.concatenate([db_g, db_v], axis=1)[0]
    g["conv_w"] = jnp.concatenate([jnp.concatenate([dw0_g, dw0_v], axis=1), jnp.concatenate([dw1_g, dw1_v], axis=1),
                                   jnp.concatenate([dw2_g, dw2_v], axis=1)], axis=0)

    def conv_bwd_b(i, r, c):
        outs = []
        for half in range(2):
            d = r[half][...]
            cols = slice(half * D_FF, (half + 1) * D_FF)
            outs.append(c[0][2:3, cols] * d + c[0][1:2, cols] * _shift_up(d, r[2 + half], i, nblk_c, 1)
                        + c[0][0:1, cols] * _shift_up(d, r[2 + half], i, nblk_c, 2))
        return tuple(outs)

    dup_g, dup_v = _rowwise("ffn_conv_bwd_x", conv_bwd_b, l, 256,
                            [(dcg, D_FF, 0, "row"), (dcv, D_FF, 0, "row"), (dcg, D_FF, 0, "next"),
                             (dcv, D_FF, 0, "next")], [s["conv_w"]], [(D_FF, BF16), (D_FF, BF16)])
    dh2 = _mm("ffn_up_dx", [(dup_g, w["w_up_g"]), (dup_v, w["w_up_v"])], trans_b=True, bn_cap=1024)
    g["w_up"] = jnp.concatenate([_mm_tn("ffn_up_dw_g", s["h2"], dup_g), _mm_tn("ffn_up_dw_v", s["h2"], dup_v)], axis=1)

    def rms_bwd_res(i, r, c):
        dx, dg = _rms_b(r[0][...], c[0][...], r[1][...], D_MODEL)
        return (dx + r[2][...], dg)

    dx1, g["norm_ffn_g"] = _rowwise(
        "rms_ffn_bwd", rms_bwd_res, l, 512, [(x1, D_MODEL, 0, "row"), (dh2, D_MODEL, 0, "row"), (dx2, D_MODEL, 0, "row")],
        [_row(p["norm_ffn_g"])], [(D_MODEL, F32)], [(1, D_MODEL)])

    dmerged = _mm("mix_out_dx", [(dx1, w["w_out"])], trans_b=True, bn_cap=1024)
    g["w_out"] = _mm_tn("mix_out_dw", s["merged"], dx1)
    gl, ya, yb, yc = s["gl"], s["ya"], s["yb"], s["yc"]

    def merge_bwd(i, r, c):
        dm = r[0][...]
        dys, dgs = [], []
        for b_ in range(3):
            gate = jax.nn.sigmoid(r[1 + b_][...] + c[0][:, b_ * D_MODEL:(b_ + 1) * D_MODEL])
            dys.append(dm * gate)
            dgs.append(dm * r[4 + b_][...] * (gate * (1.0 - gate)))
        dgl = jnp.concatenate(dgs, axis=1)
        return (*dys, dgl, jnp.sum(dgl, axis=0, keepdims=True))

    gate_ins = [(gl, D_MODEL, b_, "row") for b_ in range(3)]
    dya, dyb, dyc, dgl, db_gate = _rowwise(
        "merge_bwd", merge_bwd, l, 256,
        [(dmerged, D_MODEL, 0, "row")] + gate_ins + [(ya, D_MODEL, 0, "row"), (yb, D_MODEL, 0, "row"),
                                                     (yc, D_MODEL, 0, "row")],
        [_row(p["b_gate"])], [(D_MODEL, BF16)] * 3 + [(3 * D_MODEL, BF16)], [(1, 3 * D_MODEL)])
    g["b_gate"] = db_gate[0]

    do_c = _mm("cross_o_dx", [(dyc, w["w_o_cross"])], trans_b=True, out_dtype=BF16)
    g["w_o_cross"] = _mm_tn("cross_o_dw", s["o_c"], dyc)
    kvm = s["kvm"]
    delta_c = _attn_delta("cross_attn_delta", s["o_c"], do_c, heads=X_HEADS, bq=512)
    dxqn, dxk, dxv = _attn_bwd("cross_attn_bwd", s["xqn"], s["xk"], s["xk"].T, kvm, do_c, s["lse_c"], delta_c, qoff=0,
                               koff=0, voff=X_HEADS, heads=X_HEADS, causal=False, scale=X_HEAD_DIM ** -0.5, bq=512,
                               bk=256)
    ml = mem.shape[0]

    def headnorm_bwd(i, r, c):
        dxs, dg = [], None
        for hd in range(X_HEADS):
            cols = slice(hd * LANES, (hd + 1) * LANES)
            dx_h, dg_h = _rms_b(r[0][:, cols], c[0][...], r[1][:, cols], X_HEAD_DIM)
            dxs.append(dx_h)
            dg = dg_h if dg is None else dg + dg_h
        return (jnp.concatenate(dxs, axis=1), dg)

    dxq, dg_xq = _rowwise("cross_q_norm_bwd", headnorm_bwd, l, 512,
                          [(s["xq"], X_WIDTH, 0, "row"), (dxqn, X_WIDTH, 0, "row")], [_row(p["xq_norm_g"])],
                          [(X_WIDTH, BF16)], [(1, X_HEAD_DIM)])
    dkvm_k, dg_xk = _rowwise("cross_k_norm_bwd", headnorm_bwd, ml, 256,
                             [(kvm, X_WIDTH, 0, "row"), (dxk, X_WIDTH, 0, "row")], [_row(p["xk_norm_g"])],
                             [(X_WIDTH, F32)], [(1, X_HEAD_DIM)])
    g["xq_norm_g"], g["xk_norm_g"] = dg_xq[0], dg_xk[0]
    dkvm = jnp.concatenate([dkvm_k, dxv], axis=1)
    g["w_mem_kv"] = _mm_tn("cross_kv_dw", s["memn"], dkvm)
    dmemn = _mm("cross_kv_dx", [(dkvm, w["w_mem_kv"])], trans_b=True, bn_cap=1024)

    def rms_bwd_gain_only(i, r, c):
        return (_rms_b(r[0][...], c[0][...], r[1][...], D_MODEL)[1],)

    (dg_mem,) = _rowwise("rms_mem_bwd", rms_bwd_gain_only, ml, 256,
                         [(mem, D_MODEL, 0, "row"), (dmemn, D_MODEL, 0, "row")], [_row(p["mem_norm_g"])], [],
                         [(1, D_MODEL)])
    g["mem_norm_g"] = dg_mem[0]

    dout_b = _mm("s5_o_dx", [(dyb, w["w_o_ssm"])], trans_b=True)
    g["w_o_ssm"] = _mm_tn("s5_o_dw", s["out_b"], dyb)
    ypre, u, z = s["ypre"], s["u"], s["z"]
    d_row, b_glu = _row(p["ssm_d"]), _row(p["b_glu"])
    yuz = [(ypre, SSM_WIDTH, 0, "row"), (u, SSM_WIDTH, 0, "row"), (z, SSM_WIDTH, 0, "row")]

    def glu_bwd_z(i, r, c):
        y = _gelu(r[1][...] + c[0][...] * r[2][...])
        sg = jax.nn.sigmoid(r[3][...] + c[1][...])
        dz = r[0][...] * y * (sg * (1.0 - sg))
        return (dz, jnp.sum(dz, axis=0, keepdims=True))

    dz, db_glu = _rowwise("s5_glu_bwd_z", glu_bwd_z, l, 512, [(dout_b, SSM_WIDTH, 0, "row")] + yuz, [d_row, b_glu],
                          [(SSM_WIDTH, BF16)], [(1, SSM_WIDTH)])
    g["b_glu"] = db_glu[0]
    g["w_glu"] = _mm_tn("s5_glu_dw", s["y_b"], dz)
    dy2 = _mm("s5_glu_dx", [(dz, w["w_glu"])], trans_b=True)

    def gelu_bwd(i, r, c):
        t = r[2][...] + c[0][...] * r[3][...]
        sg = jax.nn.sigmoid(r[4][...] + c[1][...])
        dt = (r[0][...] * sg + r[1][...]) * _gelu_grad(t)
        return (dt, c[0][...] * dt, jnp.sum(dt * r[3][...], axis=0, keepdims=True))

    dypre, du_skip, dd = _rowwise(
        "s5_gelu_bwd", gelu_bwd, l, 512, [(dout_b, SSM_WIDTH, 0, "row"), (dy2, SSM_WIDTH, 0, "row")] + yuz,
        [d_row, b_glu], [(SSM_WIDTH, BF16), (SSM_WIDTH, F32)], [(1, SSM_WIDTH)])
    g["ssm_d"] = dd.reshape(SSM_GROUPS, SSM_GROUP_CH)
    cc_re_d, cc_imn_d = s["cc_d"]
    bb_re_d, bb_im_d = s["bb_d"]
    a_re_row, a_im_row = s["a_rows"]
    s_re, s_im = s["s_re"], s["s_im"]
    g_re = _mm("s5_c_dx_re", [(dypre, cc_re_d)], trans_b=True, bn_cap=1024)
    g_im = _mm("s5_c_dx_im", [(dypre, cc_imn_d)], trans_b=True, bn_cap=1024)
    dcc_re = _mm_tn("s5_c_dw_re", s_re, dypre)
    dcc_imn = _mm_tn("s5_c_dw_im", s_im, dypre)
    g["ssm_c_re"] = _diag_blocks(dcc_re, SSM_STATE, SSM_GROUP_CH).transpose(0, 2, 1)
    g["ssm_c_im"] = -_diag_blocks(dcc_imn, SSM_STATE, SSM_GROUP_CH).transpose(0, 2, 1)
    lam_re, lam_im = _scan("s5_scan_bwd", g_re, g_im, a_re_row, -a_im_row, reverse=True)

    def da_sum(i, r, c):
        lr_, li_ = r[0][...], r[1][...]
        pr, pi = _shift_down(r[2][...], r[4], i, 1), _shift_down(r[3][...], r[5], i, 1)
        return (jnp.sum(lr_ * pr + li_ * pi, axis=0, keepdims=True), jnp.sum(li_ * pr - lr_ * pi, axis=0, keepdims=True))

    da_re, da_im = _rowwise(
        "s5_da", da_sum, l, 256,
        [(lam_re, SSM_N, 0, "row"), (lam_im, SSM_N, 0, "row"), (s_re, SSM_N, 0, "row"), (s_im, SSM_N, 0, "row"),
         (s_re, SSM_N, 0, "prev"), (s_im, SSM_N, 0, "prev")], [], [], [(1, SSM_N), (1, SSM_N)])
    dbb_re_d = _mm_tn("s5_b_dw_re", u, lam_re)
    dbb_im_d = _mm_tn("s5_b_dw_im", u, lam_im)
    dbb_re = _diag_blocks(dbb_re_d, SSM_GROUP_CH, SSM_STATE).transpose(0, 2, 1).reshape(SSM_N, SSM_GROUP_CH)
    dbb_im = _diag_blocks(dbb_im_d, SSM_GROUP_CH, SSM_STATE).transpose(0, 2, 1).reshape(SSM_N, SSM_GROUP_CH)
    dlr, dli, dldt, dbr, dbi = _disc_bwd(*s["disc"], da_re.reshape(SSM_N, 1), da_im.reshape(SSM_N, 1), dbb_re, dbb_im)
    g["ssm_lambda_re"] = dlr.reshape(SSM_GROUPS, SSM_STATE)
    g["ssm_lambda_im"] = dli.reshape(SSM_GROUPS, SSM_STATE)
    g["ssm_log_dt"] = dldt.reshape(SSM_GROUPS, SSM_STATE).sum(axis=1)
    g["ssm_b_re"] = dbr.reshape(SSM_GROUPS, SSM_STATE, SSM_GROUP_CH)
    g["ssm_b_im"] = dbi.reshape(SSM_GROUPS, SSM_STATE, SSM_GROUP_CH)
    du = _mm("s5_b_dx", [(lam_re, bb_re_d), (lam_im, bb_im_d)], trans_b=True, add=du_skip, out_dtype=BF16)

    do_a = _mm("mla_o_dx", [(dya, w["wo_mla"])], trans_b=True, out_dtype=BF16, bn_cap=1024)
    dwo = _mm_tn("mla_o_dw", s["o_a"], dya)
    g["w_o_mla"] = dwo.reshape(MLA_HEADS, LANES, D_MODEL)[:, :D_V].reshape(MLA_HEADS * D_V, D_MODEL)
    delta_a = _attn_delta("mla_attn_delta", s["o_a"], do_a, heads=MLA_HEADS, bq=512)
    dq, dk, dv = _attn_bwd("mla_attn_bwd", s["q"], s["k"], s["k"].T, s["v_mla"], do_a, s["lse_a"], delta_a, qoff=0,
                           koff=0, voff=0, heads=MLA_HEADS, causal=True, scale=D_QK ** -0.5, bq=512, bk=512)
    hw = MLA_HEADS * LANES
    kr_blk = (Q_LORA + KV_LORA) // LANES
    pqkv = s["pqkv"]
    g_q, g_k = _pad_lanes(p["q_norm_g"]), _pad_lanes(p["k_norm_g"])
    lane = lax.broadcasted_iota(jnp.int32, (1, LANES), 1)
    kr_mask = jnp.logical_and(lane >= KR_LO, lane < KR_LO + D_ROPE).astype(F32)

    def prep_b_bwd(i, r, c):
        dq_ref, dk_ref, q_ref, k_ref, kr_ref, c_ref, sa_ref, sb_ref = r
        rc, sa, sb, kr = c_ref[...], sa_ref[...], sb_ref[...], kr_ref[...]
        dqs, dks, dkr, dgq, dgk = [], [], None, None, None
        for hd in range(MLA_HEADS):
            cols = slice(hd * LANES, (hd + 1) * LANES)
            dxq, dgq_h = _rms_b(q_ref[:, cols], c[0][...], _rope_b(dq_ref[:, cols], rc, sa, sb), D_QK)
            dxk, dgk_h = _rms_b(k_ref[:, cols] + kr, c[1][...], _rope_b(dk_ref[:, cols], rc, sa, sb), D_QK)
            dqs.append(dxq)
            dks.append(dxk)
            dkr = dxk if dkr is None else dkr + dxk
            dgq = dgq_h if dgq is None else dgq + dgq_h
            dgk = dgk_h if dgk is None else dgk + dgk_h
        return (jnp.concatenate(dqs, axis=1), jnp.concatenate(dks, axis=1), dkr * c[2][...], dgq, dgk)

    tab_ins = [(rope_c, LANES, 0, "row"), (rope_sa, LANES, 0, "row"), (rope_sb, LANES, 0, "row")]
    dq_raw, dk_raw, dkr, dg_q, dg_k = _rowwise(
        "mla_prep_b_bwd", prep_b_bwd, l, 256,
        [(dq, hw, 0, "row"), (dk, hw, 0, "row"), (s["q_raw"], hw, 0, "row"), (s["k_raw"], hw, 0, "row"),
         (pqkv, LANES, kr_blk, "row")] + tab_ins, [g_q, g_k, kr_mask],
        [(hw, BF16), (hw, BF16), (LANES, F32)], [(1, LANES), (1, LANES)])
    g["q_norm_g"], g["k_norm_g"] = dg_q[0, :D_QK], dg_k[0, :D_QK]
    dnq = _mm("mla_q_b_dx", [(dq_raw, w["wqb"])], trans_b=True)
    dnkv = _mm("mla_kv_b_dx", [(dk_raw, w["wk"]), (dv, w["wv"])], trans_b=True)
    dwqb = _mm_tn("mla_q_b_dw", s["nq"], dq_raw)
    g["w_q_b"] = dwqb.reshape(Q_LORA, MLA_HEADS, LANES)[:, :, :D_QK].reshape(Q_LORA, MLA_HEADS * D_QK)
    dwk = _mm_tn("mla_k_b_dw", s["nkv"], dk_raw).reshape(KV_LORA, MLA_HEADS, LANES)[:, :, :D_NOPE]
    dwv = _mm_tn("mla_v_b_dw", s["nkv"], dv).reshape(KV_LORA, MLA_HEADS, LANES)[:, :, :D_V]
    g["w_kv_b"] = jnp.concatenate([dwk, dwv], axis=2).reshape(KV_LORA, MLA_HEADS * (D_NOPE + D_V))

    def prep_a_bwd(i, r, c):
        dcq, dgqa = _rms_b(r[0][:, :Q_LORA], c[0][...], r[1][...], Q_LORA)
        dckv, dgkva = _rms_b(r[0][:, Q_LORA:Q_LORA + KV_LORA], c[1][...], r[2][...], KV_LORA)
        return (jnp.concatenate([dcq, dckv, r[3][...]], axis=1), dgqa, dgkva)

    dpqkv, dg_qa, dg_kva = _rowwise(
        "mla_prep_a_bwd", prep_a_bwd, l, 512,
        [(pqkv, QKV_W, 0, "row"), (dnq, Q_LORA, 0, "row"), (dnkv, KV_LORA, 0, "row"), (dkr, LANES, 0, "row")],
        [_row(p["q_a_norm_g"]), _row(p["kv_a_norm_g"])], [(QKV_W, BF16)], [(1, Q_LORA), (1, KV_LORA)])
    g["q_a_norm_g"], g["kv_a_norm_g"] = dg_qa[0], dg_kva[0]

    h = s["h"]
    dh = _mm("proj_dx", [(dpqkv, w["wqkv"]), (du, w["w_u"]), (dxq, w["w_xq"]), (dgl, w["w_g"])], trans_b=True,
             bn_cap=1024)
    dwqkv = _mm_tn("proj_qkv_dw", h, dpqkv)
    kr0 = Q_LORA + KV_LORA + KR_LO
    g["w_in"] = jnp.concatenate([dwqkv[:, :Q_LORA + KV_LORA], dwqkv[:, kr0:kr0 + D_ROPE], _mm_tn("proj_u_dw", h, du),
                                 _mm_tn("proj_xq_dw", h, dxq), _mm_tn("proj_gate_dw", h, dgl)], axis=1)
    dx, dg_mix = _rowwise(
        "rms_mix_bwd", rms_bwd_res, l, 512, [(x, D_MODEL, 0, "row"), (dh, D_MODEL, 0, "row"), (dx1, D_MODEL, 0, "row")],
        [_row(p["norm_mix_g"])], [(D_MODEL, F32)], [(1, D_MODEL)])
    g["norm_mix_g"] = dg_mix[0]
    g["norm_ffn_g"] = g["norm_ffn_g"][0]
    return dx, g


def _rope_tables(positions):
    inv_freq = ROPE_THETA ** (-jnp.arange(0, D_ROPE, 2, dtype=F32) / D_ROPE)
    ang = positions.astype(F32)[:, None] * inv_freq
    cos, sin = jnp.cos(ang), jnp.sin(ang)
    l = positions.shape[0]
    one, zero = jnp.ones((l, D_NOPE), F32), lambda n: jnp.zeros((l, n), F32)
    pad = LANES - D_QK
    rope_c = jnp.concatenate([one, cos, cos, zero(pad)], axis=1)
    rope_sa = jnp.concatenate([zero(D_NOPE), -sin, zero(16), zero(pad)], axis=1)
    rope_sb = jnp.concatenate([zero(D_NOPE + 16), sin, zero(pad)], axis=1)
    return rope_c, rope_sa, rope_sb


def kernel(x, mem, positions, norm_mix_g, w_in, q_a_norm_g, w_q_b, kv_a_norm_g, w_kv_b, q_norm_g, k_norm_g, w_o_mla, ssm_lambda_re, ssm_lambda_im, ssm_log_dt, ssm_b_re, ssm_b_im, ssm_c_re, ssm_c_im, ssm_d, w_glu, b_glu, w_o_ssm, mem_norm_g, w_mem_kv, xq_norm_g, xk_norm_g, w_o_cross, b_gate, w_out, norm_ffn_g, w_up, conv_w, conv_b, w_down, loss_target, m_norm_mix_g, m_w_in, m_q_a_norm_g, m_w_q_b, m_kv_a_norm_g, m_w_kv_b, m_q_norm_g, m_k_norm_g, m_w_o_mla, m_ssm_lambda_re, m_ssm_lambda_im, m_ssm_log_dt, m_ssm_b_re, m_ssm_b_im, m_ssm_c_re, m_ssm_c_im, m_ssm_d, m_w_glu, m_b_glu, m_w_o_ssm, m_mem_norm_g, m_w_mem_kv, m_xq_norm_g, m_xk_norm_g, m_w_o_cross, m_b_gate, m_w_out, m_norm_ffn_g, m_w_up, m_conv_w, m_conv_b, m_w_down, v_norm_mix_g, v_w_in, v_q_a_norm_g, v_w_q_b, v_kv_a_norm_g, v_w_kv_b, v_q_norm_g, v_k_norm_g, v_w_o_mla, v_ssm_lambda_re, v_ssm_lambda_im, v_ssm_log_dt, v_ssm_b_re, v_ssm_b_im, v_ssm_c_re, v_ssm_c_im, v_ssm_d, v_w_glu, v_b_glu, v_w_o_ssm, v_mem_norm_g, v_w_mem_kv, v_xq_norm_g, v_xk_norm_g, v_w_o_cross, v_b_gate, v_w_out, v_norm_ffn_g, v_w_up, v_conv_w, v_conv_b, v_w_down):
    a = dict(locals())
    wts = {n: a[n] for n in WEIGHT_ORDER}
    m_in = {n: a["m_" + n] for n in WEIGHT_ORDER}
    v_in = {n: a["v_" + n] for n in WEIGHT_ORDER}
    depth = norm_mix_g.shape[0]
    x0, mem0, pos0, tgt = x[0], mem[0], positions[0], loss_target[0]
    l = x0.shape[0]

    def wire(n):
        if n == "conv_w":
            return lax.bitcast_convert_type(wts[n], BF16).reshape(-1)
        return wts[n].astype(BF16).reshape(-1)

    local_flat = jnp.concatenate([wire(n) for n, _, _ in SHARDED])
    gathered = _gather_all("gather_weights", _pad_rows(local_flat, 16))
    gathered = gathered.reshape(N_DEV, -1)
    wf, off = {}, 0
    for n, _, ax in SHARDED:
        cnt = wts[n].size * (2 if n == "conv_w" else 1)
        blocks = gathered[:, off:off + cnt]
        if n == "conv_w":
            blocks = lax.bitcast_convert_type(blocks.reshape(N_DEV, -1, 2), F32)
        wf[n] = _from_shards(ax, blocks, wts[n].shape)
        off += cnt

    tabs = _rope_tables(pos0)
    layer_w = [_layer_weights(wf, i) for i in range(depth)]
    layer_p = []
    for i in range(depth):
        p = {n: wts[n][i] for n in REPLICATED}
        p["conv_w_full"] = wf["conv_w"][i]
        layer_p.append(p)

    saved = []
    xc = x0
    for i in range(depth):
        xc, s = _layer_fwd(xc, mem0, tabs, layer_w[i], layer_p[i])
        saved.append(s)

    def loss_fn(i, r, c):
        d = r[0][...] - r[1][...]
        return (d * (1.0 / D_MODEL), jnp.sum(d * d, axis=0, keepdims=True))

    dy, sq = _rowwise("loss", loss_fn, l, 512, [(xc, D_MODEL, 0, "row"), (tgt, D_MODEL, 0, "row")], [],
                      [(D_MODEL, F32)], [(1, D_MODEL)])
    loss = lax.psum(0.5 * jnp.sum(sq) / D_MODEL, ("x", "y", "c"))

    grads = [None] * depth
    dxc = dy
    for i in reversed(range(depth)):
        dxc, grads[i] = _layer_bwd(dxc, saved[i], mem0, tabs, layer_w[i], layer_p[i])
    grad_x = dxc[None]
    gfull = {n: jnp.stack([grads[i][n] for i in range(depth)]) for n in WEIGHT_ORDER}

    send = jnp.concatenate(
        [_to_shards(ax, gfull[n]) for n, _, ax in SHARDED]
        + [jnp.broadcast_to(gfull[n].reshape(1, -1), (N_DEV, gfull[n].size)) for n in REPLICATED], axis=1)
    send = _pad_rows(send, 256)
    part = _pair_sum(send, _pair_exchange("exchange_grads_core", send))
    rcv = _chip_exchange("exchange_grads_chip", part)
    order = [n for n, _, _ in SHARDED] + list(REPLICATED)

    def flat_local(d):
        return _pad_rows(jnp.concatenate([d[n].astype(F32).reshape(-1) for n in order]), 256)

    g_sum, delta, m_new, v_new = _adamw(rcv, flat_local(wts), flat_local(m_in), flat_local(v_in))
    outs = {}
    for tag, arr in (("grad", g_sum), ("delta", delta), ("m", m_new), ("v", v_new)):
        flat, off = arr.reshape(-1), 0
        for n in order:
            cnt = wts[n].size
            outs[(tag, n)] = flat[off:off + cnt].reshape(wts[n].shape)
            off += cnt
    result = [loss, grad_x]
    for tag in ("grad", "delta", "m", "v"):
        result += [outs[(tag, n)] for n in WEIGHT_ORDER]
    return tuple(result)
```

```python
import math

import jax
import jax.numpy as jnp
from jax import lax
from jax.experimental import pallas as pl
from jax.experimental.pallas import tpu as pltpu

F32 = jnp.float32
BF16 = jnp.bfloat16

N_DEV = 8
LANES = 128
LOG2E = math.log2(math.e)
VMEM_LIMIT_BYTES = 56 * 1024 * 1024

D_MODEL = 1024
EPS = 1e-6
MLA_HEADS = 8
Q_LORA = 384
KV_LORA = 256
D_NOPE = 64
D_ROPE = 32
D_QK = D_NOPE + D_ROPE
D_V = 64
ROPE_THETA = 10000.0
SSM_GROUPS = 32
SSM_GROUP_CH = 16
SSM_WIDTH = 512
SSM_STATE = 64
SSM_N = SSM_GROUPS * SSM_STATE
X_HEADS = 4
X_HEAD_DIM = 128
X_WIDTH = 512
D_FF = 2816
IN_WIDTH = Q_LORA + KV_LORA + D_ROPE + SSM_WIDTH + X_WIDTH + 3 * D_MODEL
QKV_W = Q_LORA + KV_LORA + LANES
KR_LO = D_NOPE

ADAM_LR = 0.001
ADAM_B1 = 0.9
ADAM_B2 = 0.999
ADAM_EPS = 1e-08
ADAM_WD = 0.01
ADAM_STEP = 10

REPLICATED = (
    "norm_mix_g", "q_a_norm_g", "kv_a_norm_g", "q_norm_g", "k_norm_g", "ssm_lambda_re", "ssm_lambda_im",
    "ssm_log_dt", "ssm_b_re", "ssm_b_im", "ssm_c_re", "ssm_c_im", "ssm_d", "b_glu", "mem_norm_g",
    "xq_norm_g", "xk_norm_g", "b_gate", "norm_ffn_g", "conv_b",
)
WEIGHT_ORDER = (
    "norm_mix_g", "w_in", "q_a_norm_g", "w_q_b", "kv_a_norm_g", "w_kv_b", "q_norm_g", "k_norm_g", "w_o_mla",
    "ssm_lambda_re", "ssm_lambda_im", "ssm_log_dt", "ssm_b_re", "ssm_b_im", "ssm_c_re", "ssm_c_im", "ssm_d",
    "w_glu", "b_glu", "w_o_ssm", "mem_norm_g", "w_mem_kv", "xq_norm_g", "xk_norm_g", "w_o_cross", "b_gate",
    "w_out", "norm_ffn_g", "w_up", "conv_w", "conv_b", "w_down",
)


def _params(**kw):
    return pltpu.CompilerParams(vmem_limit_bytes=VMEM_LIMIT_BYTES, **kw)


def _pick(n, cap):
    if n <= cap:
        return n
    best = None
    for m in range(LANES, cap + 1, LANES):
        if n % m == 0:
            best = m
    assert best is not None, n
    return best


_NN = (((1,), (0,)), ((), ()))
_NT = (((1,), (1,)), ((), ()))
_TN = (((0,), (0,)), ((), ()))


def _dot(a, b, dn):
    return lax.dot_general(a.astype(BF16), b.astype(BF16), dn, preferred_element_type=F32)


def _mm(name, pairs, *, trans_b=False, add=None, out_dtype=F32, bm=512, bn_cap=512):
    m = pairs[0][0].shape[0]
    n = pairs[0][1].shape[0 if trans_b else 1]
    bm = min(bm, m)
    bn = _pick(n, bn_cap)
    npair = len(pairs)

    def body(*refs):
        o_ref = refs[-1]
        acc = None
        for p in range(npair):
            d = _dot(refs[2 * p][...], refs[2 * p + 1][...], _NT if trans_b else _NN)
            acc = d if acc is None else acc + d
        if add is not None:
            acc = acc + refs[2 * npair][...]
        o_ref[...] = acc.astype(out_dtype)

    in_specs, args = [], []
    for a, b in pairs:
        k = a.shape[1]
        in_specs.append(pl.BlockSpec((bm, k), lambda i, j: (i, 0)))
        if trans_b:
            in_specs.append(pl.BlockSpec((bn, k), lambda i, j: (j, 0)))
        else:
            in_specs.append(pl.BlockSpec((k, bn), lambda i, j: (0, j)))
        args += [a, b]
    if add is not None:
        in_specs.append(pl.BlockSpec((bm, bn), lambda i, j: (i, j)))
        args.append(add)
    return pl.pallas_call(
        body, name=name, grid=(m // bm, n // bn), in_specs=in_specs,
        out_specs=pl.BlockSpec((bm, bn), lambda i, j: (i, j)),
        out_shape=jax.ShapeDtypeStruct((m, n), out_dtype), compiler_params=_params(),
    )(*args)


def _mm_tn(name, a, b, *, bm_cap=512, bn_cap=1536, bk=1024):
    l, m = a.shape
    n = b.shape[1]
    bm, bn, bk = _pick(m, bm_cap), _pick(n, bn_cap), min(bk, l)

    def body(a_ref, b_ref, o_ref):
        @pl.when(pl.program_id(2) == 0)
        def _():
            o_ref[...] = jnp.zeros_like(o_ref)

        o_ref[...] += _dot(a_ref[...], b_ref[...], _TN)

    return pl.pallas_call(
        body, name=name, grid=(m // bm, n // bn, l // bk),
        in_specs=[pl.BlockSpec((bk, bm), lambda i, j, k: (k, i)), pl.BlockSpec((bk, bn), lambda i, j, k: (k, j))],
        out_specs=pl.BlockSpec((bm, bn), lambda i, j, k: (i, j)),
        out_shape=jax.ShapeDtypeStruct((m, n), F32), compiler_params=_params(),
    )(a, b)


def _rowwise(name, fn, nrows, bm, row_ins, consts, row_outs, acc_outs=()):
    bm = min(bm, nrows)
    nblk = nrows // bm
    sub = bm // 8
    nin, nc, nro = len(row_ins), len(consts), len(row_outs)

    def body(*refs):
        i = pl.program_id(0)
        outs = fn(i, refs[:nin], refs[nin:nin + nc])
        o_refs = refs[nin + nc:nin + nc + nro]
        a_refs = refs[nin + nc + nro:]
        for r, v in zip(o_refs, outs[:nro]):
            r[...] = v.astype(r.dtype)
        if a_refs:
            @pl.when(i == 0)
            def _():
                for r in a_refs:
                    r[...] = jnp.zeros_like(r)

            for r, v in zip(a_refs, outs[nro:]):
                r[...] += v

    in_specs, args = [], []
    for arr, w, cb, kind in row_ins:
        if kind == "row":
            in_specs.append(pl.BlockSpec((bm, w), lambda i, cb=cb: (i, cb)))
        elif kind == "prev":
            in_specs.append(pl.BlockSpec((8, w), lambda i, cb=cb: (jnp.maximum(i * sub - 1, 0), cb)))
        else:
            in_specs.append(pl.BlockSpec((8, w), lambda i, cb=cb: (jnp.minimum((i + 1) * sub, nrows // 8 - 1), cb)))
        args.append(arr)
    for c in consts:
        in_specs.append(pl.BlockSpec(c.shape, lambda i: (0, 0)))
        args.append(c)
    out_specs = [pl.BlockSpec((bm, w), lambda i: (i, 0)) for w, _ in row_outs]
    out_specs += [pl.BlockSpec(s, lambda i: (0, 0)) for s in acc_outs]
    out_shape = [jax.ShapeDtypeStruct((nrows, w), dt) for w, dt in row_outs]
    out_shape += [jax.ShapeDtypeStruct(s, F32) for s in acc_outs]
    res = pl.pallas_call(
        body, name=name, grid=(nblk,), in_specs=in_specs, out_specs=out_specs, out_shape=out_shape,
        compiler_params=_params(),
    )(*args)
    return res


def _rms_f(x, g, n):
    r = lax.rsqrt(jnp.sum(x * x, axis=-1, keepdims=True) * (1.0 / n) + EPS)
    return x * r * g


def _rms_b(x, g, dy, n):
    r = lax.rsqrt(jnp.sum(x * x, axis=-1, keepdims=True) * (1.0 / n) + EPS)
    gx = dy * g
    dx = r * gx - x * (r * r * r * (jnp.sum(x * gx, axis=-1, keepdims=True) * (1.0 / n)))
    dg = jnp.sum(dy * (x * r), axis=0, keepdims=True)
    return dx, dg


def _rope_f(x, c, sa, sb):
    return x * c + pltpu.roll(x, LANES - 16, 1) * sa + pltpu.roll(x, 16, 1) * sb


def _rope_b(g, c, sa, sb):
    return g * c + pltpu.roll(g * sa, 16, 1) + pltpu.roll(g * sb, LANES - 16, 1)


def _gelu(x):
    c = math.sqrt(2.0 / math.pi)
    return 0.5 * x * (1.0 + jnp.tanh(c * (x + 0.044715 * (x * x * x))))


def _gelu_grad(x):
    c = math.sqrt(2.0 / math.pi)
    th = jnp.tanh(c * (x + 0.044715 * (x * x * x)))
    return 0.5 * (1.0 + th) + 0.5 * x * (1.0 - th * th) * (c * (1.0 + 3.0 * 0.044715 * (x * x)))


def _row_ids(bm):
    return lax.broadcasted_iota(jnp.int32, (bm, 1), 0)


def _shift_down(x, halo_ref, i, k):
    bm = x.shape[0]
    row = _row_ids(bm)
    live = (i > 0).astype(F32)
    out = pltpu.roll(x, k, 0)
    for r in range(k):
        e = (row == r).astype(F32)
        out = out * (1.0 - e) + e * (halo_ref[8 - k + r:8 - k + r + 1, :] * live)
    return out


def _shift_up(x, halo_ref, i, nblk, k):
    bm = x.shape[0]
    row = _row_ids(bm)
    live = (i < nblk - 1).astype(F32)
    out = pltpu.roll(x, bm - k, 0)
    for r in range(k):
        e = (row == bm - k + r).astype(F32)
        out = out * (1.0 - e) + e * (halo_ref[r:r + 1, :] * live)
    return out


def _live_pairs(nq, nk, bq, bk, causal, key_major):
    pairs = [(i, j) for i in range(nq) for j in range(nk) if not causal or j * bk <= i * bq + bq - 1]
    if key_major:
        pairs.sort(key=lambda ij: (ij[1], ij[0]))
    return (jnp.asarray([p[0] for p in pairs], jnp.int32), jnp.asarray([p[1] for p in pairs], jnp.int32))


def _attn_fwd(name, qa, ka, vta, *, qoff, koff, voff, heads, causal, scale, bq, bk):
    lq, lk = qa.shape[0], ka.shape[0]
    bq, bk = min(bq, lq), min(bk, lk)
    nq, nk = lq // bq, lk // bk
    c2 = scale * LOG2E
    tab_i, tab_j = _live_pairs(nq, nk, bq, bk, causal, key_major=False)

    def body(ti, tj, q_ref, k_ref, vt_ref, o_ref, lse_ref, m_s, l_s, acc_s):
        t = pl.program_id(1)
        i, j = ti[t], tj[t]
        j_last = jnp.minimum(nk - 1, (i * bq + bq - 1) // bk) if causal else nk - 1

        @pl.when(j == 0)
        def _():
            m_s[...] = jnp.full_like(m_s, -1e30)
            l_s[...] = jnp.zeros_like(l_s)
            acc_s[...] = jnp.zeros_like(acc_s)

        def step(masked):
            st = _dot(k_ref[...], q_ref[...], _NT) * c2
            if masked:
                key = j * bk + lax.broadcasted_iota(jnp.int32, (bk, bq), 0)
                qry = i * bq + lax.broadcasted_iota(jnp.int32, (bk, bq), 1)
                st = jnp.where(key <= qry, st, -1e30)
            m_prev = m_s[...]
            m_new = jnp.maximum(m_prev, jnp.max(st, axis=0, keepdims=True))
            alpha = jnp.exp2(m_prev - m_new)
            pt = jnp.exp2(st - m_new)
            l_s[...] = alpha * l_s[...] + jnp.sum(pt, axis=0, keepdims=True)
            acc_s[...] = alpha * acc_s[...] + _dot(vt_ref[...], pt, _NN)
            m_s[...] = m_new

        if causal:
            full = j * bk + bk - 1 <= i * bq
            pl.when(full)(lambda: step(False))
            pl.when(jnp.logical_not(full))(lambda: step(True))
        else:
            step(False)

        @pl.when(j == j_last)
        def _():
            l = l_s[...]
            o_ref[...] = (acc_s[...] / l).T.astype(o_ref.dtype)
            lse_ref[0] = m_s[...] + jnp.log2(l)

    grid_spec = pltpu.PrefetchScalarGridSpec(
        num_scalar_prefetch=2, grid=(heads, int(tab_i.shape[0])),
        in_specs=[pl.BlockSpec((bq, LANES), lambda h, t, ti, tj: (ti[t], qoff + h)),
                  pl.BlockSpec((bk, LANES), lambda h, t, ti, tj: (tj[t], koff + h)),
                  pl.BlockSpec((LANES, bk), lambda h, t, ti, tj: (voff + h, tj[t]))],
        out_specs=[pl.BlockSpec((bq, LANES), lambda h, t, ti, tj: (ti[t], h)),
                   pl.BlockSpec((1, 1, bq), lambda h, t, ti, tj: (h, 0, ti[t]))],
        scratch_shapes=[pltpu.VMEM((1, bq), F32), pltpu.VMEM((1, bq), F32), pltpu.VMEM((LANES, bq), F32)])
    return pl.pallas_call(
        body, name=name, grid_spec=grid_spec,
        out_shape=[jax.ShapeDtypeStruct((lq, heads * LANES), BF16), jax.ShapeDtypeStruct((heads, 1, lq), F32)],
        compiler_params=_params(),
    )(tab_i, tab_j, qa, ka, vta)


def _attn_delta(name, oa, doa, *, heads, bq):
    lq = oa.shape[0]
    bq = min(bq, lq)

    def body(o_ref, do_ref, d_ref):
        d_ref[0] = jnp.sum((o_ref[...].astype(F32) * do_ref[...].astype(F32)).T, axis=0, keepdims=True)

    blk = pl.BlockSpec((bq, LANES), lambda h, i: (i, h))
    return pl.pallas_call(
        body, name=name, grid=(heads, lq // bq), in_specs=[blk, blk],
        out_specs=pl.BlockSpec((1, 1, bq), lambda h, i: (h, 0, i)),
        out_shape=jax.ShapeDtypeStruct((heads, 1, lq), F32), compiler_params=_params(),
    )(oa, doa)


def _attn_bwd(name, qa, ka, kta, va, doa, lsea, deltaa, *, qoff, koff, voff, heads, causal, scale, bq, bk):
    lq, lk = qa.shape[0], ka.shape[0]
    bq, bk = min(bq, lq), min(bk, lk)
    nq, nk = lq // bq, lk // bk
    c2 = scale * LOG2E
    tab_i, tab_j = _live_pairs(nq, nk, bq, bk, causal, key_major=True)

    def body(ti, tj, q_ref, k_ref, kt_ref, v_ref, do_ref, lse_ref, delta_ref, dqt_ref, dk_ref, dv_ref):
        t = pl.program_id(1)
        i, j = ti[t], tj[t]
        i_first = (j * bk) // bq if causal else 0

        @pl.when(t == 0)
        def _():
            dqt_ref[...] = jnp.zeros_like(dqt_ref)

        @pl.when(i == i_first)
        def _():
            dk_ref[...] = jnp.zeros_like(dk_ref)
            dv_ref[...] = jnp.zeros_like(dv_ref)

        def step(masked):
            q, k, v, do = q_ref[...], k_ref[...], v_ref[...], do_ref[...]
            st = _dot(k, q, _NT) * c2
            if masked:
                key = j * bk + lax.broadcasted_iota(jnp.int32, (bk, bq), 0)
                qry = i * bq + lax.broadcasted_iota(jnp.int32, (bk, bq), 1)
                st = jnp.where(key <= qry, st, -1e30)
            pt = jnp.exp2(st - lse_ref[0])
            dv_ref[...] += _dot(pt, do, _NN)
            dpt = _dot(v, do, _NT)
            dst = (pt * (dpt - delta_ref[0]) * scale).astype(BF16)
            dk_ref[...] += _dot(dst, q, _NN)
            dqt_ref[0, i] += _dot(kt_ref[...], dst, _NN)

        if causal:
            full = j * bk + bk - 1 <= i * bq
            pl.when(full)(lambda: step(False))
            pl.when(jnp.logical_not(full))(lambda: step(True))
        else:
            step(False)

    q_spec = lambda off: pl.BlockSpec((bq, LANES), lambda h, t, ti, tj: (ti[t], off + h))
    kv_spec = lambda off: pl.BlockSpec((bk, LANES), lambda h, t, ti, tj: (tj[t], off + h))
    row_spec = pl.BlockSpec((1, 1, bq), lambda h, t, ti, tj: (h, 0, ti[t]))
    grid_spec = pltpu.PrefetchScalarGridSpec(
        num_scalar_prefetch=2, grid=(heads, int(tab_i.shape[0])),
        in_specs=[q_spec(qoff), kv_spec(koff), pl.BlockSpec((LANES, bk), lambda h, t, ti, tj: (koff + h, tj[t])),
                  kv_spec(voff), q_spec(0), row_spec, row_spec],
        out_specs=[pl.BlockSpec((1, nq, LANES, bq), lambda h, t, ti, tj: (h, 0, 0, 0)),
                   pl.BlockSpec((bk, LANES), lambda h, t, ti, tj: (tj[t], h)),
                   pl.BlockSpec((bk, LANES), lambda h, t, ti, tj: (tj[t], h))])
    dqt, dk, dv = pl.pallas_call(
        body, name=name, grid_spec=grid_spec,
        out_shape=[jax.ShapeDtypeStruct((heads, nq, LANES, bq), F32), jax.ShapeDtypeStruct((lk, heads * LANES), F32),
                   jax.ShapeDtypeStruct((lk, heads * LANES), F32)],
        compiler_params=_params(),
    )(tab_i, tab_j, qa, ka, kta, va, doa, lsea, deltaa)
    return dqt.transpose(1, 3, 0, 2).reshape(lq, heads * LANES), dk, dv


def _scan(name, bre, bim, are, aim, *, reverse, chunk=512, strip=512, unroll=4):
    l, n = bre.shape
    t = min(chunk, l)
    nc, ns = l // t, n // strip
    ng = t // 8
    edge = 0 if reverse else 7

    def cmap(w, c):
        return ((nc - 1 - c) if reverse else c, w)

    def grp(g):
        return pl.ds(pl.multiple_of(g * 8, 8), 8)

    def body(bre_ref, bim_ref, are_ref, aim_ref, sre_ref, sim_ref, cre_s, cim_s):
        c = pl.program_id(1)
        row8 = lax.broadcasted_iota(jnp.int32, (8, 1), 0)
        ar, ai = are_ref[...], aim_ref[...]

        def group_scan(xr, xi):
            pr, pi = ar, ai
            for d in (1, 2, 4):
                own = ((row8 < 8 - d) if reverse else (row8 >= d)).astype(F32)
                shift = (8 - d) if reverse else d
                sr, si = pltpu.roll(xr, shift, 0) * own, pltpu.roll(xi, shift, 0) * own
                xr, xi = xr + pr * sr - pi * si, xi + pr * si + pi * sr
                pr, pi = pr * pr - pi * pi, 2.0 * pr * pi
            return xr, xi

        @pl.when(c == 0)
        def _():
            cre_s[...] = jnp.zeros_like(cre_s)
            cim_s[...] = jnp.zeros_like(cim_s)

        def local(g, _):
            xr, xi = group_scan(bre_ref[grp(g), :], bim_ref[grp(g), :])
            sre_ref[grp(g), :] = xr
            sim_ref[grp(g), :] = xi
            return 0

        lax.fori_loop(0, ng, local, 0, unroll=unroll)
        e = (row8 == 7 - edge).astype(F32)
        pw_r, pw_i = group_scan(e * ar, e * ai)
        e_out = (row8 == edge).astype(F32)
        a8_r = jnp.sum(pw_r * e_out, axis=0, keepdims=True)
        a8_i = jnp.sum(pw_i * e_out, axis=0, keepdims=True)
        cr, ci = cre_s[...], cim_s[...]
        for i in range(ng):
            g = ng - 1 - i if reverse else i
            rows = slice(g * 8, g * 8 + 8)
            lr, li = sre_ref[g * 8 + edge:g * 8 + edge + 1, :], sim_ref[g * 8 + edge:g * 8 + edge + 1, :]
            sre_ref[rows, :] = sre_ref[rows, :] + pw_r * cr - pw_i * ci
            sim_ref[rows, :] = sim_ref[rows, :] + pw_r * ci + pw_i * cr
            cr, ci = lr + a8_r * cr - a8_i * ci, li + a8_r * ci + a8_i * cr
        cre_s[...] = cr
        cim_s[...] = ci

    blk = pl.BlockSpec((t, strip), cmap)
    a_blk = pl.BlockSpec((1, strip), lambda w, c: (0, w))
    return pl.pallas_call(
        body, name=name, grid=(ns, nc), in_specs=[blk, blk, a_blk, a_blk], out_specs=[blk, blk],
        out_shape=[jax.ShapeDtypeStruct((l, n), F32), jax.ShapeDtypeStruct((l, n), F32)],
        scratch_shapes=[pltpu.VMEM((1, strip), F32), pltpu.VMEM((1, strip), F32)],
        compiler_params=_params(),
    )(bre, bim, are, aim)


def _disc_math(lr, li, ldt, br, bi):
    dt = jnp.exp(ldt)
    mag = jnp.exp(lr * dt)
    a_re, a_im = mag * jnp.cos(li * dt), mag * jnp.sin(li * dt)
    den = lr * lr + li * li
    e_re, e_im = a_re - 1.0, a_im
    f_re = (e_re * lr + e_im * li) / den
    f_im = (e_im * lr - e_re * li) / den
    return a_re, a_im, f_re * br - f_im * bi, f_re * bi + f_im * br


def _disc_fwd(lr, li, ldt, br, bi):
    def body(lr_ref, li_ref, ldt_ref, br_ref, bi_ref, are_ref, aim_ref, bbr_ref, bbi_ref):
        a_re, a_im, bb_re, bb_im = _disc_math(lr_ref[...], li_ref[...], ldt_ref[...], br_ref[...], bi_ref[...])
        are_ref[...] = a_re
        aim_ref[...] = a_im
        bbr_ref[...] = bb_re
        bbi_ref[...] = bb_im

    col = jax.ShapeDtypeStruct(lr.shape, F32)
    mat = jax.ShapeDtypeStruct(br.shape, F32)
    return pl.pallas_call(body, name="s5_disc_fwd", out_shape=[col, col, mat, mat], compiler_params=_params())(
        lr, li, ldt, br, bi)


def _disc_bwd(lr, li, ldt, br, bi, da_re, da_im, dbb_re, dbb_im):
    def body(lr_ref, li_ref, ldt_ref, br_ref, bi_ref, g0, g1, g2, g3, o0, o1, o2, o3, o4):
        _, vjp = jax.vjp(_disc_math, lr_ref[...], li_ref[...], ldt_ref[...], br_ref[...], bi_ref[...])
        grads = vjp((g0[...], g1[...], g2[...], g3[...]))
        for o, g in zip((o0, o1, o2, o3, o4), grads):
            o[...] = g

    col = jax.ShapeDtypeStruct(lr.shape, F32)
    mat = jax.ShapeDtypeStruct(br.shape, F32)
    return pl.pallas_call(body, name="s5_disc_bwd", out_shape=[col, col, col, mat, mat], compiler_params=_params())(
        lr, li, ldt, br, bi, da_re, da_im, dbb_re, dbb_im)


N_CHIP = 4


def _place():
    x, y, c = lax.axis_index("x"), lax.axis_index("y"), lax.axis_index("c")
    return (x, y, c), (x, y, 1 - c), [(1 - x, y), (x, 1 - y), (1 - x, 1 - y)]


def _lin(px, py, pc):
    return 4 * px + 2 * py + pc


def _remote(src, dst, sems, k, dev):
    return pltpu.make_async_remote_copy(src_ref=src, dst_ref=dst, send_sem=sems[0].at[k], recv_sem=sems[1].at[k],
                                        device_id=dev, device_id_type=pl.DeviceIdType.MESH)


def _hbm_call(body, name, srcs, out_shapes, nsem):
    n = len(srcs)
    hbm = pl.BlockSpec(memory_space=pltpu.HBM)

    def wrapped(*refs):
        body(refs[:n], refs[n:2 * n], (refs[2 * n], refs[2 * n + 1]), refs[2 * n + 2])

    return pl.pallas_call(
        wrapped, name=name, in_specs=[hbm] * n, out_specs=[hbm] * n, out_shape=out_shapes,
        scratch_shapes=[pltpu.SemaphoreType.DMA((nsem * n,)), pltpu.SemaphoreType.DMA((nsem * n,)),
                        pltpu.SemaphoreType.DMA((n,))],
        compiler_params=pltpu.CompilerParams(has_side_effects=True),
    )(*srcs)


def _gather_all(name, srcs):
    def body(s_refs, o_refs, sems, loc_sems):
        me, sib, chips = _place()
        c = me[2]
        started, local = [], []
        for g, (s_ref, o_ref) in enumerate(zip(s_refs, o_refs)):
            slot = lambda dev, o_ref=o_ref: o_ref.at[_lin(*dev)]
            mine = pltpu.make_async_copy(s_ref, slot(me), loc_sems.at[g])
            mine.start()
            local.append(mine)
            first = [_remote(s_ref, slot(me), sems, 7 * g, sib)]
            first += [_remote(s_ref, slot(me), sems, 7 * g + 1 + j, (*chip, c)) for j, chip in enumerate(chips)]
            for cp in first:
                cp.start()
            started += first
        for g, (s_ref, o_ref) in enumerate(zip(s_refs, o_refs)):
            slot = lambda dev, o_ref=o_ref: o_ref.at[_lin(*dev)]
            for j, chip in enumerate(chips):
                _remote(s_ref, slot((*chip, c)), sems, 7 * g + 1 + j, me).wait_recv()
                cp = _remote(slot((*chip, c)), slot((*chip, c)), sems, 7 * g + 4 + j, sib)
                cp.start()
                started.append(cp)
        for g, (s_ref, o_ref) in enumerate(zip(s_refs, o_refs)):
            slot = lambda dev, o_ref=o_ref: o_ref.at[_lin(*dev)]
            _remote(s_ref, slot(sib), sems, 7 * g, me).wait_recv()
            for j, chip in enumerate(chips):
                _remote(s_ref, slot((*chip, 1 - c)), sems, 7 * g + 4 + j, me).wait_recv()
        for cp in started:
            cp.wait_send()
        for cp in local:
            cp.wait()

    return _hbm_call(body, name, srcs, [jax.ShapeDtypeStruct((N_DEV,) + s.shape, s.dtype) for s in srcs], 7)


def _pair_exchange(name, sends):
    def body(s_refs, o_refs, sems, loc_sems):
        me, sib, _ = _place()
        c = me[2]
        copies = [_remote(s_ref.at[2 * q + (1 - c)], o_ref.at[q], sems, N_CHIP * g + q, sib)
                  for g, (s_ref, o_ref) in enumerate(zip(s_refs, o_refs)) for q in range(N_CHIP)]
        for cp in copies:
            cp.start()
        for cp in copies:
            cp.wait()

    return _hbm_call(body, name, sends, [jax.ShapeDtypeStruct((N_CHIP,) + s.shape[1:], s.dtype) for s in sends],
                     N_CHIP)


def _row_block(r, cap):
    best = 8
    for m in range(8, min(r, cap) + 1, 8):
        if r % m == 0:
            best = m
    return best


def _pair_sum(name, send, got):
    _, r, cols = send.shape
    bm = _row_block(r, max(8, 256 * 1024 // cols))
    core = lax.axis_index("c").astype(jnp.int32).reshape(1)

    def body(core_ref, a_ref, b_ref, o_ref):
        o_ref[...] = a_ref[...] + b_ref[...]

    grid_spec = pltpu.PrefetchScalarGridSpec(
        num_scalar_prefetch=1, grid=(N_CHIP, r // bm),
        in_specs=[pl.BlockSpec((1, bm, cols), lambda q, i, cr: (2 * q + cr[0], i, 0)),
                  pl.BlockSpec((1, bm, cols), lambda q, i, cr: (q, i, 0))],
        out_specs=pl.BlockSpec((1, bm, cols), lambda q, i, cr: (q, i, 0)))
    return pl.pallas_call(body, name=name, grid_spec=grid_spec,
                          out_shape=jax.ShapeDtypeStruct((N_CHIP, r, cols), F32), compiler_params=_params())(
        core, send, got)


def _chip_exchange(name, parts):
    def body(p_refs, o_refs, sems, loc_sems):
        me, _, chips = _place()
        c = me[2]
        chip_id = lambda chip: 2 * chip[0] + chip[1]
        my_chip = chip_id(me)
        local, copies = [], []
        for g, (p_ref, o_ref) in enumerate(zip(p_refs, o_refs)):
            mine = pltpu.make_async_copy(p_ref.at[my_chip], o_ref.at[my_chip], loc_sems.at[g])
            mine.start()
            local.append(mine)
            for j, chip in enumerate(chips):
                cp = _remote(p_ref.at[chip_id(chip)], o_ref.at[my_chip], sems, 3 * g + j, (*chip, c))
                cp.start()
                copies.append(cp)
        for g, (p_ref, o_ref) in enumerate(zip(p_refs, o_refs)):
            for j, chip in enumerate(chips):
                _remote(p_ref.at[my_chip], o_ref.at[chip_id(chip)], sems, 3 * g + j, me).wait_recv()
        for cp in copies:
            cp.wait_send()
        for cp in local:
            cp.wait()

    return _hbm_call(body, name, parts, [jax.ShapeDtypeStruct(p.shape, p.dtype) for p in parts], 3)


def _adamw(name, rcv, w, m, v):
    r, c = w.shape
    nslot = rcv.shape[0]
    bm = _row_block(r, max(8, 256 * 1024 // c))

    def body(rcv_ref, w_ref, m_ref, v_ref, g_ref, d_ref, m2_ref, v2_ref):
        g = rcv_ref[0]
        for s in range(1, nslot):
            g = g + rcv_ref[s]
        m2 = ADAM_B1 * m_ref[...] + (1.0 - ADAM_B1) * g
        v2 = ADAM_B2 * v_ref[...] + (1.0 - ADAM_B2) * (g * g)
        m_hat = m2 / (1.0 - ADAM_B1 ** ADAM_STEP)
        v_hat = v2 / (1.0 - ADAM_B2 ** ADAM_STEP)
        g_ref[...] = g
        d_ref[...] = -ADAM_LR * (m_hat / (jnp.sqrt(v_hat) + ADAM_EPS) + ADAM_WD * w_ref[...])
        m2_ref[...] = m2
        v2_ref[...] = v2

    blk = pl.BlockSpec((bm, c), lambda i: (i, 0))
    out = jax.ShapeDtypeStruct((r, c), F32)
    return pl.pallas_call(
        body, name=name, grid=(r // bm,),
        in_specs=[pl.BlockSpec((nslot, bm, c), lambda i: (0, i, 0)), blk, blk, blk], out_specs=[blk] * 4,
        out_shape=[out] * 4, compiler_params=_params(),
    )(rcv, w, m, v)


IN_SHARD = IN_WIDTH // N_DEV
UP_SHARD = 2 * D_FF // N_DEV
GROUPS = (
    ("g128", LANES, (("w_q_b", Q_LORA, D_QK), ("w_kv_b", KV_LORA, D_NOPE + D_V), ("w_o_mla", MLA_HEADS * D_V, LANES),
                     ("w_o_ssm", SSM_WIDTH, LANES), ("w_o_cross", X_WIDTH, LANES))),
    ("g512", SSM_WIDTH, (("w_glu", SSM_WIDTH // N_DEV, SSM_WIDTH),)),
    ("g1024", D_MODEL, (("w_mem_kv", D_MODEL // N_DEV, D_MODEL), ("w_out", D_MODEL // N_DEV, D_MODEL),
                        ("w_down", D_FF // N_DEV, D_MODEL))),
    ("g640", 640, (("w_in", D_MODEL, IN_SHARD),)),
    ("g768", 768, (("w_up", D_MODEL, UP_SHARD),)),
    ("gconv", 768, (("conv_w", 3, UP_SHARD),)),
)
REP_W = 1024


def _rows8(a):
    return -(-a // 8) * 8


def _group_rows(params, depth):
    off, r = {}, 0
    for n, a, _ in params:
        for li in range(depth):
            off[(n, li)] = r
            r += _rows8(a)
    return off, r


def _group_local(width, params, vals, depth, dtype):
    blocks = []
    for n, a, b in params:
        for li in range(depth):
            blocks.append(jnp.pad(vals[n][li].astype(dtype), ((0, _rows8(a) - a), (0, width - b))))
    return jnp.concatenate(blocks, axis=0)


def _rep_rows(flat):
    n = flat.shape[-1]
    per = REP_W * 64
    tot = -(-n // per) * per
    flat = jnp.pad(flat, [(0, 0)] * (flat.ndim - 1) + [(0, tot - n)])
    return flat.reshape(flat.shape[:-1] + (tot // REP_W, REP_W))


IN_SEGS = (
    (0, Q_LORA + KV_LORA, "wqkv", 0),
    (Q_LORA + KV_LORA, Q_LORA + KV_LORA + D_ROPE, "wqkv", Q_LORA + KV_LORA + KR_LO),
    (Q_LORA + KV_LORA + D_ROPE, Q_LORA + KV_LORA + D_ROPE + SSM_WIDTH, "w_u", 0),
    (Q_LORA + KV_LORA + D_ROPE + SSM_WIDTH, Q_LORA + KV_LORA + D_ROPE + SSM_WIDTH + X_WIDTH, "w_xq", 0),
    (Q_LORA + KV_LORA + D_ROPE + SSM_WIDTH + X_WIDTH, IN_WIDTH, "w_g", 0),
)
IN_PARTS = (("wqkv", QKV_W), ("w_u", SSM_WIDTH), ("w_xq", X_WIDTH), ("w_g", 3 * D_MODEL))


def _in_pieces():
    out = []
    for d in range(N_DEV):
        for lo, hi, part, dst in IN_SEGS:
            s, e = max(lo, d * IN_SHARD), min(hi, (d + 1) * IN_SHARD)
            if s < e:
                out.append((d, s - d * IN_SHARD, e - s, [p for p, _ in IN_PARTS].index(part), dst + s - lo))
    return out


def _unpack_w_in(gathered, row0, bm=256):
    def body(x_ref, *o_refs):
        o_refs[0][...] = jnp.zeros_like(o_refs[0])
        for d, src, n, part, dst in _in_pieces():
            o_refs[part][:, dst:dst + n] = x_ref[d, :, src:src + n]

    return pl.pallas_call(
        body, name="unpack_w_in", grid=(D_MODEL // bm,),
        in_specs=[pl.BlockSpec((N_DEV, bm, gathered.shape[2]), lambda i: (0, row0 // bm + i, 0))],
        out_specs=[pl.BlockSpec((bm, w), lambda i: (i, 0)) for _, w in IN_PARTS],
        out_shape=[jax.ShapeDtypeStruct((D_MODEL, w), gathered.dtype) for _, w in IN_PARTS], compiler_params=_params(),
    )(gathered)


def _pack_w_in(parts, bm=256):
    def body(*refs):
        o_ref = refs[-1]
        o_ref[...] = jnp.zeros_like(o_ref)
        for d, src, n, part, dst in _in_pieces():
            o_ref[d, :, src:src + n] = refs[part][:, dst:dst + n]

    return pl.pallas_call(
        body, name="pack_w_in", grid=(D_MODEL // bm,),
        in_specs=[pl.BlockSpec((bm, w), lambda i: (i, 0)) for _, w in IN_PARTS],
        out_specs=pl.BlockSpec((N_DEV, bm, 640), lambda i: (0, i, 0)),
        out_shape=jax.ShapeDtypeStruct((N_DEV, D_MODEL, 640), F32), compiler_params=_params(),
    )(*parts)


def _unpack_w_up(gathered, row0, bm=256):
    def body(x_ref, o_ref):
        for d in range(N_DEV):
            o_ref[:, d * UP_SHARD:(d + 1) * UP_SHARD] = x_ref[d, :, :UP_SHARD]

    return pl.pallas_call(
        body, name="unpack_w_up", grid=(D_MODEL // bm,),
        in_specs=[pl.BlockSpec((N_DEV, bm, gathered.shape[2]), lambda i: (0, row0 // bm + i, 0))],
        out_specs=pl.BlockSpec((bm, 2 * D_FF), lambda i: (i, 0)),
        out_shape=jax.ShapeDtypeStruct((D_MODEL, 2 * D_FF), gathered.dtype), compiler_params=_params(),
    )(gathered)


def _pack_w_up(dw_g, dw_v, bm=256):
    half = N_DEV // 2

    def body(g_ref, v_ref, o_ref):
        o_ref[...] = jnp.zeros_like(o_ref)
        for d in range(N_DEV):
            src = g_ref if d < half else v_ref
            c0 = (d % half) * UP_SHARD
            o_ref[d, :, :UP_SHARD] = src[:, c0:c0 + UP_SHARD]

    blk = pl.BlockSpec((bm, D_FF), lambda i: (i, 0))
    return pl.pallas_call(
        body, name="pack_w_up", grid=(D_MODEL // bm,), in_specs=[blk, blk],
        out_specs=pl.BlockSpec((N_DEV, bm, 768), lambda i: (0, i, 0)),
        out_shape=jax.ShapeDtypeStruct((N_DEV, D_MODEL, 768), F32), compiler_params=_params(),
    )(dw_g, dw_v)


def _cols_to_rows(full):
    a, nb = full.shape
    return full.reshape(a, N_DEV, nb // N_DEV).transpose(1, 0, 2)


def _rows_to_cols(blocks):
    n, a, b = blocks.shape
    return blocks.transpose(1, 0, 2).reshape(a, n * b)


def _block_diag_in(bb):
    b3 = bb.reshape(SSM_GROUPS, SSM_STATE, SSM_GROUP_CH).transpose(0, 2, 1)
    eye = jnp.eye(SSM_GROUPS, dtype=bb.dtype)
    return (b3[:, :, None, :] * eye[:, None, :, None]).reshape(SSM_WIDTH, SSM_N)


def _block_diag_out(cc):
    c3 = cc.transpose(0, 2, 1)
    eye = jnp.eye(SSM_GROUPS, dtype=cc.dtype)
    return (c3[:, :, None, :] * eye[:, None, :, None]).reshape(SSM_N, SSM_WIDTH)


def _diag_blocks(mat, rows_per, cols_per):
    m4 = mat.reshape(SSM_GROUPS, rows_per, SSM_GROUPS, cols_per)
    eye = jnp.eye(SSM_GROUPS, dtype=mat.dtype)
    return jnp.sum(m4 * eye[:, None, :, None], axis=2)


def _layer_weights(gath, offs, li):
    def blk(grp, n, a):
        r0 = offs[grp][(n, li)]
        return gath[grp][:, r0:r0 + a, :]

    o = dict(zip([p for p, _ in IN_PARTS], _unpack_w_in(gath["g640"], offs["g640"][("w_in", li)])))
    o["wqb"] = _rows_to_cols(blk("g128", "w_q_b", Q_LORA))
    wkv = _rows_to_cols(blk("g128", "w_kv_b", KV_LORA)).reshape(KV_LORA, MLA_HEADS, D_NOPE + D_V)
    o["wk"] = jnp.pad(wkv[:, :, :D_NOPE], ((0, 0), (0, 0), (0, LANES - D_NOPE))).reshape(KV_LORA, MLA_HEADS * LANES)
    o["wv"] = jnp.pad(wkv[:, :, D_NOPE:], ((0, 0), (0, 0), (0, LANES - D_V))).reshape(KV_LORA, MLA_HEADS * LANES)
    o["wv_t"] = o["wv"].T
    wo = _rows_to_cols(blk("g128", "w_o_mla", MLA_HEADS * D_V)).reshape(MLA_HEADS, D_V, D_MODEL)
    o["wo_mla"] = jnp.pad(wo, ((0, 0), (0, LANES - D_V), (0, 0))).reshape(MLA_HEADS * LANES, D_MODEL)
    o["w_o_ssm"] = _rows_to_cols(blk("g128", "w_o_ssm", SSM_WIDTH))
    o["w_o_cross"] = _rows_to_cols(blk("g128", "w_o_cross", X_WIDTH))
    o["w_glu"] = blk("g512", "w_glu", SSM_WIDTH // N_DEV).reshape(SSM_WIDTH, SSM_WIDTH)
    o["w_mem_kv"] = blk("g1024", "w_mem_kv", D_MODEL // N_DEV).reshape(D_MODEL, 2 * X_WIDTH)
    o["w_out"] = blk("g1024", "w_out", D_MODEL // N_DEV).reshape(D_MODEL, D_MODEL)
    o["w_down"] = blk("g1024", "w_down", D_FF // N_DEV).reshape(D_FF, D_MODEL)
    o["w_up"] = _unpack_w_up(gath["g768"], offs["g768"][("w_up", li)])
    o["w_up_g"], o["w_up_v"] = o["w_up"][:, :D_FF], o["w_up"][:, D_FF:]
    o["conv_w"] = _rows_to_cols(blk("gconv", "conv_w", 3)[:, :, :UP_SHARD])
    return o


def _row(v):
    return v.reshape(1, -1).astype(F32)


def _pad_lanes(v, n=LANES):
    return jnp.pad(v, (0, n - v.shape[0])).reshape(1, n).astype(F32)


def _layer_fwd(x, mem, tabs, w, p):
    l = x.shape[0]
    rope_c, rope_sa, rope_sb = tabs
    s = {"x": x}
    g_mix, g_qa, g_kva = _row(p["norm_mix_g"]), _row(p["q_a_norm_g"]), _row(p["kv_a_norm_g"])
    g_q, g_k = _pad_lanes(p["q_norm_g"]), _pad_lanes(p["k_norm_g"])

    (h,) = _rowwise("rms_mix", lambda i, r, c: (_rms_f(r[0][...], c[0][...], D_MODEL),), l, 512,
                    [(x, D_MODEL, 0, "row")], [g_mix], [(D_MODEL, BF16)])
    pqkv = _mm("proj_qkv", [(h, w["wqkv"])])
    u = _mm("proj_u", [(h, w["w_u"])])
    xq = _mm("proj_xq", [(h, w["w_xq"])])
    gl = _mm("proj_gate", [(h, w["w_g"])], bm=1024, bn_cap=1024)
    s.update(h=h, pqkv=pqkv, u=u, xq=xq, gl=gl)

    def prep_a(i, r, c):
        return (_rms_f(r[0][:, :Q_LORA], c[0][...], Q_LORA),
                _rms_f(r[0][:, Q_LORA:Q_LORA + KV_LORA], c[1][...], KV_LORA))

    nq, nkv = _rowwise("mla_prep_a", prep_a, l, 512, [(pqkv, QKV_W, 0, "row")], [g_qa, g_kva],
                       [(Q_LORA, BF16), (KV_LORA, BF16)])
    q_raw = _mm("mla_q_b", [(nq, w["wqb"])], bn_cap=1024)
    k_raw = _mm("mla_k_b", [(nkv, w["wk"])], bn_cap=1024)
    v_mla = _mm("mla_v_b", [(nkv, w["wv"])], out_dtype=BF16, bn_cap=1024)
    vt_mla = _mm("mla_vt_b", [(w["wv_t"], nkv)], trans_b=True, out_dtype=BF16, bn_cap=1024)

    def prep_b(i, r, c):
        q_ref, k_ref, kr_ref, c_ref, sa_ref, sb_ref = r
        rc, sa, sb, kr = c_ref[...], sa_ref[...], sb_ref[...], kr_ref[...]
        qs, ks = [], []
        for hd in range(MLA_HEADS):
            cols = slice(hd * LANES, (hd + 1) * LANES)
            qs.append(_rope_f(_rms_f(q_ref[:, cols], c[0][...], D_QK), rc, sa, sb))
            ks.append(_rope_f(_rms_f(k_ref[:, cols] + kr, c[1][...], D_QK), rc, sa, sb))
        return jnp.concatenate(qs, axis=1), jnp.concatenate(ks, axis=1)

    hw = MLA_HEADS * LANES
    kr_blk = (Q_LORA + KV_LORA) // LANES
    tab_ins = [(rope_c, LANES, 0, "row"), (rope_sa, LANES, 0, "row"), (rope_sb, LANES, 0, "row")]
    q, k = _rowwise("mla_prep_b", prep_b, l, 256,
                    [(q_raw, hw, 0, "row"), (k_raw, hw, 0, "row"), (pqkv, LANES, kr_blk, "row")] + tab_ins,
                    [g_q, g_k], [(hw, BF16), (hw, BF16)])
    o_a, lse_a = _attn_fwd("mla_attn_fwd", q, k, vt_mla, qoff=0, koff=0, voff=0, heads=MLA_HEADS, causal=True,
                           scale=D_QK ** -0.5, bq=1024, bk=1024)
    ya = _mm("mla_o", [(o_a, w["wo_mla"])], bn_cap=1024)
    s.update(nq=nq, nkv=nkv, q_raw=q_raw, k_raw=k_raw, v_mla=v_mla, q=q, k=k, o_a=o_a, lse_a=lse_a, ya=ya)

    lr = p["ssm_lambda_re"].reshape(SSM_N, 1)
    li = p["ssm_lambda_im"].reshape(SSM_N, 1)
    ldt = jnp.repeat(p["ssm_log_dt"], SSM_STATE).reshape(SSM_N, 1)
    br = p["ssm_b_re"].reshape(SSM_N, SSM_GROUP_CH)
    bi = p["ssm_b_im"].reshape(SSM_N, SSM_GROUP_CH)
    a_re, a_im, bb_re, bb_im = _disc_fwd(lr, li, ldt, br, bi)
    bb_re_d, bb_im_d = _block_diag_in(bb_re).astype(BF16), _block_diag_in(bb_im).astype(BF16)
    cc_re_d = _block_diag_out(p["ssm_c_re"]).astype(BF16)
    cc_imn_d = _block_diag_out(-p["ssm_c_im"]).astype(BF16)
    a_re_row, a_im_row = a_re.reshape(1, SSM_N), a_im.reshape(1, SSM_N)
    d_row = _row(p["ssm_d"])
    b_glu = _row(p["b_glu"])
    bu_re = _mm("s5_bu_re", [(u, bb_re_d)], bn_cap=1024)
    bu_im = _mm("s5_bu_im", [(u, bb_im_d)], bn_cap=1024)
    s_re, s_im = _scan("s5_scan_fwd", bu_re, bu_im, a_re_row, a_im_row, reverse=False)
    ypre = _mm("s5_c", [(s_re, cc_re_d), (s_im, cc_imn_d)])

    def ssm_y(i, r, c):
        return (_gelu(r[0][...] + c[0][...] * r[1][...]),)

    (y_b,) = _rowwise("s5_gelu", ssm_y, l, 512, [(ypre, SSM_WIDTH, 0, "row"), (u, SSM_WIDTH, 0, "row")], [d_row],
                      [(SSM_WIDTH, BF16)])
    z = _mm("s5_glu", [(y_b, w["w_glu"])])

    def ssm_out(i, r, c):
        y = _gelu(r[0][...] + c[0][...] * r[1][...])
        return (y * jax.nn.sigmoid(r[2][...] + c[1][...]),)

    (out_b,) = _rowwise("s5_glu_out", ssm_out, l, 512,
                        [(ypre, SSM_WIDTH, 0, "row"), (u, SSM_WIDTH, 0, "row"), (z, SSM_WIDTH, 0, "row")],
                        [d_row, b_glu], [(SSM_WIDTH, BF16)])
    yb = _mm("s5_o", [(out_b, w["w_o_ssm"])], bn_cap=1024)
    s.update(disc=(lr, li, ldt, br, bi), a_rows=(a_re_row, a_im_row), bb_d=(bb_re_d, bb_im_d),
             cc_d=(cc_re_d, cc_imn_d), s_re=s_re, s_im=s_im, ypre=ypre, y_b=y_b, z=z, out_b=out_b, yb=yb)

    g_mem, g_xq, g_xk = _row(p["mem_norm_g"]), _row(p["xq_norm_g"]), _row(p["xk_norm_g"])
    ml = mem.shape[0]
    (memn,) = _rowwise("rms_mem", lambda i, r, c: (_rms_f(r[0][...], c[0][...], D_MODEL),), ml, 256,
                       [(mem, D_MODEL, 0, "row")], [g_mem], [(D_MODEL, BF16)])
    kvm = _mm("cross_kv", [(memn, w["w_mem_kv"])], bn_cap=1024)

    def headnorm(i, r, c):
        return (jnp.concatenate([_rms_f(r[0][:, hd * LANES:(hd + 1) * LANES], c[0][...], X_HEAD_DIM)
                                 for hd in range(X_HEADS)], axis=1),)

    (xk,) = _rowwise("cross_k_norm", headnorm, ml, 256, [(kvm, X_WIDTH, 0, "row")], [g_xk], [(X_WIDTH, BF16)])
    (xqn,) = _rowwise("cross_q_norm", headnorm, l, 512, [(xq, X_WIDTH, 0, "row")], [g_xq], [(X_WIDTH, BF16)])
    xvt = kvm[:, X_WIDTH:].T.astype(BF16)
    o_c, lse_c = _attn_fwd("cross_attn_fwd", xqn, xk, xvt, qoff=0, koff=0, voff=0, heads=X_HEADS,
                           causal=False, scale=X_HEAD_DIM ** -0.5, bq=1024, bk=256)
    yc = _mm("cross_o", [(o_c, w["w_o_cross"])], bn_cap=1024)
    s.update(memn=memn, kvm=kvm, xk=xk, xqn=xqn, o_c=o_c, lse_c=lse_c, yc=yc)

    b_gate = _row(p["b_gate"])

    def merge(i, r, c):
        acc = None
        for br_ in range(3):
            g = jax.nn.sigmoid(r[br_][...] + c[0][:, br_ * D_MODEL:(br_ + 1) * D_MODEL])
            t = g * r[3 + br_][...]
            acc = t if acc is None else acc + t
        return (acc,)

    gate_ins = [(gl, D_MODEL, b_, "row") for b_ in range(3)]
    (merged,) = _rowwise("merge", merge, l, 256,
                         gate_ins + [(ya, D_MODEL, 0, "row"), (yb, D_MODEL, 0, "row"), (yc, D_MODEL, 0, "row")],
                         [b_gate], [(D_MODEL, BF16)])
    x1 = _mm("mix_out", [(merged, w["w_out"])], add=x, bn_cap=1024)
    s.update(merged=merged, x1=x1)

    g_ffn = _row(p["norm_ffn_g"])
    (h2,) = _rowwise("rms_ffn", lambda i, r, c: (_rms_f(r[0][...], c[0][...], D_MODEL),), l, 512,
                     [(x1, D_MODEL, 0, "row")], [g_ffn], [(D_MODEL, BF16)])
    up = _mm("ffn_up", [(h2, w["w_up"])], bm=1024, bn_cap=1408)
    conv_w = w["conv_w"]
    conv_b = _row(p["conv_b"])

    def conv_glu(i, r, c):
        cg = _conv(r[0], r[2], i, c[0], c[1], 0)
        cv = _conv(r[1], r[3], i, c[0], c[1], D_FF)
        return (cg * jax.nn.sigmoid(cg) * cv,)

    up_ins = [(up, D_FF, 0, "row"), (up, D_FF, 1, "row"), (up, D_FF, 0, "prev"), (up, D_FF, 1, "prev")]
    (act,) = _rowwise("ffn_conv_glu", conv_glu, l, 256, up_ins, [conv_w, conv_b], [(D_FF, BF16)])
    x2 = _mm("ffn_down", [(act, w["w_down"])], add=x1, bm=1024, bn_cap=1024)
    s.update(h2=h2, up=up, act=act, conv_w=conv_w, conv_b=conv_b)
    return x2, s


def _conv(x_ref, halo_ref, i, w_ref, b_ref, col0):
    x = x_ref[...]
    cols = slice(col0, col0 + D_FF)
    return (w_ref[0:1, cols] * _shift_down(x, halo_ref, i, 2) + w_ref[1:2, cols] * _shift_down(x, halo_ref, i, 1)
            + w_ref[2:3, cols] * x + b_ref[:, cols])


def _layer_bwd(dx2, s, mem, tabs, w, p):
    l = dx2.shape[0]
    rope_c, rope_sa, rope_sb = tabs
    x, x1 = s["x"], s["x1"]
    g = {}

    dact = _mm("ffn_down_dx", [(dx2, w["w_down"])], trans_b=True, bn_cap=1408)
    g["w_down"] = _mm_tn("ffn_down_dw", s["act"], dx2).reshape(N_DEV, D_FF // N_DEV, D_MODEL)
    up = s["up"]
    nblk_c = l // min(256, l)

    def conv_bwd_a(i, r, c):
        outs, accs = [], []
        for half in range(2):
            x_ref, halo_ref = r[1 + half], r[3 + half]
            col0 = half * D_FF
            cols = slice(col0, col0 + D_FF)
            xv = x_ref[...]
            xm1, xm2 = _shift_down(xv, halo_ref, i, 1), _shift_down(xv, halo_ref, i, 2)
            cv = c[0][0:1, cols] * xm2 + c[0][1:2, cols] * xm1 + c[0][2:3, cols] * xv + c[1][:, cols]
            outs.append((cv, xv, xm1, xm2))
        (cg, xg, xg1, xg2), (cvv, xv, xv1, xv2) = outs
        sig = jax.nn.sigmoid(cg)
        da = r[0][...]
        dcv = da * (cg * sig)
        dcg = da * cvv * (sig * (1.0 + cg * (1.0 - sig)))
        for d, (x0, xa, xb) in ((dcg, (xg, xg1, xg2)), (dcv, (xv, xv1, xv2))):
            accs += [jnp.sum(d, axis=0, keepdims=True), jnp.sum(d * xb, axis=0, keepdims=True),
                     jnp.sum(d * xa, axis=0, keepdims=True), jnp.sum(d * x0, axis=0, keepdims=True)]
        return (dcg, dcv, *accs)

    up_ins = [(up, D_FF, 0, "row"), (up, D_FF, 1, "row"), (up, D_FF, 0, "prev"), (up, D_FF, 1, "prev")]
    res = _rowwise("ffn_conv_glu_bwd", conv_bwd_a, l, 256, [(dact, D_FF, 0, "row")] + up_ins,
                   [s["conv_w"], s["conv_b"]], [(D_FF, F32), (D_FF, F32)], [(1, D_FF)] * 8)
    dcg, dcv = res[0], res[1]
    db_g, dw0_g, dw1_g, dw2_g, db_v, dw0_v, dw1_v, dw2_v = res[2:]
    g["conv_b"] = jnp.concatenate([db_g, db_v], axis=1)[0]
    g["conv_w"] = _cols_to_rows(jnp.concatenate(
        [jnp.concatenate([dw0_g, dw0_v], axis=1), jnp.concatenate([dw1_g, dw1_v], axis=1),
         jnp.concatenate([dw2_g, dw2_v], axis=1)], axis=0))

    def conv_bwd_b(i, r, c):
        outs = []
        for half in range(2):
            d = r[half][...]
            cols = slice(half * D_FF, (half + 1) * D_FF)
            outs.append(c[0][2:3, cols] * d + c[0][1:2, cols] * _shift_up(d, r[2 + half], i, nblk_c, 1)
                        + c[0][0:1, cols] * _shift_up(d, r[2 + half], i, nblk_c, 2))
        return tuple(outs)

    dup_g, dup_v = _rowwise("ffn_conv_bwd_x", conv_bwd_b, l, 256,
                            [(dcg, D_FF, 0, "row"), (dcv, D_FF, 0, "row"), (dcg, D_FF, 0, "next"),
                             (dcv, D_FF, 0, "next")], [s["conv_w"]], [(D_FF, BF16), (D_FF, BF16)])
    dh2 = _mm("ffn_up_dx", [(dup_g, w["w_up_g"]), (dup_v, w["w_up_v"])], trans_b=True, bn_cap=1024)
    g["w_up"] = _pack_w_up(_mm_tn("ffn_up_dw_g", s["h2"], dup_g), _mm_tn("ffn_up_dw_v", s["h2"], dup_v))

    def rms_bwd_res(i, r, c):
        dx, dg = _rms_b(r[0][...], c[0][...], r[1][...], D_MODEL)
        return (dx + r[2][...], dg)

    dx1, g["norm_ffn_g"] = _rowwise(
        "rms_ffn_bwd", rms_bwd_res, l, 512, [(x1, D_MODEL, 0, "row"), (dh2, D_MODEL, 0, "row"), (dx2, D_MODEL, 0, "row")],
        [_row(p["norm_ffn_g"])], [(D_MODEL, F32)], [(1, D_MODEL)])

    dmerged = _mm("mix_out_dx", [(dx1, w["w_out"])], trans_b=True, bn_cap=1024)
    g["w_out"] = _mm_tn("mix_out_dw", s["merged"], dx1).reshape(N_DEV, D_MODEL // N_DEV, D_MODEL)
    gl, ya, yb, yc = s["gl"], s["ya"], s["yb"], s["yc"]

    def merge_bwd(i, r, c):
        dm = r[0][...]
        dys, dgs = [], []
        for b_ in range(3):
            gate = jax.nn.sigmoid(r[1 + b_][...] + c[0][:, b_ * D_MODEL:(b_ + 1) * D_MODEL])
            dys.append(dm * gate)
            dgs.append(dm * r[4 + b_][...] * (gate * (1.0 - gate)))
        dgl = jnp.concatenate(dgs, axis=1)
        return (*dys, dgl, jnp.sum(dgl, axis=0, keepdims=True))

    gate_ins = [(gl, D_MODEL, b_, "row") for b_ in range(3)]
    dya, dyb, dyc, dgl, db_gate = _rowwise(
        "merge_bwd", merge_bwd, l, 256,
        [(dmerged, D_MODEL, 0, "row")] + gate_ins + [(ya, D_MODEL, 0, "row"), (yb, D_MODEL, 0, "row"),
                                                     (yc, D_MODEL, 0, "row")],
        [_row(p["b_gate"])], [(D_MODEL, BF16)] * 3 + [(3 * D_MODEL, BF16)], [(1, 3 * D_MODEL)])
    g["b_gate"] = db_gate[0]

    do_c = _mm("cross_o_dx", [(dyc, w["w_o_cross"])], trans_b=True, out_dtype=BF16)
    g["w_o_cross"] = _cols_to_rows(_mm_tn("cross_o_dw", s["o_c"], dyc))
    kvm = s["kvm"]
    delta_c = _attn_delta("cross_attn_delta", s["o_c"], do_c, heads=X_HEADS, bq=512)
    dxqn, dxk, dxv = _attn_bwd("cross_attn_bwd", s["xqn"], s["xk"], s["xk"].T, kvm, do_c, s["lse_c"], delta_c, qoff=0,
                               koff=0, voff=X_HEADS, heads=X_HEADS, causal=False, scale=X_HEAD_DIM ** -0.5, bq=1024,
                               bk=256)
    ml = mem.shape[0]

    def headnorm_bwd(i, r, c):
        dxs, dg = [], None
        for hd in range(X_HEADS):
            cols = slice(hd * LANES, (hd + 1) * LANES)
            dx_h, dg_h = _rms_b(r[0][:, cols], c[0][...], r[1][:, cols], X_HEAD_DIM)
            dxs.append(dx_h)
            dg = dg_h if dg is None else dg + dg_h
        return (jnp.concatenate(dxs, axis=1), dg)

    dxq, dg_xq = _rowwise("cross_q_norm_bwd", headnorm_bwd, l, 512,
                          [(s["xq"], X_WIDTH, 0, "row"), (dxqn, X_WIDTH, 0, "row")], [_row(p["xq_norm_g"])],
                          [(X_WIDTH, BF16)], [(1, X_HEAD_DIM)])
    dkvm_k, dg_xk = _rowwise("cross_k_norm_bwd", headnorm_bwd, ml, 256,
                             [(kvm, X_WIDTH, 0, "row"), (dxk, X_WIDTH, 0, "row")], [_row(p["xk_norm_g"])],
                             [(X_WIDTH, F32)], [(1, X_HEAD_DIM)])
    g["xq_norm_g"], g["xk_norm_g"] = dg_xq[0], dg_xk[0]
    dkvm = jnp.concatenate([dkvm_k, dxv], axis=1)
    g["w_mem_kv"] = _mm_tn("cross_kv_dw", s["memn"], dkvm).reshape(N_DEV, D_MODEL // N_DEV, 2 * X_WIDTH)
    dmemn = _mm("cross_kv_dx", [(dkvm, w["w_mem_kv"])], trans_b=True, bn_cap=1024)

    def rms_bwd_gain_only(i, r, c):
        return (_rms_b(r[0][...], c[0][...], r[1][...], D_MODEL)[1],)

    (dg_mem,) = _rowwise("rms_mem_bwd", rms_bwd_gain_only, ml, 256,
                         [(mem, D_MODEL, 0, "row"), (dmemn, D_MODEL, 0, "row")], [_row(p["mem_norm_g"])], [],
                         [(1, D_MODEL)])
    g["mem_norm_g"] = dg_mem[0]

    dout_b = _mm("s5_o_dx", [(dyb, w["w_o_ssm"])], trans_b=True)
    g["w_o_ssm"] = _cols_to_rows(_mm_tn("s5_o_dw", s["out_b"], dyb))
    ypre, u, z = s["ypre"], s["u"], s["z"]
    d_row, b_glu = _row(p["ssm_d"]), _row(p["b_glu"])
    yuz = [(ypre, SSM_WIDTH, 0, "row"), (u, SSM_WIDTH, 0, "row"), (z, SSM_WIDTH, 0, "row")]

    def glu_bwd_z(i, r, c):
        y = _gelu(r[1][...] + c[0][...] * r[2][...])
        sg = jax.nn.sigmoid(r[3][...] + c[1][...])
        dz = r[0][...] * y * (sg * (1.0 - sg))
        return (dz, jnp.sum(dz, axis=0, keepdims=True))

    dz, db_glu = _rowwise("s5_glu_bwd_z", glu_bwd_z, l, 512, [(dout_b, SSM_WIDTH, 0, "row")] + yuz, [d_row, b_glu],
                          [(SSM_WIDTH, BF16)], [(1, SSM_WIDTH)])
    g["b_glu"] = db_glu[0]
    g["w_glu"] = _mm_tn("s5_glu_dw", s["y_b"], dz).reshape(N_DEV, SSM_WIDTH // N_DEV, SSM_WIDTH)
    dy2 = _mm("s5_glu_dx", [(dz, w["w_glu"])], trans_b=True)

    def gelu_bwd(i, r, c):
        t = r[2][...] + c[0][...] * r[3][...]
        sg = jax.nn.sigmoid(r[4][...] + c[1][...])
        dt = (r[0][...] * sg + r[1][...]) * _gelu_grad(t)
        return (dt, c[0][...] * dt, jnp.sum(dt * r[3][...], axis=0, keepdims=True))

    dypre, du_skip, dd = _rowwise(
        "s5_gelu_bwd", gelu_bwd, l, 512, [(dout_b, SSM_WIDTH, 0, "row"), (dy2, SSM_WIDTH, 0, "row")] + yuz,
        [d_row, b_glu], [(SSM_WIDTH, BF16), (SSM_WIDTH, F32)], [(1, SSM_WIDTH)])
    g["ssm_d"] = dd.reshape(SSM_GROUPS, SSM_GROUP_CH)
    cc_re_d, cc_imn_d = s["cc_d"]
    bb_re_d, bb_im_d = s["bb_d"]
    a_re_row, a_im_row = s["a_rows"]
    s_re, s_im = s["s_re"], s["s_im"]
    g_re = _mm("s5_c_dx_re", [(dypre, cc_re_d)], trans_b=True, bn_cap=1024)
    g_im = _mm("s5_c_dx_im", [(dypre, cc_imn_d)], trans_b=True, bn_cap=1024)
    dcc_re = _mm_tn("s5_c_dw_re", s_re, dypre)
    dcc_imn = _mm_tn("s5_c_dw_im", s_im, dypre)
    g["ssm_c_re"] = _diag_blocks(dcc_re, SSM_STATE, SSM_GROUP_CH).transpose(0, 2, 1)
    g["ssm_c_im"] = -_diag_blocks(dcc_imn, SSM_STATE, SSM_GROUP_CH).transpose(0, 2, 1)
    lam_re, lam_im = _scan("s5_scan_bwd", g_re, g_im, a_re_row, -a_im_row, reverse=True)

    def da_sum(i, r, c):
        lr_, li_ = r[0][...], r[1][...]
        pr, pi = _shift_down(r[2][...], r[4], i, 1), _shift_down(r[3][...], r[5], i, 1)
        return (jnp.sum(lr_ * pr + li_ * pi, axis=0, keepdims=True), jnp.sum(li_ * pr - lr_ * pi, axis=0, keepdims=True))

    da_re, da_im = _rowwise(
        "s5_da", da_sum, l, 256,
        [(lam_re, SSM_N, 0, "row"), (lam_im, SSM_N, 0, "row"), (s_re, SSM_N, 0, "row"), (s_im, SSM_N, 0, "row"),
         (s_re, SSM_N, 0, "prev"), (s_im, SSM_N, 0, "prev")], [], [], [(1, SSM_N), (1, SSM_N)])
    dbb_re_d = _mm_tn("s5_b_dw_re", u, lam_re)
    dbb_im_d = _mm_tn("s5_b_dw_im", u, lam_im)
    dbb_re = _diag_blocks(dbb_re_d, SSM_GROUP_CH, SSM_STATE).transpose(0, 2, 1).reshape(SSM_N, SSM_GROUP_CH)
    dbb_im = _diag_blocks(dbb_im_d, SSM_GROUP_CH, SSM_STATE).transpose(0, 2, 1).reshape(SSM_N, SSM_GROUP_CH)
    dlr, dli, dldt, dbr, dbi = _disc_bwd(*s["disc"], da_re.reshape(SSM_N, 1), da_im.reshape(SSM_N, 1), dbb_re, dbb_im)
    g["ssm_lambda_re"] = dlr.reshape(SSM_GROUPS, SSM_STATE)
    g["ssm_lambda_im"] = dli.reshape(SSM_GROUPS, SSM_STATE)
    g["ssm_log_dt"] = dldt.reshape(SSM_GROUPS, SSM_STATE).sum(axis=1)
    g["ssm_b_re"] = dbr.reshape(SSM_GROUPS, SSM_STATE, SSM_GROUP_CH)
    g["ssm_b_im"] = dbi.reshape(SSM_GROUPS, SSM_STATE, SSM_GROUP_CH)
    du = _mm("s5_b_dx", [(lam_re, bb_re_d), (lam_im, bb_im_d)], trans_b=True, add=du_skip, out_dtype=BF16)

    do_a = _mm("mla_o_dx", [(dya, w["wo_mla"])], trans_b=True, out_dtype=BF16, bn_cap=1024)
    dwo = _mm_tn("mla_o_dw", s["o_a"], dya)
    g["w_o_mla"] = _cols_to_rows(dwo.reshape(MLA_HEADS, LANES, D_MODEL)[:, :D_V].reshape(MLA_HEADS * D_V, D_MODEL))
    delta_a = _attn_delta("mla_attn_delta", s["o_a"], do_a, heads=MLA_HEADS, bq=512)
    dq, dk, dv = _attn_bwd("mla_attn_bwd", s["q"], s["k"], s["k"].T, s["v_mla"], do_a, s["lse_a"], delta_a, qoff=0,
                           koff=0, voff=0, heads=MLA_HEADS, causal=True, scale=D_QK ** -0.5, bq=1024, bk=1024)
    hw = MLA_HEADS * LANES
    kr_blk = (Q_LORA + KV_LORA) // LANES
    pqkv = s["pqkv"]
    g_q, g_k = _pad_lanes(p["q_norm_g"]), _pad_lanes(p["k_norm_g"])
    lane = lax.broadcasted_iota(jnp.int32, (1, LANES), 1)
    kr_mask = jnp.logical_and(lane >= KR_LO, lane < KR_LO + D_ROPE).astype(F32)

    def prep_b_bwd(i, r, c):
        dq_ref, dk_ref, q_ref, k_ref, kr_ref, c_ref, sa_ref, sb_ref = r
        rc, sa, sb, kr = c_ref[...], sa_ref[...], sb_ref[...], kr_ref[...]
        dqs, dks, dkr, dgq, dgk = [], [], None, None, None
        for hd in range(MLA_HEADS):
            cols = slice(hd * LANES, (hd + 1) * LANES)
            dxq, dgq_h = _rms_b(q_ref[:, cols], c[0][...], _rope_b(dq_ref[:, cols], rc, sa, sb), D_QK)
            dxk, dgk_h = _rms_b(k_ref[:, cols] + kr, c[1][...], _rope_b(dk_ref[:, cols], rc, sa, sb), D_QK)
            dqs.append(dxq)
            dks.append(dxk)
            dkr = dxk if dkr is None else dkr + dxk
            dgq = dgq_h if dgq is None else dgq + dgq_h
            dgk = dgk_h if dgk is None else dgk + dgk_h
        return (jnp.concatenate(dqs, axis=1), jnp.concatenate(dks, axis=1), dkr * c[2][...], dgq, dgk)

    tab_ins = [(rope_c, LANES, 0, "row"), (rope_sa, LANES, 0, "row"), (rope_sb, LANES, 0, "row")]
    dq_raw, dk_raw, dkr, dg_q, dg_k = _rowwise(
        "mla_prep_b_bwd", prep_b_bwd, l, 256,
        [(dq, hw, 0, "row"), (dk, hw, 0, "row"), (s["q_raw"], hw, 0, "row"), (s["k_raw"], hw, 0, "row"),
         (pqkv, LANES, kr_blk, "row")] + tab_ins, [g_q, g_k, kr_mask],
        [(hw, BF16), (hw, BF16), (LANES, F32)], [(1, LANES), (1, LANES)])
    g["q_norm_g"], g["k_norm_g"] = dg_q[0, :D_QK], dg_k[0, :D_QK]
    dnq = _mm("mla_q_b_dx", [(dq_raw, w["wqb"])], trans_b=True)
    dnkv = _mm("mla_kv_b_dx", [(dk_raw, w["wk"]), (dv, w["wv"])], trans_b=True)
    dwqb = _mm_tn("mla_q_b_dw", s["nq"], dq_raw)
    g["w_q_b"] = _cols_to_rows(dwqb)
    dwk = _mm_tn("mla_k_b_dw", s["nkv"], dk_raw).reshape(KV_LORA, MLA_HEADS, LANES)[:, :, :D_NOPE]
    dwv = _mm_tn("mla_v_b_dw", s["nkv"], dv).reshape(KV_LORA, MLA_HEADS, LANES)[:, :, :D_V]
    g["w_kv_b"] = jnp.concatenate([dwk, dwv], axis=2).transpose(1, 0, 2)

    def prep_a_bwd(i, r, c):
        dcq, dgqa = _rms_b(r[0][:, :Q_LORA], c[0][...], r[1][...], Q_LORA)
        dckv, dgkva = _rms_b(r[0][:, Q_LORA:Q_LORA + KV_LORA], c[1][...], r[2][...], KV_LORA)
        return (jnp.concatenate([dcq, dckv, r[3][...]], axis=1), dgqa, dgkva)

    dpqkv, dg_qa, dg_kva = _rowwise(
        "mla_prep_a_bwd", prep_a_bwd, l, 512,
        [(pqkv, QKV_W, 0, "row"), (dnq, Q_LORA, 0, "row"), (dnkv, KV_LORA, 0, "row"), (dkr, LANES, 0, "row")],
        [_row(p["q_a_norm_g"]), _row(p["kv_a_norm_g"])], [(QKV_W, BF16)], [(1, Q_LORA), (1, KV_LORA)])
    g["q_a_norm_g"], g["kv_a_norm_g"] = dg_qa[0], dg_kva[0]

    h = s["h"]
    dh = _mm("proj_dx", [(dpqkv, w["wqkv"]), (du, w["w_u"]), (dxq, w["w_xq"]), (dgl, w["w_g"])], trans_b=True,
             bn_cap=1024)
    dwqkv = _mm_tn("proj_qkv_dw", h, dpqkv)
    g["w_in"] = _pack_w_in([dwqkv, _mm_tn("proj_u_dw", h, du), _mm_tn("proj_xq_dw", h, dxq),
                            _mm_tn("proj_gate_dw", h, dgl)])
    dx, dg_mix = _rowwise(
        "rms_mix_bwd", rms_bwd_res, l, 512, [(x, D_MODEL, 0, "row"), (dh, D_MODEL, 0, "row"), (dx1, D_MODEL, 0, "row")],
        [_row(p["norm_mix_g"])], [(D_MODEL, F32)], [(1, D_MODEL)])
    g["norm_mix_g"] = dg_mix[0]
    g["norm_ffn_g"] = g["norm_ffn_g"][0]
    return dx, g


def _rope_tables(positions):
    inv_freq = ROPE_THETA ** (-jnp.arange(0, D_ROPE, 2, dtype=F32) / D_ROPE)
    ang = positions.astype(F32)[:, None] * inv_freq
    cos, sin = jnp.cos(ang), jnp.sin(ang)
    l = positions.shape[0]
    one, zero = jnp.ones((l, D_NOPE), F32), lambda n: jnp.zeros((l, n), F32)
    pad = LANES - D_QK
    rope_c = jnp.concatenate([one, cos, cos, zero(pad)], axis=1)
    rope_sa = jnp.concatenate([zero(D_NOPE), -sin, zero(16), zero(pad)], axis=1)
    rope_sb = jnp.concatenate([zero(D_NOPE + 16), sin, zero(pad)], axis=1)
    return rope_c, rope_sa, rope_sb


def kernel(x, mem, positions, norm_mix_g, w_in, q_a_norm_g, w_q_b, kv_a_norm_g, w_kv_b, q_norm_g, k_norm_g, w_o_mla, ssm_lambda_re, ssm_lambda_im, ssm_log_dt, ssm_b_re, ssm_b_im, ssm_c_re, ssm_c_im, ssm_d, w_glu, b_glu, w_o_ssm, mem_norm_g, w_mem_kv, xq_norm_g, xk_norm_g, w_o_cross, b_gate, w_out, norm_ffn_g, w_up, conv_w, conv_b, w_down, loss_target, m_norm_mix_g, m_w_in, m_q_a_norm_g, m_w_q_b, m_kv_a_norm_g, m_w_kv_b, m_q_norm_g, m_k_norm_g, m_w_o_mla, m_ssm_lambda_re, m_ssm_lambda_im, m_ssm_log_dt, m_ssm_b_re, m_ssm_b_im, m_ssm_c_re, m_ssm_c_im, m_ssm_d, m_w_glu, m_b_glu, m_w_o_ssm, m_mem_norm_g, m_w_mem_kv, m_xq_norm_g, m_xk_norm_g, m_w_o_cross, m_b_gate, m_w_out, m_norm_ffn_g, m_w_up, m_conv_w, m_conv_b, m_w_down, v_norm_mix_g, v_w_in, v_q_a_norm_g, v_w_q_b, v_kv_a_norm_g, v_w_kv_b, v_q_norm_g, v_k_norm_g, v_w_o_mla, v_ssm_lambda_re, v_ssm_lambda_im, v_ssm_log_dt, v_ssm_b_re, v_ssm_b_im, v_ssm_c_re, v_ssm_c_im, v_ssm_d, v_w_glu, v_b_glu, v_w_o_ssm, v_mem_norm_g, v_w_mem_kv, v_xq_norm_g, v_xk_norm_g, v_w_o_cross, v_b_gate, v_w_out, v_norm_ffn_g, v_w_up, v_conv_w, v_conv_b, v_w_down):
    a = dict(locals())
    wts = {n: a[n] for n in WEIGHT_ORDER}
    m_in = {n: a["m_" + n] for n in WEIGHT_ORDER}
    v_in = {n: a["v_" + n] for n in WEIGHT_ORDER}
    depth = norm_mix_g.shape[0]
    x0, mem0, pos0, tgt = x[0], mem[0], positions[0], loss_target[0]
    l = x0.shape[0]
    offs = {grp: _group_rows(params, depth)[0] for grp, _, params in GROUPS}

    srcs = [_group_local(width, params, wts, depth, F32 if grp == "gconv" else BF16) for grp, width, params in GROUPS]
    gath = dict(zip([grp for grp, _, _ in GROUPS], _gather_all("gather_weights", srcs)))

    tabs = _rope_tables(pos0)
    layer_w = [_layer_weights(gath, offs, i) for i in range(depth)]
    layer_p = [{n: wts[n][i] for n in REPLICATED} for i in range(depth)]

    saved = []
    xc = x0
    for i in range(depth):
        xc, s = _layer_fwd(xc, mem0, tabs, layer_w[i], layer_p[i])
        saved.append(s)

    def loss_fn(i, r, c):
        d = r[0][...] - r[1][...]
        return (d * (1.0 / D_MODEL), jnp.sum(d * d, axis=0, keepdims=True))

    dy, sq = _rowwise("loss", loss_fn, l, 512, [(xc, D_MODEL, 0, "row"), (tgt, D_MODEL, 0, "row")], [],
                      [(D_MODEL, F32)], [(1, D_MODEL)])
    loss = lax.psum(0.5 * jnp.sum(sq) / D_MODEL, ("x", "y", "c"))

    grads = [None] * depth
    dxc = dy
    for i in reversed(range(depth)):
        dxc, grads[i] = _layer_bwd(dxc, saved[i], mem0, tabs, layer_w[i], layer_p[i])
    grad_x = dxc[None]

    sends = []
    for grp, width, params in GROUPS:
        blocks = []
        for n, rows, _ in params:
            for i in range(depth):
                gb = grads[i][n]
                blocks.append(jnp.pad(gb, ((0, 0), (0, _rows8(rows) - rows), (0, width - gb.shape[2]))))
        sends.append(jnp.concatenate(blocks, axis=1))
    rep_flat = jnp.concatenate([jnp.stack([grads[i][n] for i in range(depth)]).reshape(-1) for n in REPLICATED])
    sends.append(_rep_rows(jnp.broadcast_to(rep_flat[None], (N_DEV, rep_flat.shape[0]))))
    names = [grp for grp, _, _ in GROUPS] + ["rep"]
    got = _pair_exchange("exchange_grads_core", sends)
    parts = [_pair_sum("grad_pair_sum_" + nm, s_, g_) for nm, s_, g_ in zip(names, sends, got)]
    rcvs = dict(zip(names, _chip_exchange("exchange_grads_chip", parts)))

    outs = {}
    for grp, width, params in GROUPS:
        local = [_group_local(width, params, d, depth, F32) for d in (wts, m_in, v_in)]
        res = _adamw("adamw_" + grp, rcvs[grp], *local)
        for tag, arr in zip(("grad", "delta", "m", "v"), res):
            for n, rows, cols in params:
                outs[(tag, n)] = jnp.stack([arr[offs[grp][(n, i)]:offs[grp][(n, i)] + rows, :cols]
                                            for i in range(depth)])
    rep_local = [_rep_rows(jnp.concatenate([d[n].astype(F32).reshape(-1) for n in REPLICATED])) for d in (wts, m_in, v_in)]
    res = _adamw("adamw_rep", rcvs["rep"], *rep_local)
    for tag, arr in zip(("grad", "delta", "m", "v"), res):
        flat, off = arr.reshape(-1), 0
        for n in REPLICATED:
            cnt = wts[n].size
            outs[(tag, n)] = flat[off:off + cnt].reshape(wts[n].shape)
            off += cnt
    result = [loss, grad_x]
    for tag in ("grad", "delta", "m", "v"):
        result += [outs[(tag, n)] for n in WEIGHT_ORDER]
    return tuple(result)
```

```python
import math

import jax
import jax.numpy as jnp
from jax import lax
from jax.experimental import pallas as pl
from jax.experimental.pallas import tpu as pltpu

F32 = jnp.float32
BF16 = jnp.bfloat16

N_DEV = 8
LANES = 128
LOG2E = math.log2(math.e)
VMEM_LIMIT_BYTES = 56 * 1024 * 1024

D_MODEL = 1024
EPS = 1e-6
MLA_HEADS = 8
Q_LORA = 384
KV_LORA = 256
D_NOPE = 64
D_ROPE = 32
D_QK = D_NOPE + D_ROPE
D_V = 64
ROPE_THETA = 10000.0
SSM_GROUPS = 32
SSM_GROUP_CH = 16
SSM_WIDTH = 512
SSM_STATE = 64
SSM_N = SSM_GROUPS * SSM_STATE
X_HEADS = 4
X_HEAD_DIM = 128
X_WIDTH = 512
D_FF = 2816
IN_WIDTH = Q_LORA + KV_LORA + D_ROPE + SSM_WIDTH + X_WIDTH + 3 * D_MODEL
QKV_W = Q_LORA + KV_LORA + LANES
KR_LO = D_NOPE

ADAM_LR = 0.001
ADAM_B1 = 0.9
ADAM_B2 = 0.999
ADAM_EPS = 1e-08
ADAM_WD = 0.01
ADAM_STEP = 10

REPLICATED = (
    "norm_mix_g", "q_a_norm_g", "kv_a_norm_g", "q_norm_g", "k_norm_g", "ssm_lambda_re", "ssm_lambda_im",
    "ssm_log_dt", "ssm_b_re", "ssm_b_im", "ssm_c_re", "ssm_c_im", "ssm_d", "b_glu", "mem_norm_g",
    "xq_norm_g", "xk_norm_g", "b_gate", "norm_ffn_g", "conv_b",
)
WEIGHT_ORDER = (
    "norm_mix_g", "w_in", "q_a_norm_g", "w_q_b", "kv_a_norm_g", "w_kv_b", "q_norm_g", "k_norm_g", "w_o_mla",
    "ssm_lambda_re", "ssm_lambda_im", "ssm_log_dt", "ssm_b_re", "ssm_b_im", "ssm_c_re", "ssm_c_im", "ssm_d",
    "w_glu", "b_glu", "w_o_ssm", "mem_norm_g", "w_mem_kv", "xq_norm_g", "xk_norm_g", "w_o_cross", "b_gate",
    "w_out", "norm_ffn_g", "w_up", "conv_w", "conv_b", "w_down",
)


def _params(**kw):
    return pltpu.CompilerParams(vmem_limit_bytes=VMEM_LIMIT_BYTES, **kw)


def _pick(n, cap):
    if n <= cap:
        return n
    best = None
    for m in range(LANES, cap + 1, LANES):
        if n % m == 0:
            best = m
    assert best is not None, n
    return best


_NN = (((1,), (0,)), ((), ()))
_NT = (((1,), (1,)), ((), ()))
_TN = (((0,), (0,)), ((), ()))


def _dot(a, b, dn):
    return lax.dot_general(a.astype(BF16), b.astype(BF16), dn, preferred_element_type=F32)


def _mm(name, pairs, *, trans_b=False, add=None, out_dtype=F32, bm=512, bn_cap=512):
    m = pairs[0][0].shape[0]
    n = pairs[0][1].shape[0 if trans_b else 1]
    bm = min(bm, m)
    bn = _pick(n, bn_cap)
    npair = len(pairs)

    def body(*refs):
        o_ref = refs[-1]
        acc = None
        for p in range(npair):
            d = _dot(refs[2 * p][...], refs[2 * p + 1][...], _NT if trans_b else _NN)
            acc = d if acc is None else acc + d
        if add is not None:
            acc = acc + refs[2 * npair][...]
        o_ref[...] = acc.astype(out_dtype)

    in_specs, args = [], []
    for a, b in pairs:
        k = a.shape[1]
        in_specs.append(pl.BlockSpec((bm, k), lambda i, j: (i, 0)))
        if trans_b:
            in_specs.append(pl.BlockSpec((bn, k), lambda i, j: (j, 0)))
        else:
            in_specs.append(pl.BlockSpec((k, bn), lambda i, j: (0, j)))
        args += [a, b]
    if add is not None:
        in_specs.append(pl.BlockSpec((bm, bn), lambda i, j: (i, j)))
        args.append(add)
    return pl.pallas_call(
        body, name=name, grid=(m // bm, n // bn), in_specs=in_specs,
        out_specs=pl.BlockSpec((bm, bn), lambda i, j: (i, j)),
        out_shape=jax.ShapeDtypeStruct((m, n), out_dtype), compiler_params=_params(),
    )(*args)


def _mm_tn(name, a, b, *, bm_cap=512, bn_cap=1536, bk=1024):
    l, m = a.shape
    n = b.shape[1]
    bm, bn, bk = _pick(m, bm_cap), _pick(n, bn_cap), min(bk, l)

    def body(a_ref, b_ref, o_ref):
        @pl.when(pl.program_id(2) == 0)
        def _():
            o_ref[...] = jnp.zeros_like(o_ref)

        o_ref[...] += _dot(a_ref[...], b_ref[...], _TN)

    return pl.pallas_call(
        body, name=name, grid=(m // bm, n // bn, l // bk),
        in_specs=[pl.BlockSpec((bk, bm), lambda i, j, k: (k, i)), pl.BlockSpec((bk, bn), lambda i, j, k: (k, j))],
        out_specs=pl.BlockSpec((bm, bn), lambda i, j, k: (i, j)),
        out_shape=jax.ShapeDtypeStruct((m, n), F32), compiler_params=_params(),
    )(a, b)


BD_BLOCKS = 4


def _mm_bd(name, pairs, *, trans_b=False, add=None, out_dtype=F32, bm=512):
    m = pairs[0][0].shape[0]
    n = pairs[0][1].shape[0 if trans_b else 1]
    bm, bn = min(bm, m), n // BD_BLOCKS
    npair = len(pairs)

    def body(*refs):
        o_ref = refs[-1]
        acc = None
        for p in range(npair):
            d = _dot(refs[2 * p][...], refs[2 * p + 1][...], _NT if trans_b else _NN)
            acc = d if acc is None else acc + d
        if add is not None:
            acc = acc + refs[2 * npair][...]
        o_ref[...] = acc.astype(out_dtype)

    in_specs, args = [], []
    for a, b in pairs:
        kb = a.shape[1] // BD_BLOCKS
        in_specs.append(pl.BlockSpec((bm, kb), lambda i, j: (i, j)))
        in_specs.append(pl.BlockSpec((bn, kb) if trans_b else (kb, bn), lambda i, j: (j, j)))
        args += [a, b]
    if add is not None:
        in_specs.append(pl.BlockSpec((bm, bn), lambda i, j: (i, j)))
        args.append(add)
    return pl.pallas_call(
        body, name=name, grid=(m // bm, BD_BLOCKS), in_specs=in_specs,
        out_specs=pl.BlockSpec((bm, bn), lambda i, j: (i, j)),
        out_shape=jax.ShapeDtypeStruct((m, n), out_dtype), compiler_params=_params(),
    )(*args)


def _mm_tn_bd(name, a, b, *, bk=1024):
    l, m = a.shape
    n = b.shape[1]
    mb, nb, bk = m // BD_BLOCKS, n // BD_BLOCKS, min(bk, l)

    def body(a_ref, b_ref, o_ref):
        @pl.when(pl.program_id(1) == 0)
        def _():
            o_ref[...] = jnp.zeros_like(o_ref)

        o_ref[0] += _dot(a_ref[...], b_ref[...], _TN)

    return pl.pallas_call(
        body, name=name, grid=(BD_BLOCKS, l // bk),
        in_specs=[pl.BlockSpec((bk, mb), lambda j, k: (k, j)), pl.BlockSpec((bk, nb), lambda j, k: (k, j))],
        out_specs=pl.BlockSpec((1, mb, nb), lambda j, k: (j, 0, 0)),
        out_shape=jax.ShapeDtypeStruct((BD_BLOCKS, mb, nb), F32), compiler_params=_params(),
    )(a, b)


def _rowwise(name, fn, nrows, bm, row_ins, consts, row_outs, acc_outs=()):
    bm = min(bm, nrows)
    nblk = nrows // bm
    sub = bm // 8
    nin, nc, nro = len(row_ins), len(consts), len(row_outs)

    def body(*refs):
        i = pl.program_id(0)
        outs = fn(i, refs[:nin], refs[nin:nin + nc])
        o_refs = refs[nin + nc:nin + nc + nro]
        a_refs = refs[nin + nc + nro:]
        for r, v in zip(o_refs, outs[:nro]):
            r[...] = v.astype(r.dtype)
        if a_refs:
            @pl.when(i == 0)
            def _():
                for r in a_refs:
                    r[...] = jnp.zeros_like(r)

            for r, v in zip(a_refs, outs[nro:]):
                r[...] += v

    in_specs, args = [], []
    for arr, w, cb, kind in row_ins:
        if kind == "row":
            in_specs.append(pl.BlockSpec((bm, w), lambda i, cb=cb: (i, cb)))
        elif kind == "prev":
            in_specs.append(pl.BlockSpec((8, w), lambda i, cb=cb: (jnp.maximum(i * sub - 1, 0), cb)))
        else:
            in_specs.append(pl.BlockSpec((8, w), lambda i, cb=cb: (jnp.minimum((i + 1) * sub, nrows // 8 - 1), cb)))
        args.append(arr)
    for c in consts:
        in_specs.append(pl.BlockSpec(c.shape, lambda i: (0, 0)))
        args.append(c)
    out_specs = [pl.BlockSpec((bm, w), lambda i: (i, 0)) for w, _ in row_outs]
    out_specs += [pl.BlockSpec(s, lambda i: (0, 0)) for s in acc_outs]
    out_shape = [jax.ShapeDtypeStruct((nrows, w), dt) for w, dt in row_outs]
    out_shape += [jax.ShapeDtypeStruct(s, F32) for s in acc_outs]
    res = pl.pallas_call(
        body, name=name, grid=(nblk,), in_specs=in_specs, out_specs=out_specs, out_shape=out_shape,
        compiler_params=_params(),
    )(*args)
    return res


def _rms_f(x, g, n):
    r = lax.rsqrt(jnp.sum(x * x, axis=-1, keepdims=True) * (1.0 / n) + EPS)
    return x * r * g


def _rms_b(x, g, dy, n):
    r = lax.rsqrt(jnp.sum(x * x, axis=-1, keepdims=True) * (1.0 / n) + EPS)
    gx = dy * g
    dx = r * gx - x * (r * r * r * (jnp.sum(x * gx, axis=-1, keepdims=True) * (1.0 / n)))
    dg = jnp.sum(dy * (x * r), axis=0, keepdims=True)
    return dx, dg


def _rope_f(x, c, sa, sb):
    return x * c + pltpu.roll(x, LANES - 16, 1) * sa + pltpu.roll(x, 16, 1) * sb


def _rope_b(g, c, sa, sb):
    return g * c + pltpu.roll(g * sa, 16, 1) + pltpu.roll(g * sb, LANES - 16, 1)


def _gelu(x):
    c = math.sqrt(2.0 / math.pi)
    return 0.5 * x * (1.0 + jnp.tanh(c * (x + 0.044715 * (x * x * x))))


def _gelu_grad(x):
    c = math.sqrt(2.0 / math.pi)
    th = jnp.tanh(c * (x + 0.044715 * (x * x * x)))
    return 0.5 * (1.0 + th) + 0.5 * x * (1.0 - th * th) * (c * (1.0 + 3.0 * 0.044715 * (x * x)))


def _row_ids(bm):
    return lax.broadcasted_iota(jnp.int32, (bm, 1), 0)


def _shift_down(x, halo_ref, i, k):
    live = (i > 0).astype(F32)
    out = pltpu.roll(x, k, 0)
    row = _row_ids(8)
    first = out[:8]
    for r in range(k):
        e = (row == r).astype(F32)
        first = first * (1.0 - e) + e * (halo_ref[8 - k + r:8 - k + r + 1, :] * live)
    return jnp.concatenate([first, out[8:]], axis=0)


def _shift_up(x, halo_ref, i, nblk, k):
    bm = x.shape[0]
    live = (i < nblk - 1).astype(F32)
    out = pltpu.roll(x, bm - k, 0)
    row = _row_ids(8)
    last = out[bm - 8:]
    for r in range(k):
        e = (row == 8 - k + r).astype(F32)
        last = last * (1.0 - e) + e * (halo_ref[r:r + 1, :] * live)
    return jnp.concatenate([out[:bm - 8], last], axis=0)


def _live_pairs(nq, nk, bq, bk, causal, key_major):
    pairs = [(i, j) for i in range(nq) for j in range(nk) if not causal or j * bk <= i * bq + bq - 1]
    if key_major:
        pairs.sort(key=lambda ij: (ij[1], ij[0]))
    return (jnp.asarray([p[0] for p in pairs], jnp.int32), jnp.asarray([p[1] for p in pairs], jnp.int32))


def _attn_fwd(name, qa, ka, vta, *, qoff, koff, voff, heads, causal, scale, bq, bk):
    lq, lk = qa.shape[0], ka.shape[0]
    bq, bk = min(bq, lq), min(bk, lk)
    nq, nk = lq // bq, lk // bk
    c2 = scale * LOG2E
    tab_i, tab_j = _live_pairs(nq, nk, bq, bk, causal, key_major=False)

    def body(ti, tj, q_ref, k_ref, vt_ref, o_ref, lse_ref, m_s, l_s, acc_s):
        t = pl.program_id(1)
        i, j = ti[t], tj[t]
        j_last = jnp.minimum(nk - 1, (i * bq + bq - 1) // bk) if causal else nk - 1

        @pl.when(j == 0)
        def _():
            m_s[...] = jnp.full_like(m_s, -1e30)
            l_s[...] = jnp.zeros_like(l_s)
            acc_s[...] = jnp.zeros_like(acc_s)

        def step(masked):
            st = _dot(k_ref[...], q_ref[...], _NT) * c2
            if masked:
                key = j * bk + lax.broadcasted_iota(jnp.int32, (bk, bq), 0)
                qry = i * bq + lax.broadcasted_iota(jnp.int32, (bk, bq), 1)
                st = jnp.where(key <= qry, st, -1e30)
            m_prev = m_s[...]
            m_new = jnp.maximum(m_prev, jnp.max(st, axis=0, keepdims=True))
            alpha = jnp.exp2(m_prev - m_new)
            pt = jnp.exp2(st - m_new)
            l_s[...] = alpha * l_s[...] + jnp.sum(pt, axis=0, keepdims=True)
            acc_s[...] = alpha * acc_s[...] + _dot(vt_ref[...], pt, _NN)
            m_s[...] = m_new

        if causal:
            full = j * bk + bk - 1 <= i * bq
            pl.when(full)(lambda: step(False))
            pl.when(jnp.logical_not(full))(lambda: step(True))
        else:
            step(False)

        @pl.when(j == j_last)
        def _():
            l = l_s[...]
            o_ref[...] = (acc_s[...] / l).T.astype(o_ref.dtype)
            lse_ref[0] = m_s[...] + jnp.log2(l)

    grid_spec = pltpu.PrefetchScalarGridSpec(
        num_scalar_prefetch=2, grid=(heads, int(tab_i.shape[0])),
        in_specs=[pl.BlockSpec((bq, LANES), lambda h, t, ti, tj: (ti[t], qoff + h)),
                  pl.BlockSpec((bk, LANES), lambda h, t, ti, tj: (tj[t], koff + h)),
                  pl.BlockSpec((LANES, bk), lambda h, t, ti, tj: (voff + h, tj[t]))],
        out_specs=[pl.BlockSpec((bq, LANES), lambda h, t, ti, tj: (ti[t], h)),
                   pl.BlockSpec((1, 1, bq), lambda h, t, ti, tj: (h, 0, ti[t]))],
        scratch_shapes=[pltpu.VMEM((1, bq), F32), pltpu.VMEM((1, bq), F32), pltpu.VMEM((LANES, bq), F32)])
    return pl.pallas_call(
        body, name=name, grid_spec=grid_spec,
        out_shape=[jax.ShapeDtypeStruct((lq, heads * LANES), BF16), jax.ShapeDtypeStruct((heads, 1, lq), F32)],
        compiler_params=_params(),
    )(tab_i, tab_j, qa, ka, vta)


def _attn_delta(name, oa, doa, *, heads, bq):
    lq = oa.shape[0]
    bq = min(bq, lq)

    def body(o_ref, do_ref, d_ref):
        prod = o_ref[...].astype(F32) * do_ref[...].astype(F32)
        hi = prod.astype(BF16)
        lo = (prod - hi.astype(F32)).astype(BF16)
        pick = (lax.broadcasted_iota(jnp.int32, (8, LANES), 0) == 0).astype(BF16)
        sums = _dot(pick, hi, _NT) + _dot(pick, lo, _NT)
        d_ref[0] = jnp.sum(sums, axis=0, keepdims=True)

    blk = pl.BlockSpec((bq, LANES), lambda h, i: (i, h))
    return pl.pallas_call(
        body, name=name, grid=(heads, lq // bq), in_specs=[blk, blk],
        out_specs=pl.BlockSpec((1, 1, bq), lambda h, i: (h, 0, i)),
        out_shape=jax.ShapeDtypeStruct((heads, 1, lq), F32), compiler_params=_params(),
    )(oa, doa)


def _attn_bwd(name, qa, ka, kta, va, doa, lsea, deltaa, *, qoff, koff, voff, heads, causal, scale, bq, bk):
    lq, lk = qa.shape[0], ka.shape[0]
    bq, bk = min(bq, lq), min(bk, lk)
    nq, nk = lq // bq, lk // bk
    c2 = scale * LOG2E
    tab_i, tab_j = _live_pairs(nq, nk, bq, bk, causal, key_major=True)

    def body(ti, tj, q_ref, k_ref, kt_ref, v_ref, do_ref, lse_ref, delta_ref, dqt_ref, dk_ref, dv_ref):
        t = pl.program_id(1)
        i, j = ti[t], tj[t]
        i_first = (j * bk) // bq if causal else 0

        @pl.when(t == 0)
        def _():
            dqt_ref[...] = jnp.zeros_like(dqt_ref)

        @pl.when(i == i_first)
        def _():
            dk_ref[...] = jnp.zeros_like(dk_ref)
            dv_ref[...] = jnp.zeros_like(dv_ref)

        def step(masked):
            q, k, v, do = q_ref[...], k_ref[...], v_ref[...], do_ref[...]
            st = _dot(k, q, _NT) * c2
            if masked:
                key = j * bk + lax.broadcasted_iota(jnp.int32, (bk, bq), 0)
                qry = i * bq + lax.broadcasted_iota(jnp.int32, (bk, bq), 1)
                st = jnp.where(key <= qry, st, -1e30)
            pt = jnp.exp2(st - lse_ref[0])
            dv_ref[...] += _dot(pt, do, _NN)
            dpt = _dot(v, do, _NT)
            dst = (pt * (dpt - delta_ref[0]) * scale).astype(BF16)
            dk_ref[...] += _dot(dst, q, _NN)
            dqt_ref[0, i] += _dot(kt_ref[...], dst, _NN)

        if causal:
            full = j * bk + bk - 1 <= i * bq
            pl.when(full)(lambda: step(False))
            pl.when(jnp.logical_not(full))(lambda: step(True))
        else:
            step(False)

    q_spec = lambda off: pl.BlockSpec((bq, LANES), lambda h, t, ti, tj: (ti[t], off + h))
    kv_spec = lambda off: pl.BlockSpec((bk, LANES), lambda h, t, ti, tj: (tj[t], off + h))
    row_spec = pl.BlockSpec((1, 1, bq), lambda h, t, ti, tj: (h, 0, ti[t]))
    grid_spec = pltpu.PrefetchScalarGridSpec(
        num_scalar_prefetch=2, grid=(heads, int(tab_i.shape[0])),
        in_specs=[q_spec(qoff), kv_spec(koff), pl.BlockSpec((LANES, bk), lambda h, t, ti, tj: (koff + h, tj[t])),
                  kv_spec(voff), q_spec(0), row_spec, row_spec],
        out_specs=[pl.BlockSpec((1, nq, LANES, bq), lambda h, t, ti, tj: (h, 0, 0, 0)),
                   pl.BlockSpec((bk, LANES), lambda h, t, ti, tj: (tj[t], h)),
                   pl.BlockSpec((bk, LANES), lambda h, t, ti, tj: (tj[t], h))])
    dqt, dk, dv = pl.pallas_call(
        body, name=name, grid_spec=grid_spec,
        out_shape=[jax.ShapeDtypeStruct((heads, nq, LANES, bq), F32), jax.ShapeDtypeStruct((lk, heads * LANES), F32),
                   jax.ShapeDtypeStruct((lk, heads * LANES), F32)],
        compiler_params=_params(),
    )(tab_i, tab_j, qa, ka, kta, va, doa, lsea, deltaa)
    return dqt.transpose(1, 3, 0, 2).reshape(lq, heads * LANES), dk, dv


def _scan(name, bre, bim, are, aim, *, reverse, chunk=512, strip=512, unroll=4):
    l, n = bre.shape
    t = min(chunk, l)
    nc, ns = l // t, n // strip
    ng = t // 8
    edge = 0 if reverse else 7

    def cmap(w, c):
        return ((nc - 1 - c) if reverse else c, w)

    def grp(g):
        return pl.ds(pl.multiple_of(g * 8, 8), 8)

    def body(bre_ref, bim_ref, are_ref, aim_ref, sre_ref, sim_ref, cre_s, cim_s):
        c = pl.program_id(1)
        row8 = lax.broadcasted_iota(jnp.int32, (8, 1), 0)
        ar, ai = are_ref[...], aim_ref[...]

        def group_scan(xr, xi):
            pr, pi = ar, ai
            for d in (1, 2, 4):
                own = ((row8 < 8 - d) if reverse else (row8 >= d)).astype(F32)
                shift = (8 - d) if reverse else d
                sr, si = pltpu.roll(xr, shift, 0) * own, pltpu.roll(xi, shift, 0) * own
                xr, xi = xr + pr * sr - pi * si, xi + pr * si + pi * sr
                pr, pi = pr * pr - pi * pi, 2.0 * pr * pi
            return xr, xi

        @pl.when(c == 0)
        def _():
            cre_s[...] = jnp.zeros_like(cre_s)
            cim_s[...] = jnp.zeros_like(cim_s)

        def local(g, _):
            xr, xi = group_scan(bre_ref[grp(g), :], bim_ref[grp(g), :])
            sre_ref[grp(g), :] = xr
            sim_ref[grp(g), :] = xi
            return 0

        lax.fori_loop(0, ng, local, 0, unroll=unroll)
        e = (row8 == 7 - edge).astype(F32)
        pw_r, pw_i = group_scan(e * ar, e * ai)
        e_out = (row8 == edge).astype(F32)
        a8_r = jnp.sum(pw_r * e_out, axis=0, keepdims=True)
        a8_i = jnp.sum(pw_i * e_out, axis=0, keepdims=True)
        cr, ci = cre_s[...], cim_s[...]
        for i in range(ng):
            g = ng - 1 - i if reverse else i
            rows = slice(g * 8, g * 8 + 8)
            lr, li = sre_ref[g * 8 + edge:g * 8 + edge + 1, :], sim_ref[g * 8 + edge:g * 8 + edge + 1, :]
            sre_ref[rows, :] = sre_ref[rows, :] + pw_r * cr - pw_i * ci
            sim_ref[rows, :] = sim_ref[rows, :] + pw_r * ci + pw_i * cr
            cr, ci = lr + a8_r * cr - a8_i * ci, li + a8_r * ci + a8_i * cr
        cre_s[...] = cr
        cim_s[...] = ci

    blk = pl.BlockSpec((t, strip), cmap)
    a_blk = pl.BlockSpec((1, strip), lambda w, c: (0, w))
    return pl.pallas_call(
        body, name=name, grid=(ns, nc), in_specs=[blk, blk, a_blk, a_blk], out_specs=[blk, blk],
        out_shape=[jax.ShapeDtypeStruct((l, n), F32), jax.ShapeDtypeStruct((l, n), F32)],
        scratch_shapes=[pltpu.VMEM((1, strip), F32), pltpu.VMEM((1, strip), F32)],
        compiler_params=_params(),
    )(bre, bim, are, aim)


def _disc_math(lr, li, ldt, br, bi):
    dt = jnp.exp(ldt)
    mag = jnp.exp(lr * dt)
    a_re, a_im = mag * jnp.cos(li * dt), mag * jnp.sin(li * dt)
    den = lr * lr + li * li
    e_re, e_im = a_re - 1.0, a_im
    f_re = (e_re * lr + e_im * li) / den
    f_im = (e_im * lr - e_re * li) / den
    return a_re, a_im, f_re * br - f_im * bi, f_re * bi + f_im * br


def _disc_fwd(lr, li, ldt, br, bi):
    def body(lr_ref, li_ref, ldt_ref, br_ref, bi_ref, are_ref, aim_ref, bbr_ref, bbi_ref):
        a_re, a_im, bb_re, bb_im = _disc_math(lr_ref[...], li_ref[...], ldt_ref[...], br_ref[...], bi_ref[...])
        are_ref[...] = a_re
        aim_ref[...] = a_im
        bbr_ref[...] = bb_re
        bbi_ref[...] = bb_im

    col = jax.ShapeDtypeStruct(lr.shape, F32)
    mat = jax.ShapeDtypeStruct(br.shape, F32)
    return pl.pallas_call(body, name="s5_disc_fwd", out_shape=[col, col, mat, mat], compiler_params=_params())(
        lr, li, ldt, br, bi)


def _disc_bwd(lr, li, ldt, br, bi, da_re, da_im, dbb_re, dbb_im):
    def body(lr_ref, li_ref, ldt_ref, br_ref, bi_ref, g0, g1, g2, g3, o0, o1, o2, o3, o4):
        _, vjp = jax.vjp(_disc_math, lr_ref[...], li_ref[...], ldt_ref[...], br_ref[...], bi_ref[...])
        grads = vjp((g0[...], g1[...], g2[...], g3[...]))
        for o, g in zip((o0, o1, o2, o3, o4), grads):
            o[...] = g

    col = jax.ShapeDtypeStruct(lr.shape, F32)
    mat = jax.ShapeDtypeStruct(br.shape, F32)
    return pl.pallas_call(body, name="s5_disc_bwd", out_shape=[col, col, col, mat, mat], compiler_params=_params())(
        lr, li, ldt, br, bi, da_re, da_im, dbb_re, dbb_im)


N_CHIP = 4


def _place():
    x, y, c = lax.axis_index("x"), lax.axis_index("y"), lax.axis_index("c")
    return (x, y, c), (x, y, 1 - c), [(1 - x, y), (x, 1 - y), (1 - x, 1 - y)]


def _lin(px, py, pc):
    return 4 * px + 2 * py + pc


def _remote(src, dst, sems, k, dev):
    return pltpu.make_async_remote_copy(src_ref=src, dst_ref=dst, send_sem=sems[0].at[k], recv_sem=sems[1].at[k],
                                        device_id=dev, device_id_type=pl.DeviceIdType.MESH)


def _hbm_call(body, name, srcs, out_shapes, nsem):
    n = len(srcs)
    hbm = pl.BlockSpec(memory_space=pltpu.HBM)

    def wrapped(*refs):
        body(refs[:n], refs[n:2 * n], (refs[2 * n], refs[2 * n + 1]), refs[2 * n + 2])

    return pl.pallas_call(
        wrapped, name=name, in_specs=[hbm] * n, out_specs=[hbm] * n, out_shape=out_shapes,
        scratch_shapes=[pltpu.SemaphoreType.DMA((nsem * n,)), pltpu.SemaphoreType.DMA((nsem * n,)),
                        pltpu.SemaphoreType.DMA((n,))],
        compiler_params=pltpu.CompilerParams(has_side_effects=True),
    )(*srcs)


def _gather_all(name, srcs):
    def body(s_refs, o_refs, sems, loc_sems):
        me, sib, chips = _place()
        c = me[2]
        started, local = [], []
        for g, (s_ref, o_ref) in enumerate(zip(s_refs, o_refs)):
            slot = lambda dev, o_ref=o_ref: o_ref.at[_lin(*dev)]
            mine = pltpu.make_async_copy(s_ref, slot(me), loc_sems.at[g])
            mine.start()
            local.append(mine)
            first = [_remote(s_ref, slot(me), sems, 7 * g, sib)]
            first += [_remote(s_ref, slot(me), sems, 7 * g + 1 + j, (*chip, c)) for j, chip in enumerate(chips)]
            for cp in first:
                cp.start()
            started += first
        for g, (s_ref, o_ref) in enumerate(zip(s_refs, o_refs)):
            slot = lambda dev, o_ref=o_ref: o_ref.at[_lin(*dev)]
            for j, chip in enumerate(chips):
                _remote(s_ref, slot((*chip, c)), sems, 7 * g + 1 + j, me).wait_recv()
                cp = _remote(slot((*chip, c)), slot((*chip, c)), sems, 7 * g + 4 + j, sib)
                cp.start()
                started.append(cp)
        for g, (s_ref, o_ref) in enumerate(zip(s_refs, o_refs)):
            slot = lambda dev, o_ref=o_ref: o_ref.at[_lin(*dev)]
            _remote(s_ref, slot(sib), sems, 7 * g, me).wait_recv()
            for j, chip in enumerate(chips):
                _remote(s_ref, slot((*chip, 1 - c)), sems, 7 * g + 4 + j, me).wait_recv()
        for cp in started:
            cp.wait_send()
        for cp in local:
            cp.wait()

    return _hbm_call(body, name, srcs, [jax.ShapeDtypeStruct((N_DEV,) + s.shape, s.dtype) for s in srcs], 7)


def _pair_exchange(name, sends):
    def body(s_refs, o_refs, sems, loc_sems):
        me, sib, _ = _place()
        c = me[2]
        copies = [_remote(s_ref.at[2 * q + (1 - c)], o_ref.at[q], sems, N_CHIP * g + q, sib)
                  for g, (s_ref, o_ref) in enumerate(zip(s_refs, o_refs)) for q in range(N_CHIP)]
        for cp in copies:
            cp.start()
        for cp in copies:
            cp.wait()

    return _hbm_call(body, name, sends, [jax.ShapeDtypeStruct((N_CHIP,) + s.shape[1:], s.dtype) for s in sends],
                     N_CHIP)


def _row_block(r, cap):
    best = 8
    for m in range(8, min(r, cap) + 1, 8):
        if r % m == 0:
            best = m
    return best


def _pair_sum(name, send, got):
    _, r, cols = send.shape
    bm = _row_block(r, max(8, 256 * 1024 // cols))
    core = lax.axis_index("c").astype(jnp.int32).reshape(1)

    def body(core_ref, a_ref, b_ref, o_ref):
        o_ref[...] = a_ref[...] + b_ref[...]

    grid_spec = pltpu.PrefetchScalarGridSpec(
        num_scalar_prefetch=1, grid=(N_CHIP, r // bm),
        in_specs=[pl.BlockSpec((1, bm, cols), lambda q, i, cr: (2 * q + cr[0], i, 0)),
                  pl.BlockSpec((1, bm, cols), lambda q, i, cr: (q, i, 0))],
        out_specs=pl.BlockSpec((1, bm, cols), lambda q, i, cr: (q, i, 0)))
    return pl.pallas_call(body, name=name, grid_spec=grid_spec,
                          out_shape=jax.ShapeDtypeStruct((N_CHIP, r, cols), F32), compiler_params=_params())(
        core, send, got)


def _chip_exchange(name, parts):
    def body(p_refs, o_refs, sems, loc_sems):
        me, _, chips = _place()
        c = me[2]
        chip_id = lambda chip: 2 * chip[0] + chip[1]
        my_chip = chip_id(me)
        local, copies = [], []
        for g, (p_ref, o_ref) in enumerate(zip(p_refs, o_refs)):
            mine = pltpu.make_async_copy(p_ref.at[my_chip], o_ref.at[my_chip], loc_sems.at[g])
            mine.start()
            local.append(mine)
            for j, chip in enumerate(chips):
                cp = _remote(p_ref.at[chip_id(chip)], o_ref.at[my_chip], sems, 3 * g + j, (*chip, c))
                cp.start()
                copies.append(cp)
        for g, (p_ref, o_ref) in enumerate(zip(p_refs, o_refs)):
            for j, chip in enumerate(chips):
                _remote(p_ref.at[my_chip], o_ref.at[chip_id(chip)], sems, 3 * g + j, me).wait_recv()
        for cp in copies:
            cp.wait_send()
        for cp in local:
            cp.wait()

    return _hbm_call(body, name, parts, [jax.ShapeDtypeStruct(p.shape, p.dtype) for p in parts], 3)


def _adamw(name, rcv, w, m, v):
    r, c = w.shape
    nslot = rcv.shape[0]
    bm = _row_block(r, max(8, 256 * 1024 // c))

    def body(rcv_ref, w_ref, m_ref, v_ref, g_ref, d_ref, m2_ref, v2_ref):
        g = rcv_ref[0]
        for s in range(1, nslot):
            g = g + rcv_ref[s]
        m2 = ADAM_B1 * m_ref[...] + (1.0 - ADAM_B1) * g
        v2 = ADAM_B2 * v_ref[...] + (1.0 - ADAM_B2) * (g * g)
        m_hat = m2 / (1.0 - ADAM_B1 ** ADAM_STEP)
        v_hat = v2 / (1.0 - ADAM_B2 ** ADAM_STEP)
        g_ref[...] = g
        d_ref[...] = -ADAM_LR * (m_hat / (jnp.sqrt(v_hat) + ADAM_EPS) + ADAM_WD * w_ref[...])
        m2_ref[...] = m2
        v2_ref[...] = v2

    blk = pl.BlockSpec((bm, c), lambda i: (i, 0))
    out = jax.ShapeDtypeStruct((r, c), F32)
    return pl.pallas_call(
        body, name=name, grid=(r // bm,),
        in_specs=[pl.BlockSpec((nslot, bm, c), lambda i: (0, i, 0)), blk, blk, blk], out_specs=[blk] * 4,
        out_shape=[out] * 4, compiler_params=_params(),
    )(rcv, w, m, v)


IN_SHARD = IN_WIDTH // N_DEV
UP_SHARD = 2 * D_FF // N_DEV
GROUPS = (
    ("g128", LANES, (("w_q_b", Q_LORA, D_QK), ("w_kv_b", KV_LORA, D_NOPE + D_V), ("w_o_mla", MLA_HEADS * D_V, LANES),
                     ("w_o_ssm", SSM_WIDTH, LANES), ("w_o_cross", X_WIDTH, LANES))),
    ("g512", SSM_WIDTH, (("w_glu", SSM_WIDTH // N_DEV, SSM_WIDTH),)),
    ("g1024", D_MODEL, (("w_mem_kv", D_MODEL // N_DEV, D_MODEL), ("w_out", D_MODEL // N_DEV, D_MODEL),
                        ("w_down", D_FF // N_DEV, D_MODEL))),
    ("g640", 640, (("w_in", D_MODEL, IN_SHARD),)),
    ("g768", 768, (("w_up", D_MODEL, UP_SHARD),)),
    ("gconv", 768, (("conv_w", 3, UP_SHARD),)),
)
REP_W = 1024


def _rows8(a):
    return -(-a // 8) * 8


def _group_rows(params, depth):
    off, r = {}, 0
    for n, a, _ in params:
        for li in range(depth):
            off[(n, li)] = r
            r += _rows8(a)
    return off, r


def _group_local(width, params, vals, depth, dtype):
    blocks = []
    for n, a, b in params:
        for li in range(depth):
            blocks.append(jnp.pad(vals[n][li].astype(dtype), ((0, _rows8(a) - a), (0, width - b))))
    return jnp.concatenate(blocks, axis=0)


def _rep_rows(flat):
    n = flat.shape[-1]
    per = REP_W * 64
    tot = -(-n // per) * per
    flat = jnp.pad(flat, [(0, 0)] * (flat.ndim - 1) + [(0, tot - n)])
    return flat.reshape(flat.shape[:-1] + (tot // REP_W, REP_W))


IN_SEGS = (
    (0, Q_LORA + KV_LORA, "wqkv", 0),
    (Q_LORA + KV_LORA, Q_LORA + KV_LORA + D_ROPE, "wqkv", Q_LORA + KV_LORA + KR_LO),
    (Q_LORA + KV_LORA + D_ROPE, Q_LORA + KV_LORA + D_ROPE + SSM_WIDTH, "w_u", 0),
    (Q_LORA + KV_LORA + D_ROPE + SSM_WIDTH, Q_LORA + KV_LORA + D_ROPE + SSM_WIDTH + X_WIDTH, "w_xq", 0),
    (Q_LORA + KV_LORA + D_ROPE + SSM_WIDTH + X_WIDTH, IN_WIDTH, "w_g", 0),
)
IN_PARTS = (("wqkv", QKV_W), ("w_u", SSM_WIDTH), ("w_xq", X_WIDTH), ("w_g", 3 * D_MODEL))


def _in_pieces():
    out = []
    for d in range(N_DEV):
        for lo, hi, part, dst in IN_SEGS:
            s, e = max(lo, d * IN_SHARD), min(hi, (d + 1) * IN_SHARD)
            if s < e:
                out.append((d, s - d * IN_SHARD, e - s, [p for p, _ in IN_PARTS].index(part), dst + s - lo))
    return out


def _unpack_w_in(gathered, row0, bm=256):
    def body(x_ref, *o_refs):
        o_refs[0][...] = jnp.zeros_like(o_refs[0])
        for d, src, n, part, dst in _in_pieces():
            o_refs[part][:, dst:dst + n] = x_ref[d, :, src:src + n]

    return pl.pallas_call(
        body, name="unpack_w_in", grid=(D_MODEL // bm,),
        in_specs=[pl.BlockSpec((N_DEV, bm, gathered.shape[2]), lambda i: (0, row0 // bm + i, 0))],
        out_specs=[pl.BlockSpec((bm, w), lambda i: (i, 0)) for _, w in IN_PARTS],
        out_shape=[jax.ShapeDtypeStruct((D_MODEL, w), gathered.dtype) for _, w in IN_PARTS], compiler_params=_params(),
    )(gathered)


def _pack_w_in(parts, bm=256):
    def body(*refs):
        o_ref = refs[-1]
        o_ref[...] = jnp.zeros_like(o_ref)
        for d, src, n, part, dst in _in_pieces():
            o_ref[d, :, src:src + n] = refs[part][:, dst:dst + n]

    return pl.pallas_call(
        body, name="pack_w_in", grid=(D_MODEL // bm,),
        in_specs=[pl.BlockSpec((bm, w), lambda i: (i, 0)) for _, w in IN_PARTS],
        out_specs=pl.BlockSpec((N_DEV, bm, 640), lambda i: (0, i, 0)),
        out_shape=jax.ShapeDtypeStruct((N_DEV, D_MODEL, 640), F32), compiler_params=_params(),
    )(*parts)


def _unpack_w_up(gathered, row0, bm=256):
    def body(x_ref, o_ref):
        for d in range(N_DEV):
            o_ref[:, d * UP_SHARD:(d + 1) * UP_SHARD] = x_ref[d, :, :UP_SHARD]

    return pl.pallas_call(
        body, name="unpack_w_up", grid=(D_MODEL // bm,),
        in_specs=[pl.BlockSpec((N_DEV, bm, gathered.shape[2]), lambda i: (0, row0 // bm + i, 0))],
        out_specs=pl.BlockSpec((bm, 2 * D_FF), lambda i: (i, 0)),
        out_shape=jax.ShapeDtypeStruct((D_MODEL, 2 * D_FF), gathered.dtype), compiler_params=_params(),
    )(gathered)


def _pack_w_up(dw_g, dw_v, bm=256):
    half = N_DEV // 2

    def body(g_ref, v_ref, o_ref):
        o_ref[...] = jnp.zeros_like(o_ref)
        for d in range(N_DEV):
            src = g_ref if d < half else v_ref
            c0 = (d % half) * UP_SHARD
            o_ref[d, :, :UP_SHARD] = src[:, c0:c0 + UP_SHARD]

    blk = pl.BlockSpec((bm, D_FF), lambda i: (i, 0))
    return pl.pallas_call(
        body, name="pack_w_up", grid=(D_MODEL // bm,), in_specs=[blk, blk],
        out_specs=pl.BlockSpec((N_DEV, bm, 768), lambda i: (0, i, 0)),
        out_shape=jax.ShapeDtypeStruct((N_DEV, D_MODEL, 768), F32), compiler_params=_params(),
    )(dw_g, dw_v)


def _cols_to_rows(full):
    a, nb = full.shape
    return full.reshape(a, N_DEV, nb // N_DEV).transpose(1, 0, 2)


def _rows_to_cols(blocks):
    n, a, b = blocks.shape
    return blocks.transpose(1, 0, 2).reshape(a, n * b)


def _block_diag_in(bb):
    b3 = bb.reshape(SSM_GROUPS, SSM_STATE, SSM_GROUP_CH).transpose(0, 2, 1)
    eye = jnp.eye(SSM_GROUPS, dtype=bb.dtype)
    return (b3[:, :, None, :] * eye[:, None, :, None]).reshape(SSM_WIDTH, SSM_N)


def _block_diag_out(cc):
    c3 = cc.transpose(0, 2, 1)
    eye = jnp.eye(SSM_GROUPS, dtype=cc.dtype)
    return (c3[:, :, None, :] * eye[:, None, :, None]).reshape(SSM_N, SSM_WIDTH)


def _diag_blocks(mats, rows_per, cols_per):
    per = SSM_GROUPS // BD_BLOCKS
    m5 = mats.reshape(BD_BLOCKS, per, rows_per, per, cols_per)
    eye = jnp.eye(per, dtype=mats.dtype)
    return jnp.sum(m5 * eye[None, :, None, :, None], axis=3).reshape(SSM_GROUPS, rows_per, cols_per)


def _layer_weights(gath, offs, li):
    def blk(grp, n, a):
        r0 = offs[grp][(n, li)]
        return gath[grp][:, r0:r0 + a, :]

    o = dict(zip([p for p, _ in IN_PARTS], _unpack_w_in(gath["g640"], offs["g640"][("w_in", li)])))
    o["wqb"] = _rows_to_cols(blk("g128", "w_q_b", Q_LORA))
    wkv = _rows_to_cols(blk("g128", "w_kv_b", KV_LORA)).reshape(KV_LORA, MLA_HEADS, D_NOPE + D_V)
    o["wk"] = jnp.pad(wkv[:, :, :D_NOPE], ((0, 0), (0, 0), (0, LANES - D_NOPE))).reshape(KV_LORA, MLA_HEADS * LANES)
    o["wv"] = jnp.pad(wkv[:, :, D_NOPE:], ((0, 0), (0, 0), (0, LANES - D_V))).reshape(KV_LORA, MLA_HEADS * LANES)
    o["wv_t"] = o["wv"].T
    wo = _rows_to_cols(blk("g128", "w_o_mla", MLA_HEADS * D_V)).reshape(MLA_HEADS, D_V, D_MODEL)
    o["wo_mla"] = jnp.pad(wo, ((0, 0), (0, LANES - D_V), (0, 0))).reshape(MLA_HEADS * LANES, D_MODEL)
    o["w_o_ssm"] = _rows_to_cols(blk("g128", "w_o_ssm", SSM_WIDTH))
    o["w_o_cross"] = _rows_to_cols(blk("g128", "w_o_cross", X_WIDTH))
    o["w_glu"] = blk("g512", "w_glu", SSM_WIDTH // N_DEV).reshape(SSM_WIDTH, SSM_WIDTH)
    o["w_mem_kv"] = blk("g1024", "w_mem_kv", D_MODEL // N_DEV).reshape(D_MODEL, 2 * X_WIDTH)
    o["w_out"] = blk("g1024", "w_out", D_MODEL // N_DEV).reshape(D_MODEL, D_MODEL)
    o["w_down"] = blk("g1024", "w_down", D_FF // N_DEV).reshape(D_FF, D_MODEL)
    o["w_up"] = _unpack_w_up(gath["g768"], offs["g768"][("w_up", li)])
    o["w_up_g"], o["w_up_v"] = o["w_up"][:, :D_FF], o["w_up"][:, D_FF:]
    o["conv_w"] = _rows_to_cols(blk("gconv", "conv_w", 3)[:, :, :UP_SHARD])
    return o


def _row(v):
    return v.reshape(1, -1).astype(F32)


def _pad_lanes(v, n=LANES):
    return jnp.pad(v, (0, n - v.shape[0])).reshape(1, n).astype(F32)


def _layer_fwd(x, mem, tabs, w, p):
    l = x.shape[0]
    rope_c, rope_sa, rope_sb = tabs
    s = {"x": x}
    g_mix, g_qa, g_kva = _row(p["norm_mix_g"]), _row(p["q_a_norm_g"]), _row(p["kv_a_norm_g"])
    g_q, g_k = _pad_lanes(p["q_norm_g"]), _pad_lanes(p["k_norm_g"])

    (h,) = _rowwise("rms_mix", lambda i, r, c: (_rms_f(r[0][...], c[0][...], D_MODEL),), l, 512,
                    [(x, D_MODEL, 0, "row")], [g_mix], [(D_MODEL, BF16)])
    pqkv = _mm("proj_qkv", [(h, w["wqkv"])])
    u = _mm("proj_u", [(h, w["w_u"])])
    xq = _mm("proj_xq", [(h, w["w_xq"])])
    gl = _mm("proj_gate", [(h, w["w_g"])], bm=1024, bn_cap=1024)
    s.update(h=h, pqkv=pqkv, u=u, xq=xq, gl=gl)

    def prep_a(i, r, c):
        return (_rms_f(r[0][:, :Q_LORA], c[0][...], Q_LORA),
                _rms_f(r[0][:, Q_LORA:Q_LORA + KV_LORA], c[1][...], KV_LORA))

    nq, nkv = _rowwise("mla_prep_a", prep_a, l, 512, [(pqkv, QKV_W, 0, "row")], [g_qa, g_kva],
                       [(Q_LORA, BF16), (KV_LORA, BF16)])
    q_raw = _mm("mla_q_b", [(nq, w["wqb"])], bn_cap=1024)
    k_raw = _mm("mla_k_b", [(nkv, w["wk"])], bn_cap=1024)
    v_mla = _mm("mla_v_b", [(nkv, w["wv"])], out_dtype=BF16, bn_cap=1024)
    vt_mla = _mm("mla_vt_b", [(w["wv_t"], nkv)], trans_b=True, out_dtype=BF16, bn_cap=1024)

    def prep_b(i, r, c):
        q_ref, k_ref, kr_ref, c_ref, sa_ref, sb_ref = r
        rc, sa, sb, kr = c_ref[...], sa_ref[...], sb_ref[...], kr_ref[...]
        qs, ks = [], []
        for hd in range(MLA_HEADS):
            cols = slice(hd * LANES, (hd + 1) * LANES)
            qs.append(_rope_f(_rms_f(q_ref[:, cols], c[0][...], D_QK), rc, sa, sb))
            ks.append(_rope_f(_rms_f(k_ref[:, cols] + kr, c[1][...], D_QK), rc, sa, sb))
        return jnp.concatenate(qs, axis=1), jnp.concatenate(ks, axis=1)

    hw = MLA_HEADS * LANES
    kr_blk = (Q_LORA + KV_LORA) // LANES
    tab_ins = [(rope_c, LANES, 0, "row"), (rope_sa, LANES, 0, "row"), (rope_sb, LANES, 0, "row")]
    q, k = _rowwise("mla_prep_b", prep_b, l, 256,
                    [(q_raw, hw, 0, "row"), (k_raw, hw, 0, "row"), (pqkv, LANES, kr_blk, "row")] + tab_ins,
                    [g_q, g_k], [(hw, BF16), (hw, BF16)])
    o_a, lse_a = _attn_fwd("mla_attn_fwd", q, k, vt_mla, qoff=0, koff=0, voff=0, heads=MLA_HEADS, causal=True,
                           scale=D_QK ** -0.5, bq=1024, bk=1024)
    ya = _mm("mla_o", [(o_a, w["wo_mla"])], bn_cap=1024)
    s.update(nq=nq, nkv=nkv, q_raw=q_raw, k_raw=k_raw, v_mla=v_mla, q=q, k=k, o_a=o_a, lse_a=lse_a, ya=ya)

    lr = p["ssm_lambda_re"].reshape(SSM_N, 1)
    li = p["ssm_lambda_im"].reshape(SSM_N, 1)
    ldt = jnp.repeat(p["ssm_log_dt"], SSM_STATE).reshape(SSM_N, 1)
    br = p["ssm_b_re"].reshape(SSM_N, SSM_GROUP_CH)
    bi = p["ssm_b_im"].reshape(SSM_N, SSM_GROUP_CH)
    a_re, a_im, bb_re, bb_im = _disc_fwd(lr, li, ldt, br, bi)
    bb_re_d, bb_im_d = _block_diag_in(bb_re).astype(BF16), _block_diag_in(bb_im).astype(BF16)
    cc_re_d = _block_diag_out(p["ssm_c_re"]).astype(BF16)
    cc_imn_d = _block_diag_out(-p["ssm_c_im"]).astype(BF16)
    a_re_row, a_im_row = a_re.reshape(1, SSM_N), a_im.reshape(1, SSM_N)
    d_row = _row(p["ssm_d"])
    b_glu = _row(p["b_glu"])
    bu_re = _mm_bd("s5_bu_re", [(u, bb_re_d)])
    bu_im = _mm_bd("s5_bu_im", [(u, bb_im_d)])
    s_re, s_im = _scan("s5_scan_fwd", bu_re, bu_im, a_re_row, a_im_row, reverse=False)
    ypre = _mm_bd("s5_c", [(s_re, cc_re_d), (s_im, cc_imn_d)])

    def ssm_y(i, r, c):
        return (_gelu(r[0][...] + c[0][...] * r[1][...]),)

    (y_b,) = _rowwise("s5_gelu", ssm_y, l, 512, [(ypre, SSM_WIDTH, 0, "row"), (u, SSM_WIDTH, 0, "row")], [d_row],
                      [(SSM_WIDTH, BF16)])
    z = _mm("s5_glu", [(y_b, w["w_glu"])])

    def ssm_out(i, r, c):
        y = _gelu(r[0][...] + c[0][...] * r[1][...])
        return (y * jax.nn.sigmoid(r[2][...] + c[1][...]),)

    (out_b,) = _rowwise("s5_glu_out", ssm_out, l, 512,
                        [(ypre, SSM_WIDTH, 0, "row"), (u, SSM_WIDTH, 0, "row"), (z, SSM_WIDTH, 0, "row")],
                        [d_row, b_glu], [(SSM_WIDTH, BF16)])
    yb = _mm("s5_o", [(out_b, w["w_o_ssm"])], bn_cap=1024)
    s.update(disc=(lr, li, ldt, br, bi), a_rows=(a_re_row, a_im_row), bb_d=(bb_re_d, bb_im_d),
             cc_d=(cc_re_d, cc_imn_d), s_re=s_re, s_im=s_im, ypre=ypre, y_b=y_b, z=z, out_b=out_b, yb=yb)

    g_mem, g_xq, g_xk = _row(p["mem_norm_g"]), _row(p["xq_norm_g"]), _row(p["xk_norm_g"])
    ml = mem.shape[0]
    (memn,) = _rowwise("rms_mem", lambda i, r, c: (_rms_f(r[0][...], c[0][...], D_MODEL),), ml, 256,
                       [(mem, D_MODEL, 0, "row")], [g_mem], [(D_MODEL, BF16)])
    kvm = _mm("cross_kv", [(memn, w["w_mem_kv"])], bn_cap=1024)

    def headnorm(i, r, c):
        return (jnp.concatenate([_rms_f(r[0][:, hd * LANES:(hd + 1) * LANES], c[0][...], X_HEAD_DIM)
                                 for hd in range(X_HEADS)], axis=1),)

    (xk,) = _rowwise("cross_k_norm", headnorm, ml, 256, [(kvm, X_WIDTH, 0, "row")], [g_xk], [(X_WIDTH, BF16)])
    (xqn,) = _rowwise("cross_q_norm", headnorm, l, 512, [(xq, X_WIDTH, 0, "row")], [g_xq], [(X_WIDTH, BF16)])
    xvt = kvm[:, X_WIDTH:].T.astype(BF16)
    o_c, lse_c = _attn_fwd("cross_attn_fwd", xqn, xk, xvt, qoff=0, koff=0, voff=0, heads=X_HEADS,
                           causal=False, scale=X_HEAD_DIM ** -0.5, bq=1024, bk=256)
    yc = _mm("cross_o", [(o_c, w["w_o_cross"])], bn_cap=1024)
    s.update(memn=memn, kvm=kvm, xk=xk, xqn=xqn, o_c=o_c, lse_c=lse_c, yc=yc)

    b_gate = _row(p["b_gate"])

    def merge(i, r, c):
        acc = None
        for br_ in range(3):
            g = jax.nn.sigmoid(r[br_][...] + c[0][:, br_ * D_MODEL:(br_ + 1) * D_MODEL])
            t = g * r[3 + br_][...]
            acc = t if acc is None else acc + t
        return (acc,)

    gate_ins = [(gl, D_MODEL, b_, "row") for b_ in range(3)]
    (merged,) = _rowwise("merge", merge, l, 256,
                         gate_ins + [(ya, D_MODEL, 0, "row"), (yb, D_MODEL, 0, "row"), (yc, D_MODEL, 0, "row")],
                         [b_gate], [(D_MODEL, BF16)])
    x1 = _mm("mix_out", [(merged, w["w_out"])], add=x, bn_cap=1024)
    s.update(merged=merged, x1=x1)

    g_ffn = _row(p["norm_ffn_g"])
    (h2,) = _rowwise("rms_ffn", lambda i, r, c: (_rms_f(r[0][...], c[0][...], D_MODEL),), l, 512,
                     [(x1, D_MODEL, 0, "row")], [g_ffn], [(D_MODEL, BF16)])
    up = _mm("ffn_up", [(h2, w["w_up"])], bm=1024, bn_cap=1408)
    conv_w = w["conv_w"]
    conv_b = _row(p["conv_b"])

    def conv_glu(i, r, c):
        cg = _conv(r[0], r[2], i, c[0], c[1], 0)
        cv = _conv(r[1], r[3], i, c[0], c[1], D_FF)
        return (cg * jax.nn.sigmoid(cg) * cv,)

    up_ins = [(up, D_FF, 0, "row"), (up, D_FF, 1, "row"), (up, D_FF, 0, "prev"), (up, D_FF, 1, "prev")]
    (act,) = _rowwise("ffn_conv_glu", conv_glu, l, 256, up_ins, [conv_w, conv_b], [(D_FF, BF16)])
    x2 = _mm("ffn_down", [(act, w["w_down"])], add=x1, bm=1024, bn_cap=1024)
    s.update(h2=h2, up=up, act=act, conv_w=conv_w, conv_b=conv_b)
    return x2, s


def _conv(x_ref, halo_ref, i, w_ref, b_ref, col0):
    x = x_ref[...]
    cols = slice(col0, col0 + D_FF)
    return (w_ref[0:1, cols] * _shift_down(x, halo_ref, i, 2) + w_ref[1:2, cols] * _shift_down(x, halo_ref, i, 1)
            + w_ref[2:3, cols] * x + b_ref[:, cols])


def _layer_bwd(dx2, dx2_b, s, mem, tabs, w, p):
    l = dx2.shape[0]
    rope_c, rope_sa, rope_sb = tabs
    x, x1 = s["x"], s["x1"]
    g = {}

    dact = _mm("ffn_down_dx", [(dx2_b, w["w_down"])], trans_b=True, bn_cap=1408)
    g["w_down"] = _mm_tn("ffn_down_dw", s["act"], dx2_b, bm_cap=1408).reshape(N_DEV, D_FF // N_DEV, D_MODEL)
    up = s["up"]
    nblk_c = l // min(256, l)

    def conv_bwd_a(i, r, c):
        outs, accs = [], []
        for half in range(2):
            x_ref, halo_ref = r[1 + half], r[3 + half]
            col0 = half * D_FF
            cols = slice(col0, col0 + D_FF)
            xv = x_ref[...]
            xm1, xm2 = _shift_down(xv, halo_ref, i, 1), _shift_down(xv, halo_ref, i, 2)
            cv = c[0][0:1, cols] * xm2 + c[0][1:2, cols] * xm1 + c[0][2:3, cols] * xv + c[1][:, cols]
            outs.append((cv, xv, xm1, xm2))
        (cg, xg, xg1, xg2), (cvv, xv, xv1, xv2) = outs
        sig = jax.nn.sigmoid(cg)
        da = r[0][...]
        dcv = da * (cg * sig)
        dcg = da * cvv * (sig * (1.0 + cg * (1.0 - sig)))
        for d, (x0, xa, xb) in ((dcg, (xg, xg1, xg2)), (dcv, (xv, xv1, xv2))):
            accs += [jnp.sum(d, axis=0, keepdims=True), jnp.sum(d * xb, axis=0, keepdims=True),
                     jnp.sum(d * xa, axis=0, keepdims=True), jnp.sum(d * x0, axis=0, keepdims=True)]
        return (dcg, dcv, *accs)

    up_ins = [(up, D_FF, 0, "row"), (up, D_FF, 1, "row"), (up, D_FF, 0, "prev"), (up, D_FF, 1, "prev")]
    res = _rowwise("ffn_conv_glu_bwd", conv_bwd_a, l, 256, [(dact, D_FF, 0, "row")] + up_ins,
                   [s["conv_w"], s["conv_b"]], [(D_FF, F32), (D_FF, F32)], [(1, D_FF)] * 8)
    dcg, dcv = res[0], res[1]
    db_g, dw0_g, dw1_g, dw2_g, db_v, dw0_v, dw1_v, dw2_v = res[2:]
    g["conv_b"] = jnp.concatenate([db_g, db_v], axis=1)[0]
    g["conv_w"] = _cols_to_rows(jnp.concatenate(
        [jnp.concatenate([dw0_g, dw0_v], axis=1), jnp.concatenate([dw1_g, dw1_v], axis=1),
         jnp.concatenate([dw2_g, dw2_v], axis=1)], axis=0))

    def conv_bwd_b(i, r, c):
        outs = []
        for half in range(2):
            d = r[half][...]
            cols = slice(half * D_FF, (half + 1) * D_FF)
            outs.append(c[0][2:3, cols] * d + c[0][1:2, cols] * _shift_up(d, r[2 + half], i, nblk_c, 1)
                        + c[0][0:1, cols] * _shift_up(d, r[2 + half], i, nblk_c, 2))
        return tuple(outs)

    dup_g, dup_v = _rowwise("ffn_conv_bwd_x", conv_bwd_b, l, 256,
                            [(dcg, D_FF, 0, "row"), (dcv, D_FF, 0, "row"), (dcg, D_FF, 0, "next"),
                             (dcv, D_FF, 0, "next")], [s["conv_w"]], [(D_FF, BF16), (D_FF, BF16)])
    dh2 = _mm("ffn_up_dx", [(dup_g, w["w_up_g"]), (dup_v, w["w_up_v"])], trans_b=True, bn_cap=1024)
    g["w_up"] = _pack_w_up(_mm_tn("ffn_up_dw_g", s["h2"], dup_g), _mm_tn("ffn_up_dw_v", s["h2"], dup_v))

    def rms_bwd_res(i, r, c):
        dx, dg = _rms_b(r[0][...], c[0][...], r[1][...], D_MODEL)
        dx = dx + r[2][...]
        return (dx, dx, dg)

    dx1, dx1_b, g["norm_ffn_g"] = _rowwise(
        "rms_ffn_bwd", rms_bwd_res, l, 512, [(x1, D_MODEL, 0, "row"), (dh2, D_MODEL, 0, "row"), (dx2, D_MODEL, 0, "row")],
        [_row(p["norm_ffn_g"])], [(D_MODEL, F32), (D_MODEL, BF16)], [(1, D_MODEL)])

    dmerged = _mm("mix_out_dx", [(dx1_b, w["w_out"])], trans_b=True, bn_cap=1024)
    g["w_out"] = _mm_tn("mix_out_dw", s["merged"], dx1_b).reshape(N_DEV, D_MODEL // N_DEV, D_MODEL)
    gl, ya, yb, yc = s["gl"], s["ya"], s["yb"], s["yc"]

    def merge_bwd(i, r, c):
        dm = r[0][...]
        dys, dgs = [], []
        for b_ in range(3):
            gate = jax.nn.sigmoid(r[1 + b_][...] + c[0][:, b_ * D_MODEL:(b_ + 1) * D_MODEL])
            dys.append(dm * gate)
            dgs.append(dm * r[4 + b_][...] * (gate * (1.0 - gate)))
        dgl = jnp.concatenate(dgs, axis=1)
        return (*dys, dgl, jnp.sum(dgl, axis=0, keepdims=True))

    gate_ins = [(gl, D_MODEL, b_, "row") for b_ in range(3)]
    dya, dyb, dyc, dgl, db_gate = _rowwise(
        "merge_bwd", merge_bwd, l, 256,
        [(dmerged, D_MODEL, 0, "row")] + gate_ins + [(ya, D_MODEL, 0, "row"), (yb, D_MODEL, 0, "row"),
                                                     (yc, D_MODEL, 0, "row")],
        [_row(p["b_gate"])], [(D_MODEL, BF16)] * 3 + [(3 * D_MODEL, BF16)], [(1, 3 * D_MODEL)])
    g["b_gate"] = db_gate[0]

    do_c = _mm("cross_o_dx", [(dyc, w["w_o_cross"])], trans_b=True, out_dtype=BF16)
    g["w_o_cross"] = _cols_to_rows(_mm_tn("cross_o_dw", s["o_c"], dyc))
    kvm = s["kvm"]
    delta_c = _attn_delta("cross_attn_delta", s["o_c"], do_c, heads=X_HEADS, bq=512)
    dxqn, dxk, dxv = _attn_bwd("cross_attn_bwd", s["xqn"], s["xk"], s["xk"].T, kvm, do_c, s["lse_c"], delta_c, qoff=0,
                               koff=0, voff=X_HEADS, heads=X_HEADS, causal=False, scale=X_HEAD_DIM ** -0.5, bq=1024,
                               bk=256)
    ml = mem.shape[0]

    def headnorm_bwd(i, r, c):
        dxs, dg = [], None
        for hd in range(X_HEADS):
            cols = slice(hd * LANES, (hd + 1) * LANES)
            dx_h, dg_h = _rms_b(r[0][:, cols], c[0][...], r[1][:, cols], X_HEAD_DIM)
            dxs.append(dx_h)
            dg = dg_h if dg is None else dg + dg_h
        return (jnp.concatenate(dxs, axis=1), dg)

    dxq, dg_xq = _rowwise("cross_q_norm_bwd", headnorm_bwd, l, 512,
                          [(s["xq"], X_WIDTH, 0, "row"), (dxqn, X_WIDTH, 0, "row")], [_row(p["xq_norm_g"])],
                          [(X_WIDTH, BF16)], [(1, X_HEAD_DIM)])
    dkvm_k, dg_xk = _rowwise("cross_k_norm_bwd", headnorm_bwd, ml, 256,
                             [(kvm, X_WIDTH, 0, "row"), (dxk, X_WIDTH, 0, "row")], [_row(p["xk_norm_g"])],
                             [(X_WIDTH, F32)], [(1, X_HEAD_DIM)])
    g["xq_norm_g"], g["xk_norm_g"] = dg_xq[0], dg_xk[0]
    dkvm = jnp.concatenate([dkvm_k, dxv], axis=1)
    g["w_mem_kv"] = _mm_tn("cross_kv_dw", s["memn"], dkvm).reshape(N_DEV, D_MODEL // N_DEV, 2 * X_WIDTH)
    dmemn = _mm("cross_kv_dx", [(dkvm, w["w_mem_kv"])], trans_b=True, bn_cap=1024)

    def rms_bwd_gain_only(i, r, c):
        return (_rms_b(r[0][...], c[0][...], r[1][...], D_MODEL)[1],)

    (dg_mem,) = _rowwise("rms_mem_bwd", rms_bwd_gain_only, ml, 256,
                         [(mem, D_MODEL, 0, "row"), (dmemn, D_MODEL, 0, "row")], [_row(p["mem_norm_g"])], [],
                         [(1, D_MODEL)])
    g["mem_norm_g"] = dg_mem[0]

    dout_b = _mm("s5_o_dx", [(dyb, w["w_o_ssm"])], trans_b=True)
    g["w_o_ssm"] = _cols_to_rows(_mm_tn("s5_o_dw", s["out_b"], dyb))
    ypre, u, z = s["ypre"], s["u"], s["z"]
    d_row, b_glu = _row(p["ssm_d"]), _row(p["b_glu"])
    yuz = [(ypre, SSM_WIDTH, 0, "row"), (u, SSM_WIDTH, 0, "row"), (z, SSM_WIDTH, 0, "row")]

    def glu_bwd_z(i, r, c):
        y = _gelu(r[1][...] + c[0][...] * r[2][...])
        sg = jax.nn.sigmoid(r[3][...] + c[1][...])
        dz = r[0][...] * y * (sg * (1.0 - sg))
        return (dz, jnp.sum(dz, axis=0, keepdims=True))

    dz, db_glu = _rowwise("s5_glu_bwd_z", glu_bwd_z, l, 512, [(dout_b, SSM_WIDTH, 0, "row")] + yuz, [d_row, b_glu],
                          [(SSM_WIDTH, BF16)], [(1, SSM_WIDTH)])
    g["b_glu"] = db_glu[0]
    g["w_glu"] = _mm_tn("s5_glu_dw", s["y_b"], dz).reshape(N_DEV, SSM_WIDTH // N_DEV, SSM_WIDTH)
    dy2 = _mm("s5_glu_dx", [(dz, w["w_glu"])], trans_b=True)

    def gelu_bwd(i, r, c):
        t = r[2][...] + c[0][...] * r[3][...]
        sg = jax.nn.sigmoid(r[4][...] + c[1][...])
        dt = (r[0][...] * sg + r[1][...]) * _gelu_grad(t)
        return (dt, c[0][...] * dt, jnp.sum(dt * r[3][...], axis=0, keepdims=True))

    dypre, du_skip, dd = _rowwise(
        "s5_gelu_bwd", gelu_bwd, l, 512, [(dout_b, SSM_WIDTH, 0, "row"), (dy2, SSM_WIDTH, 0, "row")] + yuz,
        [d_row, b_glu], [(SSM_WIDTH, BF16), (SSM_WIDTH, F32)], [(1, SSM_WIDTH)])
    g["ssm_d"] = dd.reshape(SSM_GROUPS, SSM_GROUP_CH)
    cc_re_d, cc_imn_d = s["cc_d"]
    bb_re_d, bb_im_d = s["bb_d"]
    a_re_row, a_im_row = s["a_rows"]
    s_re, s_im = s["s_re"], s["s_im"]
    g_re = _mm_bd("s5_c_dx_re", [(dypre, cc_re_d)], trans_b=True)
    g_im = _mm_bd("s5_c_dx_im", [(dypre, cc_imn_d)], trans_b=True)
    dcc_re = _mm_tn_bd("s5_c_dw_re", s_re, dypre)
    dcc_imn = _mm_tn_bd("s5_c_dw_im", s_im, dypre)
    g["ssm_c_re"] = _diag_blocks(dcc_re, SSM_STATE, SSM_GROUP_CH).transpose(0, 2, 1)
    g["ssm_c_im"] = -_diag_blocks(dcc_imn, SSM_STATE, SSM_GROUP_CH).transpose(0, 2, 1)
    lam_re, lam_im = _scan("s5_scan_bwd", g_re, g_im, a_re_row, -a_im_row, reverse=True)

    def da_sum(i, r, c):
        lr_, li_ = r[0][...], r[1][...]
        pr, pi = _shift_down(r[2][...], r[4], i, 1), _shift_down(r[3][...], r[5], i, 1)
        return (jnp.sum(lr_ * pr + li_ * pi, axis=0, keepdims=True), jnp.sum(li_ * pr - lr_ * pi, axis=0, keepdims=True))

    da_re, da_im = _rowwise(
        "s5_da", da_sum, l, 256,
        [(lam_re, SSM_N, 0, "row"), (lam_im, SSM_N, 0, "row"), (s_re, SSM_N, 0, "row"), (s_im, SSM_N, 0, "row"),
         (s_re, SSM_N, 0, "prev"), (s_im, SSM_N, 0, "prev")], [], [], [(1, SSM_N), (1, SSM_N)])
    dbb_re_d = _mm_tn_bd("s5_b_dw_re", u, lam_re)
    dbb_im_d = _mm_tn_bd("s5_b_dw_im", u, lam_im)
    dbb_re = _diag_blocks(dbb_re_d, SSM_GROUP_CH, SSM_STATE).transpose(0, 2, 1).reshape(SSM_N, SSM_GROUP_CH)
    dbb_im = _diag_blocks(dbb_im_d, SSM_GROUP_CH, SSM_STATE).transpose(0, 2, 1).reshape(SSM_N, SSM_GROUP_CH)
    dlr, dli, dldt, dbr, dbi = _disc_bwd(*s["disc"], da_re.reshape(SSM_N, 1), da_im.reshape(SSM_N, 1), dbb_re, dbb_im)
    g["ssm_lambda_re"] = dlr.reshape(SSM_GROUPS, SSM_STATE)
    g["ssm_lambda_im"] = dli.reshape(SSM_GROUPS, SSM_STATE)
    g["ssm_log_dt"] = dldt.reshape(SSM_GROUPS, SSM_STATE).sum(axis=1)
    g["ssm_b_re"] = dbr.reshape(SSM_GROUPS, SSM_STATE, SSM_GROUP_CH)
    g["ssm_b_im"] = dbi.reshape(SSM_GROUPS, SSM_STATE, SSM_GROUP_CH)
    du = _mm_bd("s5_b_dx", [(lam_re, bb_re_d), (lam_im, bb_im_d)], trans_b=True, add=du_skip, out_dtype=BF16)

    do_a = _mm("mla_o_dx", [(dya, w["wo_mla"])], trans_b=True, out_dtype=BF16, bn_cap=1024)
    dwo = _mm_tn("mla_o_dw", s["o_a"], dya)
    g["w_o_mla"] = _cols_to_rows(dwo.reshape(MLA_HEADS, LANES, D_MODEL)[:, :D_V].reshape(MLA_HEADS * D_V, D_MODEL))
    delta_a = _attn_delta("mla_attn_delta", s["o_a"], do_a, heads=MLA_HEADS, bq=512)
    dq, dk, dv = _attn_bwd("mla_attn_bwd", s["q"], s["k"], s["k"].T, s["v_mla"], do_a, s["lse_a"], delta_a, qoff=0,
                           koff=0, voff=0, heads=MLA_HEADS, causal=True, scale=D_QK ** -0.5, bq=1024, bk=1024)
    hw = MLA_HEADS * LANES
    kr_blk = (Q_LORA + KV_LORA) // LANES
    pqkv = s["pqkv"]
    g_q, g_k = _pad_lanes(p["q_norm_g"]), _pad_lanes(p["k_norm_g"])
    lane = lax.broadcasted_iota(jnp.int32, (1, LANES), 1)
    kr_mask = jnp.logical_and(lane >= KR_LO, lane < KR_LO + D_ROPE).astype(F32)

    def prep_b_bwd(i, r, c):
        dq_ref, dk_ref, q_ref, k_ref, kr_ref, c_ref, sa_ref, sb_ref = r
        rc, sa, sb, kr = c_ref[...], sa_ref[...], sb_ref[...], kr_ref[...]
        dqs, dks, dkr, dgq, dgk = [], [], None, None, None
        for hd in range(MLA_HEADS):
            cols = slice(hd * LANES, (hd + 1) * LANES)
            dxq, dgq_h = _rms_b(q_ref[:, cols], c[0][...], _rope_b(dq_ref[:, cols], rc, sa, sb), D_QK)
            dxk, dgk_h = _rms_b(k_ref[:, cols] + kr, c[1][...], _rope_b(dk_ref[:, cols], rc, sa, sb), D_QK)
            dqs.append(dxq)
            dks.append(dxk)
            dkr = dxk if dkr is None else dkr + dxk
            dgq = dgq_h if dgq is None else dgq + dgq_h
            dgk = dgk_h if dgk is None else dgk + dgk_h
        return (jnp.concatenate(dqs, axis=1), jnp.concatenate(dks, axis=1), dkr * c[2][...], dgq, dgk)

    tab_ins = [(rope_c, LANES, 0, "row"), (rope_sa, LANES, 0, "row"), (rope_sb, LANES, 0, "row")]
    dq_raw, dk_raw, dkr, dg_q, dg_k = _rowwise(
        "mla_prep_b_bwd", prep_b_bwd, l, 256,
        [(dq, hw, 0, "row"), (dk, hw, 0, "row"), (s["q_raw"], hw, 0, "row"), (s["k_raw"], hw, 0, "row"),
         (pqkv, LANES, kr_blk, "row")] + tab_ins, [g_q, g_k, kr_mask],
        [(hw, BF16), (hw, BF16), (LANES, F32)], [(1, LANES), (1, LANES)])
    g["q_norm_g"], g["k_norm_g"] = dg_q[0, :D_QK], dg_k[0, :D_QK]
    dnq = _mm("mla_q_b_dx", [(dq_raw, w["wqb"])], trans_b=True)
    dnkv = _mm("mla_kv_b_dx", [(dk_raw, w["wk"]), (dv, w["wv"])], trans_b=True)
    dwqb = _mm_tn("mla_q_b_dw", s["nq"], dq_raw)
    g["w_q_b"] = _cols_to_rows(dwqb)
    dwk = _mm_tn("mla_k_b_dw", s["nkv"], dk_raw).reshape(KV_LORA, MLA_HEADS, LANES)[:, :, :D_NOPE]
    dwv = _mm_tn("mla_v_b_dw", s["nkv"], dv).reshape(KV_LORA, MLA_HEADS, LANES)[:, :, :D_V]
    g["w_kv_b"] = jnp.concatenate([dwk, dwv], axis=2).transpose(1, 0, 2)

    def prep_a_bwd(i, r, c):
        dcq, dgqa = _rms_b(r[0][:, :Q_LORA], c[0][...], r[1][...], Q_LORA)
        dckv, dgkva = _rms_b(r[0][:, Q_LORA:Q_LORA + KV_LORA], c[1][...], r[2][...], KV_LORA)
        return (jnp.concatenate([dcq, dckv, r[3][...]], axis=1), dgqa, dgkva)

    dpqkv, dg_qa, dg_kva = _rowwise(
        "mla_prep_a_bwd", prep_a_bwd, l, 512,
        [(pqkv, QKV_W, 0, "row"), (dnq, Q_LORA, 0, "row"), (dnkv, KV_LORA, 0, "row"), (dkr, LANES, 0, "row")],
        [_row(p["q_a_norm_g"]), _row(p["kv_a_norm_g"])], [(QKV_W, BF16)], [(1, Q_LORA), (1, KV_LORA)])
    g["q_a_norm_g"], g["kv_a_norm_g"] = dg_qa[0], dg_kva[0]

    h = s["h"]
    dh = _mm("proj_dx", [(dpqkv, w["wqkv"]), (du, w["w_u"]), (dxq, w["w_xq"]), (dgl, w["w_g"])], trans_b=True,
             bn_cap=1024)
    dwqkv = _mm_tn("proj_qkv_dw", h, dpqkv)
    g["w_in"] = _pack_w_in([dwqkv, _mm_tn("proj_u_dw", h, du), _mm_tn("proj_xq_dw", h, dxq),
                            _mm_tn("proj_gate_dw", h, dgl)])
    dx, dx_b, dg_mix = _rowwise(
        "rms_mix_bwd", rms_bwd_res, l, 512, [(x, D_MODEL, 0, "row"), (dh, D_MODEL, 0, "row"), (dx1, D_MODEL, 0, "row")],
        [_row(p["norm_mix_g"])], [(D_MODEL, F32), (D_MODEL, BF16)], [(1, D_MODEL)])
    g["norm_mix_g"] = dg_mix[0]
    g["norm_ffn_g"] = g["norm_ffn_g"][0]
    return dx, dx_b, g


def _rope_tables(positions):
    inv_freq = ROPE_THETA ** (-jnp.arange(0, D_ROPE, 2, dtype=F32) / D_ROPE)
    ang = positions.astype(F32)[:, None] * inv_freq
    cos, sin = jnp.cos(ang), jnp.sin(ang)
    l = positions.shape[0]
    one, zero = jnp.ones((l, D_NOPE), F32), lambda n: jnp.zeros((l, n), F32)
    pad = LANES - D_QK
    rope_c = jnp.concatenate([one, cos, cos, zero(pad)], axis=1)
    rope_sa = jnp.concatenate([zero(D_NOPE), -sin, zero(16), zero(pad)], axis=1)
    rope_sb = jnp.concatenate([zero(D_NOPE + 16), sin, zero(pad)], axis=1)
    return rope_c, rope_sa, rope_sb


def kernel(x, mem, positions, norm_mix_g, w_in, q_a_norm_g, w_q_b, kv_a_norm_g, w_kv_b, q_norm_g, k_norm_g, w_o_mla, ssm_lambda_re, ssm_lambda_im, ssm_log_dt, ssm_b_re, ssm_b_im, ssm_c_re, ssm_c_im, ssm_d, w_glu, b_glu, w_o_ssm, mem_norm_g, w_mem_kv, xq_norm_g, xk_norm_g, w_o_cross, b_gate, w_out, norm_ffn_g, w_up, conv_w, conv_b, w_down, loss_target, m_norm_mix_g, m_w_in, m_q_a_norm_g, m_w_q_b, m_kv_a_norm_g, m_w_kv_b, m_q_norm_g, m_k_norm_g, m_w_o_mla, m_ssm_lambda_re, m_ssm_lambda_im, m_ssm_log_dt, m_ssm_b_re, m_ssm_b_im, m_ssm_c_re, m_ssm_c_im, m_ssm_d, m_w_glu, m_b_glu, m_w_o_ssm, m_mem_norm_g, m_w_mem_kv, m_xq_norm_g, m_xk_norm_g, m_w_o_cross, m_b_gate, m_w_out, m_norm_ffn_g, m_w_up, m_conv_w, m_conv_b, m_w_down, v_norm_mix_g, v_w_in, v_q_a_norm_g, v_w_q_b, v_kv_a_norm_g, v_w_kv_b, v_q_norm_g, v_k_norm_g, v_w_o_mla, v_ssm_lambda_re, v_ssm_lambda_im, v_ssm_log_dt, v_ssm_b_re, v_ssm_b_im, v_ssm_c_re, v_ssm_c_im, v_ssm_d, v_w_glu, v_b_glu, v_w_o_ssm, v_mem_norm_g, v_w_mem_kv, v_xq_norm_g, v_xk_norm_g, v_w_o_cross, v_b_gate, v_w_out, v_norm_ffn_g, v_w_up, v_conv_w, v_conv_b, v_w_down):
    a = dict(locals())
    wts = {n: a[n] for n in WEIGHT_ORDER}
    m_in = {n: a["m_" + n] for n in WEIGHT_ORDER}
    v_in = {n: a["v_" + n] for n in WEIGHT_ORDER}
    depth = norm_mix_g.shape[0]
    x0, mem0, pos0, tgt = x[0], mem[0], positions[0], loss_target[0]
    l = x0.shape[0]
    offs = {grp: _group_rows(params, depth)[0] for grp, _, params in GROUPS}

    srcs = [_group_local(width, params, wts, depth, F32 if grp == "gconv" else BF16) for grp, width, params in GROUPS]
    gath = dict(zip([grp for grp, _, _ in GROUPS], _gather_all("gather_weights", srcs)))

    tabs = _rope_tables(pos0)
    layer_w = [_layer_weights(gath, offs, i) for i in range(depth)]
    layer_p = [{n: wts[n][i] for n in REPLICATED} for i in range(depth)]

    saved = []
    xc = x0
    for i in range(depth):
        xc, s = _layer_fwd(xc, mem0, tabs, layer_w[i], layer_p[i])
        saved.append(s)

    def loss_fn(i, r, c):
        d = r[0][...] - r[1][...]
        dy = d * (1.0 / D_MODEL)
        return (dy, dy, jnp.sum(d * d, axis=0, keepdims=True))

    dy, dy_b, sq = _rowwise("loss", loss_fn, l, 512, [(xc, D_MODEL, 0, "row"), (tgt, D_MODEL, 0, "row")], [],
                            [(D_MODEL, F32), (D_MODEL, BF16)], [(1, D_MODEL)])
    loss = lax.psum(0.5 * jnp.sum(sq) / D_MODEL, ("x", "y", "c"))

    grads = [None] * depth
    dxc, dxc_b = dy, dy_b
    for i in reversed(range(depth)):
        dxc, dxc_b, grads[i] = _layer_bwd(dxc, dxc_b, saved[i], mem0, tabs, layer_w[i], layer_p[i])
    grad_x = dxc[None]

    sends = []
    for grp, width, params in GROUPS:
        blocks = []
        for n, rows, _ in params:
            for i in range(depth):
                gb = grads[i][n]
                blocks.append(jnp.pad(gb, ((0, 0), (0, _rows8(rows) - rows), (0, width - gb.shape[2]))))
        sends.append(jnp.concatenate(blocks, axis=1))
    rep_flat = jnp.concatenate([jnp.stack([grads[i][n] for i in range(depth)]).reshape(-1) for n in REPLICATED])
    sends.append(_rep_rows(jnp.broadcast_to(rep_flat[None], (N_DEV, rep_flat.shape[0]))))
    names = [grp for grp, _, _ in GROUPS] + ["rep"]
    got = _pair_exchange("exchange_grads_core", sends)
    parts = [_pair_sum("grad_pair_sum_" + nm, s_, g_) for nm, s_, g_ in zip(names, sends, got)]
    rcvs = dict(zip(names, _chip_exchange("exchange_grads_chip", parts)))

    outs = {}
    for grp, width, params in GROUPS:
        local = [_group_local(width, params, d, depth, F32) for d in (wts, m_in, v_in)]
        res = _adamw("adamw_" + grp, rcvs[grp], *local)
        for tag, arr in zip(("grad", "delta", "m", "v"), res):
            for n, rows, cols in params:
                outs[(tag, n)] = jnp.stack([arr[offs[grp][(n, i)]:offs[grp][(n, i)] + rows, :cols]
                                            for i in range(depth)])
    rep_local = [_rep_rows(jnp.concatenate([d[n].astype(F32).reshape(-1) for n in REPLICATED])) for d in (wts, m_in, v_in)]
    res = _adamw("adamw_rep", rcvs["rep"], *rep_local)
    for tag, arr in zip(("grad", "delta", "m", "v"), res):
        flat, off = arr.reshape(-1), 0
        for n in REPLICATED:
            cnt = wts[n].size
            outs[(tag, n)] = flat[off:off + cnt].reshape(wts[n].shape)
            off += cnt
    result = [loss, grad_x]
    for tag in ("grad", "delta", "m", "v"):
        result += [outs[(tag, n)] for n in WEIGHT_ORDER]
    return tuple(result)
```

```python
import math

import jax
import jax.numpy as jnp
from jax import lax
from jax.experimental import pallas as pl
from jax.experimental.pallas import tpu as pltpu

F32 = jnp.float32
BF16 = jnp.bfloat16

N_DEV = 8
LANES = 128
LOG2E = math.log2(math.e)
VMEM_LIMIT_BYTES = 56 * 1024 * 1024

D_MODEL = 1024
EPS = 1e-6
MLA_HEADS = 8
Q_LORA = 384
KV_LORA = 256
D_NOPE = 64
D_ROPE = 32
D_QK = D_NOPE + D_ROPE
D_V = 64
ROPE_THETA = 10000.0
SSM_GROUPS = 32
SSM_GROUP_CH = 16
SSM_WIDTH = 512
SSM_STATE = 64
SSM_N = SSM_GROUPS * SSM_STATE
X_HEADS = 4
X_HEAD_DIM = 128
X_WIDTH = 512
D_FF = 2816
IN_WIDTH = Q_LORA + KV_LORA + D_ROPE + SSM_WIDTH + X_WIDTH + 3 * D_MODEL
QKV_W = Q_LORA + KV_LORA + LANES
KR_LO = D_NOPE

ADAM_LR = 0.001
ADAM_B1 = 0.9
ADAM_B2 = 0.999
ADAM_EPS = 1e-08
ADAM_WD = 0.01
ADAM_STEP = 10

REPLICATED = (
    "norm_mix_g", "q_a_norm_g", "kv_a_norm_g", "q_norm_g", "k_norm_g", "ssm_lambda_re", "ssm_lambda_im",
    "ssm_log_dt", "ssm_b_re", "ssm_b_im", "ssm_c_re", "ssm_c_im", "ssm_d", "b_glu", "mem_norm_g",
    "xq_norm_g", "xk_norm_g", "b_gate", "norm_ffn_g", "conv_b",
)
WEIGHT_ORDER = (
    "norm_mix_g", "w_in", "q_a_norm_g", "w_q_b", "kv_a_norm_g", "w_kv_b", "q_norm_g", "k_norm_g", "w_o_mla",
    "ssm_lambda_re", "ssm_lambda_im", "ssm_log_dt", "ssm_b_re", "ssm_b_im", "ssm_c_re", "ssm_c_im", "ssm_d",
    "w_glu", "b_glu", "w_o_ssm", "mem_norm_g", "w_mem_kv", "xq_norm_g", "xk_norm_g", "w_o_cross", "b_gate",
    "w_out", "norm_ffn_g", "w_up", "conv_w", "conv_b", "w_down",
)


def _params(**kw):
    return pltpu.CompilerParams(vmem_limit_bytes=VMEM_LIMIT_BYTES, **kw)


def _pick(n, cap):
    if n <= cap:
        return n
    best = None
    for m in range(LANES, cap + 1, LANES):
        if n % m == 0:
            best = m
    assert best is not None, n
    return best


_NN = (((1,), (0,)), ((), ()))
_NT = (((1,), (1,)), ((), ()))
_TN = (((0,), (0,)), ((), ()))


def _dot(a, b, dn):
    return lax.dot_general(a.astype(BF16), b.astype(BF16), dn, preferred_element_type=F32)


def _mm(name, pairs, *, trans_b=False, add=None, out_dtype=F32, bm=512, bn_cap=512):
    m = pairs[0][0].shape[0]
    n = pairs[0][1].shape[0 if trans_b else 1]
    bm = min(bm, m)
    bn = _pick(n, bn_cap)
    npair = len(pairs)

    def body(*refs):
        o_ref = refs[-1]
        acc = None
        for p in range(npair):
            d = _dot(refs[2 * p][...], refs[2 * p + 1][...], _NT if trans_b else _NN)
            acc = d if acc is None else acc + d
        if add is not None:
            acc = acc + refs[2 * npair][...]
        o_ref[...] = acc.astype(out_dtype)

    in_specs, args = [], []
    for a, b in pairs:
        k = a.shape[1]
        in_specs.append(pl.BlockSpec((bm, k), lambda i, j: (i, 0)))
        if trans_b:
            in_specs.append(pl.BlockSpec((bn, k), lambda i, j: (j, 0)))
        else:
            in_specs.append(pl.BlockSpec((k, bn), lambda i, j: (0, j)))
        args += [a, b]
    if add is not None:
        in_specs.append(pl.BlockSpec((bm, bn), lambda i, j: (i, j)))
        args.append(add)
    return pl.pallas_call(
        body, name=name, grid=(m // bm, n // bn), in_specs=in_specs,
        out_specs=pl.BlockSpec((bm, bn), lambda i, j: (i, j)),
        out_shape=jax.ShapeDtypeStruct((m, n), out_dtype), compiler_params=_params(),
    )(*args)


def _mm_tn(name, a, b, *, bm_cap=512, bn_cap=1536, bk=1024):
    l, m = a.shape
    n = b.shape[1]
    bm, bn, bk = _pick(m, bm_cap), _pick(n, bn_cap), min(bk, l)

    def body(a_ref, b_ref, o_ref):
        @pl.when(pl.program_id(2) == 0)
        def _():
            o_ref[...] = jnp.zeros_like(o_ref)

        o_ref[...] += _dot(a_ref[...], b_ref[...], _TN)

    return pl.pallas_call(
        body, name=name, grid=(m // bm, n // bn, l // bk),
        in_specs=[pl.BlockSpec((bk, bm), lambda i, j, k: (k, i)), pl.BlockSpec((bk, bn), lambda i, j, k: (k, j))],
        out_specs=pl.BlockSpec((bm, bn), lambda i, j, k: (i, j)),
        out_shape=jax.ShapeDtypeStruct((m, n), F32), compiler_params=_params(),
    )(a, b)


def _rowwise(name, fn, nrows, bm, row_ins, consts, row_outs, acc_outs=()):
    bm = min(bm, nrows)
    nblk = nrows // bm
    sub = bm // 8
    nin, nc, nro = len(row_ins), len(consts), len(row_outs)

    def body(*refs):
        i = pl.program_id(0)
        outs = fn(i, refs[:nin], refs[nin:nin + nc])
        o_refs = refs[nin + nc:nin + nc + nro]
        a_refs = refs[nin + nc + nro:]
        for r, v in zip(o_refs, outs[:nro]):
            r[...] = v.astype(r.dtype)
        if a_refs:
            @pl.when(i == 0)
            def _():
                for r in a_refs:
                    r[...] = jnp.zeros_like(r)

            for r, v in zip(a_refs, outs[nro:]):
                r[...] += v

    in_specs, args = [], []
    for arr, w, cb, kind in row_ins:
        if kind == "row":
            in_specs.append(pl.BlockSpec((bm, w), lambda i, cb=cb: (i, cb)))
        elif kind == "prev":
            in_specs.append(pl.BlockSpec((8, w), lambda i, cb=cb: (jnp.maximum(i * sub - 1, 0), cb)))
        else:
            in_specs.append(pl.BlockSpec((8, w), lambda i, cb=cb: (jnp.minimum((i + 1) * sub, nrows // 8 - 1), cb)))
        args.append(arr)
    for c in consts:
        in_specs.append(pl.BlockSpec(c.shape, lambda i: (0, 0)))
        args.append(c)
    out_specs = [pl.BlockSpec((bm, w), lambda i: (i, 0)) for w, _ in row_outs]
    out_specs += [pl.BlockSpec(s, lambda i: (0, 0)) for s in acc_outs]
    out_shape = [jax.ShapeDtypeStruct((nrows, w), dt) for w, dt in row_outs]
    out_shape += [jax.ShapeDtypeStruct(s, F32) for s in acc_outs]
    res = pl.pallas_call(
        body, name=name, grid=(nblk,), in_specs=in_specs, out_specs=out_specs, out_shape=out_shape,
        compiler_params=_params(),
    )(*args)
    return res


def _rms_f(x, g, n):
    r = lax.rsqrt(jnp.sum(x * x, axis=-1, keepdims=True) * (1.0 / n) + EPS)
    return x * r * g


def _rms_b(x, g, dy, n):
    r = lax.rsqrt(jnp.sum(x * x, axis=-1, keepdims=True) * (1.0 / n) + EPS)
    gx = dy * g
    dx = r * gx - x * (r * r * r * (jnp.sum(x * gx, axis=-1, keepdims=True) * (1.0 / n)))
    dg = jnp.sum(dy * (x * r), axis=0, keepdims=True)
    return dx, dg


def _rope_f(x, c, sa, sb):
    return x * c + pltpu.roll(x, LANES - 16, 1) * sa + pltpu.roll(x, 16, 1) * sb


def _rope_b(g, c, sa, sb):
    return g * c + pltpu.roll(g * sa, 16, 1) + pltpu.roll(g * sb, LANES - 16, 1)


def _gelu(x):
    c = math.sqrt(2.0 / math.pi)
    return 0.5 * x * (1.0 + jnp.tanh(c * (x + 0.044715 * (x * x * x))))


def _gelu_grad(x):
    c = math.sqrt(2.0 / math.pi)
    th = jnp.tanh(c * (x + 0.044715 * (x * x * x)))
    return 0.5 * (1.0 + th) + 0.5 * x * (1.0 - th * th) * (c * (1.0 + 3.0 * 0.044715 * (x * x)))


def _row_ids(bm):
    return lax.broadcasted_iota(jnp.int32, (bm, 1), 0)


def _shift_down(x, halo_ref, i, k):
    live = (i > 0).astype(F32)
    out = pltpu.roll(x, k, 0)
    row = _row_ids(8)
    first = out[:8]
    for r in range(k):
        e = (row == r).astype(F32)
        first = first * (1.0 - e) + e * (halo_ref[8 - k + r:8 - k + r + 1, :] * live)
    return jnp.concatenate([first, out[8:]], axis=0)


def _shift_up(x, halo_ref, i, nblk, k):
    bm = x.shape[0]
    live = (i < nblk - 1).astype(F32)
    out = pltpu.roll(x, bm - k, 0)
    row = _row_ids(8)
    last = out[bm - 8:]
    for r in range(k):
        e = (row == 8 - k + r).astype(F32)
        last = last * (1.0 - e) + e * (halo_ref[r:r + 1, :] * live)
    return jnp.concatenate([out[:bm - 8], last], axis=0)


def _live_pairs(nq, nk, bq, bk, causal, key_major):
    pairs = [(i, j) for i in range(nq) for j in range(nk) if not causal or j * bk <= i * bq + bq - 1]
    if key_major:
        pairs.sort(key=lambda ij: (ij[1], ij[0]))
    return (jnp.asarray([p[0] for p in pairs], jnp.int32), jnp.asarray([p[1] for p in pairs], jnp.int32))


def _attn_fwd(name, qa, ka, vta, *, qoff, koff, voff, heads, causal, scale, bq, bk):
    lq, lk = qa.shape[0], ka.shape[0]
    bq, bk = min(bq, lq), min(bk, lk)
    nq, nk = lq // bq, lk // bk
    c2 = scale * LOG2E
    tab_i, tab_j = _live_pairs(nq, nk, bq, bk, causal, key_major=False)

    def body(ti, tj, q_ref, k_ref, vt_ref, o_ref, lse_ref, m_s, l_s, acc_s):
        t = pl.program_id(1)
        i, j = ti[t], tj[t]
        j_last = jnp.minimum(nk - 1, (i * bq + bq - 1) // bk) if causal else nk - 1

        @pl.when(j == 0)
        def _():
            m_s[...] = jnp.full_like(m_s, -1e30)
            l_s[...] = jnp.zeros_like(l_s)
            acc_s[...] = jnp.zeros_like(acc_s)

        def step(masked):
            st = _dot(k_ref[...], q_ref[...], _NT) * c2
            if masked:
                key = j * bk + lax.broadcasted_iota(jnp.int32, (bk, bq), 0)
                qry = i * bq + lax.broadcasted_iota(jnp.int32, (bk, bq), 1)
                st = jnp.where(key <= qry, st, -1e30)
            m_prev = m_s[...]
            m_new = jnp.maximum(m_prev, jnp.max(st, axis=0, keepdims=True))
            alpha = jnp.exp2(m_prev - m_new)
            pt = jnp.exp2(st - m_new)
            l_s[...] = alpha * l_s[...] + jnp.sum(pt, axis=0, keepdims=True)
            acc_s[...] = alpha * acc_s[...] + _dot(vt_ref[...], pt, _NN)
            m_s[...] = m_new

        if causal:
            full = j * bk + bk - 1 <= i * bq
            pl.when(full)(lambda: step(False))
            pl.when(jnp.logical_not(full))(lambda: step(True))
        else:
            step(False)

        @pl.when(j == j_last)
        def _():
            l = l_s[...]
            o_ref[...] = (acc_s[...] / l).T.astype(o_ref.dtype)
            lse_ref[0] = m_s[...] + jnp.log2(l)

    grid_spec = pltpu.PrefetchScalarGridSpec(
        num_scalar_prefetch=2, grid=(heads, int(tab_i.shape[0])),
        in_specs=[pl.BlockSpec((bq, LANES), lambda h, t, ti, tj: (ti[t], qoff + h)),
                  pl.BlockSpec((bk, LANES), lambda h, t, ti, tj: (tj[t], koff + h)),
                  pl.BlockSpec((LANES, bk), lambda h, t, ti, tj: (voff + h, tj[t]))],
        out_specs=[pl.BlockSpec((bq, LANES), lambda h, t, ti, tj: (ti[t], h)),
                   pl.BlockSpec((1, 1, bq), lambda h, t, ti, tj: (h, 0, ti[t]))],
        scratch_shapes=[pltpu.VMEM((1, bq), F32), pltpu.VMEM((1, bq), F32), pltpu.VMEM((LANES, bq), F32)])
    return pl.pallas_call(
        body, name=name, grid_spec=grid_spec,
        out_shape=[jax.ShapeDtypeStruct((lq, heads * LANES), BF16), jax.ShapeDtypeStruct((heads, 1, lq), F32)],
        compiler_params=_params(),
    )(tab_i, tab_j, qa, ka, vta)


def _attn_delta(name, oa, doa, *, heads, bq):
    lq = oa.shape[0]
    bq = min(bq, lq)

    def body(o_ref, do_ref, d_ref):
        prod = o_ref[...].astype(F32) * do_ref[...].astype(F32)
        hi = prod.astype(BF16)
        lo = (prod - hi.astype(F32)).astype(BF16)
        pick = (lax.broadcasted_iota(jnp.int32, (8, LANES), 0) == 0).astype(BF16)
        sums = _dot(pick, hi, _NT) + _dot(pick, lo, _NT)
        d_ref[0] = jnp.sum(sums, axis=0, keepdims=True)

    blk = pl.BlockSpec((bq, LANES), lambda h, i: (i, h))
    return pl.pallas_call(
        body, name=name, grid=(heads, lq // bq), in_specs=[blk, blk],
        out_specs=pl.BlockSpec((1, 1, bq), lambda h, i: (h, 0, i)),
        out_shape=jax.ShapeDtypeStruct((heads, 1, lq), F32), compiler_params=_params(),
    )(oa, doa)


def _attn_bwd(name, qa, ka, kta, va, doa, lsea, deltaa, *, qoff, koff, voff, heads, causal, scale, bq, bk):
    lq, lk = qa.shape[0], ka.shape[0]
    bq, bk = min(bq, lq), min(bk, lk)
    nq, nk = lq // bq, lk // bk
    c2 = scale * LOG2E
    tab_i, tab_j = _live_pairs(nq, nk, bq, bk, causal, key_major=True)

    def body(ti, tj, q_ref, k_ref, kt_ref, v_ref, do_ref, lse_ref, delta_ref, dqt_ref, dk_ref, dv_ref):
        t = pl.program_id(1)
        i, j = ti[t], tj[t]
        i_first = (j * bk) // bq if causal else 0

        @pl.when(t == 0)
        def _():
            dqt_ref[...] = jnp.zeros_like(dqt_ref)

        @pl.when(i == i_first)
        def _():
            dk_ref[...] = jnp.zeros_like(dk_ref)
            dv_ref[...] = jnp.zeros_like(dv_ref)

        def step(masked):
            q, k, v, do = q_ref[...], k_ref[...], v_ref[...], do_ref[...]
            st = _dot(k, q, _NT) * c2
            if masked:
                key = j * bk + lax.broadcasted_iota(jnp.int32, (bk, bq), 0)
                qry = i * bq + lax.broadcasted_iota(jnp.int32, (bk, bq), 1)
                st = jnp.where(key <= qry, st, -1e30)
            pt = jnp.exp2(st - lse_ref[0])
            dv_ref[...] += _dot(pt, do, _NN)
            dpt = _dot(v, do, _NT)
            dst = (pt * (dpt - delta_ref[0]) * scale).astype(BF16)
            dk_ref[...] += _dot(dst, q, _NN)
            dqt_ref[0, i] += _dot(kt_ref[...], dst, _NN)

        if causal:
            full = j * bk + bk - 1 <= i * bq
            pl.when(full)(lambda: step(False))
            pl.when(jnp.logical_not(full))(lambda: step(True))
        else:
            step(False)

    q_spec = lambda off: pl.BlockSpec((bq, LANES), lambda h, t, ti, tj: (ti[t], off + h))
    kv_spec = lambda off: pl.BlockSpec((bk, LANES), lambda h, t, ti, tj: (tj[t], off + h))
    row_spec = pl.BlockSpec((1, 1, bq), lambda h, t, ti, tj: (h, 0, ti[t]))
    grid_spec = pltpu.PrefetchScalarGridSpec(
        num_scalar_prefetch=2, grid=(heads, int(tab_i.shape[0])),
        in_specs=[q_spec(qoff), kv_spec(koff), pl.BlockSpec((LANES, bk), lambda h, t, ti, tj: (koff + h, tj[t])),
                  kv_spec(voff), q_spec(0), row_spec, row_spec],
        out_specs=[pl.BlockSpec((1, nq, LANES, bq), lambda h, t, ti, tj: (h, 0, 0, 0)),
                   pl.BlockSpec((bk, LANES), lambda h, t, ti, tj: (tj[t], h)),
                   pl.BlockSpec((bk, LANES), lambda h, t, ti, tj: (tj[t], h))])
    dqt, dk, dv = pl.pallas_call(
        body, name=name, grid_spec=grid_spec,
        out_shape=[jax.ShapeDtypeStruct((heads, nq, LANES, bq), F32), jax.ShapeDtypeStruct((lk, heads * LANES), F32),
                   jax.ShapeDtypeStruct((lk, heads * LANES), F32)],
        compiler_params=_params(),
    )(tab_i, tab_j, qa, ka, kta, va, doa, lsea, deltaa)
    return dqt.transpose(1, 3, 0, 2).reshape(lq, heads * LANES), dk, dv


S5_STRIP = SSM_N // 4
S5_CH = SSM_WIDTH // 4
S5_CHUNK = 512


def _scan_chunk(xr_ref, xi_ref, ar, ai, cre_s, cim_s, reverse, unroll=4):
    t = xr_ref.shape[0]
    ng = t // 8
    edge = 0 if reverse else 7
    row8 = lax.broadcasted_iota(jnp.int32, (8, 1), 0)

    def grp(g):
        return pl.ds(pl.multiple_of(g * 8, 8), 8)

    def group_scan(xr, xi):
        pr, pi = ar, ai
        for d in (1, 2, 4):
            own = ((row8 < 8 - d) if reverse else (row8 >= d)).astype(F32)
            shift = (8 - d) if reverse else d
            sr, si = pltpu.roll(xr, shift, 0) * own, pltpu.roll(xi, shift, 0) * own
            xr, xi = xr + pr * sr - pi * si, xi + pr * si + pi * sr
            pr, pi = pr * pr - pi * pi, 2.0 * pr * pi
        return xr, xi

    def local(g, _):
        xr, xi = group_scan(xr_ref[grp(g), :], xi_ref[grp(g), :])
        xr_ref[grp(g), :] = xr
        xi_ref[grp(g), :] = xi
        return 0

    lax.fori_loop(0, ng, local, 0, unroll=unroll)
    e = (row8 == 7 - edge).astype(F32)
    pw_r, pw_i = group_scan(e * ar, e * ai)
    e_out = (row8 == edge).astype(F32)
    a8_r = jnp.sum(pw_r * e_out, axis=0, keepdims=True)
    a8_i = jnp.sum(pw_i * e_out, axis=0, keepdims=True)
    cr, ci = cre_s[...], cim_s[...]
    for i in range(ng):
        g = ng - 1 - i if reverse else i
        rows = slice(g * 8, g * 8 + 8)
        lr, li = xr_ref[g * 8 + edge:g * 8 + edge + 1, :], xi_ref[g * 8 + edge:g * 8 + edge + 1, :]
        xr_ref[rows, :] = xr_ref[rows, :] + pw_r * cr - pw_i * ci
        xi_ref[rows, :] = xi_ref[rows, :] + pw_r * ci + pw_i * cr
        cr, ci = lr + a8_r * cr - a8_i * ci, li + a8_r * ci + a8_i * cr
    cre_s[...] = cr
    cim_s[...] = ci


def _s5_fwd(name, u, bb_re, bb_im, cc_re, cc_imn, a_re, a_im):
    l = u.shape[0]
    t = min(S5_CHUNK, l)

    def body(u_ref, bbr_ref, bbi_ref, ccr_ref, cci_ref, ar_ref, ai_ref, sre_ref, sim_ref, y_ref, cre_s, cim_s):
        @pl.when(pl.program_id(1) == 0)
        def _():
            cre_s[...] = jnp.zeros_like(cre_s)
            cim_s[...] = jnp.zeros_like(cim_s)

        uv = u_ref[...]
        sre_ref[...] = _dot(uv, bbr_ref[...], _NN)
        sim_ref[...] = _dot(uv, bbi_ref[...], _NN)
        _scan_chunk(sre_ref, sim_ref, ar_ref[...], ai_ref[...], cre_s, cim_s, reverse=False)
        y_ref[...] = _dot(sre_ref[...], ccr_ref[...], _NN) + _dot(sim_ref[...], cci_ref[...], _NN)

    rows_ch = pl.BlockSpec((t, S5_CH), lambda w, c: (c, w))
    rows_st = pl.BlockSpec((t, S5_STRIP), lambda w, c: (c, w))
    b_blk = pl.BlockSpec((S5_CH, S5_STRIP), lambda w, c: (w, w))
    c_blk = pl.BlockSpec((S5_STRIP, S5_CH), lambda w, c: (w, w))
    a_blk = pl.BlockSpec((1, S5_STRIP), lambda w, c: (0, w))
    return pl.pallas_call(
        body, name=name, grid=(SSM_N // S5_STRIP, l // t),
        in_specs=[rows_ch, b_blk, b_blk, c_blk, c_blk, a_blk, a_blk], out_specs=[rows_st, rows_st, rows_ch],
        out_shape=[jax.ShapeDtypeStruct((l, SSM_N), F32), jax.ShapeDtypeStruct((l, SSM_N), F32),
                   jax.ShapeDtypeStruct((l, SSM_WIDTH), F32)],
        scratch_shapes=[pltpu.VMEM((1, S5_STRIP), F32), pltpu.VMEM((1, S5_STRIP), F32)],
        compiler_params=_params(),
    )(u, bb_re, bb_im, cc_re, cc_imn, a_re, a_im)


def _s5_bwd(name, dy, du_skip, u, s_re, s_im, bb_re, bb_im, cc_re, cc_imn, a_re, a_im_neg):
    l = u.shape[0]
    t = min(S5_CHUNK, l)
    nc = l // t

    def body(dy_ref, skip_ref, u_ref, sre_ref, sim_ref, hre_ref, him_ref, bbr_ref, bbi_ref, ccr_ref, cci_ref, ar_ref,
             ai_ref, du_ref, dar_ref, dai_ref, dbr_ref, dbi_ref, dcr_ref, dci_ref, lr_s, li_s, cre_s, cim_s):
        c = pl.program_id(1)

        @pl.when(c == 0)
        def _():
            for r in (cre_s, cim_s, dar_ref, dai_ref, dbr_ref, dbi_ref, dcr_ref, dci_ref):
                r[...] = jnp.zeros_like(r)

        dyv, uv = dy_ref[...], u_ref[...]
        lr_s[...] = _dot(dyv, ccr_ref[...], _NT)
        li_s[...] = _dot(dyv, cci_ref[...], _NT)
        _scan_chunk(lr_s, li_s, ar_ref[...], ai_ref[...], cre_s, cim_s, reverse=True)
        lam_r, lam_i = lr_s[...], li_s[...]
        s_r, s_i = sre_ref[...], sim_ref[...]
        tc = nc - 1 - c
        sp_r, sp_i = _shift_down(s_r, hre_ref, tc, 1), _shift_down(s_i, him_ref, tc, 1)
        dar_ref[...] += jnp.sum(lam_r * sp_r + lam_i * sp_i, axis=0, keepdims=True)
        dai_ref[...] += jnp.sum(lam_i * sp_r - lam_r * sp_i, axis=0, keepdims=True)
        dbr_ref[0] += _dot(uv, lam_r, _TN)
        dbi_ref[0] += _dot(uv, lam_i, _TN)
        dcr_ref[0] += _dot(s_r, dyv, _TN)
        dci_ref[0] += _dot(s_i, dyv, _TN)
        du = _dot(lam_r, bbr_ref[...], _NT) + _dot(lam_i, bbi_ref[...], _NT) + skip_ref[...]
        du_ref[...] = du.astype(du_ref.dtype)

    rows_ch = pl.BlockSpec((t, S5_CH), lambda w, c: (nc - 1 - c, w))
    rows_st = pl.BlockSpec((t, S5_STRIP), lambda w, c: (nc - 1 - c, w))
    halo = pl.BlockSpec((8, S5_STRIP), lambda w, c: (jnp.maximum((nc - 1 - c) * (t // 8) - 1, 0), w))
    b_blk = pl.BlockSpec((S5_CH, S5_STRIP), lambda w, c: (w, w))
    c_blk = pl.BlockSpec((S5_STRIP, S5_CH), lambda w, c: (w, w))
    a_blk = pl.BlockSpec((1, S5_STRIP), lambda w, c: (0, w))
    nw = SSM_N // S5_STRIP
    return pl.pallas_call(
        body, name=name, grid=(nw, nc),
        in_specs=[rows_ch, rows_ch, rows_ch, rows_st, rows_st, halo, halo, b_blk, b_blk, c_blk, c_blk, a_blk, a_blk],
        out_specs=[rows_ch, a_blk, a_blk, pl.BlockSpec((1, S5_CH, S5_STRIP), lambda w, c: (w, 0, 0)),
                   pl.BlockSpec((1, S5_CH, S5_STRIP), lambda w, c: (w, 0, 0)),
                   pl.BlockSpec((1, S5_STRIP, S5_CH), lambda w, c: (w, 0, 0)),
                   pl.BlockSpec((1, S5_STRIP, S5_CH), lambda w, c: (w, 0, 0))],
        out_shape=[jax.ShapeDtypeStruct((l, SSM_WIDTH), BF16), jax.ShapeDtypeStruct((1, SSM_N), F32),
                   jax.ShapeDtypeStruct((1, SSM_N), F32), jax.ShapeDtypeStruct((nw, S5_CH, S5_STRIP), F32),
                   jax.ShapeDtypeStruct((nw, S5_CH, S5_STRIP), F32), jax.ShapeDtypeStruct((nw, S5_STRIP, S5_CH), F32),
                   jax.ShapeDtypeStruct((nw, S5_STRIP, S5_CH), F32)],
        scratch_shapes=[pltpu.VMEM((t, S5_STRIP), F32), pltpu.VMEM((t, S5_STRIP), F32),
                        pltpu.VMEM((1, S5_STRIP), F32), pltpu.VMEM((1, S5_STRIP), F32)],
        compiler_params=_params(),
    )(dy, du_skip, u, s_re, s_im, s_re, s_im, bb_re, bb_im, cc_re, cc_imn, a_re, a_im_neg)


def _disc_math(lr, li, ldt, br, bi):
    dt = jnp.exp(ldt)
    mag = jnp.exp(lr * dt)
    a_re, a_im = mag * jnp.cos(li * dt), mag * jnp.sin(li * dt)
    den = lr * lr + li * li
    e_re, e_im = a_re - 1.0, a_im
    f_re = (e_re * lr + e_im * li) / den
    f_im = (e_im * lr - e_re * li) / den
    return a_re, a_im, f_re * br - f_im * bi, f_re * bi + f_im * br


def _disc_fwd(lr, li, ldt, br, bi):
    def body(lr_ref, li_ref, ldt_ref, br_ref, bi_ref, are_ref, aim_ref, bbr_ref, bbi_ref):
        a_re, a_im, bb_re, bb_im = _disc_math(lr_ref[...], li_ref[...], ldt_ref[...], br_ref[...], bi_ref[...])
        are_ref[...] = a_re
        aim_ref[...] = a_im
        bbr_ref[...] = bb_re
        bbi_ref[...] = bb_im

    col = jax.ShapeDtypeStruct(lr.shape, F32)
    mat = jax.ShapeDtypeStruct(br.shape, F32)
    return pl.pallas_call(body, name="s5_disc_fwd", out_shape=[col, col, mat, mat], compiler_params=_params())(
        lr, li, ldt, br, bi)


def _disc_bwd(lr, li, ldt, br, bi, da_re, da_im, dbb_re, dbb_im):
    def body(lr_ref, li_ref, ldt_ref, br_ref, bi_ref, g0, g1, g2, g3, o0, o1, o2, o3, o4):
        _, vjp = jax.vjp(_disc_math, lr_ref[...], li_ref[...], ldt_ref[...], br_ref[...], bi_ref[...])
        grads = vjp((g0[...], g1[...], g2[...], g3[...]))
        for o, g in zip((o0, o1, o2, o3, o4), grads):
            o[...] = g

    col = jax.ShapeDtypeStruct(lr.shape, F32)
    mat = jax.ShapeDtypeStruct(br.shape, F32)
    return pl.pallas_call(body, name="s5_disc_bwd", out_shape=[col, col, col, mat, mat], compiler_params=_params())(
        lr, li, ldt, br, bi, da_re, da_im, dbb_re, dbb_im)


N_CHIP = 4


def _place():
    x, y, c = lax.axis_index("x"), lax.axis_index("y"), lax.axis_index("c")
    return (x, y, c), (x, y, 1 - c), [(1 - x, y), (x, 1 - y), (1 - x, 1 - y)]


def _lin(px, py, pc):
    return 4 * px + 2 * py + pc


def _remote(src, dst, sems, k, dev):
    return pltpu.make_async_remote_copy(src_ref=src, dst_ref=dst, send_sem=sems[0].at[k], recv_sem=sems[1].at[k],
                                        device_id=dev, device_id_type=pl.DeviceIdType.MESH)


def _hbm_call(body, name, srcs, out_shapes, nsem):
    n = len(srcs)
    hbm = pl.BlockSpec(memory_space=pltpu.HBM)

    def wrapped(*refs):
        body(refs[:n], refs[n:2 * n], (refs[2 * n], refs[2 * n + 1]), refs[2 * n + 2])

    return pl.pallas_call(
        wrapped, name=name, in_specs=[hbm] * n, out_specs=[hbm] * n, out_shape=out_shapes,
        scratch_shapes=[pltpu.SemaphoreType.DMA((nsem * n,)), pltpu.SemaphoreType.DMA((nsem * n,)),
                        pltpu.SemaphoreType.DMA((n,))],
        compiler_params=pltpu.CompilerParams(has_side_effects=True),
    )(*srcs)


def _gather_all(name, srcs):
    def body(s_refs, o_refs, sems, loc_sems):
        me, sib, chips = _place()
        c = me[2]
        started, local = [], []
        for g, (s_ref, o_ref) in enumerate(zip(s_refs, o_refs)):
            slot = lambda dev, o_ref=o_ref: o_ref.at[_lin(*dev)]
            mine = pltpu.make_async_copy(s_ref, slot(me), loc_sems.at[g])
            mine.start()
            local.append(mine)
            first = [_remote(s_ref, slot(me), sems, 7 * g, sib)]
            first += [_remote(s_ref, slot(me), sems, 7 * g + 1 + j, (*chip, c)) for j, chip in enumerate(chips)]
            for cp in first:
                cp.start()
            started += first
        for g, (s_ref, o_ref) in enumerate(zip(s_refs, o_refs)):
            slot = lambda dev, o_ref=o_ref: o_ref.at[_lin(*dev)]
            for j, chip in enumerate(chips):
                _remote(s_ref, slot((*chip, c)), sems, 7 * g + 1 + j, me).wait_recv()
                cp = _remote(slot((*chip, c)), slot((*chip, c)), sems, 7 * g + 4 + j, sib)
                cp.start()
                started.append(cp)
        for g, (s_ref, o_ref) in enumerate(zip(s_refs, o_refs)):
            slot = lambda dev, o_ref=o_ref: o_ref.at[_lin(*dev)]
            _remote(s_ref, slot(sib), sems, 7 * g, me).wait_recv()
            for j, chip in enumerate(chips):
                _remote(s_ref, slot((*chip, 1 - c)), sems, 7 * g + 4 + j, me).wait_recv()
        for cp in started:
            cp.wait_send()
        for cp in local:
            cp.wait()

    return _hbm_call(body, name, srcs, [jax.ShapeDtypeStruct((N_DEV,) + s.shape, s.dtype) for s in srcs], 7)


def _pair_exchange(name, sends):
    def body(s_refs, o_refs, sems, loc_sems):
        me, sib, _ = _place()
        c = me[2]
        copies = [_remote(s_ref.at[2 * q + (1 - c)], o_ref.at[q], sems, N_CHIP * g + q, sib)
                  for g, (s_ref, o_ref) in enumerate(zip(s_refs, o_refs)) for q in range(N_CHIP)]
        for cp in copies:
            cp.start()
        for cp in copies:
            cp.wait()

    return _hbm_call(body, name, sends, [jax.ShapeDtypeStruct((N_CHIP,) + s.shape[1:], s.dtype) for s in sends],
                     N_CHIP)


def _row_block(r, cap):
    best = 8
    for m in range(8, min(r, cap) + 1, 8):
        if r % m == 0:
            best = m
    return best


def _pair_sum(name, send, got):
    _, r, cols = send.shape
    bm = _row_block(r, max(8, 256 * 1024 // cols))
    core = lax.axis_index("c").astype(jnp.int32).reshape(1)

    def body(core_ref, a_ref, b_ref, o_ref):
        o_ref[...] = a_ref[...] + b_ref[...]

    grid_spec = pltpu.PrefetchScalarGridSpec(
        num_scalar_prefetch=1, grid=(N_CHIP, r // bm),
        in_specs=[pl.BlockSpec((1, bm, cols), lambda q, i, cr: (2 * q + cr[0], i, 0)),
                  pl.BlockSpec((1, bm, cols), lambda q, i, cr: (q, i, 0))],
        out_specs=pl.BlockSpec((1, bm, cols), lambda q, i, cr: (q, i, 0)))
    return pl.pallas_call(body, name=name, grid_spec=grid_spec,
                          out_shape=jax.ShapeDtypeStruct((N_CHIP, r, cols), F32), compiler_params=_params())(
        core, send, got)


def _chip_exchange(name, parts):
    def body(p_refs, o_refs, sems, loc_sems):
        me, _, chips = _place()
        c = me[2]
        chip_id = lambda chip: 2 * chip[0] + chip[1]
        my_chip = chip_id(me)
        local, copies = [], []
        for g, (p_ref, o_ref) in enumerate(zip(p_refs, o_refs)):
            mine = pltpu.make_async_copy(p_ref.at[my_chip], o_ref.at[my_chip], loc_sems.at[g])
            mine.start()
            local.append(mine)
            for j, chip in enumerate(chips):
                cp = _remote(p_ref.at[chip_id(chip)], o_ref.at[my_chip], sems, 3 * g + j, (*chip, c))
                cp.start()
                copies.append(cp)
        for g, (p_ref, o_ref) in enumerate(zip(p_refs, o_refs)):
            for j, chip in enumerate(chips):
                _remote(p_ref.at[my_chip], o_ref.at[chip_id(chip)], sems, 3 * g + j, me).wait_recv()
        for cp in copies:
            cp.wait_send()
        for cp in local:
            cp.wait()

    return _hbm_call(body, name, parts, [jax.ShapeDtypeStruct(p.shape, p.dtype) for p in parts], 3)


def _adamw(name, rcv, w, m, v):
    r, c = w.shape
    nslot = rcv.shape[0]
    bm = _row_block(r, max(8, 256 * 1024 // c))

    def body(rcv_ref, w_ref, m_ref, v_ref, g_ref, d_ref, m2_ref, v2_ref):
        g = rcv_ref[0]
        for s in range(1, nslot):
            g = g + rcv_ref[s]
        m2 = ADAM_B1 * m_ref[...] + (1.0 - ADAM_B1) * g
        v2 = ADAM_B2 * v_ref[...] + (1.0 - ADAM_B2) * (g * g)
        m_hat = m2 / (1.0 - ADAM_B1 ** ADAM_STEP)
        v_hat = v2 / (1.0 - ADAM_B2 ** ADAM_STEP)
        g_ref[...] = g
        d_ref[...] = -ADAM_LR * (m_hat / (jnp.sqrt(v_hat) + ADAM_EPS) + ADAM_WD * w_ref[...])
        m2_ref[...] = m2
        v2_ref[...] = v2

    blk = pl.BlockSpec((bm, c), lambda i: (i, 0))
    out = jax.ShapeDtypeStruct((r, c), F32)
    return pl.pallas_call(
        body, name=name, grid=(r // bm,),
        in_specs=[pl.BlockSpec((nslot, bm, c), lambda i: (0, i, 0)), blk, blk, blk], out_specs=[blk] * 4,
        out_shape=[out] * 4, compiler_params=_params(),
    )(rcv, w, m, v)


IN_SHARD = IN_WIDTH // N_DEV
UP_SHARD = 2 * D_FF // N_DEV
GROUPS = (
    ("g128", LANES, (("w_q_b", Q_LORA, D_QK), ("w_kv_b", KV_LORA, D_NOPE + D_V), ("w_o_mla", MLA_HEADS * D_V, LANES),
                     ("w_o_ssm", SSM_WIDTH, LANES), ("w_o_cross", X_WIDTH, LANES))),
    ("g512", SSM_WIDTH, (("w_glu", SSM_WIDTH // N_DEV, SSM_WIDTH),)),
    ("g1024", D_MODEL, (("w_mem_kv", D_MODEL // N_DEV, D_MODEL), ("w_out", D_MODEL // N_DEV, D_MODEL),
                        ("w_down", D_FF // N_DEV, D_MODEL))),
    ("g640", 640, (("w_in", D_MODEL, IN_SHARD),)),
    ("g768", 768, (("w_up", D_MODEL, UP_SHARD),)),
    ("gconv", 768, (("conv_w", 3, UP_SHARD),)),
)
REP_W = 1024


def _rows8(a):
    return -(-a // 8) * 8


def _group_rows(params, depth):
    off, r = {}, 0
    for n, a, _ in params:
        for li in range(depth):
            off[(n, li)] = r
            r += _rows8(a)
    return off, r


def _group_local(width, params, vals, depth, dtype):
    blocks = []
    for n, a, b in params:
        for li in range(depth):
            blocks.append(jnp.pad(vals[n][li].astype(dtype), ((0, _rows8(a) - a), (0, width - b))))
    return jnp.concatenate(blocks, axis=0)


def _rep_rows(flat):
    n = flat.shape[-1]
    per = REP_W * 64
    tot = -(-n // per) * per
    flat = jnp.pad(flat, [(0, 0)] * (flat.ndim - 1) + [(0, tot - n)])
    return flat.reshape(flat.shape[:-1] + (tot // REP_W, REP_W))


IN_SEGS = (
    (0, Q_LORA + KV_LORA, "wqkv", 0),
    (Q_LORA + KV_LORA, Q_LORA + KV_LORA + D_ROPE, "wqkv", Q_LORA + KV_LORA + KR_LO),
    (Q_LORA + KV_LORA + D_ROPE, Q_LORA + KV_LORA + D_ROPE + SSM_WIDTH, "w_u", 0),
    (Q_LORA + KV_LORA + D_ROPE + SSM_WIDTH, Q_LORA + KV_LORA + D_ROPE + SSM_WIDTH + X_WIDTH, "w_xq", 0),
    (Q_LORA + KV_LORA + D_ROPE + SSM_WIDTH + X_WIDTH, IN_WIDTH, "w_g", 0),
)
IN_PARTS = (("wqkv", QKV_W), ("w_u", SSM_WIDTH), ("w_xq", X_WIDTH), ("w_g", 3 * D_MODEL))


def _in_pieces():
    out = []
    for d in range(N_DEV):
        for lo, hi, part, dst in IN_SEGS:
            s, e = max(lo, d * IN_SHARD), min(hi, (d + 1) * IN_SHARD)
            if s < e:
                out.append((d, s - d * IN_SHARD, e - s, [p for p, _ in IN_PARTS].index(part), dst + s - lo))
    return out


def _unpack_w_in(gathered, row0, bm=256):
    def body(x_ref, *o_refs):
        o_refs[0][...] = jnp.zeros_like(o_refs[0])
        for d, src, n, part, dst in _in_pieces():
            o_refs[part][:, dst:dst + n] = x_ref[d, :, src:src + n]

    return pl.pallas_call(
        body, name="unpack_w_in", grid=(D_MODEL // bm,),
        in_specs=[pl.BlockSpec((N_DEV, bm, gathered.shape[2]), lambda i: (0, row0 // bm + i, 0))],
        out_specs=[pl.BlockSpec((bm, w), lambda i: (i, 0)) for _, w in IN_PARTS],
        out_shape=[jax.ShapeDtypeStruct((D_MODEL, w), gathered.dtype) for _, w in IN_PARTS], compiler_params=_params(),
    )(gathered)


def _pack_w_in(parts, bm=256):
    def body(*refs):
        o_ref = refs[-1]
        o_ref[...] = jnp.zeros_like(o_ref)
        for d, src, n, part, dst in _in_pieces():
            o_ref[d, :, src:src + n] = refs[part][:, dst:dst + n]

    return pl.pallas_call(
        body, name="pack_w_in", grid=(D_MODEL // bm,),
        in_specs=[pl.BlockSpec((bm, w), lambda i: (i, 0)) for _, w in IN_PARTS],
        out_specs=pl.BlockSpec((N_DEV, bm, 640), lambda i: (0, i, 0)),
        out_shape=jax.ShapeDtypeStruct((N_DEV, D_MODEL, 640), F32), compiler_params=_params(),
    )(*parts)


def _unpack_w_up(gathered, row0, bm=256):
    def body(x_ref, o_ref):
        for d in range(N_DEV):
            o_ref[:, d * UP_SHARD:(d + 1) * UP_SHARD] = x_ref[d, :, :UP_SHARD]

    return pl.pallas_call(
        body, name="unpack_w_up", grid=(D_MODEL // bm,),
        in_specs=[pl.BlockSpec((N_DEV, bm, gathered.shape[2]), lambda i: (0, row0 // bm + i, 0))],
        out_specs=pl.BlockSpec((bm, 2 * D_FF), lambda i: (i, 0)),
        out_shape=jax.ShapeDtypeStruct((D_MODEL, 2 * D_FF), gathered.dtype), compiler_params=_params(),
    )(gathered)


def _pack_w_up(dw_g, dw_v, bm=256):
    half = N_DEV // 2

    def body(g_ref, v_ref, o_ref):
        o_ref[...] = jnp.zeros_like(o_ref)
        for d in range(N_DEV):
            src = g_ref if d < half else v_ref
            c0 = (d % half) * UP_SHARD
            o_ref[d, :, :UP_SHARD] = src[:, c0:c0 + UP_SHARD]

    blk = pl.BlockSpec((bm, D_FF), lambda i: (i, 0))
    return pl.pallas_call(
        body, name="pack_w_up", grid=(D_MODEL // bm,), in_specs=[blk, blk],
        out_specs=pl.BlockSpec((N_DEV, bm, 768), lambda i: (0, i, 0)),
        out_shape=jax.ShapeDtypeStruct((N_DEV, D_MODEL, 768), F32), compiler_params=_params(),
    )(dw_g, dw_v)


def _cols_to_rows(full):
    a, nb = full.shape
    return full.reshape(a, N_DEV, nb // N_DEV).transpose(1, 0, 2)


def _rows_to_cols(blocks):
    n, a, b = blocks.shape
    return blocks.transpose(1, 0, 2).reshape(a, n * b)


def _block_diag_in(bb):
    b3 = bb.reshape(SSM_GROUPS, SSM_STATE, SSM_GROUP_CH).transpose(0, 2, 1)
    eye = jnp.eye(SSM_GROUPS, dtype=bb.dtype)
    return (b3[:, :, None, :] * eye[:, None, :, None]).reshape(SSM_WIDTH, SSM_N)


def _block_diag_out(cc):
    c3 = cc.transpose(0, 2, 1)
    eye = jnp.eye(SSM_GROUPS, dtype=cc.dtype)
    return (c3[:, :, None, :] * eye[:, None, :, None]).reshape(SSM_N, SSM_WIDTH)


def _diag_blocks(mats, rows_per, cols_per):
    nw = mats.shape[0]
    per = SSM_GROUPS // nw
    m5 = mats.reshape(nw, per, rows_per, per, cols_per)
    eye = jnp.eye(per, dtype=mats.dtype)
    return jnp.sum(m5 * eye[None, :, None, :, None], axis=3).reshape(SSM_GROUPS, rows_per, cols_per)


def _layer_weights(gath, offs, li):
    def blk(grp, n, a):
        r0 = offs[grp][(n, li)]
        return gath[grp][:, r0:r0 + a, :]

    o = dict(zip([p for p, _ in IN_PARTS], _unpack_w_in(gath["g640"], offs["g640"][("w_in", li)])))
    o["wqb"] = _rows_to_cols(blk("g128", "w_q_b", Q_LORA))
    wkv = _rows_to_cols(blk("g128", "w_kv_b", KV_LORA)).reshape(KV_LORA, MLA_HEADS, D_NOPE + D_V)
    o["wk"] = jnp.pad(wkv[:, :, :D_NOPE], ((0, 0), (0, 0), (0, LANES - D_NOPE))).reshape(KV_LORA, MLA_HEADS * LANES)
    o["wv"] = jnp.pad(wkv[:, :, D_NOPE:], ((0, 0), (0, 0), (0, LANES - D_V))).reshape(KV_LORA, MLA_HEADS * LANES)
    o["wv_t"] = o["wv"].T
    wo = _rows_to_cols(blk("g128", "w_o_mla", MLA_HEADS * D_V)).reshape(MLA_HEADS, D_V, D_MODEL)
    o["wo_mla"] = jnp.pad(wo, ((0, 0), (0, LANES - D_V), (0, 0))).reshape(MLA_HEADS * LANES, D_MODEL)
    o["w_o_ssm"] = _rows_to_cols(blk("g128", "w_o_ssm", SSM_WIDTH))
    o["w_o_cross"] = _rows_to_cols(blk("g128", "w_o_cross", X_WIDTH))
    o["w_glu"] = blk("g512", "w_glu", SSM_WIDTH // N_DEV).reshape(SSM_WIDTH, SSM_WIDTH)
    o["w_mem_kv"] = blk("g1024", "w_mem_kv", D_MODEL // N_DEV).reshape(D_MODEL, 2 * X_WIDTH)
    o["w_out"] = blk("g1024", "w_out", D_MODEL // N_DEV).reshape(D_MODEL, D_MODEL)
    o["w_down"] = blk("g1024", "w_down", D_FF // N_DEV).reshape(D_FF, D_MODEL)
    o["w_up"] = _unpack_w_up(gath["g768"], offs["g768"][("w_up", li)])
    o["w_up_g"], o["w_up_v"] = o["w_up"][:, :D_FF], o["w_up"][:, D_FF:]
    o["conv_w"] = _rows_to_cols(blk("gconv", "conv_w", 3)[:, :, :UP_SHARD])
    return o


def _row(v):
    return v.reshape(1, -1).astype(F32)


def _pad_lanes(v, n=LANES):
    return jnp.pad(v, (0, n - v.shape[0])).reshape(1, n).astype(F32)


def _layer_fwd(x, mem, tabs, w, p):
    l = x.shape[0]
    rope_c, rope_sa, rope_sb = tabs
    s = {"x": x}
    g_mix, g_qa, g_kva = _row(p["norm_mix_g"]), _row(p["q_a_norm_g"]), _row(p["kv_a_norm_g"])
    g_q, g_k = _pad_lanes(p["q_norm_g"]), _pad_lanes(p["k_norm_g"])

    (h,) = _rowwise("rms_mix", lambda i, r, c: (_rms_f(r[0][...], c[0][...], D_MODEL),), l, 512,
                    [(x, D_MODEL, 0, "row")], [g_mix], [(D_MODEL, BF16)])
    pqkv = _mm("proj_qkv", [(h, w["wqkv"])])
    u = _mm("proj_u", [(h, w["w_u"])])
    xq = _mm("proj_xq", [(h, w["w_xq"])])
    gl = _mm("proj_gate", [(h, w["w_g"])], bm=1024, bn_cap=1024)
    s.update(h=h, pqkv=pqkv, u=u, xq=xq, gl=gl)

    def prep_a(i, r, c):
        return (_rms_f(r[0][:, :Q_LORA], c[0][...], Q_LORA),
                _rms_f(r[0][:, Q_LORA:Q_LORA + KV_LORA], c[1][...], KV_LORA))

    nq, nkv = _rowwise("mla_prep_a", prep_a, l, 512, [(pqkv, QKV_W, 0, "row")], [g_qa, g_kva],
                       [(Q_LORA, BF16), (KV_LORA, BF16)])
    q_raw = _mm("mla_q_b", [(nq, w["wqb"])], bn_cap=1024)
    k_raw = _mm("mla_k_b", [(nkv, w["wk"])], bn_cap=1024)
    v_mla = _mm("mla_v_b", [(nkv, w["wv"])], out_dtype=BF16, bn_cap=1024)
    vt_mla = _mm("mla_vt_b", [(w["wv_t"], nkv)], trans_b=True, out_dtype=BF16, bn_cap=1024)

    def prep_b(i, r, c):
        q_ref, k_ref, kr_ref, c_ref, sa_ref, sb_ref = r
        rc, sa, sb, kr = c_ref[...], sa_ref[...], sb_ref[...], kr_ref[...]
        qs, ks = [], []
        for hd in range(MLA_HEADS):
            cols = slice(hd * LANES, (hd + 1) * LANES)
            qs.append(_rope_f(_rms_f(q_ref[:, cols], c[0][...], D_QK), rc, sa, sb))
            ks.append(_rope_f(_rms_f(k_ref[:, cols] + kr, c[1][...], D_QK), rc, sa, sb))
        return jnp.concatenate(qs, axis=1), jnp.concatenate(ks, axis=1)

    hw = MLA_HEADS * LANES
    kr_blk = (Q_LORA + KV_LORA) // LANES
    tab_ins = [(rope_c, LANES, 0, "row"), (rope_sa, LANES, 0, "row"), (rope_sb, LANES, 0, "row")]
    q, k = _rowwise("mla_prep_b", prep_b, l, 256,
                    [(q_raw, hw, 0, "row"), (k_raw, hw, 0, "row"), (pqkv, LANES, kr_blk, "row")] + tab_ins,
                    [g_q, g_k], [(hw, BF16), (hw, BF16)])
    o_a, lse_a = _attn_fwd("mla_attn_fwd", q, k, vt_mla, qoff=0, koff=0, voff=0, heads=MLA_HEADS, causal=True,
                           scale=D_QK ** -0.5, bq=1024, bk=1024)
    ya = _mm("mla_o", [(o_a, w["wo_mla"])], bn_cap=1024)
    s.update(nq=nq, nkv=nkv, q_raw=q_raw, k_raw=k_raw, v_mla=v_mla, q=q, k=k, o_a=o_a, lse_a=lse_a, ya=ya)

    lr = p["ssm_lambda_re"].reshape(SSM_N, 1)
    li = p["ssm_lambda_im"].reshape(SSM_N, 1)
    ldt = jnp.repeat(p["ssm_log_dt"], SSM_STATE).reshape(SSM_N, 1)
    br = p["ssm_b_re"].reshape(SSM_N, SSM_GROUP_CH)
    bi = p["ssm_b_im"].reshape(SSM_N, SSM_GROUP_CH)
    a_re, a_im, bb_re, bb_im = _disc_fwd(lr, li, ldt, br, bi)
    bb_re_d, bb_im_d = _block_diag_in(bb_re).astype(BF16), _block_diag_in(bb_im).astype(BF16)
    cc_re_d = _block_diag_out(p["ssm_c_re"]).astype(BF16)
    cc_imn_d = _block_diag_out(-p["ssm_c_im"]).astype(BF16)
    a_re_row, a_im_row = a_re.reshape(1, SSM_N), a_im.reshape(1, SSM_N)
    d_row = _row(p["ssm_d"])
    b_glu = _row(p["b_glu"])
    s_re, s_im, ypre = _s5_fwd("s5_fwd", u, bb_re_d, bb_im_d, cc_re_d, cc_imn_d, a_re_row, a_im_row)

    def ssm_y(i, r, c):
        return (_gelu(r[0][...] + c[0][...] * r[1][...]),)

    (y_b,) = _rowwise("s5_gelu", ssm_y, l, 512, [(ypre, SSM_WIDTH, 0, "row"), (u, SSM_WIDTH, 0, "row")], [d_row],
                      [(SSM_WIDTH, BF16)])
    z = _mm("s5_glu", [(y_b, w["w_glu"])])

    def ssm_out(i, r, c):
        y = _gelu(r[0][...] + c[0][...] * r[1][...])
        return (y * jax.nn.sigmoid(r[2][...] + c[1][...]),)

    (out_b,) = _rowwise("s5_glu_out", ssm_out, l, 512,
                        [(ypre, SSM_WIDTH, 0, "row"), (u, SSM_WIDTH, 0, "row"), (z, SSM_WIDTH, 0, "row")],
                        [d_row, b_glu], [(SSM_WIDTH, BF16)])
    yb = _mm("s5_o", [(out_b, w["w_o_ssm"])], bn_cap=1024)
    s.update(disc=(lr, li, ldt, br, bi), a_rows=(a_re_row, a_im_row), bb_d=(bb_re_d, bb_im_d),
             cc_d=(cc_re_d, cc_imn_d), s_re=s_re, s_im=s_im, ypre=ypre, y_b=y_b, z=z, out_b=out_b, yb=yb)

    g_mem, g_xq, g_xk = _row(p["mem_norm_g"]), _row(p["xq_norm_g"]), _row(p["xk_norm_g"])
    ml = mem.shape[0]
    (memn,) = _rowwise("rms_mem", lambda i, r, c: (_rms_f(r[0][...], c[0][...], D_MODEL),), ml, 256,
                       [(mem, D_MODEL, 0, "row")], [g_mem], [(D_MODEL, BF16)])
    kvm = _mm("cross_kv", [(memn, w["w_mem_kv"])], bn_cap=1024)

    def headnorm(i, r, c):
        return (jnp.concatenate([_rms_f(r[0][:, hd * LANES:(hd + 1) * LANES], c[0][...], X_HEAD_DIM)
                                 for hd in range(X_HEADS)], axis=1),)

    (xk,) = _rowwise("cross_k_norm", headnorm, ml, 256, [(kvm, X_WIDTH, 0, "row")], [g_xk], [(X_WIDTH, BF16)])
    (xqn,) = _rowwise("cross_q_norm", headnorm, l, 512, [(xq, X_WIDTH, 0, "row")], [g_xq], [(X_WIDTH, BF16)])
    xvt = kvm[:, X_WIDTH:].T.astype(BF16)
    o_c, lse_c = _attn_fwd("cross_attn_fwd", xqn, xk, xvt, qoff=0, koff=0, voff=0, heads=X_HEADS,
                           causal=False, scale=X_HEAD_DIM ** -0.5, bq=1024, bk=256)
    yc = _mm("cross_o", [(o_c, w["w_o_cross"])], bn_cap=1024)
    s.update(memn=memn, kvm=kvm, xk=xk, xqn=xqn, o_c=o_c, lse_c=lse_c, yc=yc)

    b_gate = _row(p["b_gate"])

    def merge(i, r, c):
        acc = None
        for br_ in range(3):
            g = jax.nn.sigmoid(r[br_][...] + c[0][:, br_ * D_MODEL:(br_ + 1) * D_MODEL])
            t = g * r[3 + br_][...]
            acc = t if acc is None else acc + t
        return (acc,)

    gate_ins = [(gl, D_MODEL, b_, "row") for b_ in range(3)]
    (merged,) = _rowwise("merge", merge, l, 256,
                         gate_ins + [(ya, D_MODEL, 0, "row"), (yb, D_MODEL, 0, "row"), (yc, D_MODEL, 0, "row")],
                         [b_gate], [(D_MODEL, BF16)])
    x1 = _mm("mix_out", [(merged, w["w_out"])], add=x, bn_cap=1024)
    s.update(merged=merged, x1=x1)

    g_ffn = _row(p["norm_ffn_g"])
    (h2,) = _rowwise("rms_ffn", lambda i, r, c: (_rms_f(r[0][...], c[0][...], D_MODEL),), l, 512,
                     [(x1, D_MODEL, 0, "row")], [g_ffn], [(D_MODEL, BF16)])
    up = _mm("ffn_up", [(h2, w["w_up"])], bm=1024, bn_cap=1408)
    conv_w = w["conv_w"]
    conv_b = _row(p["conv_b"])

    def conv_glu(i, r, c):
        cg = _conv(r[0], r[2], i, c[0], c[1], 0)
        cv = _conv(r[1], r[3], i, c[0], c[1], D_FF)
        return (cg * jax.nn.sigmoid(cg) * cv,)

    up_ins = [(up, D_FF, 0, "row"), (up, D_FF, 1, "row"), (up, D_FF, 0, "prev"), (up, D_FF, 1, "prev")]
    (act,) = _rowwise("ffn_conv_glu", conv_glu, l, 256, up_ins, [conv_w, conv_b], [(D_FF, BF16)])
    x2 = _mm("ffn_down", [(act, w["w_down"])], add=x1, bm=1024, bn_cap=1024)
    s.update(h2=h2, up=up, act=act, conv_w=conv_w, conv_b=conv_b)
    return x2, s


def _conv(x_ref, halo_ref, i, w_ref, b_ref, col0):
    x = x_ref[...]
    cols = slice(col0, col0 + D_FF)
    return (w_ref[0:1, cols] * _shift_down(x, halo_ref, i, 2) + w_ref[1:2, cols] * _shift_down(x, halo_ref, i, 1)
            + w_ref[2:3, cols] * x + b_ref[:, cols])


def _layer_bwd(dx2, dx2_b, s, mem, tabs, w, p):
    l = dx2.shape[0]
    rope_c, rope_sa, rope_sb = tabs
    x, x1 = s["x"], s["x1"]
    g = {}

    dact = _mm("ffn_down_dx", [(dx2_b, w["w_down"])], trans_b=True, bn_cap=1408)
    g["w_down"] = _mm_tn("ffn_down_dw", s["act"], dx2_b, bm_cap=1408).reshape(N_DEV, D_FF // N_DEV, D_MODEL)
    up = s["up"]
    nblk_c = l // min(256, l)

    def conv_bwd_a(i, r, c):
        outs, accs = [], []
        for half in range(2):
            x_ref, halo_ref = r[1 + half], r[3 + half]
            col0 = half * D_FF
            cols = slice(col0, col0 + D_FF)
            xv = x_ref[...]
            xm1, xm2 = _shift_down(xv, halo_ref, i, 1), _shift_down(xv, halo_ref, i, 2)
            cv = c[0][0:1, cols] * xm2 + c[0][1:2, cols] * xm1 + c[0][2:3, cols] * xv + c[1][:, cols]
            outs.append((cv, xv, xm1, xm2))
        (cg, xg, xg1, xg2), (cvv, xv, xv1, xv2) = outs
        sig = jax.nn.sigmoid(cg)
        da = r[0][...]
        dcv = da * (cg * sig)
        dcg = da * cvv * (sig * (1.0 + cg * (1.0 - sig)))
        for d, (x0, xa, xb) in ((dcg, (xg, xg1, xg2)), (dcv, (xv, xv1, xv2))):
            accs += [jnp.sum(d, axis=0, keepdims=True), jnp.sum(d * xb, axis=0, keepdims=True),
                     jnp.sum(d * xa, axis=0, keepdims=True), jnp.sum(d * x0, axis=0, keepdims=True)]
        return (dcg, dcv, *accs)

    up_ins = [(up, D_FF, 0, "row"), (up, D_FF, 1, "row"), (up, D_FF, 0, "prev"), (up, D_FF, 1, "prev")]
    res = _rowwise("ffn_conv_glu_bwd", conv_bwd_a, l, 256, [(dact, D_FF, 0, "row")] + up_ins,
                   [s["conv_w"], s["conv_b"]], [(D_FF, F32), (D_FF, F32)], [(1, D_FF)] * 8)
    dcg, dcv = res[0], res[1]
    db_g, dw0_g, dw1_g, dw2_g, db_v, dw0_v, dw1_v, dw2_v = res[2:]
    g["conv_b"] = jnp.concatenate([db_g, db_v], axis=1)[0]
    g["conv_w"] = _cols_to_rows(jnp.concatenate(
        [jnp.concatenate([dw0_g, dw0_v], axis=1), jnp.concatenate([dw1_g, dw1_v], axis=1),
         jnp.concatenate([dw2_g, dw2_v], axis=1)], axis=0))

    def conv_bwd_b(i, r, c):
        outs = []
        for half in range(2):
            d = r[half][...]
            cols = slice(half * D_FF, (half + 1) * D_FF)
            outs.append(c[0][2:3, cols] * d + c[0][1:2, cols] * _shift_up(d, r[2 + half], i, nblk_c, 1)
                        + c[0][0:1, cols] * _shift_up(d, r[2 + half], i, nblk_c, 2))
        return tuple(outs)

    dup_g, dup_v = _rowwise("ffn_conv_bwd_x", conv_bwd_b, l, 256,
                            [(dcg, D_FF, 0, "row"), (dcv, D_FF, 0, "row"), (dcg, D_FF, 0, "next"),
                             (dcv, D_FF, 0, "next")], [s["conv_w"]], [(D_FF, BF16), (D_FF, BF16)])
    dh2 = _mm("ffn_up_dx", [(dup_g, w["w_up_g"]), (dup_v, w["w_up_v"])], trans_b=True, bn_cap=1024)
    g["w_up"] = _pack_w_up(_mm_tn("ffn_up_dw_g", s["h2"], dup_g), _mm_tn("ffn_up_dw_v", s["h2"], dup_v))

    def rms_bwd_res(i, r, c):
        dx, dg = _rms_b(r[0][...], c[0][...], r[1][...], D_MODEL)
        dx = dx + r[2][...]
        return (dx, dx, dg)

    dx1, dx1_b, g["norm_ffn_g"] = _rowwise(
        "rms_ffn_bwd", rms_bwd_res, l, 512, [(x1, D_MODEL, 0, "row"), (dh2, D_MODEL, 0, "row"), (dx2, D_MODEL, 0, "row")],
        [_row(p["norm_ffn_g"])], [(D_MODEL, F32), (D_MODEL, BF16)], [(1, D_MODEL)])

    dmerged = _mm("mix_out_dx", [(dx1_b, w["w_out"])], trans_b=True, bn_cap=1024)
    g["w_out"] = _mm_tn("mix_out_dw", s["merged"], dx1_b).reshape(N_DEV, D_MODEL // N_DEV, D_MODEL)
    gl, ya, yb, yc = s["gl"], s["ya"], s["yb"], s["yc"]

    def merge_bwd(i, r, c):
        dm = r[0][...]
        dys, dgs = [], []
        for b_ in range(3):
            gate = jax.nn.sigmoid(r[1 + b_][...] + c[0][:, b_ * D_MODEL:(b_ + 1) * D_MODEL])
            dys.append(dm * gate)
            dgs.append(dm * r[4 + b_][...] * (gate * (1.0 - gate)))
        dgl = jnp.concatenate(dgs, axis=1)
        return (*dys, dgl, jnp.sum(dgl, axis=0, keepdims=True))

    gate_ins = [(gl, D_MODEL, b_, "row") for b_ in range(3)]
    dya, dyb, dyc, dgl, db_gate = _rowwise(
        "merge_bwd", merge_bwd, l, 256,
        [(dmerged, D_MODEL, 0, "row")] + gate_ins + [(ya, D_MODEL, 0, "row"), (yb, D_MODEL, 0, "row"),
                                                     (yc, D_MODEL, 0, "row")],
        [_row(p["b_gate"])], [(D_MODEL, BF16)] * 3 + [(3 * D_MODEL, BF16)], [(1, 3 * D_MODEL)])
    g["b_gate"] = db_gate[0]

    do_c = _mm("cross_o_dx", [(dyc, w["w_o_cross"])], trans_b=True, out_dtype=BF16)
    g["w_o_cross"] = _cols_to_rows(_mm_tn("cross_o_dw", s["o_c"], dyc))
    kvm = s["kvm"]
    delta_c = _attn_delta("cross_attn_delta", s["o_c"], do_c, heads=X_HEADS, bq=2048)
    dxqn, dxk, dxv = _attn_bwd("cross_attn_bwd", s["xqn"], s["xk"], s["xk"].T, kvm, do_c, s["lse_c"], delta_c, qoff=0,
                               koff=0, voff=X_HEADS, heads=X_HEADS, causal=False, scale=X_HEAD_DIM ** -0.5, bq=1024,
                               bk=256)
    ml = mem.shape[0]

    def headnorm_bwd(i, r, c):
        dxs, dg = [], None
        for hd in range(X_HEADS):
            cols = slice(hd * LANES, (hd + 1) * LANES)
            dx_h, dg_h = _rms_b(r[0][:, cols], c[0][...], r[1][:, cols], X_HEAD_DIM)
            dxs.append(dx_h)
            dg = dg_h if dg is None else dg + dg_h
        return (jnp.concatenate(dxs, axis=1), dg)

    dxq, dg_xq = _rowwise("cross_q_norm_bwd", headnorm_bwd, l, 512,
                          [(s["xq"], X_WIDTH, 0, "row"), (dxqn, X_WIDTH, 0, "row")], [_row(p["xq_norm_g"])],
                          [(X_WIDTH, BF16)], [(1, X_HEAD_DIM)])
    dkvm_k, dg_xk = _rowwise("cross_k_norm_bwd", headnorm_bwd, ml, 256,
                             [(kvm, X_WIDTH, 0, "row"), (dxk, X_WIDTH, 0, "row")], [_row(p["xk_norm_g"])],
                             [(X_WIDTH, F32)], [(1, X_HEAD_DIM)])
    g["xq_norm_g"], g["xk_norm_g"] = dg_xq[0], dg_xk[0]
    dkvm = jnp.concatenate([dkvm_k, dxv], axis=1)
    g["w_mem_kv"] = _mm_tn("cross_kv_dw", s["memn"], dkvm).reshape(N_DEV, D_MODEL // N_DEV, 2 * X_WIDTH)
    dmemn = _mm("cross_kv_dx", [(dkvm, w["w_mem_kv"])], trans_b=True, bn_cap=1024)

    def rms_bwd_gain_only(i, r, c):
        return (_rms_b(r[0][...], c[0][...], r[1][...], D_MODEL)[1],)

    (dg_mem,) = _rowwise("rms_mem_bwd", rms_bwd_gain_only, ml, 256,
                         [(mem, D_MODEL, 0, "row"), (dmemn, D_MODEL, 0, "row")], [_row(p["mem_norm_g"])], [],
                         [(1, D_MODEL)])
    g["mem_norm_g"] = dg_mem[0]

    dout_b = _mm("s5_o_dx", [(dyb, w["w_o_ssm"])], trans_b=True)
    g["w_o_ssm"] = _cols_to_rows(_mm_tn("s5_o_dw", s["out_b"], dyb))
    ypre, u, z = s["ypre"], s["u"], s["z"]
    d_row, b_glu = _row(p["ssm_d"]), _row(p["b_glu"])
    yuz = [(ypre, SSM_WIDTH, 0, "row"), (u, SSM_WIDTH, 0, "row"), (z, SSM_WIDTH, 0, "row")]

    def glu_bwd_z(i, r, c):
        y = _gelu(r[1][...] + c[0][...] * r[2][...])
        sg = jax.nn.sigmoid(r[3][...] + c[1][...])
        dz = r[0][...] * y * (sg * (1.0 - sg))
        return (dz, jnp.sum(dz, axis=0, keepdims=True))

    dz, db_glu = _rowwise("s5_glu_bwd_z", glu_bwd_z, l, 512, [(dout_b, SSM_WIDTH, 0, "row")] + yuz, [d_row, b_glu],
                          [(SSM_WIDTH, BF16)], [(1, SSM_WIDTH)])
    g["b_glu"] = db_glu[0]
    g["w_glu"] = _mm_tn("s5_glu_dw", s["y_b"], dz).reshape(N_DEV, SSM_WIDTH // N_DEV, SSM_WIDTH)
    dy2 = _mm("s5_glu_dx", [(dz, w["w_glu"])], trans_b=True)

    def gelu_bwd(i, r, c):
        t = r[2][...] + c[0][...] * r[3][...]
        sg = jax.nn.sigmoid(r[4][...] + c[1][...])
        dt = (r[0][...] * sg + r[1][...]) * _gelu_grad(t)
        return (dt, c[0][...] * dt, jnp.sum(dt * r[3][...], axis=0, keepdims=True))

    dypre, du_skip, dd = _rowwise(
        "s5_gelu_bwd", gelu_bwd, l, 512, [(dout_b, SSM_WIDTH, 0, "row"), (dy2, SSM_WIDTH, 0, "row")] + yuz,
        [d_row, b_glu], [(SSM_WIDTH, BF16), (SSM_WIDTH, F32)], [(1, SSM_WIDTH)])
    g["ssm_d"] = dd.reshape(SSM_GROUPS, SSM_GROUP_CH)
    cc_re_d, cc_imn_d = s["cc_d"]
    bb_re_d, bb_im_d = s["bb_d"]
    a_re_row, a_im_row = s["a_rows"]
    s_re, s_im = s["s_re"], s["s_im"]
    du, da_re, da_im, dbb_re_d, dbb_im_d, dcc_re, dcc_imn = _s5_bwd(
        "s5_bwd", dypre, du_skip, u, s_re, s_im, bb_re_d, bb_im_d, cc_re_d, cc_imn_d, a_re_row, -a_im_row)
    g["ssm_c_re"] = _diag_blocks(dcc_re, SSM_STATE, SSM_GROUP_CH).transpose(0, 2, 1)
    g["ssm_c_im"] = -_diag_blocks(dcc_imn, SSM_STATE, SSM_GROUP_CH).transpose(0, 2, 1)
    dbb_re = _diag_blocks(dbb_re_d, SSM_GROUP_CH, SSM_STATE).transpose(0, 2, 1).reshape(SSM_N, SSM_GROUP_CH)
    dbb_im = _diag_blocks(dbb_im_d, SSM_GROUP_CH, SSM_STATE).transpose(0, 2, 1).reshape(SSM_N, SSM_GROUP_CH)
    dlr, dli, dldt, dbr, dbi = _disc_bwd(*s["disc"], da_re.reshape(SSM_N, 1), da_im.reshape(SSM_N, 1), dbb_re, dbb_im)
    g["ssm_lambda_re"] = dlr.reshape(SSM_GROUPS, SSM_STATE)
    g["ssm_lambda_im"] = dli.reshape(SSM_GROUPS, SSM_STATE)
    g["ssm_log_dt"] = dldt.reshape(SSM_GROUPS, SSM_STATE).sum(axis=1)
    g["ssm_b_re"] = dbr.reshape(SSM_GROUPS, SSM_STATE, SSM_GROUP_CH)
    g["ssm_b_im"] = dbi.reshape(SSM_GROUPS, SSM_STATE, SSM_GROUP_CH)

    do_a = _mm("mla_o_dx", [(dya, w["wo_mla"])], trans_b=True, out_dtype=BF16, bn_cap=1024)
    dwo = _mm_tn("mla_o_dw", s["o_a"], dya)
    g["w_o_mla"] = _cols_to_rows(dwo.reshape(MLA_HEADS, LANES, D_MODEL)[:, :D_V].reshape(MLA_HEADS * D_V, D_MODEL))
    delta_a = _attn_delta("mla_attn_delta", s["o_a"], do_a, heads=MLA_HEADS, bq=2048)
    dq, dk, dv = _attn_bwd("mla_attn_bwd", s["q"], s["k"], s["k"].T, s["v_mla"], do_a, s["lse_a"], delta_a, qoff=0,
                           koff=0, voff=0, heads=MLA_HEADS, causal=True, scale=D_QK ** -0.5, bq=1024, bk=1024)
    hw = MLA_HEADS * LANES
    kr_blk = (Q_LORA + KV_LORA) // LANES
    pqkv = s["pqkv"]
    g_q, g_k = _pad_lanes(p["q_norm_g"]), _pad_lanes(p["k_norm_g"])
    lane = lax.broadcasted_iota(jnp.int32, (1, LANES), 1)
    kr_mask = jnp.logical_and(lane >= KR_LO, lane < KR_LO + D_ROPE).astype(F32)

    def prep_b_bwd(i, r, c):
        dq_ref, dk_ref, q_ref, k_ref, kr_ref, c_ref, sa_ref, sb_ref = r
        rc, sa, sb, kr = c_ref[...], sa_ref[...], sb_ref[...], kr_ref[...]
        dqs, dks, dkr, dgq, dgk = [], [], None, None, None
        for hd in range(MLA_HEADS):
            cols = slice(hd * LANES, (hd + 1) * LANES)
            dxq, dgq_h = _rms_b(q_ref[:, cols], c[0][...], _rope_b(dq_ref[:, cols], rc, sa, sb), D_QK)
            dxk, dgk_h = _rms_b(k_ref[:, cols] + kr, c[1][...], _rope_b(dk_ref[:, cols], rc, sa, sb), D_QK)
            dqs.append(dxq)
            dks.append(dxk)
            dkr = dxk if dkr is None else dkr + dxk
            dgq = dgq_h if dgq is None else dgq + dgq_h
            dgk = dgk_h if dgk is None else dgk + dgk_h
        return (jnp.concatenate(dqs, axis=1), jnp.concatenate(dks, axis=1), dkr * c[2][...], dgq, dgk)

    tab_ins = [(rope_c, LANES, 0, "row"), (rope_sa, LANES, 0, "row"), (rope_sb, LANES, 0, "row")]
    dq_raw, dk_raw, dkr, dg_q, dg_k = _rowwise(
        "mla_prep_b_bwd", prep_b_bwd, l, 256,
        [(dq, hw, 0, "row"), (dk, hw, 0, "row"), (s["q_raw"], hw, 0, "row"), (s["k_raw"], hw, 0, "row"),
         (pqkv, LANES, kr_blk, "row")] + tab_ins, [g_q, g_k, kr_mask],
        [(hw, BF16), (hw, BF16), (LANES, F32)], [(1, LANES), (1, LANES)])
    g["q_norm_g"], g["k_norm_g"] = dg_q[0, :D_QK], dg_k[0, :D_QK]
    dnq = _mm("mla_q_b_dx", [(dq_raw, w["wqb"])], trans_b=True)
    dnkv = _mm("mla_kv_b_dx", [(dk_raw, w["wk"]), (dv, w["wv"])], trans_b=True)
    dwqb = _mm_tn("mla_q_b_dw", s["nq"], dq_raw)
    g["w_q_b"] = _cols_to_rows(dwqb)
    dwk = _mm_tn("mla_k_b_dw", s["nkv"], dk_raw).reshape(KV_LORA, MLA_HEADS, LANES)[:, :, :D_NOPE]
    dwv = _mm_tn("mla_v_b_dw", s["nkv"], dv).reshape(KV_LORA, MLA_HEADS, LANES)[:, :, :D_V]
    g["w_kv_b"] = jnp.concatenate([dwk, dwv], axis=2).transpose(1, 0, 2)

    def prep_a_bwd(i, r, c):
        dcq, dgqa = _rms_b(r[0][:, :Q_LORA], c[0][...], r[1][...], Q_LORA)
        dckv, dgkva = _rms_b(r[0][:, Q_LORA:Q_LORA + KV_LORA], c[1][...], r[2][...], KV_LORA)
        return (jnp.concatenate([dcq, dckv, r[3][...]], axis=1), dgqa, dgkva)

    dpqkv, dg_qa, dg_kva = _rowwise(
        "mla_prep_a_bwd", prep_a_bwd, l, 512,
        [(pqkv, QKV_W, 0, "row"), (dnq, Q_LORA, 0, "row"), (dnkv, KV_LORA, 0, "row"), (dkr, LANES, 0, "row")],
        [_row(p["q_a_norm_g"]), _row(p["kv_a_norm_g"])], [(QKV_W, BF16)], [(1, Q_LORA), (1, KV_LORA)])
    g["q_a_norm_g"], g["kv_a_norm_g"] = dg_qa[0], dg_kva[0]

    h = s["h"]
    dh = _mm("proj_dx", [(dpqkv, w["wqkv"]), (du, w["w_u"]), (dxq, w["w_xq"]), (dgl, w["w_g"])], trans_b=True,
             bn_cap=1024)
    dwqkv = _mm_tn("proj_qkv_dw", h, dpqkv)
    g["w_in"] = _pack_w_in([dwqkv, _mm_tn("proj_u_dw", h, du), _mm_tn("proj_xq_dw", h, dxq),
                            _mm_tn("proj_gate_dw", h, dgl)])
    dx, dx_b, dg_mix = _rowwise(
        "rms_mix_bwd", rms_bwd_res, l, 512, [(x, D_MODEL, 0, "row"), (dh, D_MODEL, 0, "row"), (dx1, D_MODEL, 0, "row")],
        [_row(p["norm_mix_g"])], [(D_MODEL, F32), (D_MODEL, BF16)], [(1, D_MODEL)])
    g["norm_mix_g"] = dg_mix[0]
    g["norm_ffn_g"] = g["norm_ffn_g"][0]
    return dx, dx_b, g


def _rope_tables(positions):
    inv_freq = ROPE_THETA ** (-jnp.arange(0, D_ROPE, 2, dtype=F32) / D_ROPE)
    ang = positions.astype(F32)[:, None] * inv_freq
    cos, sin = jnp.cos(ang), jnp.sin(ang)
    l = positions.shape[0]
    one, zero = jnp.ones((l, D_NOPE), F32), lambda n: jnp.zeros((l, n), F32)
    pad = LANES - D_QK
    rope_c = jnp.concatenate([one, cos, cos, zero(pad)], axis=1)
    rope_sa = jnp.concatenate([zero(D_NOPE), -sin, zero(16), zero(pad)], axis=1)
    rope_sb = jnp.concatenate([zero(D_NOPE + 16), sin, zero(pad)], axis=1)
    return rope_c, rope_sa, rope_sb


def kernel(x, mem, positions, norm_mix_g, w_in, q_a_norm_g, w_q_b, kv_a_norm_g, w_kv_b, q_norm_g, k_norm_g, w_o_mla, ssm_lambda_re, ssm_lambda_im, ssm_log_dt, ssm_b_re, ssm_b_im, ssm_c_re, ssm_c_im, ssm_d, w_glu, b_glu, w_o_ssm, mem_norm_g, w_mem_kv, xq_norm_g, xk_norm_g, w_o_cross, b_gate, w_out, norm_ffn_g, w_up, conv_w, conv_b, w_down, loss_target, m_norm_mix_g, m_w_in, m_q_a_norm_g, m_w_q_b, m_kv_a_norm_g, m_w_kv_b, m_q_norm_g, m_k_norm_g, m_w_o_mla, m_ssm_lambda_re, m_ssm_lambda_im, m_ssm_log_dt, m_ssm_b_re, m_ssm_b_im, m_ssm_c_re, m_ssm_c_im, m_ssm_d, m_w_glu, m_b_glu, m_w_o_ssm, m_mem_norm_g, m_w_mem_kv, m_xq_norm_g, m_xk_norm_g, m_w_o_cross, m_b_gate, m_w_out, m_norm_ffn_g, m_w_up, m_conv_w, m_conv_b, m_w_down, v_norm_mix_g, v_w_in, v_q_a_norm_g, v_w_q_b, v_kv_a_norm_g, v_w_kv_b, v_q_norm_g, v_k_norm_g, v_w_o_mla, v_ssm_lambda_re, v_ssm_lambda_im, v_ssm_log_dt, v_ssm_b_re, v_ssm_b_im, v_ssm_c_re, v_ssm_c_im, v_ssm_d, v_w_glu, v_b_glu, v_w_o_ssm, v_mem_norm_g, v_w_mem_kv, v_xq_norm_g, v_xk_norm_g, v_w_o_cross, v_b_gate, v_w_out, v_norm_ffn_g, v_w_up, v_conv_w, v_conv_b, v_w_down):
    a = dict(locals())
    wts = {n: a[n] for n in WEIGHT_ORDER}
    m_in = {n: a["m_" + n] for n in WEIGHT_ORDER}
    v_in = {n: a["v_" + n] for n in WEIGHT_ORDER}
    depth = norm_mix_g.shape[0]
    x0, mem0, pos0, tgt = x[0], mem[0], positions[0], loss_target[0]
    l = x0.shape[0]
    offs = {grp: _group_rows(params, depth)[0] for grp, _, params in GROUPS}

    srcs = [_group_local(width, params, wts, depth, F32 if grp == "gconv" else BF16) for grp, width, params in GROUPS]
    gath = dict(zip([grp for grp, _, _ in GROUPS], _gather_all("gather_weights", srcs)))

    tabs = _rope_tables(pos0)
    layer_w = [_layer_weights(gath, offs, i) for i in range(depth)]
    layer_p = [{n: wts[n][i] for n in REPLICATED} for i in range(depth)]

    saved = []
    xc = x0
    for i in range(depth):
        xc, s = _layer_fwd(xc, mem0, tabs, layer_w[i], layer_p[i])
        saved.append(s)

    def loss_fn(i, r, c):
        d = r[0][...] - r[1][...]
        dy = d * (1.0 / D_MODEL)
        return (dy, dy, jnp.sum(d * d, axis=0, keepdims=True))

    dy, dy_b, sq = _rowwise("loss", loss_fn, l, 512, [(xc, D_MODEL, 0, "row"), (tgt, D_MODEL, 0, "row")], [],
                            [(D_MODEL, F32), (D_MODEL, BF16)], [(1, D_MODEL)])
    loss = lax.psum(0.5 * jnp.sum(sq) / D_MODEL, ("x", "y", "c"))

    grads = [None] * depth
    dxc, dxc_b = dy, dy_b
    for i in reversed(range(depth)):
        dxc, dxc_b, grads[i] = _layer_bwd(dxc, dxc_b, saved[i], mem0, tabs, layer_w[i], layer_p[i])
    grad_x = dxc[None]

    sends = []
    for grp, width, params in GROUPS:
        blocks = []
        for n, rows, _ in params:
            for i in range(depth):
                gb = grads[i][n]
                blocks.append(jnp.pad(gb, ((0, 0), (0, _rows8(rows) - rows), (0, width - gb.shape[2]))))
        sends.append(jnp.concatenate(blocks, axis=1))
    rep_flat = jnp.concatenate([jnp.stack([grads[i][n] for i in range(depth)]).reshape(-1) for n in REPLICATED])
    sends.append(_rep_rows(jnp.broadcast_to(rep_flat[None], (N_DEV, rep_flat.shape[0]))))
    names = [grp for grp, _, _ in GROUPS] + ["rep"]
    got = _pair_exchange("exchange_grads_core", sends)
    parts = [_pair_sum("grad_pair_sum_" + nm, s_, g_) for nm, s_, g_ in zip(names, sends, got)]
    rcvs = dict(zip(names, _chip_exchange("exchange_grads_chip", parts)))

    outs = {}
    for grp, width, params in GROUPS:
        local = [_group_local(width, params, d, depth, F32) for d in (wts, m_in, v_in)]
        res = _adamw("adamw_" + grp, rcvs[grp], *local)
        for tag, arr in zip(("grad", "delta", "m", "v"), res):
            for n, rows, cols in params:
                outs[(tag, n)] = jnp.stack([arr[offs[grp][(n, i)]:offs[grp][(n, i)] + rows, :cols]
                                            for i in range(depth)])
    rep_local = [_rep_rows(jnp.concatenate([d[n].astype(F32).reshape(-1) for n in REPLICATED])) for d in (wts, m_in, v_in)]
    res = _adamw("adamw_rep", rcvs["rep"], *rep_local)
    for tag, arr in zip(("grad", "delta", "m", "v"), res):
        flat, off = arr.reshape(-1), 0
        for n in REPLICATED:
            cnt = wts[n].size
            outs[(tag, n)] = flat[off:off + cnt].reshape(wts[n].shape)
            off += cnt
    result = [loss, grad_x]
    for tag in ("grad", "delta", "m", "v"):
        result += [outs[(tag, n)] for n in WEIGHT_ORDER]
    return tuple(result)
```

```python
import collections
import math

import jax
import jax.numpy as jnp
from jax import lax
from jax.experimental import pallas as pl
from jax.experimental.pallas import tpu as pltpu

F32 = jnp.float32
BF16 = jnp.bfloat16

N_DEV = 8
LANES = 128
LOG2E = math.log2(math.e)
VMEM_LIMIT_BYTES = 56 * 1024 * 1024

D_MODEL = 1024
EPS = 1e-6
MLA_HEADS = 8
Q_LORA = 384
KV_LORA = 256
D_NOPE = 64
D_ROPE = 32
D_QK = D_NOPE + D_ROPE
D_V = 64
ROPE_THETA = 10000.0
SSM_GROUPS = 32
SSM_GROUP_CH = 16
SSM_WIDTH = 512
SSM_STATE = 64
SSM_N = SSM_GROUPS * SSM_STATE
X_HEADS = 4
X_HEAD_DIM = 128
X_WIDTH = 512
D_FF = 2816
IN_WIDTH = Q_LORA + KV_LORA + D_ROPE + SSM_WIDTH + X_WIDTH + 3 * D_MODEL
QKV_W = Q_LORA + KV_LORA + LANES
KR_LO = D_NOPE

ADAM_LR = 0.001
ADAM_B1 = 0.9
ADAM_B2 = 0.999
ADAM_EPS = 1e-08
ADAM_WD = 0.01
ADAM_STEP = 10

REPLICATED = (
    "norm_mix_g", "q_a_norm_g", "kv_a_norm_g", "q_norm_g", "k_norm_g", "ssm_lambda_re", "ssm_lambda_im",
    "ssm_log_dt", "ssm_b_re", "ssm_b_im", "ssm_c_re", "ssm_c_im", "ssm_d", "b_glu", "mem_norm_g",
    "xq_norm_g", "xk_norm_g", "b_gate", "norm_ffn_g", "conv_b",
)
WEIGHT_ORDER = (
    "norm_mix_g", "w_in", "q_a_norm_g", "w_q_b", "kv_a_norm_g", "w_kv_b", "q_norm_g", "k_norm_g", "w_o_mla",
    "ssm_lambda_re", "ssm_lambda_im", "ssm_log_dt", "ssm_b_re", "ssm_b_im", "ssm_c_re", "ssm_c_im", "ssm_d",
    "w_glu", "b_glu", "w_o_ssm", "mem_norm_g", "w_mem_kv", "xq_norm_g", "xk_norm_g", "w_o_cross", "b_gate",
    "w_out", "norm_ffn_g", "w_up", "conv_w", "conv_b", "w_down",
)


def _params(**kw):
    return pltpu.CompilerParams(vmem_limit_bytes=VMEM_LIMIT_BYTES, **kw)


def _pick(n, cap):
    if n <= cap:
        return n
    best = None
    for m in range(LANES, cap + 1, LANES):
        if n % m == 0:
            best = m
    assert best is not None, n
    return best


_NN = (((1,), (0,)), ((), ()))
_NT = (((1,), (1,)), ((), ()))
_TN = (((0,), (0,)), ((), ()))


def _dot(a, b, dn):
    return lax.dot_general(a.astype(BF16), b.astype(BF16), dn, preferred_element_type=F32)


def _mm(name, pairs, *, trans_b=False, add=None, out_dtype=F32, bm=512, bn_cap=512):
    m = pairs[0][0].shape[0]
    n = pairs[0][1].shape[0 if trans_b else 1]
    bm = min(bm, m)
    bn = _pick(n, bn_cap)
    npair = len(pairs)

    def body(*refs):
        o_ref = refs[-1]
        acc = None
        for p in range(npair):
            d = _dot(refs[2 * p][...], refs[2 * p + 1][...], _NT if trans_b else _NN)
            acc = d if acc is None else acc + d
        if add is not None:
            acc = acc + refs[2 * npair][...]
        o_ref[...] = acc.astype(out_dtype)

    in_specs, args = [], []
    for a, b in pairs:
        k = a.shape[1]
        in_specs.append(pl.BlockSpec((bm, k), lambda i, j: (i, 0)))
        if trans_b:
            in_specs.append(pl.BlockSpec((bn, k), lambda i, j: (j, 0)))
        else:
            in_specs.append(pl.BlockSpec((k, bn), lambda i, j: (0, j)))
        args += [a, b]
    if add is not None:
        in_specs.append(pl.BlockSpec((bm, bn), lambda i, j: (i, j)))
        args.append(add)
    return pl.pallas_call(
        body, name=name, grid=(m // bm, n // bn), in_specs=in_specs,
        out_specs=pl.BlockSpec((bm, bn), lambda i, j: (i, j)),
        out_shape=jax.ShapeDtypeStruct((m, n), out_dtype), compiler_params=_params(),
    )(*args)


def _mm_tn(name, a, b, *, bm_cap=512, bn_cap=1536, bk=1024):
    l, m = a.shape
    n = b.shape[1]
    bm, bn, bk = _pick(m, bm_cap), _pick(n, bn_cap), min(bk, l)

    def body(a_ref, b_ref, o_ref):
        @pl.when(pl.program_id(2) == 0)
        def _():
            o_ref[...] = jnp.zeros_like(o_ref)

        o_ref[...] += _dot(a_ref[...], b_ref[...], _TN)

    return pl.pallas_call(
        body, name=name, grid=(m // bm, n // bn, l // bk),
        in_specs=[pl.BlockSpec((bk, bm), lambda i, j, k: (k, i)), pl.BlockSpec((bk, bn), lambda i, j, k: (k, j))],
        out_specs=pl.BlockSpec((bm, bn), lambda i, j, k: (i, j)),
        out_shape=jax.ShapeDtypeStruct((m, n), F32), compiler_params=_params(),
    )(a, b)


def _rowwise(name, fn, nrows, bm, row_ins, consts, row_outs, acc_outs=()):
    bm = min(bm, nrows)
    nblk = nrows // bm
    sub = bm // 8
    nin, nc, nro = len(row_ins), len(consts), len(row_outs)

    def body(*refs):
        i = pl.program_id(0)
        outs = fn(i, refs[:nin], refs[nin:nin + nc])
        o_refs = refs[nin + nc:nin + nc + nro]
        a_refs = refs[nin + nc + nro:]
        for r, v in zip(o_refs, outs[:nro]):
            r[...] = v.astype(r.dtype)
        if a_refs:
            @pl.when(i == 0)
            def _():
                for r in a_refs:
                    r[...] = jnp.zeros_like(r)

            for r, v in zip(a_refs, outs[nro:]):
                r[...] += v

    in_specs, args = [], []
    for arr, w, cb, kind in row_ins:
        if kind == "row":
            in_specs.append(pl.BlockSpec((bm, w), lambda i, cb=cb: (i, cb)))
        elif kind == "prev":
            in_specs.append(pl.BlockSpec((8, w), lambda i, cb=cb: (jnp.maximum(i * sub - 1, 0), cb)))
        else:
            in_specs.append(pl.BlockSpec((8, w), lambda i, cb=cb: (jnp.minimum((i + 1) * sub, nrows // 8 - 1), cb)))
        args.append(arr)
    for c in consts:
        in_specs.append(pl.BlockSpec(c.shape, lambda i: (0, 0)))
        args.append(c)
    out_specs = [pl.BlockSpec((bm, w), lambda i: (i, 0)) for w, _ in row_outs]
    out_specs += [pl.BlockSpec(s, lambda i: (0, 0)) for s in acc_outs]
    out_shape = [jax.ShapeDtypeStruct((nrows, w), dt) for w, dt in row_outs]
    out_shape += [jax.ShapeDtypeStruct(s, F32) for s in acc_outs]
    res = pl.pallas_call(
        body, name=name, grid=(nblk,), in_specs=in_specs, out_specs=out_specs, out_shape=out_shape,
        compiler_params=_params(),
    )(*args)
    return res


def _rms_f(x, g, n):
    r = lax.rsqrt(jnp.sum(x * x, axis=-1, keepdims=True) * (1.0 / n) + EPS)
    return x * r * g


def _rms_b(x, g, dy, n):
    r = lax.rsqrt(jnp.sum(x * x, axis=-1, keepdims=True) * (1.0 / n) + EPS)
    gx = dy * g
    dx = r * gx - x * (r * r * r * (jnp.sum(x * gx, axis=-1, keepdims=True) * (1.0 / n)))
    dg = jnp.sum(dy * (x * r), axis=0, keepdims=True)
    return dx, dg


def _rope_f(x, c, sa, sb):
    return x * c + pltpu.roll(x, LANES - 16, 1) * sa + pltpu.roll(x, 16, 1) * sb


def _rope_b(g, c, sa, sb):
    return g * c + pltpu.roll(g * sa, 16, 1) + pltpu.roll(g * sb, LANES - 16, 1)


def _gelu(x):
    c = math.sqrt(2.0 / math.pi)
    return 0.5 * x * (1.0 + jnp.tanh(c * (x + 0.044715 * (x * x * x))))


def _gelu_grad(x):
    c = math.sqrt(2.0 / math.pi)
    th = jnp.tanh(c * (x + 0.044715 * (x * x * x)))
    return 0.5 * (1.0 + th) + 0.5 * x * (1.0 - th * th) * (c * (1.0 + 3.0 * 0.044715 * (x * x)))


def _row_ids(bm):
    return lax.broadcasted_iota(jnp.int32, (bm, 1), 0)


def _shift_down(x, halo_ref, i, k):
    live = (i > 0).astype(F32)
    out = pltpu.roll(x, k, 0)
    row = _row_ids(8)
    first = out[:8]
    for r in range(k):
        e = (row == r).astype(F32)
        first = first * (1.0 - e) + e * (halo_ref[8 - k + r:8 - k + r + 1, :] * live)
    return jnp.concatenate([first, out[8:]], axis=0)


def _shift_up(x, halo_ref, i, nblk, k):
    bm = x.shape[0]
    live = (i < nblk - 1).astype(F32)
    out = pltpu.roll(x, bm - k, 0)
    row = _row_ids(8)
    last = out[bm - 8:]
    for r in range(k):
        e = (row == 8 - k + r).astype(F32)
        last = last * (1.0 - e) + e * (halo_ref[r:r + 1, :] * live)
    return jnp.concatenate([out[:bm - 8], last], axis=0)


def _live_pairs(nq, nk, bq, bk, causal, key_major):
    pairs = [(i, j) for i in range(nq) for j in range(nk) if not causal or j * bk <= i * bq + bq - 1]
    if key_major:
        pairs.sort(key=lambda ij: (ij[1], ij[0]))
    return (jnp.asarray([p[0] for p in pairs], jnp.int32), jnp.asarray([p[1] for p in pairs], jnp.int32))


def _attn_fwd(name, qa, ka, vta, *, qoff, koff, voff, heads, causal, scale, bq, bk, side=None):
    lq, lk = qa.shape[0], ka.shape[0]
    bq, bk = min(bq, lq), min(bk, lk)
    nq, nk = lq // bq, lk // bk
    c2 = scale * LOG2E
    tab_i, tab_j = _live_pairs(nq, nk, bq, bk, causal, key_major=False)

    def body(ti, tj, q_ref, k_ref, vt_ref, o_ref, lse_ref, m_s, l_s, acc_s):
        t = pl.program_id(1)
        i, j = ti[t], tj[t]
        j_last = jnp.minimum(nk - 1, (i * bq + bq - 1) // bk) if causal else nk - 1

        @pl.when(j == 0)
        def _():
            m_s[...] = jnp.full_like(m_s, -1e30)
            l_s[...] = jnp.zeros_like(l_s)
            acc_s[...] = jnp.zeros_like(acc_s)

        def step(masked):
            st = _dot(k_ref[...], q_ref[...], _NT) * c2
            if masked:
                key = j * bk + lax.broadcasted_iota(jnp.int32, (bk, bq), 0)
                qry = i * bq + lax.broadcasted_iota(jnp.int32, (bk, bq), 1)
                st = jnp.where(key <= qry, st, -1e30)
            m_prev = m_s[...]
            m_new = jnp.maximum(m_prev, jnp.max(st, axis=0, keepdims=True))
            alpha = jnp.exp2(m_prev - m_new)
            pt = jnp.exp2(st - m_new)
            l_s[...] = alpha * l_s[...] + jnp.sum(pt, axis=0, keepdims=True)
            acc_s[...] = alpha * acc_s[...] + _dot(vt_ref[...], pt, _NN)
            m_s[...] = m_new

        if causal:
            full = j * bk + bk - 1 <= i * bq
            pl.when(full)(lambda: step(False))
            pl.when(jnp.logical_not(full))(lambda: step(True))
        else:
            step(False)

        @pl.when(j == j_last)
        def _():
            l = l_s[...]
            o_ref[...] = (acc_s[...] / l).T.astype(o_ref.dtype)
            lse_ref[0] = m_s[...] + jnp.log2(l)

    in_specs = [pl.BlockSpec((bq, LANES), lambda h, t, ti, tj: (ti[t], qoff + h)),
                pl.BlockSpec((bk, LANES), lambda h, t, ti, tj: (tj[t], koff + h)),
                pl.BlockSpec((LANES, bk), lambda h, t, ti, tj: (voff + h, tj[t]))]
    out_specs = [pl.BlockSpec((bq, LANES), lambda h, t, ti, tj: (ti[t], h)),
                 pl.BlockSpec((1, 1, bq), lambda h, t, ti, tj: (h, 0, ti[t]))]
    scratch = [pltpu.VMEM((1, bq), F32), pltpu.VMEM((1, bq), F32), pltpu.VMEM((LANES, bq), F32)]
    out_shape = [jax.ShapeDtypeStruct((lq, heads * LANES), BF16), jax.ShapeDtypeStruct((heads, 1, lq), F32)]
    (o, lse), rode = _pair_grid_call(body, name, heads, (tab_i, tab_j), [qa, ka, vta], in_specs, out_specs, out_shape,
                                     scratch, side)
    return o, lse, rode


_Side = collections.namedtuple("_Side", "srcs out_shapes nsem phases")


def _pair_grid_call(body, name, heads, tabs, ins, in_specs, out_specs, out_shape, scratch, side):
    npairs = int(tabs[0].shape[0])
    n_in, n_out = len(ins), len(out_shape)
    n = len(side.srcs) if side else 0

    def wrapped(*refs):
        pre = len(tabs)
        if not side:
            return body(*refs)
        s_refs = refs[pre + n_in:pre + n_in + n]
        o_refs = refs[pre + n_in + n + n_out:pre + n_in + 2 * n + n_out]
        send, recv, loc = refs[-3:]
        args = (s_refs, o_refs, (send, recv), loc)
        h, t = pl.program_id(0), pl.program_id(1)
        pl.when(jnp.logical_and(h == 0, t == 0))(lambda: side.phases[0](*args))
        if len(side.phases) == 3:
            pl.when(jnp.logical_and(h == heads // 2, t == 0))(lambda: side.phases[1](*args))
        body(*refs[:pre + n_in], *refs[pre + n_in + n:pre + n_in + n + n_out], *refs[pre + n_in + 2 * n + n_out:-3])
        pl.when(jnp.logical_and(h == heads - 1, t == npairs - 1))(lambda: side.phases[-1](*args))

    hbm = pl.BlockSpec(memory_space=pltpu.HBM)
    sems = [pltpu.SemaphoreType.DMA((side.nsem * n,)), pltpu.SemaphoreType.DMA((side.nsem * n,)),
            pltpu.SemaphoreType.DMA((n,))] if side else []
    grid_spec = pltpu.PrefetchScalarGridSpec(
        num_scalar_prefetch=len(tabs), grid=(heads, npairs), in_specs=in_specs + [hbm] * n,
        out_specs=out_specs + [hbm] * n, scratch_shapes=scratch + sems)
    res = pl.pallas_call(
        wrapped, name=name, grid_spec=grid_spec, out_shape=out_shape + (list(side.out_shapes) if side else []),
        compiler_params=_params(has_side_effects=True) if side else _params(),
    )(*tabs, *ins, *(side.srcs if side else []))
    return res[:n_out], res[n_out:]


def _attn_delta(name, oa, doa, *, heads, bq):
    lq = oa.shape[0]
    bq = min(bq, lq)

    def body(o_ref, do_ref, d_ref):
        prod = o_ref[...].astype(F32) * do_ref[...].astype(F32)
        hi = prod.astype(BF16)
        lo = (prod - hi.astype(F32)).astype(BF16)
        pick = (lax.broadcasted_iota(jnp.int32, (8, LANES), 0) == 0).astype(BF16)
        sums = _dot(pick, hi, _NT) + _dot(pick, lo, _NT)
        d_ref[0] = jnp.sum(sums, axis=0, keepdims=True)

    blk = pl.BlockSpec((bq, LANES), lambda h, i: (i, h))
    return pl.pallas_call(
        body, name=name, grid=(heads, lq // bq), in_specs=[blk, blk],
        out_specs=pl.BlockSpec((1, 1, bq), lambda h, i: (h, 0, i)),
        out_shape=jax.ShapeDtypeStruct((heads, 1, lq), F32), compiler_params=_params(),
    )(oa, doa)


def _attn_bwd(name, qa, ka, kta, va, doa, lsea, deltaa, *, qoff, koff, voff, heads, causal, scale, bq, bk, side=None):
    lq, lk = qa.shape[0], ka.shape[0]
    bq, bk = min(bq, lq), min(bk, lk)
    nq, nk = lq // bq, lk // bk
    c2 = scale * LOG2E
    tab_i, tab_j = _live_pairs(nq, nk, bq, bk, causal, key_major=True)

    def body(ti, tj, q_ref, k_ref, kt_ref, v_ref, do_ref, lse_ref, delta_ref, dqt_ref, dk_ref, dv_ref):
        t = pl.program_id(1)
        i, j = ti[t], tj[t]
        i_first = (j * bk) // bq if causal else 0

        @pl.when(t == 0)
        def _():
            dqt_ref[...] = jnp.zeros_like(dqt_ref)

        @pl.when(i == i_first)
        def _():
            dk_ref[...] = jnp.zeros_like(dk_ref)
            dv_ref[...] = jnp.zeros_like(dv_ref)

        def step(masked):
            q, k, v, do = q_ref[...], k_ref[...], v_ref[...], do_ref[...]
            st = _dot(k, q, _NT) * c2
            if masked:
                key = j * bk + lax.broadcasted_iota(jnp.int32, (bk, bq), 0)
                qry = i * bq + lax.broadcasted_iota(jnp.int32, (bk, bq), 1)
                st = jnp.where(key <= qry, st, -1e30)
            pt = jnp.exp2(st - lse_ref[0])
            dv_ref[...] += _dot(pt, do, _NN)
            dpt = _dot(v, do, _NT)
            dst = (pt * (dpt - delta_ref[0]) * scale).astype(BF16)
            dk_ref[...] += _dot(dst, q, _NN)
            dqt_ref[0, i] += _dot(kt_ref[...], dst, _NN)

        if causal:
            full = j * bk + bk - 1 <= i * bq
            pl.when(full)(lambda: step(False))
            pl.when(jnp.logical_not(full))(lambda: step(True))
        else:
            step(False)

    q_spec = lambda off: pl.BlockSpec((bq, LANES), lambda h, t, ti, tj: (ti[t], off + h))
    kv_spec = lambda off: pl.BlockSpec((bk, LANES), lambda h, t, ti, tj: (tj[t], off + h))
    row_spec = pl.BlockSpec((1, 1, bq), lambda h, t, ti, tj: (h, 0, ti[t]))
    in_specs = [q_spec(qoff), kv_spec(koff), pl.BlockSpec((LANES, bk), lambda h, t, ti, tj: (koff + h, tj[t])),
                kv_spec(voff), q_spec(0), row_spec, row_spec]
    out_specs = [pl.BlockSpec((1, nq, LANES, bq), lambda h, t, ti, tj: (h, 0, 0, 0)),
                 pl.BlockSpec((bk, LANES), lambda h, t, ti, tj: (tj[t], h)),
                 pl.BlockSpec((bk, LANES), lambda h, t, ti, tj: (tj[t], h))]
    out_shape = [jax.ShapeDtypeStruct((heads, nq, LANES, bq), F32), jax.ShapeDtypeStruct((lk, heads * LANES), F32),
                 jax.ShapeDtypeStruct((lk, heads * LANES), F32)]
    (dqt, dk, dv), rode = _pair_grid_call(body, name, heads, (tab_i, tab_j), [qa, ka, kta, va, doa, lsea, deltaa],
                                          in_specs, out_specs, out_shape, [], side)
    return dqt.transpose(1, 3, 0, 2).reshape(lq, heads * LANES), dk, dv, rode


S5_STRIP = SSM_N // 4
S5_CH = SSM_WIDTH // 4
S5_CHUNK = 512


def _scan_chunk(xr_ref, xi_ref, ar, ai, cre_s, cim_s, reverse, unroll=4):
    t = xr_ref.shape[0]
    ng = t // 8
    edge = 0 if reverse else 7
    row8 = lax.broadcasted_iota(jnp.int32, (8, 1), 0)

    def grp(g):
        return pl.ds(pl.multiple_of(g * 8, 8), 8)

    def group_scan(xr, xi):
        pr, pi = ar, ai
        for d in (1, 2, 4):
            own = ((row8 < 8 - d) if reverse else (row8 >= d)).astype(F32)
            shift = (8 - d) if reverse else d
            sr, si = pltpu.roll(xr, shift, 0) * own, pltpu.roll(xi, shift, 0) * own
            xr, xi = xr + pr * sr - pi * si, xi + pr * si + pi * sr
            pr, pi = pr * pr - pi * pi, 2.0 * pr * pi
        return xr, xi

    def local(g, _):
        xr, xi = group_scan(xr_ref[grp(g), :], xi_ref[grp(g), :])
        xr_ref[grp(g), :] = xr
        xi_ref[grp(g), :] = xi
        return 0

    lax.fori_loop(0, ng, local, 0, unroll=unroll)
    e = (row8 == 7 - edge).astype(F32)
    pw_r, pw_i = group_scan(e * ar, e * ai)
    e_out = (row8 == edge).astype(F32)
    a8_r = jnp.sum(pw_r * e_out, axis=0, keepdims=True)
    a8_i = jnp.sum(pw_i * e_out, axis=0, keepdims=True)
    cr, ci = cre_s[...], cim_s[...]
    for i in range(ng):
        g = ng - 1 - i if reverse else i
        rows = slice(g * 8, g * 8 + 8)
        lr, li = xr_ref[g * 8 + edge:g * 8 + edge + 1, :], xi_ref[g * 8 + edge:g * 8 + edge + 1, :]
        xr_ref[rows, :] = xr_ref[rows, :] + pw_r * cr - pw_i * ci
        xi_ref[rows, :] = xi_ref[rows, :] + pw_r * ci + pw_i * cr
        cr, ci = lr + a8_r * cr - a8_i * ci, li + a8_r * ci + a8_i * cr
    cre_s[...] = cr
    cim_s[...] = ci


def _s5_fwd(name, u, bb_re, bb_im, cc_re, cc_imn, a_re, a_im):
    l = u.shape[0]
    t = min(S5_CHUNK, l)

    def body(u_ref, bbr_ref, bbi_ref, ccr_ref, cci_ref, ar_ref, ai_ref, sre_ref, sim_ref, y_ref, cre_s, cim_s):
        @pl.when(pl.program_id(1) == 0)
        def _():
            cre_s[...] = jnp.zeros_like(cre_s)
            cim_s[...] = jnp.zeros_like(cim_s)

        uv = u_ref[...]
        sre_ref[...] = _dot(uv, bbr_ref[...], _NN)
        sim_ref[...] = _dot(uv, bbi_ref[...], _NN)
        _scan_chunk(sre_ref, sim_ref, ar_ref[...], ai_ref[...], cre_s, cim_s, reverse=False)
        y_ref[...] = _dot(sre_ref[...], ccr_ref[...], _NN) + _dot(sim_ref[...], cci_ref[...], _NN)

    rows_ch = pl.BlockSpec((t, S5_CH), lambda w, c: (c, w))
    rows_st = pl.BlockSpec((t, S5_STRIP), lambda w, c: (c, w))
    b_blk = pl.BlockSpec((S5_CH, S5_STRIP), lambda w, c: (w, w))
    c_blk = pl.BlockSpec((S5_STRIP, S5_CH), lambda w, c: (w, w))
    a_blk = pl.BlockSpec((1, S5_STRIP), lambda w, c: (0, w))
    return pl.pallas_call(
        body, name=name, grid=(SSM_N // S5_STRIP, l // t),
        in_specs=[rows_ch, b_blk, b_blk, c_blk, c_blk, a_blk, a_blk], out_specs=[rows_st, rows_st, rows_ch],
        out_shape=[jax.ShapeDtypeStruct((l, SSM_N), F32), jax.ShapeDtypeStruct((l, SSM_N), F32),
                   jax.ShapeDtypeStruct((l, SSM_WIDTH), F32)],
        scratch_shapes=[pltpu.VMEM((1, S5_STRIP), F32), pltpu.VMEM((1, S5_STRIP), F32)],
        compiler_params=_params(),
    )(u, bb_re, bb_im, cc_re, cc_imn, a_re, a_im)


def _s5_bwd(name, dy, du_skip, u, s_re, s_im, bb_re, bb_im, cc_re, cc_imn, a_re, a_im_neg):
    l = u.shape[0]
    t = min(S5_CHUNK, l)
    nc = l // t

    def body(dy_ref, skip_ref, u_ref, sre_ref, sim_ref, hre_ref, him_ref, bbr_ref, bbi_ref, ccr_ref, cci_ref, ar_ref,
             ai_ref, du_ref, dar_ref, dai_ref, dbr_ref, dbi_ref, dcr_ref, dci_ref, lr_s, li_s, cre_s, cim_s):
        c = pl.program_id(1)

        @pl.when(c == 0)
        def _():
            for r in (cre_s, cim_s, dar_ref, dai_ref, dbr_ref, dbi_ref, dcr_ref, dci_ref):
                r[...] = jnp.zeros_like(r)

        dyv, uv = dy_ref[...], u_ref[...]
        lr_s[...] = _dot(dyv, ccr_ref[...], _NT)
        li_s[...] = _dot(dyv, cci_ref[...], _NT)
        _scan_chunk(lr_s, li_s, ar_ref[...], ai_ref[...], cre_s, cim_s, reverse=True)
        lam_r, lam_i = lr_s[...], li_s[...]
        s_r, s_i = sre_ref[...], sim_ref[...]
        tc = nc - 1 - c
        sp_r, sp_i = _shift_down(s_r, hre_ref, tc, 1), _shift_down(s_i, him_ref, tc, 1)
        dar_ref[...] += jnp.sum(lam_r * sp_r + lam_i * sp_i, axis=0, keepdims=True)
        dai_ref[...] += jnp.sum(lam_i * sp_r - lam_r * sp_i, axis=0, keepdims=True)
        dbr_ref[0] += _dot(uv, lam_r, _TN)
        dbi_ref[0] += _dot(uv, lam_i, _TN)
        dcr_ref[0] += _dot(s_r, dyv, _TN)
        dci_ref[0] += _dot(s_i, dyv, _TN)
        du = _dot(lam_r, bbr_ref[...], _NT) + _dot(lam_i, bbi_ref[...], _NT) + skip_ref[...]
        du_ref[...] = du.astype(du_ref.dtype)

    rows_ch = pl.BlockSpec((t, S5_CH), lambda w, c: (nc - 1 - c, w))
    rows_st = pl.BlockSpec((t, S5_STRIP), lambda w, c: (nc - 1 - c, w))
    halo = pl.BlockSpec((8, S5_STRIP), lambda w, c: (jnp.maximum((nc - 1 - c) * (t // 8) - 1, 0), w))
    b_blk = pl.BlockSpec((S5_CH, S5_STRIP), lambda w, c: (w, w))
    c_blk = pl.BlockSpec((S5_STRIP, S5_CH), lambda w, c: (w, w))
    a_blk = pl.BlockSpec((1, S5_STRIP), lambda w, c: (0, w))
    nw = SSM_N // S5_STRIP
    return pl.pallas_call(
        body, name=name, grid=(nw, nc),
        in_specs=[rows_ch, rows_ch, rows_ch, rows_st, rows_st, halo, halo, b_blk, b_blk, c_blk, c_blk, a_blk, a_blk],
        out_specs=[rows_ch, a_blk, a_blk, pl.BlockSpec((1, S5_CH, S5_STRIP), lambda w, c: (w, 0, 0)),
                   pl.BlockSpec((1, S5_CH, S5_STRIP), lambda w, c: (w, 0, 0)),
                   pl.BlockSpec((1, S5_STRIP, S5_CH), lambda w, c: (w, 0, 0)),
                   pl.BlockSpec((1, S5_STRIP, S5_CH), lambda w, c: (w, 0, 0))],
        out_shape=[jax.ShapeDtypeStruct((l, SSM_WIDTH), BF16), jax.ShapeDtypeStruct((1, SSM_N), F32),
                   jax.ShapeDtypeStruct((1, SSM_N), F32), jax.ShapeDtypeStruct((nw, S5_CH, S5_STRIP), F32),
                   jax.ShapeDtypeStruct((nw, S5_CH, S5_STRIP), F32), jax.ShapeDtypeStruct((nw, S5_STRIP, S5_CH), F32),
                   jax.ShapeDtypeStruct((nw, S5_STRIP, S5_CH), F32)],
        scratch_shapes=[pltpu.VMEM((t, S5_STRIP), F32), pltpu.VMEM((t, S5_STRIP), F32),
                        pltpu.VMEM((1, S5_STRIP), F32), pltpu.VMEM((1, S5_STRIP), F32)],
        compiler_params=_params(),
    )(dy, du_skip, u, s_re, s_im, s_re, s_im, bb_re, bb_im, cc_re, cc_imn, a_re, a_im_neg)


def _disc_math(lr, li, ldt, br, bi):
    dt = jnp.exp(ldt)
    mag = jnp.exp(lr * dt)
    a_re, a_im = mag * jnp.cos(li * dt), mag * jnp.sin(li * dt)
    den = lr * lr + li * li
    e_re, e_im = a_re - 1.0, a_im
    f_re = (e_re * lr + e_im * li) / den
    f_im = (e_im * lr - e_re * li) / den
    return a_re, a_im, f_re * br - f_im * bi, f_re * bi + f_im * br


def _disc_fwd(lr, li, ldt, br, bi):
    def body(lr_ref, li_ref, ldt_ref, br_ref, bi_ref, are_ref, aim_ref, bbr_ref, bbi_ref):
        a_re, a_im, bb_re, bb_im = _disc_math(lr_ref[...], li_ref[...], ldt_ref[...], br_ref[...], bi_ref[...])
        are_ref[...] = a_re
        aim_ref[...] = a_im
        bbr_ref[...] = bb_re
        bbi_ref[...] = bb_im

    col = jax.ShapeDtypeStruct(lr.shape, F32)
    mat = jax.ShapeDtypeStruct(br.shape, F32)
    return pl.pallas_call(body, name="s5_disc_fwd", out_shape=[col, col, mat, mat], compiler_params=_params())(
        lr, li, ldt, br, bi)


def _disc_bwd(lr, li, ldt, br, bi, da_re, da_im, dbb_re, dbb_im):
    def body(lr_ref, li_ref, ldt_ref, br_ref, bi_ref, g0, g1, g2, g3, o0, o1, o2, o3, o4):
        _, vjp = jax.vjp(_disc_math, lr_ref[...], li_ref[...], ldt_ref[...], br_ref[...], bi_ref[...])
        grads = vjp((g0[...], g1[...], g2[...], g3[...]))
        for o, g in zip((o0, o1, o2, o3, o4), grads):
            o[...] = g

    col = jax.ShapeDtypeStruct(lr.shape, F32)
    mat = jax.ShapeDtypeStruct(br.shape, F32)
    return pl.pallas_call(body, name="s5_disc_bwd", out_shape=[col, col, col, mat, mat], compiler_params=_params())(
        lr, li, ldt, br, bi, da_re, da_im, dbb_re, dbb_im)


N_CHIP = 4


def _place():
    x, y, c = lax.axis_index("x"), lax.axis_index("y"), lax.axis_index("c")
    return (x, y, c), (x, y, 1 - c), [(1 - x, y), (x, 1 - y), (1 - x, 1 - y)]


def _lin(px, py, pc):
    return 4 * px + 2 * py + pc


def _remote(src, dst, sems, k, dev):
    return pltpu.make_async_remote_copy(src_ref=src, dst_ref=dst, send_sem=sems[0].at[k], recv_sem=sems[1].at[k],
                                        device_id=dev, device_id_type=pl.DeviceIdType.MESH)


def _hbm_call(body, name, srcs, out_shapes, nsem):
    n = len(srcs)
    hbm = pl.BlockSpec(memory_space=pltpu.HBM)

    def wrapped(*refs):
        body(refs[:n], refs[n:2 * n], (refs[2 * n], refs[2 * n + 1]), refs[2 * n + 2])

    return pl.pallas_call(
        wrapped, name=name, in_specs=[hbm] * n, out_specs=[hbm] * n, out_shape=out_shapes,
        scratch_shapes=[pltpu.SemaphoreType.DMA((nsem * n,)), pltpu.SemaphoreType.DMA((nsem * n,)),
                        pltpu.SemaphoreType.DMA((n,))],
        compiler_params=pltpu.CompilerParams(has_side_effects=True),
    )(*srcs)


GATHER_SEMS = 7
CHIP_SEMS = 3


def _gather_copies(s_refs, o_refs, sems, loc_sems):
    me, sib, chips = _place()
    c = me[2]
    out = []
    for g, (s_ref, o_ref) in enumerate(zip(s_refs, o_refs)):
        slot = lambda dev, o_ref=o_ref: o_ref.at[_lin(*dev)]
        k0 = GATHER_SEMS * g
        mine = pltpu.make_async_copy(s_ref, slot(me), loc_sems.at[g])
        first = [_remote(s_ref, slot(me), sems, k0, sib)]
        first += [_remote(s_ref, slot(me), sems, k0 + 1 + j, (*chip, c)) for j, chip in enumerate(chips)]
        passed = [_remote(slot((*chip, c)), slot((*chip, c)), sems, k0 + 4 + j, sib) for j, chip in enumerate(chips)]
        arrive = [_remote(s_ref, slot(sib), sems, k0, me)]
        arrive += [_remote(s_ref, slot((*chip, c)), sems, k0 + 1 + j, me) for j, chip in enumerate(chips)]
        arrive += [_remote(s_ref, slot((*chip, 1 - c)), sems, k0 + 4 + j, me) for j, chip in enumerate(chips)]
        out.append((mine, first, passed, arrive))
    return out


def _gather_start(*refs):
    for mine, first, _, _ in _gather_copies(*refs):
        mine.start()
        for cp in first:
            cp.start()


def _gather_relay(*refs):
    for _, _, passed, arrive in _gather_copies(*refs):
        for j, cp in enumerate(passed):
            arrive[1 + j].wait_recv()
            cp.start()


def _gather_finish(*refs):
    for mine, first, passed, arrive in _gather_copies(*refs):
        arrive[0].wait_recv()
        for cp in arrive[4:]:
            cp.wait_recv()
        for cp in first + passed:
            cp.wait_send()
        mine.wait()


def _gather_shapes(srcs):
    return [jax.ShapeDtypeStruct((N_DEV,) + s.shape, s.dtype) for s in srcs]


def _gather_all(name, srcs):
    def body(*refs):
        _gather_start(*refs)
        _gather_relay(*refs)
        _gather_finish(*refs)

    return _hbm_call(body, name, srcs, _gather_shapes(srcs), GATHER_SEMS)


def _pair_exchange(name, sends):
    def body(s_refs, o_refs, sems, loc_sems):
        me, sib, _ = _place()
        c = me[2]
        copies = [_remote(s_ref.at[2 * q + (1 - c)], o_ref.at[q], sems, N_CHIP * g + q, sib)
                  for g, (s_ref, o_ref) in enumerate(zip(s_refs, o_refs)) for q in range(N_CHIP)]
        for cp in copies:
            cp.start()
        for cp in copies:
            cp.wait()

    return _hbm_call(body, name, sends, [jax.ShapeDtypeStruct((N_CHIP,) + s.shape[1:], s.dtype) for s in sends],
                     N_CHIP)


def _row_block(r, cap):
    best = 8
    for m in range(8, min(r, cap) + 1, 8):
        if r % m == 0:
            best = m
    return best


def _pair_sum(name, send, got):
    _, r, cols = send.shape
    bm = _row_block(r, max(8, 256 * 1024 // cols))
    core = lax.axis_index("c").astype(jnp.int32).reshape(1)

    def body(core_ref, a_ref, b_ref, o_ref):
        o_ref[...] = a_ref[...] + b_ref[...]

    grid_spec = pltpu.PrefetchScalarGridSpec(
        num_scalar_prefetch=1, grid=(N_CHIP, r // bm),
        in_specs=[pl.BlockSpec((1, bm, cols), lambda q, i, cr: (2 * q + cr[0], i, 0)),
                  pl.BlockSpec((1, bm, cols), lambda q, i, cr: (q, i, 0))],
        out_specs=pl.BlockSpec((1, bm, cols), lambda q, i, cr: (q, i, 0)))
    return pl.pallas_call(body, name=name, grid_spec=grid_spec,
                          out_shape=jax.ShapeDtypeStruct((N_CHIP, r, cols), F32), compiler_params=_params())(
        core, send, got)


def _chip_exchange(name, parts):
    def body(*refs):
        _chip_start(*refs)
        _chip_finish(*refs)

    return _hbm_call(body, name, parts, _chip_shapes(parts), CHIP_SEMS)


def _chip_copies(p_refs, o_refs, sems, loc_sems):
    me, _, chips = _place()
    c = me[2]
    chip_id = lambda chip: 2 * chip[0] + chip[1]
    my_chip = chip_id(me)
    out = []
    for g, (p_ref, o_ref) in enumerate(zip(p_refs, o_refs)):
        mine = pltpu.make_async_copy(p_ref.at[my_chip], o_ref.at[my_chip], loc_sems.at[g])
        send = [_remote(p_ref.at[chip_id(chip)], o_ref.at[my_chip], sems, CHIP_SEMS * g + j, (*chip, c))
                for j, chip in enumerate(chips)]
        arrive = [_remote(p_ref.at[my_chip], o_ref.at[chip_id(chip)], sems, CHIP_SEMS * g + j, me)
                  for j, chip in enumerate(chips)]
        out.append((mine, send, arrive))
    return out


def _chip_start(*refs):
    for mine, send, _ in _chip_copies(*refs):
        mine.start()
        for cp in send:
            cp.start()


def _chip_finish(*refs):
    for mine, send, arrive in _chip_copies(*refs):
        for cp in arrive:
            cp.wait_recv()
        for cp in send:
            cp.wait_send()
        mine.wait()


def _chip_shapes(parts):
    return [jax.ShapeDtypeStruct(p.shape, p.dtype) for p in parts]


def _adamw(name, rcv, w, m, v):
    r, c = w.shape
    nslot = rcv.shape[0]
    bm = _row_block(r, max(8, 256 * 1024 // c))

    def body(rcv_ref, w_ref, m_ref, v_ref, g_ref, d_ref, m2_ref, v2_ref):
        g = rcv_ref[0]
        for s in range(1, nslot):
            g = g + rcv_ref[s]
        m2 = ADAM_B1 * m_ref[...] + (1.0 - ADAM_B1) * g
        v2 = ADAM_B2 * v_ref[...] + (1.0 - ADAM_B2) * (g * g)
        m_hat = m2 / (1.0 - ADAM_B1 ** ADAM_STEP)
        v_hat = v2 / (1.0 - ADAM_B2 ** ADAM_STEP)
        g_ref[...] = g
        d_ref[...] = -ADAM_LR * (m_hat / (jnp.sqrt(v_hat) + ADAM_EPS) + ADAM_WD * w_ref[...])
        m2_ref[...] = m2
        v2_ref[...] = v2

    blk = pl.BlockSpec((bm, c), lambda i: (i, 0))
    out = jax.ShapeDtypeStruct((r, c), F32)
    return pl.pallas_call(
        body, name=name, grid=(r // bm,),
        in_specs=[pl.BlockSpec((nslot, bm, c), lambda i: (0, i, 0)), blk, blk, blk], out_specs=[blk] * 4,
        out_shape=[out] * 4, compiler_params=_params(),
    )(rcv, w, m, v)


IN_SHARD = IN_WIDTH // N_DEV
UP_SHARD = 2 * D_FF // N_DEV
GROUPS = (
    ("g128", LANES, (("w_q_b", Q_LORA, D_QK), ("w_kv_b", KV_LORA, D_NOPE + D_V), ("w_o_mla", MLA_HEADS * D_V, LANES),
                     ("w_o_ssm", SSM_WIDTH, LANES), ("w_o_cross", X_WIDTH, LANES))),
    ("g512", SSM_WIDTH, (("w_glu", SSM_WIDTH // N_DEV, SSM_WIDTH),)),
    ("g1024", D_MODEL, (("w_mem_kv", D_MODEL // N_DEV, D_MODEL), ("w_out", D_MODEL // N_DEV, D_MODEL),
                        ("w_down", D_FF // N_DEV, D_MODEL))),
    ("g640", 640, (("w_in", D_MODEL, IN_SHARD),)),
    ("g768", 768, (("w_up", D_MODEL, UP_SHARD),)),
    ("gconv", 768, (("conv_w", 3, UP_SHARD),)),
)
REP_W = 1024


def _rows8(a):
    return -(-a // 8) * 8


def _group_rows(params):
    off, r = {}, 0
    for n, a, _ in params:
        off[n] = r
        r += _rows8(a)
    return off


def _group_local(width, params, vals, li, dtype):
    return jnp.concatenate([jnp.pad(vals[n][li].astype(dtype), ((0, _rows8(a) - a), (0, width - b)))
                            for n, a, b in params], axis=0)


def _rep_rows(flat):
    n = flat.shape[-1]
    per = REP_W * 64
    tot = -(-n // per) * per
    flat = jnp.pad(flat, [(0, 0)] * (flat.ndim - 1) + [(0, tot - n)])
    return flat.reshape(flat.shape[:-1] + (tot // REP_W, REP_W))


IN_SEGS = (
    (0, Q_LORA + KV_LORA, "wqkv", 0),
    (Q_LORA + KV_LORA, Q_LORA + KV_LORA + D_ROPE, "wqkv", Q_LORA + KV_LORA + KR_LO),
    (Q_LORA + KV_LORA + D_ROPE, Q_LORA + KV_LORA + D_ROPE + SSM_WIDTH, "w_u", 0),
    (Q_LORA + KV_LORA + D_ROPE + SSM_WIDTH, Q_LORA + KV_LORA + D_ROPE + SSM_WIDTH + X_WIDTH, "w_xq", 0),
    (Q_LORA + KV_LORA + D_ROPE + SSM_WIDTH + X_WIDTH, IN_WIDTH, "w_g", 0),
)
IN_PARTS = (("wqkv", QKV_W), ("w_u", SSM_WIDTH), ("w_xq", X_WIDTH), ("w_g", 3 * D_MODEL))


def _in_pieces():
    out = []
    for d in range(N_DEV):
        for lo, hi, part, dst in IN_SEGS:
            s, e = max(lo, d * IN_SHARD), min(hi, (d + 1) * IN_SHARD)
            if s < e:
                out.append((d, s - d * IN_SHARD, e - s, [p for p, _ in IN_PARTS].index(part), dst + s - lo))
    return out


def _unpack_w_in(gathered, row0, bm=256):
    def body(x_ref, *o_refs):
        o_refs[0][...] = jnp.zeros_like(o_refs[0])
        for d, src, n, part, dst in _in_pieces():
            o_refs[part][:, dst:dst + n] = x_ref[d, :, src:src + n]

    return pl.pallas_call(
        body, name="unpack_w_in", grid=(D_MODEL // bm,),
        in_specs=[pl.BlockSpec((N_DEV, bm, gathered.shape[2]), lambda i: (0, row0 // bm + i, 0))],
        out_specs=[pl.BlockSpec((bm, w), lambda i: (i, 0)) for _, w in IN_PARTS],
        out_shape=[jax.ShapeDtypeStruct((D_MODEL, w), gathered.dtype) for _, w in IN_PARTS], compiler_params=_params(),
    )(gathered)


def _pack_w_in(parts, bm=256):
    def body(*refs):
        o_ref = refs[-1]
        o_ref[...] = jnp.zeros_like(o_ref)
        for d, src, n, part, dst in _in_pieces():
            o_ref[d, :, src:src + n] = refs[part][:, dst:dst + n]

    return pl.pallas_call(
        body, name="pack_w_in", grid=(D_MODEL // bm,),
        in_specs=[pl.BlockSpec((bm, w), lambda i: (i, 0)) for _, w in IN_PARTS],
        out_specs=pl.BlockSpec((N_DEV, bm, 640), lambda i: (0, i, 0)),
        out_shape=jax.ShapeDtypeStruct((N_DEV, D_MODEL, 640), F32), compiler_params=_params(),
    )(*parts)


def _unpack_w_up(gathered, row0, bm=256):
    def body(x_ref, o_ref):
        for d in range(N_DEV):
            o_ref[:, d * UP_SHARD:(d + 1) * UP_SHARD] = x_ref[d, :, :UP_SHARD]

    return pl.pallas_call(
        body, name="unpack_w_up", grid=(D_MODEL // bm,),
        in_specs=[pl.BlockSpec((N_DEV, bm, gathered.shape[2]), lambda i: (0, row0 // bm + i, 0))],
        out_specs=pl.BlockSpec((bm, 2 * D_FF), lambda i: (i, 0)),
        out_shape=jax.ShapeDtypeStruct((D_MODEL, 2 * D_FF), gathered.dtype), compiler_params=_params(),
    )(gathered)


def _pack_w_up(dw_g, dw_v, bm=256):
    half = N_DEV // 2

    def body(g_ref, v_ref, o_ref):
        o_ref[...] = jnp.zeros_like(o_ref)
        for d in range(N_DEV):
            src = g_ref if d < half else v_ref
            c0 = (d % half) * UP_SHARD
            o_ref[d, :, :UP_SHARD] = src[:, c0:c0 + UP_SHARD]

    blk = pl.BlockSpec((bm, D_FF), lambda i: (i, 0))
    return pl.pallas_call(
        body, name="pack_w_up", grid=(D_MODEL // bm,), in_specs=[blk, blk],
        out_specs=pl.BlockSpec((N_DEV, bm, 768), lambda i: (0, i, 0)),
        out_shape=jax.ShapeDtypeStruct((N_DEV, D_MODEL, 768), F32), compiler_params=_params(),
    )(dw_g, dw_v)


def _cols_to_rows(full):
    a, nb = full.shape
    return full.reshape(a, N_DEV, nb // N_DEV).transpose(1, 0, 2)


def _rows_to_cols(blocks):
    n, a, b = blocks.shape
    return blocks.transpose(1, 0, 2).reshape(a, n * b)


def _block_diag_in(bb):
    b3 = bb.reshape(SSM_GROUPS, SSM_STATE, SSM_GROUP_CH).transpose(0, 2, 1)
    eye = jnp.eye(SSM_GROUPS, dtype=bb.dtype)
    return (b3[:, :, None, :] * eye[:, None, :, None]).reshape(SSM_WIDTH, SSM_N)


def _block_diag_out(cc):
    c3 = cc.transpose(0, 2, 1)
    eye = jnp.eye(SSM_GROUPS, dtype=cc.dtype)
    return (c3[:, :, None, :] * eye[:, None, :, None]).reshape(SSM_N, SSM_WIDTH)


def _diag_blocks(mats, rows_per, cols_per):
    nw = mats.shape[0]
    per = SSM_GROUPS // nw
    m5 = mats.reshape(nw, per, rows_per, per, cols_per)
    eye = jnp.eye(per, dtype=mats.dtype)
    return jnp.sum(m5 * eye[None, :, None, :, None], axis=3).reshape(SSM_GROUPS, rows_per, cols_per)


def _layer_weights(gath, offs):
    def blk(grp, n, a):
        r0 = offs[grp][n]
        return gath[grp][:, r0:r0 + a, :]

    o = dict(zip([p for p, _ in IN_PARTS], _unpack_w_in(gath["g640"], offs["g640"]["w_in"])))
    o["wqb"] = _rows_to_cols(blk("g128", "w_q_b", Q_LORA))
    wkv = _rows_to_cols(blk("g128", "w_kv_b", KV_LORA)).reshape(KV_LORA, MLA_HEADS, D_NOPE + D_V)
    o["wk"] = jnp.pad(wkv[:, :, :D_NOPE], ((0, 0), (0, 0), (0, LANES - D_NOPE))).reshape(KV_LORA, MLA_HEADS * LANES)
    o["wv"] = jnp.pad(wkv[:, :, D_NOPE:], ((0, 0), (0, 0), (0, LANES - D_V))).reshape(KV_LORA, MLA_HEADS * LANES)
    o["wv_t"] = o["wv"].T
    wo = _rows_to_cols(blk("g128", "w_o_mla", MLA_HEADS * D_V)).reshape(MLA_HEADS, D_V, D_MODEL)
    o["wo_mla"] = jnp.pad(wo, ((0, 0), (0, LANES - D_V), (0, 0))).reshape(MLA_HEADS * LANES, D_MODEL)
    o["w_o_ssm"] = _rows_to_cols(blk("g128", "w_o_ssm", SSM_WIDTH))
    o["w_o_cross"] = _rows_to_cols(blk("g128", "w_o_cross", X_WIDTH))
    o["w_glu"] = blk("g512", "w_glu", SSM_WIDTH // N_DEV).reshape(SSM_WIDTH, SSM_WIDTH)
    o["w_mem_kv"] = blk("g1024", "w_mem_kv", D_MODEL // N_DEV).reshape(D_MODEL, 2 * X_WIDTH)
    o["w_out"] = blk("g1024", "w_out", D_MODEL // N_DEV).reshape(D_MODEL, D_MODEL)
    o["w_down"] = blk("g1024", "w_down", D_FF // N_DEV).reshape(D_FF, D_MODEL)
    o["w_up"] = _unpack_w_up(gath["g768"], offs["g768"]["w_up"])
    o["w_up_g"], o["w_up_v"] = o["w_up"][:, :D_FF], o["w_up"][:, D_FF:]
    o["conv_w"] = _rows_to_cols(blk("gconv", "conv_w", 3)[:, :, :UP_SHARD])
    return o


def _row(v):
    return v.reshape(1, -1).astype(F32)


def _pad_lanes(v, n=LANES):
    return jnp.pad(v, (0, n - v.shape[0])).reshape(1, n).astype(F32)


def _layer_fwd(x, mem, tabs, w, p, side=None):
    l = x.shape[0]
    rope_c, rope_sa, rope_sb = tabs
    s = {"x": x}
    g_mix, g_qa, g_kva = _row(p["norm_mix_g"]), _row(p["q_a_norm_g"]), _row(p["kv_a_norm_g"])
    g_q, g_k = _pad_lanes(p["q_norm_g"]), _pad_lanes(p["k_norm_g"])

    (h,) = _rowwise("rms_mix", lambda i, r, c: (_rms_f(r[0][...], c[0][...], D_MODEL),), l, 512,
                    [(x, D_MODEL, 0, "row")], [g_mix], [(D_MODEL, BF16)])
    pqkv = _mm("proj_qkv", [(h, w["wqkv"])])
    u = _mm("proj_u", [(h, w["w_u"])])
    xq = _mm("proj_xq", [(h, w["w_xq"])])
    gl = _mm("proj_gate", [(h, w["w_g"])], bm=1024, bn_cap=1024)
    s.update(h=h, pqkv=pqkv, u=u, xq=xq, gl=gl)

    def prep_a(i, r, c):
        return (_rms_f(r[0][:, :Q_LORA], c[0][...], Q_LORA),
                _rms_f(r[0][:, Q_LORA:Q_LORA + KV_LORA], c[1][...], KV_LORA))

    nq, nkv = _rowwise("mla_prep_a", prep_a, l, 512, [(pqkv, QKV_W, 0, "row")], [g_qa, g_kva],
                       [(Q_LORA, BF16), (KV_LORA, BF16)])
    q_raw = _mm("mla_q_b", [(nq, w["wqb"])], bn_cap=1024)
    k_raw = _mm("mla_k_b", [(nkv, w["wk"])], bn_cap=1024)
    v_mla = _mm("mla_v_b", [(nkv, w["wv"])], out_dtype=BF16, bn_cap=1024)
    vt_mla = _mm("mla_vt_b", [(w["wv_t"], nkv)], trans_b=True, out_dtype=BF16, bn_cap=1024)

    def prep_b(i, r, c):
        q_ref, k_ref, kr_ref, c_ref, sa_ref, sb_ref = r
        rc, sa, sb, kr = c_ref[...], sa_ref[...], sb_ref[...], kr_ref[...]
        qs, ks = [], []
        for hd in range(MLA_HEADS):
            cols = slice(hd * LANES, (hd + 1) * LANES)
            qs.append(_rope_f(_rms_f(q_ref[:, cols], c[0][...], D_QK), rc, sa, sb))
            ks.append(_rope_f(_rms_f(k_ref[:, cols] + kr, c[1][...], D_QK), rc, sa, sb))
        return jnp.concatenate(qs, axis=1), jnp.concatenate(ks, axis=1)

    hw = MLA_HEADS * LANES
    kr_blk = (Q_LORA + KV_LORA) // LANES
    tab_ins = [(rope_c, LANES, 0, "row"), (rope_sa, LANES, 0, "row"), (rope_sb, LANES, 0, "row")]
    q, k = _rowwise("mla_prep_b", prep_b, l, 256,
                    [(q_raw, hw, 0, "row"), (k_raw, hw, 0, "row"), (pqkv, LANES, kr_blk, "row")] + tab_ins,
                    [g_q, g_k], [(hw, BF16), (hw, BF16)])
    o_a, lse_a, rode = _attn_fwd("mla_attn_fwd", q, k, vt_mla, qoff=0, koff=0, voff=0, heads=MLA_HEADS, causal=True,
                                 scale=D_QK ** -0.5, bq=1024, bk=1024, side=side)
    ya = _mm("mla_o", [(o_a, w["wo_mla"])], bn_cap=1024)
    s.update(nq=nq, nkv=nkv, q_raw=q_raw, k_raw=k_raw, v_mla=v_mla, q=q, k=k, o_a=o_a, lse_a=lse_a, ya=ya)

    lr = p["ssm_lambda_re"].reshape(SSM_N, 1)
    li = p["ssm_lambda_im"].reshape(SSM_N, 1)
    ldt = jnp.repeat(p["ssm_log_dt"], SSM_STATE).reshape(SSM_N, 1)
    br = p["ssm_b_re"].reshape(SSM_N, SSM_GROUP_CH)
    bi = p["ssm_b_im"].reshape(SSM_N, SSM_GROUP_CH)
    a_re, a_im, bb_re, bb_im = _disc_fwd(lr, li, ldt, br, bi)
    bb_re_d, bb_im_d = _block_diag_in(bb_re).astype(BF16), _block_diag_in(bb_im).astype(BF16)
    cc_re_d = _block_diag_out(p["ssm_c_re"]).astype(BF16)
    cc_imn_d = _block_diag_out(-p["ssm_c_im"]).astype(BF16)
    a_re_row, a_im_row = a_re.reshape(1, SSM_N), a_im.reshape(1, SSM_N)
    d_row = _row(p["ssm_d"])
    b_glu = _row(p["b_glu"])
    s_re, s_im, ypre = _s5_fwd("s5_fwd", u, bb_re_d, bb_im_d, cc_re_d, cc_imn_d, a_re_row, a_im_row)

    def ssm_y(i, r, c):
        return (_gelu(r[0][...] + c[0][...] * r[1][...]),)

    (y_b,) = _rowwise("s5_gelu", ssm_y, l, 512, [(ypre, SSM_WIDTH, 0, "row"), (u, SSM_WIDTH, 0, "row")], [d_row],
                      [(SSM_WIDTH, BF16)])
    z = _mm("s5_glu", [(y_b, w["w_glu"])])

    def ssm_out(i, r, c):
        y = _gelu(r[0][...] + c[0][...] * r[1][...])
        return (y * jax.nn.sigmoid(r[2][...] + c[1][...]),)

    (out_b,) = _rowwise("s5_glu_out", ssm_out, l, 512,
                        [(ypre, SSM_WIDTH, 0, "row"), (u, SSM_WIDTH, 0, "row"), (z, SSM_WIDTH, 0, "row")],
                        [d_row, b_glu], [(SSM_WIDTH, BF16)])
    yb = _mm("s5_o", [(out_b, w["w_o_ssm"])], bn_cap=1024)
    s.update(disc=(lr, li, ldt, br, bi), a_rows=(a_re_row, a_im_row), bb_d=(bb_re_d, bb_im_d),
             cc_d=(cc_re_d, cc_imn_d), s_re=s_re, s_im=s_im, ypre=ypre, y_b=y_b, z=z, out_b=out_b, yb=yb)

    g_mem, g_xq, g_xk = _row(p["mem_norm_g"]), _row(p["xq_norm_g"]), _row(p["xk_norm_g"])
    ml = mem.shape[0]
    (memn,) = _rowwise("rms_mem", lambda i, r, c: (_rms_f(r[0][...], c[0][...], D_MODEL),), ml, 256,
                       [(mem, D_MODEL, 0, "row")], [g_mem], [(D_MODEL, BF16)])
    kvm = _mm("cross_kv", [(memn, w["w_mem_kv"])], bn_cap=1024)

    def headnorm(i, r, c):
        return (jnp.concatenate([_rms_f(r[0][:, hd * LANES:(hd + 1) * LANES], c[0][...], X_HEAD_DIM)
                                 for hd in range(X_HEADS)], axis=1),)

    (xk,) = _rowwise("cross_k_norm", headnorm, ml, 256, [(kvm, X_WIDTH, 0, "row")], [g_xk], [(X_WIDTH, BF16)])
    (xqn,) = _rowwise("cross_q_norm", headnorm, l, 512, [(xq, X_WIDTH, 0, "row")], [g_xq], [(X_WIDTH, BF16)])
    xvt = kvm[:, X_WIDTH:].T.astype(BF16)
    o_c, lse_c, _ = _attn_fwd("cross_attn_fwd", xqn, xk, xvt, qoff=0, koff=0, voff=0, heads=X_HEADS,
                              causal=False, scale=X_HEAD_DIM ** -0.5, bq=1024, bk=256)
    yc = _mm("cross_o", [(o_c, w["w_o_cross"])], bn_cap=1024)
    s.update(memn=memn, kvm=kvm, xk=xk, xqn=xqn, o_c=o_c, lse_c=lse_c, yc=yc)

    b_gate = _row(p["b_gate"])

    def merge(i, r, c):
        acc = None
        for br_ in range(3):
            g = jax.nn.sigmoid(r[br_][...] + c[0][:, br_ * D_MODEL:(br_ + 1) * D_MODEL])
            t = g * r[3 + br_][...]
            acc = t if acc is None else acc + t
        return (acc,)

    gate_ins = [(gl, D_MODEL, b_, "row") for b_ in range(3)]
    (merged,) = _rowwise("merge", merge, l, 256,
                         gate_ins + [(ya, D_MODEL, 0, "row"), (yb, D_MODEL, 0, "row"), (yc, D_MODEL, 0, "row")],
                         [b_gate], [(D_MODEL, BF16)])
    x1 = _mm("mix_out", [(merged, w["w_out"])], add=x, bn_cap=1024)
    s.update(merged=merged, x1=x1)

    g_ffn = _row(p["norm_ffn_g"])
    (h2,) = _rowwise("rms_ffn", lambda i, r, c: (_rms_f(r[0][...], c[0][...], D_MODEL),), l, 512,
                     [(x1, D_MODEL, 0, "row")], [g_ffn], [(D_MODEL, BF16)])
    up = _mm("ffn_up", [(h2, w["w_up"])], bm=1024, bn_cap=1408)
    conv_w = w["conv_w"]
    conv_b = _row(p["conv_b"])

    def conv_glu(i, r, c):
        cg = _conv(r[0], r[2], i, c[0], c[1], 0)
        cv = _conv(r[1], r[3], i, c[0], c[1], D_FF)
        return (cg * jax.nn.sigmoid(cg) * cv,)

    up_ins = [(up, D_FF, 0, "row"), (up, D_FF, 1, "row"), (up, D_FF, 0, "prev"), (up, D_FF, 1, "prev")]
    (act,) = _rowwise("ffn_conv_glu", conv_glu, l, 256, up_ins, [conv_w, conv_b], [(D_FF, BF16)])
    x2 = _mm("ffn_down", [(act, w["w_down"])], add=x1, bm=1024, bn_cap=1024)
    s.update(h2=h2, up=up, act=act, conv_w=conv_w, conv_b=conv_b)
    return x2, s, rode


def _conv(x_ref, halo_ref, i, w_ref, b_ref, col0):
    x = x_ref[...]
    cols = slice(col0, col0 + D_FF)
    return (w_ref[0:1, cols] * _shift_down(x, halo_ref, i, 2) + w_ref[1:2, cols] * _shift_down(x, halo_ref, i, 1)
            + w_ref[2:3, cols] * x + b_ref[:, cols])


def _layer_bwd(dx2, dx2_b, s, mem, tabs, w, p, side=None):
    l = dx2.shape[0]
    rope_c, rope_sa, rope_sb = tabs
    x, x1 = s["x"], s["x1"]
    g = {}

    dact = _mm("ffn_down_dx", [(dx2_b, w["w_down"])], trans_b=True, bn_cap=1408)
    g["w_down"] = _mm_tn("ffn_down_dw", s["act"], dx2_b, bm_cap=1408).reshape(N_DEV, D_FF // N_DEV, D_MODEL)
    up = s["up"]
    nblk_c = l // min(256, l)

    def conv_bwd_a(i, r, c):
        outs, accs = [], []
        for half in range(2):
            x_ref, halo_ref = r[1 + half], r[3 + half]
            col0 = half * D_FF
            cols = slice(col0, col0 + D_FF)
            xv = x_ref[...]
            xm1, xm2 = _shift_down(xv, halo_ref, i, 1), _shift_down(xv, halo_ref, i, 2)
            cv = c[0][0:1, cols] * xm2 + c[0][1:2, cols] * xm1 + c[0][2:3, cols] * xv + c[1][:, cols]
            outs.append((cv, xv, xm1, xm2))
        (cg, xg, xg1, xg2), (cvv, xv, xv1, xv2) = outs
        sig = jax.nn.sigmoid(cg)
        da = r[0][...]
        dcv = da * (cg * sig)
        dcg = da * cvv * (sig * (1.0 + cg * (1.0 - sig)))
        for d, (x0, xa, xb) in ((dcg, (xg, xg1, xg2)), (dcv, (xv, xv1, xv2))):
            accs += [jnp.sum(d, axis=0, keepdims=True), jnp.sum(d * xb, axis=0, keepdims=True),
                     jnp.sum(d * xa, axis=0, keepdims=True), jnp.sum(d * x0, axis=0, keepdims=True)]
        return (dcg, dcv, *accs)

    up_ins = [(up, D_FF, 0, "row"), (up, D_FF, 1, "row"), (up, D_FF, 0, "prev"), (up, D_FF, 1, "prev")]
    res = _rowwise("ffn_conv_glu_bwd", conv_bwd_a, l, 256, [(dact, D_FF, 0, "row")] + up_ins,
                   [s["conv_w"], s["conv_b"]], [(D_FF, F32), (D_FF, F32)], [(1, D_FF)] * 8)
    dcg, dcv = res[0], res[1]
    db_g, dw0_g, dw1_g, dw2_g, db_v, dw0_v, dw1_v, dw2_v = res[2:]
    g["conv_b"] = jnp.concatenate([db_g, db_v], axis=1)[0]
    g["conv_w"] = _cols_to_rows(jnp.concatenate(
        [jnp.concatenate([dw0_g, dw0_v], axis=1), jnp.concatenate([dw1_g, dw1_v], axis=1),
         jnp.concatenate([dw2_g, dw2_v], axis=1)], axis=0))

    def conv_bwd_b(i, r, c):
        outs = []
        for half in range(2):
            d = r[half][...]
            cols = slice(half * D_FF, (half + 1) * D_FF)
            outs.append(c[0][2:3, cols] * d + c[0][1:2, cols] * _shift_up(d, r[2 + half], i, nblk_c, 1)
                        + c[0][0:1, cols] * _shift_up(d, r[2 + half], i, nblk_c, 2))
        return tuple(outs)

    dup_g, dup_v = _rowwise("ffn_conv_bwd_x", conv_bwd_b, l, 256,
                            [(dcg, D_FF, 0, "row"), (dcv, D_FF, 0, "row"), (dcg, D_FF, 0, "next"),
                             (dcv, D_FF, 0, "next")], [s["conv_w"]], [(D_FF, BF16), (D_FF, BF16)])
    dh2 = _mm("ffn_up_dx", [(dup_g, w["w_up_g"]), (dup_v, w["w_up_v"])], trans_b=True, bn_cap=1024)
    g["w_up"] = _pack_w_up(_mm_tn("ffn_up_dw_g", s["h2"], dup_g), _mm_tn("ffn_up_dw_v", s["h2"], dup_v))

    def rms_bwd_res(i, r, c):
        dx, dg = _rms_b(r[0][...], c[0][...], r[1][...], D_MODEL)
        dx = dx + r[2][...]
        return (dx, dx, dg)

    dx1, dx1_b, g["norm_ffn_g"] = _rowwise(
        "rms_ffn_bwd", rms_bwd_res, l, 512, [(x1, D_MODEL, 0, "row"), (dh2, D_MODEL, 0, "row"), (dx2, D_MODEL, 0, "row")],
        [_row(p["norm_ffn_g"])], [(D_MODEL, F32), (D_MODEL, BF16)], [(1, D_MODEL)])

    dmerged = _mm("mix_out_dx", [(dx1_b, w["w_out"])], trans_b=True, bn_cap=1024)
    g["w_out"] = _mm_tn("mix_out_dw", s["merged"], dx1_b).reshape(N_DEV, D_MODEL // N_DEV, D_MODEL)
    gl, ya, yb, yc = s["gl"], s["ya"], s["yb"], s["yc"]

    def merge_bwd(i, r, c):
        dm = r[0][...]
        dys, dgs = [], []
        for b_ in range(3):
            gate = jax.nn.sigmoid(r[1 + b_][...] + c[0][:, b_ * D_MODEL:(b_ + 1) * D_MODEL])
            dys.append(dm * gate)
            dgs.append(dm * r[4 + b_][...] * (gate * (1.0 - gate)))
        dgl = jnp.concatenate(dgs, axis=1)
        return (*dys, dgl, jnp.sum(dgl, axis=0, keepdims=True))

    gate_ins = [(gl, D_MODEL, b_, "row") for b_ in range(3)]
    dya, dyb, dyc, dgl, db_gate = _rowwise(
        "merge_bwd", merge_bwd, l, 256,
        [(dmerged, D_MODEL, 0, "row")] + gate_ins + [(ya, D_MODEL, 0, "row"), (yb, D_MODEL, 0, "row"),
                                                     (yc, D_MODEL, 0, "row")],
        [_row(p["b_gate"])], [(D_MODEL, BF16)] * 3 + [(3 * D_MODEL, BF16)], [(1, 3 * D_MODEL)])
    g["b_gate"] = db_gate[0]

    do_c = _mm("cross_o_dx", [(dyc, w["w_o_cross"])], trans_b=True, out_dtype=BF16)
    g["w_o_cross"] = _cols_to_rows(_mm_tn("cross_o_dw", s["o_c"], dyc))
    kvm = s["kvm"]
    delta_c = _attn_delta("cross_attn_delta", s["o_c"], do_c, heads=X_HEADS, bq=2048)
    dxqn, dxk, dxv, _ = _attn_bwd("cross_attn_bwd", s["xqn"], s["xk"], s["xk"].T, kvm, do_c, s["lse_c"], delta_c,
                                  qoff=0, koff=0, voff=X_HEADS, heads=X_HEADS, causal=False,
                                  scale=X_HEAD_DIM ** -0.5, bq=1024, bk=256)
    ml = mem.shape[0]

    def headnorm_bwd(i, r, c):
        dxs, dg = [], None
        for hd in range(X_HEADS):
            cols = slice(hd * LANES, (hd + 1) * LANES)
            dx_h, dg_h = _rms_b(r[0][:, cols], c[0][...], r[1][:, cols], X_HEAD_DIM)
            dxs.append(dx_h)
            dg = dg_h if dg is None else dg + dg_h
        return (jnp.concatenate(dxs, axis=1), dg)

    dxq, dg_xq = _rowwise("cross_q_norm_bwd", headnorm_bwd, l, 512,
                          [(s["xq"], X_WIDTH, 0, "row"), (dxqn, X_WIDTH, 0, "row")], [_row(p["xq_norm_g"])],
                          [(X_WIDTH, BF16)], [(1, X_HEAD_DIM)])
    dkvm_k, dg_xk = _rowwise("cross_k_norm_bwd", headnorm_bwd, ml, 256,
                             [(kvm, X_WIDTH, 0, "row"), (dxk, X_WIDTH, 0, "row")], [_row(p["xk_norm_g"])],
                             [(X_WIDTH, F32)], [(1, X_HEAD_DIM)])
    g["xq_norm_g"], g["xk_norm_g"] = dg_xq[0], dg_xk[0]
    dkvm = jnp.concatenate([dkvm_k, dxv], axis=1)
    g["w_mem_kv"] = _mm_tn("cross_kv_dw", s["memn"], dkvm).reshape(N_DEV, D_MODEL // N_DEV, 2 * X_WIDTH)
    dmemn = _mm("cross_kv_dx", [(dkvm, w["w_mem_kv"])], trans_b=True, bn_cap=1024)

    def rms_bwd_gain_only(i, r, c):
        return (_rms_b(r[0][...], c[0][...], r[1][...], D_MODEL)[1],)

    (dg_mem,) = _rowwise("rms_mem_bwd", rms_bwd_gain_only, ml, 256,
                         [(mem, D_MODEL, 0, "row"), (dmemn, D_MODEL, 0, "row")], [_row(p["mem_norm_g"])], [],
                         [(1, D_MODEL)])
    g["mem_norm_g"] = dg_mem[0]

    dout_b = _mm("s5_o_dx", [(dyb, w["w_o_ssm"])], trans_b=True)
    g["w_o_ssm"] = _cols_to_rows(_mm_tn("s5_o_dw", s["out_b"], dyb))
    ypre, u, z = s["ypre"], s["u"], s["z"]
    d_row, b_glu = _row(p["ssm_d"]), _row(p["b_glu"])
    yuz = [(ypre, SSM_WIDTH, 0, "row"), (u, SSM_WIDTH, 0, "row"), (z, SSM_WIDTH, 0, "row")]

    def glu_bwd_z(i, r, c):
        y = _gelu(r[1][...] + c[0][...] * r[2][...])
        sg = jax.nn.sigmoid(r[3][...] + c[1][...])
        dz = r[0][...] * y * (sg * (1.0 - sg))
        return (dz, jnp.sum(dz, axis=0, keepdims=True))

    dz, db_glu = _rowwise("s5_glu_bwd_z", glu_bwd_z, l, 512, [(dout_b, SSM_WIDTH, 0, "row")] + yuz, [d_row, b_glu],
                          [(SSM_WIDTH, BF16)], [(1, SSM_WIDTH)])
    g["b_glu"] = db_glu[0]
    g["w_glu"] = _mm_tn("s5_glu_dw", s["y_b"], dz).reshape(N_DEV, SSM_WIDTH // N_DEV, SSM_WIDTH)
    dy2 = _mm("s5_glu_dx", [(dz, w["w_glu"])], trans_b=True)

    def gelu_bwd(i, r, c):
        t = r[2][...] + c[0][...] * r[3][...]
        sg = jax.nn.sigmoid(r[4][...] + c[1][...])
        dt = (r[0][...] * sg + r[1][...]) * _gelu_grad(t)
        return (dt, c[0][...] * dt, jnp.sum(dt * r[3][...], axis=0, keepdims=True))

    dypre, du_skip, dd = _rowwise(
        "s5_gelu_bwd", gelu_bwd, l, 512, [(dout_b, SSM_WIDTH, 0, "row"), (dy2, SSM_WIDTH, 0, "row")] + yuz,
        [d_row, b_glu], [(SSM_WIDTH, BF16), (SSM_WIDTH, F32)], [(1, SSM_WIDTH)])
    g["ssm_d"] = dd.reshape(SSM_GROUPS, SSM_GROUP_CH)
    cc_re_d, cc_imn_d = s["cc_d"]
    bb_re_d, bb_im_d = s["bb_d"]
    a_re_row, a_im_row = s["a_rows"]
    s_re, s_im = s["s_re"], s["s_im"]
    du, da_re, da_im, dbb_re_d, dbb_im_d, dcc_re, dcc_imn = _s5_bwd(
        "s5_bwd", dypre, du_skip, u, s_re, s_im, bb_re_d, bb_im_d, cc_re_d, cc_imn_d, a_re_row, -a_im_row)
    g["ssm_c_re"] = _diag_blocks(dcc_re, SSM_STATE, SSM_GROUP_CH).transpose(0, 2, 1)
    g["ssm_c_im"] = -_diag_blocks(dcc_imn, SSM_STATE, SSM_GROUP_CH).transpose(0, 2, 1)
    dbb_re = _diag_blocks(dbb_re_d, SSM_GROUP_CH, SSM_STATE).transpose(0, 2, 1).reshape(SSM_N, SSM_GROUP_CH)
    dbb_im = _diag_blocks(dbb_im_d, SSM_GROUP_CH, SSM_STATE).transpose(0, 2, 1).reshape(SSM_N, SSM_GROUP_CH)
    dlr, dli, dldt, dbr, dbi = _disc_bwd(*s["disc"], da_re.reshape(SSM_N, 1), da_im.reshape(SSM_N, 1), dbb_re, dbb_im)
    g["ssm_lambda_re"] = dlr.reshape(SSM_GROUPS, SSM_STATE)
    g["ssm_lambda_im"] = dli.reshape(SSM_GROUPS, SSM_STATE)
    g["ssm_log_dt"] = dldt.reshape(SSM_GROUPS, SSM_STATE).sum(axis=1)
    g["ssm_b_re"] = dbr.reshape(SSM_GROUPS, SSM_STATE, SSM_GROUP_CH)
    g["ssm_b_im"] = dbi.reshape(SSM_GROUPS, SSM_STATE, SSM_GROUP_CH)

    do_a = _mm("mla_o_dx", [(dya, w["wo_mla"])], trans_b=True, out_dtype=BF16, bn_cap=1024)
    dwo = _mm_tn("mla_o_dw", s["o_a"], dya)
    g["w_o_mla"] = _cols_to_rows(dwo.reshape(MLA_HEADS, LANES, D_MODEL)[:, :D_V].reshape(MLA_HEADS * D_V, D_MODEL))
    delta_a = _attn_delta("mla_attn_delta", s["o_a"], do_a, heads=MLA_HEADS, bq=2048)
    dq, dk, dv, rode = _attn_bwd("mla_attn_bwd", s["q"], s["k"], s["k"].T, s["v_mla"], do_a, s["lse_a"], delta_a,
                                 qoff=0, koff=0, voff=0, heads=MLA_HEADS, causal=True, scale=D_QK ** -0.5, bq=1024,
                                 bk=1024, side=side)
    hw = MLA_HEADS * LANES
    kr_blk = (Q_LORA + KV_LORA) // LANES
    pqkv = s["pqkv"]
    g_q, g_k = _pad_lanes(p["q_norm_g"]), _pad_lanes(p["k_norm_g"])
    lane = lax.broadcasted_iota(jnp.int32, (1, LANES), 1)
    kr_mask = jnp.logical_and(lane >= KR_LO, lane < KR_LO + D_ROPE).astype(F32)

    def prep_b_bwd(i, r, c):
        dq_ref, dk_ref, q_ref, k_ref, kr_ref, c_ref, sa_ref, sb_ref = r
        rc, sa, sb, kr = c_ref[...], sa_ref[...], sb_ref[...], kr_ref[...]
        dqs, dks, dkr, dgq, dgk = [], [], None, None, None
        for hd in range(MLA_HEADS):
            cols = slice(hd * LANES, (hd + 1) * LANES)
            dxq, dgq_h = _rms_b(q_ref[:, cols], c[0][...], _rope_b(dq_ref[:, cols], rc, sa, sb), D_QK)
            dxk, dgk_h = _rms_b(k_ref[:, cols] + kr, c[1][...], _rope_b(dk_ref[:, cols], rc, sa, sb), D_QK)
            dqs.append(dxq)
            dks.append(dxk)
            dkr = dxk if dkr is None else dkr + dxk
            dgq = dgq_h if dgq is None else dgq + dgq_h
            dgk = dgk_h if dgk is None else dgk + dgk_h
        return (jnp.concatenate(dqs, axis=1), jnp.concatenate(dks, axis=1), dkr * c[2][...], dgq, dgk)

    tab_ins = [(rope_c, LANES, 0, "row"), (rope_sa, LANES, 0, "row"), (rope_sb, LANES, 0, "row")]
    dq_raw, dk_raw, dkr, dg_q, dg_k = _rowwise(
        "mla_prep_b_bwd", prep_b_bwd, l, 256,
        [(dq, hw, 0, "row"), (dk, hw, 0, "row"), (s["q_raw"], hw, 0, "row"), (s["k_raw"], hw, 0, "row"),
         (pqkv, LANES, kr_blk, "row")] + tab_ins, [g_q, g_k, kr_mask],
        [(hw, BF16), (hw, BF16), (LANES, F32)], [(1, LANES), (1, LANES)])
    g["q_norm_g"], g["k_norm_g"] = dg_q[0, :D_QK], dg_k[0, :D_QK]
    dnq = _mm("mla_q_b_dx", [(dq_raw, w["wqb"])], trans_b=True)
    dnkv = _mm("mla_kv_b_dx", [(dk_raw, w["wk"]), (dv, w["wv"])], trans_b=True)
    dwqb = _mm_tn("mla_q_b_dw", s["nq"], dq_raw)
    g["w_q_b"] = _cols_to_rows(dwqb)
    dwk = _mm_tn("mla_k_b_dw", s["nkv"], dk_raw).reshape(KV_LORA, MLA_HEADS, LANES)[:, :, :D_NOPE]
    dwv = _mm_tn("mla_v_b_dw", s["nkv"], dv).reshape(KV_LORA, MLA_HEADS, LANES)[:, :, :D_V]
    g["w_kv_b"] = jnp.concatenate([dwk, dwv], axis=2).transpose(1, 0, 2)

    def prep_a_bwd(i, r, c):
        dcq, dgqa = _rms_b(r[0][:, :Q_LORA], c[0][...], r[1][...], Q_LORA)
        dckv, dgkva = _rms_b(r[0][:, Q_LORA:Q_LORA + KV_LORA], c[1][...], r[2][...], KV_LORA)
        return (jnp.concatenate([dcq, dckv, r[3][...]], axis=1), dgqa, dgkva)

    dpqkv, dg_qa, dg_kva = _rowwise(
        "mla_prep_a_bwd", prep_a_bwd, l, 512,
        [(pqkv, QKV_W, 0, "row"), (dnq, Q_LORA, 0, "row"), (dnkv, KV_LORA, 0, "row"), (dkr, LANES, 0, "row")],
        [_row(p["q_a_norm_g"]), _row(p["kv_a_norm_g"])], [(QKV_W, BF16)], [(1, Q_LORA), (1, KV_LORA)])
    g["q_a_norm_g"], g["kv_a_norm_g"] = dg_qa[0], dg_kva[0]

    h = s["h"]
    dh = _mm("proj_dx", [(dpqkv, w["wqkv"]), (du, w["w_u"]), (dxq, w["w_xq"]), (dgl, w["w_g"])], trans_b=True,
             bn_cap=1024)
    dwqkv = _mm_tn("proj_qkv_dw", h, dpqkv)
    g["w_in"] = _pack_w_in([dwqkv, _mm_tn("proj_u_dw", h, du), _mm_tn("proj_xq_dw", h, dxq),
                            _mm_tn("proj_gate_dw", h, dgl)])
    dx, dx_b, dg_mix = _rowwise(
        "rms_mix_bwd", rms_bwd_res, l, 512, [(x, D_MODEL, 0, "row"), (dh, D_MODEL, 0, "row"), (dx1, D_MODEL, 0, "row")],
        [_row(p["norm_mix_g"])], [(D_MODEL, F32), (D_MODEL, BF16)], [(1, D_MODEL)])
    g["norm_mix_g"] = dg_mix[0]
    g["norm_ffn_g"] = g["norm_ffn_g"][0]
    return dx, dx_b, g, rode


def _rope_tables(positions):
    inv_freq = ROPE_THETA ** (-jnp.arange(0, D_ROPE, 2, dtype=F32) / D_ROPE)
    ang = positions.astype(F32)[:, None] * inv_freq
    cos, sin = jnp.cos(ang), jnp.sin(ang)
    l = positions.shape[0]
    one, zero = jnp.ones((l, D_NOPE), F32), lambda n: jnp.zeros((l, n), F32)
    pad = LANES - D_QK
    rope_c = jnp.concatenate([one, cos, cos, zero(pad)], axis=1)
    rope_sa = jnp.concatenate([zero(D_NOPE), -sin, zero(16), zero(pad)], axis=1)
    rope_sb = jnp.concatenate([zero(D_NOPE + 16), sin, zero(pad)], axis=1)
    return rope_c, rope_sa, rope_sb


def kernel(x, mem, positions, norm_mix_g, w_in, q_a_norm_g, w_q_b, kv_a_norm_g, w_kv_b, q_norm_g, k_norm_g, w_o_mla, ssm_lambda_re, ssm_lambda_im, ssm_log_dt, ssm_b_re, ssm_b_im, ssm_c_re, ssm_c_im, ssm_d, w_glu, b_glu, w_o_ssm, mem_norm_g, w_mem_kv, xq_norm_g, xk_norm_g, w_o_cross, b_gate, w_out, norm_ffn_g, w_up, conv_w, conv_b, w_down, loss_target, m_norm_mix_g, m_w_in, m_q_a_norm_g, m_w_q_b, m_kv_a_norm_g, m_w_kv_b, m_q_norm_g, m_k_norm_g, m_w_o_mla, m_ssm_lambda_re, m_ssm_lambda_im, m_ssm_log_dt, m_ssm_b_re, m_ssm_b_im, m_ssm_c_re, m_ssm_c_im, m_ssm_d, m_w_glu, m_b_glu, m_w_o_ssm, m_mem_norm_g, m_w_mem_kv, m_xq_norm_g, m_xk_norm_g, m_w_o_cross, m_b_gate, m_w_out, m_norm_ffn_g, m_w_up, m_conv_w, m_conv_b, m_w_down, v_norm_mix_g, v_w_in, v_q_a_norm_g, v_w_q_b, v_kv_a_norm_g, v_w_kv_b, v_q_norm_g, v_k_norm_g, v_w_o_mla, v_ssm_lambda_re, v_ssm_lambda_im, v_ssm_log_dt, v_ssm_b_re, v_ssm_b_im, v_ssm_c_re, v_ssm_c_im, v_ssm_d, v_w_glu, v_b_glu, v_w_o_ssm, v_mem_norm_g, v_w_mem_kv, v_xq_norm_g, v_xk_norm_g, v_w_o_cross, v_b_gate, v_w_out, v_norm_ffn_g, v_w_up, v_conv_w, v_conv_b, v_w_down):
    a = dict(locals())
    wts = {n: a[n] for n in WEIGHT_ORDER}
    m_in = {n: a["m_" + n] for n in WEIGHT_ORDER}
    v_in = {n: a["v_" + n] for n in WEIGHT_ORDER}
    depth = norm_mix_g.shape[0]
    x0, mem0, pos0, tgt = x[0], mem[0], positions[0], loss_target[0]
    l = x0.shape[0]
    offs = {grp: _group_rows(params) for grp, _, params in GROUPS}
    names = [grp for grp, _, _ in GROUPS]

    def layer_srcs(i):
        return [_group_local(width, params, wts, i, F32 if grp == "gconv" else BF16) for grp, width, params in GROUPS]

    tabs = _rope_tables(pos0)
    layer_p = [{n: wts[n][i] for n in REPLICATED} for i in range(depth)]
    gath = dict(zip(names, _gather_all("gather_weights", layer_srcs(0))))

    saved, layer_w = [], []
    xc = x0
    for i in range(depth):
        layer_w.append(_layer_weights(gath, offs))
        side = None
        if i + 1 < depth:
            nxt = layer_srcs(i + 1)
            side = _Side(nxt, _gather_shapes(nxt), GATHER_SEMS, (_gather_start, _gather_relay, _gather_finish))
        xc, s, rode = _layer_fwd(xc, mem0, tabs, layer_w[i], layer_p[i], side)
        gath = dict(zip(names, rode))
        saved.append(s)

    def loss_fn(i, r, c):
        d = r[0][...] - r[1][...]
        dy = d * (1.0 / D_MODEL)
        return (dy, dy, jnp.sum(d * d, axis=0, keepdims=True))

    dy, dy_b, sq = _rowwise("loss", loss_fn, l, 512, [(xc, D_MODEL, 0, "row"), (tgt, D_MODEL, 0, "row")], [],
                            [(D_MODEL, F32), (D_MODEL, BF16)], [(1, D_MODEL)])
    loss = lax.psum(0.5 * jnp.sum(sq) / D_MODEL, ("x", "y", "c"))

    def core_sums(i, extra=()):
        sends = []
        for _, width, params in GROUPS:
            blocks = [jnp.pad(grads[i][n], ((0, 0), (0, _rows8(rows) - rows), (0, width - grads[i][n].shape[2])))
                      for n, rows, _ in params]
            sends.append(jnp.concatenate(blocks, axis=1))
        sends += list(extra)
        got = _pair_exchange("exchange_grads_core_l%d" % i, sends)
        tags = names + ["rep"]
        return [_pair_sum("grad_pair_sum_l%d_%s" % (i, tags[k]), s_, g_) for k, (s_, g_) in enumerate(zip(sends, got))]

    grads, rcvs = [None] * depth, [None] * depth
    dxc, dxc_b = dy, dy_b
    waiting = None
    for i in reversed(range(depth)):
        side = None if waiting is None else _Side(waiting, _chip_shapes(waiting), CHIP_SEMS, (_chip_start, _chip_finish))
        dxc, dxc_b, grads[i], rode = _layer_bwd(dxc, dxc_b, saved[i], mem0, tabs, layer_w[i], layer_p[i], side)
        if waiting is not None:
            rcvs[i + 1] = rode
        waiting = core_sums(i) if i > 0 else None
    grad_x = dxc[None]
    rep_flat = jnp.concatenate([jnp.stack([grads[i][n] for i in range(depth)]).reshape(-1) for n in REPLICATED])
    rep_send = _rep_rows(jnp.broadcast_to(rep_flat[None], (N_DEV, rep_flat.shape[0])))
    last = _chip_exchange("exchange_grads_chip", core_sums(0, [rep_send]))
    rcvs[0], rcv_rep = last[:-1], last[-1]

    per_layer = {}
    for i in range(depth):
        for k, (grp, width, params) in enumerate(GROUPS):
            local = [_group_local(width, params, d, i, F32) for d in (wts, m_in, v_in)]
            res = _adamw("adamw_l%d_%s" % (i, grp), rcvs[i][k], *local)
            for tag, arr in zip(("grad", "delta", "m", "v"), res):
                for n, rows, cols in params:
                    per_layer[(tag, n, i)] = arr[offs[grp][n]:offs[grp][n] + rows, :cols]
    outs = {(tag, n): jnp.stack([per_layer[(tag, n, i)] for i in range(depth)])
            for tag in ("grad", "delta", "m", "v") for _, _, params in GROUPS for n, _, _ in params}
    rep_local = [_rep_rows(jnp.concatenate([d[n].astype(F32).reshape(-1) for n in REPLICATED])) for d in (wts, m_in, v_in)]
    res = _adamw("adamw_rep", rcv_rep, *rep_local)
    for tag, arr in zip(("grad", "delta", "m", "v"), res):
        flat, off = arr.reshape(-1), 0
        for n in REPLICATED:
            cnt = wts[n].size
            outs[(tag, n)] = flat[off:off + cnt].reshape(wts[n].shape)
            off += cnt
    result = [loss, grad_x]
    for tag in ("grad", "delta", "m", "v"):
        result += [outs[(tag, n)] for n in WEIGHT_ORDER]
    return tuple(result)
```

```python
import collections
import math

import jax
import jax.numpy as jnp
from jax import lax
from jax.experimental import pallas as pl
from jax.experimental.pallas import tpu as pltpu

F32 = jnp.float32
BF16 = jnp.bfloat16

N_DEV = 8
LANES = 128
LOG2E = math.log2(math.e)
VMEM_LIMIT_BYTES = 56 * 1024 * 1024

D_MODEL = 1024
EPS = 1e-6
MLA_HEADS = 8
Q_LORA = 384
KV_LORA = 256
D_NOPE = 64
D_ROPE = 32
D_QK = D_NOPE + D_ROPE
D_V = 64
ROPE_THETA = 10000.0
SSM_GROUPS = 32
SSM_GROUP_CH = 16
SSM_WIDTH = 512
SSM_STATE = 64
SSM_N = SSM_GROUPS * SSM_STATE
X_HEADS = 4
X_HEAD_DIM = 128
X_WIDTH = 512
D_FF = 2816
IN_WIDTH = Q_LORA + KV_LORA + D_ROPE + SSM_WIDTH + X_WIDTH + 3 * D_MODEL
QKV_W = Q_LORA + KV_LORA + LANES
KR_LO = D_NOPE

ADAM_LR = 0.001
ADAM_B1 = 0.9
ADAM_B2 = 0.999
ADAM_EPS = 1e-08
ADAM_WD = 0.01
ADAM_STEP = 10

REPLICATED = (
    "norm_mix_g", "q_a_norm_g", "kv_a_norm_g", "q_norm_g", "k_norm_g", "ssm_lambda_re", "ssm_lambda_im",
    "ssm_log_dt", "ssm_b_re", "ssm_b_im", "ssm_c_re", "ssm_c_im", "ssm_d", "b_glu", "mem_norm_g",
    "xq_norm_g", "xk_norm_g", "b_gate", "norm_ffn_g", "conv_b",
)
WEIGHT_ORDER = (
    "norm_mix_g", "w_in", "q_a_norm_g", "w_q_b", "kv_a_norm_g", "w_kv_b", "q_norm_g", "k_norm_g", "w_o_mla",
    "ssm_lambda_re", "ssm_lambda_im", "ssm_log_dt", "ssm_b_re", "ssm_b_im", "ssm_c_re", "ssm_c_im", "ssm_d",
    "w_glu", "b_glu", "w_o_ssm", "mem_norm_g", "w_mem_kv", "xq_norm_g", "xk_norm_g", "w_o_cross", "b_gate",
    "w_out", "norm_ffn_g", "w_up", "conv_w", "conv_b", "w_down",
)


def _params(**kw):
    return pltpu.CompilerParams(vmem_limit_bytes=VMEM_LIMIT_BYTES, **kw)


def _pick(n, cap):
    if n <= cap:
        return n
    best = None
    for m in range(LANES, cap + 1, LANES):
        if n % m == 0:
            best = m
    assert best is not None, n
    return best


_NN = (((1,), (0,)), ((), ()))
_NT = (((1,), (1,)), ((), ()))
_TN = (((0,), (0,)), ((), ()))


def _dot(a, b, dn):
    return lax.dot_general(a.astype(BF16), b.astype(BF16), dn, preferred_element_type=F32)


def _mm(name, pairs, *, trans_b=False, add=None, out_dtype=F32, bm=512, bn_cap=512):
    m = pairs[0][0].shape[0]
    n = pairs[0][1].shape[0 if trans_b else 1]
    bm = min(bm, m)
    bn = _pick(n, bn_cap)
    npair = len(pairs)

    def body(*refs):
        o_ref = refs[-1]
        acc = None
        for p in range(npair):
            d = _dot(refs[2 * p][...], refs[2 * p + 1][...], _NT if trans_b else _NN)
            acc = d if acc is None else acc + d
        if add is not None:
            acc = acc + refs[2 * npair][...]
        o_ref[...] = acc.astype(out_dtype)

    in_specs, args = [], []
    for a, b in pairs:
        k = a.shape[1]
        in_specs.append(pl.BlockSpec((bm, k), lambda i, j: (i, 0)))
        if trans_b:
            in_specs.append(pl.BlockSpec((bn, k), lambda i, j: (j, 0)))
        else:
            in_specs.append(pl.BlockSpec((k, bn), lambda i, j: (0, j)))
        args += [a, b]
    if add is not None:
        in_specs.append(pl.BlockSpec((bm, bn), lambda i, j: (i, j)))
        args.append(add)
    return pl.pallas_call(
        body, name=name, grid=(m // bm, n // bn), in_specs=in_specs,
        out_specs=pl.BlockSpec((bm, bn), lambda i, j: (i, j)),
        out_shape=jax.ShapeDtypeStruct((m, n), out_dtype), compiler_params=_params(),
    )(*args)


def _mm_tn(name, a, b, *, bm_cap=512, bn_cap=1536, bk=1024):
    l, m = a.shape
    n = b.shape[1]
    bm, bn, bk = _pick(m, bm_cap), _pick(n, bn_cap), min(bk, l)

    def body(a_ref, b_ref, o_ref):
        @pl.when(pl.program_id(2) == 0)
        def _():
            o_ref[...] = jnp.zeros_like(o_ref)

        o_ref[...] += _dot(a_ref[...], b_ref[...], _TN)

    return pl.pallas_call(
        body, name=name, grid=(m // bm, n // bn, l // bk),
        in_specs=[pl.BlockSpec((bk, bm), lambda i, j, k: (k, i)), pl.BlockSpec((bk, bn), lambda i, j, k: (k, j))],
        out_specs=pl.BlockSpec((bm, bn), lambda i, j, k: (i, j)),
        out_shape=jax.ShapeDtypeStruct((m, n), F32), compiler_params=_params(),
    )(a, b)


def _rowwise(name, fn, nrows, bm, row_ins, consts, row_outs, acc_outs=()):
    bm = min(bm, nrows)
    nblk = nrows // bm
    sub = bm // 8
    nin, nc, nro = len(row_ins), len(consts), len(row_outs)

    def body(*refs):
        i = pl.program_id(0)
        outs = fn(i, refs[:nin], refs[nin:nin + nc])
        o_refs = refs[nin + nc:nin + nc + nro]
        a_refs = refs[nin + nc + nro:]
        for r, v in zip(o_refs, outs[:nro]):
            r[...] = v.astype(r.dtype)
        if a_refs:
            @pl.when(i == 0)
            def _():
                for r in a_refs:
                    r[...] = jnp.zeros_like(r)

            for r, v in zip(a_refs, outs[nro:]):
                r[...] += v

    in_specs, args = [], []
    for arr, w, cb, kind in row_ins:
        if kind == "row":
            in_specs.append(pl.BlockSpec((bm, w), lambda i, cb=cb: (i, cb)))
        elif kind == "prev":
            in_specs.append(pl.BlockSpec((8, w), lambda i, cb=cb: (jnp.maximum(i * sub - 1, 0), cb)))
        else:
            in_specs.append(pl.BlockSpec((8, w), lambda i, cb=cb: (jnp.minimum((i + 1) * sub, nrows // 8 - 1), cb)))
        args.append(arr)
    for c in consts:
        in_specs.append(pl.BlockSpec(c.shape, lambda i: (0, 0)))
        args.append(c)
    out_specs = [pl.BlockSpec((bm, w), lambda i: (i, 0)) for w, _ in row_outs]
    out_specs += [pl.BlockSpec(s, lambda i: (0, 0)) for s in acc_outs]
    out_shape = [jax.ShapeDtypeStruct((nrows, w), dt) for w, dt in row_outs]
    out_shape += [jax.ShapeDtypeStruct(s, F32) for s in acc_outs]
    res = pl.pallas_call(
        body, name=name, grid=(nblk,), in_specs=in_specs, out_specs=out_specs, out_shape=out_shape,
        compiler_params=_params(),
    )(*args)
    return res


def _rms_f(x, g, n):
    r = lax.rsqrt(jnp.sum(x * x, axis=-1, keepdims=True) * (1.0 / n) + EPS)
    return x * r * g


def _rms_b(x, g, dy, n):
    r = lax.rsqrt(jnp.sum(x * x, axis=-1, keepdims=True) * (1.0 / n) + EPS)
    gx = dy * g
    dx = r * gx - x * (r * r * r * (jnp.sum(x * gx, axis=-1, keepdims=True) * (1.0 / n)))
    dg = jnp.sum(dy * (x * r), axis=0, keepdims=True)
    return dx, dg


def _rope_f(x, c, sa, sb):
    return x * c + pltpu.roll(x, LANES - 16, 1) * sa + pltpu.roll(x, 16, 1) * sb


def _rope_b(g, c, sa, sb):
    return g * c + pltpu.roll(g * sa, 16, 1) + pltpu.roll(g * sb, LANES - 16, 1)


def _gelu(x):
    c = math.sqrt(2.0 / math.pi)
    return 0.5 * x * (1.0 + jnp.tanh(c * (x + 0.044715 * (x * x * x))))


def _gelu_grad(x):
    c = math.sqrt(2.0 / math.pi)
    th = jnp.tanh(c * (x + 0.044715 * (x * x * x)))
    return 0.5 * (1.0 + th) + 0.5 * x * (1.0 - th * th) * (c * (1.0 + 3.0 * 0.044715 * (x * x)))


def _row_ids(bm):
    return lax.broadcasted_iota(jnp.int32, (bm, 1), 0)


def _shift_down(x, halo_ref, i, k):
    live = (i > 0).astype(F32)
    out = pltpu.roll(x, k, 0)
    row = _row_ids(8)
    first = out[:8]
    for r in range(k):
        e = (row == r).astype(F32)
        first = first * (1.0 - e) + e * (halo_ref[8 - k + r:8 - k + r + 1, :] * live)
    return jnp.concatenate([first, out[8:]], axis=0)


def _shift_up(x, halo_ref, i, nblk, k):
    bm = x.shape[0]
    live = (i < nblk - 1).astype(F32)
    out = pltpu.roll(x, bm - k, 0)
    row = _row_ids(8)
    last = out[bm - 8:]
    for r in range(k):
        e = (row == 8 - k + r).astype(F32)
        last = last * (1.0 - e) + e * (halo_ref[r:r + 1, :] * live)
    return jnp.concatenate([out[:bm - 8], last], axis=0)


def _live_pairs(nq, nk, bq, bk, causal, key_major):
    pairs = [(i, j) for i in range(nq) for j in range(nk) if not causal or j * bk <= i * bq + bq - 1]
    if key_major:
        pairs.sort(key=lambda ij: (ij[1], ij[0]))
    return (jnp.asarray([p[0] for p in pairs], jnp.int32), jnp.asarray([p[1] for p in pairs], jnp.int32))


def _attn_fwd(name, qa, ka, vta, *, qoff, koff, voff, heads, causal, scale, bq, bk, side=None):
    lq, lk = qa.shape[0], ka.shape[0]
    bq, bk = min(bq, lq), min(bk, lk)
    nq, nk = lq // bq, lk // bk
    c2 = scale * LOG2E
    tab_i, tab_j = _live_pairs(nq, nk, bq, bk, causal, key_major=False)

    def body(ti, tj, q_ref, k_ref, vt_ref, o_ref, lse_ref, m_s, l_s, acc_s):
        t = pl.program_id(1)
        i, j = ti[t], tj[t]
        j_last = jnp.minimum(nk - 1, (i * bq + bq - 1) // bk) if causal else nk - 1

        @pl.when(j == 0)
        def _():
            m_s[...] = jnp.full_like(m_s, -1e30)
            l_s[...] = jnp.zeros_like(l_s)
            acc_s[...] = jnp.zeros_like(acc_s)

        def step(masked):
            st = _dot(k_ref[...], q_ref[...], _NT) * c2
            if masked:
                key = j * bk + lax.broadcasted_iota(jnp.int32, (bk, bq), 0)
                qry = i * bq + lax.broadcasted_iota(jnp.int32, (bk, bq), 1)
                st = jnp.where(key <= qry, st, -1e30)
            m_prev = m_s[...]
            m_new = jnp.maximum(m_prev, jnp.max(st, axis=0, keepdims=True))
            alpha = jnp.exp2(m_prev - m_new)
            pt = jnp.exp2(st - m_new)
            l_s[...] = alpha * l_s[...] + jnp.sum(pt, axis=0, keepdims=True)
            acc_s[...] = alpha * acc_s[...] + _dot(vt_ref[...], pt, _NN)
            m_s[...] = m_new

        if causal:
            full = j * bk + bk - 1 <= i * bq
            pl.when(full)(lambda: step(False))
            pl.when(jnp.logical_not(full))(lambda: step(True))
        else:
            step(False)

        @pl.when(j == j_last)
        def _():
            l = l_s[...]
            o_ref[...] = (acc_s[...] / l).T.astype(o_ref.dtype)
            lse_ref[0] = m_s[...] + jnp.log2(l)

    in_specs = [pl.BlockSpec((bq, LANES), lambda h, t, ti, tj: (ti[t], qoff + h)),
                pl.BlockSpec((bk, LANES), lambda h, t, ti, tj: (tj[t], koff + h)),
                pl.BlockSpec((LANES, bk), lambda h, t, ti, tj: (voff + h, tj[t]))]
    out_specs = [pl.BlockSpec((bq, LANES), lambda h, t, ti, tj: (ti[t], h)),
                 pl.BlockSpec((1, 1, bq), lambda h, t, ti, tj: (h, 0, ti[t]))]
    scratch = [pltpu.VMEM((1, bq), F32), pltpu.VMEM((1, bq), F32), pltpu.VMEM((LANES, bq), F32)]
    out_shape = [jax.ShapeDtypeStruct((lq, heads * LANES), BF16), jax.ShapeDtypeStruct((heads, 1, lq), F32)]
    (o, lse), rode = _pair_grid_call(body, name, heads, (tab_i, tab_j), [qa, ka, vta], in_specs, out_specs, out_shape,
                                     scratch, side)
    return o, lse, rode


_Side = collections.namedtuple("_Side", "srcs out_shapes nsem phases")


def _pair_grid_call(body, name, heads, tabs, ins, in_specs, out_specs, out_shape, scratch, side):
    npairs = int(tabs[0].shape[0])
    n_in, n_out = len(ins), len(out_shape)
    n = len(side.srcs) if side else 0

    def wrapped(*refs):
        pre = len(tabs)
        if not side:
            return body(*refs)
        s_refs = refs[pre + n_in:pre + n_in + n]
        o_refs = refs[pre + n_in + n + n_out:pre + n_in + 2 * n + n_out]
        send, recv, loc = refs[-3:]
        args = (s_refs, o_refs, (send, recv), loc)
        h, t = pl.program_id(0), pl.program_id(1)
        pl.when(jnp.logical_and(h == 0, t == 0))(lambda: side.phases[0](*args))
        if len(side.phases) == 3:
            pl.when(jnp.logical_and(h == heads // 2, t == 0))(lambda: side.phases[1](*args))
        body(*refs[:pre + n_in], *refs[pre + n_in + n:pre + n_in + n + n_out], *refs[pre + n_in + 2 * n + n_out:-3])
        pl.when(jnp.logical_and(h == heads - 1, t == npairs - 1))(lambda: side.phases[-1](*args))

    hbm = pl.BlockSpec(memory_space=pltpu.HBM)
    sems = [pltpu.SemaphoreType.DMA((side.nsem * n,)), pltpu.SemaphoreType.DMA((side.nsem * n,)),
            pltpu.SemaphoreType.DMA((n,))] if side else []
    grid_spec = pltpu.PrefetchScalarGridSpec(
        num_scalar_prefetch=len(tabs), grid=(heads, npairs), in_specs=in_specs + [hbm] * n,
        out_specs=out_specs + [hbm] * n, scratch_shapes=scratch + sems)
    res = pl.pallas_call(
        wrapped, name=name, grid_spec=grid_spec, out_shape=out_shape + (list(side.out_shapes) if side else []),
        compiler_params=_params(has_side_effects=True) if side else _params(),
    )(*tabs, *ins, *(side.srcs if side else []))
    return res[:n_out], res[n_out:]


def _attn_delta(name, oa, doa, *, heads, bq):
    lq = oa.shape[0]
    bq = min(bq, lq)

    def body(o_ref, do_ref, d_ref):
        prod = o_ref[...].astype(F32) * do_ref[...].astype(F32)
        hi = prod.astype(BF16)
        lo = (prod - hi.astype(F32)).astype(BF16)
        pick = (lax.broadcasted_iota(jnp.int32, (8, LANES), 0) == 0).astype(BF16)
        sums = _dot(pick, hi, _NT) + _dot(pick, lo, _NT)
        d_ref[0] = jnp.sum(sums, axis=0, keepdims=True)

    blk = pl.BlockSpec((bq, LANES), lambda h, i: (i, h))
    return pl.pallas_call(
        body, name=name, grid=(heads, lq // bq), in_specs=[blk, blk],
        out_specs=pl.BlockSpec((1, 1, bq), lambda h, i: (h, 0, i)),
        out_shape=jax.ShapeDtypeStruct((heads, 1, lq), F32), compiler_params=_params(),
    )(oa, doa)


def _attn_bwd(name, qa, ka, kta, va, doa, lsea, deltaa, *, qoff, koff, voff, heads, causal, scale, bq, bk, side=None):
    lq, lk = qa.shape[0], ka.shape[0]
    bq, bk = min(bq, lq), min(bk, lk)
    nq, nk = lq // bq, lk // bk
    c2 = scale * LOG2E
    tab_i, tab_j = _live_pairs(nq, nk, bq, bk, causal, key_major=True)

    def body(ti, tj, q_ref, k_ref, kt_ref, v_ref, do_ref, lse_ref, delta_ref, dqt_ref, dk_ref, dv_ref):
        t = pl.program_id(1)
        i, j = ti[t], tj[t]
        i_first = (j * bk) // bq if causal else 0

        @pl.when(t == 0)
        def _():
            dqt_ref[...] = jnp.zeros_like(dqt_ref)

        @pl.when(i == i_first)
        def _():
            dk_ref[...] = jnp.zeros_like(dk_ref)
            dv_ref[...] = jnp.zeros_like(dv_ref)

        def step(masked):
            q, k, v, do = q_ref[...], k_ref[...], v_ref[...], do_ref[...]
            st = _dot(k, q, _NT) * c2
            if masked:
                key = j * bk + lax.broadcasted_iota(jnp.int32, (bk, bq), 0)
                qry = i * bq + lax.broadcasted_iota(jnp.int32, (bk, bq), 1)
                st = jnp.where(key <= qry, st, -1e30)
            pt = jnp.exp2(st - lse_ref[0])
            dv_ref[...] += _dot(pt, do, _NN)
            dpt = _dot(v, do, _NT)
            dst = (pt * (dpt - delta_ref[0]) * scale).astype(BF16)
            dk_ref[...] += _dot(dst, q, _NN)
            dqt_ref[0, i] += _dot(kt_ref[...], dst, _NN)

        if causal:
            full = j * bk + bk - 1 <= i * bq
            pl.when(full)(lambda: step(False))
            pl.when(jnp.logical_not(full))(lambda: step(True))
        else:
            step(False)

    q_spec = lambda off: pl.BlockSpec((bq, LANES), lambda h, t, ti, tj: (ti[t], off + h))
    kv_spec = lambda off: pl.BlockSpec((bk, LANES), lambda h, t, ti, tj: (tj[t], off + h))
    row_spec = pl.BlockSpec((1, 1, bq), lambda h, t, ti, tj: (h, 0, ti[t]))
    in_specs = [q_spec(qoff), kv_spec(koff), pl.BlockSpec((LANES, bk), lambda h, t, ti, tj: (koff + h, tj[t])),
                kv_spec(voff), q_spec(0), row_spec, row_spec]
    out_specs = [pl.BlockSpec((1, nq, LANES, bq), lambda h, t, ti, tj: (h, 0, 0, 0)),
                 pl.BlockSpec((bk, LANES), lambda h, t, ti, tj: (tj[t], h)),
                 pl.BlockSpec((bk, LANES), lambda h, t, ti, tj: (tj[t], h))]
    out_shape = [jax.ShapeDtypeStruct((heads, nq, LANES, bq), F32), jax.ShapeDtypeStruct((lk, heads * LANES), F32),
                 jax.ShapeDtypeStruct((lk, heads * LANES), F32)]
    (dqt, dk, dv), rode = _pair_grid_call(body, name, heads, (tab_i, tab_j), [qa, ka, kta, va, doa, lsea, deltaa],
                                          in_specs, out_specs, out_shape, [], side)
    return dqt.transpose(1, 3, 0, 2).reshape(lq, heads * LANES), dk, dv, rode


S5_STRIP = SSM_N // 4
S5_CH = SSM_WIDTH // 4
S5_CHUNK = 512


def _scan_chunk(xr_ref, xi_ref, ar, ai, cre_s, cim_s, reverse, unroll=4):
    t = xr_ref.shape[0]
    ng = t // 8
    edge = 0 if reverse else 7
    row8 = lax.broadcasted_iota(jnp.int32, (8, 1), 0)

    def grp(g):
        return pl.ds(pl.multiple_of(g * 8, 8), 8)

    def group_scan(xr, xi):
        pr, pi = ar, ai
        for d in (1, 2, 4):
            own = ((row8 < 8 - d) if reverse else (row8 >= d)).astype(F32)
            shift = (8 - d) if reverse else d
            sr, si = pltpu.roll(xr, shift, 0) * own, pltpu.roll(xi, shift, 0) * own
            xr, xi = xr + pr * sr - pi * si, xi + pr * si + pi * sr
            pr, pi = pr * pr - pi * pi, 2.0 * pr * pi
        return xr, xi

    def local(g, _):
        xr, xi = group_scan(xr_ref[grp(g), :], xi_ref[grp(g), :])
        xr_ref[grp(g), :] = xr
        xi_ref[grp(g), :] = xi
        return 0

    lax.fori_loop(0, ng, local, 0, unroll=unroll)
    e = (row8 == 7 - edge).astype(F32)
    pw_r, pw_i = group_scan(e * ar, e * ai)
    e_out = (row8 == edge).astype(F32)
    a8_r = jnp.sum(pw_r * e_out, axis=0, keepdims=True)
    a8_i = jnp.sum(pw_i * e_out, axis=0, keepdims=True)
    cr, ci = cre_s[...], cim_s[...]
    for i in range(ng):
        g = ng - 1 - i if reverse else i
        rows = slice(g * 8, g * 8 + 8)
        lr, li = xr_ref[g * 8 + edge:g * 8 + edge + 1, :], xi_ref[g * 8 + edge:g * 8 + edge + 1, :]
        xr_ref[rows, :] = xr_ref[rows, :] + pw_r * cr - pw_i * ci
        xi_ref[rows, :] = xi_ref[rows, :] + pw_r * ci + pw_i * cr
        cr, ci = lr + a8_r * cr - a8_i * ci, li + a8_r * ci + a8_i * cr
    cre_s[...] = cr
    cim_s[...] = ci


def _s5_fwd(name, u, bb_re, bb_im, cc_re, cc_imn, a_re, a_im):
    l = u.shape[0]
    t = min(S5_CHUNK, l)

    def body(u_ref, bbr_ref, bbi_ref, ccr_ref, cci_ref, ar_ref, ai_ref, sre_ref, sim_ref, y_ref, cre_s, cim_s):
        @pl.when(pl.program_id(1) == 0)
        def _():
            cre_s[...] = jnp.zeros_like(cre_s)
            cim_s[...] = jnp.zeros_like(cim_s)

        uv = u_ref[...]
        sre_ref[...] = _dot(uv, bbr_ref[...], _NN)
        sim_ref[...] = _dot(uv, bbi_ref[...], _NN)
        _scan_chunk(sre_ref, sim_ref, ar_ref[...], ai_ref[...], cre_s, cim_s, reverse=False)
        y_ref[...] = _dot(sre_ref[...], ccr_ref[...], _NN) + _dot(sim_ref[...], cci_ref[...], _NN)

    rows_ch = pl.BlockSpec((t, S5_CH), lambda w, c: (c, w))
    rows_st = pl.BlockSpec((t, S5_STRIP), lambda w, c: (c, w))
    b_blk = pl.BlockSpec((S5_CH, S5_STRIP), lambda w, c: (w, w))
    c_blk = pl.BlockSpec((S5_STRIP, S5_CH), lambda w, c: (w, w))
    a_blk = pl.BlockSpec((1, S5_STRIP), lambda w, c: (0, w))
    return pl.pallas_call(
        body, name=name, grid=(SSM_N // S5_STRIP, l // t),
        in_specs=[rows_ch, b_blk, b_blk, c_blk, c_blk, a_blk, a_blk], out_specs=[rows_st, rows_st, rows_ch],
        out_shape=[jax.ShapeDtypeStruct((l, SSM_N), F32), jax.ShapeDtypeStruct((l, SSM_N), F32),
                   jax.ShapeDtypeStruct((l, SSM_WIDTH), F32)],
        scratch_shapes=[pltpu.VMEM((1, S5_STRIP), F32), pltpu.VMEM((1, S5_STRIP), F32)],
        compiler_params=_params(),
    )(u, bb_re, bb_im, cc_re, cc_imn, a_re, a_im)


def _s5_bwd(name, dy, du_skip, u, s_re, s_im, bb_re, bb_im, cc_re, cc_imn, a_re, a_im_neg):
    l = u.shape[0]
    t = min(S5_CHUNK, l)
    nc = l // t

    def body(dy_ref, skip_ref, u_ref, sre_ref, sim_ref, hre_ref, him_ref, bbr_ref, bbi_ref, ccr_ref, cci_ref, ar_ref,
             ai_ref, du_ref, dar_ref, dai_ref, dbr_ref, dbi_ref, dcr_ref, dci_ref, lr_s, li_s, cre_s, cim_s):
        c = pl.program_id(1)

        @pl.when(c == 0)
        def _():
            for r in (cre_s, cim_s, dar_ref, dai_ref, dbr_ref, dbi_ref, dcr_ref, dci_ref):
                r[...] = jnp.zeros_like(r)

        dyv, uv = dy_ref[...], u_ref[...]
        lr_s[...] = _dot(dyv, ccr_ref[...], _NT)
        li_s[...] = _dot(dyv, cci_ref[...], _NT)
        _scan_chunk(lr_s, li_s, ar_ref[...], ai_ref[...], cre_s, cim_s, reverse=True)
        lam_r, lam_i = lr_s[...], li_s[...]
        s_r, s_i = sre_ref[...], sim_ref[...]
        tc = nc - 1 - c
        sp_r, sp_i = _shift_down(s_r, hre_ref, tc, 1), _shift_down(s_i, him_ref, tc, 1)
        dar_ref[...] += jnp.sum(lam_r * sp_r + lam_i * sp_i, axis=0, keepdims=True)
        dai_ref[...] += jnp.sum(lam_i * sp_r - lam_r * sp_i, axis=0, keepdims=True)
        dbr_ref[0] += _dot(uv, lam_r, _TN)
        dbi_ref[0] += _dot(uv, lam_i, _TN)
        dcr_ref[0] += _dot(s_r, dyv, _TN)
        dci_ref[0] += _dot(s_i, dyv, _TN)
        du = _dot(lam_r, bbr_ref[...], _NT) + _dot(lam_i, bbi_ref[...], _NT) + skip_ref[...]
        du_ref[...] = du.astype(du_ref.dtype)

    rows_ch = pl.BlockSpec((t, S5_CH), lambda w, c: (nc - 1 - c, w))
    rows_st = pl.BlockSpec((t, S5_STRIP), lambda w, c: (nc - 1 - c, w))
    halo = pl.BlockSpec((8, S5_STRIP), lambda w, c: (jnp.maximum((nc - 1 - c) * (t // 8) - 1, 0), w))
    b_blk = pl.BlockSpec((S5_CH, S5_STRIP), lambda w, c: (w, w))
    c_blk = pl.BlockSpec((S5_STRIP, S5_CH), lambda w, c: (w, w))
    a_blk = pl.BlockSpec((1, S5_STRIP), lambda w, c: (0, w))
    nw = SSM_N // S5_STRIP
    return pl.pallas_call(
        body, name=name, grid=(nw, nc),
        in_specs=[rows_ch, rows_ch, rows_ch, rows_st, rows_st, halo, halo, b_blk, b_blk, c_blk, c_blk, a_blk, a_blk],
        out_specs=[rows_ch, a_blk, a_blk, pl.BlockSpec((1, S5_CH, S5_STRIP), lambda w, c: (w, 0, 0)),
                   pl.BlockSpec((1, S5_CH, S5_STRIP), lambda w, c: (w, 0, 0)),
                   pl.BlockSpec((1, S5_STRIP, S5_CH), lambda w, c: (w, 0, 0)),
                   pl.BlockSpec((1, S5_STRIP, S5_CH), lambda w, c: (w, 0, 0))],
        out_shape=[jax.ShapeDtypeStruct((l, SSM_WIDTH), BF16), jax.ShapeDtypeStruct((1, SSM_N), F32),
                   jax.ShapeDtypeStruct((1, SSM_N), F32), jax.ShapeDtypeStruct((nw, S5_CH, S5_STRIP), F32),
                   jax.ShapeDtypeStruct((nw, S5_CH, S5_STRIP), F32), jax.ShapeDtypeStruct((nw, S5_STRIP, S5_CH), F32),
                   jax.ShapeDtypeStruct((nw, S5_STRIP, S5_CH), F32)],
        scratch_shapes=[pltpu.VMEM((t, S5_STRIP), F32), pltpu.VMEM((t, S5_STRIP), F32),
                        pltpu.VMEM((1, S5_STRIP), F32), pltpu.VMEM((1, S5_STRIP), F32)],
        compiler_params=_params(),
    )(dy, du_skip, u, s_re, s_im, s_re, s_im, bb_re, bb_im, cc_re, cc_imn, a_re, a_im_neg)


def _disc_math(lr, li, ldt, br, bi):
    dt = jnp.exp(ldt)
    mag = jnp.exp(lr * dt)
    a_re, a_im = mag * jnp.cos(li * dt), mag * jnp.sin(li * dt)
    den = lr * lr + li * li
    e_re, e_im = a_re - 1.0, a_im
    f_re = (e_re * lr + e_im * li) / den
    f_im = (e_im * lr - e_re * li) / den
    return a_re, a_im, f_re * br - f_im * bi, f_re * bi + f_im * br


def _disc_fwd(lr, li, ldt, br, bi):
    def body(lr_ref, li_ref, ldt_ref, br_ref, bi_ref, are_ref, aim_ref, bbr_ref, bbi_ref):
        a_re, a_im, bb_re, bb_im = _disc_math(lr_ref[...], li_ref[...], ldt_ref[...], br_ref[...], bi_ref[...])
        are_ref[...] = a_re
        aim_ref[...] = a_im
        bbr_ref[...] = bb_re
        bbi_ref[...] = bb_im

    col = jax.ShapeDtypeStruct(lr.shape, F32)
    mat = jax.ShapeDtypeStruct(br.shape, F32)
    return pl.pallas_call(body, name="s5_disc_fwd", out_shape=[col, col, mat, mat], compiler_params=_params())(
        lr, li, ldt, br, bi)


def _disc_bwd(lr, li, ldt, br, bi, da_re, da_im, dbb_re, dbb_im):
    def body(lr_ref, li_ref, ldt_ref, br_ref, bi_ref, g0, g1, g2, g3, o0, o1, o2, o3, o4):
        _, vjp = jax.vjp(_disc_math, lr_ref[...], li_ref[...], ldt_ref[...], br_ref[...], bi_ref[...])
        grads = vjp((g0[...], g1[...], g2[...], g3[...]))
        for o, g in zip((o0, o1, o2, o3, o4), grads):
            o[...] = g

    col = jax.ShapeDtypeStruct(lr.shape, F32)
    mat = jax.ShapeDtypeStruct(br.shape, F32)
    return pl.pallas_call(body, name="s5_disc_bwd", out_shape=[col, col, col, mat, mat], compiler_params=_params())(
        lr, li, ldt, br, bi, da_re, da_im, dbb_re, dbb_im)


N_CHIP = 4


def _place():
    x, y, c = lax.axis_index("x"), lax.axis_index("y"), lax.axis_index("c")
    return (x, y, c), (x, y, 1 - c), [(1 - x, y), (x, 1 - y), (1 - x, 1 - y)]


def _lin(px, py, pc):
    return 4 * px + 2 * py + pc


def _remote(src, dst, sems, k, dev):
    return pltpu.make_async_remote_copy(src_ref=src, dst_ref=dst, send_sem=sems[0].at[k], recv_sem=sems[1].at[k],
                                        device_id=dev, device_id_type=pl.DeviceIdType.MESH)


def _hbm_call(body, name, srcs, out_shapes, nsem):
    n = len(srcs)
    hbm = pl.BlockSpec(memory_space=pltpu.HBM)

    def wrapped(*refs):
        body(refs[:n], refs[n:2 * n], (refs[2 * n], refs[2 * n + 1]), refs[2 * n + 2])

    return pl.pallas_call(
        wrapped, name=name, in_specs=[hbm] * n, out_specs=[hbm] * n, out_shape=out_shapes,
        scratch_shapes=[pltpu.SemaphoreType.DMA((nsem * n,)), pltpu.SemaphoreType.DMA((nsem * n,)),
                        pltpu.SemaphoreType.DMA((n,))],
        compiler_params=pltpu.CompilerParams(has_side_effects=True),
    )(*srcs)


GATHER_SEMS = 7
CHIP_SEMS = 3


def _gather_copies(s_refs, o_refs, sems, loc_sems):
    me, sib, chips = _place()
    c = me[2]
    out = []
    for g, (s_ref, o_ref) in enumerate(zip(s_refs, o_refs)):
        slot = lambda dev, o_ref=o_ref: o_ref.at[_lin(*dev)]
        k0 = GATHER_SEMS * g
        mine = pltpu.make_async_copy(s_ref, slot(me), loc_sems.at[g])
        first = [_remote(s_ref, slot(me), sems, k0, sib)]
        first += [_remote(s_ref, slot(me), sems, k0 + 1 + j, (*chip, c)) for j, chip in enumerate(chips)]
        passed = [_remote(slot((*chip, c)), slot((*chip, c)), sems, k0 + 4 + j, sib) for j, chip in enumerate(chips)]
        arrive = [_remote(s_ref, slot(sib), sems, k0, me)]
        arrive += [_remote(s_ref, slot((*chip, c)), sems, k0 + 1 + j, me) for j, chip in enumerate(chips)]
        arrive += [_remote(s_ref, slot((*chip, 1 - c)), sems, k0 + 4 + j, me) for j, chip in enumerate(chips)]
        out.append((mine, first, passed, arrive))
    return out


def _gather_start(*refs):
    for mine, first, _, _ in _gather_copies(*refs):
        mine.start()
        for cp in first:
            cp.start()


def _gather_relay(*refs):
    for _, _, passed, arrive in _gather_copies(*refs):
        for j, cp in enumerate(passed):
            arrive[1 + j].wait_recv()
            cp.start()


def _gather_finish(*refs):
    for mine, first, passed, arrive in _gather_copies(*refs):
        arrive[0].wait_recv()
        for cp in arrive[4:]:
            cp.wait_recv()
        for cp in first + passed:
            cp.wait_send()
        mine.wait()


def _gather_shapes(srcs):
    return [jax.ShapeDtypeStruct((N_DEV,) + s.shape, s.dtype) for s in srcs]


def _gather_all(name, srcs):
    def body(*refs):
        _gather_start(*refs)
        _gather_relay(*refs)
        _gather_finish(*refs)

    return _hbm_call(body, name, srcs, _gather_shapes(srcs), GATHER_SEMS)


def _pair_exchange(name, sends):
    def body(s_refs, o_refs, sems, loc_sems):
        me, sib, _ = _place()
        c = me[2]
        copies = [_remote(s_ref.at[2 * q + (1 - c)], o_ref.at[q], sems, N_CHIP * g + q, sib)
                  for g, (s_ref, o_ref) in enumerate(zip(s_refs, o_refs)) for q in range(N_CHIP)]
        for cp in copies:
            cp.start()
        for cp in copies:
            cp.wait()

    return _hbm_call(body, name, sends, [jax.ShapeDtypeStruct((N_CHIP,) + s.shape[1:], s.dtype) for s in sends],
                     N_CHIP)


def _row_block(r, cap):
    best = 8
    for m in range(8, min(r, cap) + 1, 8):
        if r % m == 0:
            best = m
    return best


def _pair_sum(name, send, got):
    _, r, cols = send.shape
    bm = _row_block(r, max(8, 256 * 1024 // cols))
    core = lax.axis_index("c").astype(jnp.int32).reshape(1)

    def body(core_ref, a_ref, b_ref, o_ref):
        o_ref[...] = a_ref[...] + b_ref[...]

    grid_spec = pltpu.PrefetchScalarGridSpec(
        num_scalar_prefetch=1, grid=(N_CHIP, r // bm),
        in_specs=[pl.BlockSpec((1, bm, cols), lambda q, i, cr: (2 * q + cr[0], i, 0)),
                  pl.BlockSpec((1, bm, cols), lambda q, i, cr: (q, i, 0))],
        out_specs=pl.BlockSpec((1, bm, cols), lambda q, i, cr: (q, i, 0)))
    return pl.pallas_call(body, name=name, grid_spec=grid_spec,
                          out_shape=jax.ShapeDtypeStruct((N_CHIP, r, cols), F32), compiler_params=_params())(
        core, send, got)


def _chip_exchange(name, parts):
    def body(*refs):
        _chip_start(*refs)
        _chip_finish(*refs)

    return _hbm_call(body, name, parts, _chip_shapes(parts), CHIP_SEMS)


def _chip_copies(p_refs, o_refs, sems, loc_sems):
    me, _, chips = _place()
    c = me[2]
    chip_id = lambda chip: 2 * chip[0] + chip[1]
    my_chip = chip_id(me)
    out = []
    for g, (p_ref, o_ref) in enumerate(zip(p_refs, o_refs)):
        mine = pltpu.make_async_copy(p_ref.at[my_chip], o_ref.at[my_chip], loc_sems.at[g])
        send = [_remote(p_ref.at[chip_id(chip)], o_ref.at[my_chip], sems, CHIP_SEMS * g + j, (*chip, c))
                for j, chip in enumerate(chips)]
        arrive = [_remote(p_ref.at[my_chip], o_ref.at[chip_id(chip)], sems, CHIP_SEMS * g + j, me)
                  for j, chip in enumerate(chips)]
        out.append((mine, send, arrive))
    return out


def _chip_start(*refs):
    for mine, send, _ in _chip_copies(*refs):
        mine.start()
        for cp in send:
            cp.start()


def _chip_finish(*refs):
    for mine, send, arrive in _chip_copies(*refs):
        for cp in arrive:
            cp.wait_recv()
        for cp in send:
            cp.wait_send()
        mine.wait()


def _chip_shapes(parts):
    return [jax.ShapeDtypeStruct(p.shape, p.dtype) for p in parts]


def _adamw(name, rcv, w, m, v):
    r, c = w.shape
    nslot = rcv.shape[0]
    bm = _row_block(r, max(8, 256 * 1024 // c))

    def body(rcv_ref, w_ref, m_ref, v_ref, g_ref, d_ref, m2_ref, v2_ref):
        g = rcv_ref[0]
        for s in range(1, nslot):
            g = g + rcv_ref[s]
        m2 = ADAM_B1 * m_ref[...] + (1.0 - ADAM_B1) * g
        v2 = ADAM_B2 * v_ref[...] + (1.0 - ADAM_B2) * (g * g)
        m_hat = m2 / (1.0 - ADAM_B1 ** ADAM_STEP)
        v_hat = v2 / (1.0 - ADAM_B2 ** ADAM_STEP)
        g_ref[...] = g
        d_ref[...] = -ADAM_LR * (m_hat / (jnp.sqrt(v_hat) + ADAM_EPS) + ADAM_WD * w_ref[...])
        m2_ref[...] = m2
        v2_ref[...] = v2

    blk = pl.BlockSpec((bm, c), lambda i: (i, 0))
    out = jax.ShapeDtypeStruct((r, c), F32)
    return pl.pallas_call(
        body, name=name, grid=(r // bm,),
        in_specs=[pl.BlockSpec((nslot, bm, c), lambda i: (0, i, 0)), blk, blk, blk], out_specs=[blk] * 4,
        out_shape=[out] * 4, compiler_params=_params(),
    )(rcv, w, m, v)


IN_SHARD = IN_WIDTH // N_DEV
UP_SHARD = 2 * D_FF // N_DEV
GROUPS = (
    ("g128a", LANES, (("w_q_b", Q_LORA, D_QK), ("w_kv_b", KV_LORA, D_NOPE + D_V))),
    ("g128b", LANES, (("w_o_mla", MLA_HEADS * D_V, LANES), ("w_o_ssm", SSM_WIDTH, LANES),
                      ("w_o_cross", X_WIDTH, LANES))),
    ("g512", SSM_WIDTH, (("w_glu", SSM_WIDTH // N_DEV, SSM_WIDTH),)),
    ("g1024", D_MODEL, (("w_mem_kv", D_MODEL // N_DEV, D_MODEL), ("w_out", D_MODEL // N_DEV, D_MODEL),
                        ("w_down", D_FF // N_DEV, D_MODEL))),
    ("g640", 640, (("w_in", D_MODEL, IN_SHARD),)),
    ("g768", 768, (("w_up", D_MODEL, UP_SHARD),)),
    ("gconv", 768, (("conv_w", 3, UP_SHARD),)),
)
EARLY = ("g640", "g128a")
EARLY_GROUPS = tuple(g for g in GROUPS if g[0] in EARLY)
LATE_GROUPS = tuple(g for g in GROUPS if g[0] not in EARLY)
REP_W = 1024


def _rows8(a):
    return -(-a // 8) * 8


def _group_rows(params):
    off, r = {}, 0
    for n, a, _ in params:
        off[n] = r
        r += _rows8(a)
    return off


def _group_local(width, params, vals, li, dtype):
    return jnp.concatenate([jnp.pad(vals[n][li].astype(dtype), ((0, _rows8(a) - a), (0, width - b)))
                            for n, a, b in params], axis=0)


def _rep_rows(flat):
    n = flat.shape[-1]
    per = REP_W * 64
    tot = -(-n // per) * per
    flat = jnp.pad(flat, [(0, 0)] * (flat.ndim - 1) + [(0, tot - n)])
    return flat.reshape(flat.shape[:-1] + (tot // REP_W, REP_W))


IN_SEGS = (
    (0, Q_LORA + KV_LORA, "wqkv", 0),
    (Q_LORA + KV_LORA, Q_LORA + KV_LORA + D_ROPE, "wqkv", Q_LORA + KV_LORA + KR_LO),
    (Q_LORA + KV_LORA + D_ROPE, Q_LORA + KV_LORA + D_ROPE + SSM_WIDTH, "w_u", 0),
    (Q_LORA + KV_LORA + D_ROPE + SSM_WIDTH, Q_LORA + KV_LORA + D_ROPE + SSM_WIDTH + X_WIDTH, "w_xq", 0),
    (Q_LORA + KV_LORA + D_ROPE + SSM_WIDTH + X_WIDTH, IN_WIDTH, "w_g", 0),
)
IN_PARTS = (("wqkv", QKV_W), ("w_u", SSM_WIDTH), ("w_xq", X_WIDTH), ("w_g", 3 * D_MODEL))


def _in_pieces():
    out = []
    for d in range(N_DEV):
        for lo, hi, part, dst in IN_SEGS:
            s, e = max(lo, d * IN_SHARD), min(hi, (d + 1) * IN_SHARD)
            if s < e:
                out.append((d, s - d * IN_SHARD, e - s, [p for p, _ in IN_PARTS].index(part), dst + s - lo))
    return out


def _unpack_w_in(gathered, row0, bm=256):
    def body(x_ref, *o_refs):
        o_refs[0][...] = jnp.zeros_like(o_refs[0])
        for d, src, n, part, dst in _in_pieces():
            o_refs[part][:, dst:dst + n] = x_ref[d, :, src:src + n]

    return pl.pallas_call(
        body, name="unpack_w_in", grid=(D_MODEL // bm,),
        in_specs=[pl.BlockSpec((N_DEV, bm, gathered.shape[2]), lambda i: (0, row0 // bm + i, 0))],
        out_specs=[pl.BlockSpec((bm, w), lambda i: (i, 0)) for _, w in IN_PARTS],
        out_shape=[jax.ShapeDtypeStruct((D_MODEL, w), gathered.dtype) for _, w in IN_PARTS], compiler_params=_params(),
    )(gathered)


def _pack_w_in(parts, bm=256):
    def body(*refs):
        o_ref = refs[-1]
        o_ref[...] = jnp.zeros_like(o_ref)
        for d, src, n, part, dst in _in_pieces():
            o_ref[d, :, src:src + n] = refs[part][:, dst:dst + n]

    return pl.pallas_call(
        body, name="pack_w_in", grid=(D_MODEL // bm,),
        in_specs=[pl.BlockSpec((bm, w), lambda i: (i, 0)) for _, w in IN_PARTS],
        out_specs=pl.BlockSpec((N_DEV, bm, 640), lambda i: (0, i, 0)),
        out_shape=jax.ShapeDtypeStruct((N_DEV, D_MODEL, 640), F32), compiler_params=_params(),
    )(*parts)


def _unpack_w_up(gathered, row0, bm=256):
    def body(x_ref, o_ref):
        for d in range(N_DEV):
            o_ref[:, d * UP_SHARD:(d + 1) * UP_SHARD] = x_ref[d, :, :UP_SHARD]

    return pl.pallas_call(
        body, name="unpack_w_up", grid=(D_MODEL // bm,),
        in_specs=[pl.BlockSpec((N_DEV, bm, gathered.shape[2]), lambda i: (0, row0 // bm + i, 0))],
        out_specs=pl.BlockSpec((bm, 2 * D_FF), lambda i: (i, 0)),
        out_shape=jax.ShapeDtypeStruct((D_MODEL, 2 * D_FF), gathered.dtype), compiler_params=_params(),
    )(gathered)


def _pack_w_up(dw_g, dw_v, bm=256):
    half = N_DEV // 2

    def body(g_ref, v_ref, o_ref):
        o_ref[...] = jnp.zeros_like(o_ref)
        for d in range(N_DEV):
            src = g_ref if d < half else v_ref
            c0 = (d % half) * UP_SHARD
            o_ref[d, :, :UP_SHARD] = src[:, c0:c0 + UP_SHARD]

    blk = pl.BlockSpec((bm, D_FF), lambda i: (i, 0))
    return pl.pallas_call(
        body, name="pack_w_up", grid=(D_MODEL // bm,), in_specs=[blk, blk],
        out_specs=pl.BlockSpec((N_DEV, bm, 768), lambda i: (0, i, 0)),
        out_shape=jax.ShapeDtypeStruct((N_DEV, D_MODEL, 768), F32), compiler_params=_params(),
    )(dw_g, dw_v)


def _cols_to_rows(full):
    a, nb = full.shape
    return full.reshape(a, N_DEV, nb // N_DEV).transpose(1, 0, 2)


def _rows_to_cols(blocks):
    n, a, b = blocks.shape
    return blocks.transpose(1, 0, 2).reshape(a, n * b)


def _block_diag_in(bb):
    b3 = bb.reshape(SSM_GROUPS, SSM_STATE, SSM_GROUP_CH).transpose(0, 2, 1)
    eye = jnp.eye(SSM_GROUPS, dtype=bb.dtype)
    return (b3[:, :, None, :] * eye[:, None, :, None]).reshape(SSM_WIDTH, SSM_N)


def _block_diag_out(cc):
    c3 = cc.transpose(0, 2, 1)
    eye = jnp.eye(SSM_GROUPS, dtype=cc.dtype)
    return (c3[:, :, None, :] * eye[:, None, :, None]).reshape(SSM_N, SSM_WIDTH)


def _diag_blocks(mats, rows_per, cols_per):
    nw = mats.shape[0]
    per = SSM_GROUPS // nw
    m5 = mats.reshape(nw, per, rows_per, per, cols_per)
    eye = jnp.eye(per, dtype=mats.dtype)
    return jnp.sum(m5 * eye[None, :, None, :, None], axis=3).reshape(SSM_GROUPS, rows_per, cols_per)


def _blk(gath, offs, grp, n, a):
    r0 = offs[grp][n]
    return gath[grp][:, r0:r0 + a, :]


def _early_weights(gath, offs):
    o = dict(zip([p for p, _ in IN_PARTS], _unpack_w_in(gath["g640"], offs["g640"]["w_in"])))
    o["wqb"] = _rows_to_cols(_blk(gath, offs, "g128a", "w_q_b", Q_LORA))
    wkv = _rows_to_cols(_blk(gath, offs, "g128a", "w_kv_b", KV_LORA)).reshape(KV_LORA, MLA_HEADS, D_NOPE + D_V)
    o["wk"] = jnp.pad(wkv[:, :, :D_NOPE], ((0, 0), (0, 0), (0, LANES - D_NOPE))).reshape(KV_LORA, MLA_HEADS * LANES)
    o["wv"] = jnp.pad(wkv[:, :, D_NOPE:], ((0, 0), (0, 0), (0, LANES - D_V))).reshape(KV_LORA, MLA_HEADS * LANES)
    o["wv_t"] = o["wv"].T
    return o


def _late_weights(gath, offs):
    def blk(grp, n, a):
        return _blk(gath, offs, grp, n, a)

    o = {}
    wo = _rows_to_cols(blk("g128b", "w_o_mla", MLA_HEADS * D_V)).reshape(MLA_HEADS, D_V, D_MODEL)
    o["wo_mla"] = jnp.pad(wo, ((0, 0), (0, LANES - D_V), (0, 0))).reshape(MLA_HEADS * LANES, D_MODEL)
    o["w_o_ssm"] = _rows_to_cols(blk("g128b", "w_o_ssm", SSM_WIDTH))
    o["w_o_cross"] = _rows_to_cols(blk("g128b", "w_o_cross", X_WIDTH))
    o["w_glu"] = blk("g512", "w_glu", SSM_WIDTH // N_DEV).reshape(SSM_WIDTH, SSM_WIDTH)
    o["w_mem_kv"] = blk("g1024", "w_mem_kv", D_MODEL // N_DEV).reshape(D_MODEL, 2 * X_WIDTH)
    o["w_out"] = blk("g1024", "w_out", D_MODEL // N_DEV).reshape(D_MODEL, D_MODEL)
    o["w_down"] = blk("g1024", "w_down", D_FF // N_DEV).reshape(D_FF, D_MODEL)
    o["w_up"] = _unpack_w_up(gath["g768"], offs["g768"]["w_up"])
    o["w_up_g"], o["w_up_v"] = o["w_up"][:, :D_FF], o["w_up"][:, D_FF:]
    o["conv_w"] = _rows_to_cols(blk("gconv", "conv_w", 3)[:, :, :UP_SHARD])
    return o


def _row(v):
    return v.reshape(1, -1).astype(F32)


def _pad_lanes(v, n=LANES):
    return jnp.pad(v, (0, n - v.shape[0])).reshape(1, n).astype(F32)


def _layer_fwd(x, mem, tabs, w, p, side, late):
    l = x.shape[0]
    rope_c, rope_sa, rope_sb = tabs
    s = {"x": x}
    g_mix, g_qa, g_kva = _row(p["norm_mix_g"]), _row(p["q_a_norm_g"]), _row(p["kv_a_norm_g"])
    g_q, g_k = _pad_lanes(p["q_norm_g"]), _pad_lanes(p["k_norm_g"])

    (h,) = _rowwise("rms_mix", lambda i, r, c: (_rms_f(r[0][...], c[0][...], D_MODEL),), l, 512,
                    [(x, D_MODEL, 0, "row")], [g_mix], [(D_MODEL, BF16)])
    pqkv = _mm("proj_qkv", [(h, w["wqkv"])])
    u = _mm("proj_u", [(h, w["w_u"])])
    xq = _mm("proj_xq", [(h, w["w_xq"])])
    gl = _mm("proj_gate", [(h, w["w_g"])], bm=1024, bn_cap=1024)
    s.update(h=h, pqkv=pqkv, u=u, xq=xq, gl=gl)

    def prep_a(i, r, c):
        return (_rms_f(r[0][:, :Q_LORA], c[0][...], Q_LORA),
                _rms_f(r[0][:, Q_LORA:Q_LORA + KV_LORA], c[1][...], KV_LORA))

    nq, nkv = _rowwise("mla_prep_a", prep_a, l, 512, [(pqkv, QKV_W, 0, "row")], [g_qa, g_kva],
                       [(Q_LORA, BF16), (KV_LORA, BF16)])
    q_raw = _mm("mla_q_b", [(nq, w["wqb"])], bn_cap=1024)
    k_raw = _mm("mla_k_b", [(nkv, w["wk"])], bn_cap=1024)
    v_mla = _mm("mla_v_b", [(nkv, w["wv"])], out_dtype=BF16, bn_cap=1024)
    vt_mla = _mm("mla_vt_b", [(w["wv_t"], nkv)], trans_b=True, out_dtype=BF16, bn_cap=1024)

    def prep_b(i, r, c):
        q_ref, k_ref, kr_ref, c_ref, sa_ref, sb_ref = r
        rc, sa, sb, kr = c_ref[...], sa_ref[...], sb_ref[...], kr_ref[...]
        qs, ks = [], []
        for hd in range(MLA_HEADS):
            cols = slice(hd * LANES, (hd + 1) * LANES)
            qs.append(_rope_f(_rms_f(q_ref[:, cols], c[0][...], D_QK), rc, sa, sb))
            ks.append(_rope_f(_rms_f(k_ref[:, cols] + kr, c[1][...], D_QK), rc, sa, sb))
        return jnp.concatenate(qs, axis=1), jnp.concatenate(ks, axis=1)

    hw = MLA_HEADS * LANES
    kr_blk = (Q_LORA + KV_LORA) // LANES
    tab_ins = [(rope_c, LANES, 0, "row"), (rope_sa, LANES, 0, "row"), (rope_sb, LANES, 0, "row")]
    q, k = _rowwise("mla_prep_b", prep_b, l, 256,
                    [(q_raw, hw, 0, "row"), (k_raw, hw, 0, "row"), (pqkv, LANES, kr_blk, "row")] + tab_ins,
                    [g_q, g_k], [(hw, BF16), (hw, BF16)])
    o_a, lse_a, rode = _attn_fwd("mla_attn_fwd", q, k, vt_mla, qoff=0, koff=0, voff=0, heads=MLA_HEADS, causal=True,
                                 scale=D_QK ** -0.5, bq=1024, bk=1024, side=side)
    w_late, rode = late(rode)
    w = {**w, **w_late}
    ya = _mm("mla_o", [(o_a, w["wo_mla"])], bn_cap=1024)
    s.update(nq=nq, nkv=nkv, q_raw=q_raw, k_raw=k_raw, v_mla=v_mla, q=q, k=k, o_a=o_a, lse_a=lse_a, ya=ya)

    lr = p["ssm_lambda_re"].reshape(SSM_N, 1)
    li = p["ssm_lambda_im"].reshape(SSM_N, 1)
    ldt = jnp.repeat(p["ssm_log_dt"], SSM_STATE).reshape(SSM_N, 1)
    br = p["ssm_b_re"].reshape(SSM_N, SSM_GROUP_CH)
    bi = p["ssm_b_im"].reshape(SSM_N, SSM_GROUP_CH)
    a_re, a_im, bb_re, bb_im = _disc_fwd(lr, li, ldt, br, bi)
    bb_re_d, bb_im_d = _block_diag_in(bb_re).astype(BF16), _block_diag_in(bb_im).astype(BF16)
    cc_re_d = _block_diag_out(p["ssm_c_re"]).astype(BF16)
    cc_imn_d = _block_diag_out(-p["ssm_c_im"]).astype(BF16)
    a_re_row, a_im_row = a_re.reshape(1, SSM_N), a_im.reshape(1, SSM_N)
    d_row = _row(p["ssm_d"])
    b_glu = _row(p["b_glu"])
    s_re, s_im, ypre = _s5_fwd("s5_fwd", u, bb_re_d, bb_im_d, cc_re_d, cc_imn_d, a_re_row, a_im_row)

    def ssm_y(i, r, c):
        return (_gelu(r[0][...] + c[0][...] * r[1][...]),)

    (y_b,) = _rowwise("s5_gelu", ssm_y, l, 512, [(ypre, SSM_WIDTH, 0, "row"), (u, SSM_WIDTH, 0, "row")], [d_row],
                      [(SSM_WIDTH, BF16)])
    z = _mm("s5_glu", [(y_b, w["w_glu"])])

    def ssm_out(i, r, c):
        y = _gelu(r[0][...] + c[0][...] * r[1][...])
        return (y * jax.nn.sigmoid(r[2][...] + c[1][...]),)

    (out_b,) = _rowwise("s5_glu_out", ssm_out, l, 512,
                        [(ypre, SSM_WIDTH, 0, "row"), (u, SSM_WIDTH, 0, "row"), (z, SSM_WIDTH, 0, "row")],
                        [d_row, b_glu], [(SSM_WIDTH, BF16)])
    yb = _mm("s5_o", [(out_b, w["w_o_ssm"])], bn_cap=1024)
    s.update(disc=(lr, li, ldt, br, bi), a_rows=(a_re_row, a_im_row), bb_d=(bb_re_d, bb_im_d),
             cc_d=(cc_re_d, cc_imn_d), s_re=s_re, s_im=s_im, ypre=ypre, y_b=y_b, z=z, out_b=out_b, yb=yb)

    g_mem, g_xq, g_xk = _row(p["mem_norm_g"]), _row(p["xq_norm_g"]), _row(p["xk_norm_g"])
    ml = mem.shape[0]
    (memn,) = _rowwise("rms_mem", lambda i, r, c: (_rms_f(r[0][...], c[0][...], D_MODEL),), ml, 256,
                       [(mem, D_MODEL, 0, "row")], [g_mem], [(D_MODEL, BF16)])
    kvm = _mm("cross_kv", [(memn, w["w_mem_kv"])], bn_cap=1024)

    def headnorm(i, r, c):
        return (jnp.concatenate([_rms_f(r[0][:, hd * LANES:(hd + 1) * LANES], c[0][...], X_HEAD_DIM)
                                 for hd in range(X_HEADS)], axis=1),)

    (xk,) = _rowwise("cross_k_norm", headnorm, ml, 256, [(kvm, X_WIDTH, 0, "row")], [g_xk], [(X_WIDTH, BF16)])
    (xqn,) = _rowwise("cross_q_norm", headnorm, l, 512, [(xq, X_WIDTH, 0, "row")], [g_xq], [(X_WIDTH, BF16)])
    xvt = kvm[:, X_WIDTH:].T.astype(BF16)
    o_c, lse_c, _ = _attn_fwd("cross_attn_fwd", xqn, xk, xvt, qoff=0, koff=0, voff=0, heads=X_HEADS,
                              causal=False, scale=X_HEAD_DIM ** -0.5, bq=1024, bk=256)
    yc = _mm("cross_o", [(o_c, w["w_o_cross"])], bn_cap=1024)
    s.update(memn=memn, kvm=kvm, xk=xk, xqn=xqn, o_c=o_c, lse_c=lse_c, yc=yc)

    b_gate = _row(p["b_gate"])

    def merge(i, r, c):
        acc = None
        for br_ in range(3):
            g = jax.nn.sigmoid(r[br_][...] + c[0][:, br_ * D_MODEL:(br_ + 1) * D_MODEL])
            t = g * r[3 + br_][...]
            acc = t if acc is None else acc + t
        return (acc,)

    gate_ins = [(gl, D_MODEL, b_, "row") for b_ in range(3)]
    (merged,) = _rowwise("merge", merge, l, 256,
                         gate_ins + [(ya, D_MODEL, 0, "row"), (yb, D_MODEL, 0, "row"), (yc, D_MODEL, 0, "row")],
                         [b_gate], [(D_MODEL, BF16)])
    x1 = _mm("mix_out", [(merged, w["w_out"])], add=x, bn_cap=1024)
    s.update(merged=merged, x1=x1)

    g_ffn = _row(p["norm_ffn_g"])
    (h2,) = _rowwise("rms_ffn", lambda i, r, c: (_rms_f(r[0][...], c[0][...], D_MODEL),), l, 512,
                     [(x1, D_MODEL, 0, "row")], [g_ffn], [(D_MODEL, BF16)])
    up = _mm("ffn_up", [(h2, w["w_up"])], bm=1024, bn_cap=1408)
    conv_w = w["conv_w"]
    conv_b = _row(p["conv_b"])

    def conv_glu(i, r, c):
        cg = _conv(r[0], r[2], i, c[0], c[1], 0)
        cv = _conv(r[1], r[3], i, c[0], c[1], D_FF)
        return (cg * jax.nn.sigmoid(cg) * cv,)

    up_ins = [(up, D_FF, 0, "row"), (up, D_FF, 1, "row"), (up, D_FF, 0, "prev"), (up, D_FF, 1, "prev")]
    (act,) = _rowwise("ffn_conv_glu", conv_glu, l, 256, up_ins, [conv_w, conv_b], [(D_FF, BF16)])
    x2 = _mm("ffn_down", [(act, w["w_down"])], add=x1, bm=1024, bn_cap=1024)
    s.update(h2=h2, up=up, act=act, conv_w=conv_w, conv_b=conv_b)
    return x2, s, w, rode


def _conv(x_ref, halo_ref, i, w_ref, b_ref, col0):
    x = x_ref[...]
    cols = slice(col0, col0 + D_FF)
    return (w_ref[0:1, cols] * _shift_down(x, halo_ref, i, 2) + w_ref[1:2, cols] * _shift_down(x, halo_ref, i, 1)
            + w_ref[2:3, cols] * x + b_ref[:, cols])


def _layer_bwd(dx2, dx2_b, s, mem, tabs, w, p, make_side):
    l = dx2.shape[0]
    rope_c, rope_sa, rope_sb = tabs
    x, x1 = s["x"], s["x1"]
    g = {}

    dact = _mm("ffn_down_dx", [(dx2_b, w["w_down"])], trans_b=True, bn_cap=1408)
    g["w_down"] = _mm_tn("ffn_down_dw", s["act"], dx2_b, bm_cap=1408).reshape(N_DEV, D_FF // N_DEV, D_MODEL)
    up = s["up"]
    nblk_c = l // min(256, l)

    def conv_bwd_a(i, r, c):
        outs, accs = [], []
        for half in range(2):
            x_ref, halo_ref = r[1 + half], r[3 + half]
            col0 = half * D_FF
            cols = slice(col0, col0 + D_FF)
            xv = x_ref[...]
            xm1, xm2 = _shift_down(xv, halo_ref, i, 1), _shift_down(xv, halo_ref, i, 2)
            cv = c[0][0:1, cols] * xm2 + c[0][1:2, cols] * xm1 + c[0][2:3, cols] * xv + c[1][:, cols]
            outs.append((cv, xv, xm1, xm2))
        (cg, xg, xg1, xg2), (cvv, xv, xv1, xv2) = outs
        sig = jax.nn.sigmoid(cg)
        da = r[0][...]
        dcv = da * (cg * sig)
        dcg = da * cvv * (sig * (1.0 + cg * (1.0 - sig)))
        for d, (x0, xa, xb) in ((dcg, (xg, xg1, xg2)), (dcv, (xv, xv1, xv2))):
            accs += [jnp.sum(d, axis=0, keepdims=True), jnp.sum(d * xb, axis=0, keepdims=True),
                     jnp.sum(d * xa, axis=0, keepdims=True), jnp.sum(d * x0, axis=0, keepdims=True)]
        return (dcg, dcv, *accs)

    up_ins = [(up, D_FF, 0, "row"), (up, D_FF, 1, "row"), (up, D_FF, 0, "prev"), (up, D_FF, 1, "prev")]
    res = _rowwise("ffn_conv_glu_bwd", conv_bwd_a, l, 256, [(dact, D_FF, 0, "row")] + up_ins,
                   [s["conv_w"], s["conv_b"]], [(D_FF, F32), (D_FF, F32)], [(1, D_FF)] * 8)
    dcg, dcv = res[0], res[1]
    db_g, dw0_g, dw1_g, dw2_g, db_v, dw0_v, dw1_v, dw2_v = res[2:]
    g["conv_b"] = jnp.concatenate([db_g, db_v], axis=1)[0]
    g["conv_w"] = _cols_to_rows(jnp.concatenate(
        [jnp.concatenate([dw0_g, dw0_v], axis=1), jnp.concatenate([dw1_g, dw1_v], axis=1),
         jnp.concatenate([dw2_g, dw2_v], axis=1)], axis=0))

    def conv_bwd_b(i, r, c):
        outs = []
        for half in range(2):
            d = r[half][...]
            cols = slice(half * D_FF, (half + 1) * D_FF)
            outs.append(c[0][2:3, cols] * d + c[0][1:2, cols] * _shift_up(d, r[2 + half], i, nblk_c, 1)
                        + c[0][0:1, cols] * _shift_up(d, r[2 + half], i, nblk_c, 2))
        return tuple(outs)

    dup_g, dup_v = _rowwise("ffn_conv_bwd_x", conv_bwd_b, l, 256,
                            [(dcg, D_FF, 0, "row"), (dcv, D_FF, 0, "row"), (dcg, D_FF, 0, "next"),
                             (dcv, D_FF, 0, "next")], [s["conv_w"]], [(D_FF, BF16), (D_FF, BF16)])
    dh2 = _mm("ffn_up_dx", [(dup_g, w["w_up_g"]), (dup_v, w["w_up_v"])], trans_b=True, bn_cap=1024)
    g["w_up"] = _pack_w_up(_mm_tn("ffn_up_dw_g", s["h2"], dup_g), _mm_tn("ffn_up_dw_v", s["h2"], dup_v))

    def rms_bwd_res(i, r, c):
        dx, dg = _rms_b(r[0][...], c[0][...], r[1][...], D_MODEL)
        dx = dx + r[2][...]
        return (dx, dx, dg)

    dx1, dx1_b, g["norm_ffn_g"] = _rowwise(
        "rms_ffn_bwd", rms_bwd_res, l, 512, [(x1, D_MODEL, 0, "row"), (dh2, D_MODEL, 0, "row"), (dx2, D_MODEL, 0, "row")],
        [_row(p["norm_ffn_g"])], [(D_MODEL, F32), (D_MODEL, BF16)], [(1, D_MODEL)])

    dmerged = _mm("mix_out_dx", [(dx1_b, w["w_out"])], trans_b=True, bn_cap=1024)
    g["w_out"] = _mm_tn("mix_out_dw", s["merged"], dx1_b).reshape(N_DEV, D_MODEL // N_DEV, D_MODEL)
    gl, ya, yb, yc = s["gl"], s["ya"], s["yb"], s["yc"]

    def merge_bwd(i, r, c):
        dm = r[0][...]
        dys, dgs = [], []
        for b_ in range(3):
            gate = jax.nn.sigmoid(r[1 + b_][...] + c[0][:, b_ * D_MODEL:(b_ + 1) * D_MODEL])
            dys.append(dm * gate)
            dgs.append(dm * r[4 + b_][...] * (gate * (1.0 - gate)))
        dgl = jnp.concatenate(dgs, axis=1)
        return (*dys, dgl, jnp.sum(dgl, axis=0, keepdims=True))

    gate_ins = [(gl, D_MODEL, b_, "row") for b_ in range(3)]
    dya, dyb, dyc, dgl, db_gate = _rowwise(
        "merge_bwd", merge_bwd, l, 256,
        [(dmerged, D_MODEL, 0, "row")] + gate_ins + [(ya, D_MODEL, 0, "row"), (yb, D_MODEL, 0, "row"),
                                                     (yc, D_MODEL, 0, "row")],
        [_row(p["b_gate"])], [(D_MODEL, BF16)] * 3 + [(3 * D_MODEL, BF16)], [(1, 3 * D_MODEL)])
    g["b_gate"] = db_gate[0]

    do_c = _mm("cross_o_dx", [(dyc, w["w_o_cross"])], trans_b=True, out_dtype=BF16)
    g["w_o_cross"] = _cols_to_rows(_mm_tn("cross_o_dw", s["o_c"], dyc))
    kvm = s["kvm"]
    delta_c = _attn_delta("cross_attn_delta", s["o_c"], do_c, heads=X_HEADS, bq=2048)
    dxqn, dxk, dxv, _ = _attn_bwd("cross_attn_bwd", s["xqn"], s["xk"], s["xk"].T, kvm, do_c, s["lse_c"], delta_c,
                                  qoff=0, koff=0, voff=X_HEADS, heads=X_HEADS, causal=False,
                                  scale=X_HEAD_DIM ** -0.5, bq=1024, bk=256)
    ml = mem.shape[0]

    def headnorm_bwd(i, r, c):
        dxs, dg = [], None
        for hd in range(X_HEADS):
            cols = slice(hd * LANES, (hd + 1) * LANES)
            dx_h, dg_h = _rms_b(r[0][:, cols], c[0][...], r[1][:, cols], X_HEAD_DIM)
            dxs.append(dx_h)
            dg = dg_h if dg is None else dg + dg_h
        return (jnp.concatenate(dxs, axis=1), dg)

    dxq, dg_xq = _rowwise("cross_q_norm_bwd", headnorm_bwd, l, 512,
                          [(s["xq"], X_WIDTH, 0, "row"), (dxqn, X_WIDTH, 0, "row")], [_row(p["xq_norm_g"])],
                          [(X_WIDTH, BF16)], [(1, X_HEAD_DIM)])
    dkvm_k, dg_xk = _rowwise("cross_k_norm_bwd", headnorm_bwd, ml, 256,
                             [(kvm, X_WIDTH, 0, "row"), (dxk, X_WIDTH, 0, "row")], [_row(p["xk_norm_g"])],
                             [(X_WIDTH, F32)], [(1, X_HEAD_DIM)])
    g["xq_norm_g"], g["xk_norm_g"] = dg_xq[0], dg_xk[0]
    dkvm = jnp.concatenate([dkvm_k, dxv], axis=1)
    g["w_mem_kv"] = _mm_tn("cross_kv_dw", s["memn"], dkvm).reshape(N_DEV, D_MODEL // N_DEV, 2 * X_WIDTH)
    dmemn = _mm("cross_kv_dx", [(dkvm, w["w_mem_kv"])], trans_b=True, bn_cap=1024)

    def rms_bwd_gain_only(i, r, c):
        return (_rms_b(r[0][...], c[0][...], r[1][...], D_MODEL)[1],)

    (dg_mem,) = _rowwise("rms_mem_bwd", rms_bwd_gain_only, ml, 256,
                         [(mem, D_MODEL, 0, "row"), (dmemn, D_MODEL, 0, "row")], [_row(p["mem_norm_g"])], [],
                         [(1, D_MODEL)])
    g["mem_norm_g"] = dg_mem[0]

    dout_b = _mm("s5_o_dx", [(dyb, w["w_o_ssm"])], trans_b=True)
    g["w_o_ssm"] = _cols_to_rows(_mm_tn("s5_o_dw", s["out_b"], dyb))
    ypre, u, z = s["ypre"], s["u"], s["z"]
    d_row, b_glu = _row(p["ssm_d"]), _row(p["b_glu"])
    yuz = [(ypre, SSM_WIDTH, 0, "row"), (u, SSM_WIDTH, 0, "row"), (z, SSM_WIDTH, 0, "row")]

    def glu_bwd_z(i, r, c):
        y = _gelu(r[1][...] + c[0][...] * r[2][...])
        sg = jax.nn.sigmoid(r[3][...] + c[1][...])
        dz = r[0][...] * y * (sg * (1.0 - sg))
        return (dz, jnp.sum(dz, axis=0, keepdims=True))

    dz, db_glu = _rowwise("s5_glu_bwd_z", glu_bwd_z, l, 512, [(dout_b, SSM_WIDTH, 0, "row")] + yuz, [d_row, b_glu],
                          [(SSM_WIDTH, BF16)], [(1, SSM_WIDTH)])
    g["b_glu"] = db_glu[0]
    g["w_glu"] = _mm_tn("s5_glu_dw", s["y_b"], dz).reshape(N_DEV, SSM_WIDTH // N_DEV, SSM_WIDTH)
    dy2 = _mm("s5_glu_dx", [(dz, w["w_glu"])], trans_b=True)

    def gelu_bwd(i, r, c):
        t = r[2][...] + c[0][...] * r[3][...]
        sg = jax.nn.sigmoid(r[4][...] + c[1][...])
        dt = (r[0][...] * sg + r[1][...]) * _gelu_grad(t)
        return (dt, c[0][...] * dt, jnp.sum(dt * r[3][...], axis=0, keepdims=True))

    dypre, du_skip, dd = _rowwise(
        "s5_gelu_bwd", gelu_bwd, l, 512, [(dout_b, SSM_WIDTH, 0, "row"), (dy2, SSM_WIDTH, 0, "row")] + yuz,
        [d_row, b_glu], [(SSM_WIDTH, BF16), (SSM_WIDTH, F32)], [(1, SSM_WIDTH)])
    g["ssm_d"] = dd.reshape(SSM_GROUPS, SSM_GROUP_CH)
    cc_re_d, cc_imn_d = s["cc_d"]
    bb_re_d, bb_im_d = s["bb_d"]
    a_re_row, a_im_row = s["a_rows"]
    s_re, s_im = s["s_re"], s["s_im"]
    du, da_re, da_im, dbb_re_d, dbb_im_d, dcc_re, dcc_imn = _s5_bwd(
        "s5_bwd", dypre, du_skip, u, s_re, s_im, bb_re_d, bb_im_d, cc_re_d, cc_imn_d, a_re_row, -a_im_row)
    g["ssm_c_re"] = _diag_blocks(dcc_re, SSM_STATE, SSM_GROUP_CH).transpose(0, 2, 1)
    g["ssm_c_im"] = -_diag_blocks(dcc_imn, SSM_STATE, SSM_GROUP_CH).transpose(0, 2, 1)
    dbb_re = _diag_blocks(dbb_re_d, SSM_GROUP_CH, SSM_STATE).transpose(0, 2, 1).reshape(SSM_N, SSM_GROUP_CH)
    dbb_im = _diag_blocks(dbb_im_d, SSM_GROUP_CH, SSM_STATE).transpose(0, 2, 1).reshape(SSM_N, SSM_GROUP_CH)
    dlr, dli, dldt, dbr, dbi = _disc_bwd(*s["disc"], da_re.reshape(SSM_N, 1), da_im.reshape(SSM_N, 1), dbb_re, dbb_im)
    g["ssm_lambda_re"] = dlr.reshape(SSM_GROUPS, SSM_STATE)
    g["ssm_lambda_im"] = dli.reshape(SSM_GROUPS, SSM_STATE)
    g["ssm_log_dt"] = dldt.reshape(SSM_GROUPS, SSM_STATE).sum(axis=1)
    g["ssm_b_re"] = dbr.reshape(SSM_GROUPS, SSM_STATE, SSM_GROUP_CH)
    g["ssm_b_im"] = dbi.reshape(SSM_GROUPS, SSM_STATE, SSM_GROUP_CH)

    do_a = _mm("mla_o_dx", [(dya, w["wo_mla"])], trans_b=True, out_dtype=BF16, bn_cap=1024)
    dwo = _mm_tn("mla_o_dw", s["o_a"], dya)
    g["w_o_mla"] = _cols_to_rows(dwo.reshape(MLA_HEADS, LANES, D_MODEL)[:, :D_V].reshape(MLA_HEADS * D_V, D_MODEL))
    delta_a = _attn_delta("mla_attn_delta", s["o_a"], do_a, heads=MLA_HEADS, bq=2048)
    dq, dk, dv, rode = _attn_bwd("mla_attn_bwd", s["q"], s["k"], s["k"].T, s["v_mla"], do_a, s["lse_a"], delta_a,
                                 qoff=0, koff=0, voff=0, heads=MLA_HEADS, causal=True, scale=D_QK ** -0.5, bq=1024,
                                 bk=1024, side=make_side(g))
    hw = MLA_HEADS * LANES
    kr_blk = (Q_LORA + KV_LORA) // LANES
    pqkv = s["pqkv"]
    g_q, g_k = _pad_lanes(p["q_norm_g"]), _pad_lanes(p["k_norm_g"])
    lane = lax.broadcasted_iota(jnp.int32, (1, LANES), 1)
    kr_mask = jnp.logical_and(lane >= KR_LO, lane < KR_LO + D_ROPE).astype(F32)

    def prep_b_bwd(i, r, c):
        dq_ref, dk_ref, q_ref, k_ref, kr_ref, c_ref, sa_ref, sb_ref = r
        rc, sa, sb, kr = c_ref[...], sa_ref[...], sb_ref[...], kr_ref[...]
        dqs, dks, dkr, dgq, dgk = [], [], None, None, None
        for hd in range(MLA_HEADS):
            cols = slice(hd * LANES, (hd + 1) * LANES)
            dxq, dgq_h = _rms_b(q_ref[:, cols], c[0][...], _rope_b(dq_ref[:, cols], rc, sa, sb), D_QK)
            dxk, dgk_h = _rms_b(k_ref[:, cols] + kr, c[1][...], _rope_b(dk_ref[:, cols], rc, sa, sb), D_QK)
            dqs.append(dxq)
            dks.append(dxk)
            dkr = dxk if dkr is None else dkr + dxk
            dgq = dgq_h if dgq is None else dgq + dgq_h
            dgk = dgk_h if dgk is None else dgk + dgk_h
        return (jnp.concatenate(dqs, axis=1), jnp.concatenate(dks, axis=1), dkr * c[2][...], dgq, dgk)

    tab_ins = [(rope_c, LANES, 0, "row"), (rope_sa, LANES, 0, "row"), (rope_sb, LANES, 0, "row")]
    dq_raw, dk_raw, dkr, dg_q, dg_k = _rowwise(
        "mla_prep_b_bwd", prep_b_bwd, l, 256,
        [(dq, hw, 0, "row"), (dk, hw, 0, "row"), (s["q_raw"], hw, 0, "row"), (s["k_raw"], hw, 0, "row"),
         (pqkv, LANES, kr_blk, "row")] + tab_ins, [g_q, g_k, kr_mask],
        [(hw, BF16), (hw, BF16), (LANES, F32)], [(1, LANES), (1, LANES)])
    g["q_norm_g"], g["k_norm_g"] = dg_q[0, :D_QK], dg_k[0, :D_QK]
    dnq = _mm("mla_q_b_dx", [(dq_raw, w["wqb"])], trans_b=True)
    dnkv = _mm("mla_kv_b_dx", [(dk_raw, w["wk"]), (dv, w["wv"])], trans_b=True)
    dwqb = _mm_tn("mla_q_b_dw", s["nq"], dq_raw)
    g["w_q_b"] = _cols_to_rows(dwqb)
    dwk = _mm_tn("mla_k_b_dw", s["nkv"], dk_raw).reshape(KV_LORA, MLA_HEADS, LANES)[:, :, :D_NOPE]
    dwv = _mm_tn("mla_v_b_dw", s["nkv"], dv).reshape(KV_LORA, MLA_HEADS, LANES)[:, :, :D_V]
    g["w_kv_b"] = jnp.concatenate([dwk, dwv], axis=2).transpose(1, 0, 2)

    def prep_a_bwd(i, r, c):
        dcq, dgqa = _rms_b(r[0][:, :Q_LORA], c[0][...], r[1][...], Q_LORA)
        dckv, dgkva = _rms_b(r[0][:, Q_LORA:Q_LORA + KV_LORA], c[1][...], r[2][...], KV_LORA)
        return (jnp.concatenate([dcq, dckv, r[3][...]], axis=1), dgqa, dgkva)

    dpqkv, dg_qa, dg_kva = _rowwise(
        "mla_prep_a_bwd", prep_a_bwd, l, 512,
        [(pqkv, QKV_W, 0, "row"), (dnq, Q_LORA, 0, "row"), (dnkv, KV_LORA, 0, "row"), (dkr, LANES, 0, "row")],
        [_row(p["q_a_norm_g"]), _row(p["kv_a_norm_g"])], [(QKV_W, BF16)], [(1, Q_LORA), (1, KV_LORA)])
    g["q_a_norm_g"], g["kv_a_norm_g"] = dg_qa[0], dg_kva[0]

    h = s["h"]
    dh = _mm("proj_dx", [(dpqkv, w["wqkv"]), (du, w["w_u"]), (dxq, w["w_xq"]), (dgl, w["w_g"])], trans_b=True,
             bn_cap=1024)
    dwqkv = _mm_tn("proj_qkv_dw", h, dpqkv)
    g["w_in"] = _pack_w_in([dwqkv, _mm_tn("proj_u_dw", h, du), _mm_tn("proj_xq_dw", h, dxq),
                            _mm_tn("proj_gate_dw", h, dgl)])
    dx, dx_b, dg_mix = _rowwise(
        "rms_mix_bwd", rms_bwd_res, l, 512, [(x, D_MODEL, 0, "row"), (dh, D_MODEL, 0, "row"), (dx1, D_MODEL, 0, "row")],
        [_row(p["norm_mix_g"])], [(D_MODEL, F32), (D_MODEL, BF16)], [(1, D_MODEL)])
    g["norm_mix_g"] = dg_mix[0]
    g["norm_ffn_g"] = g["norm_ffn_g"][0]
    return dx, dx_b, g, rode


def _rope_tables(positions):
    inv_freq = ROPE_THETA ** (-jnp.arange(0, D_ROPE, 2, dtype=F32) / D_ROPE)
    ang = positions.astype(F32)[:, None] * inv_freq
    cos, sin = jnp.cos(ang), jnp.sin(ang)
    l = positions.shape[0]
    one, zero = jnp.ones((l, D_NOPE), F32), lambda n: jnp.zeros((l, n), F32)
    pad = LANES - D_QK
    rope_c = jnp.concatenate([one, cos, cos, zero(pad)], axis=1)
    rope_sa = jnp.concatenate([zero(D_NOPE), -sin, zero(16), zero(pad)], axis=1)
    rope_sb = jnp.concatenate([zero(D_NOPE + 16), sin, zero(pad)], axis=1)
    return rope_c, rope_sa, rope_sb


def kernel(x, mem, positions, norm_mix_g, w_in, q_a_norm_g, w_q_b, kv_a_norm_g, w_kv_b, q_norm_g, k_norm_g, w_o_mla, ssm_lambda_re, ssm_lambda_im, ssm_log_dt, ssm_b_re, ssm_b_im, ssm_c_re, ssm_c_im, ssm_d, w_glu, b_glu, w_o_ssm, mem_norm_g, w_mem_kv, xq_norm_g, xk_norm_g, w_o_cross, b_gate, w_out, norm_ffn_g, w_up, conv_w, conv_b, w_down, loss_target, m_norm_mix_g, m_w_in, m_q_a_norm_g, m_w_q_b, m_kv_a_norm_g, m_w_kv_b, m_q_norm_g, m_k_norm_g, m_w_o_mla, m_ssm_lambda_re, m_ssm_lambda_im, m_ssm_log_dt, m_ssm_b_re, m_ssm_b_im, m_ssm_c_re, m_ssm_c_im, m_ssm_d, m_w_glu, m_b_glu, m_w_o_ssm, m_mem_norm_g, m_w_mem_kv, m_xq_norm_g, m_xk_norm_g, m_w_o_cross, m_b_gate, m_w_out, m_norm_ffn_g, m_w_up, m_conv_w, m_conv_b, m_w_down, v_norm_mix_g, v_w_in, v_q_a_norm_g, v_w_q_b, v_kv_a_norm_g, v_w_kv_b, v_q_norm_g, v_k_norm_g, v_w_o_mla, v_ssm_lambda_re, v_ssm_lambda_im, v_ssm_log_dt, v_ssm_b_re, v_ssm_b_im, v_ssm_c_re, v_ssm_c_im, v_ssm_d, v_w_glu, v_b_glu, v_w_o_ssm, v_mem_norm_g, v_w_mem_kv, v_xq_norm_g, v_xk_norm_g, v_w_o_cross, v_b_gate, v_w_out, v_norm_ffn_g, v_w_up, v_conv_w, v_conv_b, v_w_down):
    a = dict(locals())
    wts = {n: a[n] for n in WEIGHT_ORDER}
    m_in = {n: a["m_" + n] for n in WEIGHT_ORDER}
    v_in = {n: a["v_" + n] for n in WEIGHT_ORDER}
    depth = norm_mix_g.shape[0]
    x0, mem0, pos0, tgt = x[0], mem[0], positions[0], loss_target[0]
    l = x0.shape[0]
    offs = {grp: _group_rows(params) for grp, _, params in GROUPS}
    early_names, late_names = [g[0] for g in EARLY_GROUPS], [g[0] for g in LATE_GROUPS]

    def srcs(i, groups):
        return [_group_local(width, params, wts, i, F32 if grp == "gconv" else BF16) for grp, width, params in groups]

    tabs = _rope_tables(pos0)
    layer_p = [{n: wts[n][i] for n in REPLICATED} for i in range(depth)]
    gath = dict(zip(early_names, _gather_all("gather_weights", srcs(0, EARLY_GROUPS))))

    def late(rode):
        return _late_weights(dict(zip(late_names, rode)), offs), rode[len(late_names):]

    saved, layer_w = [], []
    xc = x0
    for i in range(depth):
        ride = srcs(i, LATE_GROUPS) + (srcs(i + 1, EARLY_GROUPS) if i + 1 < depth else [])
        side = _Side(ride, _gather_shapes(ride), GATHER_SEMS, (_gather_start, _gather_relay, _gather_finish))
        xc, s, w_all, rode = _layer_fwd(xc, mem0, tabs, _early_weights(gath, offs), layer_p[i], side, late)
        gath = dict(zip(early_names, rode))
        layer_w.append(w_all)
        saved.append(s)

    def loss_fn(i, r, c):
        d = r[0][...] - r[1][...]
        dy = d * (1.0 / D_MODEL)
        return (dy, dy, jnp.sum(d * d, axis=0, keepdims=True))

    dy, dy_b, sq = _rowwise("loss", loss_fn, l, 512, [(xc, D_MODEL, 0, "row"), (tgt, D_MODEL, 0, "row")], [],
                            [(D_MODEL, F32), (D_MODEL, BF16)], [(1, D_MODEL)])
    loss = lax.psum(0.5 * jnp.sum(sq) / D_MODEL, ("x", "y", "c"))

    def core_sums(g, tag, groups, extra=()):
        sends = []
        for _, width, params in groups:
            blocks = [jnp.pad(g[n], ((0, 0), (0, _rows8(rows) - rows), (0, width - g[n].shape[2])))
                      for n, rows, _ in params]
            sends.append(jnp.concatenate(blocks, axis=1))
        sends += list(extra)
        got = _pair_exchange("exchange_grads_core_" + tag, sends)
        return [_pair_sum("grad_pair_sum_%s_%d" % (tag, k), s_, g_) for k, (s_, g_) in enumerate(zip(sends, got))]

    grads, rcvs = [None] * depth, {}
    dxc, dxc_b = dy, dy_b
    waiting = []
    for i in reversed(range(depth)):
        def make_side(g, i=i, waiting=waiting):
            parts = core_sums(g, "l%d_late" % i, LATE_GROUPS) + waiting
            return _Side(parts, _chip_shapes(parts), CHIP_SEMS, (_chip_start, _chip_finish))

        dxc, dxc_b, grads[i], rode = _layer_bwd(dxc, dxc_b, saved[i], mem0, tabs, layer_w[i], layer_p[i], make_side)
        rcvs.update({(i, nm): r for nm, r in zip(late_names, rode)})
        rcvs.update({(i + 1, nm): r for nm, r in zip(early_names, rode[len(late_names):])})
        waiting = core_sums(grads[i], "l%d_early" % i, EARLY_GROUPS) if i > 0 else []
    grad_x = dxc[None]
    rep_flat = jnp.concatenate([jnp.stack([grads[i][n] for i in range(depth)]).reshape(-1) for n in REPLICATED])
    rep_send = _rep_rows(jnp.broadcast_to(rep_flat[None], (N_DEV, rep_flat.shape[0])))
    last = _chip_exchange("exchange_grads_chip", core_sums(grads[0], "l0_early", EARLY_GROUPS, [rep_send]))
    rcvs.update({(0, nm): r for nm, r in zip(early_names, last)})
    rcv_rep = last[-1]

    per_layer = {}
    for i in range(depth):
        for grp, width, params in GROUPS:
            local = [_group_local(width, params, d, i, F32) for d in (wts, m_in, v_in)]
            res = _adamw("adamw_l%d_%s" % (i, grp), rcvs[(i, grp)], *local)
            for tag, arr in zip(("grad", "delta", "m", "v"), res):
                for n, rows, cols in params:
                    per_layer[(tag, n, i)] = arr[offs[grp][n]:offs[grp][n] + rows, :cols]
    outs = {(tag, n): jnp.stack([per_layer[(tag, n, i)] for i in range(depth)])
            for tag in ("grad", "delta", "m", "v") for _, _, params in GROUPS for n, _, _ in params}
    rep_local = [_rep_rows(jnp.concatenate([d[n].astype(F32).reshape(-1) for n in REPLICATED])) for d in (wts, m_in, v_in)]
    res = _adamw("adamw_rep", rcv_rep, *rep_local)
    for tag, arr in zip(("grad", "delta", "m", "v"), res):
        flat, off = arr.reshape(-1), 0
        for n in REPLICATED:
            cnt = wts[n].size
            outs[(tag, n)] = flat[off:off + cnt].reshape(wts[n].shape)
            off += cnt
    result = [loss, grad_x]
    for tag in ("grad", "delta", "m", "v"):
        result += [outs[(tag, n)] for n in WEIGHT_ORDER]
    return tuple(result)
```

```python
import collections
import math

import jax
import jax.numpy as jnp
from jax import lax
from jax.experimental import pallas as pl
from jax.experimental.pallas import tpu as pltpu

F32 = jnp.float32
BF16 = jnp.bfloat16

N_DEV = 8
LANES = 128
LOG2E = math.log2(math.e)
VMEM_LIMIT_BYTES = 56 * 1024 * 1024

D_MODEL = 1024
EPS = 1e-6
MLA_HEADS = 8
Q_LORA = 384
KV_LORA = 256
D_NOPE = 64
D_ROPE = 32
D_QK = D_NOPE + D_ROPE
D_V = 64
ROPE_THETA = 10000.0
SSM_GROUPS = 32
SSM_GROUP_CH = 16
SSM_WIDTH = 512
SSM_STATE = 64
SSM_N = SSM_GROUPS * SSM_STATE
X_HEADS = 4
X_HEAD_DIM = 128
X_WIDTH = 512
D_FF = 2816
IN_WIDTH = Q_LORA + KV_LORA + D_ROPE + SSM_WIDTH + X_WIDTH + 3 * D_MODEL
QKV_W = Q_LORA + KV_LORA + LANES
KR_LO = D_NOPE

ADAM_LR = 0.001
ADAM_B1 = 0.9
ADAM_B2 = 0.999
ADAM_EPS = 1e-08
ADAM_WD = 0.01
ADAM_STEP = 10

REPLICATED = (
    "norm_mix_g", "q_a_norm_g", "kv_a_norm_g", "q_norm_g", "k_norm_g", "ssm_lambda_re", "ssm_lambda_im",
    "ssm_log_dt", "ssm_b_re", "ssm_b_im", "ssm_c_re", "ssm_c_im", "ssm_d", "b_glu", "mem_norm_g",
    "xq_norm_g", "xk_norm_g", "b_gate", "norm_ffn_g", "conv_b",
)
WEIGHT_ORDER = (
    "norm_mix_g", "w_in", "q_a_norm_g", "w_q_b", "kv_a_norm_g", "w_kv_b", "q_norm_g", "k_norm_g", "w_o_mla",
    "ssm_lambda_re", "ssm_lambda_im", "ssm_log_dt", "ssm_b_re", "ssm_b_im", "ssm_c_re", "ssm_c_im", "ssm_d",
    "w_glu", "b_glu", "w_o_ssm", "mem_norm_g", "w_mem_kv", "xq_norm_g", "xk_norm_g", "w_o_cross", "b_gate",
    "w_out", "norm_ffn_g", "w_up", "conv_w", "conv_b", "w_down",
)


def _params(**kw):
    return pltpu.CompilerParams(vmem_limit_bytes=VMEM_LIMIT_BYTES, **kw)


def _pick(n, cap):
    if n <= cap:
        return n
    best = None
    for m in range(LANES, cap + 1, LANES):
        if n % m == 0:
            best = m
    assert best is not None, n
    return best


_NN = (((1,), (0,)), ((), ()))
_NT = (((1,), (1,)), ((), ()))
_TN = (((0,), (0,)), ((), ()))


def _dot(a, b, dn):
    return lax.dot_general(a.astype(BF16), b.astype(BF16), dn, preferred_element_type=F32)


def _mm(name, pairs, *, trans_b=False, add=None, out_dtype=F32, bm=512, bn_cap=512):
    m = pairs[0][0].shape[0]
    n = pairs[0][1].shape[0 if trans_b else 1]
    bm = min(bm, m)
    bn = _pick(n, bn_cap)
    npair = len(pairs)

    def body(*refs):
        o_ref = refs[-1]
        acc = None
        for p in range(npair):
            d = _dot(refs[2 * p][...], refs[2 * p + 1][...], _NT if trans_b else _NN)
            acc = d if acc is None else acc + d
        if add is not None:
            acc = acc + refs[2 * npair][...]
        o_ref[...] = acc.astype(out_dtype)

    in_specs, args = [], []
    for a, b in pairs:
        k = a.shape[1]
        in_specs.append(pl.BlockSpec((bm, k), lambda i, j: (i, 0)))
        if trans_b:
            in_specs.append(pl.BlockSpec((bn, k), lambda i, j: (j, 0)))
        else:
            in_specs.append(pl.BlockSpec((k, bn), lambda i, j: (0, j)))
        args += [a, b]
    if add is not None:
        in_specs.append(pl.BlockSpec((bm, bn), lambda i, j: (i, j)))
        args.append(add)
    return pl.pallas_call(
        body, name=name, grid=(m // bm, n // bn), in_specs=in_specs,
        out_specs=pl.BlockSpec((bm, bn), lambda i, j: (i, j)),
        out_shape=jax.ShapeDtypeStruct((m, n), out_dtype), compiler_params=_params(),
    )(*args)


def _mm_tn(name, a, b, *, bm_cap=512, bn_cap=1536, bk=1024):
    l, m = a.shape
    n = b.shape[1]
    bm, bn, bk = _pick(m, bm_cap), _pick(n, bn_cap), min(bk, l)

    def body(a_ref, b_ref, o_ref):
        @pl.when(pl.program_id(2) == 0)
        def _():
            o_ref[...] = jnp.zeros_like(o_ref)

        o_ref[...] += _dot(a_ref[...], b_ref[...], _TN)

    return pl.pallas_call(
        body, name=name, grid=(m // bm, n // bn, l // bk),
        in_specs=[pl.BlockSpec((bk, bm), lambda i, j, k: (k, i)), pl.BlockSpec((bk, bn), lambda i, j, k: (k, j))],
        out_specs=pl.BlockSpec((bm, bn), lambda i, j, k: (i, j)),
        out_shape=jax.ShapeDtypeStruct((m, n), F32), compiler_params=_params(),
    )(a, b)


def _rowwise(name, fn, nrows, bm, row_ins, consts, row_outs, acc_outs=()):
    bm = min(bm, nrows)
    nblk = nrows // bm
    sub = bm // 8
    nin, nc, nro = len(row_ins), len(consts), len(row_outs)

    def body(*refs):
        i = pl.program_id(0)
        outs = fn(i, refs[:nin], refs[nin:nin + nc])
        o_refs = refs[nin + nc:nin + nc + nro]
        a_refs = refs[nin + nc + nro:]
        for r, v in zip(o_refs, outs[:nro]):
            r[...] = v.astype(r.dtype)
        if a_refs:
            @pl.when(i == 0)
            def _():
                for r in a_refs:
                    r[...] = jnp.zeros_like(r)

            for r, v in zip(a_refs, outs[nro:]):
                r[...] += v

    in_specs, args = [], []
    for arr, w, cb, kind in row_ins:
        if kind == "row":
            in_specs.append(pl.BlockSpec((bm, w), lambda i, cb=cb: (i, cb)))
        elif kind == "prev":
            in_specs.append(pl.BlockSpec((8, w), lambda i, cb=cb: (jnp.maximum(i * sub - 1, 0), cb)))
        else:
            in_specs.append(pl.BlockSpec((8, w), lambda i, cb=cb: (jnp.minimum((i + 1) * sub, nrows // 8 - 1), cb)))
        args.append(arr)
    for c in consts:
        in_specs.append(pl.BlockSpec(c.shape, lambda i: (0, 0)))
        args.append(c)
    out_specs = [pl.BlockSpec((bm, w), lambda i: (i, 0)) for w, _ in row_outs]
    out_specs += [pl.BlockSpec(s, lambda i: (0, 0)) for s in acc_outs]
    out_shape = [jax.ShapeDtypeStruct((nrows, w), dt) for w, dt in row_outs]
    out_shape += [jax.ShapeDtypeStruct(s, F32) for s in acc_outs]
    res = pl.pallas_call(
        body, name=name, grid=(nblk,), in_specs=in_specs, out_specs=out_specs, out_shape=out_shape,
        compiler_params=_params(),
    )(*args)
    return res


def _rms_f(x, g, n):
    r = lax.rsqrt(jnp.sum(x * x, axis=-1, keepdims=True) * (1.0 / n) + EPS)
    return x * r * g


def _rms_b(x, g, dy, n):
    r = lax.rsqrt(jnp.sum(x * x, axis=-1, keepdims=True) * (1.0 / n) + EPS)
    gx = dy * g
    dx = r * gx - x * (r * r * r * (jnp.sum(x * gx, axis=-1, keepdims=True) * (1.0 / n)))
    dg = jnp.sum(dy * (x * r), axis=0, keepdims=True)
    return dx, dg


def _rope_f(x, c, sa, sb):
    return x * c + pltpu.roll(x, LANES - 16, 1) * sa + pltpu.roll(x, 16, 1) * sb


def _rope_b(g, c, sa, sb):
    return g * c + pltpu.roll(g * sa, 16, 1) + pltpu.roll(g * sb, LANES - 16, 1)


def _gelu(x):
    c = math.sqrt(2.0 / math.pi)
    return 0.5 * x * (1.0 + jnp.tanh(c * (x + 0.044715 * (x * x * x))))


def _gelu_grad(x):
    c = math.sqrt(2.0 / math.pi)
    th = jnp.tanh(c * (x + 0.044715 * (x * x * x)))
    return 0.5 * (1.0 + th) + 0.5 * x * (1.0 - th * th) * (c * (1.0 + 3.0 * 0.044715 * (x * x)))


def _row_ids(bm):
    return lax.broadcasted_iota(jnp.int32, (bm, 1), 0)


def _shift_down(x, halo_ref, i, k):
    live = (i > 0).astype(F32)
    out = pltpu.roll(x, k, 0)
    row = _row_ids(8)
    first = out[:8]
    for r in range(k):
        e = (row == r).astype(F32)
        first = first * (1.0 - e) + e * (halo_ref[8 - k + r:8 - k + r + 1, :] * live)
    return jnp.concatenate([first, out[8:]], axis=0)


def _live_pairs(nq, nk, bq, bk, causal, key_major):
    pairs = [(i, j) for i in range(nq) for j in range(nk) if not causal or j * bk <= i * bq + bq - 1]
    if key_major:
        pairs.sort(key=lambda ij: (ij[1], ij[0]))
    return (jnp.asarray([p[0] for p in pairs], jnp.int32), jnp.asarray([p[1] for p in pairs], jnp.int32))


def _attn_fwd(name, qa, ka, vta, *, qoff, koff, voff, heads, causal, scale, bq, bk, side=None):
    lq, lk = qa.shape[0], ka.shape[0]
    bq, bk = min(bq, lq), min(bk, lk)
    nq, nk = lq // bq, lk // bk
    c2 = scale * LOG2E
    tab_i, tab_j = _live_pairs(nq, nk, bq, bk, causal, key_major=False)

    def body(ti, tj, q_ref, k_ref, vt_ref, o_ref, lse_ref, m_s, l_s, acc_s):
        t = pl.program_id(1)
        i, j = ti[t], tj[t]
        j_last = jnp.minimum(nk - 1, (i * bq + bq - 1) // bk) if causal else nk - 1

        @pl.when(j == 0)
        def _():
            m_s[...] = jnp.full_like(m_s, -1e30)
            l_s[...] = jnp.zeros_like(l_s)
            acc_s[...] = jnp.zeros_like(acc_s)

        def step(masked):
            st = _dot(k_ref[...], q_ref[...], _NT) * c2
            if masked:
                key = j * bk + lax.broadcasted_iota(jnp.int32, (bk, bq), 0)
                qry = i * bq + lax.broadcasted_iota(jnp.int32, (bk, bq), 1)
                st = jnp.where(key <= qry, st, -1e30)
            m_prev = m_s[...]
            m_new = jnp.maximum(m_prev, jnp.max(st, axis=0, keepdims=True))
            alpha = jnp.exp2(m_prev - m_new)
            pt = jnp.exp2(st - m_new)
            l_s[...] = alpha * l_s[...] + jnp.sum(pt, axis=0, keepdims=True)
            acc_s[...] = alpha * acc_s[...] + _dot(vt_ref[...], pt, _NN)
            m_s[...] = m_new

        if causal:
            full = j * bk + bk - 1 <= i * bq
            pl.when(full)(lambda: step(False))
            pl.when(jnp.logical_not(full))(lambda: step(True))
        else:
            step(False)

        @pl.when(j == j_last)
        def _():
            l = l_s[...]
            o_ref[...] = (acc_s[...] / l).T.astype(o_ref.dtype)
            lse_ref[0] = m_s[...] + jnp.log2(l)

    in_specs = [pl.BlockSpec((bq, LANES), lambda h, t, ti, tj: (ti[t], qoff + h)),
                pl.BlockSpec((bk, LANES), lambda h, t, ti, tj: (tj[t], koff + h)),
                pl.BlockSpec((LANES, bk), lambda h, t, ti, tj: (voff + h, tj[t]))]
    out_specs = [pl.BlockSpec((bq, LANES), lambda h, t, ti, tj: (ti[t], h)),
                 pl.BlockSpec((1, 1, bq), lambda h, t, ti, tj: (h, 0, ti[t]))]
    scratch = [pltpu.VMEM((1, bq), F32), pltpu.VMEM((1, bq), F32), pltpu.VMEM((LANES, bq), F32)]
    out_shape = [jax.ShapeDtypeStruct((lq, heads * LANES), BF16), jax.ShapeDtypeStruct((heads, 1, lq), F32)]
    (o, lse), rode = _pair_grid_call(body, name, heads, (tab_i, tab_j), [qa, ka, vta], in_specs, out_specs, out_shape,
                                     scratch, side)
    return o, lse, rode


_Side = collections.namedtuple("_Side", "srcs out_shapes nsem phases")


def _pair_grid_call(body, name, heads, tabs, ins, in_specs, out_specs, out_shape, scratch, side):
    npairs = int(tabs[0].shape[0])
    n_in, n_out = len(ins), len(out_shape)
    n = len(side.srcs) if side else 0

    def wrapped(*refs):
        pre = len(tabs)
        if not side:
            return body(*refs)
        s_refs = refs[pre + n_in:pre + n_in + n]
        o_refs = refs[pre + n_in + n + n_out:pre + n_in + 2 * n + n_out]
        send, recv, loc = refs[-3:]
        args = (s_refs, o_refs, (send, recv), loc)
        h, t = pl.program_id(0), pl.program_id(1)
        pl.when(jnp.logical_and(h == 0, t == 0))(lambda: side.phases[0](*args))
        if len(side.phases) == 3:
            pl.when(jnp.logical_and(h == heads // 2, t == 0))(lambda: side.phases[1](*args))
        body(*refs[:pre + n_in], *refs[pre + n_in + n:pre + n_in + n + n_out], *refs[pre + n_in + 2 * n + n_out:-3])
        pl.when(jnp.logical_and(h == heads - 1, t == npairs - 1))(lambda: side.phases[-1](*args))

    hbm = pl.BlockSpec(memory_space=pltpu.HBM)
    sems = [pltpu.SemaphoreType.DMA((side.nsem * n,)), pltpu.SemaphoreType.DMA((side.nsem * n,)),
            pltpu.SemaphoreType.DMA((n,))] if side else []
    grid_spec = pltpu.PrefetchScalarGridSpec(
        num_scalar_prefetch=len(tabs), grid=(heads, npairs), in_specs=in_specs + [hbm] * n,
        out_specs=out_specs + [hbm] * n, scratch_shapes=scratch + sems)
    res = pl.pallas_call(
        wrapped, name=name, grid_spec=grid_spec, out_shape=out_shape + (list(side.out_shapes) if side else []),
        compiler_params=_params(has_side_effects=True) if side else _params(),
    )(*tabs, *ins, *(side.srcs if side else []))
    return res[:n_out], res[n_out:]


def _attn_delta(name, oa, doa, *, heads, bq):
    lq = oa.shape[0]
    bq = min(bq, lq)

    def body(o_ref, do_ref, d_ref):
        prod = o_ref[...].astype(F32) * do_ref[...].astype(F32)
        hi = prod.astype(BF16)
        lo = (prod - hi.astype(F32)).astype(BF16)
        pick = (lax.broadcasted_iota(jnp.int32, (8, LANES), 0) == 0).astype(BF16)
        sums = _dot(pick, hi, _NT) + _dot(pick, lo, _NT)
        d_ref[0] = jnp.sum(sums, axis=0, keepdims=True)

    blk = pl.BlockSpec((bq, LANES), lambda h, i: (i, h))
    return pl.pallas_call(
        body, name=name, grid=(heads, lq // bq), in_specs=[blk, blk],
        out_specs=pl.BlockSpec((1, 1, bq), lambda h, i: (h, 0, i)),
        out_shape=jax.ShapeDtypeStruct((heads, 1, lq), F32), compiler_params=_params(),
    )(oa, doa)


def _attn_bwd(name, qa, ka, kta, va, doa, lsea, deltaa, *, qoff, koff, voff, heads, causal, scale, bq, bk, side=None):
    lq, lk = qa.shape[0], ka.shape[0]
    bq, bk = min(bq, lq), min(bk, lk)
    nq, nk = lq // bq, lk // bk
    c2 = scale * LOG2E
    tab_i, tab_j = _live_pairs(nq, nk, bq, bk, causal, key_major=True)

    def body(ti, tj, q_ref, k_ref, kt_ref, v_ref, do_ref, lse_ref, delta_ref, dqt_ref, dk_ref, dv_ref):
        t = pl.program_id(1)
        i, j = ti[t], tj[t]
        i_first = (j * bk) // bq if causal else 0

        @pl.when(t == 0)
        def _():
            dqt_ref[...] = jnp.zeros_like(dqt_ref)

        @pl.when(i == i_first)
        def _():
            dk_ref[...] = jnp.zeros_like(dk_ref)
            dv_ref[...] = jnp.zeros_like(dv_ref)

        def step(masked):
            q, k, v, do = q_ref[...], k_ref[...], v_ref[...], do_ref[...]
            st = _dot(k, q, _NT) * c2
            if masked:
                key = j * bk + lax.broadcasted_iota(jnp.int32, (bk, bq), 0)
                qry = i * bq + lax.broadcasted_iota(jnp.int32, (bk, bq), 1)
                st = jnp.where(key <= qry, st, -1e30)
            pt = jnp.exp2(st - lse_ref[0])
            dv_ref[...] += _dot(pt, do, _NN)
            dpt = _dot(v, do, _NT)
            dst = (pt * (dpt - delta_ref[0]) * scale).astype(BF16)
            dk_ref[...] += _dot(dst, q, _NN)
            dqt_ref[0, i] += _dot(kt_ref[...], dst, _NN)

        if causal:
            full = j * bk + bk - 1 <= i * bq
            pl.when(full)(lambda: step(False))
            pl.when(jnp.logical_not(full))(lambda: step(True))
        else:
            step(False)

    q_spec = lambda off: pl.BlockSpec((bq, LANES), lambda h, t, ti, tj: (ti[t], off + h))
    kv_spec = lambda off: pl.BlockSpec((bk, LANES), lambda h, t, ti, tj: (tj[t], off + h))
    row_spec = pl.BlockSpec((1, 1, bq), lambda h, t, ti, tj: (h, 0, ti[t]))
    in_specs = [q_spec(qoff), kv_spec(koff), pl.BlockSpec((LANES, bk), lambda h, t, ti, tj: (koff + h, tj[t])),
                kv_spec(voff), q_spec(0), row_spec, row_spec]
    out_specs = [pl.BlockSpec((1, nq, LANES, bq), lambda h, t, ti, tj: (h, 0, 0, 0)),
                 pl.BlockSpec((bk, LANES), lambda h, t, ti, tj: (tj[t], h)),
                 pl.BlockSpec((bk, LANES), lambda h, t, ti, tj: (tj[t], h))]
    out_shape = [jax.ShapeDtypeStruct((heads, nq, LANES, bq), F32), jax.ShapeDtypeStruct((lk, heads * LANES), F32),
                 jax.ShapeDtypeStruct((lk, heads * LANES), F32)]
    (dqt, dk, dv), rode = _pair_grid_call(body, name, heads, (tab_i, tab_j), [qa, ka, kta, va, doa, lsea, deltaa],
                                          in_specs, out_specs, out_shape, [], side)
    return dqt.transpose(1, 3, 0, 2).reshape(lq, heads * LANES), dk, dv, rode


S5_STRIP = SSM_N // 4
S5_CH = SSM_WIDTH // 4
S5_CHUNK = 512


def _scan_chunk(xr_ref, xi_ref, ar, ai, cre_s, cim_s, reverse, unroll=4):
    t = xr_ref.shape[0]
    ng = t // 8
    edge = 0 if reverse else 7
    row8 = lax.broadcasted_iota(jnp.int32, (8, 1), 0)

    def grp(g):
        return pl.ds(pl.multiple_of(g * 8, 8), 8)

    def group_scan(xr, xi):
        pr, pi = ar, ai
        for d in (1, 2, 4):
            own = ((row8 < 8 - d) if reverse else (row8 >= d)).astype(F32)
            shift = (8 - d) if reverse else d
            sr, si = pltpu.roll(xr, shift, 0) * own, pltpu.roll(xi, shift, 0) * own
            xr, xi = xr + pr * sr - pi * si, xi + pr * si + pi * sr
            pr, pi = pr * pr - pi * pi, 2.0 * pr * pi
        return xr, xi

    def local(g, _):
        xr, xi = group_scan(xr_ref[grp(g), :], xi_ref[grp(g), :])
        xr_ref[grp(g), :] = xr
        xi_ref[grp(g), :] = xi
        return 0

    lax.fori_loop(0, ng, local, 0, unroll=unroll)
    e = (row8 == 7 - edge).astype(F32)
    pw_r, pw_i = group_scan(e * ar, e * ai)
    e_out = (row8 == edge).astype(F32)
    a8_r = jnp.sum(pw_r * e_out, axis=0, keepdims=True)
    a8_i = jnp.sum(pw_i * e_out, axis=0, keepdims=True)
    cr, ci = cre_s[...], cim_s[...]
    for i in range(ng):
        g = ng - 1 - i if reverse else i
        rows = slice(g * 8, g * 8 + 8)
        lr, li = xr_ref[g * 8 + edge:g * 8 + edge + 1, :], xi_ref[g * 8 + edge:g * 8 + edge + 1, :]
        xr_ref[rows, :] = xr_ref[rows, :] + pw_r * cr - pw_i * ci
        xi_ref[rows, :] = xi_ref[rows, :] + pw_r * ci + pw_i * cr
        cr, ci = lr + a8_r * cr - a8_i * ci, li + a8_r * ci + a8_i * cr
    cre_s[...] = cr
    cim_s[...] = ci


def _s5_fwd(name, u, bb_re, bb_im, cc_re, cc_imn, a_re, a_im):
    l = u.shape[0]
    t = min(S5_CHUNK, l)

    def body(u_ref, bbr_ref, bbi_ref, ccr_ref, cci_ref, ar_ref, ai_ref, sre_ref, sim_ref, y_ref, cre_s, cim_s):
        @pl.when(pl.program_id(1) == 0)
        def _():
            cre_s[...] = jnp.zeros_like(cre_s)
            cim_s[...] = jnp.zeros_like(cim_s)

        uv = u_ref[...]
        sre_ref[...] = _dot(uv, bbr_ref[...], _NN)
        sim_ref[...] = _dot(uv, bbi_ref[...], _NN)
        _scan_chunk(sre_ref, sim_ref, ar_ref[...], ai_ref[...], cre_s, cim_s, reverse=False)
        y_ref[...] = _dot(sre_ref[...], ccr_ref[...], _NN) + _dot(sim_ref[...], cci_ref[...], _NN)

    rows_ch = pl.BlockSpec((t, S5_CH), lambda w, c: (c, w))
    rows_st = pl.BlockSpec((t, S5_STRIP), lambda w, c: (c, w))
    b_blk = pl.BlockSpec((S5_CH, S5_STRIP), lambda w, c: (w, w))
    c_blk = pl.BlockSpec((S5_STRIP, S5_CH), lambda w, c: (w, w))
    a_blk = pl.BlockSpec((1, S5_STRIP), lambda w, c: (0, w))
    return pl.pallas_call(
        body, name=name, grid=(SSM_N // S5_STRIP, l // t),
        in_specs=[rows_ch, b_blk, b_blk, c_blk, c_blk, a_blk, a_blk], out_specs=[rows_st, rows_st, rows_ch],
        out_shape=[jax.ShapeDtypeStruct((l, SSM_N), F32), jax.ShapeDtypeStruct((l, SSM_N), F32),
                   jax.ShapeDtypeStruct((l, SSM_WIDTH), F32)],
        scratch_shapes=[pltpu.VMEM((1, S5_STRIP), F32), pltpu.VMEM((1, S5_STRIP), F32)],
        compiler_params=_params(),
    )(u, bb_re, bb_im, cc_re, cc_imn, a_re, a_im)


def _s5_bwd(name, dy, du_skip, u, s_re, s_im, bb_re, bb_im, cc_re, cc_imn, a_re, a_im_neg):
    l = u.shape[0]
    t = min(S5_CHUNK, l)
    nc = l // t

    def body(dy_ref, skip_ref, u_ref, sre_ref, sim_ref, hre_ref, him_ref, bbr_ref, bbi_ref, ccr_ref, cci_ref, ar_ref,
             ai_ref, du_ref, dar_ref, dai_ref, dbr_ref, dbi_ref, dcr_ref, dci_ref, lr_s, li_s, cre_s, cim_s):
        c = pl.program_id(1)

        @pl.when(c == 0)
        def _():
            for r in (cre_s, cim_s, dar_ref, dai_ref, dbr_ref, dbi_ref, dcr_ref, dci_ref):
                r[...] = jnp.zeros_like(r)

        dyv, uv = dy_ref[...], u_ref[...]
        lr_s[...] = _dot(dyv, ccr_ref[...], _NT)
        li_s[...] = _dot(dyv, cci_ref[...], _NT)
        _scan_chunk(lr_s, li_s, ar_ref[...], ai_ref[...], cre_s, cim_s, reverse=True)
        lam_r, lam_i = lr_s[...], li_s[...]
        s_r, s_i = sre_ref[...], sim_ref[...]
        tc = nc - 1 - c
        sp_r, sp_i = _shift_down(s_r, hre_ref, tc, 1), _shift_down(s_i, him_ref, tc, 1)
        dar_ref[...] += jnp.sum(lam_r * sp_r + lam_i * sp_i, axis=0, keepdims=True)
        dai_ref[...] += jnp.sum(lam_i * sp_r - lam_r * sp_i, axis=0, keepdims=True)
        dbr_ref[0] += _dot(uv, lam_r, _TN)
        dbi_ref[0] += _dot(uv, lam_i, _TN)
        dcr_ref[0] += _dot(s_r, dyv, _TN)
        dci_ref[0] += _dot(s_i, dyv, _TN)
        du = _dot(lam_r, bbr_ref[...], _NT) + _dot(lam_i, bbi_ref[...], _NT) + skip_ref[...]
        du_ref[...] = du.astype(du_ref.dtype)

    rows_ch = pl.BlockSpec((t, S5_CH), lambda w, c: (nc - 1 - c, w))
    rows_st = pl.BlockSpec((t, S5_STRIP), lambda w, c: (nc - 1 - c, w))
    halo = pl.BlockSpec((8, S5_STRIP), lambda w, c: (jnp.maximum((nc - 1 - c) * (t // 8) - 1, 0), w))
    b_blk = pl.BlockSpec((S5_CH, S5_STRIP), lambda w, c: (w, w))
    c_blk = pl.BlockSpec((S5_STRIP, S5_CH), lambda w, c: (w, w))
    a_blk = pl.BlockSpec((1, S5_STRIP), lambda w, c: (0, w))
    nw = SSM_N // S5_STRIP
    return pl.pallas_call(
        body, name=name, grid=(nw, nc),
        in_specs=[rows_ch, rows_ch, rows_ch, rows_st, rows_st, halo, halo, b_blk, b_blk, c_blk, c_blk, a_blk, a_blk],
        out_specs=[rows_ch, a_blk, a_blk, pl.BlockSpec((1, S5_CH, S5_STRIP), lambda w, c: (w, 0, 0)),
                   pl.BlockSpec((1, S5_CH, S5_STRIP), lambda w, c: (w, 0, 0)),
                   pl.BlockSpec((1, S5_STRIP, S5_CH), lambda w, c: (w, 0, 0)),
                   pl.BlockSpec((1, S5_STRIP, S5_CH), lambda w, c: (w, 0, 0))],
        out_shape=[jax.ShapeDtypeStruct((l, SSM_WIDTH), BF16), jax.ShapeDtypeStruct((1, SSM_N), F32),
                   jax.ShapeDtypeStruct((1, SSM_N), F32), jax.ShapeDtypeStruct((nw, S5_CH, S5_STRIP), F32),
                   jax.ShapeDtypeStruct((nw, S5_CH, S5_STRIP), F32), jax.ShapeDtypeStruct((nw, S5_STRIP, S5_CH), F32),
                   jax.ShapeDtypeStruct((nw, S5_STRIP, S5_CH), F32)],
        scratch_shapes=[pltpu.VMEM((t, S5_STRIP), F32), pltpu.VMEM((t, S5_STRIP), F32),
                        pltpu.VMEM((1, S5_STRIP), F32), pltpu.VMEM((1, S5_STRIP), F32)],
        compiler_params=_params(),
    )(dy, du_skip, u, s_re, s_im, s_re, s_im, bb_re, bb_im, cc_re, cc_imn, a_re, a_im_neg)


def _disc_math(lr, li, ldt, br, bi):
    dt = jnp.exp(ldt)
    mag = jnp.exp(lr * dt)
    a_re, a_im = mag * jnp.cos(li * dt), mag * jnp.sin(li * dt)
    den = lr * lr + li * li
    e_re, e_im = a_re - 1.0, a_im
    f_re = (e_re * lr + e_im * li) / den
    f_im = (e_im * lr - e_re * li) / den
    return a_re, a_im, f_re * br - f_im * bi, f_re * bi + f_im * br


def _disc_fwd(lr, li, ldt, br, bi):
    def body(lr_ref, li_ref, ldt_ref, br_ref, bi_ref, are_ref, aim_ref, bbr_ref, bbi_ref):
        a_re, a_im, bb_re, bb_im = _disc_math(lr_ref[...], li_ref[...], ldt_ref[...], br_ref[...], bi_ref[...])
        are_ref[...] = a_re
        aim_ref[...] = a_im
        bbr_ref[...] = bb_re
        bbi_ref[...] = bb_im

    col = jax.ShapeDtypeStruct(lr.shape, F32)
    mat = jax.ShapeDtypeStruct(br.shape, F32)
    return pl.pallas_call(body, name="s5_disc_fwd", out_shape=[col, col, mat, mat], compiler_params=_params())(
        lr, li, ldt, br, bi)


def _disc_bwd(lr, li, ldt, br, bi, da_re, da_im, dbb_re, dbb_im):
    def body(lr_ref, li_ref, ldt_ref, br_ref, bi_ref, g0, g1, g2, g3, o0, o1, o2, o3, o4):
        _, vjp = jax.vjp(_disc_math, lr_ref[...], li_ref[...], ldt_ref[...], br_ref[...], bi_ref[...])
        grads = vjp((g0[...], g1[...], g2[...], g3[...]))
        for o, g in zip((o0, o1, o2, o3, o4), grads):
            o[...] = g

    col = jax.ShapeDtypeStruct(lr.shape, F32)
    mat = jax.ShapeDtypeStruct(br.shape, F32)
    return pl.pallas_call(body, name="s5_disc_bwd", out_shape=[col, col, col, mat, mat], compiler_params=_params())(
        lr, li, ldt, br, bi, da_re, da_im, dbb_re, dbb_im)


N_CHIP = 4


def _place():
    x, y, c = lax.axis_index("x"), lax.axis_index("y"), lax.axis_index("c")
    return (x, y, c), (x, y, 1 - c), [(1 - x, y), (x, 1 - y), (1 - x, 1 - y)]


def _lin(px, py, pc):
    return 4 * px + 2 * py + pc


def _remote(src, dst, sems, k, dev):
    return pltpu.make_async_remote_copy(src_ref=src, dst_ref=dst, send_sem=sems[0].at[k], recv_sem=sems[1].at[k],
                                        device_id=dev, device_id_type=pl.DeviceIdType.MESH)


def _hbm_call(body, name, srcs, out_shapes, nsem):
    n = len(srcs)
    hbm = pl.BlockSpec(memory_space=pltpu.HBM)

    def wrapped(*refs):
        body(refs[:n], refs[n:2 * n], (refs[2 * n], refs[2 * n + 1]), refs[2 * n + 2])

    return pl.pallas_call(
        wrapped, name=name, in_specs=[hbm] * n, out_specs=[hbm] * n, out_shape=out_shapes,
        scratch_shapes=[pltpu.SemaphoreType.DMA((nsem * n,)), pltpu.SemaphoreType.DMA((nsem * n,)),
                        pltpu.SemaphoreType.DMA((n,))],
        compiler_params=pltpu.CompilerParams(has_side_effects=True),
    )(*srcs)


GATHER_SEMS = 7
CHIP_SEMS = 3


def _gather_copies(s_refs, o_refs, sems, loc_sems):
    me, sib, chips = _place()
    c = me[2]
    out = []
    for g, (s_ref, o_ref) in enumerate(zip(s_refs, o_refs)):
        slot = lambda dev, o_ref=o_ref: o_ref.at[_lin(*dev)]
        k0 = GATHER_SEMS * g
        mine = pltpu.make_async_copy(s_ref, slot(me), loc_sems.at[g])
        first = [_remote(s_ref, slot(me), sems, k0, sib)]
        first += [_remote(s_ref, slot(me), sems, k0 + 1 + j, (*chip, c)) for j, chip in enumerate(chips)]
        passed = [_remote(slot((*chip, c)), slot((*chip, c)), sems, k0 + 4 + j, sib) for j, chip in enumerate(chips)]
        arrive = [_remote(s_ref, slot(sib), sems, k0, me)]
        arrive += [_remote(s_ref, slot((*chip, c)), sems, k0 + 1 + j, me) for j, chip in enumerate(chips)]
        arrive += [_remote(s_ref, slot((*chip, 1 - c)), sems, k0 + 4 + j, me) for j, chip in enumerate(chips)]
        out.append((mine, first, passed, arrive))
    return out


def _gather_start(*refs):
    for mine, first, _, _ in _gather_copies(*refs):
        mine.start()
        for cp in first:
            cp.start()


def _gather_relay(*refs):
    for _, _, passed, arrive in _gather_copies(*refs):
        for j, cp in enumerate(passed):
            arrive[1 + j].wait_recv()
            cp.start()


def _gather_finish(*refs):
    for mine, first, passed, arrive in _gather_copies(*refs):
        arrive[0].wait_recv()
        for cp in arrive[4:]:
            cp.wait_recv()
        for cp in first + passed:
            cp.wait_send()
        mine.wait()


def _gather_shapes(srcs):
    return [jax.ShapeDtypeStruct((N_DEV,) + s.shape, s.dtype) for s in srcs]


def _gather_all(name, srcs):
    def body(*refs):
        _gather_start(*refs)
        _gather_relay(*refs)
        _gather_finish(*refs)

    return _hbm_call(body, name, srcs, _gather_shapes(srcs), GATHER_SEMS)


def _pair_exchange(name, sends):
    def body(s_refs, o_refs, sems, loc_sems):
        me, sib, _ = _place()
        c = me[2]
        copies = [_remote(s_ref.at[2 * q + (1 - c)], o_ref.at[q], sems, N_CHIP * g + q, sib)
                  for g, (s_ref, o_ref) in enumerate(zip(s_refs, o_refs)) for q in range(N_CHIP)]
        for cp in copies:
            cp.start()
        for cp in copies:
            cp.wait()

    return _hbm_call(body, name, sends, [jax.ShapeDtypeStruct((N_CHIP,) + s.shape[1:], s.dtype) for s in sends],
                     N_CHIP)


def _row_block(r, cap):
    best = 8
    for m in range(8, min(r, cap) + 1, 8):
        if r % m == 0:
            best = m
    return best


def _pair_sum(name, send, got):
    _, r, cols = send.shape
    bm = _row_block(r, max(8, 256 * 1024 // cols))
    core = lax.axis_index("c").astype(jnp.int32).reshape(1)

    def body(core_ref, a_ref, b_ref, o_ref):
        o_ref[...] = a_ref[...] + b_ref[...]

    grid_spec = pltpu.PrefetchScalarGridSpec(
        num_scalar_prefetch=1, grid=(N_CHIP, r // bm),
        in_specs=[pl.BlockSpec((1, bm, cols), lambda q, i, cr: (2 * q + cr[0], i, 0)),
                  pl.BlockSpec((1, bm, cols), lambda q, i, cr: (q, i, 0))],
        out_specs=pl.BlockSpec((1, bm, cols), lambda q, i, cr: (q, i, 0)))
    return pl.pallas_call(body, name=name, grid_spec=grid_spec,
                          out_shape=jax.ShapeDtypeStruct((N_CHIP, r, cols), F32), compiler_params=_params())(
        core, send, got)


def _chip_exchange(name, parts):
    def body(*refs):
        _chip_start(*refs)
        _chip_finish(*refs)

    return _hbm_call(body, name, parts, _chip_shapes(parts), CHIP_SEMS)


def _chip_copies(p_refs, o_refs, sems, loc_sems):
    me, _, chips = _place()
    c = me[2]
    chip_id = lambda chip: 2 * chip[0] + chip[1]
    my_chip = chip_id(me)
    out = []
    for g, (p_ref, o_ref) in enumerate(zip(p_refs, o_refs)):
        mine = pltpu.make_async_copy(p_ref.at[my_chip], o_ref.at[my_chip], loc_sems.at[g])
        send = [_remote(p_ref.at[chip_id(chip)], o_ref.at[my_chip], sems, CHIP_SEMS * g + j, (*chip, c))
                for j, chip in enumerate(chips)]
        arrive = [_remote(p_ref.at[my_chip], o_ref.at[chip_id(chip)], sems, CHIP_SEMS * g + j, me)
                  for j, chip in enumerate(chips)]
        out.append((mine, send, arrive))
    return out


def _chip_start(*refs):
    for mine, send, _ in _chip_copies(*refs):
        mine.start()
        for cp in send:
            cp.start()


def _chip_finish(*refs):
    for mine, send, arrive in _chip_copies(*refs):
        for cp in arrive:
            cp.wait_recv()
        for cp in send:
            cp.wait_send()
        mine.wait()


def _chip_shapes(parts):
    return [jax.ShapeDtypeStruct(p.shape, p.dtype) for p in parts]


def _adamw(name, rcv, w, m, v):
    r, c = w.shape
    nslot = rcv.shape[0]
    bm = _row_block(r, max(8, 256 * 1024 // c))

    def body(rcv_ref, w_ref, m_ref, v_ref, g_ref, d_ref, m2_ref, v2_ref):
        g = rcv_ref[0]
        for s in range(1, nslot):
            g = g + rcv_ref[s]
        m2 = ADAM_B1 * m_ref[...] + (1.0 - ADAM_B1) * g
        v2 = ADAM_B2 * v_ref[...] + (1.0 - ADAM_B2) * (g * g)
        m_hat = m2 / (1.0 - ADAM_B1 ** ADAM_STEP)
        v_hat = v2 / (1.0 - ADAM_B2 ** ADAM_STEP)
        g_ref[...] = g
        d_ref[...] = -ADAM_LR * (m_hat / (jnp.sqrt(v_hat) + ADAM_EPS) + ADAM_WD * w_ref[...])
        m2_ref[...] = m2
        v2_ref[...] = v2

    blk = pl.BlockSpec((bm, c), lambda i: (i, 0))
    out = jax.ShapeDtypeStruct((r, c), F32)
    return pl.pallas_call(
        body, name=name, grid=(r // bm,),
        in_specs=[pl.BlockSpec((nslot, bm, c), lambda i: (0, i, 0)), blk, blk, blk], out_specs=[blk] * 4,
        out_shape=[out] * 4, compiler_params=_params(),
    )(rcv, w, m, v)


IN_SHARD = IN_WIDTH // N_DEV
UP_SHARD = 2 * D_FF // N_DEV
GROUPS = (
    ("g128a", LANES, (("w_q_b", Q_LORA, D_QK), ("w_kv_b", KV_LORA, D_NOPE + D_V))),
    ("g128b", LANES, (("w_o_mla", MLA_HEADS * D_V, LANES), ("w_o_ssm", SSM_WIDTH, LANES),
                      ("w_o_cross", X_WIDTH, LANES))),
    ("g512", SSM_WIDTH, (("w_glu", SSM_WIDTH // N_DEV, SSM_WIDTH),)),
    ("g1024", D_MODEL, (("w_mem_kv", D_MODEL // N_DEV, D_MODEL), ("w_out", D_MODEL // N_DEV, D_MODEL),
                        ("w_down", D_FF // N_DEV, D_MODEL))),
    ("g640", 640, (("w_in", D_MODEL, IN_SHARD),)),
    ("g768", 768, (("w_up", D_MODEL, UP_SHARD),)),
    ("gconv", 768, (("conv_w", 3, UP_SHARD),)),
)
EARLY = ("g640", "g128a")
EARLY_GROUPS = tuple(g for g in GROUPS if g[0] in EARLY)
LATE_GROUPS = tuple(g for g in GROUPS if g[0] not in EARLY)
REP_W = 1024


def _rows8(a):
    return -(-a // 8) * 8


def _group_rows(params):
    off, r = {}, 0
    for n, a, _ in params:
        off[n] = r
        r += _rows8(a)
    return off


def _group_local(width, params, vals, li, dtype):
    return jnp.concatenate([jnp.pad(vals[n][li].astype(dtype), ((0, _rows8(a) - a), (0, width - b)))
                            for n, a, b in params], axis=0)


def _rep_rows(flat):
    n = flat.shape[-1]
    per = REP_W * 64
    tot = -(-n // per) * per
    flat = jnp.pad(flat, [(0, 0)] * (flat.ndim - 1) + [(0, tot - n)])
    return flat.reshape(flat.shape[:-1] + (tot // REP_W, REP_W))


IN_SEGS = (
    (0, Q_LORA + KV_LORA, "wqkv", 0),
    (Q_LORA + KV_LORA, Q_LORA + KV_LORA + D_ROPE, "wqkv", Q_LORA + KV_LORA + KR_LO),
    (Q_LORA + KV_LORA + D_ROPE, Q_LORA + KV_LORA + D_ROPE + SSM_WIDTH, "w_u", 0),
    (Q_LORA + KV_LORA + D_ROPE + SSM_WIDTH, Q_LORA + KV_LORA + D_ROPE + SSM_WIDTH + X_WIDTH, "w_xq", 0),
    (Q_LORA + KV_LORA + D_ROPE + SSM_WIDTH + X_WIDTH, IN_WIDTH, "w_g", 0),
)
IN_PARTS = (("wqkv", QKV_W), ("w_u", SSM_WIDTH), ("w_xq", X_WIDTH), ("w_g", 3 * D_MODEL))


def _in_pieces():
    out = []
    for d in range(N_DEV):
        for lo, hi, part, dst in IN_SEGS:
            s, e = max(lo, d * IN_SHARD), min(hi, (d + 1) * IN_SHARD)
            if s < e:
                out.append((d, s - d * IN_SHARD, e - s, [p for p, _ in IN_PARTS].index(part), dst + s - lo))
    return out


def _unpack_w_in(gathered, row0, bm=256):
    def body(x_ref, *o_refs):
        o_refs[0][...] = jnp.zeros_like(o_refs[0])
        for d, src, n, part, dst in _in_pieces():
            o_refs[part][:, dst:dst + n] = x_ref[d, :, src:src + n]

    return pl.pallas_call(
        body, name="unpack_w_in", grid=(D_MODEL // bm,),
        in_specs=[pl.BlockSpec((N_DEV, bm, gathered.shape[2]), lambda i: (0, row0 // bm + i, 0))],
        out_specs=[pl.BlockSpec((bm, w), lambda i: (i, 0)) for _, w in IN_PARTS],
        out_shape=[jax.ShapeDtypeStruct((D_MODEL, w), gathered.dtype) for _, w in IN_PARTS], compiler_params=_params(),
    )(gathered)


def _pack_w_in(parts, bm=256):
    def body(*refs):
        o_ref = refs[-1]
        o_ref[...] = jnp.zeros_like(o_ref)
        for d, src, n, part, dst in _in_pieces():
            o_ref[d, :, src:src + n] = refs[part][:, dst:dst + n]

    return pl.pallas_call(
        body, name="pack_w_in", grid=(D_MODEL // bm,),
        in_specs=[pl.BlockSpec((bm, w), lambda i: (i, 0)) for _, w in IN_PARTS],
        out_specs=pl.BlockSpec((N_DEV, bm, 640), lambda i: (0, i, 0)),
        out_shape=jax.ShapeDtypeStruct((N_DEV, D_MODEL, 640), F32), compiler_params=_params(),
    )(*parts)


def _unpack_w_up(gathered, row0, bm=256):
    def body(x_ref, o_ref):
        for d in range(N_DEV):
            o_ref[:, d * UP_SHARD:(d + 1) * UP_SHARD] = x_ref[d, :, :UP_SHARD]

    return pl.pallas_call(
        body, name="unpack_w_up", grid=(D_MODEL // bm,),
        in_specs=[pl.BlockSpec((N_DEV, bm, gathered.shape[2]), lambda i: (0, row0 // bm + i, 0))],
        out_specs=pl.BlockSpec((bm, 2 * D_FF), lambda i: (i, 0)),
        out_shape=jax.ShapeDtypeStruct((D_MODEL, 2 * D_FF), gathered.dtype), compiler_params=_params(),
    )(gathered)


def _pack_w_up(dw_g, dw_v, bm=256):
    half = N_DEV // 2

    def body(g_ref, v_ref, o_ref):
        o_ref[...] = jnp.zeros_like(o_ref)
        for d in range(N_DEV):
            src = g_ref if d < half else v_ref
            c0 = (d % half) * UP_SHARD
            o_ref[d, :, :UP_SHARD] = src[:, c0:c0 + UP_SHARD]

    blk = pl.BlockSpec((bm, D_FF), lambda i: (i, 0))
    return pl.pallas_call(
        body, name="pack_w_up", grid=(D_MODEL // bm,), in_specs=[blk, blk],
        out_specs=pl.BlockSpec((N_DEV, bm, 768), lambda i: (0, i, 0)),
        out_shape=jax.ShapeDtypeStruct((N_DEV, D_MODEL, 768), F32), compiler_params=_params(),
    )(dw_g, dw_v)


def _cols_to_rows(full):
    a, nb = full.shape
    return full.reshape(a, N_DEV, nb // N_DEV).transpose(1, 0, 2)


def _rows_to_cols(blocks):
    n, a, b = blocks.shape
    return blocks.transpose(1, 0, 2).reshape(a, n * b)


def _block_diag_in(bb):
    b3 = bb.reshape(SSM_GROUPS, SSM_STATE, SSM_GROUP_CH).transpose(0, 2, 1)
    eye = jnp.eye(SSM_GROUPS, dtype=bb.dtype)
    return (b3[:, :, None, :] * eye[:, None, :, None]).reshape(SSM_WIDTH, SSM_N)


def _block_diag_out(cc):
    c3 = cc.transpose(0, 2, 1)
    eye = jnp.eye(SSM_GROUPS, dtype=cc.dtype)
    return (c3[:, :, None, :] * eye[:, None, :, None]).reshape(SSM_N, SSM_WIDTH)


def _diag_blocks(mats, rows_per, cols_per):
    nw = mats.shape[0]
    per = SSM_GROUPS // nw
    m5 = mats.reshape(nw, per, rows_per, per, cols_per)
    eye = jnp.eye(per, dtype=mats.dtype)
    return jnp.sum(m5 * eye[None, :, None, :, None], axis=3).reshape(SSM_GROUPS, rows_per, cols_per)


def _blk(gath, offs, grp, n, a):
    r0 = offs[grp][n]
    return gath[grp][:, r0:r0 + a, :]


def _early_weights(gath, offs):
    o = dict(zip([p for p, _ in IN_PARTS], _unpack_w_in(gath["g640"], offs["g640"]["w_in"])))
    o["wqb"] = _rows_to_cols(_blk(gath, offs, "g128a", "w_q_b", Q_LORA))
    wkv = _rows_to_cols(_blk(gath, offs, "g128a", "w_kv_b", KV_LORA)).reshape(KV_LORA, MLA_HEADS, D_NOPE + D_V)
    o["wk"] = jnp.pad(wkv[:, :, :D_NOPE], ((0, 0), (0, 0), (0, LANES - D_NOPE))).reshape(KV_LORA, MLA_HEADS * LANES)
    o["wv"] = jnp.pad(wkv[:, :, D_NOPE:], ((0, 0), (0, 0), (0, LANES - D_V))).reshape(KV_LORA, MLA_HEADS * LANES)
    o["wv_t"] = o["wv"].T
    return o


def _late_weights(gath, offs):
    def blk(grp, n, a):
        return _blk(gath, offs, grp, n, a)

    o = {}
    wo = _rows_to_cols(blk("g128b", "w_o_mla", MLA_HEADS * D_V)).reshape(MLA_HEADS, D_V, D_MODEL)
    o["wo_mla"] = jnp.pad(wo, ((0, 0), (0, LANES - D_V), (0, 0))).reshape(MLA_HEADS * LANES, D_MODEL)
    o["w_o_ssm"] = _rows_to_cols(blk("g128b", "w_o_ssm", SSM_WIDTH))
    o["w_o_cross"] = _rows_to_cols(blk("g128b", "w_o_cross", X_WIDTH))
    o["w_glu"] = blk("g512", "w_glu", SSM_WIDTH // N_DEV).reshape(SSM_WIDTH, SSM_WIDTH)
    o["w_mem_kv"] = blk("g1024", "w_mem_kv", D_MODEL // N_DEV).reshape(D_MODEL, 2 * X_WIDTH)
    o["w_out"] = blk("g1024", "w_out", D_MODEL // N_DEV).reshape(D_MODEL, D_MODEL)
    o["w_down"] = blk("g1024", "w_down", D_FF // N_DEV).reshape(D_FF, D_MODEL)
    o["w_up"] = _unpack_w_up(gath["g768"], offs["g768"]["w_up"])
    o["w_up_g"], o["w_up_v"] = o["w_up"][:, :D_FF], o["w_up"][:, D_FF:]
    o["conv_w"] = _rows_to_cols(blk("gconv", "conv_w", 3)[:, :, :UP_SHARD])
    return o


def _row(v):
    return v.reshape(1, -1).astype(F32)


def _pad_lanes(v, n=LANES):
    return jnp.pad(v, (0, n - v.shape[0])).reshape(1, n).astype(F32)


def _layer_fwd(x, mem, tabs, w, p, side, late):
    l = x.shape[0]
    rope_c, rope_sa, rope_sb = tabs
    s = {"x": x}
    g_mix, g_qa, g_kva = _row(p["norm_mix_g"]), _row(p["q_a_norm_g"]), _row(p["kv_a_norm_g"])
    g_q, g_k = _pad_lanes(p["q_norm_g"]), _pad_lanes(p["k_norm_g"])

    (h,) = _rowwise("rms_mix", lambda i, r, c: (_rms_f(r[0][...], c[0][...], D_MODEL),), l, 512,
                    [(x, D_MODEL, 0, "row")], [g_mix], [(D_MODEL, BF16)])
    pqkv = _mm("proj_qkv", [(h, w["wqkv"])])
    u = _mm("proj_u", [(h, w["w_u"])])
    xq = _mm("proj_xq", [(h, w["w_xq"])])
    gl = _mm("proj_gate", [(h, w["w_g"])], bm=1024, bn_cap=1024)
    s.update(h=h, pqkv=pqkv, u=u, xq=xq, gl=gl)

    def prep_a(i, r, c):
        return (_rms_f(r[0][:, :Q_LORA], c[0][...], Q_LORA),
                _rms_f(r[0][:, Q_LORA:Q_LORA + KV_LORA], c[1][...], KV_LORA))

    nq, nkv = _rowwise("mla_prep_a", prep_a, l, 512, [(pqkv, QKV_W, 0, "row")], [g_qa, g_kva],
                       [(Q_LORA, BF16), (KV_LORA, BF16)])
    q_raw = _mm("mla_q_b", [(nq, w["wqb"])], bn_cap=1024)
    k_raw = _mm("mla_k_b", [(nkv, w["wk"])], bn_cap=1024)
    v_mla = _mm("mla_v_b", [(nkv, w["wv"])], out_dtype=BF16, bn_cap=1024)
    vt_mla = _mm("mla_vt_b", [(w["wv_t"], nkv)], trans_b=True, out_dtype=BF16, bn_cap=1024)

    def prep_b(i, r, c):
        q_ref, k_ref, kr_ref, c_ref, sa_ref, sb_ref = r
        rc, sa, sb, kr = c_ref[...], sa_ref[...], sb_ref[...], kr_ref[...]
        qs, ks = [], []
        for hd in range(MLA_HEADS):
            cols = slice(hd * LANES, (hd + 1) * LANES)
            qs.append(_rope_f(_rms_f(q_ref[:, cols], c[0][...], D_QK), rc, sa, sb))
            ks.append(_rope_f(_rms_f(k_ref[:, cols] + kr, c[1][...], D_QK), rc, sa, sb))
        return jnp.concatenate(qs, axis=1), jnp.concatenate(ks, axis=1)

    hw = MLA_HEADS * LANES
    kr_blk = (Q_LORA + KV_LORA) // LANES
    tab_ins = [(rope_c, LANES, 0, "row"), (rope_sa, LANES, 0, "row"), (rope_sb, LANES, 0, "row")]
    q, k = _rowwise("mla_prep_b", prep_b, l, 256,
                    [(q_raw, hw, 0, "row"), (k_raw, hw, 0, "row"), (pqkv, LANES, kr_blk, "row")] + tab_ins,
                    [g_q, g_k], [(hw, BF16), (hw, BF16)])
    o_a, lse_a, rode = _attn_fwd("mla_attn_fwd", q, k, vt_mla, qoff=0, koff=0, voff=0, heads=MLA_HEADS, causal=True,
                                 scale=D_QK ** -0.5, bq=1024, bk=1024, side=side)
    w_late, rode = late(rode)
    w = {**w, **w_late}
    ya = _mm("mla_o", [(o_a, w["wo_mla"])], bn_cap=1024)
    s.update(nq=nq, nkv=nkv, q_raw=q_raw, k_raw=k_raw, v_mla=v_mla, q=q, k=k, o_a=o_a, lse_a=lse_a, ya=ya)

    lr = p["ssm_lambda_re"].reshape(SSM_N, 1)
    li = p["ssm_lambda_im"].reshape(SSM_N, 1)
    ldt = jnp.repeat(p["ssm_log_dt"], SSM_STATE).reshape(SSM_N, 1)
    br = p["ssm_b_re"].reshape(SSM_N, SSM_GROUP_CH)
    bi = p["ssm_b_im"].reshape(SSM_N, SSM_GROUP_CH)
    a_re, a_im, bb_re, bb_im = _disc_fwd(lr, li, ldt, br, bi)
    bb_re_d, bb_im_d = _block_diag_in(bb_re).astype(BF16), _block_diag_in(bb_im).astype(BF16)
    cc_re_d = _block_diag_out(p["ssm_c_re"]).astype(BF16)
    cc_imn_d = _block_diag_out(-p["ssm_c_im"]).astype(BF16)
    a_re_row, a_im_row = a_re.reshape(1, SSM_N), a_im.reshape(1, SSM_N)
    d_row = _row(p["ssm_d"])
    b_glu = _row(p["b_glu"])
    s_re, s_im, ypre = _s5_fwd("s5_fwd", u, bb_re_d, bb_im_d, cc_re_d, cc_imn_d, a_re_row, a_im_row)

    def ssm_y(i, r, c):
        return (_gelu(r[0][...] + c[0][...] * r[1][...]),)

    (y_b,) = _rowwise("s5_gelu", ssm_y, l, 512, [(ypre, SSM_WIDTH, 0, "row"), (u, SSM_WIDTH, 0, "row")], [d_row],
                      [(SSM_WIDTH, BF16)])
    z = _mm("s5_glu", [(y_b, w["w_glu"])])

    def ssm_out(i, r, c):
        y = _gelu(r[0][...] + c[0][...] * r[1][...])
        return (y * jax.nn.sigmoid(r[2][...] + c[1][...]),)

    (out_b,) = _rowwise("s5_glu_out", ssm_out, l, 512,
                        [(ypre, SSM_WIDTH, 0, "row"), (u, SSM_WIDTH, 0, "row"), (z, SSM_WIDTH, 0, "row")],
                        [d_row, b_glu], [(SSM_WIDTH, BF16)])
    yb = _mm("s5_o", [(out_b, w["w_o_ssm"])], bn_cap=1024)
    s.update(disc=(lr, li, ldt, br, bi), a_rows=(a_re_row, a_im_row), bb_d=(bb_re_d, bb_im_d),
             cc_d=(cc_re_d, cc_imn_d), s_re=s_re, s_im=s_im, ypre=ypre, y_b=y_b, z=z, out_b=out_b, yb=yb)

    g_mem, g_xq, g_xk = _row(p["mem_norm_g"]), _row(p["xq_norm_g"]), _row(p["xk_norm_g"])
    ml = mem.shape[0]
    (memn,) = _rowwise("rms_mem", lambda i, r, c: (_rms_f(r[0][...], c[0][...], D_MODEL),), ml, 256,
                       [(mem, D_MODEL, 0, "row")], [g_mem], [(D_MODEL, BF16)])
    kvm = _mm("cross_kv", [(memn, w["w_mem_kv"])], bn_cap=1024)

    def headnorm(i, r, c):
        return (jnp.concatenate([_rms_f(r[0][:, hd * LANES:(hd + 1) * LANES], c[0][...], X_HEAD_DIM)
                                 for hd in range(X_HEADS)], axis=1),)

    (xk,) = _rowwise("cross_k_norm", headnorm, ml, 256, [(kvm, X_WIDTH, 0, "row")], [g_xk], [(X_WIDTH, BF16)])
    (xqn,) = _rowwise("cross_q_norm", headnorm, l, 512, [(xq, X_WIDTH, 0, "row")], [g_xq], [(X_WIDTH, BF16)])
    xvt = kvm[:, X_WIDTH:].T.astype(BF16)
    o_c, lse_c, _ = _attn_fwd("cross_attn_fwd", xqn, xk, xvt, qoff=0, koff=0, voff=0, heads=X_HEADS,
                              causal=False, scale=X_HEAD_DIM ** -0.5, bq=1024, bk=256)
    yc = _mm("cross_o", [(o_c, w["w_o_cross"])], bn_cap=1024)
    s.update(memn=memn, kvm=kvm, xk=xk, xqn=xqn, o_c=o_c, lse_c=lse_c, yc=yc)

    b_gate = _row(p["b_gate"])

    def merge(i, r, c):
        acc = None
        for br_ in range(3):
            g = jax.nn.sigmoid(r[br_][...] + c[0][:, br_ * D_MODEL:(br_ + 1) * D_MODEL])
            t = g * r[3 + br_][...]
            acc = t if acc is None else acc + t
        return (acc,)

    gate_ins = [(gl, D_MODEL, b_, "row") for b_ in range(3)]
    (merged,) = _rowwise("merge", merge, l, 256,
                         gate_ins + [(ya, D_MODEL, 0, "row"), (yb, D_MODEL, 0, "row"), (yc, D_MODEL, 0, "row")],
                         [b_gate], [(D_MODEL, BF16)])
    x1 = _mm("mix_out", [(merged, w["w_out"])], add=x, bn_cap=1024)
    s.update(merged=merged, x1=x1)

    g_ffn = _row(p["norm_ffn_g"])
    (h2,) = _rowwise("rms_ffn", lambda i, r, c: (_rms_f(r[0][...], c[0][...], D_MODEL),), l, 512,
                     [(x1, D_MODEL, 0, "row")], [g_ffn], [(D_MODEL, BF16)])
    up = _mm("ffn_up", [(h2, w["w_up"])], bm=1024, bn_cap=1408)
    conv_w = w["conv_w"]
    conv_b = _row(p["conv_b"])

    def conv_glu(i, r, c):
        cg = _conv(r[0], r[2], i, c[0], c[1], 0)
        cv = _conv(r[1], r[3], i, c[0], c[1], D_FF)
        return (cg * jax.nn.sigmoid(cg) * cv,)

    up_ins = [(up, D_FF, 0, "row"), (up, D_FF, 1, "row"), (up, D_FF, 0, "prev"), (up, D_FF, 1, "prev")]
    (act,) = _rowwise("ffn_conv_glu", conv_glu, l, 256, up_ins, [conv_w, conv_b], [(D_FF, BF16)])
    x2 = _mm("ffn_down", [(act, w["w_down"])], add=x1, bm=1024, bn_cap=1024)
    s.update(h2=h2, up=up, act=act, conv_w=conv_w, conv_b=conv_b)
    return x2, s, w, rode


def _conv(x_ref, halo_ref, i, w_ref, b_ref, col0):
    x = x_ref[...]
    cols = slice(col0, col0 + D_FF)
    return (w_ref[0:1, cols] * _shift_down(x, halo_ref, i, 2) + w_ref[1:2, cols] * _shift_down(x, halo_ref, i, 1)
            + w_ref[2:3, cols] * x + b_ref[:, cols])


def _layer_bwd(dx2, dx2_b, s, mem, tabs, w, p, make_side):
    l = dx2.shape[0]
    rope_c, rope_sa, rope_sb = tabs
    x, x1 = s["x"], s["x1"]
    g = {}

    dact = _mm("ffn_down_dx", [(dx2_b, w["w_down"])], trans_b=True, bn_cap=1408)
    g["w_down"] = _mm_tn("ffn_down_dw", s["act"], dx2_b, bm_cap=1408).reshape(N_DEV, D_FF // N_DEV, D_MODEL)
    up = s["up"]
    nblk_c = l // min(256, l)

    bm_c = min(256, l)

    def conv_bwd(i, r, c):
        da_ref, xg_ref, xv_ref, hg_ref, hv_ref, dan_ref, ng_ref, nv_ref = r
        row8 = _row_ids(8)
        live_next = (i < nblk_c - 1).astype(F32)

        def conv_rows(x, xm1, xm2, cols):
            return c[0][0:1, cols] * xm2 + c[0][1:2, cols] * xm1 + c[0][2:3, cols] * x + c[1][:, cols]

        def tail_shift(x8, x_ref, k):
            out = pltpu.roll(x8, k, 0)
            for q in range(k):
                e = (row8 == q).astype(F32)
                out = out * (1.0 - e) + e * x_ref[bm_c - k + q:bm_c - k + q + 1, :]
            return out

        def glu_grads(da, cg, cv):
            sig = jax.nn.sigmoid(cg)
            return da * cv * (sig * (1.0 + cg * (1.0 - sig))), da * (cg * sig)

        def up_shift(d, d8, k):
            out = pltpu.roll(d, bm_c - k, 0)
            last = out[bm_c - 8:]
            for q in range(k):
                e = (row8 == 8 - k + q).astype(F32)
                nxt = jnp.sum(d8 * (row8 == q).astype(F32), axis=0, keepdims=True)
                last = last * (1.0 - e) + e * nxt
            return jnp.concatenate([out[:bm_c - 8], last], axis=0)

        halves = []
        for half, (x_ref, h_ref, n_ref) in enumerate(((xg_ref, hg_ref, ng_ref), (xv_ref, hv_ref, nv_ref))):
            cols = slice(half * D_FF, (half + 1) * D_FF)
            x, x8 = x_ref[...], n_ref[...]
            xm1, xm2 = _shift_down(x, h_ref, i, 1), _shift_down(x, h_ref, i, 2)
            halves.append((conv_rows(x, xm1, xm2, cols), x, xm1, xm2,
                           conv_rows(x8, tail_shift(x8, x_ref, 1), tail_shift(x8, x_ref, 2), cols)))
        (cg, xg, xg1, xg2, cg8), (cvv, xv, xv1, xv2, cv8) = halves
        dcg, dcv = glu_grads(da_ref[...], cg, cvv)
        dcg8, dcv8 = glu_grads(dan_ref[...] * live_next, cg8, cv8)
        outs, accs = [], []
        for half, (d, d8, (x0, xa, xb)) in enumerate(((dcg, dcg8, (xg, xg1, xg2)), (dcv, dcv8, (xv, xv1, xv2)))):
            cols = slice(half * D_FF, (half + 1) * D_FF)
            accs += [jnp.sum(d, axis=0, keepdims=True), jnp.sum(d * xb, axis=0, keepdims=True),
                     jnp.sum(d * xa, axis=0, keepdims=True), jnp.sum(d * x0, axis=0, keepdims=True)]
            outs.append(c[0][2:3, cols] * d + c[0][1:2, cols] * up_shift(d, d8, 1) + c[0][0:1, cols] * up_shift(d, d8, 2))
        return (*outs, *accs)

    conv_ins = [(dact, D_FF, 0, "row"), (up, D_FF, 0, "row"), (up, D_FF, 1, "row"), (up, D_FF, 0, "prev"),
                (up, D_FF, 1, "prev"), (dact, D_FF, 0, "next"), (up, D_FF, 0, "next"), (up, D_FF, 1, "next")]
    res = _rowwise("ffn_conv_glu_bwd", conv_bwd, l, bm_c, conv_ins, [s["conv_w"], s["conv_b"]],
                   [(D_FF, BF16), (D_FF, BF16)], [(1, D_FF)] * 8)
    dup_g, dup_v = res[0], res[1]
    db_g, dw0_g, dw1_g, dw2_g, db_v, dw0_v, dw1_v, dw2_v = res[2:]
    g["conv_b"] = jnp.concatenate([db_g, db_v], axis=1)[0]
    g["conv_w"] = _cols_to_rows(jnp.concatenate(
        [jnp.concatenate([dw0_g, dw0_v], axis=1), jnp.concatenate([dw1_g, dw1_v], axis=1),
         jnp.concatenate([dw2_g, dw2_v], axis=1)], axis=0))

    dh2 = _mm("ffn_up_dx", [(dup_g, w["w_up_g"]), (dup_v, w["w_up_v"])], trans_b=True, bn_cap=1024)
    g["w_up"] = _pack_w_up(_mm_tn("ffn_up_dw_g", s["h2"], dup_g), _mm_tn("ffn_up_dw_v", s["h2"], dup_v))

    def rms_bwd_res(i, r, c):
        dx, dg = _rms_b(r[0][...], c[0][...], r[1][...], D_MODEL)
        dx = dx + r[2][...]
        return (dx, dx, dg)

    dx1, dx1_b, g["norm_ffn_g"] = _rowwise(
        "rms_ffn_bwd", rms_bwd_res, l, 512, [(x1, D_MODEL, 0, "row"), (dh2, D_MODEL, 0, "row"), (dx2, D_MODEL, 0, "row")],
        [_row(p["norm_ffn_g"])], [(D_MODEL, F32), (D_MODEL, BF16)], [(1, D_MODEL)])

    dmerged = _mm("mix_out_dx", [(dx1_b, w["w_out"])], trans_b=True, bn_cap=1024)
    g["w_out"] = _mm_tn("mix_out_dw", s["merged"], dx1_b).reshape(N_DEV, D_MODEL // N_DEV, D_MODEL)
    gl, ya, yb, yc = s["gl"], s["ya"], s["yb"], s["yc"]

    def merge_bwd(i, r, c):
        dm = r[0][...]
        dys, dgs = [], []
        for b_ in range(3):
            gate = jax.nn.sigmoid(r[1 + b_][...] + c[0][:, b_ * D_MODEL:(b_ + 1) * D_MODEL])
            dys.append(dm * gate)
            dgs.append(dm * r[4 + b_][...] * (gate * (1.0 - gate)))
        dgl = jnp.concatenate(dgs, axis=1)
        return (*dys, dgl, jnp.sum(dgl, axis=0, keepdims=True))

    gate_ins = [(gl, D_MODEL, b_, "row") for b_ in range(3)]
    dya, dyb, dyc, dgl, db_gate = _rowwise(
        "merge_bwd", merge_bwd, l, 256,
        [(dmerged, D_MODEL, 0, "row")] + gate_ins + [(ya, D_MODEL, 0, "row"), (yb, D_MODEL, 0, "row"),
                                                     (yc, D_MODEL, 0, "row")],
        [_row(p["b_gate"])], [(D_MODEL, BF16)] * 3 + [(3 * D_MODEL, BF16)], [(1, 3 * D_MODEL)])
    g["b_gate"] = db_gate[0]

    do_c = _mm("cross_o_dx", [(dyc, w["w_o_cross"])], trans_b=True, out_dtype=BF16)
    g["w_o_cross"] = _cols_to_rows(_mm_tn("cross_o_dw", s["o_c"], dyc))
    kvm = s["kvm"]
    delta_c = _attn_delta("cross_attn_delta", s["o_c"], do_c, heads=X_HEADS, bq=2048)
    dxqn, dxk, dxv, _ = _attn_bwd("cross_attn_bwd", s["xqn"], s["xk"], s["xk"].T, kvm, do_c, s["lse_c"], delta_c,
                                  qoff=0, koff=0, voff=X_HEADS, heads=X_HEADS, causal=False,
                                  scale=X_HEAD_DIM ** -0.5, bq=1024, bk=256)
    ml = mem.shape[0]

    def headnorm_bwd(i, r, c):
        dxs, dg = [], None
        for hd in range(X_HEADS):
            cols = slice(hd * LANES, (hd + 1) * LANES)
            dx_h, dg_h = _rms_b(r[0][:, cols], c[0][...], r[1][:, cols], X_HEAD_DIM)
            dxs.append(dx_h)
            dg = dg_h if dg is None else dg + dg_h
        return (jnp.concatenate(dxs, axis=1), dg)

    dxq, dg_xq = _rowwise("cross_q_norm_bwd", headnorm_bwd, l, 512,
                          [(s["xq"], X_WIDTH, 0, "row"), (dxqn, X_WIDTH, 0, "row")], [_row(p["xq_norm_g"])],
                          [(X_WIDTH, BF16)], [(1, X_HEAD_DIM)])
    dkvm_k, dg_xk = _rowwise("cross_k_norm_bwd", headnorm_bwd, ml, 256,
                             [(kvm, X_WIDTH, 0, "row"), (dxk, X_WIDTH, 0, "row")], [_row(p["xk_norm_g"])],
                             [(X_WIDTH, F32)], [(1, X_HEAD_DIM)])
    g["xq_norm_g"], g["xk_norm_g"] = dg_xq[0], dg_xk[0]
    dkvm = jnp.concatenate([dkvm_k, dxv], axis=1)
    g["w_mem_kv"] = _mm_tn("cross_kv_dw", s["memn"], dkvm).reshape(N_DEV, D_MODEL // N_DEV, 2 * X_WIDTH)
    dmemn = _mm("cross_kv_dx", [(dkvm, w["w_mem_kv"])], trans_b=True, bn_cap=1024)

    def rms_bwd_gain_only(i, r, c):
        return (_rms_b(r[0][...], c[0][...], r[1][...], D_MODEL)[1],)

    (dg_mem,) = _rowwise("rms_mem_bwd", rms_bwd_gain_only, ml, 256,
                         [(mem, D_MODEL, 0, "row"), (dmemn, D_MODEL, 0, "row")], [_row(p["mem_norm_g"])], [],
                         [(1, D_MODEL)])
    g["mem_norm_g"] = dg_mem[0]

    dout_b = _mm("s5_o_dx", [(dyb, w["w_o_ssm"])], trans_b=True)
    g["w_o_ssm"] = _cols_to_rows(_mm_tn("s5_o_dw", s["out_b"], dyb))
    ypre, u, z = s["ypre"], s["u"], s["z"]
    d_row, b_glu = _row(p["ssm_d"]), _row(p["b_glu"])
    yuz = [(ypre, SSM_WIDTH, 0, "row"), (u, SSM_WIDTH, 0, "row"), (z, SSM_WIDTH, 0, "row")]

    def glu_bwd_z(i, r, c):
        y = _gelu(r[1][...] + c[0][...] * r[2][...])
        sg = jax.nn.sigmoid(r[3][...] + c[1][...])
        dz = r[0][...] * y * (sg * (1.0 - sg))
        return (dz, jnp.sum(dz, axis=0, keepdims=True))

    dz, db_glu = _rowwise("s5_glu_bwd_z", glu_bwd_z, l, 512, [(dout_b, SSM_WIDTH, 0, "row")] + yuz, [d_row, b_glu],
                          [(SSM_WIDTH, BF16)], [(1, SSM_WIDTH)])
    g["b_glu"] = db_glu[0]
    g["w_glu"] = _mm_tn("s5_glu_dw", s["y_b"], dz).reshape(N_DEV, SSM_WIDTH // N_DEV, SSM_WIDTH)
    dy2 = _mm("s5_glu_dx", [(dz, w["w_glu"])], trans_b=True)

    def gelu_bwd(i, r, c):
        t = r[2][...] + c[0][...] * r[3][...]
        sg = jax.nn.sigmoid(r[4][...] + c[1][...])
        dt = (r[0][...] * sg + r[1][...]) * _gelu_grad(t)
        return (dt, c[0][...] * dt, jnp.sum(dt * r[3][...], axis=0, keepdims=True))

    dypre, du_skip, dd = _rowwise(
        "s5_gelu_bwd", gelu_bwd, l, 512, [(dout_b, SSM_WIDTH, 0, "row"), (dy2, SSM_WIDTH, 0, "row")] + yuz,
        [d_row, b_glu], [(SSM_WIDTH, BF16), (SSM_WIDTH, F32)], [(1, SSM_WIDTH)])
    g["ssm_d"] = dd.reshape(SSM_GROUPS, SSM_GROUP_CH)
    cc_re_d, cc_imn_d = s["cc_d"]
    bb_re_d, bb_im_d = s["bb_d"]
    a_re_row, a_im_row = s["a_rows"]
    s_re, s_im = s["s_re"], s["s_im"]
    du, da_re, da_im, dbb_re_d, dbb_im_d, dcc_re, dcc_imn = _s5_bwd(
        "s5_bwd", dypre, du_skip, u, s_re, s_im, bb_re_d, bb_im_d, cc_re_d, cc_imn_d, a_re_row, -a_im_row)
    g["ssm_c_re"] = _diag_blocks(dcc_re, SSM_STATE, SSM_GROUP_CH).transpose(0, 2, 1)
    g["ssm_c_im"] = -_diag_blocks(dcc_imn, SSM_STATE, SSM_GROUP_CH).transpose(0, 2, 1)
    dbb_re = _diag_blocks(dbb_re_d, SSM_GROUP_CH, SSM_STATE).transpose(0, 2, 1).reshape(SSM_N, SSM_GROUP_CH)
    dbb_im = _diag_blocks(dbb_im_d, SSM_GROUP_CH, SSM_STATE).transpose(0, 2, 1).reshape(SSM_N, SSM_GROUP_CH)
    dlr, dli, dldt, dbr, dbi = _disc_bwd(*s["disc"], da_re.reshape(SSM_N, 1), da_im.reshape(SSM_N, 1), dbb_re, dbb_im)
    g["ssm_lambda_re"] = dlr.reshape(SSM_GROUPS, SSM_STATE)
    g["ssm_lambda_im"] = dli.reshape(SSM_GROUPS, SSM_STATE)
    g["ssm_log_dt"] = dldt.reshape(SSM_GROUPS, SSM_STATE).sum(axis=1)
    g["ssm_b_re"] = dbr.reshape(SSM_GROUPS, SSM_STATE, SSM_GROUP_CH)
    g["ssm_b_im"] = dbi.reshape(SSM_GROUPS, SSM_STATE, SSM_GROUP_CH)

    do_a = _mm("mla_o_dx", [(dya, w["wo_mla"])], trans_b=True, out_dtype=BF16, bn_cap=1024)
    dwo = _mm_tn("mla_o_dw", s["o_a"], dya)
    g["w_o_mla"] = _cols_to_rows(dwo.reshape(MLA_HEADS, LANES, D_MODEL)[:, :D_V].reshape(MLA_HEADS * D_V, D_MODEL))
    delta_a = _attn_delta("mla_attn_delta", s["o_a"], do_a, heads=MLA_HEADS, bq=2048)
    dq, dk, dv, rode = _attn_bwd("mla_attn_bwd", s["q"], s["k"], s["k"].T, s["v_mla"], do_a, s["lse_a"], delta_a,
                                 qoff=0, koff=0, voff=0, heads=MLA_HEADS, causal=True, scale=D_QK ** -0.5, bq=1024,
                                 bk=1024, side=make_side(g))
    hw = MLA_HEADS * LANES
    kr_blk = (Q_LORA + KV_LORA) // LANES
    pqkv = s["pqkv"]
    g_q, g_k = _pad_lanes(p["q_norm_g"]), _pad_lanes(p["k_norm_g"])
    lane = lax.broadcasted_iota(jnp.int32, (1, LANES), 1)
    kr_mask = jnp.logical_and(lane >= KR_LO, lane < KR_LO + D_ROPE).astype(F32)

    def prep_b_bwd(i, r, c):
        dq_ref, dk_ref, q_ref, k_ref, kr_ref, c_ref, sa_ref, sb_ref = r
        rc, sa, sb, kr = c_ref[...], sa_ref[...], sb_ref[...], kr_ref[...]
        dqs, dks, dkr, dgq, dgk = [], [], None, None, None
        for hd in range(MLA_HEADS):
            cols = slice(hd * LANES, (hd + 1) * LANES)
            dxq, dgq_h = _rms_b(q_ref[:, cols], c[0][...], _rope_b(dq_ref[:, cols], rc, sa, sb), D_QK)
            dxk, dgk_h = _rms_b(k_ref[:, cols] + kr, c[1][...], _rope_b(dk_ref[:, cols], rc, sa, sb), D_QK)
            dqs.append(dxq)
            dks.append(dxk)
            dkr = dxk if dkr is None else dkr + dxk
            dgq = dgq_h if dgq is None else dgq + dgq_h
            dgk = dgk_h if dgk is None else dgk + dgk_h
        return (jnp.concatenate(dqs, axis=1), jnp.concatenate(dks, axis=1), dkr * c[2][...], dgq, dgk)

    tab_ins = [(rope_c, LANES, 0, "row"), (rope_sa, LANES, 0, "row"), (rope_sb, LANES, 0, "row")]
    dq_raw, dk_raw, dkr, dg_q, dg_k = _rowwise(
        "mla_prep_b_bwd", prep_b_bwd, l, 256,
        [(dq, hw, 0, "row"), (dk, hw, 0, "row"), (s["q_raw"], hw, 0, "row"), (s["k_raw"], hw, 0, "row"),
         (pqkv, LANES, kr_blk, "row")] + tab_ins, [g_q, g_k, kr_mask],
        [(hw, BF16), (hw, BF16), (LANES, F32)], [(1, LANES), (1, LANES)])
    g["q_norm_g"], g["k_norm_g"] = dg_q[0, :D_QK], dg_k[0, :D_QK]
    dnq = _mm("mla_q_b_dx", [(dq_raw, w["wqb"])], trans_b=True)
    dnkv = _mm("mla_kv_b_dx", [(dk_raw, w["wk"]), (dv, w["wv"])], trans_b=True)
    dwqb = _mm_tn("mla_q_b_dw", s["nq"], dq_raw)
    g["w_q_b"] = _cols_to_rows(dwqb)
    dwk = _mm_tn("mla_k_b_dw", s["nkv"], dk_raw).reshape(KV_LORA, MLA_HEADS, LANES)[:, :, :D_NOPE]
    dwv = _mm_tn("mla_v_b_dw", s["nkv"], dv).reshape(KV_LORA, MLA_HEADS, LANES)[:, :, :D_V]
    g["w_kv_b"] = jnp.concatenate([dwk, dwv], axis=2).transpose(1, 0, 2)

    def prep_a_bwd(i, r, c):
        dcq, dgqa = _rms_b(r[0][:, :Q_LORA], c[0][...], r[1][...], Q_LORA)
        dckv, dgkva = _rms_b(r[0][:, Q_LORA:Q_LORA + KV_LORA], c[1][...], r[2][...], KV_LORA)
        return (jnp.concatenate([dcq, dckv, r[3][...]], axis=1), dgqa, dgkva)

    dpqkv, dg_qa, dg_kva = _rowwise(
        "mla_prep_a_bwd", prep_a_bwd, l, 512,
        [(pqkv, QKV_W, 0, "row"), (dnq, Q_LORA, 0, "row"), (dnkv, KV_LORA, 0, "row"), (dkr, LANES, 0, "row")],
        [_row(p["q_a_norm_g"]), _row(p["kv_a_norm_g"])], [(QKV_W, BF16)], [(1, Q_LORA), (1, KV_LORA)])
    g["q_a_norm_g"], g["kv_a_norm_g"] = dg_qa[0], dg_kva[0]

    h = s["h"]
    dh = _mm("proj_dx", [(dpqkv, w["wqkv"]), (du, w["w_u"]), (dxq, w["w_xq"]), (dgl, w["w_g"])], trans_b=True,
             bn_cap=1024)
    dwqkv = _mm_tn("proj_qkv_dw", h, dpqkv)
    g["w_in"] = _pack_w_in([dwqkv, _mm_tn("proj_u_dw", h, du), _mm_tn("proj_xq_dw", h, dxq),
                            _mm_tn("proj_gate_dw", h, dgl)])
    dx, dx_b, dg_mix = _rowwise(
        "rms_mix_bwd", rms_bwd_res, l, 512, [(x, D_MODEL, 0, "row"), (dh, D_MODEL, 0, "row"), (dx1, D_MODEL, 0, "row")],
        [_row(p["norm_mix_g"])], [(D_MODEL, F32), (D_MODEL, BF16)], [(1, D_MODEL)])
    g["norm_mix_g"] = dg_mix[0]
    g["norm_ffn_g"] = g["norm_ffn_g"][0]
    return dx, dx_b, g, rode


def _rope_tables(positions):
    inv_freq = ROPE_THETA ** (-jnp.arange(0, D_ROPE, 2, dtype=F32) / D_ROPE)
    ang = positions.astype(F32)[:, None] * inv_freq
    cos, sin = jnp.cos(ang), jnp.sin(ang)
    l = positions.shape[0]
    one, zero = jnp.ones((l, D_NOPE), F32), lambda n: jnp.zeros((l, n), F32)
    pad = LANES - D_QK
    rope_c = jnp.concatenate([one, cos, cos, zero(pad)], axis=1)
    rope_sa = jnp.concatenate([zero(D_NOPE), -sin, zero(16), zero(pad)], axis=1)
    rope_sb = jnp.concatenate([zero(D_NOPE + 16), sin, zero(pad)], axis=1)
    return rope_c, rope_sa, rope_sb


def kernel(x, mem, positions, norm_mix_g, w_in, q_a_norm_g, w_q_b, kv_a_norm_g, w_kv_b, q_norm_g, k_norm_g, w_o_mla, ssm_lambda_re, ssm_lambda_im, ssm_log_dt, ssm_b_re, ssm_b_im, ssm_c_re, ssm_c_im, ssm_d, w_glu, b_glu, w_o_ssm, mem_norm_g, w_mem_kv, xq_norm_g, xk_norm_g, w_o_cross, b_gate, w_out, norm_ffn_g, w_up, conv_w, conv_b, w_down, loss_target, m_norm_mix_g, m_w_in, m_q_a_norm_g, m_w_q_b, m_kv_a_norm_g, m_w_kv_b, m_q_norm_g, m_k_norm_g, m_w_o_mla, m_ssm_lambda_re, m_ssm_lambda_im, m_ssm_log_dt, m_ssm_b_re, m_ssm_b_im, m_ssm_c_re, m_ssm_c_im, m_ssm_d, m_w_glu, m_b_glu, m_w_o_ssm, m_mem_norm_g, m_w_mem_kv, m_xq_norm_g, m_xk_norm_g, m_w_o_cross, m_b_gate, m_w_out, m_norm_ffn_g, m_w_up, m_conv_w, m_conv_b, m_w_down, v_norm_mix_g, v_w_in, v_q_a_norm_g, v_w_q_b, v_kv_a_norm_g, v_w_kv_b, v_q_norm_g, v_k_norm_g, v_w_o_mla, v_ssm_lambda_re, v_ssm_lambda_im, v_ssm_log_dt, v_ssm_b_re, v_ssm_b_im, v_ssm_c_re, v_ssm_c_im, v_ssm_d, v_w_glu, v_b_glu, v_w_o_ssm, v_mem_norm_g, v_w_mem_kv, v_xq_norm_g, v_xk_norm_g, v_w_o_cross, v_b_gate, v_w_out, v_norm_ffn_g, v_w_up, v_conv_w, v_conv_b, v_w_down):
    a = dict(locals())
    wts = {n: a[n] for n in WEIGHT_ORDER}
    m_in = {n: a["m_" + n] for n in WEIGHT_ORDER}
    v_in = {n: a["v_" + n] for n in WEIGHT_ORDER}
    depth = norm_mix_g.shape[0]
    x0, mem0, pos0, tgt = x[0], mem[0], positions[0], loss_target[0]
    l = x0.shape[0]
    offs = {grp: _group_rows(params) for grp, _, params in GROUPS}
    early_names, late_names = [g[0] for g in EARLY_GROUPS], [g[0] for g in LATE_GROUPS]

    def srcs(i, groups):
        return [_group_local(width, params, wts, i, F32 if grp == "gconv" else BF16) for grp, width, params in groups]

    tabs = _rope_tables(pos0)
    layer_p = [{n: wts[n][i] for n in REPLICATED} for i in range(depth)]
    gath = dict(zip(early_names, _gather_all("gather_weights", srcs(0, EARLY_GROUPS))))

    def late(rode):
        return _late_weights(dict(zip(late_names, rode)), offs), rode[len(late_names):]

    saved, layer_w = [], []
    xc = x0
    for i in range(depth):
        ride = srcs(i, LATE_GROUPS) + (srcs(i + 1, EARLY_GROUPS) if i + 1 < depth else [])
        side = _Side(ride, _gather_shapes(ride), GATHER_SEMS, (_gather_start, _gather_relay, _gather_finish))
        xc, s, w_all, rode = _layer_fwd(xc, mem0, tabs, _early_weights(gath, offs), layer_p[i], side, late)
        gath = dict(zip(early_names, rode))
        layer_w.append(w_all)
        saved.append(s)

    def loss_fn(i, r, c):
        d = r[0][...] - r[1][...]
        dy = d * (1.0 / D_MODEL)
        return (dy, dy, jnp.sum(d * d, axis=0, keepdims=True))

    dy, dy_b, sq = _rowwise("loss", loss_fn, l, 512, [(xc, D_MODEL, 0, "row"), (tgt, D_MODEL, 0, "row")], [],
                            [(D_MODEL, F32), (D_MODEL, BF16)], [(1, D_MODEL)])
    loss = lax.psum(0.5 * jnp.sum(sq) / D_MODEL, ("x", "y", "c"))

    def core_sums(g, tag, groups, extra=()):
        sends = []
        for _, width, params in groups:
            blocks = [jnp.pad(g[n], ((0, 0), (0, _rows8(rows) - rows), (0, width - g[n].shape[2])))
                      for n, rows, _ in params]
            sends.append(jnp.concatenate(blocks, axis=1))
        sends += list(extra)
        got = _pair_exchange("exchange_grads_core_" + tag, sends)
        return [_pair_sum("grad_pair_sum_%s_%d" % (tag, k), s_, g_) for k, (s_, g_) in enumerate(zip(sends, got))]

    grads, rcvs = [None] * depth, {}
    dxc, dxc_b = dy, dy_b
    waiting = []
    for i in reversed(range(depth)):
        def make_side(g, i=i, waiting=waiting):
            parts = core_sums(g, "l%d_late" % i, LATE_GROUPS) + waiting
            return _Side(parts, _chip_shapes(parts), CHIP_SEMS, (_chip_start, _chip_finish))

        dxc, dxc_b, grads[i], rode = _layer_bwd(dxc, dxc_b, saved[i], mem0, tabs, layer_w[i], layer_p[i], make_side)
        rcvs.update({(i, nm): r for nm, r in zip(late_names, rode)})
        rcvs.update({(i + 1, nm): r for nm, r in zip(early_names, rode[len(late_names):])})
        waiting = core_sums(grads[i], "l%d_early" % i, EARLY_GROUPS) if i > 0 else []
    grad_x = dxc[None]
    rep_flat = jnp.concatenate([jnp.stack([grads[i][n] for i in range(depth)]).reshape(-1) for n in REPLICATED])
    rep_send = _rep_rows(jnp.broadcast_to(rep_flat[None], (N_DEV, rep_flat.shape[0])))
    last = _chip_exchange("exchange_grads_chip", core_sums(grads[0], "l0_early", EARLY_GROUPS, [rep_send]))
    rcvs.update({(0, nm): r for nm, r in zip(early_names, last)})
    rcv_rep = last[-1]

    per_layer = {}
    for i in range(depth):
        for grp, width, params in GROUPS:
            local = [_group_local(width, params, d, i, F32) for d in (wts, m_in, v_in)]
            res = _adamw("adamw_l%d_%s" % (i, grp), rcvs[(i, grp)], *local)
            for tag, arr in zip(("grad", "delta", "m", "v"), res):
                for n, rows, cols in params:
                    per_layer[(tag, n, i)] = arr[offs[grp][n]:offs[grp][n] + rows, :cols]
    outs = {(tag, n): jnp.stack([per_layer[(tag, n, i)] for i in range(depth)])
            for tag in ("grad", "delta", "m", "v") for _, _, params in GROUPS for n, _, _ in params}
    rep_local = [_rep_rows(jnp.concatenate([d[n].astype(F32).reshape(-1) for n in REPLICATED])) for d in (wts, m_in, v_in)]
    res = _adamw("adamw_rep", rcv_rep, *rep_local)
    for tag, arr in zip(("grad", "delta", "m", "v"), res):
        flat, off = arr.reshape(-1), 0
        for n in REPLICATED:
            cnt = wts[n].size
            outs[(tag, n)] = flat[off:off + cnt].reshape(wts[n].shape)
            off += cnt
    result = [loss, grad_x]
    for tag in ("grad", "delta", "m", "v"):
        result += [outs[(tag, n)] for n in WEIGHT_ORDER]
    return tuple(result)
```

```python
import collections
import math

import jax
import jax.numpy as jnp
from jax import lax
from jax.experimental import pallas as pl
from jax.experimental.pallas import tpu as pltpu

F32 = jnp.float32
BF16 = jnp.bfloat16

N_DEV = 8
LANES = 128
LOG2E = math.log2(math.e)
VMEM_LIMIT_BYTES = 56 * 1024 * 1024

D_MODEL = 1024
EPS = 1e-6
MLA_HEADS = 8
Q_LORA = 384
KV_LORA = 256
D_NOPE = 64
D_ROPE = 32
D_QK = D_NOPE + D_ROPE
D_V = 64
ROPE_THETA = 10000.0
SSM_GROUPS = 32
SSM_GROUP_CH = 16
SSM_WIDTH = 512
SSM_STATE = 64
SSM_N = SSM_GROUPS * SSM_STATE
X_HEADS = 4
X_HEAD_DIM = 128
X_WIDTH = 512
D_FF = 2816
IN_WIDTH = Q_LORA + KV_LORA + D_ROPE + SSM_WIDTH + X_WIDTH + 3 * D_MODEL
QKV_W = Q_LORA + KV_LORA + LANES
KR_LO = D_NOPE

ADAM_LR = 0.001
ADAM_B1 = 0.9
ADAM_B2 = 0.999
ADAM_EPS = 1e-08
ADAM_WD = 0.01
ADAM_STEP = 10

REPLICATED = (
    "norm_mix_g", "q_a_norm_g", "kv_a_norm_g", "q_norm_g", "k_norm_g", "ssm_lambda_re", "ssm_lambda_im",
    "ssm_log_dt", "ssm_b_re", "ssm_b_im", "ssm_c_re", "ssm_c_im", "ssm_d", "b_glu", "mem_norm_g",
    "xq_norm_g", "xk_norm_g", "b_gate", "norm_ffn_g", "conv_b",
)
WEIGHT_ORDER = (
    "norm_mix_g", "w_in", "q_a_norm_g", "w_q_b", "kv_a_norm_g", "w_kv_b", "q_norm_g", "k_norm_g", "w_o_mla",
    "ssm_lambda_re", "ssm_lambda_im", "ssm_log_dt", "ssm_b_re", "ssm_b_im", "ssm_c_re", "ssm_c_im", "ssm_d",
    "w_glu", "b_glu", "w_o_ssm", "mem_norm_g", "w_mem_kv", "xq_norm_g", "xk_norm_g", "w_o_cross", "b_gate",
    "w_out", "norm_ffn_g", "w_up", "conv_w", "conv_b", "w_down",
)


def _params(**kw):
    return pltpu.CompilerParams(vmem_limit_bytes=VMEM_LIMIT_BYTES, **kw)


def _pick(n, cap):
    if n <= cap:
        return n
    best = None
    for m in range(LANES, cap + 1, LANES):
        if n % m == 0:
            best = m
    assert best is not None, n
    return best


_NN = (((1,), (0,)), ((), ()))
_NT = (((1,), (1,)), ((), ()))
_TN = (((0,), (0,)), ((), ()))


def _dot(a, b, dn):
    return lax.dot_general(a.astype(BF16), b.astype(BF16), dn, preferred_element_type=F32)


def _mm(name, pairs, *, trans_b=False, add=None, out_dtype=F32, bm=512, bn_cap=512):
    m = pairs[0][0].shape[0]
    n = pairs[0][1].shape[0 if trans_b else 1]
    bm = min(bm, m)
    bn = _pick(n, bn_cap)
    npair = len(pairs)

    def body(*refs):
        o_ref = refs[-1]
        acc = None
        for p in range(npair):
            d = _dot(refs[2 * p][...], refs[2 * p + 1][...], _NT if trans_b else _NN)
            acc = d if acc is None else acc + d
        if add is not None:
            acc = acc + refs[2 * npair][...]
        o_ref[...] = acc.astype(out_dtype)

    in_specs, args = [], []
    for a, b in pairs:
        k = a.shape[1]
        in_specs.append(pl.BlockSpec((bm, k), lambda i, j: (i, 0)))
        if trans_b:
            in_specs.append(pl.BlockSpec((bn, k), lambda i, j: (j, 0)))
        else:
            in_specs.append(pl.BlockSpec((k, bn), lambda i, j: (0, j)))
        args += [a, b]
    if add is not None:
        in_specs.append(pl.BlockSpec((bm, bn), lambda i, j: (i, j)))
        args.append(add)
    return pl.pallas_call(
        body, name=name, grid=(m // bm, n // bn), in_specs=in_specs,
        out_specs=pl.BlockSpec((bm, bn), lambda i, j: (i, j)),
        out_shape=jax.ShapeDtypeStruct((m, n), out_dtype), compiler_params=_params(),
    )(*args)


def _mm_tn(name, a, b, *, bm_cap=512, bn_cap=1536, bk=1024):
    l, m = a.shape
    n = b.shape[1]
    bm, bn, bk = _pick(m, bm_cap), _pick(n, bn_cap), min(bk, l)

    def body(a_ref, b_ref, o_ref):
        @pl.when(pl.program_id(2) == 0)
        def _():
            o_ref[...] = jnp.zeros_like(o_ref)

        o_ref[...] += _dot(a_ref[...], b_ref[...], _TN)

    return pl.pallas_call(
        body, name=name, grid=(m // bm, n // bn, l // bk),
        in_specs=[pl.BlockSpec((bk, bm), lambda i, j, k: (k, i)), pl.BlockSpec((bk, bn), lambda i, j, k: (k, j))],
        out_specs=pl.BlockSpec((bm, bn), lambda i, j, k: (i, j)),
        out_shape=jax.ShapeDtypeStruct((m, n), F32), compiler_params=_params(),
    )(a, b)


def _rowwise(name, fn, nrows, bm, row_ins, consts, row_outs, acc_outs=()):
    bm = min(bm, nrows)
    nblk = nrows // bm
    sub = bm // 8
    nin, nc, nro = len(row_ins), len(consts), len(row_outs)

    def body(*refs):
        i = pl.program_id(0)
        outs = fn(i, refs[:nin], refs[nin:nin + nc])
        o_refs = refs[nin + nc:nin + nc + nro]
        a_refs = refs[nin + nc + nro:]
        for r, v in zip(o_refs, outs[:nro]):
            r[...] = v.astype(r.dtype)
        if a_refs:
            @pl.when(i == 0)
            def _():
                for r in a_refs:
                    r[...] = jnp.zeros_like(r)

            for r, v in zip(a_refs, outs[nro:]):
                r[...] += v

    in_specs, args = [], []
    for arr, w, cb, kind in row_ins:
        if kind == "row":
            in_specs.append(pl.BlockSpec((bm, w), lambda i, cb=cb: (i, cb)))
        elif kind == "prev":
            in_specs.append(pl.BlockSpec((8, w), lambda i, cb=cb: (jnp.maximum(i * sub - 1, 0), cb)))
        else:
            in_specs.append(pl.BlockSpec((8, w), lambda i, cb=cb: (jnp.minimum((i + 1) * sub, nrows // 8 - 1), cb)))
        args.append(arr)
    for c in consts:
        in_specs.append(pl.BlockSpec(c.shape, lambda i: (0, 0)))
        args.append(c)
    out_specs = [pl.BlockSpec((bm, w), lambda i: (i, 0)) for w, _ in row_outs]
    out_specs += [pl.BlockSpec(s, lambda i: (0, 0)) for s in acc_outs]
    out_shape = [jax.ShapeDtypeStruct((nrows, w), dt) for w, dt in row_outs]
    out_shape += [jax.ShapeDtypeStruct(s, F32) for s in acc_outs]
    res = pl.pallas_call(
        body, name=name, grid=(nblk,), in_specs=in_specs, out_specs=out_specs, out_shape=out_shape,
        compiler_params=_params(),
    )(*args)
    return res


def _rms_f(x, g, n):
    r = lax.rsqrt(jnp.sum(x * x, axis=-1, keepdims=True) * (1.0 / n) + EPS)
    return x * r * g


def _rms_b(x, g, dy, n):
    r = lax.rsqrt(jnp.sum(x * x, axis=-1, keepdims=True) * (1.0 / n) + EPS)
    gx = dy * g
    dx = r * gx - x * (r * r * r * (jnp.sum(x * gx, axis=-1, keepdims=True) * (1.0 / n)))
    dg = jnp.sum(dy * (x * r), axis=0, keepdims=True)
    return dx, dg


def _rope_f(x, c, sa, sb):
    return x * c + pltpu.roll(x, LANES - 16, 1) * sa + pltpu.roll(x, 16, 1) * sb


def _rope_b(g, c, sa, sb):
    return g * c + pltpu.roll(g * sa, 16, 1) + pltpu.roll(g * sb, LANES - 16, 1)


def _gelu(x):
    c = math.sqrt(2.0 / math.pi)
    return 0.5 * x * (1.0 + jnp.tanh(c * (x + 0.044715 * (x * x * x))))


def _gelu_grad(x):
    c = math.sqrt(2.0 / math.pi)
    th = jnp.tanh(c * (x + 0.044715 * (x * x * x)))
    return 0.5 * (1.0 + th) + 0.5 * x * (1.0 - th * th) * (c * (1.0 + 3.0 * 0.044715 * (x * x)))


def _row_ids(bm):
    return lax.broadcasted_iota(jnp.int32, (bm, 1), 0)


def _shift_down(x, halo_ref, i, k):
    live = (i > 0).astype(F32)
    out = pltpu.roll(x, k, 0)
    row = _row_ids(8)
    first = out[:8]
    for r in range(k):
        e = (row == r).astype(F32)
        first = first * (1.0 - e) + e * (halo_ref[8 - k + r:8 - k + r + 1, :] * live)
    return jnp.concatenate([first, out[8:]], axis=0)


def _live_pairs(nq, nk, bq, bk, causal, key_major):
    pairs = [(i, j) for i in range(nq) for j in range(nk) if not causal or j * bk <= i * bq + bq - 1]
    if key_major:
        pairs.sort(key=lambda ij: (ij[1], ij[0]))
    return (jnp.asarray([p[0] for p in pairs], jnp.int32), jnp.asarray([p[1] for p in pairs], jnp.int32))


def _attn_fwd(name, qa, ka, vta, *, qoff, koff, voff, heads, causal, scale, bq, bk, side=None):
    lq, lk = qa.shape[0], ka.shape[0]
    bq, bk = min(bq, lq), min(bk, lk)
    nq, nk = lq // bq, lk // bk
    c2 = scale * LOG2E
    tab_i, tab_j = _live_pairs(nq, nk, bq, bk, causal, key_major=False)

    def body(ti, tj, q_ref, k_ref, vt_ref, o_ref, lse_ref, m_s, l_s, acc_s):
        t = pl.program_id(1)
        i, j = ti[t], tj[t]
        j_last = jnp.minimum(nk - 1, (i * bq + bq - 1) // bk) if causal else nk - 1

        @pl.when(j == 0)
        def _():
            m_s[...] = jnp.full_like(m_s, -1e30)
            l_s[...] = jnp.zeros_like(l_s)
            acc_s[...] = jnp.zeros_like(acc_s)

        def step(masked):
            st = _dot(k_ref[...], q_ref[...], _NT) * c2
            if masked:
                key = j * bk + lax.broadcasted_iota(jnp.int32, (bk, bq), 0)
                qry = i * bq + lax.broadcasted_iota(jnp.int32, (bk, bq), 1)
                st = jnp.where(key <= qry, st, -1e30)
            m_prev = m_s[...]
            m_new = jnp.maximum(m_prev, jnp.max(st, axis=0, keepdims=True))
            alpha = jnp.exp2(m_prev - m_new)
            pt = jnp.exp2(st - m_new)
            l_s[...] = alpha * l_s[...] + jnp.sum(pt, axis=0, keepdims=True)
            acc_s[...] = alpha * acc_s[...] + _dot(vt_ref[...], pt, _NN)
            m_s[...] = m_new

        if causal:
            full = j * bk + bk - 1 <= i * bq
            pl.when(full)(lambda: step(False))
            pl.when(jnp.logical_not(full))(lambda: step(True))
        else:
            step(False)

        @pl.when(j == j_last)
        def _():
            l = l_s[...]
            o_ref[...] = (acc_s[...] / l).T.astype(o_ref.dtype)
            lse_ref[0] = m_s[...] + jnp.log2(l)

    in_specs = [pl.BlockSpec((bq, LANES), lambda h, t, ti, tj: (ti[t], qoff + h)),
                pl.BlockSpec((bk, LANES), lambda h, t, ti, tj: (tj[t], koff + h)),
                pl.BlockSpec((LANES, bk), lambda h, t, ti, tj: (voff + h, tj[t]))]
    out_specs = [pl.BlockSpec((bq, LANES), lambda h, t, ti, tj: (ti[t], h)),
                 pl.BlockSpec((1, 1, bq), lambda h, t, ti, tj: (h, 0, ti[t]))]
    scratch = [pltpu.VMEM((1, bq), F32), pltpu.VMEM((1, bq), F32), pltpu.VMEM((LANES, bq), F32)]
    out_shape = [jax.ShapeDtypeStruct((lq, heads * LANES), BF16), jax.ShapeDtypeStruct((heads, 1, lq), F32)]
    (o, lse), rode = _pair_grid_call(body, name, heads, (tab_i, tab_j), [qa, ka, vta], in_specs, out_specs, out_shape,
                                     scratch, side)
    return o, lse, rode


_Side = collections.namedtuple("_Side", "srcs out_shapes nsem phases")


def _pair_grid_call(body, name, heads, tabs, ins, in_specs, out_specs, out_shape, scratch, side):
    npairs = int(tabs[0].shape[0])
    n_in, n_out = len(ins), len(out_shape)
    n = len(side.srcs) if side else 0

    def wrapped(*refs):
        pre = len(tabs)
        if not side:
            return body(*refs)
        s_refs = refs[pre + n_in:pre + n_in + n]
        o_refs = refs[pre + n_in + n + n_out:pre + n_in + 2 * n + n_out]
        send, recv, loc = refs[-3:]
        args = (s_refs, o_refs, (send, recv), loc)
        h, t = pl.program_id(0), pl.program_id(1)
        pl.when(jnp.logical_and(h == 0, t == 0))(lambda: side.phases[0](*args))
        if len(side.phases) == 3:
            pl.when(jnp.logical_and(h == heads // 2, t == 0))(lambda: side.phases[1](*args))
        body(*refs[:pre + n_in], *refs[pre + n_in + n:pre + n_in + n + n_out], *refs[pre + n_in + 2 * n + n_out:-3])
        pl.when(jnp.logical_and(h == heads - 1, t == npairs - 1))(lambda: side.phases[-1](*args))

    hbm = pl.BlockSpec(memory_space=pltpu.HBM)
    sems = [pltpu.SemaphoreType.DMA((side.nsem * n,)), pltpu.SemaphoreType.DMA((side.nsem * n,)),
            pltpu.SemaphoreType.DMA((n,))] if side else []
    grid_spec = pltpu.PrefetchScalarGridSpec(
        num_scalar_prefetch=len(tabs), grid=(heads, npairs), in_specs=in_specs + [hbm] * n,
        out_specs=out_specs + [hbm] * n, scratch_shapes=scratch + sems)
    res = pl.pallas_call(
        wrapped, name=name, grid_spec=grid_spec, out_shape=out_shape + (list(side.out_shapes) if side else []),
        compiler_params=_params(has_side_effects=True) if side else _params(),
    )(*tabs, *ins, *(side.srcs if side else []))
    return res[:n_out], res[n_out:]


def _attn_delta(name, oa, doa, *, heads, bq):
    lq = oa.shape[0]
    bq = min(bq, lq)

    def body(o_ref, do_ref, d_ref):
        prod = o_ref[...].astype(F32) * do_ref[...].astype(F32)
        hi = prod.astype(BF16)
        lo = (prod - hi.astype(F32)).astype(BF16)
        pick = (lax.broadcasted_iota(jnp.int32, (8, LANES), 0) == 0).astype(BF16)
        sums = _dot(pick, hi, _NT) + _dot(pick, lo, _NT)
        d_ref[0] = jnp.sum(sums, axis=0, keepdims=True)

    blk = pl.BlockSpec((bq, LANES), lambda h, i: (i, h))
    return pl.pallas_call(
        body, name=name, grid=(heads, lq // bq), in_specs=[blk, blk],
        out_specs=pl.BlockSpec((1, 1, bq), lambda h, i: (h, 0, i)),
        out_shape=jax.ShapeDtypeStruct((heads, 1, lq), F32), compiler_params=_params(),
    )(oa, doa)


def _attn_bwd(name, qa, ka, kta, va, doa, lsea, deltaa, *, qoff, koff, voff, heads, causal, scale, bq, bk, side=None):
    lq, lk = qa.shape[0], ka.shape[0]
    bq, bk = min(bq, lq), min(bk, lk)
    nq, nk = lq // bq, lk // bk
    c2 = scale * LOG2E
    tab_i, tab_j = _live_pairs(nq, nk, bq, bk, causal, key_major=True)

    def body(ti, tj, q_ref, k_ref, kt_ref, v_ref, do_ref, lse_ref, delta_ref, dqt_ref, dk_ref, dv_ref):
        t = pl.program_id(1)
        i, j = ti[t], tj[t]
        i_first = (j * bk) // bq if causal else 0

        @pl.when(t == 0)
        def _():
            dqt_ref[...] = jnp.zeros_like(dqt_ref)

        @pl.when(i == i_first)
        def _():
            dk_ref[...] = jnp.zeros_like(dk_ref)
            dv_ref[...] = jnp.zeros_like(dv_ref)

        def step(masked):
            q, k, v, do = q_ref[...], k_ref[...], v_ref[...], do_ref[...]
            st = _dot(k, q, _NT) * c2
            if masked:
                key = j * bk + lax.broadcasted_iota(jnp.int32, (bk, bq), 0)
                qry = i * bq + lax.broadcasted_iota(jnp.int32, (bk, bq), 1)
                st = jnp.where(key <= qry, st, -1e30)
            pt = jnp.exp2(st - lse_ref[0])
            dv_ref[...] += _dot(pt, do, _NN)
            dpt = _dot(v, do, _NT)
            dst = (pt * (dpt - delta_ref[0]) * scale).astype(BF16)
            dk_ref[...] += _dot(dst, q, _NN)
            dqt_ref[0, i] += _dot(kt_ref[...], dst, _NN)

        if causal:
            full = j * bk + bk - 1 <= i * bq
            pl.when(full)(lambda: step(False))
            pl.when(jnp.logical_not(full))(lambda: step(True))
        else:
            step(False)

    q_spec = lambda off: pl.BlockSpec((bq, LANES), lambda h, t, ti, tj: (ti[t], off + h))
    kv_spec = lambda off: pl.BlockSpec((bk, LANES), lambda h, t, ti, tj: (tj[t], off + h))
    row_spec = pl.BlockSpec((1, 1, bq), lambda h, t, ti, tj: (h, 0, ti[t]))
    in_specs = [q_spec(qoff), kv_spec(koff), pl.BlockSpec((LANES, bk), lambda h, t, ti, tj: (koff + h, tj[t])),
                kv_spec(voff), q_spec(0), row_spec, row_spec]
    out_specs = [pl.BlockSpec((1, nq, LANES, bq), lambda h, t, ti, tj: (h, 0, 0, 0)),
                 pl.BlockSpec((bk, LANES), lambda h, t, ti, tj: (tj[t], h)),
                 pl.BlockSpec((bk, LANES), lambda h, t, ti, tj: (tj[t], h))]
    out_shape = [jax.ShapeDtypeStruct((heads, nq, LANES, bq), F32), jax.ShapeDtypeStruct((lk, heads * LANES), F32),
                 jax.ShapeDtypeStruct((lk, heads * LANES), F32)]
    (dqt, dk, dv), rode = _pair_grid_call(body, name, heads, (tab_i, tab_j), [qa, ka, kta, va, doa, lsea, deltaa],
                                          in_specs, out_specs, out_shape, [], side)
    return dqt.transpose(1, 3, 0, 2).reshape(lq, heads * LANES), dk, dv, rode


S5_STRIP = SSM_N // 4
S5_CH = SSM_WIDTH // 4
S5_CHUNK = 512


def _scan_chunk(xr_ref, xi_ref, ar, ai, cre_s, cim_s, reverse, unroll=4):
    t = xr_ref.shape[0]
    ng = t // 8
    edge = 0 if reverse else 7
    row8 = lax.broadcasted_iota(jnp.int32, (8, 1), 0)

    def grp(g):
        return pl.ds(pl.multiple_of(g * 8, 8), 8)

    steps, pr, pi = [], ar, ai
    for d in (1, 2, 4):
        own = ((row8 < 8 - d) if reverse else (row8 >= d)).astype(F32)
        steps.append(((8 - d) if reverse else d, pr * own, pi * own))
        pr, pi = pr * pr - pi * pi, 2.0 * pr * pi

    def group_scan(xr, xi):
        for shift, mr, mi in steps:
            sr, si = pltpu.roll(xr, shift, 0), pltpu.roll(xi, shift, 0)
            xr, xi = xr + mr * sr - mi * si, xi + mr * si + mi * sr
        return xr, xi

    def local(g, _):
        xr, xi = group_scan(xr_ref[grp(g), :], xi_ref[grp(g), :])
        xr_ref[grp(g), :] = xr
        xi_ref[grp(g), :] = xi
        return 0

    lax.fori_loop(0, ng, local, 0, unroll=unroll)
    e = (row8 == 7 - edge).astype(F32)
    pw_r, pw_i = group_scan(e * ar, e * ai)
    e_out = (row8 == edge).astype(F32)
    a8_r = jnp.sum(pw_r * e_out, axis=0, keepdims=True)
    a8_i = jnp.sum(pw_i * e_out, axis=0, keepdims=True)
    cr, ci = cre_s[...], cim_s[...]
    for i in range(ng):
        g = ng - 1 - i if reverse else i
        rows = slice(g * 8, g * 8 + 8)
        lr, li = xr_ref[g * 8 + edge:g * 8 + edge + 1, :], xi_ref[g * 8 + edge:g * 8 + edge + 1, :]
        xr_ref[rows, :] = xr_ref[rows, :] + pw_r * cr - pw_i * ci
        xi_ref[rows, :] = xi_ref[rows, :] + pw_r * ci + pw_i * cr
        cr, ci = lr + a8_r * cr - a8_i * ci, li + a8_r * ci + a8_i * cr
    cre_s[...] = cr
    cim_s[...] = ci


def _s5_fwd(name, u, bb_re, bb_im, cc_re, cc_imn, a_re, a_im):
    l = u.shape[0]
    t = min(S5_CHUNK, l)

    def body(u_ref, bbr_ref, bbi_ref, ccr_ref, cci_ref, ar_ref, ai_ref, sre_ref, sim_ref, y_ref, cre_s, cim_s):
        @pl.when(pl.program_id(1) == 0)
        def _():
            cre_s[...] = jnp.zeros_like(cre_s)
            cim_s[...] = jnp.zeros_like(cim_s)

        uv = u_ref[...]
        sre_ref[...] = _dot(uv, bbr_ref[...], _NN)
        sim_ref[...] = _dot(uv, bbi_ref[...], _NN)
        _scan_chunk(sre_ref, sim_ref, ar_ref[...], ai_ref[...], cre_s, cim_s, reverse=False)
        y_ref[...] = _dot(sre_ref[...], ccr_ref[...], _NN) + _dot(sim_ref[...], cci_ref[...], _NN)

    rows_ch = pl.BlockSpec((t, S5_CH), lambda w, c: (c, w))
    rows_st = pl.BlockSpec((t, S5_STRIP), lambda w, c: (c, w))
    b_blk = pl.BlockSpec((S5_CH, S5_STRIP), lambda w, c: (w, w))
    c_blk = pl.BlockSpec((S5_STRIP, S5_CH), lambda w, c: (w, w))
    a_blk = pl.BlockSpec((1, S5_STRIP), lambda w, c: (0, w))
    return pl.pallas_call(
        body, name=name, grid=(SSM_N // S5_STRIP, l // t),
        in_specs=[rows_ch, b_blk, b_blk, c_blk, c_blk, a_blk, a_blk], out_specs=[rows_st, rows_st, rows_ch],
        out_shape=[jax.ShapeDtypeStruct((l, SSM_N), F32), jax.ShapeDtypeStruct((l, SSM_N), F32),
                   jax.ShapeDtypeStruct((l, SSM_WIDTH), F32)],
        scratch_shapes=[pltpu.VMEM((1, S5_STRIP), F32), pltpu.VMEM((1, S5_STRIP), F32)],
        compiler_params=_params(),
    )(u, bb_re, bb_im, cc_re, cc_imn, a_re, a_im)


def _s5_bwd(name, dy, du_skip, u, s_re, s_im, bb_re, bb_im, cc_re, cc_imn, a_re, a_im_neg):
    l = u.shape[0]
    t = min(S5_CHUNK, l)
    nc = l // t

    def body(dy_ref, skip_ref, u_ref, sre_ref, sim_ref, hre_ref, him_ref, bbr_ref, bbi_ref, ccr_ref, cci_ref, ar_ref,
             ai_ref, du_ref, dar_ref, dai_ref, dbr_ref, dbi_ref, dcr_ref, dci_ref, lr_s, li_s, cre_s, cim_s):
        c = pl.program_id(1)

        @pl.when(c == 0)
        def _():
            for r in (cre_s, cim_s, dar_ref, dai_ref, dbr_ref, dbi_ref, dcr_ref, dci_ref):
                r[...] = jnp.zeros_like(r)

        dyv, uv = dy_ref[...], u_ref[...]
        lr_s[...] = _dot(dyv, ccr_ref[...], _NT)
        li_s[...] = _dot(dyv, cci_ref[...], _NT)
        _scan_chunk(lr_s, li_s, ar_ref[...], ai_ref[...], cre_s, cim_s, reverse=True)
        lam_r, lam_i = lr_s[...], li_s[...]
        s_r, s_i = sre_ref[...], sim_ref[...]
        tc = nc - 1 - c
        sp_r, sp_i = _shift_down(s_r, hre_ref, tc, 1), _shift_down(s_i, him_ref, tc, 1)
        dar_ref[...] += jnp.sum(lam_r * sp_r + lam_i * sp_i, axis=0, keepdims=True)
        dai_ref[...] += jnp.sum(lam_i * sp_r - lam_r * sp_i, axis=0, keepdims=True)
        dbr_ref[0] += _dot(uv, lam_r, _TN)
        dbi_ref[0] += _dot(uv, lam_i, _TN)
        dcr_ref[0] += _dot(s_r, dyv, _TN)
        dci_ref[0] += _dot(s_i, dyv, _TN)
        du = _dot(lam_r, bbr_ref[...], _NT) + _dot(lam_i, bbi_ref[...], _NT) + skip_ref[...]
        du_ref[...] = du.astype(du_ref.dtype)

    rows_ch = pl.BlockSpec((t, S5_CH), lambda w, c: (nc - 1 - c, w))
    rows_st = pl.BlockSpec((t, S5_STRIP), lambda w, c: (nc - 1 - c, w))
    halo = pl.BlockSpec((8, S5_STRIP), lambda w, c: (jnp.maximum((nc - 1 - c) * (t // 8) - 1, 0), w))
    b_blk = pl.BlockSpec((S5_CH, S5_STRIP), lambda w, c: (w, w))
    c_blk = pl.BlockSpec((S5_STRIP, S5_CH), lambda w, c: (w, w))
    a_blk = pl.BlockSpec((1, S5_STRIP), lambda w, c: (0, w))
    nw = SSM_N // S5_STRIP
    return pl.pallas_call(
        body, name=name, grid=(nw, nc),
        in_specs=[rows_ch, rows_ch, rows_ch, rows_st, rows_st, halo, halo, b_blk, b_blk, c_blk, c_blk, a_blk, a_blk],
        out_specs=[rows_ch, a_blk, a_blk, pl.BlockSpec((1, S5_CH, S5_STRIP), lambda w, c: (w, 0, 0)),
                   pl.BlockSpec((1, S5_CH, S5_STRIP), lambda w, c: (w, 0, 0)),
                   pl.BlockSpec((1, S5_STRIP, S5_CH), lambda w, c: (w, 0, 0)),
                   pl.BlockSpec((1, S5_STRIP, S5_CH), lambda w, c: (w, 0, 0))],
        out_shape=[jax.ShapeDtypeStruct((l, SSM_WIDTH), BF16), jax.ShapeDtypeStruct((1, SSM_N), F32),
                   jax.ShapeDtypeStruct((1, SSM_N), F32), jax.ShapeDtypeStruct((nw, S5_CH, S5_STRIP), F32),
                   jax.ShapeDtypeStruct((nw, S5_CH, S5_STRIP), F32), jax.ShapeDtypeStruct((nw, S5_STRIP, S5_CH), F32),
                   jax.ShapeDtypeStruct((nw, S5_STRIP, S5_CH), F32)],
        scratch_shapes=[pltpu.VMEM((t, S5_STRIP), F32), pltpu.VMEM((t, S5_STRIP), F32),
                        pltpu.VMEM((1, S5_STRIP), F32), pltpu.VMEM((1, S5_STRIP), F32)],
        compiler_params=_params(),
    )(dy, du_skip, u, s_re, s_im, s_re, s_im, bb_re, bb_im, cc_re, cc_imn, a_re, a_im_neg)


def _disc_math(lr, li, ldt, br, bi):
    dt = jnp.exp(ldt)
    mag = jnp.exp(lr * dt)
    a_re, a_im = mag * jnp.cos(li * dt), mag * jnp.sin(li * dt)
    den = lr * lr + li * li
    e_re, e_im = a_re - 1.0, a_im
    f_re = (e_re * lr + e_im * li) / den
    f_im = (e_im * lr - e_re * li) / den
    return a_re, a_im, f_re * br - f_im * bi, f_re * bi + f_im * br


def _disc_fwd(lr, li, ldt, br, bi):
    def body(lr_ref, li_ref, ldt_ref, br_ref, bi_ref, are_ref, aim_ref, bbr_ref, bbi_ref):
        a_re, a_im, bb_re, bb_im = _disc_math(lr_ref[...], li_ref[...], ldt_ref[...], br_ref[...], bi_ref[...])
        are_ref[...] = a_re
        aim_ref[...] = a_im
        bbr_ref[...] = bb_re
        bbi_ref[...] = bb_im

    col = jax.ShapeDtypeStruct(lr.shape, F32)
    mat = jax.ShapeDtypeStruct(br.shape, F32)
    return pl.pallas_call(body, name="s5_disc_fwd", out_shape=[col, col, mat, mat], compiler_params=_params())(
        lr, li, ldt, br, bi)


def _disc_bwd(lr, li, ldt, br, bi, da_re, da_im, dbb_re, dbb_im):
    def body(lr_ref, li_ref, ldt_ref, br_ref, bi_ref, g0, g1, g2, g3, o0, o1, o2, o3, o4):
        _, vjp = jax.vjp(_disc_math, lr_ref[...], li_ref[...], ldt_ref[...], br_ref[...], bi_ref[...])
        grads = vjp((g0[...], g1[...], g2[...], g3[...]))
        for o, g in zip((o0, o1, o2, o3, o4), grads):
            o[...] = g

    col = jax.ShapeDtypeStruct(lr.shape, F32)
    mat = jax.ShapeDtypeStruct(br.shape, F32)
    return pl.pallas_call(body, name="s5_disc_bwd", out_shape=[col, col, col, mat, mat], compiler_params=_params())(
        lr, li, ldt, br, bi, da_re, da_im, dbb_re, dbb_im)


N_CHIP = 4


def _place():
    x, y, c = lax.axis_index("x"), lax.axis_index("y"), lax.axis_index("c")
    return (x, y, c), (x, y, 1 - c), [(1 - x, y), (x, 1 - y), (1 - x, 1 - y)]


def _lin(px, py, pc):
    return 4 * px + 2 * py + pc


def _remote(src, dst, sems, k, dev):
    return pltpu.make_async_remote_copy(src_ref=src, dst_ref=dst, send_sem=sems[0].at[k], recv_sem=sems[1].at[k],
                                        device_id=dev, device_id_type=pl.DeviceIdType.MESH)


def _hbm_call(body, name, srcs, out_shapes, nsem):
    n = len(srcs)
    hbm = pl.BlockSpec(memory_space=pltpu.HBM)

    def wrapped(*refs):
        body(refs[:n], refs[n:2 * n], (refs[2 * n], refs[2 * n + 1]), refs[2 * n + 2])

    return pl.pallas_call(
        wrapped, name=name, in_specs=[hbm] * n, out_specs=[hbm] * n, out_shape=out_shapes,
        scratch_shapes=[pltpu.SemaphoreType.DMA((nsem * n,)), pltpu.SemaphoreType.DMA((nsem * n,)),
                        pltpu.SemaphoreType.DMA((n,))],
        compiler_params=pltpu.CompilerParams(has_side_effects=True),
    )(*srcs)


GATHER_SEMS = 7
CHIP_SEMS = 3


def _gather_copies(s_refs, o_refs, sems, loc_sems):
    me, sib, chips = _place()
    c = me[2]
    out = []
    for g, (s_ref, o_ref) in enumerate(zip(s_refs, o_refs)):
        slot = lambda dev, o_ref=o_ref: o_ref.at[_lin(*dev)]
        k0 = GATHER_SEMS * g
        mine = pltpu.make_async_copy(s_ref, slot(me), loc_sems.at[g])
        first = [_remote(s_ref, slot(me), sems, k0, sib)]
        first += [_remote(s_ref, slot(me), sems, k0 + 1 + j, (*chip, c)) for j, chip in enumerate(chips)]
        passed = [_remote(slot((*chip, c)), slot((*chip, c)), sems, k0 + 4 + j, sib) for j, chip in enumerate(chips)]
        arrive = [_remote(s_ref, slot(sib), sems, k0, me)]
        arrive += [_remote(s_ref, slot((*chip, c)), sems, k0 + 1 + j, me) for j, chip in enumerate(chips)]
        arrive += [_remote(s_ref, slot((*chip, 1 - c)), sems, k0 + 4 + j, me) for j, chip in enumerate(chips)]
        out.append((mine, first, passed, arrive))
    return out


def _gather_start(*refs):
    for mine, first, _, _ in _gather_copies(*refs):
        mine.start()
        for cp in first:
            cp.start()


def _gather_relay(*refs):
    for _, _, passed, arrive in _gather_copies(*refs):
        for j, cp in enumerate(passed):
            arrive[1 + j].wait_recv()
            cp.start()


def _gather_finish(*refs):
    for mine, first, passed, arrive in _gather_copies(*refs):
        arrive[0].wait_recv()
        for cp in arrive[4:]:
            cp.wait_recv()
        for cp in first + passed:
            cp.wait_send()
        mine.wait()


def _gather_shapes(srcs):
    return [jax.ShapeDtypeStruct((N_DEV,) + s.shape, s.dtype) for s in srcs]


def _gather_all(name, srcs):
    def body(*refs):
        _gather_start(*refs)
        _gather_relay(*refs)
        _gather_finish(*refs)

    return _hbm_call(body, name, srcs, _gather_shapes(srcs), GATHER_SEMS)


def _pair_exchange(name, sends):
    def body(s_refs, o_refs, sems, loc_sems):
        me, sib, _ = _place()
        c = me[2]
        copies = [_remote(s_ref.at[2 * q + (1 - c)], o_ref.at[q], sems, N_CHIP * g + q, sib)
                  for g, (s_ref, o_ref) in enumerate(zip(s_refs, o_refs)) for q in range(N_CHIP)]
        for cp in copies:
            cp.start()
        for cp in copies:
            cp.wait()

    return _hbm_call(body, name, sends, [jax.ShapeDtypeStruct((N_CHIP,) + s.shape[1:], s.dtype) for s in sends],
                     N_CHIP)


def _row_block(r, cap):
    best = 8
    for m in range(8, min(r, cap) + 1, 8):
        if r % m == 0:
            best = m
    return best


def _pair_sum(name, send, got):
    _, r, cols = send.shape
    bm = _row_block(r, max(8, 256 * 1024 // cols))
    core = lax.axis_index("c").astype(jnp.int32).reshape(1)

    def body(core_ref, a_ref, b_ref, o_ref):
        o_ref[...] = a_ref[...] + b_ref[...]

    grid_spec = pltpu.PrefetchScalarGridSpec(
        num_scalar_prefetch=1, grid=(N_CHIP, r // bm),
        in_specs=[pl.BlockSpec((1, bm, cols), lambda q, i, cr: (2 * q + cr[0], i, 0)),
                  pl.BlockSpec((1, bm, cols), lambda q, i, cr: (q, i, 0))],
        out_specs=pl.BlockSpec((1, bm, cols), lambda q, i, cr: (q, i, 0)))
    return pl.pallas_call(body, name=name, grid_spec=grid_spec,
                          out_shape=jax.ShapeDtypeStruct((N_CHIP, r, cols), F32), compiler_params=_params())(
        core, send, got)


def _chip_exchange(name, parts):
    def body(*refs):
        _chip_start(*refs)
        _chip_finish(*refs)

    return _hbm_call(body, name, parts, _chip_shapes(parts), CHIP_SEMS)


def _chip_copies(p_refs, o_refs, sems, loc_sems):
    me, _, chips = _place()
    c = me[2]
    chip_id = lambda chip: 2 * chip[0] + chip[1]
    my_chip = chip_id(me)
    out = []
    for g, (p_ref, o_ref) in enumerate(zip(p_refs, o_refs)):
        mine = pltpu.make_async_copy(p_ref.at[my_chip], o_ref.at[my_chip], loc_sems.at[g])
        send = [_remote(p_ref.at[chip_id(chip)], o_ref.at[my_chip], sems, CHIP_SEMS * g + j, (*chip, c))
                for j, chip in enumerate(chips)]
        arrive = [_remote(p_ref.at[my_chip], o_ref.at[chip_id(chip)], sems, CHIP_SEMS * g + j, me)
                  for j, chip in enumerate(chips)]
        out.append((mine, send, arrive))
    return out


def _chip_start(*refs):
    for mine, send, _ in _chip_copies(*refs):
        mine.start()
        for cp in send:
            cp.start()


def _chip_finish(*refs):
    for mine, send, arrive in _chip_copies(*refs):
        for cp in arrive:
            cp.wait_recv()
        for cp in send:
            cp.wait_send()
        mine.wait()


def _chip_shapes(parts):
    return [jax.ShapeDtypeStruct(p.shape, p.dtype) for p in parts]


def _adamw(name, rcv, w, m, v):
    r, c = w.shape
    nslot = rcv.shape[0]
    bm = _row_block(r, max(8, 256 * 1024 // c))

    def body(rcv_ref, w_ref, m_ref, v_ref, g_ref, d_ref, m2_ref, v2_ref):
        g = rcv_ref[0]
        for s in range(1, nslot):
            g = g + rcv_ref[s]
        m2 = ADAM_B1 * m_ref[...] + (1.0 - ADAM_B1) * g
        v2 = ADAM_B2 * v_ref[...] + (1.0 - ADAM_B2) * (g * g)
        m_hat = m2 / (1.0 - ADAM_B1 ** ADAM_STEP)
        v_hat = v2 / (1.0 - ADAM_B2 ** ADAM_STEP)
        g_ref[...] = g
        d_ref[...] = -ADAM_LR * (m_hat / (jnp.sqrt(v_hat) + ADAM_EPS) + ADAM_WD * w_ref[...])
        m2_ref[...] = m2
        v2_ref[...] = v2

    blk = pl.BlockSpec((bm, c), lambda i: (i, 0))
    out = jax.ShapeDtypeStruct((r, c), F32)
    return pl.pallas_call(
        body, name=name, grid=(r // bm,),
        in_specs=[pl.BlockSpec((nslot, bm, c), lambda i: (0, i, 0)), blk, blk, blk], out_specs=[blk] * 4,
        out_shape=[out] * 4, compiler_params=_params(),
    )(rcv, w, m, v)


IN_SHARD = IN_WIDTH // N_DEV
UP_SHARD = 2 * D_FF // N_DEV
GROUPS = (
    ("g128a", LANES, (("w_q_b", Q_LORA, D_QK), ("w_kv_b", KV_LORA, D_NOPE + D_V))),
    ("g128b", LANES, (("w_o_mla", MLA_HEADS * D_V, LANES), ("w_o_ssm", SSM_WIDTH, LANES),
                      ("w_o_cross", X_WIDTH, LANES))),
    ("g512", SSM_WIDTH, (("w_glu", SSM_WIDTH // N_DEV, SSM_WIDTH),)),
    ("g1024", D_MODEL, (("w_mem_kv", D_MODEL // N_DEV, D_MODEL), ("w_out", D_MODEL // N_DEV, D_MODEL),
                        ("w_down", D_FF // N_DEV, D_MODEL))),
    ("g640", 640, (("w_in", D_MODEL, IN_SHARD),)),
    ("g768", 768, (("w_up", D_MODEL, UP_SHARD),)),
    ("gconv", 768, (("conv_w", 3, UP_SHARD),)),
)
EARLY = ("g640", "g128a")
EARLY_GROUPS = tuple(g for g in GROUPS if g[0] in EARLY)
LATE_GROUPS = tuple(g for g in GROUPS if g[0] not in EARLY)
REP_W = 1024


def _rows8(a):
    return -(-a // 8) * 8


def _group_rows(params):
    off, r = {}, 0
    for n, a, _ in params:
        off[n] = r
        r += _rows8(a)
    return off


def _group_local(width, params, vals, li, dtype):
    return jnp.concatenate([jnp.pad(vals[n][li].astype(dtype), ((0, _rows8(a) - a), (0, width - b)))
                            for n, a, b in params], axis=0)


def _rep_rows(flat):
    n = flat.shape[-1]
    per = REP_W * 64
    tot = -(-n // per) * per
    flat = jnp.pad(flat, [(0, 0)] * (flat.ndim - 1) + [(0, tot - n)])
    return flat.reshape(flat.shape[:-1] + (tot // REP_W, REP_W))


IN_SEGS = (
    (0, Q_LORA + KV_LORA, "wqkv", 0),
    (Q_LORA + KV_LORA, Q_LORA + KV_LORA + D_ROPE, "wqkv", Q_LORA + KV_LORA + KR_LO),
    (Q_LORA + KV_LORA + D_ROPE, Q_LORA + KV_LORA + D_ROPE + SSM_WIDTH, "w_u", 0),
    (Q_LORA + KV_LORA + D_ROPE + SSM_WIDTH, Q_LORA + KV_LORA + D_ROPE + SSM_WIDTH + X_WIDTH, "w_xq", 0),
    (Q_LORA + KV_LORA + D_ROPE + SSM_WIDTH + X_WIDTH, IN_WIDTH, "w_g", 0),
)
IN_PARTS = (("wqkv", QKV_W), ("w_u", SSM_WIDTH), ("w_xq", X_WIDTH), ("w_g", 3 * D_MODEL))


def _in_pieces():
    out = []
    for d in range(N_DEV):
        for lo, hi, part, dst in IN_SEGS:
            s, e = max(lo, d * IN_SHARD), min(hi, (d + 1) * IN_SHARD)
            if s < e:
                out.append((d, s - d * IN_SHARD, e - s, [p for p, _ in IN_PARTS].index(part), dst + s - lo))
    return out


def _unpack_w_in(gathered, row0, bm=256):
    def body(x_ref, *o_refs):
        o_refs[0][...] = jnp.zeros_like(o_refs[0])
        for d, src, n, part, dst in _in_pieces():
            o_refs[part][:, dst:dst + n] = x_ref[d, :, src:src + n]

    return pl.pallas_call(
        body, name="unpack_w_in", grid=(D_MODEL // bm,),
        in_specs=[pl.BlockSpec((N_DEV, bm, gathered.shape[2]), lambda i: (0, row0 // bm + i, 0))],
        out_specs=[pl.BlockSpec((bm, w), lambda i: (i, 0)) for _, w in IN_PARTS],
        out_shape=[jax.ShapeDtypeStruct((D_MODEL, w), gathered.dtype) for _, w in IN_PARTS], compiler_params=_params(),
    )(gathered)


def _pack_w_in(parts, bm=256):
    def body(*refs):
        o_ref = refs[-1]
        o_ref[...] = jnp.zeros_like(o_ref)
        for d, src, n, part, dst in _in_pieces():
            o_ref[d, :, src:src + n] = refs[part][:, dst:dst + n]

    return pl.pallas_call(
        body, name="pack_w_in", grid=(D_MODEL // bm,),
        in_specs=[pl.BlockSpec((bm, w), lambda i: (i, 0)) for _, w in IN_PARTS],
        out_specs=pl.BlockSpec((N_DEV, bm, 640), lambda i: (0, i, 0)),
        out_shape=jax.ShapeDtypeStruct((N_DEV, D_MODEL, 640), F32), compiler_params=_params(),
    )(*parts)


def _unpack_w_up(gathered, row0, bm=256):
    def body(x_ref, o_ref):
        for d in range(N_DEV):
            o_ref[:, d * UP_SHARD:(d + 1) * UP_SHARD] = x_ref[d, :, :UP_SHARD]

    return pl.pallas_call(
        body, name="unpack_w_up", grid=(D_MODEL // bm,),
        in_specs=[pl.BlockSpec((N_DEV, bm, gathered.shape[2]), lambda i: (0, row0 // bm + i, 0))],
        out_specs=pl.BlockSpec((bm, 2 * D_FF), lambda i: (i, 0)),
        out_shape=jax.ShapeDtypeStruct((D_MODEL, 2 * D_FF), gathered.dtype), compiler_params=_params(),
    )(gathered)


def _pack_w_up(dw_g, dw_v, bm=256):
    half = N_DEV // 2

    def body(g_ref, v_ref, o_ref):
        o_ref[...] = jnp.zeros_like(o_ref)
        for d in range(N_DEV):
            src = g_ref if d < half else v_ref
            c0 = (d % half) * UP_SHARD
            o_ref[d, :, :UP_SHARD] = src[:, c0:c0 + UP_SHARD]

    blk = pl.BlockSpec((bm, D_FF), lambda i: (i, 0))
    return pl.pallas_call(
        body, name="pack_w_up", grid=(D_MODEL // bm,), in_specs=[blk, blk],
        out_specs=pl.BlockSpec((N_DEV, bm, 768), lambda i: (0, i, 0)),
        out_shape=jax.ShapeDtypeStruct((N_DEV, D_MODEL, 768), F32), compiler_params=_params(),
    )(dw_g, dw_v)


def _cols_to_rows(full):
    a, nb = full.shape
    return full.reshape(a, N_DEV, nb // N_DEV).transpose(1, 0, 2)


def _rows_to_cols(blocks):
    n, a, b = blocks.shape
    return blocks.transpose(1, 0, 2).reshape(a, n * b)


def _block_diag_in(bb):
    b3 = bb.reshape(SSM_GROUPS, SSM_STATE, SSM_GROUP_CH).transpose(0, 2, 1)
    eye = jnp.eye(SSM_GROUPS, dtype=bb.dtype)
    return (b3[:, :, None, :] * eye[:, None, :, None]).reshape(SSM_WIDTH, SSM_N)


def _block_diag_out(cc):
    c3 = cc.transpose(0, 2, 1)
    eye = jnp.eye(SSM_GROUPS, dtype=cc.dtype)
    return (c3[:, :, None, :] * eye[:, None, :, None]).reshape(SSM_N, SSM_WIDTH)


def _diag_blocks(mats, rows_per, cols_per):
    nw = mats.shape[0]
    per = SSM_GROUPS // nw
    m5 = mats.reshape(nw, per, rows_per, per, cols_per)
    eye = jnp.eye(per, dtype=mats.dtype)
    return jnp.sum(m5 * eye[None, :, None, :, None], axis=3).reshape(SSM_GROUPS, rows_per, cols_per)


def _blk(gath, offs, grp, n, a):
    r0 = offs[grp][n]
    return gath[grp][:, r0:r0 + a, :]


def _early_weights(gath, offs):
    o = dict(zip([p for p, _ in IN_PARTS], _unpack_w_in(gath["g640"], offs["g640"]["w_in"])))
    o["wqb"] = _rows_to_cols(_blk(gath, offs, "g128a", "w_q_b", Q_LORA))
    wkv = _rows_to_cols(_blk(gath, offs, "g128a", "w_kv_b", KV_LORA)).reshape(KV_LORA, MLA_HEADS, D_NOPE + D_V)
    o["wk"] = jnp.pad(wkv[:, :, :D_NOPE], ((0, 0), (0, 0), (0, LANES - D_NOPE))).reshape(KV_LORA, MLA_HEADS * LANES)
    o["wv"] = jnp.pad(wkv[:, :, D_NOPE:], ((0, 0), (0, 0), (0, LANES - D_V))).reshape(KV_LORA, MLA_HEADS * LANES)
    o["wv_t"] = o["wv"].T
    return o


def _late_weights(gath, offs):
    def blk(grp, n, a):
        return _blk(gath, offs, grp, n, a)

    o = {}
    wo = _rows_to_cols(blk("g128b", "w_o_mla", MLA_HEADS * D_V)).reshape(MLA_HEADS, D_V, D_MODEL)
    o["wo_mla"] = jnp.pad(wo, ((0, 0), (0, LANES - D_V), (0, 0))).reshape(MLA_HEADS * LANES, D_MODEL)
    o["w_o_ssm"] = _rows_to_cols(blk("g128b", "w_o_ssm", SSM_WIDTH))
    o["w_o_cross"] = _rows_to_cols(blk("g128b", "w_o_cross", X_WIDTH))
    o["w_glu"] = blk("g512", "w_glu", SSM_WIDTH // N_DEV).reshape(SSM_WIDTH, SSM_WIDTH)
    o["w_mem_kv"] = blk("g1024", "w_mem_kv", D_MODEL // N_DEV).reshape(D_MODEL, 2 * X_WIDTH)
    o["w_out"] = blk("g1024", "w_out", D_MODEL // N_DEV).reshape(D_MODEL, D_MODEL)
    o["w_down"] = blk("g1024", "w_down", D_FF // N_DEV).reshape(D_FF, D_MODEL)
    o["w_up"] = _unpack_w_up(gath["g768"], offs["g768"]["w_up"])
    o["w_up_g"], o["w_up_v"] = o["w_up"][:, :D_FF], o["w_up"][:, D_FF:]
    o["conv_w"] = _rows_to_cols(blk("gconv", "conv_w", 3)[:, :, :UP_SHARD])
    return o


def _row(v):
    return v.reshape(1, -1).astype(F32)


def _pad_lanes(v, n=LANES):
    return jnp.pad(v, (0, n - v.shape[0])).reshape(1, n).astype(F32)


def _layer_fwd(x, mem, tabs, w, p, side, late):
    l = x.shape[0]
    rope_c, rope_sa, rope_sb = tabs
    s = {"x": x}
    g_mix, g_qa, g_kva = _row(p["norm_mix_g"]), _row(p["q_a_norm_g"]), _row(p["kv_a_norm_g"])
    g_q, g_k = _pad_lanes(p["q_norm_g"]), _pad_lanes(p["k_norm_g"])

    (h,) = _rowwise("rms_mix", lambda i, r, c: (_rms_f(r[0][...], c[0][...], D_MODEL),), l, 512,
                    [(x, D_MODEL, 0, "row")], [g_mix], [(D_MODEL, BF16)])
    pqkv = _mm("proj_qkv", [(h, w["wqkv"])])
    u = _mm("proj_u", [(h, w["w_u"])])
    xq = _mm("proj_xq", [(h, w["w_xq"])])
    gl = _mm("proj_gate", [(h, w["w_g"])], bm=1024, bn_cap=1024)
    s.update(h=h, pqkv=pqkv, u=u, xq=xq, gl=gl)

    def prep_a(i, r, c):
        return (_rms_f(r[0][:, :Q_LORA], c[0][...], Q_LORA),
                _rms_f(r[0][:, Q_LORA:Q_LORA + KV_LORA], c[1][...], KV_LORA))

    nq, nkv = _rowwise("mla_prep_a", prep_a, l, 512, [(pqkv, QKV_W, 0, "row")], [g_qa, g_kva],
                       [(Q_LORA, BF16), (KV_LORA, BF16)])
    q_raw = _mm("mla_q_b", [(nq, w["wqb"])], bn_cap=1024)
    k_raw = _mm("mla_k_b", [(nkv, w["wk"])], bn_cap=1024)
    v_mla = _mm("mla_v_b", [(nkv, w["wv"])], out_dtype=BF16, bn_cap=1024)
    vt_mla = _mm("mla_vt_b", [(w["wv_t"], nkv)], trans_b=True, out_dtype=BF16, bn_cap=1024)

    def prep_b(i, r, c):
        q_ref, k_ref, kr_ref, c_ref, sa_ref, sb_ref = r
        rc, sa, sb, kr = c_ref[...], sa_ref[...], sb_ref[...], kr_ref[...]
        qs, ks = [], []
        for hd in range(MLA_HEADS):
            cols = slice(hd * LANES, (hd + 1) * LANES)
            qs.append(_rope_f(_rms_f(q_ref[:, cols], c[0][...], D_QK), rc, sa, sb))
            ks.append(_rope_f(_rms_f(k_ref[:, cols] + kr, c[1][...], D_QK), rc, sa, sb))
        return jnp.concatenate(qs, axis=1), jnp.concatenate(ks, axis=1)

    hw = MLA_HEADS * LANES
    kr_blk = (Q_LORA + KV_LORA) // LANES
    tab_ins = [(rope_c, LANES, 0, "row"), (rope_sa, LANES, 0, "row"), (rope_sb, LANES, 0, "row")]
    q, k = _rowwise("mla_prep_b", prep_b, l, 256,
                    [(q_raw, hw, 0, "row"), (k_raw, hw, 0, "row"), (pqkv, LANES, kr_blk, "row")] + tab_ins,
                    [g_q, g_k], [(hw, BF16), (hw, BF16)])
    o_a, lse_a, rode = _attn_fwd("mla_attn_fwd", q, k, vt_mla, qoff=0, koff=0, voff=0, heads=MLA_HEADS, causal=True,
                                 scale=D_QK ** -0.5, bq=1024, bk=1024, side=side)
    w_late, rode = late(rode)
    w = {**w, **w_late}
    ya = _mm("mla_o", [(o_a, w["wo_mla"])], bn_cap=1024)
    s.update(nq=nq, nkv=nkv, q_raw=q_raw, k_raw=k_raw, v_mla=v_mla, q=q, k=k, o_a=o_a, lse_a=lse_a, ya=ya)

    lr = p["ssm_lambda_re"].reshape(SSM_N, 1)
    li = p["ssm_lambda_im"].reshape(SSM_N, 1)
    ldt = jnp.repeat(p["ssm_log_dt"], SSM_STATE).reshape(SSM_N, 1)
    br = p["ssm_b_re"].reshape(SSM_N, SSM_GROUP_CH)
    bi = p["ssm_b_im"].reshape(SSM_N, SSM_GROUP_CH)
    a_re, a_im, bb_re, bb_im = _disc_fwd(lr, li, ldt, br, bi)
    bb_re_d, bb_im_d = _block_diag_in(bb_re).astype(BF16), _block_diag_in(bb_im).astype(BF16)
    cc_re_d = _block_diag_out(p["ssm_c_re"]).astype(BF16)
    cc_imn_d = _block_diag_out(-p["ssm_c_im"]).astype(BF16)
    a_re_row, a_im_row = a_re.reshape(1, SSM_N), a_im.reshape(1, SSM_N)
    d_row = _row(p["ssm_d"])
    b_glu = _row(p["b_glu"])
    s_re, s_im, ypre = _s5_fwd("s5_fwd", u, bb_re_d, bb_im_d, cc_re_d, cc_imn_d, a_re_row, a_im_row)

    def ssm_y(i, r, c):
        return (_gelu(r[0][...] + c[0][...] * r[1][...]),)

    (y_b,) = _rowwise("s5_gelu", ssm_y, l, 512, [(ypre, SSM_WIDTH, 0, "row"), (u, SSM_WIDTH, 0, "row")], [d_row],
                      [(SSM_WIDTH, BF16)])
    z = _mm("s5_glu", [(y_b, w["w_glu"])])

    def ssm_out(i, r, c):
        y = _gelu(r[0][...] + c[0][...] * r[1][...])
        return (y * jax.nn.sigmoid(r[2][...] + c[1][...]),)

    (out_b,) = _rowwise("s5_glu_out", ssm_out, l, 512,
                        [(ypre, SSM_WIDTH, 0, "row"), (u, SSM_WIDTH, 0, "row"), (z, SSM_WIDTH, 0, "row")],
                        [d_row, b_glu], [(SSM_WIDTH, BF16)])
    yb = _mm("s5_o", [(out_b, w["w_o_ssm"])], bn_cap=1024)
    s.update(disc=(lr, li, ldt, br, bi), a_rows=(a_re_row, a_im_row), bb_d=(bb_re_d, bb_im_d),
             cc_d=(cc_re_d, cc_imn_d), s_re=s_re, s_im=s_im, ypre=ypre, y_b=y_b, z=z, out_b=out_b, yb=yb)

    g_mem, g_xq, g_xk = _row(p["mem_norm_g"]), _row(p["xq_norm_g"]), _row(p["xk_norm_g"])
    ml = mem.shape[0]
    (memn,) = _rowwise("rms_mem", lambda i, r, c: (_rms_f(r[0][...], c[0][...], D_MODEL),), ml, 256,
                       [(mem, D_MODEL, 0, "row")], [g_mem], [(D_MODEL, BF16)])
    kvm = _mm("cross_kv", [(memn, w["w_mem_kv"])], bn_cap=1024)

    def headnorm(i, r, c):
        return (jnp.concatenate([_rms_f(r[0][:, hd * LANES:(hd + 1) * LANES], c[0][...], X_HEAD_DIM)
                                 for hd in range(X_HEADS)], axis=1),)

    (xk,) = _rowwise("cross_k_norm", headnorm, ml, 256, [(kvm, X_WIDTH, 0, "row")], [g_xk], [(X_WIDTH, BF16)])
    (xqn,) = _rowwise("cross_q_norm", headnorm, l, 512, [(xq, X_WIDTH, 0, "row")], [g_xq], [(X_WIDTH, BF16)])
    xvt = kvm[:, X_WIDTH:].T.astype(BF16)
    o_c, lse_c, _ = _attn_fwd("cross_attn_fwd", xqn, xk, xvt, qoff=0, koff=0, voff=0, heads=X_HEADS,
                              causal=False, scale=X_HEAD_DIM ** -0.5, bq=1024, bk=256)
    yc = _mm("cross_o", [(o_c, w["w_o_cross"])], bn_cap=1024)
    s.update(memn=memn, kvm=kvm, xk=xk, xqn=xqn, o_c=o_c, lse_c=lse_c, yc=yc)

    b_gate = _row(p["b_gate"])

    def merge(i, r, c):
        acc = None
        for br_ in range(3):
            g = jax.nn.sigmoid(r[br_][...] + c[0][:, br_ * D_MODEL:(br_ + 1) * D_MODEL])
            t = g * r[3 + br_][...]
            acc = t if acc is None else acc + t
        return (acc,)

    gate_ins = [(gl, D_MODEL, b_, "row") for b_ in range(3)]
    (merged,) = _rowwise("merge", merge, l, 256,
                         gate_ins + [(ya, D_MODEL, 0, "row"), (yb, D_MODEL, 0, "row"), (yc, D_MODEL, 0, "row")],
                         [b_gate], [(D_MODEL, BF16)])
    x1 = _mm("mix_out", [(merged, w["w_out"])], add=x, bn_cap=1024)
    s.update(merged=merged, x1=x1)

    g_ffn = _row(p["norm_ffn_g"])
    (h2,) = _rowwise("rms_ffn", lambda i, r, c: (_rms_f(r[0][...], c[0][...], D_MODEL),), l, 512,
                     [(x1, D_MODEL, 0, "row")], [g_ffn], [(D_MODEL, BF16)])
    up = _mm("ffn_up", [(h2, w["w_up"])], bm=1024, bn_cap=1408)
    conv_w = w["conv_w"]
    conv_b = _row(p["conv_b"])

    def conv_glu(i, r, c):
        cg = _conv(r[0], r[2], i, c[0], c[1], 0)
        cv = _conv(r[1], r[3], i, c[0], c[1], D_FF)
        return (cg * jax.nn.sigmoid(cg) * cv,)

    up_ins = [(up, D_FF, 0, "row"), (up, D_FF, 1, "row"), (up, D_FF, 0, "prev"), (up, D_FF, 1, "prev")]
    (act,) = _rowwise("ffn_conv_glu", conv_glu, l, 256, up_ins, [conv_w, conv_b], [(D_FF, BF16)])
    x2 = _mm("ffn_down", [(act, w["w_down"])], add=x1, bm=1024, bn_cap=1024)
    s.update(h2=h2, up=up, act=act, conv_w=conv_w, conv_b=conv_b)
    return x2, s, w, rode


def _conv(x_ref, halo_ref, i, w_ref, b_ref, col0):
    x = x_ref[...]
    cols = slice(col0, col0 + D_FF)
    return (w_ref[0:1, cols] * _shift_down(x, halo_ref, i, 2) + w_ref[1:2, cols] * _shift_down(x, halo_ref, i, 1)
            + w_ref[2:3, cols] * x + b_ref[:, cols])


def _layer_bwd(dx2, dx2_b, s, mem, tabs, w, p, make_side):
    l = dx2.shape[0]
    rope_c, rope_sa, rope_sb = tabs
    x, x1 = s["x"], s["x1"]
    g = {}

    dact = _mm("ffn_down_dx", [(dx2_b, w["w_down"])], trans_b=True, bn_cap=1408)
    g["w_down"] = _mm_tn("ffn_down_dw", s["act"], dx2_b, bm_cap=1408).reshape(N_DEV, D_FF // N_DEV, D_MODEL)
    up = s["up"]
    nblk_c = l // min(256, l)

    bm_c = min(256, l)

    def conv_bwd(i, r, c):
        da_ref, xg_ref, xv_ref, hg_ref, hv_ref, dan_ref, ng_ref, nv_ref = r
        row8 = _row_ids(8)
        live_next = (i < nblk_c - 1).astype(F32)

        def conv_rows(x, xm1, xm2, cols):
            return c[0][0:1, cols] * xm2 + c[0][1:2, cols] * xm1 + c[0][2:3, cols] * x + c[1][:, cols]

        def tail_shift(x8, x_ref, k):
            out = pltpu.roll(x8, k, 0)
            for q in range(k):
                e = (row8 == q).astype(F32)
                out = out * (1.0 - e) + e * x_ref[bm_c - k + q:bm_c - k + q + 1, :]
            return out

        def glu_grads(da, cg, cv):
            sig = jax.nn.sigmoid(cg)
            return da * cv * (sig * (1.0 + cg * (1.0 - sig))), da * (cg * sig)

        def up_shift(d, d8, k):
            out = pltpu.roll(d, bm_c - k, 0)
            last = out[bm_c - 8:]
            for q in range(k):
                e = (row8 == 8 - k + q).astype(F32)
                nxt = jnp.sum(d8 * (row8 == q).astype(F32), axis=0, keepdims=True)
                last = last * (1.0 - e) + e * nxt
            return jnp.concatenate([out[:bm_c - 8], last], axis=0)

        halves = []
        for half, (x_ref, h_ref, n_ref) in enumerate(((xg_ref, hg_ref, ng_ref), (xv_ref, hv_ref, nv_ref))):
            cols = slice(half * D_FF, (half + 1) * D_FF)
            x, x8 = x_ref[...], n_ref[...]
            xm1, xm2 = _shift_down(x, h_ref, i, 1), _shift_down(x, h_ref, i, 2)
            halves.append((conv_rows(x, xm1, xm2, cols), x, xm1, xm2,
                           conv_rows(x8, tail_shift(x8, x_ref, 1), tail_shift(x8, x_ref, 2), cols)))
        (cg, xg, xg1, xg2, cg8), (cvv, xv, xv1, xv2, cv8) = halves
        dcg, dcv = glu_grads(da_ref[...], cg, cvv)
        dcg8, dcv8 = glu_grads(dan_ref[...] * live_next, cg8, cv8)
        outs, accs = [], []
        for half, (d, d8, (x0, xa, xb)) in enumerate(((dcg, dcg8, (xg, xg1, xg2)), (dcv, dcv8, (xv, xv1, xv2)))):
            cols = slice(half * D_FF, (half + 1) * D_FF)
            accs += [jnp.sum(d, axis=0, keepdims=True), jnp.sum(d * xb, axis=0, keepdims=True),
                     jnp.sum(d * xa, axis=0, keepdims=True), jnp.sum(d * x0, axis=0, keepdims=True)]
            outs.append(c[0][2:3, cols] * d + c[0][1:2, cols] * up_shift(d, d8, 1) + c[0][0:1, cols] * up_shift(d, d8, 2))
        return (*outs, *accs)

    conv_ins = [(dact, D_FF, 0, "row"), (up, D_FF, 0, "row"), (up, D_FF, 1, "row"), (up, D_FF, 0, "prev"),
                (up, D_FF, 1, "prev"), (dact, D_FF, 0, "next"), (up, D_FF, 0, "next"), (up, D_FF, 1, "next")]
    res = _rowwise("ffn_conv_glu_bwd", conv_bwd, l, bm_c, conv_ins, [s["conv_w"], s["conv_b"]],
                   [(D_FF, BF16), (D_FF, BF16)], [(1, D_FF)] * 8)
    dup_g, dup_v = res[0], res[1]
    db_g, dw0_g, dw1_g, dw2_g, db_v, dw0_v, dw1_v, dw2_v = res[2:]
    g["conv_b"] = jnp.concatenate([db_g, db_v], axis=1)[0]
    g["conv_w"] = _cols_to_rows(jnp.concatenate(
        [jnp.concatenate([dw0_g, dw0_v], axis=1), jnp.concatenate([dw1_g, dw1_v], axis=1),
         jnp.concatenate([dw2_g, dw2_v], axis=1)], axis=0))

    dh2 = _mm("ffn_up_dx", [(dup_g, w["w_up_g"]), (dup_v, w["w_up_v"])], trans_b=True, bn_cap=1024)
    g["w_up"] = _pack_w_up(_mm_tn("ffn_up_dw_g", s["h2"], dup_g), _mm_tn("ffn_up_dw_v", s["h2"], dup_v))

    def rms_bwd_res(i, r, c):
        dx, dg = _rms_b(r[0][...], c[0][...], r[1][...], D_MODEL)
        dx = dx + r[2][...]
        return (dx, dx, dg)

    dx1, dx1_b, g["norm_ffn_g"] = _rowwise(
        "rms_ffn_bwd", rms_bwd_res, l, 512, [(x1, D_MODEL, 0, "row"), (dh2, D_MODEL, 0, "row"), (dx2, D_MODEL, 0, "row")],
        [_row(p["norm_ffn_g"])], [(D_MODEL, F32), (D_MODEL, BF16)], [(1, D_MODEL)])

    dmerged = _mm("mix_out_dx", [(dx1_b, w["w_out"])], trans_b=True, bn_cap=1024)
    g["w_out"] = _mm_tn("mix_out_dw", s["merged"], dx1_b).reshape(N_DEV, D_MODEL // N_DEV, D_MODEL)
    gl, ya, yb, yc = s["gl"], s["ya"], s["yb"], s["yc"]

    def merge_bwd(i, r, c):
        dm = r[0][...]
        dys, dgs = [], []
        for b_ in range(3):
            gate = jax.nn.sigmoid(r[1 + b_][...] + c[0][:, b_ * D_MODEL:(b_ + 1) * D_MODEL])
            dys.append(dm * gate)
            dgs.append(dm * r[4 + b_][...] * (gate * (1.0 - gate)))
        dgl = jnp.concatenate(dgs, axis=1)
        return (*dys, dgl, jnp.sum(dgl, axis=0, keepdims=True))

    gate_ins = [(gl, D_MODEL, b_, "row") for b_ in range(3)]
    dya, dyb, dyc, dgl, db_gate = _rowwise(
        "merge_bwd", merge_bwd, l, 256,
        [(dmerged, D_MODEL, 0, "row")] + gate_ins + [(ya, D_MODEL, 0, "row"), (yb, D_MODEL, 0, "row"),
                                                     (yc, D_MODEL, 0, "row")],
        [_row(p["b_gate"])], [(D_MODEL, BF16)] * 3 + [(3 * D_MODEL, BF16)], [(1, 3 * D_MODEL)])
    g["b_gate"] = db_gate[0]

    do_c = _mm("cross_o_dx", [(dyc, w["w_o_cross"])], trans_b=True, out_dtype=BF16)
    g["w_o_cross"] = _cols_to_rows(_mm_tn("cross_o_dw", s["o_c"], dyc))
    kvm = s["kvm"]
    delta_c = _attn_delta("cross_attn_delta", s["o_c"], do_c, heads=X_HEADS, bq=2048)
    dxqn, dxk, dxv, _ = _attn_bwd("cross_attn_bwd", s["xqn"], s["xk"], s["xk"].T, kvm, do_c, s["lse_c"], delta_c,
                                  qoff=0, koff=0, voff=X_HEADS, heads=X_HEADS, causal=False,
                                  scale=X_HEAD_DIM ** -0.5, bq=1024, bk=256)
    ml = mem.shape[0]

    def headnorm_bwd(i, r, c):
        dxs, dg = [], None
        for hd in range(X_HEADS):
            cols = slice(hd * LANES, (hd + 1) * LANES)
            dx_h, dg_h = _rms_b(r[0][:, cols], c[0][...], r[1][:, cols], X_HEAD_DIM)
            dxs.append(dx_h)
            dg = dg_h if dg is None else dg + dg_h
        return (jnp.concatenate(dxs, axis=1), dg)

    dxq, dg_xq = _rowwise("cross_q_norm_bwd", headnorm_bwd, l, 512,
                          [(s["xq"], X_WIDTH, 0, "row"), (dxqn, X_WIDTH, 0, "row")], [_row(p["xq_norm_g"])],
                          [(X_WIDTH, BF16)], [(1, X_HEAD_DIM)])
    dkvm_k, dg_xk = _rowwise("cross_k_norm_bwd", headnorm_bwd, ml, 256,
                             [(kvm, X_WIDTH, 0, "row"), (dxk, X_WIDTH, 0, "row")], [_row(p["xk_norm_g"])],
                             [(X_WIDTH, F32)], [(1, X_HEAD_DIM)])
    g["xq_norm_g"], g["xk_norm_g"] = dg_xq[0], dg_xk[0]
    dkvm = jnp.concatenate([dkvm_k, dxv], axis=1)
    g["w_mem_kv"] = _mm_tn("cross_kv_dw", s["memn"], dkvm).reshape(N_DEV, D_MODEL // N_DEV, 2 * X_WIDTH)
    dmemn = _mm("cross_kv_dx", [(dkvm, w["w_mem_kv"])], trans_b=True, bn_cap=1024)

    def rms_bwd_gain_only(i, r, c):
        return (_rms_b(r[0][...], c[0][...], r[1][...], D_MODEL)[1],)

    (dg_mem,) = _rowwise("rms_mem_bwd", rms_bwd_gain_only, ml, 256,
                         [(mem, D_MODEL, 0, "row"), (dmemn, D_MODEL, 0, "row")], [_row(p["mem_norm_g"])], [],
                         [(1, D_MODEL)])
    g["mem_norm_g"] = dg_mem[0]

    dout_b = _mm("s5_o_dx", [(dyb, w["w_o_ssm"])], trans_b=True)
    g["w_o_ssm"] = _cols_to_rows(_mm_tn("s5_o_dw", s["out_b"], dyb))
    ypre, u, z = s["ypre"], s["u"], s["z"]
    d_row, b_glu = _row(p["ssm_d"]), _row(p["b_glu"])
    yuz = [(ypre, SSM_WIDTH, 0, "row"), (u, SSM_WIDTH, 0, "row"), (z, SSM_WIDTH, 0, "row")]

    def glu_bwd_z(i, r, c):
        y = _gelu(r[1][...] + c[0][...] * r[2][...])
        sg = jax.nn.sigmoid(r[3][...] + c[1][...])
        dz = r[0][...] * y * (sg * (1.0 - sg))
        return (dz, jnp.sum(dz, axis=0, keepdims=True))

    dz, db_glu = _rowwise("s5_glu_bwd_z", glu_bwd_z, l, 512, [(dout_b, SSM_WIDTH, 0, "row")] + yuz, [d_row, b_glu],
                          [(SSM_WIDTH, BF16)], [(1, SSM_WIDTH)])
    g["b_glu"] = db_glu[0]
    g["w_glu"] = _mm_tn("s5_glu_dw", s["y_b"], dz).reshape(N_DEV, SSM_WIDTH // N_DEV, SSM_WIDTH)
    dy2 = _mm("s5_glu_dx", [(dz, w["w_glu"])], trans_b=True)

    def gelu_bwd(i, r, c):
        t = r[2][...] + c[0][...] * r[3][...]
        sg = jax.nn.sigmoid(r[4][...] + c[1][...])
        dt = (r[0][...] * sg + r[1][...]) * _gelu_grad(t)
        return (dt, c[0][...] * dt, jnp.sum(dt * r[3][...], axis=0, keepdims=True))

    dypre, du_skip, dd = _rowwise(
        "s5_gelu_bwd", gelu_bwd, l, 512, [(dout_b, SSM_WIDTH, 0, "row"), (dy2, SSM_WIDTH, 0, "row")] + yuz,
        [d_row, b_glu], [(SSM_WIDTH, BF16), (SSM_WIDTH, F32)], [(1, SSM_WIDTH)])
    g["ssm_d"] = dd.reshape(SSM_GROUPS, SSM_GROUP_CH)
    cc_re_d, cc_imn_d = s["cc_d"]
    bb_re_d, bb_im_d = s["bb_d"]
    a_re_row, a_im_row = s["a_rows"]
    s_re, s_im = s["s_re"], s["s_im"]
    du, da_re, da_im, dbb_re_d, dbb_im_d, dcc_re, dcc_imn = _s5_bwd(
        "s5_bwd", dypre, du_skip, u, s_re, s_im, bb_re_d, bb_im_d, cc_re_d, cc_imn_d, a_re_row, -a_im_row)
    g["ssm_c_re"] = _diag_blocks(dcc_re, SSM_STATE, SSM_GROUP_CH).transpose(0, 2, 1)
    g["ssm_c_im"] = -_diag_blocks(dcc_imn, SSM_STATE, SSM_GROUP_CH).transpose(0, 2, 1)
    dbb_re = _diag_blocks(dbb_re_d, SSM_GROUP_CH, SSM_STATE).transpose(0, 2, 1).reshape(SSM_N, SSM_GROUP_CH)
    dbb_im = _diag_blocks(dbb_im_d, SSM_GROUP_CH, SSM_STATE).transpose(0, 2, 1).reshape(SSM_N, SSM_GROUP_CH)
    dlr, dli, dldt, dbr, dbi = _disc_bwd(*s["disc"], da_re.reshape(SSM_N, 1), da_im.reshape(SSM_N, 1), dbb_re, dbb_im)
    g["ssm_lambda_re"] = dlr.reshape(SSM_GROUPS, SSM_STATE)
    g["ssm_lambda_im"] = dli.reshape(SSM_GROUPS, SSM_STATE)
    g["ssm_log_dt"] = dldt.reshape(SSM_GROUPS, SSM_STATE).sum(axis=1)
    g["ssm_b_re"] = dbr.reshape(SSM_GROUPS, SSM_STATE, SSM_GROUP_CH)
    g["ssm_b_im"] = dbi.reshape(SSM_GROUPS, SSM_STATE, SSM_GROUP_CH)

    do_a = _mm("mla_o_dx", [(dya, w["wo_mla"])], trans_b=True, out_dtype=BF16, bn_cap=1024)
    dwo = _mm_tn("mla_o_dw", s["o_a"], dya)
    g["w_o_mla"] = _cols_to_rows(dwo.reshape(MLA_HEADS, LANES, D_MODEL)[:, :D_V].reshape(MLA_HEADS * D_V, D_MODEL))
    delta_a = _attn_delta("mla_attn_delta", s["o_a"], do_a, heads=MLA_HEADS, bq=2048)
    dq, dk, dv, rode = _attn_bwd("mla_attn_bwd", s["q"], s["k"], s["k"].T, s["v_mla"], do_a, s["lse_a"], delta_a,
                                 qoff=0, koff=0, voff=0, heads=MLA_HEADS, causal=True, scale=D_QK ** -0.5, bq=1024,
                                 bk=1024, side=make_side(g))
    hw = MLA_HEADS * LANES
    kr_blk = (Q_LORA + KV_LORA) // LANES
    pqkv = s["pqkv"]
    g_q, g_k = _pad_lanes(p["q_norm_g"]), _pad_lanes(p["k_norm_g"])
    lane = lax.broadcasted_iota(jnp.int32, (1, LANES), 1)
    kr_mask = jnp.logical_and(lane >= KR_LO, lane < KR_LO + D_ROPE).astype(F32)

    def prep_b_bwd(i, r, c):
        dq_ref, dk_ref, q_ref, k_ref, kr_ref, c_ref, sa_ref, sb_ref = r
        rc, sa, sb, kr = c_ref[...], sa_ref[...], sb_ref[...], kr_ref[...]
        dqs, dks, dkr, dgq, dgk = [], [], None, None, None
        for hd in range(MLA_HEADS):
            cols = slice(hd * LANES, (hd + 1) * LANES)
            dxq, dgq_h = _rms_b(q_ref[:, cols], c[0][...], _rope_b(dq_ref[:, cols], rc, sa, sb), D_QK)
            dxk, dgk_h = _rms_b(k_ref[:, cols] + kr, c[1][...], _rope_b(dk_ref[:, cols], rc, sa, sb), D_QK)
            dqs.append(dxq)
            dks.append(dxk)
            dkr = dxk if dkr is None else dkr + dxk
            dgq = dgq_h if dgq is None else dgq + dgq_h
            dgk = dgk_h if dgk is None else dgk + dgk_h
        return (jnp.concatenate(dqs, axis=1), jnp.concatenate(dks, axis=1), dkr * c[2][...], dgq, dgk)

    tab_ins = [(rope_c, LANES, 0, "row"), (rope_sa, LANES, 0, "row"), (rope_sb, LANES, 0, "row")]
    dq_raw, dk_raw, dkr, dg_q, dg_k = _rowwise(
        "mla_prep_b_bwd", prep_b_bwd, l, 256,
        [(dq, hw, 0, "row"), (dk, hw, 0, "row"), (s["q_raw"], hw, 0, "row"), (s["k_raw"], hw, 0, "row"),
         (pqkv, LANES, kr_blk, "row")] + tab_ins, [g_q, g_k, kr_mask],
        [(hw, BF16), (hw, BF16), (LANES, F32)], [(1, LANES), (1, LANES)])
    g["q_norm_g"], g["k_norm_g"] = dg_q[0, :D_QK], dg_k[0, :D_QK]
    dnq = _mm("mla_q_b_dx", [(dq_raw, w["wqb"])], trans_b=True)
    dnkv = _mm("mla_kv_b_dx", [(dk_raw, w["wk"]), (dv, w["wv"])], trans_b=True)
    dwqb = _mm_tn("mla_q_b_dw", s["nq"], dq_raw)
    g["w_q_b"] = _cols_to_rows(dwqb)
    dwk = _mm_tn("mla_k_b_dw", s["nkv"], dk_raw).reshape(KV_LORA, MLA_HEADS, LANES)[:, :, :D_NOPE]
    dwv = _mm_tn("mla_v_b_dw", s["nkv"], dv).reshape(KV_LORA, MLA_HEADS, LANES)[:, :, :D_V]
    g["w_kv_b"] = jnp.concatenate([dwk, dwv], axis=2).transpose(1, 0, 2)

    def prep_a_bwd(i, r, c):
        dcq, dgqa = _rms_b(r[0][:, :Q_LORA], c[0][...], r[1][...], Q_LORA)
        dckv, dgkva = _rms_b(r[0][:, Q_LORA:Q_LORA + KV_LORA], c[1][...], r[2][...], KV_LORA)
        return (jnp.concatenate([dcq, dckv, r[3][...]], axis=1), dgqa, dgkva)

    dpqkv, dg_qa, dg_kva = _rowwise(
        "mla_prep_a_bwd", prep_a_bwd, l, 512,
        [(pqkv, QKV_W, 0, "row"), (dnq, Q_LORA, 0, "row"), (dnkv, KV_LORA, 0, "row"), (dkr, LANES, 0, "row")],
        [_row(p["q_a_norm_g"]), _row(p["kv_a_norm_g"])], [(QKV_W, BF16)], [(1, Q_LORA), (1, KV_LORA)])
    g["q_a_norm_g"], g["kv_a_norm_g"] = dg_qa[0], dg_kva[0]

    h = s["h"]
    dh = _mm("proj_dx", [(dpqkv, w["wqkv"]), (du, w["w_u"]), (dxq, w["w_xq"]), (dgl, w["w_g"])], trans_b=True,
             bn_cap=1024)
    dwqkv = _mm_tn("proj_qkv_dw", h, dpqkv)
    g["w_in"] = _pack_w_in([dwqkv, _mm_tn("proj_u_dw", h, du), _mm_tn("proj_xq_dw", h, dxq),
                            _mm_tn("proj_gate_dw", h, dgl)])
    dx, dx_b, dg_mix = _rowwise(
        "rms_mix_bwd", rms_bwd_res, l, 512, [(x, D_MODEL, 0, "row"), (dh, D_MODEL, 0, "row"), (dx1, D_MODEL, 0, "row")],
        [_row(p["norm_mix_g"])], [(D_MODEL, F32), (D_MODEL, BF16)], [(1, D_MODEL)])
    g["norm_mix_g"] = dg_mix[0]
    g["norm_ffn_g"] = g["norm_ffn_g"][0]
    return dx, dx_b, g, rode


def _rope_tables(positions):
    inv_freq = ROPE_THETA ** (-jnp.arange(0, D_ROPE, 2, dtype=F32) / D_ROPE)
    ang = positions.astype(F32)[:, None] * inv_freq
    cos, sin = jnp.cos(ang), jnp.sin(ang)
    l = positions.shape[0]
    one, zero = jnp.ones((l, D_NOPE), F32), lambda n: jnp.zeros((l, n), F32)
    pad = LANES - D_QK
    rope_c = jnp.concatenate([one, cos, cos, zero(pad)], axis=1)
    rope_sa = jnp.concatenate([zero(D_NOPE), -sin, zero(16), zero(pad)], axis=1)
    rope_sb = jnp.concatenate([zero(D_NOPE + 16), sin, zero(pad)], axis=1)
    return rope_c, rope_sa, rope_sb


def kernel(x, mem, positions, norm_mix_g, w_in, q_a_norm_g, w_q_b, kv_a_norm_g, w_kv_b, q_norm_g, k_norm_g, w_o_mla, ssm_lambda_re, ssm_lambda_im, ssm_log_dt, ssm_b_re, ssm_b_im, ssm_c_re, ssm_c_im, ssm_d, w_glu, b_glu, w_o_ssm, mem_norm_g, w_mem_kv, xq_norm_g, xk_norm_g, w_o_cross, b_gate, w_out, norm_ffn_g, w_up, conv_w, conv_b, w_down, loss_target, m_norm_mix_g, m_w_in, m_q_a_norm_g, m_w_q_b, m_kv_a_norm_g, m_w_kv_b, m_q_norm_g, m_k_norm_g, m_w_o_mla, m_ssm_lambda_re, m_ssm_lambda_im, m_ssm_log_dt, m_ssm_b_re, m_ssm_b_im, m_ssm_c_re, m_ssm_c_im, m_ssm_d, m_w_glu, m_b_glu, m_w_o_ssm, m_mem_norm_g, m_w_mem_kv, m_xq_norm_g, m_xk_norm_g, m_w_o_cross, m_b_gate, m_w_out, m_norm_ffn_g, m_w_up, m_conv_w, m_conv_b, m_w_down, v_norm_mix_g, v_w_in, v_q_a_norm_g, v_w_q_b, v_kv_a_norm_g, v_w_kv_b, v_q_norm_g, v_k_norm_g, v_w_o_mla, v_ssm_lambda_re, v_ssm_lambda_im, v_ssm_log_dt, v_ssm_b_re, v_ssm_b_im, v_ssm_c_re, v_ssm_c_im, v_ssm_d, v_w_glu, v_b_glu, v_w_o_ssm, v_mem_norm_g, v_w_mem_kv, v_xq_norm_g, v_xk_norm_g, v_w_o_cross, v_b_gate, v_w_out, v_norm_ffn_g, v_w_up, v_conv_w, v_conv_b, v_w_down):
    a = dict(locals())
    wts = {n: a[n] for n in WEIGHT_ORDER}
    m_in = {n: a["m_" + n] for n in WEIGHT_ORDER}
    v_in = {n: a["v_" + n] for n in WEIGHT_ORDER}
    depth = norm_mix_g.shape[0]
    x0, mem0, pos0, tgt = x[0], mem[0], positions[0], loss_target[0]
    l = x0.shape[0]
    offs = {grp: _group_rows(params) for grp, _, params in GROUPS}
    early_names, late_names = [g[0] for g in EARLY_GROUPS], [g[0] for g in LATE_GROUPS]

    def srcs(i, groups):
        return [_group_local(width, params, wts, i, F32 if grp == "gconv" else BF16) for grp, width, params in groups]

    tabs = _rope_tables(pos0)
    layer_p = [{n: wts[n][i] for n in REPLICATED} for i in range(depth)]
    gath = dict(zip(early_names, _gather_all("gather_weights", srcs(0, EARLY_GROUPS))))

    def late(rode):
        return _late_weights(dict(zip(late_names, rode)), offs), rode[len(late_names):]

    saved, layer_w = [], []
    xc = x0
    for i in range(depth):
        ride = srcs(i, LATE_GROUPS) + (srcs(i + 1, EARLY_GROUPS) if i + 1 < depth else [])
        side = _Side(ride, _gather_shapes(ride), GATHER_SEMS, (_gather_start, _gather_relay, _gather_finish))
        xc, s, w_all, rode = _layer_fwd(xc, mem0, tabs, _early_weights(gath, offs), layer_p[i], side, late)
        gath = dict(zip(early_names, rode))
        layer_w.append(w_all)
        saved.append(s)

    def loss_fn(i, r, c):
        d = r[0][...] - r[1][...]
        dy = d * (1.0 / D_MODEL)
        return (dy, dy, jnp.sum(d * d, axis=0, keepdims=True))

    dy, dy_b, sq = _rowwise("loss", loss_fn, l, 512, [(xc, D_MODEL, 0, "row"), (tgt, D_MODEL, 0, "row")], [],
                            [(D_MODEL, F32), (D_MODEL, BF16)], [(1, D_MODEL)])
    loss = lax.psum(0.5 * jnp.sum(sq) / D_MODEL, ("x", "y", "c"))

    def core_sums(g, tag, groups, extra=()):
        sends = []
        for _, width, params in groups:
            blocks = [jnp.pad(g[n], ((0, 0), (0, _rows8(rows) - rows), (0, width - g[n].shape[2])))
                      for n, rows, _ in params]
            sends.append(jnp.concatenate(blocks, axis=1))
        sends += list(extra)
        got = _pair_exchange("exchange_grads_core_" + tag, sends)
        return [_pair_sum("grad_pair_sum_%s_%d" % (tag, k), s_, g_) for k, (s_, g_) in enumerate(zip(sends, got))]

    grads, rcvs = [None] * depth, {}
    dxc, dxc_b = dy, dy_b
    waiting = []
    for i in reversed(range(depth)):
        def make_side(g, i=i, waiting=waiting):
            parts = core_sums(g, "l%d_late" % i, LATE_GROUPS) + waiting
            return _Side(parts, _chip_shapes(parts), CHIP_SEMS, (_chip_start, _chip_finish))

        dxc, dxc_b, grads[i], rode = _layer_bwd(dxc, dxc_b, saved[i], mem0, tabs, layer_w[i], layer_p[i], make_side)
        rcvs.update({(i, nm): r for nm, r in zip(late_names, rode)})
        rcvs.update({(i + 1, nm): r for nm, r in zip(early_names, rode[len(late_names):])})
        waiting = core_sums(grads[i], "l%d_early" % i, EARLY_GROUPS) if i > 0 else []
    grad_x = dxc[None]
    rep_flat = jnp.concatenate([jnp.stack([grads[i][n] for i in range(depth)]).reshape(-1) for n in REPLICATED])
    rep_send = _rep_rows(jnp.broadcast_to(rep_flat[None], (N_DEV, rep_flat.shape[0])))
    last = _chip_exchange("exchange_grads_chip", core_sums(grads[0], "l0_early", EARLY_GROUPS, [rep_send]))
    rcvs.update({(0, nm): r for nm, r in zip(early_names, last)})
    rcv_rep = last[-1]

    per_layer = {}
    for i in range(depth):
        for grp, width, params in GROUPS:
            local = [_group_local(width, params, d, i, F32) for d in (wts, m_in, v_in)]
            res = _adamw("adamw_l%d_%s" % (i, grp), rcvs[(i, grp)], *local)
            for tag, arr in zip(("grad", "delta", "m", "v"), res):
                for n, rows, cols in params:
                    per_layer[(tag, n, i)] = arr[offs[grp][n]:offs[grp][n] + rows, :cols]
    outs = {(tag, n): jnp.stack([per_layer[(tag, n, i)] for i in range(depth)])
            for tag in ("grad", "delta", "m", "v") for _, _, params in GROUPS for n, _, _ in params}
    rep_local = [_rep_rows(jnp.concatenate([d[n].astype(F32).reshape(-1) for n in REPLICATED])) for d in (wts, m_in, v_in)]
    res = _adamw("adamw_rep", rcv_rep, *rep_local)
    for tag, arr in zip(("grad", "delta", "m", "v"), res):
        flat, off = arr.reshape(-1), 0
        for n in REPLICATED:
            cnt = wts[n].size
            outs[(tag, n)] = flat[off:off + cnt].reshape(wts[n].shape)
            off += cnt
    result = [loss, grad_x]
    for tag in ("grad", "delta", "m", "v"):
        result += [outs[(tag, n)] for n in WEIGHT_ORDER]
    return tuple(result)
```

```python
import collections
import math

import jax
import jax.numpy as jnp
from jax import lax
from jax.experimental import pallas as pl
from jax.experimental.pallas import tpu as pltpu

F32 = jnp.float32
BF16 = jnp.bfloat16

N_DEV = 8
LANES = 128
LOG2E = math.log2(math.e)
VMEM_LIMIT_BYTES = 56 * 1024 * 1024

D_MODEL = 1024
EPS = 1e-6
MLA_HEADS = 8
Q_LORA = 384
KV_LORA = 256
D_NOPE = 64
D_ROPE = 32
D_QK = D_NOPE + D_ROPE
D_V = 64
ROPE_THETA = 10000.0
SSM_GROUPS = 32
SSM_GROUP_CH = 16
SSM_WIDTH = 512
SSM_STATE = 64
SSM_N = SSM_GROUPS * SSM_STATE
X_HEADS = 4
X_HEAD_DIM = 128
X_WIDTH = 512
D_FF = 2816
IN_WIDTH = Q_LORA + KV_LORA + D_ROPE + SSM_WIDTH + X_WIDTH + 3 * D_MODEL
QKV_W = Q_LORA + KV_LORA + LANES
KR_LO = D_NOPE

ADAM_LR = 0.001
ADAM_B1 = 0.9
ADAM_B2 = 0.999
ADAM_EPS = 1e-08
ADAM_WD = 0.01
ADAM_STEP = 10

REPLICATED = (
    "norm_mix_g", "q_a_norm_g", "kv_a_norm_g", "q_norm_g", "k_norm_g", "ssm_lambda_re", "ssm_lambda_im",
    "ssm_log_dt", "ssm_b_re", "ssm_b_im", "ssm_c_re", "ssm_c_im", "ssm_d", "b_glu", "mem_norm_g",
    "xq_norm_g", "xk_norm_g", "b_gate", "norm_ffn_g", "conv_b",
)
WEIGHT_ORDER = (
    "norm_mix_g", "w_in", "q_a_norm_g", "w_q_b", "kv_a_norm_g", "w_kv_b", "q_norm_g", "k_norm_g", "w_o_mla",
    "ssm_lambda_re", "ssm_lambda_im", "ssm_log_dt", "ssm_b_re", "ssm_b_im", "ssm_c_re", "ssm_c_im", "ssm_d",
    "w_glu", "b_glu", "w_o_ssm", "mem_norm_g", "w_mem_kv", "xq_norm_g", "xk_norm_g", "w_o_cross", "b_gate",
    "w_out", "norm_ffn_g", "w_up", "conv_w", "conv_b", "w_down",
)


def _params(**kw):
    return pltpu.CompilerParams(vmem_limit_bytes=VMEM_LIMIT_BYTES, **kw)


def _pick(n, cap):
    if n <= cap:
        return n
    best = None
    for m in range(LANES, cap + 1, LANES):
        if n % m == 0:
            best = m
    assert best is not None, n
    return best


_NN = (((1,), (0,)), ((), ()))
_NT = (((1,), (1,)), ((), ()))
_TN = (((0,), (0,)), ((), ()))


def _dot(a, b, dn):
    return lax.dot_general(a.astype(BF16), b.astype(BF16), dn, preferred_element_type=F32)


def _mm(name, pairs, *, trans_b=False, add=None, out_dtype=F32, bm=512, bn_cap=512):
    m = pairs[0][0].shape[0]
    n = pairs[0][1].shape[0 if trans_b else 1]
    bm = min(bm, m)
    bn = _pick(n, bn_cap)
    npair = len(pairs)

    def body(*refs):
        o_ref = refs[-1]
        acc = None
        for p in range(npair):
            d = _dot(refs[2 * p][...], refs[2 * p + 1][...], _NT if trans_b else _NN)
            acc = d if acc is None else acc + d
        if add is not None:
            acc = acc + refs[2 * npair][...]
        o_ref[...] = acc.astype(out_dtype)

    in_specs, args = [], []
    for a, b in pairs:
        k = a.shape[1]
        in_specs.append(pl.BlockSpec((bm, k), lambda i, j: (i, 0)))
        if trans_b:
            in_specs.append(pl.BlockSpec((bn, k), lambda i, j: (j, 0)))
        else:
            in_specs.append(pl.BlockSpec((k, bn), lambda i, j: (0, j)))
        args += [a, b]
    if add is not None:
        in_specs.append(pl.BlockSpec((bm, bn), lambda i, j: (i, j)))
        args.append(add)
    return pl.pallas_call(
        body, name=name, grid=(m // bm, n // bn), in_specs=in_specs,
        out_specs=pl.BlockSpec((bm, bn), lambda i, j: (i, j)),
        out_shape=jax.ShapeDtypeStruct((m, n), out_dtype), compiler_params=_params(),
    )(*args)


def _mm_tn(name, a, b, *, bm_cap=512, bn_cap=1536, bk=2048):
    l, m = a.shape
    n = b.shape[1]
    bm, bn, bk = _pick(m, bm_cap), _pick(n, bn_cap), min(bk, l)

    def body(a_ref, b_ref, o_ref):
        @pl.when(pl.program_id(2) == 0)
        def _():
            o_ref[...] = jnp.zeros_like(o_ref)

        o_ref[...] += _dot(a_ref[...], b_ref[...], _TN)

    return pl.pallas_call(
        body, name=name, grid=(m // bm, n // bn, l // bk),
        in_specs=[pl.BlockSpec((bk, bm), lambda i, j, k: (k, i)), pl.BlockSpec((bk, bn), lambda i, j, k: (k, j))],
        out_specs=pl.BlockSpec((bm, bn), lambda i, j, k: (i, j)),
        out_shape=jax.ShapeDtypeStruct((m, n), F32), compiler_params=_params(),
    )(a, b)


def _rowwise(name, fn, nrows, bm, row_ins, consts, row_outs, acc_outs=()):
    bm = min(bm, nrows)
    nblk = nrows // bm
    sub = bm // 8
    nin, nc, nro = len(row_ins), len(consts), len(row_outs)

    def body(*refs):
        i = pl.program_id(0)
        outs = fn(i, refs[:nin], refs[nin:nin + nc])
        o_refs = refs[nin + nc:nin + nc + nro]
        a_refs = refs[nin + nc + nro:]
        for r, v in zip(o_refs, outs[:nro]):
            r[...] = v.astype(r.dtype)
        if a_refs:
            @pl.when(i == 0)
            def _():
                for r in a_refs:
                    r[...] = jnp.zeros_like(r)

            for r, v in zip(a_refs, outs[nro:]):
                r[...] += v

    in_specs, args = [], []
    for arr, w, cb, kind in row_ins:
        if kind == "row":
            in_specs.append(pl.BlockSpec((bm, w), lambda i, cb=cb: (i, cb)))
        elif kind == "prev":
            in_specs.append(pl.BlockSpec((8, w), lambda i, cb=cb: (jnp.maximum(i * sub - 1, 0), cb)))
        else:
            in_specs.append(pl.BlockSpec((8, w), lambda i, cb=cb: (jnp.minimum((i + 1) * sub, nrows // 8 - 1), cb)))
        args.append(arr)
    for c in consts:
        in_specs.append(pl.BlockSpec(c.shape, lambda i: (0, 0)))
        args.append(c)
    out_specs = [pl.BlockSpec((bm, w), lambda i: (i, 0)) for w, _ in row_outs]
    out_specs += [pl.BlockSpec(s, lambda i: (0, 0)) for s in acc_outs]
    out_shape = [jax.ShapeDtypeStruct((nrows, w), dt) for w, dt in row_outs]
    out_shape += [jax.ShapeDtypeStruct(s, F32) for s in acc_outs]
    res = pl.pallas_call(
        body, name=name, grid=(nblk,), in_specs=in_specs, out_specs=out_specs, out_shape=out_shape,
        compiler_params=_params(),
    )(*args)
    return res


def _rms_f(x, g, n):
    r = lax.rsqrt(jnp.sum(x * x, axis=-1, keepdims=True) * (1.0 / n) + EPS)
    return x * r * g


def _rms_b(x, g, dy, n):
    r = lax.rsqrt(jnp.sum(x * x, axis=-1, keepdims=True) * (1.0 / n) + EPS)
    gx = dy * g
    dx = r * gx - x * (r * r * r * (jnp.sum(x * gx, axis=-1, keepdims=True) * (1.0 / n)))
    dg = jnp.sum(dy * (x * r), axis=0, keepdims=True)
    return dx, dg


def _rope_f(x, c, sa, sb):
    return x * c + pltpu.roll(x, LANES - 16, 1) * sa + pltpu.roll(x, 16, 1) * sb


def _rope_b(g, c, sa, sb):
    return g * c + pltpu.roll(g * sa, 16, 1) + pltpu.roll(g * sb, LANES - 16, 1)


def _gelu(x):
    c = math.sqrt(2.0 / math.pi)
    return 0.5 * x * (1.0 + jnp.tanh(c * (x + 0.044715 * (x * x * x))))


def _gelu_grad(x):
    c = math.sqrt(2.0 / math.pi)
    th = jnp.tanh(c * (x + 0.044715 * (x * x * x)))
    return 0.5 * (1.0 + th) + 0.5 * x * (1.0 - th * th) * (c * (1.0 + 3.0 * 0.044715 * (x * x)))


def _row_ids(bm):
    return lax.broadcasted_iota(jnp.int32, (bm, 1), 0)


def _shift_down(x, halo_ref, i, k):
    live = (i > 0).astype(F32)
    out = pltpu.roll(x, k, 0)
    row = _row_ids(8)
    first = out[:8]
    for r in range(k):
        e = (row == r).astype(F32)
        first = first * (1.0 - e) + e * (halo_ref[8 - k + r:8 - k + r + 1, :] * live)
    return jnp.concatenate([first, out[8:]], axis=0)


def _live_pairs(nq, nk, bq, bk, causal, key_major):
    pairs = [(i, j) for i in range(nq) for j in range(nk) if not causal or j * bk <= i * bq + bq - 1]
    if key_major:
        pairs.sort(key=lambda ij: (ij[1], ij[0]))
    return (jnp.asarray([p[0] for p in pairs], jnp.int32), jnp.asarray([p[1] for p in pairs], jnp.int32))


def _attn_fwd(name, qa, ka, vta, *, qoff, koff, voff, heads, causal, scale, bq, bk, side=None):
    lq, lk = qa.shape[0], ka.shape[0]
    bq, bk = min(bq, lq), min(bk, lk)
    nq, nk = lq // bq, lk // bk
    c2 = scale * LOG2E
    tab_i, tab_j = _live_pairs(nq, nk, bq, bk, causal, key_major=False)

    def body(ti, tj, q_ref, k_ref, vt_ref, o_ref, lse_ref, m_s, l_s, acc_s):
        t = pl.program_id(1)
        i, j = ti[t], tj[t]
        j_last = jnp.minimum(nk - 1, (i * bq + bq - 1) // bk) if causal else nk - 1

        @pl.when(j == 0)
        def _():
            m_s[...] = jnp.full_like(m_s, -1e30)
            l_s[...] = jnp.zeros_like(l_s)
            acc_s[...] = jnp.zeros_like(acc_s)

        def step(masked):
            st = _dot(k_ref[...], q_ref[...], _NT) * c2
            if masked:
                key = j * bk + lax.broadcasted_iota(jnp.int32, (bk, bq), 0)
                qry = i * bq + lax.broadcasted_iota(jnp.int32, (bk, bq), 1)
                st = jnp.where(key <= qry, st, -1e30)
            m_prev = m_s[...]
            m_new = jnp.maximum(m_prev, jnp.max(st, axis=0, keepdims=True))
            alpha = jnp.exp2(m_prev - m_new)
            pt = jnp.exp2(st - m_new)
            l_s[...] = alpha * l_s[...] + jnp.sum(pt, axis=0, keepdims=True)
            acc_s[...] = alpha * acc_s[...] + _dot(vt_ref[...], pt, _NN)
            m_s[...] = m_new

        if causal:
            full = j * bk + bk - 1 <= i * bq
            pl.when(full)(lambda: step(False))
            pl.when(jnp.logical_not(full))(lambda: step(True))
        else:
            step(False)

        @pl.when(j == j_last)
        def _():
            l = l_s[...]
            o_ref[...] = (acc_s[...] / l).T.astype(o_ref.dtype)
            lse_ref[0] = m_s[...] + jnp.log2(l)

    in_specs = [pl.BlockSpec((bq, LANES), lambda h, t, ti, tj: (ti[t], qoff + h)),
                pl.BlockSpec((bk, LANES), lambda h, t, ti, tj: (tj[t], koff + h)),
                pl.BlockSpec((LANES, bk), lambda h, t, ti, tj: (voff + h, tj[t]))]
    out_specs = [pl.BlockSpec((bq, LANES), lambda h, t, ti, tj: (ti[t], h)),
                 pl.BlockSpec((1, 1, bq), lambda h, t, ti, tj: (h, 0, ti[t]))]
    scratch = [pltpu.VMEM((1, bq), F32), pltpu.VMEM((1, bq), F32), pltpu.VMEM((LANES, bq), F32)]
    out_shape = [jax.ShapeDtypeStruct((lq, heads * LANES), BF16), jax.ShapeDtypeStruct((heads, 1, lq), F32)]
    (o, lse), rode = _pair_grid_call(body, name, heads, (tab_i, tab_j), [qa, ka, vta], in_specs, out_specs, out_shape,
                                     scratch, side)
    return o, lse, rode


_Side = collections.namedtuple("_Side", "srcs out_shapes nsem phases")


def _pair_grid_call(body, name, heads, tabs, ins, in_specs, out_specs, out_shape, scratch, side):
    npairs = int(tabs[0].shape[0])
    n_in, n_out = len(ins), len(out_shape)
    n = len(side.srcs) if side else 0

    def wrapped(*refs):
        pre = len(tabs)
        if not side:
            return body(*refs)
        s_refs = refs[pre + n_in:pre + n_in + n]
        o_refs = refs[pre + n_in + n + n_out:pre + n_in + 2 * n + n_out]
        send, recv, loc = refs[-3:]
        args = (s_refs, o_refs, (send, recv), loc)
        h, t = pl.program_id(0), pl.program_id(1)
        pl.when(jnp.logical_and(h == 0, t == 0))(lambda: side.phases[0](*args))
        if len(side.phases) == 3:
            pl.when(jnp.logical_and(h == heads // 2, t == 0))(lambda: side.phases[1](*args))
        body(*refs[:pre + n_in], *refs[pre + n_in + n:pre + n_in + n + n_out], *refs[pre + n_in + 2 * n + n_out:-3])
        pl.when(jnp.logical_and(h == heads - 1, t == npairs - 1))(lambda: side.phases[-1](*args))

    hbm = pl.BlockSpec(memory_space=pltpu.HBM)
    sems = [pltpu.SemaphoreType.DMA((side.nsem * n,)), pltpu.SemaphoreType.DMA((side.nsem * n,)),
            pltpu.SemaphoreType.DMA((n,))] if side else []
    grid_spec = pltpu.PrefetchScalarGridSpec(
        num_scalar_prefetch=len(tabs), grid=(heads, npairs), in_specs=in_specs + [hbm] * n,
        out_specs=out_specs + [hbm] * n, scratch_shapes=scratch + sems)
    res = pl.pallas_call(
        wrapped, name=name, grid_spec=grid_spec, out_shape=out_shape + (list(side.out_shapes) if side else []),
        compiler_params=_params(has_side_effects=True) if side else _params(),
    )(*tabs, *ins, *(side.srcs if side else []))
    return res[:n_out], res[n_out:]


def _attn_delta(name, oa, doa, *, heads, bq):
    lq = oa.shape[0]
    bq = min(bq, lq)

    def body(o_ref, do_ref, d_ref):
        prod = o_ref[...].astype(F32) * do_ref[...].astype(F32)
        hi = prod.astype(BF16)
        lo = (prod - hi.astype(F32)).astype(BF16)
        pick = (lax.broadcasted_iota(jnp.int32, (8, LANES), 0) == 0).astype(BF16)
        sums = _dot(pick, hi, _NT) + _dot(pick, lo, _NT)
        d_ref[0] = jnp.sum(sums, axis=0, keepdims=True)

    blk = pl.BlockSpec((bq, LANES), lambda h, i: (i, h))
    return pl.pallas_call(
        body, name=name, grid=(heads, lq // bq), in_specs=[blk, blk],
        out_specs=pl.BlockSpec((1, 1, bq), lambda h, i: (h, 0, i)),
        out_shape=jax.ShapeDtypeStruct((heads, 1, lq), F32), compiler_params=_params(),
    )(oa, doa)


def _attn_bwd(name, qa, ka, kta, va, doa, lsea, deltaa, *, qoff, koff, voff, heads, causal, scale, bq, bk, side=None):
    lq, lk = qa.shape[0], ka.shape[0]
    bq, bk = min(bq, lq), min(bk, lk)
    nq, nk = lq // bq, lk // bk
    c2 = scale * LOG2E
    tab_i, tab_j = _live_pairs(nq, nk, bq, bk, causal, key_major=True)

    def body(ti, tj, q_ref, k_ref, kt_ref, v_ref, do_ref, lse_ref, delta_ref, dqt_ref, dk_ref, dv_ref):
        t = pl.program_id(1)
        i, j = ti[t], tj[t]
        i_first = (j * bk) // bq if causal else 0

        @pl.when(t == 0)
        def _():
            dqt_ref[...] = jnp.zeros_like(dqt_ref)

        @pl.when(i == i_first)
        def _():
            dk_ref[...] = jnp.zeros_like(dk_ref)
            dv_ref[...] = jnp.zeros_like(dv_ref)

        def step(masked):
            q, k, v, do = q_ref[...], k_ref[...], v_ref[...], do_ref[...]
            st = _dot(k, q, _NT) * c2
            if masked:
                key = j * bk + lax.broadcasted_iota(jnp.int32, (bk, bq), 0)
                qry = i * bq + lax.broadcasted_iota(jnp.int32, (bk, bq), 1)
                st = jnp.where(key <= qry, st, -1e30)
            pt = jnp.exp2(st - lse_ref[0])
            dv_ref[...] += _dot(pt, do, _NN)
            dpt = _dot(v, do, _NT)
            dst = (pt * (dpt - delta_ref[0]) * scale).astype(BF16)
            dk_ref[...] += _dot(dst, q, _NN)
            dqt_ref[0, i] += _dot(kt_ref[...], dst, _NN)

        if causal:
            full = j * bk + bk - 1 <= i * bq
            pl.when(full)(lambda: step(False))
            pl.when(jnp.logical_not(full))(lambda: step(True))
        else:
            step(False)

    q_spec = lambda off: pl.BlockSpec((bq, LANES), lambda h, t, ti, tj: (ti[t], off + h))
    kv_spec = lambda off: pl.BlockSpec((bk, LANES), lambda h, t, ti, tj: (tj[t], off + h))
    row_spec = pl.BlockSpec((1, 1, bq), lambda h, t, ti, tj: (h, 0, ti[t]))
    in_specs = [q_spec(qoff), kv_spec(koff), pl.BlockSpec((LANES, bk), lambda h, t, ti, tj: (koff + h, tj[t])),
                kv_spec(voff), q_spec(0), row_spec, row_spec]
    out_specs = [pl.BlockSpec((1, nq, LANES, bq), lambda h, t, ti, tj: (h, 0, 0, 0)),
                 pl.BlockSpec((bk, LANES), lambda h, t, ti, tj: (tj[t], h)),
                 pl.BlockSpec((bk, LANES), lambda h, t, ti, tj: (tj[t], h))]
    out_shape = [jax.ShapeDtypeStruct((heads, nq, LANES, bq), F32), jax.ShapeDtypeStruct((lk, heads * LANES), F32),
                 jax.ShapeDtypeStruct((lk, heads * LANES), F32)]
    (dqt, dk, dv), rode = _pair_grid_call(body, name, heads, (tab_i, tab_j), [qa, ka, kta, va, doa, lsea, deltaa],
                                          in_specs, out_specs, out_shape, [], side)
    return dqt.transpose(1, 3, 0, 2).reshape(lq, heads * LANES), dk, dv, rode


S5_STRIP = SSM_N // 4
S5_CH = SSM_WIDTH // 4
S5_CHUNK = 1024


def _scan_chunk(xr_ref, xi_ref, ar, ai, cre_s, cim_s, reverse, unroll=4):
    t = xr_ref.shape[0]
    ng = t // 8
    edge = 0 if reverse else 7
    row8 = lax.broadcasted_iota(jnp.int32, (8, 1), 0)

    def grp(g):
        return pl.ds(pl.multiple_of(g * 8, 8), 8)

    steps, pr, pi = [], ar, ai
    for d in (1, 2, 4):
        own = ((row8 < 8 - d) if reverse else (row8 >= d)).astype(F32)
        steps.append(((8 - d) if reverse else d, pr * own, pi * own))
        pr, pi = pr * pr - pi * pi, 2.0 * pr * pi

    def group_scan(xr, xi):
        for shift, mr, mi in steps:
            sr, si = pltpu.roll(xr, shift, 0), pltpu.roll(xi, shift, 0)
            xr, xi = xr + mr * sr - mi * si, xi + mr * si + mi * sr
        return xr, xi

    def local(g, _):
        xr, xi = group_scan(xr_ref[grp(g), :], xi_ref[grp(g), :])
        xr_ref[grp(g), :] = xr
        xi_ref[grp(g), :] = xi
        return 0

    lax.fori_loop(0, ng, local, 0, unroll=unroll)
    e = (row8 == 7 - edge).astype(F32)
    pw_r, pw_i = group_scan(e * ar, e * ai)
    e_out = (row8 == edge).astype(F32)
    a8_r = jnp.sum(pw_r * e_out, axis=0, keepdims=True)
    a8_i = jnp.sum(pw_i * e_out, axis=0, keepdims=True)
    cr, ci = cre_s[...], cim_s[...]
    for i in range(ng):
        g = ng - 1 - i if reverse else i
        rows = slice(g * 8, g * 8 + 8)
        lr, li = xr_ref[g * 8 + edge:g * 8 + edge + 1, :], xi_ref[g * 8 + edge:g * 8 + edge + 1, :]
        xr_ref[rows, :] = xr_ref[rows, :] + pw_r * cr - pw_i * ci
        xi_ref[rows, :] = xi_ref[rows, :] + pw_r * ci + pw_i * cr
        cr, ci = lr + a8_r * cr - a8_i * ci, li + a8_r * ci + a8_i * cr
    cre_s[...] = cr
    cim_s[...] = ci


def _s5_fwd(name, u, bb_re, bb_im, cc_re, cc_imn, a_re, a_im):
    l = u.shape[0]
    t = min(S5_CHUNK, l)

    def body(u_ref, bbr_ref, bbi_ref, ccr_ref, cci_ref, ar_ref, ai_ref, sre_ref, sim_ref, y_ref, cre_s, cim_s):
        @pl.when(pl.program_id(1) == 0)
        def _():
            cre_s[...] = jnp.zeros_like(cre_s)
            cim_s[...] = jnp.zeros_like(cim_s)

        uv = u_ref[...]
        sre_ref[...] = _dot(uv, bbr_ref[...], _NN)
        sim_ref[...] = _dot(uv, bbi_ref[...], _NN)
        _scan_chunk(sre_ref, sim_ref, ar_ref[...], ai_ref[...], cre_s, cim_s, reverse=False)
        y_ref[...] = _dot(sre_ref[...], ccr_ref[...], _NN) + _dot(sim_ref[...], cci_ref[...], _NN)

    rows_ch = pl.BlockSpec((t, S5_CH), lambda w, c: (c, w))
    rows_st = pl.BlockSpec((t, S5_STRIP), lambda w, c: (c, w))
    b_blk = pl.BlockSpec((S5_CH, S5_STRIP), lambda w, c: (w, w))
    c_blk = pl.BlockSpec((S5_STRIP, S5_CH), lambda w, c: (w, w))
    a_blk = pl.BlockSpec((1, S5_STRIP), lambda w, c: (0, w))
    return pl.pallas_call(
        body, name=name, grid=(SSM_N // S5_STRIP, l // t),
        in_specs=[rows_ch, b_blk, b_blk, c_blk, c_blk, a_blk, a_blk], out_specs=[rows_st, rows_st, rows_ch],
        out_shape=[jax.ShapeDtypeStruct((l, SSM_N), F32), jax.ShapeDtypeStruct((l, SSM_N), F32),
                   jax.ShapeDtypeStruct((l, SSM_WIDTH), F32)],
        scratch_shapes=[pltpu.VMEM((1, S5_STRIP), F32), pltpu.VMEM((1, S5_STRIP), F32)],
        compiler_params=_params(),
    )(u, bb_re, bb_im, cc_re, cc_imn, a_re, a_im)


def _s5_bwd(name, dy, du_skip, u, s_re, s_im, bb_re, bb_im, cc_re, cc_imn, a_re, a_im_neg):
    l = u.shape[0]
    t = min(S5_CHUNK, l)
    nc = l // t

    def body(dy_ref, skip_ref, u_ref, sre_ref, sim_ref, hre_ref, him_ref, bbr_ref, bbi_ref, ccr_ref, cci_ref, ar_ref,
             ai_ref, du_ref, dar_ref, dai_ref, dbr_ref, dbi_ref, dcr_ref, dci_ref, lr_s, li_s, cre_s, cim_s):
        c = pl.program_id(1)

        @pl.when(c == 0)
        def _():
            for r in (cre_s, cim_s, dar_ref, dai_ref, dbr_ref, dbi_ref, dcr_ref, dci_ref):
                r[...] = jnp.zeros_like(r)

        dyv, uv = dy_ref[...], u_ref[...]
        lr_s[...] = _dot(dyv, ccr_ref[...], _NT)
        li_s[...] = _dot(dyv, cci_ref[...], _NT)
        _scan_chunk(lr_s, li_s, ar_ref[...], ai_ref[...], cre_s, cim_s, reverse=True)
        lam_r, lam_i = lr_s[...], li_s[...]
        s_r, s_i = sre_ref[...], sim_ref[...]
        tc = nc - 1 - c
        sp_r, sp_i = _shift_down(s_r, hre_ref, tc, 1), _shift_down(s_i, him_ref, tc, 1)
        dar_ref[...] += jnp.sum(lam_r * sp_r + lam_i * sp_i, axis=0, keepdims=True)
        dai_ref[...] += jnp.sum(lam_i * sp_r - lam_r * sp_i, axis=0, keepdims=True)
        dbr_ref[0] += _dot(uv, lam_r, _TN)
        dbi_ref[0] += _dot(uv, lam_i, _TN)
        dcr_ref[0] += _dot(s_r, dyv, _TN)
        dci_ref[0] += _dot(s_i, dyv, _TN)
        du = _dot(lam_r, bbr_ref[...], _NT) + _dot(lam_i, bbi_ref[...], _NT) + skip_ref[...]
        du_ref[...] = du.astype(du_ref.dtype)

    rows_ch = pl.BlockSpec((t, S5_CH), lambda w, c: (nc - 1 - c, w))
    rows_st = pl.BlockSpec((t, S5_STRIP), lambda w, c: (nc - 1 - c, w))
    halo = pl.BlockSpec((8, S5_STRIP), lambda w, c: (jnp.maximum((nc - 1 - c) * (t // 8) - 1, 0), w))
    b_blk = pl.BlockSpec((S5_CH, S5_STRIP), lambda w, c: (w, w))
    c_blk = pl.BlockSpec((S5_STRIP, S5_CH), lambda w, c: (w, w))
    a_blk = pl.BlockSpec((1, S5_STRIP), lambda w, c: (0, w))
    nw = SSM_N // S5_STRIP
    return pl.pallas_call(
        body, name=name, grid=(nw, nc),
        in_specs=[rows_ch, rows_ch, rows_ch, rows_st, rows_st, halo, halo, b_blk, b_blk, c_blk, c_blk, a_blk, a_blk],
        out_specs=[rows_ch, a_blk, a_blk, pl.BlockSpec((1, S5_CH, S5_STRIP), lambda w, c: (w, 0, 0)),
                   pl.BlockSpec((1, S5_CH, S5_STRIP), lambda w, c: (w, 0, 0)),
                   pl.BlockSpec((1, S5_STRIP, S5_CH), lambda w, c: (w, 0, 0)),
                   pl.BlockSpec((1, S5_STRIP, S5_CH), lambda w, c: (w, 0, 0))],
        out_shape=[jax.ShapeDtypeStruct((l, SSM_WIDTH), BF16), jax.ShapeDtypeStruct((1, SSM_N), F32),
                   jax.ShapeDtypeStruct((1, SSM_N), F32), jax.ShapeDtypeStruct((nw, S5_CH, S5_STRIP), F32),
                   jax.ShapeDtypeStruct((nw, S5_CH, S5_STRIP), F32), jax.ShapeDtypeStruct((nw, S5_STRIP, S5_CH), F32),
                   jax.ShapeDtypeStruct((nw, S5_STRIP, S5_CH), F32)],
        scratch_shapes=[pltpu.VMEM((t, S5_STRIP), F32), pltpu.VMEM((t, S5_STRIP), F32),
                        pltpu.VMEM((1, S5_STRIP), F32), pltpu.VMEM((1, S5_STRIP), F32)],
        compiler_params=_params(),
    )(dy, du_skip, u, s_re, s_im, s_re, s_im, bb_re, bb_im, cc_re, cc_imn, a_re, a_im_neg)


def _disc_math(lr, li, ldt, br, bi):
    dt = jnp.exp(ldt)
    mag = jnp.exp(lr * dt)
    a_re, a_im = mag * jnp.cos(li * dt), mag * jnp.sin(li * dt)
    den = lr * lr + li * li
    e_re, e_im = a_re - 1.0, a_im
    f_re = (e_re * lr + e_im * li) / den
    f_im = (e_im * lr - e_re * li) / den
    return a_re, a_im, f_re * br - f_im * bi, f_re * bi + f_im * br


def _disc_fwd(lr, li, ldt, br, bi):
    def body(lr_ref, li_ref, ldt_ref, br_ref, bi_ref, are_ref, aim_ref, bbr_ref, bbi_ref):
        a_re, a_im, bb_re, bb_im = _disc_math(lr_ref[...], li_ref[...], ldt_ref[...], br_ref[...], bi_ref[...])
        are_ref[...] = a_re
        aim_ref[...] = a_im
        bbr_ref[...] = bb_re
        bbi_ref[...] = bb_im

    col = jax.ShapeDtypeStruct(lr.shape, F32)
    mat = jax.ShapeDtypeStruct(br.shape, F32)
    return pl.pallas_call(body, name="s5_disc_fwd", out_shape=[col, col, mat, mat], compiler_params=_params())(
        lr, li, ldt, br, bi)


def _disc_bwd(lr, li, ldt, br, bi, da_re, da_im, dbb_re, dbb_im):
    def body(lr_ref, li_ref, ldt_ref, br_ref, bi_ref, g0, g1, g2, g3, o0, o1, o2, o3, o4):
        _, vjp = jax.vjp(_disc_math, lr_ref[...], li_ref[...], ldt_ref[...], br_ref[...], bi_ref[...])
        grads = vjp((g0[...], g1[...], g2[...], g3[...]))
        for o, g in zip((o0, o1, o2, o3, o4), grads):
            o[...] = g

    col = jax.ShapeDtypeStruct(lr.shape, F32)
    mat = jax.ShapeDtypeStruct(br.shape, F32)
    return pl.pallas_call(body, name="s5_disc_bwd", out_shape=[col, col, col, mat, mat], compiler_params=_params())(
        lr, li, ldt, br, bi, da_re, da_im, dbb_re, dbb_im)


N_CHIP = 4


def _place():
    x, y, c = lax.axis_index("x"), lax.axis_index("y"), lax.axis_index("c")
    return (x, y, c), (x, y, 1 - c), [(1 - x, y), (x, 1 - y), (1 - x, 1 - y)]


def _lin(px, py, pc):
    return 4 * px + 2 * py + pc


def _remote(src, dst, sems, k, dev):
    return pltpu.make_async_remote_copy(src_ref=src, dst_ref=dst, send_sem=sems[0].at[k], recv_sem=sems[1].at[k],
                                        device_id=dev, device_id_type=pl.DeviceIdType.MESH)


def _hbm_call(body, name, srcs, out_shapes, nsem):
    n = len(srcs)
    hbm = pl.BlockSpec(memory_space=pltpu.HBM)

    def wrapped(*refs):
        body(refs[:n], refs[n:2 * n], (refs[2 * n], refs[2 * n + 1]), refs[2 * n + 2])

    return pl.pallas_call(
        wrapped, name=name, in_specs=[hbm] * n, out_specs=[hbm] * n, out_shape=out_shapes,
        scratch_shapes=[pltpu.SemaphoreType.DMA((nsem * n,)), pltpu.SemaphoreType.DMA((nsem * n,)),
                        pltpu.SemaphoreType.DMA((n,))],
        compiler_params=pltpu.CompilerParams(has_side_effects=True),
    )(*srcs)


GATHER_SEMS = 7
CHIP_SEMS = 3


def _gather_copies(s_refs, o_refs, sems, loc_sems):
    me, sib, chips = _place()
    c = me[2]
    out = []
    for g, (s_ref, o_ref) in enumerate(zip(s_refs, o_refs)):
        slot = lambda dev, o_ref=o_ref: o_ref.at[_lin(*dev)]
        k0 = GATHER_SEMS * g
        mine = pltpu.make_async_copy(s_ref, slot(me), loc_sems.at[g])
        first = [_remote(s_ref, slot(me), sems, k0, sib)]
        first += [_remote(s_ref, slot(me), sems, k0 + 1 + j, (*chip, c)) for j, chip in enumerate(chips)]
        passed = [_remote(slot((*chip, c)), slot((*chip, c)), sems, k0 + 4 + j, sib) for j, chip in enumerate(chips)]
        arrive = [_remote(s_ref, slot(sib), sems, k0, me)]
        arrive += [_remote(s_ref, slot((*chip, c)), sems, k0 + 1 + j, me) for j, chip in enumerate(chips)]
        arrive += [_remote(s_ref, slot((*chip, 1 - c)), sems, k0 + 4 + j, me) for j, chip in enumerate(chips)]
        out.append((mine, first, passed, arrive))
    return out


def _gather_start(*refs):
    for mine, first, _, _ in _gather_copies(*refs):
        mine.start()
        for cp in first:
            cp.start()


def _gather_relay(*refs):
    for _, _, passed, arrive in _gather_copies(*refs):
        for j, cp in enumerate(passed):
            arrive[1 + j].wait_recv()
            cp.start()


def _gather_finish(*refs):
    for mine, first, passed, arrive in _gather_copies(*refs):
        arrive[0].wait_recv()
        for cp in arrive[4:]:
            cp.wait_recv()
        for cp in first + passed:
            cp.wait_send()
        mine.wait()


def _gather_shapes(srcs):
    return [jax.ShapeDtypeStruct((N_DEV,) + s.shape, s.dtype) for s in srcs]


def _gather_all(name, srcs):
    def body(*refs):
        _gather_start(*refs)
        _gather_relay(*refs)
        _gather_finish(*refs)

    return _hbm_call(body, name, srcs, _gather_shapes(srcs), GATHER_SEMS)


def _pair_exchange(name, sends):
    def body(s_refs, o_refs, sems, loc_sems):
        me, sib, _ = _place()
        c = me[2]
        copies = [_remote(s_ref.at[2 * q + (1 - c)], o_ref.at[q], sems, N_CHIP * g + q, sib)
                  for g, (s_ref, o_ref) in enumerate(zip(s_refs, o_refs)) for q in range(N_CHIP)]
        for cp in copies:
            cp.start()
        for cp in copies:
            cp.wait()

    return _hbm_call(body, name, sends, [jax.ShapeDtypeStruct((N_CHIP,) + s.shape[1:], s.dtype) for s in sends],
                     N_CHIP)


def _row_block(r, cap):
    best = 8
    for m in range(8, min(r, cap) + 1, 8):
        if r % m == 0:
            best = m
    return best


def _pair_sum(name, send, got):
    _, r, cols = send.shape
    bm = _row_block(r, max(8, 256 * 1024 // cols))
    core = lax.axis_index("c").astype(jnp.int32).reshape(1)

    def body(core_ref, a_ref, b_ref, o_ref):
        o_ref[...] = a_ref[...] + b_ref[...]

    grid_spec = pltpu.PrefetchScalarGridSpec(
        num_scalar_prefetch=1, grid=(N_CHIP, r // bm),
        in_specs=[pl.BlockSpec((1, bm, cols), lambda q, i, cr: (2 * q + cr[0], i, 0)),
                  pl.BlockSpec((1, bm, cols), lambda q, i, cr: (q, i, 0))],
        out_specs=pl.BlockSpec((1, bm, cols), lambda q, i, cr: (q, i, 0)))
    return pl.pallas_call(body, name=name, grid_spec=grid_spec,
                          out_shape=jax.ShapeDtypeStruct((N_CHIP, r, cols), F32), compiler_params=_params())(
        core, send, got)


def _chip_exchange(name, parts):
    def body(*refs):
        _chip_start(*refs)
        _chip_finish(*refs)

    return _hbm_call(body, name, parts, _chip_shapes(parts), CHIP_SEMS)


def _chip_copies(p_refs, o_refs, sems, loc_sems):
    me, _, chips = _place()
    c = me[2]
    chip_id = lambda chip: 2 * chip[0] + chip[1]
    my_chip = chip_id(me)
    out = []
    for g, (p_ref, o_ref) in enumerate(zip(p_refs, o_refs)):
        mine = pltpu.make_async_copy(p_ref.at[my_chip], o_ref.at[my_chip], loc_sems.at[g])
        send = [_remote(p_ref.at[chip_id(chip)], o_ref.at[my_chip], sems, CHIP_SEMS * g + j, (*chip, c))
                for j, chip in enumerate(chips)]
        arrive = [_remote(p_ref.at[my_chip], o_ref.at[chip_id(chip)], sems, CHIP_SEMS * g + j, me)
                  for j, chip in enumerate(chips)]
        out.append((mine, send, arrive))
    return out


def _chip_start(*refs):
    for mine, send, _ in _chip_copies(*refs):
        mine.start()
        for cp in send:
            cp.start()


def _chip_finish(*refs):
    for mine, send, arrive in _chip_copies(*refs):
        for cp in arrive:
            cp.wait_recv()
        for cp in send:
            cp.wait_send()
        mine.wait()


def _chip_shapes(parts):
    return [jax.ShapeDtypeStruct(p.shape, p.dtype) for p in parts]


def _adamw(name, rcv, w, m, v):
    r, c = w.shape
    nslot = rcv.shape[0]
    bm = _row_block(r, max(8, 256 * 1024 // c))

    def body(rcv_ref, w_ref, m_ref, v_ref, g_ref, d_ref, m2_ref, v2_ref):
        g = rcv_ref[0]
        for s in range(1, nslot):
            g = g + rcv_ref[s]
        m2 = ADAM_B1 * m_ref[...] + (1.0 - ADAM_B1) * g
        v2 = ADAM_B2 * v_ref[...] + (1.0 - ADAM_B2) * (g * g)
        m_hat = m2 / (1.0 - ADAM_B1 ** ADAM_STEP)
        v_hat = v2 / (1.0 - ADAM_B2 ** ADAM_STEP)
        g_ref[...] = g
        d_ref[...] = -ADAM_LR * (m_hat / (jnp.sqrt(v_hat) + ADAM_EPS) + ADAM_WD * w_ref[...])
        m2_ref[...] = m2
        v2_ref[...] = v2

    blk = pl.BlockSpec((bm, c), lambda i: (i, 0))
    out = jax.ShapeDtypeStruct((r, c), F32)
    return pl.pallas_call(
        body, name=name, grid=(r // bm,),
        in_specs=[pl.BlockSpec((nslot, bm, c), lambda i: (0, i, 0)), blk, blk, blk], out_specs=[blk] * 4,
        out_shape=[out] * 4, compiler_params=_params(),
    )(rcv, w, m, v)


IN_SHARD = IN_WIDTH // N_DEV
UP_SHARD = 2 * D_FF // N_DEV
GROUPS = (
    ("g128a", LANES, (("w_q_b", Q_LORA, D_QK), ("w_kv_b", KV_LORA, D_NOPE + D_V))),
    ("g128b", LANES, (("w_o_mla", MLA_HEADS * D_V, LANES), ("w_o_ssm", SSM_WIDTH, LANES),
                      ("w_o_cross", X_WIDTH, LANES))),
    ("g512", SSM_WIDTH, (("w_glu", SSM_WIDTH // N_DEV, SSM_WIDTH),)),
    ("g1024", D_MODEL, (("w_mem_kv", D_MODEL // N_DEV, D_MODEL), ("w_out", D_MODEL // N_DEV, D_MODEL),
                        ("w_down", D_FF // N_DEV, D_MODEL))),
    ("g640", 640, (("w_in", D_MODEL, IN_SHARD),)),
    ("g768", 768, (("w_up", D_MODEL, UP_SHARD),)),
    ("gconv", 768, (("conv_w", 3, UP_SHARD),)),
)
EARLY = ("g640", "g128a")
EARLY_GROUPS = tuple(g for g in GROUPS if g[0] in EARLY)
LATE_GROUPS = tuple(g for g in GROUPS if g[0] not in EARLY)
REP_W = 1024


def _rows8(a):
    return -(-a // 8) * 8


def _group_rows(params):
    off, r = {}, 0
    for n, a, _ in params:
        off[n] = r
        r += _rows8(a)
    return off


def _group_local(width, params, vals, li, dtype):
    return jnp.concatenate([jnp.pad(vals[n][li].astype(dtype), ((0, _rows8(a) - a), (0, width - b)))
                            for n, a, b in params], axis=0)


def _rep_rows(flat):
    n = flat.shape[-1]
    per = REP_W * 64
    tot = -(-n // per) * per
    flat = jnp.pad(flat, [(0, 0)] * (flat.ndim - 1) + [(0, tot - n)])
    return flat.reshape(flat.shape[:-1] + (tot // REP_W, REP_W))


IN_SEGS = (
    (0, Q_LORA + KV_LORA, "wqkv", 0),
    (Q_LORA + KV_LORA, Q_LORA + KV_LORA + D_ROPE, "wqkv", Q_LORA + KV_LORA + KR_LO),
    (Q_LORA + KV_LORA + D_ROPE, Q_LORA + KV_LORA + D_ROPE + SSM_WIDTH, "w_u", 0),
    (Q_LORA + KV_LORA + D_ROPE + SSM_WIDTH, Q_LORA + KV_LORA + D_ROPE + SSM_WIDTH + X_WIDTH, "w_xq", 0),
    (Q_LORA + KV_LORA + D_ROPE + SSM_WIDTH + X_WIDTH, IN_WIDTH, "w_g", 0),
)
IN_PARTS = (("wqkv", QKV_W), ("w_u", SSM_WIDTH), ("w_xq", X_WIDTH), ("w_g", 3 * D_MODEL))


def _in_pieces():
    out = []
    for d in range(N_DEV):
        for lo, hi, part, dst in IN_SEGS:
            s, e = max(lo, d * IN_SHARD), min(hi, (d + 1) * IN_SHARD)
            if s < e:
                out.append((d, s - d * IN_SHARD, e - s, [p for p, _ in IN_PARTS].index(part), dst + s - lo))
    return out


def _unpack_w_in(gathered, row0, bm=256):
    def body(x_ref, *o_refs):
        o_refs[0][...] = jnp.zeros_like(o_refs[0])
        for d, src, n, part, dst in _in_pieces():
            o_refs[part][:, dst:dst + n] = x_ref[d, :, src:src + n]

    return pl.pallas_call(
        body, name="unpack_w_in", grid=(D_MODEL // bm,),
        in_specs=[pl.BlockSpec((N_DEV, bm, gathered.shape[2]), lambda i: (0, row0 // bm + i, 0))],
        out_specs=[pl.BlockSpec((bm, w), lambda i: (i, 0)) for _, w in IN_PARTS],
        out_shape=[jax.ShapeDtypeStruct((D_MODEL, w), gathered.dtype) for _, w in IN_PARTS], compiler_params=_params(),
    )(gathered)


def _pack_w_in(parts, bm=256):
    def body(*refs):
        o_ref = refs[-1]
        o_ref[...] = jnp.zeros_like(o_ref)
        for d, src, n, part, dst in _in_pieces():
            o_ref[d, :, src:src + n] = refs[part][:, dst:dst + n]

    return pl.pallas_call(
        body, name="pack_w_in", grid=(D_MODEL // bm,),
        in_specs=[pl.BlockSpec((bm, w), lambda i: (i, 0)) for _, w in IN_PARTS],
        out_specs=pl.BlockSpec((N_DEV, bm, 640), lambda i: (0, i, 0)),
        out_shape=jax.ShapeDtypeStruct((N_DEV, D_MODEL, 640), F32), compiler_params=_params(),
    )(*parts)


def _unpack_w_up(gathered, row0, bm=256):
    def body(x_ref, o_ref):
        for d in range(N_DEV):
            o_ref[:, d * UP_SHARD:(d + 1) * UP_SHARD] = x_ref[d, :, :UP_SHARD]

    return pl.pallas_call(
        body, name="unpack_w_up", grid=(D_MODEL // bm,),
        in_specs=[pl.BlockSpec((N_DEV, bm, gathered.shape[2]), lambda i: (0, row0 // bm + i, 0))],
        out_specs=pl.BlockSpec((bm, 2 * D_FF), lambda i: (i, 0)),
        out_shape=jax.ShapeDtypeStruct((D_MODEL, 2 * D_FF), gathered.dtype), compiler_params=_params(),
    )(gathered)


def _pack_w_up(dw_g, dw_v, bm=256):
    half = N_DEV // 2

    def body(g_ref, v_ref, o_ref):
        o_ref[...] = jnp.zeros_like(o_ref)
        for d in range(N_DEV):
            src = g_ref if d < half else v_ref
            c0 = (d % half) * UP_SHARD
            o_ref[d, :, :UP_SHARD] = src[:, c0:c0 + UP_SHARD]

    blk = pl.BlockSpec((bm, D_FF), lambda i: (i, 0))
    return pl.pallas_call(
        body, name="pack_w_up", grid=(D_MODEL // bm,), in_specs=[blk, blk],
        out_specs=pl.BlockSpec((N_DEV, bm, 768), lambda i: (0, i, 0)),
        out_shape=jax.ShapeDtypeStruct((N_DEV, D_MODEL, 768), F32), compiler_params=_params(),
    )(dw_g, dw_v)


def _cols_to_rows(full):
    a, nb = full.shape
    return full.reshape(a, N_DEV, nb // N_DEV).transpose(1, 0, 2)


def _rows_to_cols(blocks):
    n, a, b = blocks.shape
    return blocks.transpose(1, 0, 2).reshape(a, n * b)


def _block_diag_in(bb):
    b3 = bb.reshape(SSM_GROUPS, SSM_STATE, SSM_GROUP_CH).transpose(0, 2, 1)
    eye = jnp.eye(SSM_GROUPS, dtype=bb.dtype)
    return (b3[:, :, None, :] * eye[:, None, :, None]).reshape(SSM_WIDTH, SSM_N)


def _block_diag_out(cc):
    c3 = cc.transpose(0, 2, 1)
    eye = jnp.eye(SSM_GROUPS, dtype=cc.dtype)
    return (c3[:, :, None, :] * eye[:, None, :, None]).reshape(SSM_N, SSM_WIDTH)


def _diag_blocks(mats, rows_per, cols_per):
    nw = mats.shape[0]
    per = SSM_GROUPS // nw
    m5 = mats.reshape(nw, per, rows_per, per, cols_per)
    eye = jnp.eye(per, dtype=mats.dtype)
    return jnp.sum(m5 * eye[None, :, None, :, None], axis=3).reshape(SSM_GROUPS, rows_per, cols_per)


def _blk(gath, offs, grp, n, a):
    r0 = offs[grp][n]
    return gath[grp][:, r0:r0 + a, :]


def _early_weights(gath, offs):
    o = dict(zip([p for p, _ in IN_PARTS], _unpack_w_in(gath["g640"], offs["g640"]["w_in"])))
    o["wqb"] = _rows_to_cols(_blk(gath, offs, "g128a", "w_q_b", Q_LORA))
    wkv = _rows_to_cols(_blk(gath, offs, "g128a", "w_kv_b", KV_LORA)).reshape(KV_LORA, MLA_HEADS, D_NOPE + D_V)
    o["wk"] = jnp.pad(wkv[:, :, :D_NOPE], ((0, 0), (0, 0), (0, LANES - D_NOPE))).reshape(KV_LORA, MLA_HEADS * LANES)
    o["wv"] = jnp.pad(wkv[:, :, D_NOPE:], ((0, 0), (0, 0), (0, LANES - D_V))).reshape(KV_LORA, MLA_HEADS * LANES)
    o["wv_t"] = o["wv"].T
    return o


def _late_weights(gath, offs):
    def blk(grp, n, a):
        return _blk(gath, offs, grp, n, a)

    o = {}
    wo = _rows_to_cols(blk("g128b", "w_o_mla", MLA_HEADS * D_V)).reshape(MLA_HEADS, D_V, D_MODEL)
    o["wo_mla"] = jnp.pad(wo, ((0, 0), (0, LANES - D_V), (0, 0))).reshape(MLA_HEADS * LANES, D_MODEL)
    o["w_o_ssm"] = _rows_to_cols(blk("g128b", "w_o_ssm", SSM_WIDTH))
    o["w_o_cross"] = _rows_to_cols(blk("g128b", "w_o_cross", X_WIDTH))
    o["w_glu"] = blk("g512", "w_glu", SSM_WIDTH // N_DEV).reshape(SSM_WIDTH, SSM_WIDTH)
    o["w_mem_kv"] = blk("g1024", "w_mem_kv", D_MODEL // N_DEV).reshape(D_MODEL, 2 * X_WIDTH)
    o["w_out"] = blk("g1024", "w_out", D_MODEL // N_DEV).reshape(D_MODEL, D_MODEL)
    o["w_down"] = blk("g1024", "w_down", D_FF // N_DEV).reshape(D_FF, D_MODEL)
    o["w_up"] = _unpack_w_up(gath["g768"], offs["g768"]["w_up"])
    o["w_up_g"], o["w_up_v"] = o["w_up"][:, :D_FF], o["w_up"][:, D_FF:]
    o["conv_w"] = _rows_to_cols(blk("gconv", "conv_w", 3)[:, :, :UP_SHARD])
    return o


def _row(v):
    return v.reshape(1, -1).astype(F32)


def _pad_lanes(v, n=LANES):
    return jnp.pad(v, (0, n - v.shape[0])).reshape(1, n).astype(F32)


def _layer_fwd(x, mem, tabs, w, p, side, late):
    l = x.shape[0]
    rope_c, rope_sa, rope_sb = tabs
    s = {"x": x}
    g_mix, g_qa, g_kva = _row(p["norm_mix_g"]), _row(p["q_a_norm_g"]), _row(p["kv_a_norm_g"])
    g_q, g_k = _pad_lanes(p["q_norm_g"]), _pad_lanes(p["k_norm_g"])

    (h,) = _rowwise("rms_mix", lambda i, r, c: (_rms_f(r[0][...], c[0][...], D_MODEL),), l, 512,
                    [(x, D_MODEL, 0, "row")], [g_mix], [(D_MODEL, BF16)])
    pqkv = _mm("proj_qkv", [(h, w["wqkv"])])
    u = _mm("proj_u", [(h, w["w_u"])])
    xq = _mm("proj_xq", [(h, w["w_xq"])])
    gl = _mm("proj_gate", [(h, w["w_g"])], bm=1024, bn_cap=1024)
    s.update(h=h, pqkv=pqkv, u=u, xq=xq, gl=gl)

    def prep_a(i, r, c):
        return (_rms_f(r[0][:, :Q_LORA], c[0][...], Q_LORA),
                _rms_f(r[0][:, Q_LORA:Q_LORA + KV_LORA], c[1][...], KV_LORA))

    nq, nkv = _rowwise("mla_prep_a", prep_a, l, 512, [(pqkv, QKV_W, 0, "row")], [g_qa, g_kva],
                       [(Q_LORA, BF16), (KV_LORA, BF16)])
    q_raw = _mm("mla_q_b", [(nq, w["wqb"])], bn_cap=1024)
    k_raw = _mm("mla_k_b", [(nkv, w["wk"])], bn_cap=1024)
    v_mla = _mm("mla_v_b", [(nkv, w["wv"])], out_dtype=BF16, bn_cap=1024)
    vt_mla = _mm("mla_vt_b", [(w["wv_t"], nkv)], trans_b=True, out_dtype=BF16, bn_cap=1024)

    def prep_b(i, r, c):
        q_ref, k_ref, kr_ref, c_ref, sa_ref, sb_ref = r
        rc, sa, sb, kr = c_ref[...], sa_ref[...], sb_ref[...], kr_ref[...]
        qs, ks = [], []
        for hd in range(MLA_HEADS):
            cols = slice(hd * LANES, (hd + 1) * LANES)
            qs.append(_rope_f(_rms_f(q_ref[:, cols], c[0][...], D_QK), rc, sa, sb))
            ks.append(_rope_f(_rms_f(k_ref[:, cols] + kr, c[1][...], D_QK), rc, sa, sb))
        return jnp.concatenate(qs, axis=1), jnp.concatenate(ks, axis=1)

    hw = MLA_HEADS * LANES
    kr_blk = (Q_LORA + KV_LORA) // LANES
    tab_ins = [(rope_c, LANES, 0, "row"), (rope_sa, LANES, 0, "row"), (rope_sb, LANES, 0, "row")]
    q, k = _rowwise("mla_prep_b", prep_b, l, 256,
                    [(q_raw, hw, 0, "row"), (k_raw, hw, 0, "row"), (pqkv, LANES, kr_blk, "row")] + tab_ins,
                    [g_q, g_k], [(hw, BF16), (hw, BF16)])
    o_a, lse_a, rode = _attn_fwd("mla_attn_fwd", q, k, vt_mla, qoff=0, koff=0, voff=0, heads=MLA_HEADS, causal=True,
                                 scale=D_QK ** -0.5, bq=1024, bk=1024, side=side)
    w_late, rode = late(rode)
    w = {**w, **w_late}
    ya = _mm("mla_o", [(o_a, w["wo_mla"])], bn_cap=1024)
    s.update(nq=nq, nkv=nkv, q_raw=q_raw, k_raw=k_raw, v_mla=v_mla, q=q, k=k, o_a=o_a, lse_a=lse_a, ya=ya)

    lr = p["ssm_lambda_re"].reshape(SSM_N, 1)
    li = p["ssm_lambda_im"].reshape(SSM_N, 1)
    ldt = jnp.repeat(p["ssm_log_dt"], SSM_STATE).reshape(SSM_N, 1)
    br = p["ssm_b_re"].reshape(SSM_N, SSM_GROUP_CH)
    bi = p["ssm_b_im"].reshape(SSM_N, SSM_GROUP_CH)
    a_re, a_im, bb_re, bb_im = _disc_fwd(lr, li, ldt, br, bi)
    bb_re_d, bb_im_d = _block_diag_in(bb_re).astype(BF16), _block_diag_in(bb_im).astype(BF16)
    cc_re_d = _block_diag_out(p["ssm_c_re"]).astype(BF16)
    cc_imn_d = _block_diag_out(-p["ssm_c_im"]).astype(BF16)
    a_re_row, a_im_row = a_re.reshape(1, SSM_N), a_im.reshape(1, SSM_N)
    d_row = _row(p["ssm_d"])
    b_glu = _row(p["b_glu"])
    s_re, s_im, ypre = _s5_fwd("s5_fwd", u, bb_re_d, bb_im_d, cc_re_d, cc_imn_d, a_re_row, a_im_row)

    def ssm_y(i, r, c):
        return (_gelu(r[0][...] + c[0][...] * r[1][...]),)

    (y_b,) = _rowwise("s5_gelu", ssm_y, l, 512, [(ypre, SSM_WIDTH, 0, "row"), (u, SSM_WIDTH, 0, "row")], [d_row],
                      [(SSM_WIDTH, BF16)])
    z = _mm("s5_glu", [(y_b, w["w_glu"])])

    def ssm_out(i, r, c):
        y = _gelu(r[0][...] + c[0][...] * r[1][...])
        return (y * jax.nn.sigmoid(r[2][...] + c[1][...]),)

    (out_b,) = _rowwise("s5_glu_out", ssm_out, l, 512,
                        [(ypre, SSM_WIDTH, 0, "row"), (u, SSM_WIDTH, 0, "row"), (z, SSM_WIDTH, 0, "row")],
                        [d_row, b_glu], [(SSM_WIDTH, BF16)])
    yb = _mm("s5_o", [(out_b, w["w_o_ssm"])], bn_cap=1024)
    s.update(disc=(lr, li, ldt, br, bi), a_rows=(a_re_row, a_im_row), bb_d=(bb_re_d, bb_im_d),
             cc_d=(cc_re_d, cc_imn_d), s_re=s_re, s_im=s_im, ypre=ypre, y_b=y_b, z=z, out_b=out_b, yb=yb)

    g_mem, g_xq, g_xk = _row(p["mem_norm_g"]), _row(p["xq_norm_g"]), _row(p["xk_norm_g"])
    ml = mem.shape[0]
    (memn,) = _rowwise("rms_mem", lambda i, r, c: (_rms_f(r[0][...], c[0][...], D_MODEL),), ml, 256,
                       [(mem, D_MODEL, 0, "row")], [g_mem], [(D_MODEL, BF16)])
    kvm = _mm("cross_kv", [(memn, w["w_mem_kv"])], bn_cap=1024)

    def headnorm(i, r, c):
        return (jnp.concatenate([_rms_f(r[0][:, hd * LANES:(hd + 1) * LANES], c[0][...], X_HEAD_DIM)
                                 for hd in range(X_HEADS)], axis=1),)

    (xk,) = _rowwise("cross_k_norm", headnorm, ml, 256, [(kvm, X_WIDTH, 0, "row")], [g_xk], [(X_WIDTH, BF16)])
    (xqn,) = _rowwise("cross_q_norm", headnorm, l, 512, [(xq, X_WIDTH, 0, "row")], [g_xq], [(X_WIDTH, BF16)])
    xvt = kvm[:, X_WIDTH:].T.astype(BF16)
    o_c, lse_c, _ = _attn_fwd("cross_attn_fwd", xqn, xk, xvt, qoff=0, koff=0, voff=0, heads=X_HEADS,
                              causal=False, scale=X_HEAD_DIM ** -0.5, bq=1024, bk=256)
    yc = _mm("cross_o", [(o_c, w["w_o_cross"])], bn_cap=1024)
    s.update(memn=memn, kvm=kvm, xk=xk, xqn=xqn, o_c=o_c, lse_c=lse_c, yc=yc)

    b_gate = _row(p["b_gate"])

    def merge(i, r, c):
        acc = None
        for br_ in range(3):
            g = jax.nn.sigmoid(r[br_][...] + c[0][:, br_ * D_MODEL:(br_ + 1) * D_MODEL])
            t = g * r[3 + br_][...]
            acc = t if acc is None else acc + t
        return (acc,)

    gate_ins = [(gl, D_MODEL, b_, "row") for b_ in range(3)]
    (merged,) = _rowwise("merge", merge, l, 256,
                         gate_ins + [(ya, D_MODEL, 0, "row"), (yb, D_MODEL, 0, "row"), (yc, D_MODEL, 0, "row")],
                         [b_gate], [(D_MODEL, BF16)])
    x1 = _mm("mix_out", [(merged, w["w_out"])], add=x, bn_cap=1024)
    s.update(merged=merged, x1=x1)

    g_ffn = _row(p["norm_ffn_g"])
    (h2,) = _rowwise("rms_ffn", lambda i, r, c: (_rms_f(r[0][...], c[0][...], D_MODEL),), l, 512,
                     [(x1, D_MODEL, 0, "row")], [g_ffn], [(D_MODEL, BF16)])
    up = _mm("ffn_up", [(h2, w["w_up"])], bm=1024, bn_cap=1408)
    conv_w = w["conv_w"]
    conv_b = _row(p["conv_b"])

    def conv_glu(i, r, c):
        cg = _conv(r[0], r[2], i, c[0], c[1], 0)
        cv = _conv(r[1], r[3], i, c[0], c[1], D_FF)
        return (cg * jax.nn.sigmoid(cg) * cv,)

    up_ins = [(up, D_FF, 0, "row"), (up, D_FF, 1, "row"), (up, D_FF, 0, "prev"), (up, D_FF, 1, "prev")]
    (act,) = _rowwise("ffn_conv_glu", conv_glu, l, 256, up_ins, [conv_w, conv_b], [(D_FF, BF16)])
    x2 = _mm("ffn_down", [(act, w["w_down"])], add=x1, bm=1024, bn_cap=1024)
    s.update(h2=h2, up=up, act=act, conv_w=conv_w, conv_b=conv_b)
    return x2, s, w, rode


def _conv(x_ref, halo_ref, i, w_ref, b_ref, col0):
    x = x_ref[...]
    cols = slice(col0, col0 + D_FF)
    return (w_ref[0:1, cols] * _shift_down(x, halo_ref, i, 2) + w_ref[1:2, cols] * _shift_down(x, halo_ref, i, 1)
            + w_ref[2:3, cols] * x + b_ref[:, cols])


def _layer_bwd(dx2, dx2_b, s, mem, tabs, w, p, make_side):
    l = dx2.shape[0]
    rope_c, rope_sa, rope_sb = tabs
    x, x1 = s["x"], s["x1"]
    g = {}

    dact = _mm("ffn_down_dx", [(dx2_b, w["w_down"])], trans_b=True, bn_cap=1408)
    g["w_down"] = _mm_tn("ffn_down_dw", s["act"], dx2_b, bm_cap=1408).reshape(N_DEV, D_FF // N_DEV, D_MODEL)
    up = s["up"]
    nblk_c = l // min(256, l)

    bm_c = min(256, l)

    def conv_bwd(i, r, c):
        da_ref, xg_ref, xv_ref, hg_ref, hv_ref, dan_ref, ng_ref, nv_ref = r
        row8 = _row_ids(8)
        live_next = (i < nblk_c - 1).astype(F32)

        def conv_rows(x, xm1, xm2, cols):
            return c[0][0:1, cols] * xm2 + c[0][1:2, cols] * xm1 + c[0][2:3, cols] * x + c[1][:, cols]

        def tail_shift(x8, x_ref, k):
            out = pltpu.roll(x8, k, 0)
            for q in range(k):
                e = (row8 == q).astype(F32)
                out = out * (1.0 - e) + e * x_ref[bm_c - k + q:bm_c - k + q + 1, :]
            return out

        def glu_grads(da, cg, cv):
            sig = jax.nn.sigmoid(cg)
            return da * cv * (sig * (1.0 + cg * (1.0 - sig))), da * (cg * sig)

        def up_shift(d, d8, k):
            out = pltpu.roll(d, bm_c - k, 0)
            last = out[bm_c - 8:]
            for q in range(k):
                e = (row8 == 8 - k + q).astype(F32)
                nxt = jnp.sum(d8 * (row8 == q).astype(F32), axis=0, keepdims=True)
                last = last * (1.0 - e) + e * nxt
            return jnp.concatenate([out[:bm_c - 8], last], axis=0)

        halves = []
        for half, (x_ref, h_ref, n_ref) in enumerate(((xg_ref, hg_ref, ng_ref), (xv_ref, hv_ref, nv_ref))):
            cols = slice(half * D_FF, (half + 1) * D_FF)
            x, x8 = x_ref[...], n_ref[...]
            xm1, xm2 = _shift_down(x, h_ref, i, 1), _shift_down(x, h_ref, i, 2)
            halves.append((conv_rows(x, xm1, xm2, cols), x, xm1, xm2,
                           conv_rows(x8, tail_shift(x8, x_ref, 1), tail_shift(x8, x_ref, 2), cols)))
        (cg, xg, xg1, xg2, cg8), (cvv, xv, xv1, xv2, cv8) = halves
        dcg, dcv = glu_grads(da_ref[...], cg, cvv)
        dcg8, dcv8 = glu_grads(dan_ref[...] * live_next, cg8, cv8)
        outs, accs = [], []
        for half, (d, d8, (x0, xa, xb)) in enumerate(((dcg, dcg8, (xg, xg1, xg2)), (dcv, dcv8, (xv, xv1, xv2)))):
            cols = slice(half * D_FF, (half + 1) * D_FF)
            accs += [jnp.sum(d, axis=0, keepdims=True), jnp.sum(d * xb, axis=0, keepdims=True),
                     jnp.sum(d * xa, axis=0, keepdims=True), jnp.sum(d * x0, axis=0, keepdims=True)]
            outs.append(c[0][2:3, cols] * d + c[0][1:2, cols] * up_shift(d, d8, 1) + c[0][0:1, cols] * up_shift(d, d8, 2))
        return (*outs, *accs)

    conv_ins = [(dact, D_FF, 0, "row"), (up, D_FF, 0, "row"), (up, D_FF, 1, "row"), (up, D_FF, 0, "prev"),
                (up, D_FF, 1, "prev"), (dact, D_FF, 0, "next"), (up, D_FF, 0, "next"), (up, D_FF, 1, "next")]
    res = _rowwise("ffn_conv_glu_bwd", conv_bwd, l, bm_c, conv_ins, [s["conv_w"], s["conv_b"]],
                   [(D_FF, BF16), (D_FF, BF16)], [(1, D_FF)] * 8)
    dup_g, dup_v = res[0], res[1]
    db_g, dw0_g, dw1_g, dw2_g, db_v, dw0_v, dw1_v, dw2_v = res[2:]
    g["conv_b"] = jnp.concatenate([db_g, db_v], axis=1)[0]
    g["conv_w"] = _cols_to_rows(jnp.concatenate(
        [jnp.concatenate([dw0_g, dw0_v], axis=1), jnp.concatenate([dw1_g, dw1_v], axis=1),
         jnp.concatenate([dw2_g, dw2_v], axis=1)], axis=0))

    dh2 = _mm("ffn_up_dx", [(dup_g, w["w_up_g"]), (dup_v, w["w_up_v"])], trans_b=True, bn_cap=1024)
    g["w_up"] = _pack_w_up(_mm_tn("ffn_up_dw_g", s["h2"], dup_g), _mm_tn("ffn_up_dw_v", s["h2"], dup_v))

    def rms_bwd_res(i, r, c):
        dx, dg = _rms_b(r[0][...], c[0][...], r[1][...], D_MODEL)
        dx = dx + r[2][...]
        return (dx, dx, dg)

    dx1, dx1_b, g["norm_ffn_g"] = _rowwise(
        "rms_ffn_bwd", rms_bwd_res, l, 512, [(x1, D_MODEL, 0, "row"), (dh2, D_MODEL, 0, "row"), (dx2, D_MODEL, 0, "row")],
        [_row(p["norm_ffn_g"])], [(D_MODEL, F32), (D_MODEL, BF16)], [(1, D_MODEL)])

    dmerged = _mm("mix_out_dx", [(dx1_b, w["w_out"])], trans_b=True, bn_cap=1024)
    g["w_out"] = _mm_tn("mix_out_dw", s["merged"], dx1_b).reshape(N_DEV, D_MODEL // N_DEV, D_MODEL)
    gl, ya, yb, yc = s["gl"], s["ya"], s["yb"], s["yc"]

    def merge_bwd(i, r, c):
        dm = r[0][...]
        dys, dgs = [], []
        for b_ in range(3):
            gate = jax.nn.sigmoid(r[1 + b_][...] + c[0][:, b_ * D_MODEL:(b_ + 1) * D_MODEL])
            dys.append(dm * gate)
            dgs.append(dm * r[4 + b_][...] * (gate * (1.0 - gate)))
        dgl = jnp.concatenate(dgs, axis=1)
        return (*dys, dgl, jnp.sum(dgl, axis=0, keepdims=True))

    gate_ins = [(gl, D_MODEL, b_, "row") for b_ in range(3)]
    dya, dyb, dyc, dgl, db_gate = _rowwise(
        "merge_bwd", merge_bwd, l, 256,
        [(dmerged, D_MODEL, 0, "row")] + gate_ins + [(ya, D_MODEL, 0, "row"), (yb, D_MODEL, 0, "row"),
                                                     (yc, D_MODEL, 0, "row")],
        [_row(p["b_gate"])], [(D_MODEL, BF16)] * 3 + [(3 * D_MODEL, BF16)], [(1, 3 * D_MODEL)])
    g["b_gate"] = db_gate[0]

    do_c = _mm("cross_o_dx", [(dyc, w["w_o_cross"])], trans_b=True, out_dtype=BF16)
    g["w_o_cross"] = _cols_to_rows(_mm_tn("cross_o_dw", s["o_c"], dyc))
    kvm = s["kvm"]
    delta_c = _attn_delta("cross_attn_delta", s["o_c"], do_c, heads=X_HEADS, bq=2048)
    dxqn, dxk, dxv, _ = _attn_bwd("cross_attn_bwd", s["xqn"], s["xk"], s["xk"].T, kvm, do_c, s["lse_c"], delta_c,
                                  qoff=0, koff=0, voff=X_HEADS, heads=X_HEADS, causal=False,
                                  scale=X_HEAD_DIM ** -0.5, bq=1024, bk=256)
    ml = mem.shape[0]

    def headnorm_bwd(i, r, c):
        dxs, dg = [], None
        for hd in range(X_HEADS):
            cols = slice(hd * LANES, (hd + 1) * LANES)
            dx_h, dg_h = _rms_b(r[0][:, cols], c[0][...], r[1][:, cols], X_HEAD_DIM)
            dxs.append(dx_h)
            dg = dg_h if dg is None else dg + dg_h
        return (jnp.concatenate(dxs, axis=1), dg)

    dxq, dg_xq = _rowwise("cross_q_norm_bwd", headnorm_bwd, l, 512,
                          [(s["xq"], X_WIDTH, 0, "row"), (dxqn, X_WIDTH, 0, "row")], [_row(p["xq_norm_g"])],
                          [(X_WIDTH, BF16)], [(1, X_HEAD_DIM)])
    dkvm_k, dg_xk = _rowwise("cross_k_norm_bwd", headnorm_bwd, ml, 256,
                             [(kvm, X_WIDTH, 0, "row"), (dxk, X_WIDTH, 0, "row")], [_row(p["xk_norm_g"])],
                             [(X_WIDTH, F32)], [(1, X_HEAD_DIM)])
    g["xq_norm_g"], g["xk_norm_g"] = dg_xq[0], dg_xk[0]
    dkvm = jnp.concatenate([dkvm_k, dxv], axis=1)
    g["w_mem_kv"] = _mm_tn("cross_kv_dw", s["memn"], dkvm).reshape(N_DEV, D_MODEL // N_DEV, 2 * X_WIDTH)
    dmemn = _mm("cross_kv_dx", [(dkvm, w["w_mem_kv"])], trans_b=True, bn_cap=1024)

    def rms_bwd_gain_only(i, r, c):
        return (_rms_b(r[0][...], c[0][...], r[1][...], D_MODEL)[1],)

    (dg_mem,) = _rowwise("rms_mem_bwd", rms_bwd_gain_only, ml, 256,
                         [(mem, D_MODEL, 0, "row"), (dmemn, D_MODEL, 0, "row")], [_row(p["mem_norm_g"])], [],
                         [(1, D_MODEL)])
    g["mem_norm_g"] = dg_mem[0]

    dout_b = _mm("s5_o_dx", [(dyb, w["w_o_ssm"])], trans_b=True)
    g["w_o_ssm"] = _cols_to_rows(_mm_tn("s5_o_dw", s["out_b"], dyb))
    ypre, u, z = s["ypre"], s["u"], s["z"]
    d_row, b_glu = _row(p["ssm_d"]), _row(p["b_glu"])
    yuz = [(ypre, SSM_WIDTH, 0, "row"), (u, SSM_WIDTH, 0, "row"), (z, SSM_WIDTH, 0, "row")]

    def glu_bwd_z(i, r, c):
        y = _gelu(r[1][...] + c[0][...] * r[2][...])
        sg = jax.nn.sigmoid(r[3][...] + c[1][...])
        dz = r[0][...] * y * (sg * (1.0 - sg))
        return (dz, jnp.sum(dz, axis=0, keepdims=True))

    dz, db_glu = _rowwise("s5_glu_bwd_z", glu_bwd_z, l, 512, [(dout_b, SSM_WIDTH, 0, "row")] + yuz, [d_row, b_glu],
                          [(SSM_WIDTH, BF16)], [(1, SSM_WIDTH)])
    g["b_glu"] = db_glu[0]
    g["w_glu"] = _mm_tn("s5_glu_dw", s["y_b"], dz).reshape(N_DEV, SSM_WIDTH // N_DEV, SSM_WIDTH)
    dy2 = _mm("s5_glu_dx", [(dz, w["w_glu"])], trans_b=True)

    def gelu_bwd(i, r, c):
        t = r[2][...] + c[0][...] * r[3][...]
        sg = jax.nn.sigmoid(r[4][...] + c[1][...])
        dt = (r[0][...] * sg + r[1][...]) * _gelu_grad(t)
        return (dt, c[0][...] * dt, jnp.sum(dt * r[3][...], axis=0, keepdims=True))

    dypre, du_skip, dd = _rowwise(
        "s5_gelu_bwd", gelu_bwd, l, 512, [(dout_b, SSM_WIDTH, 0, "row"), (dy2, SSM_WIDTH, 0, "row")] + yuz,
        [d_row, b_glu], [(SSM_WIDTH, BF16), (SSM_WIDTH, F32)], [(1, SSM_WIDTH)])
    g["ssm_d"] = dd.reshape(SSM_GROUPS, SSM_GROUP_CH)
    cc_re_d, cc_imn_d = s["cc_d"]
    bb_re_d, bb_im_d = s["bb_d"]
    a_re_row, a_im_row = s["a_rows"]
    s_re, s_im = s["s_re"], s["s_im"]
    du, da_re, da_im, dbb_re_d, dbb_im_d, dcc_re, dcc_imn = _s5_bwd(
        "s5_bwd", dypre, du_skip, u, s_re, s_im, bb_re_d, bb_im_d, cc_re_d, cc_imn_d, a_re_row, -a_im_row)
    g["ssm_c_re"] = _diag_blocks(dcc_re, SSM_STATE, SSM_GROUP_CH).transpose(0, 2, 1)
    g["ssm_c_im"] = -_diag_blocks(dcc_imn, SSM_STATE, SSM_GROUP_CH).transpose(0, 2, 1)
    dbb_re = _diag_blocks(dbb_re_d, SSM_GROUP_CH, SSM_STATE).transpose(0, 2, 1).reshape(SSM_N, SSM_GROUP_CH)
    dbb_im = _diag_blocks(dbb_im_d, SSM_GROUP_CH, SSM_STATE).transpose(0, 2, 1).reshape(SSM_N, SSM_GROUP_CH)
    dlr, dli, dldt, dbr, dbi = _disc_bwd(*s["disc"], da_re.reshape(SSM_N, 1), da_im.reshape(SSM_N, 1), dbb_re, dbb_im)
    g["ssm_lambda_re"] = dlr.reshape(SSM_GROUPS, SSM_STATE)
    g["ssm_lambda_im"] = dli.reshape(SSM_GROUPS, SSM_STATE)
    g["ssm_log_dt"] = dldt.reshape(SSM_GROUPS, SSM_STATE).sum(axis=1)
    g["ssm_b_re"] = dbr.reshape(SSM_GROUPS, SSM_STATE, SSM_GROUP_CH)
    g["ssm_b_im"] = dbi.reshape(SSM_GROUPS, SSM_STATE, SSM_GROUP_CH)

    do_a = _mm("mla_o_dx", [(dya, w["wo_mla"])], trans_b=True, out_dtype=BF16, bn_cap=1024)
    dwo = _mm_tn("mla_o_dw", s["o_a"], dya)
    g["w_o_mla"] = _cols_to_rows(dwo.reshape(MLA_HEADS, LANES, D_MODEL)[:, :D_V].reshape(MLA_HEADS * D_V, D_MODEL))
    delta_a = _attn_delta("mla_attn_delta", s["o_a"], do_a, heads=MLA_HEADS, bq=2048)
    dq, dk, dv, rode = _attn_bwd("mla_attn_bwd", s["q"], s["k"], s["k"].T, s["v_mla"], do_a, s["lse_a"], delta_a,
                                 qoff=0, koff=0, voff=0, heads=MLA_HEADS, causal=True, scale=D_QK ** -0.5, bq=1024,
                                 bk=1024, side=make_side(g))
    hw = MLA_HEADS * LANES
    kr_blk = (Q_LORA + KV_LORA) // LANES
    pqkv = s["pqkv"]
    g_q, g_k = _pad_lanes(p["q_norm_g"]), _pad_lanes(p["k_norm_g"])
    lane = lax.broadcasted_iota(jnp.int32, (1, LANES), 1)
    kr_mask = jnp.logical_and(lane >= KR_LO, lane < KR_LO + D_ROPE).astype(F32)

    def prep_b_bwd(i, r, c):
        dq_ref, dk_ref, q_ref, k_ref, kr_ref, c_ref, sa_ref, sb_ref = r
        rc, sa, sb, kr = c_ref[...], sa_ref[...], sb_ref[...], kr_ref[...]
        dqs, dks, dkr, dgq, dgk = [], [], None, None, None
        for hd in range(MLA_HEADS):
            cols = slice(hd * LANES, (hd + 1) * LANES)
            dxq, dgq_h = _rms_b(q_ref[:, cols], c[0][...], _rope_b(dq_ref[:, cols], rc, sa, sb), D_QK)
            dxk, dgk_h = _rms_b(k_ref[:, cols] + kr, c[1][...], _rope_b(dk_ref[:, cols], rc, sa, sb), D_QK)
            dqs.append(dxq)
            dks.append(dxk)
            dkr = dxk if dkr is None else dkr + dxk
            dgq = dgq_h if dgq is None else dgq + dgq_h
            dgk = dgk_h if dgk is None else dgk + dgk_h
        return (jnp.concatenate(dqs, axis=1), jnp.concatenate(dks, axis=1), dkr * c[2][...], dgq, dgk)

    tab_ins = [(rope_c, LANES, 0, "row"), (rope_sa, LANES, 0, "row"), (rope_sb, LANES, 0, "row")]
    dq_raw, dk_raw, dkr, dg_q, dg_k = _rowwise(
        "mla_prep_b_bwd", prep_b_bwd, l, 256,
        [(dq, hw, 0, "row"), (dk, hw, 0, "row"), (s["q_raw"], hw, 0, "row"), (s["k_raw"], hw, 0, "row"),
         (pqkv, LANES, kr_blk, "row")] + tab_ins, [g_q, g_k, kr_mask],
        [(hw, BF16), (hw, BF16), (LANES, F32)], [(1, LANES), (1, LANES)])
    g["q_norm_g"], g["k_norm_g"] = dg_q[0, :D_QK], dg_k[0, :D_QK]
    dnq = _mm("mla_q_b_dx", [(dq_raw, w["wqb"])], trans_b=True)
    dnkv = _mm("mla_kv_b_dx", [(dk_raw, w["wk"]), (dv, w["wv"])], trans_b=True)
    dwqb = _mm_tn("mla_q_b_dw", s["nq"], dq_raw)
    g["w_q_b"] = _cols_to_rows(dwqb)
    dwk = _mm_tn("mla_k_b_dw", s["nkv"], dk_raw).reshape(KV_LORA, MLA_HEADS, LANES)[:, :, :D_NOPE]
    dwv = _mm_tn("mla_v_b_dw", s["nkv"], dv).reshape(KV_LORA, MLA_HEADS, LANES)[:, :, :D_V]
    g["w_kv_b"] = jnp.concatenate([dwk, dwv], axis=2).transpose(1, 0, 2)

    def prep_a_bwd(i, r, c):
        dcq, dgqa = _rms_b(r[0][:, :Q_LORA], c[0][...], r[1][...], Q_LORA)
        dckv, dgkva = _rms_b(r[0][:, Q_LORA:Q_LORA + KV_LORA], c[1][...], r[2][...], KV_LORA)
        return (jnp.concatenate([dcq, dckv, r[3][...]], axis=1), dgqa, dgkva)

    dpqkv, dg_qa, dg_kva = _rowwise(
        "mla_prep_a_bwd", prep_a_bwd, l, 512,
        [(pqkv, QKV_W, 0, "row"), (dnq, Q_LORA, 0, "row"), (dnkv, KV_LORA, 0, "row"), (dkr, LANES, 0, "row")],
        [_row(p["q_a_norm_g"]), _row(p["kv_a_norm_g"])], [(QKV_W, BF16)], [(1, Q_LORA), (1, KV_LORA)])
    g["q_a_norm_g"], g["kv_a_norm_g"] = dg_qa[0], dg_kva[0]

    h = s["h"]
    dh = _mm("proj_dx", [(dpqkv, w["wqkv"]), (du, w["w_u"]), (dxq, w["w_xq"]), (dgl, w["w_g"])], trans_b=True,
             bn_cap=1024)
    dwqkv = _mm_tn("proj_qkv_dw", h, dpqkv)
    g["w_in"] = _pack_w_in([dwqkv, _mm_tn("proj_u_dw", h, du), _mm_tn("proj_xq_dw", h, dxq),
                            _mm_tn("proj_gate_dw", h, dgl)])
    dx, dx_b, dg_mix = _rowwise(
        "rms_mix_bwd", rms_bwd_res, l, 512, [(x, D_MODEL, 0, "row"), (dh, D_MODEL, 0, "row"), (dx1, D_MODEL, 0, "row")],
        [_row(p["norm_mix_g"])], [(D_MODEL, F32), (D_MODEL, BF16)], [(1, D_MODEL)])
    g["norm_mix_g"] = dg_mix[0]
    g["norm_ffn_g"] = g["norm_ffn_g"][0]
    return dx, dx_b, g, rode


def _rope_tables(positions):
    inv_freq = ROPE_THETA ** (-jnp.arange(0, D_ROPE, 2, dtype=F32) / D_ROPE)
    ang = positions.astype(F32)[:, None] * inv_freq
    cos, sin = jnp.cos(ang), jnp.sin(ang)
    l = positions.shape[0]
    one, zero = jnp.ones((l, D_NOPE), F32), lambda n: jnp.zeros((l, n), F32)
    pad = LANES - D_QK
    rope_c = jnp.concatenate([one, cos, cos, zero(pad)], axis=1)
    rope_sa = jnp.concatenate([zero(D_NOPE), -sin, zero(16), zero(pad)], axis=1)
    rope_sb = jnp.concatenate([zero(D_NOPE + 16), sin, zero(pad)], axis=1)
    return rope_c, rope_sa, rope_sb


def kernel(x, mem, positions, norm_mix_g, w_in, q_a_norm_g, w_q_b, kv_a_norm_g, w_kv_b, q_norm_g, k_norm_g, w_o_mla, ssm_lambda_re, ssm_lambda_im, ssm_log_dt, ssm_b_re, ssm_b_im, ssm_c_re, ssm_c_im, ssm_d, w_glu, b_glu, w_o_ssm, mem_norm_g, w_mem_kv, xq_norm_g, xk_norm_g, w_o_cross, b_gate, w_out, norm_ffn_g, w_up, conv_w, conv_b, w_down, loss_target, m_norm_mix_g, m_w_in, m_q_a_norm_g, m_w_q_b, m_kv_a_norm_g, m_w_kv_b, m_q_norm_g, m_k_norm_g, m_w_o_mla, m_ssm_lambda_re, m_ssm_lambda_im, m_ssm_log_dt, m_ssm_b_re, m_ssm_b_im, m_ssm_c_re, m_ssm_c_im, m_ssm_d, m_w_glu, m_b_glu, m_w_o_ssm, m_mem_norm_g, m_w_mem_kv, m_xq_norm_g, m_xk_norm_g, m_w_o_cross, m_b_gate, m_w_out, m_norm_ffn_g, m_w_up, m_conv_w, m_conv_b, m_w_down, v_norm_mix_g, v_w_in, v_q_a_norm_g, v_w_q_b, v_kv_a_norm_g, v_w_kv_b, v_q_norm_g, v_k_norm_g, v_w_o_mla, v_ssm_lambda_re, v_ssm_lambda_im, v_ssm_log_dt, v_ssm_b_re, v_ssm_b_im, v_ssm_c_re, v_ssm_c_im, v_ssm_d, v_w_glu, v_b_glu, v_w_o_ssm, v_mem_norm_g, v_w_mem_kv, v_xq_norm_g, v_xk_norm_g, v_w_o_cross, v_b_gate, v_w_out, v_norm_ffn_g, v_w_up, v_conv_w, v_conv_b, v_w_down):
    a = dict(locals())
    wts = {n: a[n] for n in WEIGHT_ORDER}
    m_in = {n: a["m_" + n] for n in WEIGHT_ORDER}
    v_in = {n: a["v_" + n] for n in WEIGHT_ORDER}
    depth = norm_mix_g.shape[0]
    x0, mem0, pos0, tgt = x[0], mem[0], positions[0], loss_target[0]
    l = x0.shape[0]
    offs = {grp: _group_rows(params) for grp, _, params in GROUPS}
    early_names, late_names = [g[0] for g in EARLY_GROUPS], [g[0] for g in LATE_GROUPS]

    def srcs(i, groups):
        return [_group_local(width, params, wts, i, F32 if grp == "gconv" else BF16) for grp, width, params in groups]

    tabs = _rope_tables(pos0)
    layer_p = [{n: wts[n][i] for n in REPLICATED} for i in range(depth)]
    gath = dict(zip(early_names, _gather_all("gather_weights", srcs(0, EARLY_GROUPS))))

    def late(rode):
        return _late_weights(dict(zip(late_names, rode)), offs), rode[len(late_names):]

    saved, layer_w = [], []
    xc = x0
    for i in range(depth):
        ride = srcs(i, LATE_GROUPS) + (srcs(i + 1, EARLY_GROUPS) if i + 1 < depth else [])
        side = _Side(ride, _gather_shapes(ride), GATHER_SEMS, (_gather_start, _gather_relay, _gather_finish))
        xc, s, w_all, rode = _layer_fwd(xc, mem0, tabs, _early_weights(gath, offs), layer_p[i], side, late)
        gath = dict(zip(early_names, rode))
        layer_w.append(w_all)
        saved.append(s)

    def loss_fn(i, r, c):
        d = r[0][...] - r[1][...]
        dy = d * (1.0 / D_MODEL)
        return (dy, dy, jnp.sum(d * d, axis=0, keepdims=True))

    dy, dy_b, sq = _rowwise("loss", loss_fn, l, 512, [(xc, D_MODEL, 0, "row"), (tgt, D_MODEL, 0, "row")], [],
                            [(D_MODEL, F32), (D_MODEL, BF16)], [(1, D_MODEL)])
    loss = lax.psum(0.5 * jnp.sum(sq) / D_MODEL, ("x", "y", "c"))

    def core_sums(g, tag, groups, extra=()):
        sends = []
        for _, width, params in groups:
            blocks = [jnp.pad(g[n], ((0, 0), (0, _rows8(rows) - rows), (0, width - g[n].shape[2])))
                      for n, rows, _ in params]
            sends.append(jnp.concatenate(blocks, axis=1))
        sends += list(extra)
        got = _pair_exchange("exchange_grads_core_" + tag, sends)
        return [_pair_sum("grad_pair_sum_%s_%d" % (tag, k), s_, g_) for k, (s_, g_) in enumerate(zip(sends, got))]

    grads, rcvs = [None] * depth, {}
    dxc, dxc_b = dy, dy_b
    waiting = []
    for i in reversed(range(depth)):
        def make_side(g, i=i, waiting=waiting):
            parts = core_sums(g, "l%d_late" % i, LATE_GROUPS) + waiting
            return _Side(parts, _chip_shapes(parts), CHIP_SEMS, (_chip_start, _chip_finish))

        dxc, dxc_b, grads[i], rode = _layer_bwd(dxc, dxc_b, saved[i], mem0, tabs, layer_w[i], layer_p[i], make_side)
        rcvs.update({(i, nm): r for nm, r in zip(late_names, rode)})
        rcvs.update({(i + 1, nm): r for nm, r in zip(early_names, rode[len(late_names):])})
        waiting = core_sums(grads[i], "l%d_early" % i, EARLY_GROUPS) if i > 0 else []
    grad_x = dxc[None]
    rep_flat = jnp.concatenate([jnp.stack([grads[i][n] for i in range(depth)]).reshape(-1) for n in REPLICATED])
    rep_send = _rep_rows(jnp.broadcast_to(rep_flat[None], (N_DEV, rep_flat.shape[0])))
    last = _chip_exchange("exchange_grads_chip", core_sums(grads[0], "l0_early", EARLY_GROUPS, [rep_send]))
    rcvs.update({(0, nm): r for nm, r in zip(early_names, last)})
    rcv_rep = last[-1]

    per_layer = {}
    for i in range(depth):
        for grp, width, params in GROUPS:
            local = [_group_local(width, params, d, i, F32) for d in (wts, m_in, v_in)]
            res = _adamw("adamw_l%d_%s" % (i, grp), rcvs[(i, grp)], *local)
            for tag, arr in zip(("grad", "delta", "m", "v"), res):
                for n, rows, cols in params:
                    per_layer[(tag, n, i)] = arr[offs[grp][n]:offs[grp][n] + rows, :cols]
    outs = {(tag, n): jnp.stack([per_layer[(tag, n, i)] for i in range(depth)])
            for tag in ("grad", "delta", "m", "v") for _, _, params in GROUPS for n, _, _ in params}
    rep_local = [_rep_rows(jnp.concatenate([d[n].astype(F32).reshape(-1) for n in REPLICATED])) for d in (wts, m_in, v_in)]
    res = _adamw("adamw_rep", rcv_rep, *rep_local)
    for tag, arr in zip(("grad", "delta", "m", "v"), res):
        flat, off = arr.reshape(-1), 0
        for n in REPLICATED:
            cnt = wts[n].size
            outs[(tag, n)] = flat[off:off + cnt].reshape(wts[n].shape)
            off += cnt
    result = [loss, grad_x]
    for tag in ("grad", "delta", "m", "v"):
        result += [outs[(tag, n)] for n in WEIGHT_ORDER]
    return tuple(result)
```

```python
import collections
import math

import jax
import jax.numpy as jnp
from jax import lax
from jax.experimental import pallas as pl
from jax.experimental.pallas import tpu as pltpu

F32 = jnp.float32
BF16 = jnp.bfloat16

N_DEV = 8
LANES = 128
LOG2E = math.log2(math.e)
VMEM_LIMIT_BYTES = 56 * 1024 * 1024

D_MODEL = 1024
EPS = 1e-6
MLA_HEADS = 8
Q_LORA = 384
KV_LORA = 256
D_NOPE = 64
D_ROPE = 32
D_QK = D_NOPE + D_ROPE
D_V = 64
ROPE_THETA = 10000.0
SSM_GROUPS = 32
SSM_GROUP_CH = 16
SSM_WIDTH = 512
SSM_STATE = 64
SSM_N = SSM_GROUPS * SSM_STATE
X_HEADS = 4
X_HEAD_DIM = 128
X_WIDTH = 512
D_FF = 2816
IN_WIDTH = Q_LORA + KV_LORA + D_ROPE + SSM_WIDTH + X_WIDTH + 3 * D_MODEL
QKV_W = Q_LORA + KV_LORA + LANES
KR_LO = D_NOPE

ADAM_LR = 0.001
ADAM_B1 = 0.9
ADAM_B2 = 0.999
ADAM_EPS = 1e-08
ADAM_WD = 0.01
ADAM_STEP = 10

REPLICATED = (
    "norm_mix_g", "q_a_norm_g", "kv_a_norm_g", "q_norm_g", "k_norm_g", "ssm_lambda_re", "ssm_lambda_im",
    "ssm_log_dt", "ssm_b_re", "ssm_b_im", "ssm_c_re", "ssm_c_im", "ssm_d", "b_glu", "mem_norm_g",
    "xq_norm_g", "xk_norm_g", "b_gate", "norm_ffn_g", "conv_b",
)
WEIGHT_ORDER = (
    "norm_mix_g", "w_in", "q_a_norm_g", "w_q_b", "kv_a_norm_g", "w_kv_b", "q_norm_g", "k_norm_g", "w_o_mla",
    "ssm_lambda_re", "ssm_lambda_im", "ssm_log_dt", "ssm_b_re", "ssm_b_im", "ssm_c_re", "ssm_c_im", "ssm_d",
    "w_glu", "b_glu", "w_o_ssm", "mem_norm_g", "w_mem_kv", "xq_norm_g", "xk_norm_g", "w_o_cross", "b_gate",
    "w_out", "norm_ffn_g", "w_up", "conv_w", "conv_b", "w_down",
)


def _params(**kw):
    return pltpu.CompilerParams(vmem_limit_bytes=VMEM_LIMIT_BYTES, **kw)


def _pick(n, cap):
    if n <= cap:
        return n
    best = None
    for m in range(LANES, cap + 1, LANES):
        if n % m == 0:
            best = m
    assert best is not None, n
    return best


_NN = (((1,), (0,)), ((), ()))
_NT = (((1,), (1,)), ((), ()))
_TN = (((0,), (0,)), ((), ()))


def _dot(a, b, dn):
    return lax.dot_general(a.astype(BF16), b.astype(BF16), dn, preferred_element_type=F32)


def _mm(name, pairs, *, trans_b=False, add=None, out_dtype=F32, bm=1024, bn_cap=512):
    m = pairs[0][0].shape[0]
    n = pairs[0][1].shape[0 if trans_b else 1]
    bm = min(bm, m)
    bn = _pick(n, bn_cap)
    npair = len(pairs)

    def body(*refs):
        o_ref = refs[-1]
        acc = None
        for p in range(npair):
            d = _dot(refs[2 * p][...], refs[2 * p + 1][...], _NT if trans_b else _NN)
            acc = d if acc is None else acc + d
        if add is not None:
            acc = acc + refs[2 * npair][...]
        o_ref[...] = acc.astype(out_dtype)

    in_specs, args = [], []
    for a, b in pairs:
        k = a.shape[1]
        in_specs.append(pl.BlockSpec((bm, k), lambda i, j: (i, 0)))
        if trans_b:
            in_specs.append(pl.BlockSpec((bn, k), lambda i, j: (j, 0)))
        else:
            in_specs.append(pl.BlockSpec((k, bn), lambda i, j: (0, j)))
        args += [a, b]
    if add is not None:
        in_specs.append(pl.BlockSpec((bm, bn), lambda i, j: (i, j)))
        args.append(add)
    return pl.pallas_call(
        body, name=name, grid=(m // bm, n // bn), in_specs=in_specs,
        out_specs=pl.BlockSpec((bm, bn), lambda i, j: (i, j)),
        out_shape=jax.ShapeDtypeStruct((m, n), out_dtype), compiler_params=_params(),
    )(*args)


def _mm_tn(name, a, b, *, bm_cap=512, bn_cap=1536, bk=2048):
    l, m = a.shape
    n = b.shape[1]
    bm, bn, bk = _pick(m, bm_cap), _pick(n, bn_cap), min(bk, l)

    def body(a_ref, b_ref, o_ref):
        @pl.when(pl.program_id(2) == 0)
        def _():
            o_ref[...] = jnp.zeros_like(o_ref)

        o_ref[...] += _dot(a_ref[...], b_ref[...], _TN)

    return pl.pallas_call(
        body, name=name, grid=(m // bm, n // bn, l // bk),
        in_specs=[pl.BlockSpec((bk, bm), lambda i, j, k: (k, i)), pl.BlockSpec((bk, bn), lambda i, j, k: (k, j))],
        out_specs=pl.BlockSpec((bm, bn), lambda i, j, k: (i, j)),
        out_shape=jax.ShapeDtypeStruct((m, n), F32), compiler_params=_params(),
    )(a, b)


def _rowwise(name, fn, nrows, bm, row_ins, consts, row_outs, acc_outs=()):
    bm = min(bm, nrows)
    nblk = nrows // bm
    sub = bm // 8
    nin, nc, nro = len(row_ins), len(consts), len(row_outs)

    def body(*refs):
        i = pl.program_id(0)
        outs = fn(i, refs[:nin], refs[nin:nin + nc])
        o_refs = refs[nin + nc:nin + nc + nro]
        a_refs = refs[nin + nc + nro:]
        for r, v in zip(o_refs, outs[:nro]):
            r[...] = v.astype(r.dtype)
        if a_refs:
            @pl.when(i == 0)
            def _():
                for r in a_refs:
                    r[...] = jnp.zeros_like(r)

            for r, v in zip(a_refs, outs[nro:]):
                r[...] += v

    in_specs, args = [], []
    for arr, w, cb, kind in row_ins:
        if kind == "row":
            in_specs.append(pl.BlockSpec((bm, w), lambda i, cb=cb: (i, cb)))
        elif kind == "prev":
            in_specs.append(pl.BlockSpec((8, w), lambda i, cb=cb: (jnp.maximum(i * sub - 1, 0), cb)))
        else:
            in_specs.append(pl.BlockSpec((8, w), lambda i, cb=cb: (jnp.minimum((i + 1) * sub, nrows // 8 - 1), cb)))
        args.append(arr)
    for c in consts:
        in_specs.append(pl.BlockSpec(c.shape, lambda i: (0, 0)))
        args.append(c)
    out_specs = [pl.BlockSpec((bm, w), lambda i: (i, 0)) for w, _ in row_outs]
    out_specs += [pl.BlockSpec(s, lambda i: (0, 0)) for s in acc_outs]
    out_shape = [jax.ShapeDtypeStruct((nrows, w), dt) for w, dt in row_outs]
    out_shape += [jax.ShapeDtypeStruct(s, F32) for s in acc_outs]
    res = pl.pallas_call(
        body, name=name, grid=(nblk,), in_specs=in_specs, out_specs=out_specs, out_shape=out_shape,
        compiler_params=_params(),
    )(*args)
    return res


def _rms_f(x, g, n):
    r = lax.rsqrt(jnp.sum(x * x, axis=-1, keepdims=True) * (1.0 / n) + EPS)
    return x * r * g


def _rms_b(x, g, dy, n):
    r = lax.rsqrt(jnp.sum(x * x, axis=-1, keepdims=True) * (1.0 / n) + EPS)
    gx = dy * g
    dx = r * gx - x * (r * r * r * (jnp.sum(x * gx, axis=-1, keepdims=True) * (1.0 / n)))
    dg = jnp.sum(dy * (x * r), axis=0, keepdims=True)
    return dx, dg


def _rope_f(x, c, sa, sb):
    return x * c + pltpu.roll(x, LANES - 16, 1) * sa + pltpu.roll(x, 16, 1) * sb


def _rope_b(g, c, sa, sb):
    return g * c + pltpu.roll(g * sa, 16, 1) + pltpu.roll(g * sb, LANES - 16, 1)


def _gelu(x):
    c = math.sqrt(2.0 / math.pi)
    return 0.5 * x * (1.0 + jnp.tanh(c * (x + 0.044715 * (x * x * x))))


def _gelu_grad(x):
    c = math.sqrt(2.0 / math.pi)
    th = jnp.tanh(c * (x + 0.044715 * (x * x * x)))
    return 0.5 * (1.0 + th) + 0.5 * x * (1.0 - th * th) * (c * (1.0 + 3.0 * 0.044715 * (x * x)))


def _row_ids(bm):
    return lax.broadcasted_iota(jnp.int32, (bm, 1), 0)


def _shift_down(x, halo_ref, i, k):
    live = (i > 0).astype(F32)
    out = pltpu.roll(x, k, 0)
    row = _row_ids(8)
    first = out[:8]
    for r in range(k):
        e = (row == r).astype(F32)
        first = first * (1.0 - e) + e * (halo_ref[8 - k + r:8 - k + r + 1, :] * live)
    return jnp.concatenate([first, out[8:]], axis=0)


def _live_pairs(nq, nk, bq, bk, causal, key_major):
    pairs = [(i, j) for i in range(nq) for j in range(nk) if not causal or j * bk <= i * bq + bq - 1]
    if key_major:
        pairs.sort(key=lambda ij: (ij[1], ij[0]))
    return (jnp.asarray([p[0] for p in pairs], jnp.int32), jnp.asarray([p[1] for p in pairs], jnp.int32))


def _attn_fwd(name, qa, ka, vta, *, qoff, koff, voff, heads, causal, scale, bq, bk, side=None):
    lq, lk = qa.shape[0], ka.shape[0]
    bq, bk = min(bq, lq), min(bk, lk)
    nq, nk = lq // bq, lk // bk
    c2 = scale * LOG2E
    tab_i, tab_j = _live_pairs(nq, nk, bq, bk, causal, key_major=False)

    def body(ti, tj, q_ref, k_ref, vt_ref, o_ref, lse_ref, m_s, l_s, acc_s):
        t = pl.program_id(1)
        i, j = ti[t], tj[t]
        j_last = jnp.minimum(nk - 1, (i * bq + bq - 1) // bk) if causal else nk - 1

        @pl.when(j == 0)
        def _():
            m_s[...] = jnp.full_like(m_s, -1e30)
            l_s[...] = jnp.zeros_like(l_s)
            acc_s[...] = jnp.zeros_like(acc_s)

        def step(masked):
            st = _dot(k_ref[...], q_ref[...], _NT) * c2
            if masked:
                key = j * bk + lax.broadcasted_iota(jnp.int32, (bk, bq), 0)
                qry = i * bq + lax.broadcasted_iota(jnp.int32, (bk, bq), 1)
                st = jnp.where(key <= qry, st, -1e30)
            m_prev = m_s[...]
            m_new = jnp.maximum(m_prev, jnp.max(st, axis=0, keepdims=True))
            alpha = jnp.exp2(m_prev - m_new)
            pt = jnp.exp2(st - m_new)
            l_s[...] = alpha * l_s[...] + jnp.sum(pt, axis=0, keepdims=True)
            acc_s[...] = alpha * acc_s[...] + _dot(vt_ref[...], pt, _NN)
            m_s[...] = m_new

        if causal:
            full = j * bk + bk - 1 <= i * bq
            pl.when(full)(lambda: step(False))
            pl.when(jnp.logical_not(full))(lambda: step(True))
        else:
            step(False)

        @pl.when(j == j_last)
        def _():
            l = l_s[...]
            o_ref[...] = (acc_s[...] / l).T.astype(o_ref.dtype)
            lse_ref[0] = m_s[...] + jnp.log2(l)

    in_specs = [pl.BlockSpec((bq, LANES), lambda h, t, ti, tj: (ti[t], qoff + h)),
                pl.BlockSpec((bk, LANES), lambda h, t, ti, tj: (tj[t], koff + h)),
                pl.BlockSpec((LANES, bk), lambda h, t, ti, tj: (voff + h, tj[t]))]
    out_specs = [pl.BlockSpec((bq, LANES), lambda h, t, ti, tj: (ti[t], h)),
                 pl.BlockSpec((1, 1, bq), lambda h, t, ti, tj: (h, 0, ti[t]))]
    scratch = [pltpu.VMEM((1, bq), F32), pltpu.VMEM((1, bq), F32), pltpu.VMEM((LANES, bq), F32)]
    out_shape = [jax.ShapeDtypeStruct((lq, heads * LANES), BF16), jax.ShapeDtypeStruct((heads, 1, lq), F32)]
    (o, lse), rode = _pair_grid_call(body, name, heads, (tab_i, tab_j), [qa, ka, vta], in_specs, out_specs, out_shape,
                                     scratch, side)
    return o, lse, rode


_Side = collections.namedtuple("_Side", "srcs out_shapes nsem phases")


def _pair_grid_call(body, name, heads, tabs, ins, in_specs, out_specs, out_shape, scratch, side):
    npairs = int(tabs[0].shape[0])
    n_in, n_out = len(ins), len(out_shape)
    n = len(side.srcs) if side else 0

    def wrapped(*refs):
        pre = len(tabs)
        if not side:
            return body(*refs)
        s_refs = refs[pre + n_in:pre + n_in + n]
        o_refs = refs[pre + n_in + n + n_out:pre + n_in + 2 * n + n_out]
        send, recv, loc = refs[-3:]
        args = (s_refs, o_refs, (send, recv), loc)
        h, t = pl.program_id(0), pl.program_id(1)
        pl.when(jnp.logical_and(h == 0, t == 0))(lambda: side.phases[0](*args))
        if len(side.phases) == 3:
            pl.when(jnp.logical_and(h == heads // 2, t == 0))(lambda: side.phases[1](*args))
        body(*refs[:pre + n_in], *refs[pre + n_in + n:pre + n_in + n + n_out], *refs[pre + n_in + 2 * n + n_out:-3])
        pl.when(jnp.logical_and(h == heads - 1, t == npairs - 1))(lambda: side.phases[-1](*args))

    hbm = pl.BlockSpec(memory_space=pltpu.HBM)
    sems = [pltpu.SemaphoreType.DMA((side.nsem * n,)), pltpu.SemaphoreType.DMA((side.nsem * n,)),
            pltpu.SemaphoreType.DMA((n,))] if side else []
    grid_spec = pltpu.PrefetchScalarGridSpec(
        num_scalar_prefetch=len(tabs), grid=(heads, npairs), in_specs=in_specs + [hbm] * n,
        out_specs=out_specs + [hbm] * n, scratch_shapes=scratch + sems)
    res = pl.pallas_call(
        wrapped, name=name, grid_spec=grid_spec, out_shape=out_shape + (list(side.out_shapes) if side else []),
        compiler_params=_params(has_side_effects=True) if side else _params(),
    )(*tabs, *ins, *(side.srcs if side else []))
    return res[:n_out], res[n_out:]


def _attn_delta(name, oa, doa, *, heads, bq):
    lq = oa.shape[0]
    bq = min(bq, lq)

    def body(o_ref, do_ref, d_ref):
        prod = o_ref[...].astype(F32) * do_ref[...].astype(F32)
        hi = prod.astype(BF16)
        lo = (prod - hi.astype(F32)).astype(BF16)
        pick = (lax.broadcasted_iota(jnp.int32, (8, LANES), 0) == 0).astype(BF16)
        sums = _dot(pick, hi, _NT) + _dot(pick, lo, _NT)
        d_ref[0] = jnp.sum(sums, axis=0, keepdims=True)

    blk = pl.BlockSpec((bq, LANES), lambda h, i: (i, h))
    return pl.pallas_call(
        body, name=name, grid=(heads, lq // bq), in_specs=[blk, blk],
        out_specs=pl.BlockSpec((1, 1, bq), lambda h, i: (h, 0, i)),
        out_shape=jax.ShapeDtypeStruct((heads, 1, lq), F32), compiler_params=_params(),
    )(oa, doa)


def _attn_bwd(name, qa, ka, kta, va, doa, lsea, deltaa, *, qoff, koff, voff, heads, causal, scale, bq, bk, side=None):
    lq, lk = qa.shape[0], ka.shape[0]
    bq, bk = min(bq, lq), min(bk, lk)
    nq, nk = lq // bq, lk // bk
    c2 = scale * LOG2E
    tab_i, tab_j = _live_pairs(nq, nk, bq, bk, causal, key_major=True)

    def body(ti, tj, q_ref, k_ref, kt_ref, v_ref, do_ref, lse_ref, delta_ref, dqt_ref, dk_ref, dv_ref):
        t = pl.program_id(1)
        i, j = ti[t], tj[t]
        i_first = (j * bk) // bq if causal else 0

        @pl.when(t == 0)
        def _():
            dqt_ref[...] = jnp.zeros_like(dqt_ref)

        @pl.when(i == i_first)
        def _():
            dk_ref[...] = jnp.zeros_like(dk_ref)
            dv_ref[...] = jnp.zeros_like(dv_ref)

        def step(masked):
            q, k, v, do = q_ref[...], k_ref[...], v_ref[...], do_ref[...]
            st = _dot(k, q, _NT) * c2
            if masked:
                key = j * bk + lax.broadcasted_iota(jnp.int32, (bk, bq), 0)
                qry = i * bq + lax.broadcasted_iota(jnp.int32, (bk, bq), 1)
                st = jnp.where(key <= qry, st, -1e30)
            pt = jnp.exp2(st - lse_ref[0])
            dv_ref[...] += _dot(pt, do, _NN)
            dpt = _dot(v, do, _NT)
            dst = (pt * (dpt - delta_ref[0]) * scale).astype(BF16)
            dk_ref[...] += _dot(dst, q, _NN)
            dqt_ref[0, i] += _dot(kt_ref[...], dst, _NN)

        if causal:
            full = j * bk + bk - 1 <= i * bq
            pl.when(full)(lambda: step(False))
            pl.when(jnp.logical_not(full))(lambda: step(True))
        else:
            step(False)

    q_spec = lambda off: pl.BlockSpec((bq, LANES), lambda h, t, ti, tj: (ti[t], off + h))
    kv_spec = lambda off: pl.BlockSpec((bk, LANES), lambda h, t, ti, tj: (tj[t], off + h))
    row_spec = pl.BlockSpec((1, 1, bq), lambda h, t, ti, tj: (h, 0, ti[t]))
    in_specs = [q_spec(qoff), kv_spec(koff), pl.BlockSpec((LANES, bk), lambda h, t, ti, tj: (koff + h, tj[t])),
                kv_spec(voff), q_spec(0), row_spec, row_spec]
    out_specs = [pl.BlockSpec((1, nq, LANES, bq), lambda h, t, ti, tj: (h, 0, 0, 0)),
                 pl.BlockSpec((bk, LANES), lambda h, t, ti, tj: (tj[t], h)),
                 pl.BlockSpec((bk, LANES), lambda h, t, ti, tj: (tj[t], h))]
    out_shape = [jax.ShapeDtypeStruct((heads, nq, LANES, bq), F32), jax.ShapeDtypeStruct((lk, heads * LANES), F32),
                 jax.ShapeDtypeStruct((lk, heads * LANES), F32)]
    (dqt, dk, dv), rode = _pair_grid_call(body, name, heads, (tab_i, tab_j), [qa, ka, kta, va, doa, lsea, deltaa],
                                          in_specs, out_specs, out_shape, [], side)
    return dqt.transpose(1, 3, 0, 2).reshape(lq, heads * LANES), dk, dv, rode


S5_STRIP = SSM_N // 4
S5_CH = SSM_WIDTH // 4
S5_CHUNK = 1024


def _scan_chunk(xr_ref, xi_ref, ar, ai, cre_s, cim_s, reverse, unroll=4):
    t = xr_ref.shape[0]
    ng = t // 8
    edge = 0 if reverse else 7
    row8 = lax.broadcasted_iota(jnp.int32, (8, 1), 0)

    def grp(g):
        return pl.ds(pl.multiple_of(g * 8, 8), 8)

    steps, pr, pi = [], ar, ai
    for d in (1, 2, 4):
        own = ((row8 < 8 - d) if reverse else (row8 >= d)).astype(F32)
        steps.append(((8 - d) if reverse else d, pr * own, pi * own))
        pr, pi = pr * pr - pi * pi, 2.0 * pr * pi

    def group_scan(xr, xi):
        for shift, mr, mi in steps:
            sr, si = pltpu.roll(xr, shift, 0), pltpu.roll(xi, shift, 0)
            xr, xi = xr + mr * sr - mi * si, xi + mr * si + mi * sr
        return xr, xi

    def local(g, _):
        xr, xi = group_scan(xr_ref[grp(g), :], xi_ref[grp(g), :])
        xr_ref[grp(g), :] = xr
        xi_ref[grp(g), :] = xi
        return 0

    lax.fori_loop(0, ng, local, 0, unroll=unroll)
    e = (row8 == 7 - edge).astype(F32)
    pw_r, pw_i = group_scan(e * ar, e * ai)
    e_out = (row8 == edge).astype(F32)
    a8_r = jnp.sum(pw_r * e_out, axis=0, keepdims=True)
    a8_i = jnp.sum(pw_i * e_out, axis=0, keepdims=True)
    cr, ci = cre_s[...], cim_s[...]
    for i in range(ng):
        g = ng - 1 - i if reverse else i
        rows = slice(g * 8, g * 8 + 8)
        lr, li = xr_ref[g * 8 + edge:g * 8 + edge + 1, :], xi_ref[g * 8 + edge:g * 8 + edge + 1, :]
        xr_ref[rows, :] = xr_ref[rows, :] + pw_r * cr - pw_i * ci
        xi_ref[rows, :] = xi_ref[rows, :] + pw_r * ci + pw_i * cr
        cr, ci = lr + a8_r * cr - a8_i * ci, li + a8_r * ci + a8_i * cr
    cre_s[...] = cr
    cim_s[...] = ci


def _s5_fwd(name, u, bb_re, bb_im, cc_re, cc_imn, a_re, a_im):
    l = u.shape[0]
    t = min(S5_CHUNK, l)

    def body(u_ref, bbr_ref, bbi_ref, ccr_ref, cci_ref, ar_ref, ai_ref, sre_ref, sim_ref, y_ref, cre_s, cim_s):
        @pl.when(pl.program_id(1) == 0)
        def _():
            cre_s[...] = jnp.zeros_like(cre_s)
            cim_s[...] = jnp.zeros_like(cim_s)

        uv = u_ref[...]
        sre_ref[...] = _dot(uv, bbr_ref[...], _NN)
        sim_ref[...] = _dot(uv, bbi_ref[...], _NN)
        _scan_chunk(sre_ref, sim_ref, ar_ref[...], ai_ref[...], cre_s, cim_s, reverse=False)
        y_ref[...] = _dot(sre_ref[...], ccr_ref[...], _NN) + _dot(sim_ref[...], cci_ref[...], _NN)

    rows_ch = pl.BlockSpec((t, S5_CH), lambda w, c: (c, w))
    rows_st = pl.BlockSpec((t, S5_STRIP), lambda w, c: (c, w))
    b_blk = pl.BlockSpec((S5_CH, S5_STRIP), lambda w, c: (w, w))
    c_blk = pl.BlockSpec((S5_STRIP, S5_CH), lambda w, c: (w, w))
    a_blk = pl.BlockSpec((1, S5_STRIP), lambda w, c: (0, w))
    return pl.pallas_call(
        body, name=name, grid=(SSM_N // S5_STRIP, l // t),
        in_specs=[rows_ch, b_blk, b_blk, c_blk, c_blk, a_blk, a_blk], out_specs=[rows_st, rows_st, rows_ch],
        out_shape=[jax.ShapeDtypeStruct((l, SSM_N), F32), jax.ShapeDtypeStruct((l, SSM_N), F32),
                   jax.ShapeDtypeStruct((l, SSM_WIDTH), F32)],
        scratch_shapes=[pltpu.VMEM((1, S5_STRIP), F32), pltpu.VMEM((1, S5_STRIP), F32)],
        compiler_params=_params(),
    )(u, bb_re, bb_im, cc_re, cc_imn, a_re, a_im)


def _s5_bwd(name, dy, du_skip, u, s_re, s_im, bb_re, bb_im, cc_re, cc_imn, a_re, a_im_neg):
    l = u.shape[0]
    t = min(S5_CHUNK, l)
    nc = l // t

    def body(dy_ref, skip_ref, u_ref, sre_ref, sim_ref, hre_ref, him_ref, bbr_ref, bbi_ref, ccr_ref, cci_ref, ar_ref,
             ai_ref, du_ref, dar_ref, dai_ref, dbr_ref, dbi_ref, dcr_ref, dci_ref, lr_s, li_s, cre_s, cim_s):
        c = pl.program_id(1)

        @pl.when(c == 0)
        def _():
            for r in (cre_s, cim_s, dar_ref, dai_ref, dbr_ref, dbi_ref, dcr_ref, dci_ref):
                r[...] = jnp.zeros_like(r)

        dyv, uv = dy_ref[...], u_ref[...]
        lr_s[...] = _dot(dyv, ccr_ref[...], _NT)
        li_s[...] = _dot(dyv, cci_ref[...], _NT)
        _scan_chunk(lr_s, li_s, ar_ref[...], ai_ref[...], cre_s, cim_s, reverse=True)
        lam_r, lam_i = lr_s[...], li_s[...]
        s_r, s_i = sre_ref[...], sim_ref[...]
        tc = nc - 1 - c
        sp_r, sp_i = _shift_down(s_r, hre_ref, tc, 1), _shift_down(s_i, him_ref, tc, 1)
        dar_ref[...] += jnp.sum(lam_r * sp_r + lam_i * sp_i, axis=0, keepdims=True)
        dai_ref[...] += jnp.sum(lam_i * sp_r - lam_r * sp_i, axis=0, keepdims=True)
        dbr_ref[0] += _dot(uv, lam_r, _TN)
        dbi_ref[0] += _dot(uv, lam_i, _TN)
        dcr_ref[0] += _dot(s_r, dyv, _TN)
        dci_ref[0] += _dot(s_i, dyv, _TN)
        du = _dot(lam_r, bbr_ref[...], _NT) + _dot(lam_i, bbi_ref[...], _NT) + skip_ref[...]
        du_ref[...] = du.astype(du_ref.dtype)

    rows_ch = pl.BlockSpec((t, S5_CH), lambda w, c: (nc - 1 - c, w))
    rows_st = pl.BlockSpec((t, S5_STRIP), lambda w, c: (nc - 1 - c, w))
    halo = pl.BlockSpec((8, S5_STRIP), lambda w, c: (jnp.maximum((nc - 1 - c) * (t // 8) - 1, 0), w))
    b_blk = pl.BlockSpec((S5_CH, S5_STRIP), lambda w, c: (w, w))
    c_blk = pl.BlockSpec((S5_STRIP, S5_CH), lambda w, c: (w, w))
    a_blk = pl.BlockSpec((1, S5_STRIP), lambda w, c: (0, w))
    nw = SSM_N // S5_STRIP
    return pl.pallas_call(
        body, name=name, grid=(nw, nc),
        in_specs=[rows_ch, rows_ch, rows_ch, rows_st, rows_st, halo, halo, b_blk, b_blk, c_blk, c_blk, a_blk, a_blk],
        out_specs=[rows_ch, a_blk, a_blk, pl.BlockSpec((1, S5_CH, S5_STRIP), lambda w, c: (w, 0, 0)),
                   pl.BlockSpec((1, S5_CH, S5_STRIP), lambda w, c: (w, 0, 0)),
                   pl.BlockSpec((1, S5_STRIP, S5_CH), lambda w, c: (w, 0, 0)),
                   pl.BlockSpec((1, S5_STRIP, S5_CH), lambda w, c: (w, 0, 0))],
        out_shape=[jax.ShapeDtypeStruct((l, SSM_WIDTH), BF16), jax.ShapeDtypeStruct((1, SSM_N), F32),
                   jax.ShapeDtypeStruct((1, SSM_N), F32), jax.ShapeDtypeStruct((nw, S5_CH, S5_STRIP), F32),
                   jax.ShapeDtypeStruct((nw, S5_CH, S5_STRIP), F32), jax.ShapeDtypeStruct((nw, S5_STRIP, S5_CH), F32),
                   jax.ShapeDtypeStruct((nw, S5_STRIP, S5_CH), F32)],
        scratch_shapes=[pltpu.VMEM((t, S5_STRIP), F32), pltpu.VMEM((t, S5_STRIP), F32),
                        pltpu.VMEM((1, S5_STRIP), F32), pltpu.VMEM((1, S5_STRIP), F32)],
        compiler_params=_params(),
    )(dy, du_skip, u, s_re, s_im, s_re, s_im, bb_re, bb_im, cc_re, cc_imn, a_re, a_im_neg)


def _disc_math(lr, li, ldt, br, bi):
    dt = jnp.exp(ldt)
    mag = jnp.exp(lr * dt)
    a_re, a_im = mag * jnp.cos(li * dt), mag * jnp.sin(li * dt)
    den = lr * lr + li * li
    e_re, e_im = a_re - 1.0, a_im
    f_re = (e_re * lr + e_im * li) / den
    f_im = (e_im * lr - e_re * li) / den
    return a_re, a_im, f_re * br - f_im * bi, f_re * bi + f_im * br


def _disc_fwd(lr, li, ldt, br, bi):
    def body(lr_ref, li_ref, ldt_ref, br_ref, bi_ref, are_ref, aim_ref, bbr_ref, bbi_ref):
        a_re, a_im, bb_re, bb_im = _disc_math(lr_ref[...], li_ref[...], ldt_ref[...], br_ref[...], bi_ref[...])
        are_ref[...] = a_re
        aim_ref[...] = a_im
        bbr_ref[...] = bb_re
        bbi_ref[...] = bb_im

    col = jax.ShapeDtypeStruct(lr.shape, F32)
    mat = jax.ShapeDtypeStruct(br.shape, F32)
    return pl.pallas_call(body, name="s5_disc_fwd", out_shape=[col, col, mat, mat], compiler_params=_params())(
        lr, li, ldt, br, bi)


def _disc_bwd(lr, li, ldt, br, bi, da_re, da_im, dbb_re, dbb_im):
    def body(lr_ref, li_ref, ldt_ref, br_ref, bi_ref, g0, g1, g2, g3, o0, o1, o2, o3, o4):
        _, vjp = jax.vjp(_disc_math, lr_ref[...], li_ref[...], ldt_ref[...], br_ref[...], bi_ref[...])
        grads = vjp((g0[...], g1[...], g2[...], g3[...]))
        for o, g in zip((o0, o1, o2, o3, o4), grads):
            o[...] = g

    col = jax.ShapeDtypeStruct(lr.shape, F32)
    mat = jax.ShapeDtypeStruct(br.shape, F32)
    return pl.pallas_call(body, name="s5_disc_bwd", out_shape=[col, col, col, mat, mat], compiler_params=_params())(
        lr, li, ldt, br, bi, da_re, da_im, dbb_re, dbb_im)


N_CHIP = 4


def _place():
    x, y, c = lax.axis_index("x"), lax.axis_index("y"), lax.axis_index("c")
    return (x, y, c), (x, y, 1 - c), [(1 - x, y), (x, 1 - y), (1 - x, 1 - y)]


def _lin(px, py, pc):
    return 4 * px + 2 * py + pc


def _remote(src, dst, sems, k, dev):
    return pltpu.make_async_remote_copy(src_ref=src, dst_ref=dst, send_sem=sems[0].at[k], recv_sem=sems[1].at[k],
                                        device_id=dev, device_id_type=pl.DeviceIdType.MESH)


def _hbm_call(body, name, srcs, out_shapes, nsem):
    n = len(srcs)
    hbm = pl.BlockSpec(memory_space=pltpu.HBM)

    def wrapped(*refs):
        body(refs[:n], refs[n:2 * n], (refs[2 * n], refs[2 * n + 1]), refs[2 * n + 2])

    return pl.pallas_call(
        wrapped, name=name, in_specs=[hbm] * n, out_specs=[hbm] * n, out_shape=out_shapes,
        scratch_shapes=[pltpu.SemaphoreType.DMA((nsem * n,)), pltpu.SemaphoreType.DMA((nsem * n,)),
                        pltpu.SemaphoreType.DMA((n,))],
        compiler_params=pltpu.CompilerParams(has_side_effects=True),
    )(*srcs)


GATHER_SEMS = 7
CHIP_SEMS = 3


def _gather_copies(s_refs, o_refs, sems, loc_sems):
    me, sib, chips = _place()
    c = me[2]
    out = []
    for g, (s_ref, o_ref) in enumerate(zip(s_refs, o_refs)):
        slot = lambda dev, o_ref=o_ref: o_ref.at[_lin(*dev)]
        k0 = GATHER_SEMS * g
        mine = pltpu.make_async_copy(s_ref, slot(me), loc_sems.at[g])
        first = [_remote(s_ref, slot(me), sems, k0, sib)]
        first += [_remote(s_ref, slot(me), sems, k0 + 1 + j, (*chip, c)) for j, chip in enumerate(chips)]
        passed = [_remote(slot((*chip, c)), slot((*chip, c)), sems, k0 + 4 + j, sib) for j, chip in enumerate(chips)]
        arrive = [_remote(s_ref, slot(sib), sems, k0, me)]
        arrive += [_remote(s_ref, slot((*chip, c)), sems, k0 + 1 + j, me) for j, chip in enumerate(chips)]
        arrive += [_remote(s_ref, slot((*chip, 1 - c)), sems, k0 + 4 + j, me) for j, chip in enumerate(chips)]
        out.append((mine, first, passed, arrive))
    return out


def _gather_start(*refs):
    for mine, first, _, _ in _gather_copies(*refs):
        mine.start()
        for cp in first:
            cp.start()


def _gather_relay(*refs):
    for _, _, passed, arrive in _gather_copies(*refs):
        for j, cp in enumerate(passed):
            arrive[1 + j].wait_recv()
            cp.start()


def _gather_finish(*refs):
    for mine, first, passed, arrive in _gather_copies(*refs):
        arrive[0].wait_recv()
        for cp in arrive[4:]:
            cp.wait_recv()
        for cp in first + passed:
            cp.wait_send()
        mine.wait()


def _gather_shapes(srcs):
    return [jax.ShapeDtypeStruct((N_DEV,) + s.shape, s.dtype) for s in srcs]


def _gather_all(name, srcs):
    def body(*refs):
        _gather_start(*refs)
        _gather_relay(*refs)
        _gather_finish(*refs)

    return _hbm_call(body, name, srcs, _gather_shapes(srcs), GATHER_SEMS)


def _pair_exchange(name, sends):
    def body(s_refs, o_refs, sems, loc_sems):
        me, sib, _ = _place()
        c = me[2]
        copies = [_remote(s_ref.at[2 * q + (1 - c)], o_ref.at[q], sems, N_CHIP * g + q, sib)
                  for g, (s_ref, o_ref) in enumerate(zip(s_refs, o_refs)) for q in range(N_CHIP)]
        for cp in copies:
            cp.start()
        for cp in copies:
            cp.wait()

    return _hbm_call(body, name, sends, [jax.ShapeDtypeStruct((N_CHIP,) + s.shape[1:], s.dtype) for s in sends],
                     N_CHIP)


def _row_block(r, cap):
    best = 8
    for m in range(8, min(r, cap) + 1, 8):
        if r % m == 0:
            best = m
    return best


def _pair_sum(name, send, got):
    _, r, cols = send.shape
    bm = _row_block(r, max(8, 256 * 1024 // cols))
    core = lax.axis_index("c").astype(jnp.int32).reshape(1)

    def body(core_ref, a_ref, b_ref, o_ref):
        o_ref[...] = a_ref[...] + b_ref[...]

    grid_spec = pltpu.PrefetchScalarGridSpec(
        num_scalar_prefetch=1, grid=(N_CHIP, r // bm),
        in_specs=[pl.BlockSpec((1, bm, cols), lambda q, i, cr: (2 * q + cr[0], i, 0)),
                  pl.BlockSpec((1, bm, cols), lambda q, i, cr: (q, i, 0))],
        out_specs=pl.BlockSpec((1, bm, cols), lambda q, i, cr: (q, i, 0)))
    return pl.pallas_call(body, name=name, grid_spec=grid_spec,
                          out_shape=jax.ShapeDtypeStruct((N_CHIP, r, cols), F32), compiler_params=_params())(
        core, send, got)


def _chip_exchange(name, parts):
    def body(*refs):
        _chip_start(*refs)
        _chip_finish(*refs)

    return _hbm_call(body, name, parts, _chip_shapes(parts), CHIP_SEMS)


def _chip_copies(p_refs, o_refs, sems, loc_sems):
    me, _, chips = _place()
    c = me[2]
    chip_id = lambda chip: 2 * chip[0] + chip[1]
    my_chip = chip_id(me)
    out = []
    for g, (p_ref, o_ref) in enumerate(zip(p_refs, o_refs)):
        mine = pltpu.make_async_copy(p_ref.at[my_chip], o_ref.at[my_chip], loc_sems.at[g])
        send = [_remote(p_ref.at[chip_id(chip)], o_ref.at[my_chip], sems, CHIP_SEMS * g + j, (*chip, c))
                for j, chip in enumerate(chips)]
        arrive = [_remote(p_ref.at[my_chip], o_ref.at[chip_id(chip)], sems, CHIP_SEMS * g + j, me)
                  for j, chip in enumerate(chips)]
        out.append((mine, send, arrive))
    return out


def _chip_start(*refs):
    for mine, send, _ in _chip_copies(*refs):
        mine.start()
        for cp in send:
            cp.start()


def _chip_finish(*refs):
    for mine, send, arrive in _chip_copies(*refs):
        for cp in arrive:
            cp.wait_recv()
        for cp in send:
            cp.wait_send()
        mine.wait()


def _chip_shapes(parts):
    return [jax.ShapeDtypeStruct(p.shape, p.dtype) for p in parts]


def _adamw(name, rcv, w, m, v):
    r, c = w.shape
    nslot = rcv.shape[0]
    bm = _row_block(r, max(8, 256 * 1024 // c))

    def body(rcv_ref, w_ref, m_ref, v_ref, g_ref, d_ref, m2_ref, v2_ref):
        g = rcv_ref[0]
        for s in range(1, nslot):
            g = g + rcv_ref[s]
        m2 = ADAM_B1 * m_ref[...] + (1.0 - ADAM_B1) * g
        v2 = ADAM_B2 * v_ref[...] + (1.0 - ADAM_B2) * (g * g)
        m_hat = m2 / (1.0 - ADAM_B1 ** ADAM_STEP)
        v_hat = v2 / (1.0 - ADAM_B2 ** ADAM_STEP)
        g_ref[...] = g
        d_ref[...] = -ADAM_LR * (m_hat / (jnp.sqrt(v_hat) + ADAM_EPS) + ADAM_WD * w_ref[...])
        m2_ref[...] = m2
        v2_ref[...] = v2

    blk = pl.BlockSpec((bm, c), lambda i: (i, 0))
    out = jax.ShapeDtypeStruct((r, c), F32)
    return pl.pallas_call(
        body, name=name, grid=(r // bm,),
        in_specs=[pl.BlockSpec((nslot, bm, c), lambda i: (0, i, 0)), blk, blk, blk], out_specs=[blk] * 4,
        out_shape=[out] * 4, compiler_params=_params(),
    )(rcv, w, m, v)


IN_SHARD = IN_WIDTH // N_DEV
UP_SHARD = 2 * D_FF // N_DEV
GROUPS = (
    ("g128a", LANES, (("w_q_b", Q_LORA, D_QK), ("w_kv_b", KV_LORA, D_NOPE + D_V))),
    ("g128b", LANES, (("w_o_mla", MLA_HEADS * D_V, LANES), ("w_o_ssm", SSM_WIDTH, LANES),
                      ("w_o_cross", X_WIDTH, LANES))),
    ("g512", SSM_WIDTH, (("w_glu", SSM_WIDTH // N_DEV, SSM_WIDTH),)),
    ("g1024", D_MODEL, (("w_mem_kv", D_MODEL // N_DEV, D_MODEL), ("w_out", D_MODEL // N_DEV, D_MODEL),
                        ("w_down", D_FF // N_DEV, D_MODEL))),
    ("g640", 640, (("w_in", D_MODEL, IN_SHARD),)),
    ("g768", 768, (("w_up", D_MODEL, UP_SHARD),)),
    ("gconv", 768, (("conv_w", 3, UP_SHARD),)),
)
EARLY = ("g640", "g128a")
EARLY_GROUPS = tuple(g for g in GROUPS if g[0] in EARLY)
LATE_GROUPS = tuple(g for g in GROUPS if g[0] not in EARLY)
REP_W = 1024


def _rows8(a):
    return -(-a // 8) * 8


def _group_rows(params):
    off, r = {}, 0
    for n, a, _ in params:
        off[n] = r
        r += _rows8(a)
    return off


def _group_local(width, params, vals, li, dtype):
    return jnp.concatenate([jnp.pad(vals[n][li].astype(dtype), ((0, _rows8(a) - a), (0, width - b)))
                            for n, a, b in params], axis=0)


def _rep_rows(flat):
    n = flat.shape[-1]
    per = REP_W * 64
    tot = -(-n // per) * per
    flat = jnp.pad(flat, [(0, 0)] * (flat.ndim - 1) + [(0, tot - n)])
    return flat.reshape(flat.shape[:-1] + (tot // REP_W, REP_W))


IN_SEGS = (
    (0, Q_LORA + KV_LORA, "wqkv", 0),
    (Q_LORA + KV_LORA, Q_LORA + KV_LORA + D_ROPE, "wqkv", Q_LORA + KV_LORA + KR_LO),
    (Q_LORA + KV_LORA + D_ROPE, Q_LORA + KV_LORA + D_ROPE + SSM_WIDTH, "w_u", 0),
    (Q_LORA + KV_LORA + D_ROPE + SSM_WIDTH, Q_LORA + KV_LORA + D_ROPE + SSM_WIDTH + X_WIDTH, "w_xq", 0),
    (Q_LORA + KV_LORA + D_ROPE + SSM_WIDTH + X_WIDTH, IN_WIDTH, "w_g", 0),
)
IN_PARTS = (("wqkv", QKV_W), ("w_u", SSM_WIDTH), ("w_xq", X_WIDTH), ("w_g", 3 * D_MODEL))


def _in_pieces():
    out = []
    for d in range(N_DEV):
        for lo, hi, part, dst in IN_SEGS:
            s, e = max(lo, d * IN_SHARD), min(hi, (d + 1) * IN_SHARD)
            if s < e:
                out.append((d, s - d * IN_SHARD, e - s, [p for p, _ in IN_PARTS].index(part), dst + s - lo))
    return out


def _unpack_w_in(gathered, row0, bm=256):
    def body(x_ref, *o_refs):
        o_refs[0][...] = jnp.zeros_like(o_refs[0])
        for d, src, n, part, dst in _in_pieces():
            o_refs[part][:, dst:dst + n] = x_ref[d, :, src:src + n]

    return pl.pallas_call(
        body, name="unpack_w_in", grid=(D_MODEL // bm,),
        in_specs=[pl.BlockSpec((N_DEV, bm, gathered.shape[2]), lambda i: (0, row0 // bm + i, 0))],
        out_specs=[pl.BlockSpec((bm, w), lambda i: (i, 0)) for _, w in IN_PARTS],
        out_shape=[jax.ShapeDtypeStruct((D_MODEL, w), gathered.dtype) for _, w in IN_PARTS], compiler_params=_params(),
    )(gathered)


def _pack_w_in(parts, bm=256):
    def body(*refs):
        o_ref = refs[-1]
        o_ref[...] = jnp.zeros_like(o_ref)
        for d, src, n, part, dst in _in_pieces():
            o_ref[d, :, src:src + n] = refs[part][:, dst:dst + n]

    return pl.pallas_call(
        body, name="pack_w_in", grid=(D_MODEL // bm,),
        in_specs=[pl.BlockSpec((bm, w), lambda i: (i, 0)) for _, w in IN_PARTS],
        out_specs=pl.BlockSpec((N_DEV, bm, 640), lambda i: (0, i, 0)),
        out_shape=jax.ShapeDtypeStruct((N_DEV, D_MODEL, 640), F32), compiler_params=_params(),
    )(*parts)


def _unpack_w_up(gathered, row0, bm=256):
    def body(x_ref, o_ref):
        for d in range(N_DEV):
            o_ref[:, d * UP_SHARD:(d + 1) * UP_SHARD] = x_ref[d, :, :UP_SHARD]

    return pl.pallas_call(
        body, name="unpack_w_up", grid=(D_MODEL // bm,),
        in_specs=[pl.BlockSpec((N_DEV, bm, gathered.shape[2]), lambda i: (0, row0 // bm + i, 0))],
        out_specs=pl.BlockSpec((bm, 2 * D_FF), lambda i: (i, 0)),
        out_shape=jax.ShapeDtypeStruct((D_MODEL, 2 * D_FF), gathered.dtype), compiler_params=_params(),
    )(gathered)


def _pack_w_up(dw_g, dw_v, bm=256):
    half = N_DEV // 2

    def body(g_ref, v_ref, o_ref):
        o_ref[...] = jnp.zeros_like(o_ref)
        for d in range(N_DEV):
            src = g_ref if d < half else v_ref
            c0 = (d % half) * UP_SHARD
            o_ref[d, :, :UP_SHARD] = src[:, c0:c0 + UP_SHARD]

    blk = pl.BlockSpec((bm, D_FF), lambda i: (i, 0))
    return pl.pallas_call(
        body, name="pack_w_up", grid=(D_MODEL // bm,), in_specs=[blk, blk],
        out_specs=pl.BlockSpec((N_DEV, bm, 768), lambda i: (0, i, 0)),
        out_shape=jax.ShapeDtypeStruct((N_DEV, D_MODEL, 768), F32), compiler_params=_params(),
    )(dw_g, dw_v)


def _cols_to_rows(full):
    a, nb = full.shape
    return full.reshape(a, N_DEV, nb // N_DEV).transpose(1, 0, 2)


def _rows_to_cols(blocks):
    n, a, b = blocks.shape
    return blocks.transpose(1, 0, 2).reshape(a, n * b)


def _block_diag_in(bb):
    b3 = bb.reshape(SSM_GROUPS, SSM_STATE, SSM_GROUP_CH).transpose(0, 2, 1)
    eye = jnp.eye(SSM_GROUPS, dtype=bb.dtype)
    return (b3[:, :, None, :] * eye[:, None, :, None]).reshape(SSM_WIDTH, SSM_N)


def _block_diag_out(cc):
    c3 = cc.transpose(0, 2, 1)
    eye = jnp.eye(SSM_GROUPS, dtype=cc.dtype)
    return (c3[:, :, None, :] * eye[:, None, :, None]).reshape(SSM_N, SSM_WIDTH)


def _diag_blocks(mats, rows_per, cols_per):
    nw = mats.shape[0]
    per = SSM_GROUPS // nw
    m5 = mats.reshape(nw, per, rows_per, per, cols_per)
    eye = jnp.eye(per, dtype=mats.dtype)
    return jnp.sum(m5 * eye[None, :, None, :, None], axis=3).reshape(SSM_GROUPS, rows_per, cols_per)


def _blk(gath, offs, grp, n, a):
    r0 = offs[grp][n]
    return gath[grp][:, r0:r0 + a, :]


def _early_weights(gath, offs):
    o = dict(zip([p for p, _ in IN_PARTS], _unpack_w_in(gath["g640"], offs["g640"]["w_in"])))
    o["wqb"] = _rows_to_cols(_blk(gath, offs, "g128a", "w_q_b", Q_LORA))
    wkv = _rows_to_cols(_blk(gath, offs, "g128a", "w_kv_b", KV_LORA)).reshape(KV_LORA, MLA_HEADS, D_NOPE + D_V)
    o["wk"] = jnp.pad(wkv[:, :, :D_NOPE], ((0, 0), (0, 0), (0, LANES - D_NOPE))).reshape(KV_LORA, MLA_HEADS * LANES)
    o["wv"] = jnp.pad(wkv[:, :, D_NOPE:], ((0, 0), (0, 0), (0, LANES - D_V))).reshape(KV_LORA, MLA_HEADS * LANES)
    o["wv_t"] = o["wv"].T
    return o


def _late_weights(gath, offs):
    def blk(grp, n, a):
        return _blk(gath, offs, grp, n, a)

    o = {}
    wo = _rows_to_cols(blk("g128b", "w_o_mla", MLA_HEADS * D_V)).reshape(MLA_HEADS, D_V, D_MODEL)
    o["wo_mla"] = jnp.pad(wo, ((0, 0), (0, LANES - D_V), (0, 0))).reshape(MLA_HEADS * LANES, D_MODEL)
    o["w_o_ssm"] = _rows_to_cols(blk("g128b", "w_o_ssm", SSM_WIDTH))
    o["w_o_cross"] = _rows_to_cols(blk("g128b", "w_o_cross", X_WIDTH))
    o["w_glu"] = blk("g512", "w_glu", SSM_WIDTH // N_DEV).reshape(SSM_WIDTH, SSM_WIDTH)
    o["w_mem_kv"] = blk("g1024", "w_mem_kv", D_MODEL // N_DEV).reshape(D_MODEL, 2 * X_WIDTH)
    o["w_out"] = blk("g1024", "w_out", D_MODEL // N_DEV).reshape(D_MODEL, D_MODEL)
    o["w_down"] = blk("g1024", "w_down", D_FF // N_DEV).reshape(D_FF, D_MODEL)
    o["w_up"] = _unpack_w_up(gath["g768"], offs["g768"]["w_up"])
    o["w_up_g"], o["w_up_v"] = o["w_up"][:, :D_FF], o["w_up"][:, D_FF:]
    o["conv_w"] = _rows_to_cols(blk("gconv", "conv_w", 3)[:, :, :UP_SHARD])
    return o


def _row(v):
    return v.reshape(1, -1).astype(F32)


def _pad_lanes(v, n=LANES):
    return jnp.pad(v, (0, n - v.shape[0])).reshape(1, n).astype(F32)


def _layer_fwd(x, mem, tabs, w, p, side, late):
    l = x.shape[0]
    rope_c, rope_sa, rope_sb = tabs
    s = {"x": x}
    g_mix, g_qa, g_kva = _row(p["norm_mix_g"]), _row(p["q_a_norm_g"]), _row(p["kv_a_norm_g"])
    g_q, g_k = _pad_lanes(p["q_norm_g"]), _pad_lanes(p["k_norm_g"])

    (h,) = _rowwise("rms_mix", lambda i, r, c: (_rms_f(r[0][...], c[0][...], D_MODEL),), l, 512,
                    [(x, D_MODEL, 0, "row")], [g_mix], [(D_MODEL, BF16)])
    pqkv = _mm("proj_qkv", [(h, w["wqkv"])])
    u = _mm("proj_u", [(h, w["w_u"])])
    xq = _mm("proj_xq", [(h, w["w_xq"])])
    gl = _mm("proj_gate", [(h, w["w_g"])], bm=1024, bn_cap=1024)
    s.update(h=h, pqkv=pqkv, u=u, xq=xq, gl=gl)

    def prep_a(i, r, c):
        return (_rms_f(r[0][:, :Q_LORA], c[0][...], Q_LORA),
                _rms_f(r[0][:, Q_LORA:Q_LORA + KV_LORA], c[1][...], KV_LORA))

    nq, nkv = _rowwise("mla_prep_a", prep_a, l, 512, [(pqkv, QKV_W, 0, "row")], [g_qa, g_kva],
                       [(Q_LORA, BF16), (KV_LORA, BF16)])
    q_raw = _mm("mla_q_b", [(nq, w["wqb"])], bn_cap=1024)
    k_raw = _mm("mla_k_b", [(nkv, w["wk"])], bn_cap=1024)
    v_mla = _mm("mla_v_b", [(nkv, w["wv"])], out_dtype=BF16, bn_cap=1024)
    vt_mla = _mm("mla_vt_b", [(w["wv_t"], nkv)], trans_b=True, out_dtype=BF16, bn_cap=1024)

    def prep_b(i, r, c):
        q_ref, k_ref, kr_ref, c_ref, sa_ref, sb_ref = r
        rc, sa, sb, kr = c_ref[...], sa_ref[...], sb_ref[...], kr_ref[...]
        qs, ks = [], []
        for hd in range(MLA_HEADS):
            cols = slice(hd * LANES, (hd + 1) * LANES)
            qs.append(_rope_f(_rms_f(q_ref[:, cols], c[0][...], D_QK), rc, sa, sb))
            ks.append(_rope_f(_rms_f(k_ref[:, cols] + kr, c[1][...], D_QK), rc, sa, sb))
        return jnp.concatenate(qs, axis=1), jnp.concatenate(ks, axis=1)

    hw = MLA_HEADS * LANES
    kr_blk = (Q_LORA + KV_LORA) // LANES
    tab_ins = [(rope_c, LANES, 0, "row"), (rope_sa, LANES, 0, "row"), (rope_sb, LANES, 0, "row")]
    q, k = _rowwise("mla_prep_b", prep_b, l, 256,
                    [(q_raw, hw, 0, "row"), (k_raw, hw, 0, "row"), (pqkv, LANES, kr_blk, "row")] + tab_ins,
                    [g_q, g_k], [(hw, BF16), (hw, BF16)])
    o_a, lse_a, rode = _attn_fwd("mla_attn_fwd", q, k, vt_mla, qoff=0, koff=0, voff=0, heads=MLA_HEADS, causal=True,
                                 scale=D_QK ** -0.5, bq=1024, bk=1024, side=side)
    w_late, rode = late(rode)
    w = {**w, **w_late}
    ya = _mm("mla_o", [(o_a, w["wo_mla"])], bn_cap=1024)
    s.update(nq=nq, nkv=nkv, q_raw=q_raw, k_raw=k_raw, v_mla=v_mla, q=q, k=k, o_a=o_a, lse_a=lse_a, ya=ya)

    lr = p["ssm_lambda_re"].reshape(SSM_N, 1)
    li = p["ssm_lambda_im"].reshape(SSM_N, 1)
    ldt = jnp.repeat(p["ssm_log_dt"], SSM_STATE).reshape(SSM_N, 1)
    br = p["ssm_b_re"].reshape(SSM_N, SSM_GROUP_CH)
    bi = p["ssm_b_im"].reshape(SSM_N, SSM_GROUP_CH)
    a_re, a_im, bb_re, bb_im = _disc_fwd(lr, li, ldt, br, bi)
    bb_re_d, bb_im_d = _block_diag_in(bb_re).astype(BF16), _block_diag_in(bb_im).astype(BF16)
    cc_re_d = _block_diag_out(p["ssm_c_re"]).astype(BF16)
    cc_imn_d = _block_diag_out(-p["ssm_c_im"]).astype(BF16)
    a_re_row, a_im_row = a_re.reshape(1, SSM_N), a_im.reshape(1, SSM_N)
    d_row = _row(p["ssm_d"])
    b_glu = _row(p["b_glu"])
    s_re, s_im, ypre = _s5_fwd("s5_fwd", u, bb_re_d, bb_im_d, cc_re_d, cc_imn_d, a_re_row, a_im_row)

    def ssm_y(i, r, c):
        return (_gelu(r[0][...] + c[0][...] * r[1][...]),)

    (y_b,) = _rowwise("s5_gelu", ssm_y, l, 512, [(ypre, SSM_WIDTH, 0, "row"), (u, SSM_WIDTH, 0, "row")], [d_row],
                      [(SSM_WIDTH, BF16)])
    z = _mm("s5_glu", [(y_b, w["w_glu"])])

    def ssm_out(i, r, c):
        y = _gelu(r[0][...] + c[0][...] * r[1][...])
        return (y * jax.nn.sigmoid(r[2][...] + c[1][...]),)

    (out_b,) = _rowwise("s5_glu_out", ssm_out, l, 512,
                        [(ypre, SSM_WIDTH, 0, "row"), (u, SSM_WIDTH, 0, "row"), (z, SSM_WIDTH, 0, "row")],
                        [d_row, b_glu], [(SSM_WIDTH, BF16)])
    yb = _mm("s5_o", [(out_b, w["w_o_ssm"])], bn_cap=1024)
    s.update(disc=(lr, li, ldt, br, bi), a_rows=(a_re_row, a_im_row), bb_d=(bb_re_d, bb_im_d),
             cc_d=(cc_re_d, cc_imn_d), s_re=s_re, s_im=s_im, ypre=ypre, y_b=y_b, z=z, out_b=out_b, yb=yb)

    g_mem, g_xq, g_xk = _row(p["mem_norm_g"]), _row(p["xq_norm_g"]), _row(p["xk_norm_g"])
    ml = mem.shape[0]
    (memn,) = _rowwise("rms_mem", lambda i, r, c: (_rms_f(r[0][...], c[0][...], D_MODEL),), ml, 256,
                       [(mem, D_MODEL, 0, "row")], [g_mem], [(D_MODEL, BF16)])
    kvm = _mm("cross_kv", [(memn, w["w_mem_kv"])], bn_cap=1024)

    def headnorm(i, r, c):
        return (jnp.concatenate([_rms_f(r[0][:, hd * LANES:(hd + 1) * LANES], c[0][...], X_HEAD_DIM)
                                 for hd in range(X_HEADS)], axis=1),)

    (xk,) = _rowwise("cross_k_norm", headnorm, ml, 256, [(kvm, X_WIDTH, 0, "row")], [g_xk], [(X_WIDTH, BF16)])
    (xqn,) = _rowwise("cross_q_norm", headnorm, l, 512, [(xq, X_WIDTH, 0, "row")], [g_xq], [(X_WIDTH, BF16)])
    xvt = kvm[:, X_WIDTH:].T.astype(BF16)
    o_c, lse_c, _ = _attn_fwd("cross_attn_fwd", xqn, xk, xvt, qoff=0, koff=0, voff=0, heads=X_HEADS,
                              causal=False, scale=X_HEAD_DIM ** -0.5, bq=1024, bk=256)
    yc = _mm("cross_o", [(o_c, w["w_o_cross"])], bn_cap=1024)
    s.update(memn=memn, kvm=kvm, xk=xk, xqn=xqn, o_c=o_c, lse_c=lse_c, yc=yc)

    b_gate = _row(p["b_gate"])

    def merge(i, r, c):
        acc = None
        for br_ in range(3):
            g = jax.nn.sigmoid(r[br_][...] + c[0][:, br_ * D_MODEL:(br_ + 1) * D_MODEL])
            t = g * r[3 + br_][...]
            acc = t if acc is None else acc + t
        return (acc,)

    gate_ins = [(gl, D_MODEL, b_, "row") for b_ in range(3)]
    (merged,) = _rowwise("merge", merge, l, 256,
                         gate_ins + [(ya, D_MODEL, 0, "row"), (yb, D_MODEL, 0, "row"), (yc, D_MODEL, 0, "row")],
                         [b_gate], [(D_MODEL, BF16)])
    x1 = _mm("mix_out", [(merged, w["w_out"])], add=x, bn_cap=1024)
    s.update(merged=merged, x1=x1)

    g_ffn = _row(p["norm_ffn_g"])
    (h2,) = _rowwise("rms_ffn", lambda i, r, c: (_rms_f(r[0][...], c[0][...], D_MODEL),), l, 512,
                     [(x1, D_MODEL, 0, "row")], [g_ffn], [(D_MODEL, BF16)])
    up = _mm("ffn_up", [(h2, w["w_up"])], bm=1024, bn_cap=1408)
    conv_w = w["conv_w"]
    conv_b = _row(p["conv_b"])

    def conv_glu(i, r, c):
        cg = _conv(r[0], r[2], i, c[0], c[1], 0)
        cv = _conv(r[1], r[3], i, c[0], c[1], D_FF)
        return (cg * jax.nn.sigmoid(cg) * cv,)

    up_ins = [(up, D_FF, 0, "row"), (up, D_FF, 1, "row"), (up, D_FF, 0, "prev"), (up, D_FF, 1, "prev")]
    (act,) = _rowwise("ffn_conv_glu", conv_glu, l, 256, up_ins, [conv_w, conv_b], [(D_FF, BF16)])
    x2 = _mm("ffn_down", [(act, w["w_down"])], add=x1, bm=1024, bn_cap=1024)
    s.update(h2=h2, up=up, act=act, conv_w=conv_w, conv_b=conv_b)
    return x2, s, w, rode


def _conv(x_ref, halo_ref, i, w_ref, b_ref, col0):
    x = x_ref[...]
    cols = slice(col0, col0 + D_FF)
    return (w_ref[0:1, cols] * _shift_down(x, halo_ref, i, 2) + w_ref[1:2, cols] * _shift_down(x, halo_ref, i, 1)
            + w_ref[2:3, cols] * x + b_ref[:, cols])


def _layer_bwd(dx2, dx2_b, s, mem, tabs, w, p, make_side):
    l = dx2.shape[0]
    rope_c, rope_sa, rope_sb = tabs
    x, x1 = s["x"], s["x1"]
    g = {}

    dact = _mm("ffn_down_dx", [(dx2_b, w["w_down"])], trans_b=True, bn_cap=1408)
    g["w_down"] = _mm_tn("ffn_down_dw", s["act"], dx2_b, bm_cap=1408).reshape(N_DEV, D_FF // N_DEV, D_MODEL)
    up = s["up"]
    nblk_c = l // min(256, l)

    bm_c = min(256, l)

    def conv_bwd(i, r, c):
        da_ref, xg_ref, xv_ref, hg_ref, hv_ref, dan_ref, ng_ref, nv_ref = r
        row8 = _row_ids(8)
        live_next = (i < nblk_c - 1).astype(F32)

        def conv_rows(x, xm1, xm2, cols):
            return c[0][0:1, cols] * xm2 + c[0][1:2, cols] * xm1 + c[0][2:3, cols] * x + c[1][:, cols]

        def tail_shift(x8, x_ref, k):
            out = pltpu.roll(x8, k, 0)
            for q in range(k):
                e = (row8 == q).astype(F32)
                out = out * (1.0 - e) + e * x_ref[bm_c - k + q:bm_c - k + q + 1, :]
            return out

        def glu_grads(da, cg, cv):
            sig = jax.nn.sigmoid(cg)
            return da * cv * (sig * (1.0 + cg * (1.0 - sig))), da * (cg * sig)

        def up_shift(d, d8, k):
            out = pltpu.roll(d, bm_c - k, 0)
            last = out[bm_c - 8:]
            for q in range(k):
                e = (row8 == 8 - k + q).astype(F32)
                nxt = jnp.sum(d8 * (row8 == q).astype(F32), axis=0, keepdims=True)
                last = last * (1.0 - e) + e * nxt
            return jnp.concatenate([out[:bm_c - 8], last], axis=0)

        halves = []
        for half, (x_ref, h_ref, n_ref) in enumerate(((xg_ref, hg_ref, ng_ref), (xv_ref, hv_ref, nv_ref))):
            cols = slice(half * D_FF, (half + 1) * D_FF)
            x, x8 = x_ref[...], n_ref[...]
            xm1, xm2 = _shift_down(x, h_ref, i, 1), _shift_down(x, h_ref, i, 2)
            halves.append((conv_rows(x, xm1, xm2, cols), x, xm1, xm2,
                           conv_rows(x8, tail_shift(x8, x_ref, 1), tail_shift(x8, x_ref, 2), cols)))
        (cg, xg, xg1, xg2, cg8), (cvv, xv, xv1, xv2, cv8) = halves
        dcg, dcv = glu_grads(da_ref[...], cg, cvv)
        dcg8, dcv8 = glu_grads(dan_ref[...] * live_next, cg8, cv8)
        outs, accs = [], []
        for half, (d, d8, (x0, xa, xb)) in enumerate(((dcg, dcg8, (xg, xg1, xg2)), (dcv, dcv8, (xv, xv1, xv2)))):
            cols = slice(half * D_FF, (half + 1) * D_FF)
            accs += [jnp.sum(d, axis=0, keepdims=True), jnp.sum(d * xb, axis=0, keepdims=True),
                     jnp.sum(d * xa, axis=0, keepdims=True), jnp.sum(d * x0, axis=0, keepdims=True)]
            outs.append(c[0][2:3, cols] * d + c[0][1:2, cols] * up_shift(d, d8, 1) + c[0][0:1, cols] * up_shift(d, d8, 2))
        return (*outs, *accs)

    conv_ins = [(dact, D_FF, 0, "row"), (up, D_FF, 0, "row"), (up, D_FF, 1, "row"), (up, D_FF, 0, "prev"),
                (up, D_FF, 1, "prev"), (dact, D_FF, 0, "next"), (up, D_FF, 0, "next"), (up, D_FF, 1, "next")]
    res = _rowwise("ffn_conv_glu_bwd", conv_bwd, l, bm_c, conv_ins, [s["conv_w"], s["conv_b"]],
                   [(D_FF, BF16), (D_FF, BF16)], [(1, D_FF)] * 8)
    dup_g, dup_v = res[0], res[1]
    db_g, dw0_g, dw1_g, dw2_g, db_v, dw0_v, dw1_v, dw2_v = res[2:]
    g["conv_b"] = jnp.concatenate([db_g, db_v], axis=1)[0]
    g["conv_w"] = _cols_to_rows(jnp.concatenate(
        [jnp.concatenate([dw0_g, dw0_v], axis=1), jnp.concatenate([dw1_g, dw1_v], axis=1),
         jnp.concatenate([dw2_g, dw2_v], axis=1)], axis=0))

    dh2 = _mm("ffn_up_dx", [(dup_g, w["w_up_g"]), (dup_v, w["w_up_v"])], trans_b=True, bm=512, bn_cap=1024)
    g["w_up"] = _pack_w_up(_mm_tn("ffn_up_dw_g", s["h2"], dup_g), _mm_tn("ffn_up_dw_v", s["h2"], dup_v))

    def rms_bwd_res(i, r, c):
        dx, dg = _rms_b(r[0][...], c[0][...], r[1][...], D_MODEL)
        dx = dx + r[2][...]
        return (dx, dx, dg)

    dx1, dx1_b, g["norm_ffn_g"] = _rowwise(
        "rms_ffn_bwd", rms_bwd_res, l, 512, [(x1, D_MODEL, 0, "row"), (dh2, D_MODEL, 0, "row"), (dx2, D_MODEL, 0, "row")],
        [_row(p["norm_ffn_g"])], [(D_MODEL, F32), (D_MODEL, BF16)], [(1, D_MODEL)])

    dmerged = _mm("mix_out_dx", [(dx1_b, w["w_out"])], trans_b=True, bn_cap=1024)
    g["w_out"] = _mm_tn("mix_out_dw", s["merged"], dx1_b).reshape(N_DEV, D_MODEL // N_DEV, D_MODEL)
    gl, ya, yb, yc = s["gl"], s["ya"], s["yb"], s["yc"]

    def merge_bwd(i, r, c):
        dm = r[0][...]
        dys, dgs = [], []
        for b_ in range(3):
            gate = jax.nn.sigmoid(r[1 + b_][...] + c[0][:, b_ * D_MODEL:(b_ + 1) * D_MODEL])
            dys.append(dm * gate)
            dgs.append(dm * r[4 + b_][...] * (gate * (1.0 - gate)))
        dgl = jnp.concatenate(dgs, axis=1)
        return (*dys, dgl, jnp.sum(dgl, axis=0, keepdims=True))

    gate_ins = [(gl, D_MODEL, b_, "row") for b_ in range(3)]
    dya, dyb, dyc, dgl, db_gate = _rowwise(
        "merge_bwd", merge_bwd, l, 256,
        [(dmerged, D_MODEL, 0, "row")] + gate_ins + [(ya, D_MODEL, 0, "row"), (yb, D_MODEL, 0, "row"),
                                                     (yc, D_MODEL, 0, "row")],
        [_row(p["b_gate"])], [(D_MODEL, BF16)] * 3 + [(3 * D_MODEL, BF16)], [(1, 3 * D_MODEL)])
    g["b_gate"] = db_gate[0]

    do_c = _mm("cross_o_dx", [(dyc, w["w_o_cross"])], trans_b=True, out_dtype=BF16)
    g["w_o_cross"] = _cols_to_rows(_mm_tn("cross_o_dw", s["o_c"], dyc))
    kvm = s["kvm"]
    delta_c = _attn_delta("cross_attn_delta", s["o_c"], do_c, heads=X_HEADS, bq=2048)
    dxqn, dxk, dxv, _ = _attn_bwd("cross_attn_bwd", s["xqn"], s["xk"], s["xk"].T, kvm, do_c, s["lse_c"], delta_c,
                                  qoff=0, koff=0, voff=X_HEADS, heads=X_HEADS, causal=False,
                                  scale=X_HEAD_DIM ** -0.5, bq=1024, bk=256)
    ml = mem.shape[0]

    def headnorm_bwd(i, r, c):
        dxs, dg = [], None
        for hd in range(X_HEADS):
            cols = slice(hd * LANES, (hd + 1) * LANES)
            dx_h, dg_h = _rms_b(r[0][:, cols], c[0][...], r[1][:, cols], X_HEAD_DIM)
            dxs.append(dx_h)
            dg = dg_h if dg is None else dg + dg_h
        return (jnp.concatenate(dxs, axis=1), dg)

    dxq, dg_xq = _rowwise("cross_q_norm_bwd", headnorm_bwd, l, 512,
                          [(s["xq"], X_WIDTH, 0, "row"), (dxqn, X_WIDTH, 0, "row")], [_row(p["xq_norm_g"])],
                          [(X_WIDTH, BF16)], [(1, X_HEAD_DIM)])
    dkvm_k, dg_xk = _rowwise("cross_k_norm_bwd", headnorm_bwd, ml, 256,
                             [(kvm, X_WIDTH, 0, "row"), (dxk, X_WIDTH, 0, "row")], [_row(p["xk_norm_g"])],
                             [(X_WIDTH, F32)], [(1, X_HEAD_DIM)])
    g["xq_norm_g"], g["xk_norm_g"] = dg_xq[0], dg_xk[0]
    dkvm = jnp.concatenate([dkvm_k, dxv], axis=1)
    g["w_mem_kv"] = _mm_tn("cross_kv_dw", s["memn"], dkvm).reshape(N_DEV, D_MODEL // N_DEV, 2 * X_WIDTH)
    dmemn = _mm("cross_kv_dx", [(dkvm, w["w_mem_kv"])], trans_b=True, bn_cap=1024)

    def rms_bwd_gain_only(i, r, c):
        return (_rms_b(r[0][...], c[0][...], r[1][...], D_MODEL)[1],)

    (dg_mem,) = _rowwise("rms_mem_bwd", rms_bwd_gain_only, ml, 256,
                         [(mem, D_MODEL, 0, "row"), (dmemn, D_MODEL, 0, "row")], [_row(p["mem_norm_g"])], [],
                         [(1, D_MODEL)])
    g["mem_norm_g"] = dg_mem[0]

    dout_b = _mm("s5_o_dx", [(dyb, w["w_o_ssm"])], trans_b=True)
    g["w_o_ssm"] = _cols_to_rows(_mm_tn("s5_o_dw", s["out_b"], dyb))
    ypre, u, z = s["ypre"], s["u"], s["z"]
    d_row, b_glu = _row(p["ssm_d"]), _row(p["b_glu"])
    yuz = [(ypre, SSM_WIDTH, 0, "row"), (u, SSM_WIDTH, 0, "row"), (z, SSM_WIDTH, 0, "row")]

    def glu_bwd_z(i, r, c):
        y = _gelu(r[1][...] + c[0][...] * r[2][...])
        sg = jax.nn.sigmoid(r[3][...] + c[1][...])
        dz = r[0][...] * y * (sg * (1.0 - sg))
        return (dz, jnp.sum(dz, axis=0, keepdims=True))

    dz, db_glu = _rowwise("s5_glu_bwd_z", glu_bwd_z, l, 512, [(dout_b, SSM_WIDTH, 0, "row")] + yuz, [d_row, b_glu],
                          [(SSM_WIDTH, BF16)], [(1, SSM_WIDTH)])
    g["b_glu"] = db_glu[0]
    g["w_glu"] = _mm_tn("s5_glu_dw", s["y_b"], dz).reshape(N_DEV, SSM_WIDTH // N_DEV, SSM_WIDTH)
    dy2 = _mm("s5_glu_dx", [(dz, w["w_glu"])], trans_b=True)

    def gelu_bwd(i, r, c):
        t = r[2][...] + c[0][...] * r[3][...]
        sg = jax.nn.sigmoid(r[4][...] + c[1][...])
        dt = (r[0][...] * sg + r[1][...]) * _gelu_grad(t)
        return (dt, c[0][...] * dt, jnp.sum(dt * r[3][...], axis=0, keepdims=True))

    dypre, du_skip, dd = _rowwise(
        "s5_gelu_bwd", gelu_bwd, l, 512, [(dout_b, SSM_WIDTH, 0, "row"), (dy2, SSM_WIDTH, 0, "row")] + yuz,
        [d_row, b_glu], [(SSM_WIDTH, BF16), (SSM_WIDTH, F32)], [(1, SSM_WIDTH)])
    g["ssm_d"] = dd.reshape(SSM_GROUPS, SSM_GROUP_CH)
    cc_re_d, cc_imn_d = s["cc_d"]
    bb_re_d, bb_im_d = s["bb_d"]
    a_re_row, a_im_row = s["a_rows"]
    s_re, s_im = s["s_re"], s["s_im"]
    du, da_re, da_im, dbb_re_d, dbb_im_d, dcc_re, dcc_imn = _s5_bwd(
        "s5_bwd", dypre, du_skip, u, s_re, s_im, bb_re_d, bb_im_d, cc_re_d, cc_imn_d, a_re_row, -a_im_row)
    g["ssm_c_re"] = _diag_blocks(dcc_re, SSM_STATE, SSM_GROUP_CH).transpose(0, 2, 1)
    g["ssm_c_im"] = -_diag_blocks(dcc_imn, SSM_STATE, SSM_GROUP_CH).transpose(0, 2, 1)
    dbb_re = _diag_blocks(dbb_re_d, SSM_GROUP_CH, SSM_STATE).transpose(0, 2, 1).reshape(SSM_N, SSM_GROUP_CH)
    dbb_im = _diag_blocks(dbb_im_d, SSM_GROUP_CH, SSM_STATE).transpose(0, 2, 1).reshape(SSM_N, SSM_GROUP_CH)
    dlr, dli, dldt, dbr, dbi = _disc_bwd(*s["disc"], da_re.reshape(SSM_N, 1), da_im.reshape(SSM_N, 1), dbb_re, dbb_im)
    g["ssm_lambda_re"] = dlr.reshape(SSM_GROUPS, SSM_STATE)
    g["ssm_lambda_im"] = dli.reshape(SSM_GROUPS, SSM_STATE)
    g["ssm_log_dt"] = dldt.reshape(SSM_GROUPS, SSM_STATE).sum(axis=1)
    g["ssm_b_re"] = dbr.reshape(SSM_GROUPS, SSM_STATE, SSM_GROUP_CH)
    g["ssm_b_im"] = dbi.reshape(SSM_GROUPS, SSM_STATE, SSM_GROUP_CH)

    do_a = _mm("mla_o_dx", [(dya, w["wo_mla"])], trans_b=True, out_dtype=BF16, bn_cap=1024)
    dwo = _mm_tn("mla_o_dw", s["o_a"], dya)
    g["w_o_mla"] = _cols_to_rows(dwo.reshape(MLA_HEADS, LANES, D_MODEL)[:, :D_V].reshape(MLA_HEADS * D_V, D_MODEL))
    delta_a = _attn_delta("mla_attn_delta", s["o_a"], do_a, heads=MLA_HEADS, bq=2048)
    dq, dk, dv, rode = _attn_bwd("mla_attn_bwd", s["q"], s["k"], s["k"].T, s["v_mla"], do_a, s["lse_a"], delta_a,
                                 qoff=0, koff=0, voff=0, heads=MLA_HEADS, causal=True, scale=D_QK ** -0.5, bq=1024,
                                 bk=1024, side=make_side(g))
    hw = MLA_HEADS * LANES
    kr_blk = (Q_LORA + KV_LORA) // LANES
    pqkv = s["pqkv"]
    g_q, g_k = _pad_lanes(p["q_norm_g"]), _pad_lanes(p["k_norm_g"])
    lane = lax.broadcasted_iota(jnp.int32, (1, LANES), 1)
    kr_mask = jnp.logical_and(lane >= KR_LO, lane < KR_LO + D_ROPE).astype(F32)

    def prep_b_bwd(i, r, c):
        dq_ref, dk_ref, q_ref, k_ref, kr_ref, c_ref, sa_ref, sb_ref = r
        rc, sa, sb, kr = c_ref[...], sa_ref[...], sb_ref[...], kr_ref[...]
        dqs, dks, dkr, dgq, dgk = [], [], None, None, None
        for hd in range(MLA_HEADS):
            cols = slice(hd * LANES, (hd + 1) * LANES)
            dxq, dgq_h = _rms_b(q_ref[:, cols], c[0][...], _rope_b(dq_ref[:, cols], rc, sa, sb), D_QK)
            dxk, dgk_h = _rms_b(k_ref[:, cols] + kr, c[1][...], _rope_b(dk_ref[:, cols], rc, sa, sb), D_QK)
            dqs.append(dxq)
            dks.append(dxk)
            dkr = dxk if dkr is None else dkr + dxk
            dgq = dgq_h if dgq is None else dgq + dgq_h
            dgk = dgk_h if dgk is None else dgk + dgk_h
        return (jnp.concatenate(dqs, axis=1), jnp.concatenate(dks, axis=1), dkr * c[2][...], dgq, dgk)

    tab_ins = [(rope_c, LANES, 0, "row"), (rope_sa, LANES, 0, "row"), (rope_sb, LANES, 0, "row")]
    dq_raw, dk_raw, dkr, dg_q, dg_k = _rowwise(
        "mla_prep_b_bwd", prep_b_bwd, l, 256,
        [(dq, hw, 0, "row"), (dk, hw, 0, "row"), (s["q_raw"], hw, 0, "row"), (s["k_raw"], hw, 0, "row"),
         (pqkv, LANES, kr_blk, "row")] + tab_ins, [g_q, g_k, kr_mask],
        [(hw, BF16), (hw, BF16), (LANES, F32)], [(1, LANES), (1, LANES)])
    g["q_norm_g"], g["k_norm_g"] = dg_q[0, :D_QK], dg_k[0, :D_QK]
    dnq = _mm("mla_q_b_dx", [(dq_raw, w["wqb"])], trans_b=True)
    dnkv = _mm("mla_kv_b_dx", [(dk_raw, w["wk"]), (dv, w["wv"])], trans_b=True)
    dwqb = _mm_tn("mla_q_b_dw", s["nq"], dq_raw)
    g["w_q_b"] = _cols_to_rows(dwqb)
    dwk = _mm_tn("mla_k_b_dw", s["nkv"], dk_raw).reshape(KV_LORA, MLA_HEADS, LANES)[:, :, :D_NOPE]
    dwv = _mm_tn("mla_v_b_dw", s["nkv"], dv).reshape(KV_LORA, MLA_HEADS, LANES)[:, :, :D_V]
    g["w_kv_b"] = jnp.concatenate([dwk, dwv], axis=2).transpose(1, 0, 2)

    def prep_a_bwd(i, r, c):
        dcq, dgqa = _rms_b(r[0][:, :Q_LORA], c[0][...], r[1][...], Q_LORA)
        dckv, dgkva = _rms_b(r[0][:, Q_LORA:Q_LORA + KV_LORA], c[1][...], r[2][...], KV_LORA)
        return (jnp.concatenate([dcq, dckv, r[3][...]], axis=1), dgqa, dgkva)

    dpqkv, dg_qa, dg_kva = _rowwise(
        "mla_prep_a_bwd", prep_a_bwd, l, 512,
        [(pqkv, QKV_W, 0, "row"), (dnq, Q_LORA, 0, "row"), (dnkv, KV_LORA, 0, "row"), (dkr, LANES, 0, "row")],
        [_row(p["q_a_norm_g"]), _row(p["kv_a_norm_g"])], [(QKV_W, BF16)], [(1, Q_LORA), (1, KV_LORA)])
    g["q_a_norm_g"], g["kv_a_norm_g"] = dg_qa[0], dg_kva[0]

    h = s["h"]
    dh = _mm("proj_dx", [(dpqkv, w["wqkv"]), (du, w["w_u"]), (dxq, w["w_xq"]), (dgl, w["w_g"])], trans_b=True,
             bm=512, bn_cap=1024)
    dwqkv = _mm_tn("proj_qkv_dw", h, dpqkv)
    g["w_in"] = _pack_w_in([dwqkv, _mm_tn("proj_u_dw", h, du), _mm_tn("proj_xq_dw", h, dxq),
                            _mm_tn("proj_gate_dw", h, dgl)])
    dx, dx_b, dg_mix = _rowwise(
        "rms_mix_bwd", rms_bwd_res, l, 512, [(x, D_MODEL, 0, "row"), (dh, D_MODEL, 0, "row"), (dx1, D_MODEL, 0, "row")],
        [_row(p["norm_mix_g"])], [(D_MODEL, F32), (D_MODEL, BF16)], [(1, D_MODEL)])
    g["norm_mix_g"] = dg_mix[0]
    g["norm_ffn_g"] = g["norm_ffn_g"][0]
    return dx, dx_b, g, rode


def _rope_tables(positions):
    inv_freq = ROPE_THETA ** (-jnp.arange(0, D_ROPE, 2, dtype=F32) / D_ROPE)
    ang = positions.astype(F32)[:, None] * inv_freq
    cos, sin = jnp.cos(ang), jnp.sin(ang)
    l = positions.shape[0]
    one, zero = jnp.ones((l, D_NOPE), F32), lambda n: jnp.zeros((l, n), F32)
    pad = LANES - D_QK
    rope_c = jnp.concatenate([one, cos, cos, zero(pad)], axis=1)
    rope_sa = jnp.concatenate([zero(D_NOPE), -sin, zero(16), zero(pad)], axis=1)
    rope_sb = jnp.concatenate([zero(D_NOPE + 16), sin, zero(pad)], axis=1)
    return rope_c, rope_sa, rope_sb


def kernel(x, mem, positions, norm_mix_g, w_in, q_a_norm_g, w_q_b, kv_a_norm_g, w_kv_b, q_norm_g, k_norm_g, w_o_mla, ssm_lambda_re, ssm_lambda_im, ssm_log_dt, ssm_b_re, ssm_b_im, ssm_c_re, ssm_c_im, ssm_d, w_glu, b_glu, w_o_ssm, mem_norm_g, w_mem_kv, xq_norm_g, xk_norm_g, w_o_cross, b_gate, w_out, norm_ffn_g, w_up, conv_w, conv_b, w_down, loss_target, m_norm_mix_g, m_w_in, m_q_a_norm_g, m_w_q_b, m_kv_a_norm_g, m_w_kv_b, m_q_norm_g, m_k_norm_g, m_w_o_mla, m_ssm_lambda_re, m_ssm_lambda_im, m_ssm_log_dt, m_ssm_b_re, m_ssm_b_im, m_ssm_c_re, m_ssm_c_im, m_ssm_d, m_w_glu, m_b_glu, m_w_o_ssm, m_mem_norm_g, m_w_mem_kv, m_xq_norm_g, m_xk_norm_g, m_w_o_cross, m_b_gate, m_w_out, m_norm_ffn_g, m_w_up, m_conv_w, m_conv_b, m_w_down, v_norm_mix_g, v_w_in, v_q_a_norm_g, v_w_q_b, v_kv_a_norm_g, v_w_kv_b, v_q_norm_g, v_k_norm_g, v_w_o_mla, v_ssm_lambda_re, v_ssm_lambda_im, v_ssm_log_dt, v_ssm_b_re, v_ssm_b_im, v_ssm_c_re, v_ssm_c_im, v_ssm_d, v_w_glu, v_b_glu, v_w_o_ssm, v_mem_norm_g, v_w_mem_kv, v_xq_norm_g, v_xk_norm_g, v_w_o_cross, v_b_gate, v_w_out, v_norm_ffn_g, v_w_up, v_conv_w, v_conv_b, v_w_down):
    a = dict(locals())
    wts = {n: a[n] for n in WEIGHT_ORDER}
    m_in = {n: a["m_" + n] for n in WEIGHT_ORDER}
    v_in = {n: a["v_" + n] for n in WEIGHT_ORDER}
    depth = norm_mix_g.shape[0]
    x0, mem0, pos0, tgt = x[0], mem[0], positions[0], loss_target[0]
    l = x0.shape[0]
    offs = {grp: _group_rows(params) for grp, _, params in GROUPS}
    early_names, late_names = [g[0] for g in EARLY_GROUPS], [g[0] for g in LATE_GROUPS]

    def srcs(i, groups):
        return [_group_local(width, params, wts, i, F32 if grp == "gconv" else BF16) for grp, width, params in groups]

    tabs = _rope_tables(pos0)
    layer_p = [{n: wts[n][i] for n in REPLICATED} for i in range(depth)]
    gath = dict(zip(early_names, _gather_all("gather_weights", srcs(0, EARLY_GROUPS))))

    def late(rode):
        return _late_weights(dict(zip(late_names, rode)), offs), rode[len(late_names):]

    saved, layer_w = [], []
    xc = x0
    for i in range(depth):
        ride = srcs(i, LATE_GROUPS) + (srcs(i + 1, EARLY_GROUPS) if i + 1 < depth else [])
        side = _Side(ride, _gather_shapes(ride), GATHER_SEMS, (_gather_start, _gather_relay, _gather_finish))
        xc, s, w_all, rode = _layer_fwd(xc, mem0, tabs, _early_weights(gath, offs), layer_p[i], side, late)
        gath = dict(zip(early_names, rode))
        layer_w.append(w_all)
        saved.append(s)

    def loss_fn(i, r, c):
        d = r[0][...] - r[1][...]
        dy = d * (1.0 / D_MODEL)
        return (dy, dy, jnp.sum(d * d, axis=0, keepdims=True))

    dy, dy_b, sq = _rowwise("loss", loss_fn, l, 512, [(xc, D_MODEL, 0, "row"), (tgt, D_MODEL, 0, "row")], [],
                            [(D_MODEL, F32), (D_MODEL, BF16)], [(1, D_MODEL)])
    loss = lax.psum(0.5 * jnp.sum(sq) / D_MODEL, ("x", "y", "c"))

    def core_sums(g, tag, groups, extra=()):
        sends = []
        for _, width, params in groups:
            blocks = [jnp.pad(g[n], ((0, 0), (0, _rows8(rows) - rows), (0, width - g[n].shape[2])))
                      for n, rows, _ in params]
            sends.append(jnp.concatenate(blocks, axis=1))
        sends += list(extra)
        got = _pair_exchange("exchange_grads_core_" + tag, sends)
        return [_pair_sum("grad_pair_sum_%s_%d" % (tag, k), s_, g_) for k, (s_, g_) in enumerate(zip(sends, got))]

    grads, rcvs = [None] * depth, {}
    dxc, dxc_b = dy, dy_b
    waiting = []
    for i in reversed(range(depth)):
        def make_side(g, i=i, waiting=waiting):
            parts = core_sums(g, "l%d_late" % i, LATE_GROUPS) + waiting
            return _Side(parts, _chip_shapes(parts), CHIP_SEMS, (_chip_start, _chip_finish))

        dxc, dxc_b, grads[i], rode = _layer_bwd(dxc, dxc_b, saved[i], mem0, tabs, layer_w[i], layer_p[i], make_side)
        rcvs.update({(i, nm): r for nm, r in zip(late_names, rode)})
        rcvs.update({(i + 1, nm): r for nm, r in zip(early_names, rode[len(late_names):])})
        waiting = core_sums(grads[i], "l%d_early" % i, EARLY_GROUPS) if i > 0 else []
    grad_x = dxc[None]
    rep_flat = jnp.concatenate([jnp.stack([grads[i][n] for i in range(depth)]).reshape(-1) for n in REPLICATED])
    rep_send = _rep_rows(jnp.broadcast_to(rep_flat[None], (N_DEV, rep_flat.shape[0])))
    last = _chip_exchange("exchange_grads_chip", core_sums(grads[0], "l0_early", EARLY_GROUPS, [rep_send]))
    rcvs.update({(0, nm): r for nm, r in zip(early_names, last)})
    rcv_rep = last[-1]

    per_layer = {}
    for i in range(depth):
        for grp, width, params in GROUPS:
            local = [_group_local(width, params, d, i, F32) for d in (wts, m_in, v_in)]
            res = _adamw("adamw_l%d_%s" % (i, grp), rcvs[(i, grp)], *local)
            for tag, arr in zip(("grad", "delta", "m", "v"), res):
                for n, rows, cols in params:
                    per_layer[(tag, n, i)] = arr[offs[grp][n]:offs[grp][n] + rows, :cols]
    outs = {(tag, n): jnp.stack([per_layer[(tag, n, i)] for i in range(depth)])
            for tag in ("grad", "delta", "m", "v") for _, _, params in GROUPS for n, _, _ in params}
    rep_local = [_rep_rows(jnp.concatenate([d[n].astype(F32).reshape(-1) for n in REPLICATED])) for d in (wts, m_in, v_in)]
    res = _adamw("adamw_rep", rcv_rep, *rep_local)
    for tag, arr in zip(("grad", "delta", "m", "v"), res):
        flat, off = arr.reshape(-1), 0
        for n in REPLICATED:
            cnt = wts[n].size
            outs[(tag, n)] = flat[off:off + cnt].reshape(wts[n].shape)
            off += cnt
    result = [loss, grad_x]
    for tag in ("grad", "delta", "m", "v"):
        result += [outs[(tag, n)] for n in WEIGHT_ORDER]
    return tuple(result)
```

```python
import collections
import math

import jax
import jax.numpy as jnp
from jax import lax
from jax.experimental import pallas as pl
from jax.experimental.pallas import tpu as pltpu

F32 = jnp.float32
BF16 = jnp.bfloat16

N_DEV = 8
LANES = 128
LOG2E = math.log2(math.e)
VMEM_LIMIT_BYTES = 56 * 1024 * 1024

D_MODEL = 1024
EPS = 1e-6
MLA_HEADS = 8
Q_LORA = 384
KV_LORA = 256
D_NOPE = 64
D_ROPE = 32
D_QK = D_NOPE + D_ROPE
D_V = 64
ROPE_THETA = 10000.0
SSM_GROUPS = 32
SSM_GROUP_CH = 16
SSM_WIDTH = 512
SSM_STATE = 64
SSM_N = SSM_GROUPS * SSM_STATE
X_HEADS = 4
X_HEAD_DIM = 128
X_WIDTH = 512
D_FF = 2816
IN_WIDTH = Q_LORA + KV_LORA + D_ROPE + SSM_WIDTH + X_WIDTH + 3 * D_MODEL
QKV_W = Q_LORA + KV_LORA + LANES
KR_LO = D_NOPE

ADAM_LR = 0.001
ADAM_B1 = 0.9
ADAM_B2 = 0.999
ADAM_EPS = 1e-08
ADAM_WD = 0.01
ADAM_STEP = 10

REPLICATED = (
    "norm_mix_g", "q_a_norm_g", "kv_a_norm_g", "q_norm_g", "k_norm_g", "ssm_lambda_re", "ssm_lambda_im",
    "ssm_log_dt", "ssm_b_re", "ssm_b_im", "ssm_c_re", "ssm_c_im", "ssm_d", "b_glu", "mem_norm_g",
    "xq_norm_g", "xk_norm_g", "b_gate", "norm_ffn_g", "conv_b",
)
WEIGHT_ORDER = (
    "norm_mix_g", "w_in", "q_a_norm_g", "w_q_b", "kv_a_norm_g", "w_kv_b", "q_norm_g", "k_norm_g", "w_o_mla",
    "ssm_lambda_re", "ssm_lambda_im", "ssm_log_dt", "ssm_b_re", "ssm_b_im", "ssm_c_re", "ssm_c_im", "ssm_d",
    "w_glu", "b_glu", "w_o_ssm", "mem_norm_g", "w_mem_kv", "xq_norm_g", "xk_norm_g", "w_o_cross", "b_gate",
    "w_out", "norm_ffn_g", "w_up", "conv_w", "conv_b", "w_down",
)


def _params(**kw):
    return pltpu.CompilerParams(vmem_limit_bytes=VMEM_LIMIT_BYTES, **kw)


def _pick(n, cap):
    if n <= cap:
        return n
    best = None
    for m in range(LANES, cap + 1, LANES):
        if n % m == 0:
            best = m
    assert best is not None, n
    return best


_NN = (((1,), (0,)), ((), ()))
_NT = (((1,), (1,)), ((), ()))
_TN = (((0,), (0,)), ((), ()))


def _dot(a, b, dn):
    return lax.dot_general(a.astype(BF16), b.astype(BF16), dn, preferred_element_type=F32)


def _mm(name, pairs, *, trans_b=False, add=None, out_dtype=F32, bm=1024, bn_cap=512):
    m = pairs[0][0].shape[0]
    n = pairs[0][1].shape[0 if trans_b else 1]
    bm = min(bm, m)
    bn = _pick(n, bn_cap)
    npair = len(pairs)

    def body(*refs):
        o_ref = refs[-1]
        acc = None
        for p in range(npair):
            d = _dot(refs[2 * p][...], refs[2 * p + 1][...], _NT if trans_b else _NN)
            acc = d if acc is None else acc + d
        if add is not None:
            acc = acc + refs[2 * npair][...]
        o_ref[...] = acc.astype(out_dtype)

    in_specs, args = [], []
    for a, b in pairs:
        k = a.shape[1]
        in_specs.append(pl.BlockSpec((bm, k), lambda i, j: (i, 0)))
        if trans_b:
            in_specs.append(pl.BlockSpec((bn, k), lambda i, j: (j, 0)))
        else:
            in_specs.append(pl.BlockSpec((k, bn), lambda i, j: (0, j)))
        args += [a, b]
    if add is not None:
        in_specs.append(pl.BlockSpec((bm, bn), lambda i, j: (i, j)))
        args.append(add)
    return pl.pallas_call(
        body, name=name, grid=(m // bm, n // bn), in_specs=in_specs,
        out_specs=pl.BlockSpec((bm, bn), lambda i, j: (i, j)),
        out_shape=jax.ShapeDtypeStruct((m, n), out_dtype), compiler_params=_params(),
    )(*args)


def _mm_tn(name, a, b, *, bm_cap=512, bn_cap=1536, bk=2048):
    l, m = a.shape
    n = b.shape[1]
    bm, bn, bk = _pick(m, bm_cap), _pick(n, bn_cap), min(bk, l)

    def body(a_ref, b_ref, o_ref):
        @pl.when(pl.program_id(2) == 0)
        def _():
            o_ref[...] = jnp.zeros_like(o_ref)

        o_ref[...] += _dot(a_ref[...], b_ref[...], _TN)

    return pl.pallas_call(
        body, name=name, grid=(m // bm, n // bn, l // bk),
        in_specs=[pl.BlockSpec((bk, bm), lambda i, j, k: (k, i)), pl.BlockSpec((bk, bn), lambda i, j, k: (k, j))],
        out_specs=pl.BlockSpec((bm, bn), lambda i, j, k: (i, j)),
        out_shape=jax.ShapeDtypeStruct((m, n), F32), compiler_params=_params(),
    )(a, b)


def _rowwise(name, fn, nrows, bm, row_ins, consts, row_outs, acc_outs=()):
    bm = min(bm, nrows)
    nblk = nrows // bm
    sub = bm // 8
    nin, nc, nro = len(row_ins), len(consts), len(row_outs)

    def body(*refs):
        i = pl.program_id(0)
        outs = fn(i, refs[:nin], refs[nin:nin + nc])
        o_refs = refs[nin + nc:nin + nc + nro]
        a_refs = refs[nin + nc + nro:]
        for r, v in zip(o_refs, outs[:nro]):
            r[...] = v.astype(r.dtype)
        if a_refs:
            @pl.when(i == 0)
            def _():
                for r in a_refs:
                    r[...] = jnp.zeros_like(r)

            for r, v in zip(a_refs, outs[nro:]):
                r[...] += v

    in_specs, args = [], []
    for arr, w, cb, kind in row_ins:
        if kind == "row":
            in_specs.append(pl.BlockSpec((bm, w), lambda i, cb=cb: (i, cb)))
        elif kind == "prev":
            in_specs.append(pl.BlockSpec((8, w), lambda i, cb=cb: (jnp.maximum(i * sub - 1, 0), cb)))
        else:
            in_specs.append(pl.BlockSpec((8, w), lambda i, cb=cb: (jnp.minimum((i + 1) * sub, nrows // 8 - 1), cb)))
        args.append(arr)
    for c in consts:
        in_specs.append(pl.BlockSpec(c.shape, lambda i: (0, 0)))
        args.append(c)
    out_specs = [pl.BlockSpec((bm, w), lambda i: (i, 0)) for w, _ in row_outs]
    out_specs += [pl.BlockSpec(s, lambda i: (0, 0)) for s in acc_outs]
    out_shape = [jax.ShapeDtypeStruct((nrows, w), dt) for w, dt in row_outs]
    out_shape += [jax.ShapeDtypeStruct(s, F32) for s in acc_outs]
    res = pl.pallas_call(
        body, name=name, grid=(nblk,), in_specs=in_specs, out_specs=out_specs, out_shape=out_shape,
        compiler_params=_params(),
    )(*args)
    return res


def _rms_f(x, g, n):
    r = lax.rsqrt(jnp.sum(x * x, axis=-1, keepdims=True) * (1.0 / n) + EPS)
    return x * r * g


def _rms_b(x, g, dy, n):
    r = lax.rsqrt(jnp.sum(x * x, axis=-1, keepdims=True) * (1.0 / n) + EPS)
    gx = dy * g
    dx = r * gx - x * (r * r * r * (jnp.sum(x * gx, axis=-1, keepdims=True) * (1.0 / n)))
    dg = jnp.sum(dy * (x * r), axis=0, keepdims=True)
    return dx, dg


def _rope_f(x, c, sa, sb):
    return x * c + pltpu.roll(x, LANES - 16, 1) * sa + pltpu.roll(x, 16, 1) * sb


def _rope_b(g, c, sa, sb):
    return g * c + pltpu.roll(g * sa, 16, 1) + pltpu.roll(g * sb, LANES - 16, 1)


def _gelu(x):
    c = math.sqrt(2.0 / math.pi)
    return 0.5 * x * (1.0 + jnp.tanh(c * (x + 0.044715 * (x * x * x))))


def _gelu_grad(x):
    c = math.sqrt(2.0 / math.pi)
    th = jnp.tanh(c * (x + 0.044715 * (x * x * x)))
    return 0.5 * (1.0 + th) + 0.5 * x * (1.0 - th * th) * (c * (1.0 + 3.0 * 0.044715 * (x * x)))


def _row_ids(bm):
    return lax.broadcasted_iota(jnp.int32, (bm, 1), 0)


def _shift_down(x, halo_ref, i, k):
    live = (i > 0).astype(F32)
    out = pltpu.roll(x, k, 0)
    row = _row_ids(8)
    first = out[:8]
    for r in range(k):
        e = (row == r).astype(F32)
        first = first * (1.0 - e) + e * (halo_ref[8 - k + r:8 - k + r + 1, :] * live)
    return jnp.concatenate([first, out[8:]], axis=0)


def _live_pairs(nq, nk, bq, bk, causal, key_major):
    pairs = [(i, j) for i in range(nq) for j in range(nk) if not causal or j * bk <= i * bq + bq - 1]
    if key_major:
        pairs.sort(key=lambda ij: (ij[1], ij[0]))
    return (jnp.asarray([p[0] for p in pairs], jnp.int32), jnp.asarray([p[1] for p in pairs], jnp.int32))


def _attn_fwd(name, qa, ka, vta, *, qoff, koff, voff, heads, causal, scale, bq, bk, side=None):
    lq, lk = qa.shape[0], ka.shape[0]
    bq, bk = min(bq, lq), min(bk, lk)
    nq, nk = lq // bq, lk // bk
    c2 = scale * LOG2E
    tab_i, tab_j = _live_pairs(nq, nk, bq, bk, causal, key_major=False)

    def body(ti, tj, q_ref, k_ref, vt_ref, o_ref, lse_ref, m_s, l_s, acc_s):
        t = pl.program_id(1)
        i, j = ti[t], tj[t]
        j_last = jnp.minimum(nk - 1, (i * bq + bq - 1) // bk) if causal else nk - 1

        @pl.when(j == 0)
        def _():
            m_s[...] = jnp.full_like(m_s, -1e30)
            l_s[...] = jnp.zeros_like(l_s)
            acc_s[...] = jnp.zeros_like(acc_s)

        def step(masked):
            st = _dot(k_ref[...], q_ref[...], _NT) * c2
            if masked:
                key = j * bk + lax.broadcasted_iota(jnp.int32, (bk, bq), 0)
                qry = i * bq + lax.broadcasted_iota(jnp.int32, (bk, bq), 1)
                st = jnp.where(key <= qry, st, -1e30)
            m_prev = m_s[...]
            m_new = jnp.maximum(m_prev, jnp.max(st, axis=0, keepdims=True))
            alpha = jnp.exp2(m_prev - m_new)
            pt = jnp.exp2(st - m_new)
            l_s[...] = alpha * l_s[...] + jnp.sum(pt, axis=0, keepdims=True)
            acc_s[...] = alpha * acc_s[...] + _dot(vt_ref[...], pt, _NN)
            m_s[...] = m_new

        if causal:
            full = j * bk + bk - 1 <= i * bq
            pl.when(full)(lambda: step(False))
            pl.when(jnp.logical_not(full))(lambda: step(True))
        else:
            step(False)

        @pl.when(j == j_last)
        def _():
            l = l_s[...]
            o_ref[...] = (acc_s[...] / l).T.astype(o_ref.dtype)
            lse_ref[0] = m_s[...] + jnp.log2(l)

    in_specs = [pl.BlockSpec((bq, LANES), lambda h, t, ti, tj: (ti[t], qoff + h)),
                pl.BlockSpec((bk, LANES), lambda h, t, ti, tj: (tj[t], koff + h)),
                pl.BlockSpec((LANES, bk), lambda h, t, ti, tj: (voff + h, tj[t]))]
    out_specs = [pl.BlockSpec((bq, LANES), lambda h, t, ti, tj: (ti[t], h)),
                 pl.BlockSpec((1, 1, bq), lambda h, t, ti, tj: (h, 0, ti[t]))]
    scratch = [pltpu.VMEM((1, bq), F32), pltpu.VMEM((1, bq), F32), pltpu.VMEM((LANES, bq), F32)]
    out_shape = [jax.ShapeDtypeStruct((lq, heads * LANES), BF16), jax.ShapeDtypeStruct((heads, 1, lq), F32)]
    (o, lse), rode = _pair_grid_call(body, name, heads, (tab_i, tab_j), [qa, ka, vta], in_specs, out_specs, out_shape,
                                     scratch, side)
    return o, lse, rode


_Side = collections.namedtuple("_Side", "srcs out_shapes nsem phases")


def _pair_grid_call(body, name, heads, tabs, ins, in_specs, out_specs, out_shape, scratch, side):
    npairs = int(tabs[0].shape[0])
    n_in, n_out = len(ins), len(out_shape)
    n = len(side.srcs) if side else 0

    def wrapped(*refs):
        pre = len(tabs)
        if not side:
            return body(*refs)
        s_refs = refs[pre + n_in:pre + n_in + n]
        o_refs = refs[pre + n_in + n + n_out:pre + n_in + 2 * n + n_out]
        send, recv, loc = refs[-3:]
        args = (s_refs, o_refs, (send, recv), loc)
        h, t = pl.program_id(0), pl.program_id(1)
        pl.when(jnp.logical_and(h == 0, t == 0))(lambda: side.phases[0](*args))
        if len(side.phases) == 3:
            pl.when(jnp.logical_and(h == heads // 2, t == 0))(lambda: side.phases[1](*args))
        body(*refs[:pre + n_in], *refs[pre + n_in + n:pre + n_in + n + n_out], *refs[pre + n_in + 2 * n + n_out:-3])
        pl.when(jnp.logical_and(h == heads - 1, t == npairs - 1))(lambda: side.phases[-1](*args))

    hbm = pl.BlockSpec(memory_space=pltpu.HBM)
    sems = [pltpu.SemaphoreType.DMA((side.nsem * n,)), pltpu.SemaphoreType.DMA((side.nsem * n,)),
            pltpu.SemaphoreType.DMA((n,))] if side else []
    grid_spec = pltpu.PrefetchScalarGridSpec(
        num_scalar_prefetch=len(tabs), grid=(heads, npairs), in_specs=in_specs + [hbm] * n,
        out_specs=out_specs + [hbm] * n, scratch_shapes=scratch + sems)
    res = pl.pallas_call(
        wrapped, name=name, grid_spec=grid_spec, out_shape=out_shape + (list(side.out_shapes) if side else []),
        compiler_params=_params(has_side_effects=True) if side else _params(),
    )(*tabs, *ins, *(side.srcs if side else []))
    return res[:n_out], res[n_out:]


def _attn_delta(name, oa, doa, *, heads, bq):
    lq = oa.shape[0]
    bq = min(bq, lq)

    def body(o_ref, do_ref, d_ref):
        prod = o_ref[...].astype(F32) * do_ref[...].astype(F32)
        hi = prod.astype(BF16)
        lo = (prod - hi.astype(F32)).astype(BF16)
        pick = (lax.broadcasted_iota(jnp.int32, (8, LANES), 0) == 0).astype(BF16)
        sums = _dot(pick, hi, _NT) + _dot(pick, lo, _NT)
        d_ref[0] = jnp.sum(sums, axis=0, keepdims=True)

    blk = pl.BlockSpec((bq, LANES), lambda h, i: (i, h))
    return pl.pallas_call(
        body, name=name, grid=(heads, lq // bq), in_specs=[blk, blk],
        out_specs=pl.BlockSpec((1, 1, bq), lambda h, i: (h, 0, i)),
        out_shape=jax.ShapeDtypeStruct((heads, 1, lq), F32), compiler_params=_params(),
    )(oa, doa)


def _attn_bwd(name, qa, ka, kta, va, doa, lsea, deltaa, *, qoff, koff, voff, heads, causal, scale, bq, bk, side=None):
    lq, lk = qa.shape[0], ka.shape[0]
    bq, bk = min(bq, lq), min(bk, lk)
    nq, nk = lq // bq, lk // bk
    c2 = scale * LOG2E
    tab_i, tab_j = _live_pairs(nq, nk, bq, bk, causal, key_major=True)

    def body(ti, tj, q_ref, k_ref, kt_ref, v_ref, do_ref, lse_ref, delta_ref, dqt_ref, dk_ref, dv_ref):
        t = pl.program_id(1)
        i, j = ti[t], tj[t]
        i_first = (j * bk) // bq if causal else 0

        @pl.when(t == 0)
        def _():
            dqt_ref[...] = jnp.zeros_like(dqt_ref)

        @pl.when(i == i_first)
        def _():
            dk_ref[...] = jnp.zeros_like(dk_ref)
            dv_ref[...] = jnp.zeros_like(dv_ref)

        def step(masked):
            q, k, v, do = q_ref[...], k_ref[...], v_ref[...], do_ref[...]
            st = _dot(k, q, _NT) * c2
            if masked:
                key = j * bk + lax.broadcasted_iota(jnp.int32, (bk, bq), 0)
                qry = i * bq + lax.broadcasted_iota(jnp.int32, (bk, bq), 1)
                st = jnp.where(key <= qry, st, -1e30)
            pt = jnp.exp2(st - lse_ref[0])
            dv_ref[...] += _dot(pt, do, _NN)
            dpt = _dot(v, do, _NT)
            dst = (pt * (dpt - delta_ref[0]) * scale).astype(BF16)
            dk_ref[...] += _dot(dst, q, _NN)
            dqt_ref[0, i] += _dot(kt_ref[...], dst, _NN)

        if causal:
            full = j * bk + bk - 1 <= i * bq
            pl.when(full)(lambda: step(False))
            pl.when(jnp.logical_not(full))(lambda: step(True))
        else:
            step(False)

    q_spec = lambda off: pl.BlockSpec((bq, LANES), lambda h, t, ti, tj: (ti[t], off + h))
    kv_spec = lambda off: pl.BlockSpec((bk, LANES), lambda h, t, ti, tj: (tj[t], off + h))
    row_spec = pl.BlockSpec((1, 1, bq), lambda h, t, ti, tj: (h, 0, ti[t]))
    in_specs = [q_spec(qoff), kv_spec(koff), pl.BlockSpec((LANES, bk), lambda h, t, ti, tj: (koff + h, tj[t])),
                kv_spec(voff), q_spec(0), row_spec, row_spec]
    out_specs = [pl.BlockSpec((1, nq, LANES, bq), lambda h, t, ti, tj: (h, 0, 0, 0)),
                 pl.BlockSpec((bk, LANES), lambda h, t, ti, tj: (tj[t], h)),
                 pl.BlockSpec((bk, LANES), lambda h, t, ti, tj: (tj[t], h))]
    out_shape = [jax.ShapeDtypeStruct((heads, nq, LANES, bq), F32), jax.ShapeDtypeStruct((lk, heads * LANES), F32),
                 jax.ShapeDtypeStruct((lk, heads * LANES), F32)]
    (dqt, dk, dv), rode = _pair_grid_call(body, name, heads, (tab_i, tab_j), [qa, ka, kta, va, doa, lsea, deltaa],
                                          in_specs, out_specs, out_shape, [], side)
    return dqt.transpose(1, 3, 0, 2).reshape(lq, heads * LANES), dk, dv, rode


S5_STRIP = SSM_N // 4
S5_CH = SSM_WIDTH // 4
S5_CHUNK = 1024


def _scan_chunk(xr_ref, xi_ref, ar, ai, cre_s, cim_s, reverse, unroll=4):
    t = xr_ref.shape[0]
    ng = t // 8
    edge = 0 if reverse else 7
    row8 = lax.broadcasted_iota(jnp.int32, (8, 1), 0)

    def grp(g):
        return pl.ds(pl.multiple_of(g * 8, 8), 8)

    steps, pr, pi = [], ar, ai
    for d in (1, 2, 4):
        own = ((row8 < 8 - d) if reverse else (row8 >= d)).astype(F32)
        steps.append(((8 - d) if reverse else d, pr * own, pi * own))
        pr, pi = pr * pr - pi * pi, 2.0 * pr * pi

    def group_scan(xr, xi):
        for shift, mr, mi in steps:
            sr, si = pltpu.roll(xr, shift, 0), pltpu.roll(xi, shift, 0)
            xr, xi = xr + mr * sr - mi * si, xi + mr * si + mi * sr
        return xr, xi

    def local(g, _):
        xr, xi = group_scan(xr_ref[grp(g), :], xi_ref[grp(g), :])
        xr_ref[grp(g), :] = xr
        xi_ref[grp(g), :] = xi
        return 0

    lax.fori_loop(0, ng, local, 0, unroll=unroll)
    e = (row8 == 7 - edge).astype(F32)
    pw_r, pw_i = group_scan(e * ar, e * ai)
    e_out = (row8 == edge).astype(F32)
    a8_r = jnp.sum(pw_r * e_out, axis=0, keepdims=True)
    a8_i = jnp.sum(pw_i * e_out, axis=0, keepdims=True)
    cr, ci = cre_s[...], cim_s[...]
    for i in range(ng):
        g = ng - 1 - i if reverse else i
        rows = slice(g * 8, g * 8 + 8)
        lr, li = xr_ref[g * 8 + edge:g * 8 + edge + 1, :], xi_ref[g * 8 + edge:g * 8 + edge + 1, :]
        xr_ref[rows, :] = xr_ref[rows, :] + pw_r * cr - pw_i * ci
        xi_ref[rows, :] = xi_ref[rows, :] + pw_r * ci + pw_i * cr
        cr, ci = lr + a8_r * cr - a8_i * ci, li + a8_r * ci + a8_i * cr
    cre_s[...] = cr
    cim_s[...] = ci


def _s5_fwd(name, u, bb_re, bb_im, cc_re, cc_imn, a_re, a_im):
    l = u.shape[0]
    t = min(S5_CHUNK, l)

    def body(u_ref, bbr_ref, bbi_ref, ccr_ref, cci_ref, ar_ref, ai_ref, sre_ref, sim_ref, y_ref, cre_s, cim_s):
        @pl.when(pl.program_id(1) == 0)
        def _():
            cre_s[...] = jnp.zeros_like(cre_s)
            cim_s[...] = jnp.zeros_like(cim_s)

        uv = u_ref[...]
        sre_ref[...] = _dot(uv, bbr_ref[...], _NN)
        sim_ref[...] = _dot(uv, bbi_ref[...], _NN)
        _scan_chunk(sre_ref, sim_ref, ar_ref[...], ai_ref[...], cre_s, cim_s, reverse=False)
        y_ref[...] = _dot(sre_ref[...], ccr_ref[...], _NN) + _dot(sim_ref[...], cci_ref[...], _NN)

    rows_ch = pl.BlockSpec((t, S5_CH), lambda w, c: (c, w))
    rows_st = pl.BlockSpec((t, S5_STRIP), lambda w, c: (c, w))
    b_blk = pl.BlockSpec((S5_CH, S5_STRIP), lambda w, c: (w, w))
    c_blk = pl.BlockSpec((S5_STRIP, S5_CH), lambda w, c: (w, w))
    a_blk = pl.BlockSpec((1, S5_STRIP), lambda w, c: (0, w))
    return pl.pallas_call(
        body, name=name, grid=(SSM_N // S5_STRIP, l // t),
        in_specs=[rows_ch, b_blk, b_blk, c_blk, c_blk, a_blk, a_blk], out_specs=[rows_st, rows_st, rows_ch],
        out_shape=[jax.ShapeDtypeStruct((l, SSM_N), F32), jax.ShapeDtypeStruct((l, SSM_N), F32),
                   jax.ShapeDtypeStruct((l, SSM_WIDTH), F32)],
        scratch_shapes=[pltpu.VMEM((1, S5_STRIP), F32), pltpu.VMEM((1, S5_STRIP), F32)],
        compiler_params=_params(),
    )(u, bb_re, bb_im, cc_re, cc_imn, a_re, a_im)


def _s5_bwd(name, dy, du_skip, u, s_re, s_im, bb_re, bb_im, cc_re, cc_imn, a_re, a_im_neg):
    l = u.shape[0]
    t = min(S5_CHUNK, l)
    nc = l // t

    def body(dy_ref, skip_ref, u_ref, sre_ref, sim_ref, hre_ref, him_ref, bbr_ref, bbi_ref, ccr_ref, cci_ref, ar_ref,
             ai_ref, du_ref, dar_ref, dai_ref, dbr_ref, dbi_ref, dcr_ref, dci_ref, lr_s, li_s, cre_s, cim_s):
        c = pl.program_id(1)

        @pl.when(c == 0)
        def _():
            for r in (cre_s, cim_s, dar_ref, dai_ref, dbr_ref, dbi_ref, dcr_ref, dci_ref):
                r[...] = jnp.zeros_like(r)

        dyv, uv = dy_ref[...], u_ref[...]
        lr_s[...] = _dot(dyv, ccr_ref[...], _NT)
        li_s[...] = _dot(dyv, cci_ref[...], _NT)
        _scan_chunk(lr_s, li_s, ar_ref[...], ai_ref[...], cre_s, cim_s, reverse=True)
        lam_r, lam_i = lr_s[...], li_s[...]
        s_r, s_i = sre_ref[...], sim_ref[...]
        tc = nc - 1 - c
        sp_r, sp_i = _shift_down(s_r, hre_ref, tc, 1), _shift_down(s_i, him_ref, tc, 1)
        dar_ref[...] += jnp.sum(lam_r * sp_r + lam_i * sp_i, axis=0, keepdims=True)
        dai_ref[...] += jnp.sum(lam_i * sp_r - lam_r * sp_i, axis=0, keepdims=True)
        dbr_ref[0] += _dot(uv, lam_r, _TN)
        dbi_ref[0] += _dot(uv, lam_i, _TN)
        dcr_ref[0] += _dot(s_r, dyv, _TN)
        dci_ref[0] += _dot(s_i, dyv, _TN)
        du = _dot(lam_r, bbr_ref[...], _NT) + _dot(lam_i, bbi_ref[...], _NT) + skip_ref[...]
        du_ref[...] = du.astype(du_ref.dtype)

    rows_ch = pl.BlockSpec((t, S5_CH), lambda w, c: (nc - 1 - c, w))
    rows_st = pl.BlockSpec((t, S5_STRIP), lambda w, c: (nc - 1 - c, w))
    halo = pl.BlockSpec((8, S5_STRIP), lambda w, c: (jnp.maximum((nc - 1 - c) * (t // 8) - 1, 0), w))
    b_blk = pl.BlockSpec((S5_CH, S5_STRIP), lambda w, c: (w, w))
    c_blk = pl.BlockSpec((S5_STRIP, S5_CH), lambda w, c: (w, w))
    a_blk = pl.BlockSpec((1, S5_STRIP), lambda w, c: (0, w))
    nw = SSM_N // S5_STRIP
    return pl.pallas_call(
        body, name=name, grid=(nw, nc),
        in_specs=[rows_ch, rows_ch, rows_ch, rows_st, rows_st, halo, halo, b_blk, b_blk, c_blk, c_blk, a_blk, a_blk],
        out_specs=[rows_ch, a_blk, a_blk, pl.BlockSpec((1, S5_CH, S5_STRIP), lambda w, c: (w, 0, 0)),
                   pl.BlockSpec((1, S5_CH, S5_STRIP), lambda w, c: (w, 0, 0)),
                   pl.BlockSpec((1, S5_STRIP, S5_CH), lambda w, c: (w, 0, 0)),
                   pl.BlockSpec((1, S5_STRIP, S5_CH), lambda w, c: (w, 0, 0))],
        out_shape=[jax.ShapeDtypeStruct((l, SSM_WIDTH), BF16), jax.ShapeDtypeStruct((1, SSM_N), F32),
                   jax.ShapeDtypeStruct((1, SSM_N), F32), jax.ShapeDtypeStruct((nw, S5_CH, S5_STRIP), F32),
                   jax.ShapeDtypeStruct((nw, S5_CH, S5_STRIP), F32), jax.ShapeDtypeStruct((nw, S5_STRIP, S5_CH), F32),
                   jax.ShapeDtypeStruct((nw, S5_STRIP, S5_CH), F32)],
        scratch_shapes=[pltpu.VMEM((t, S5_STRIP), F32), pltpu.VMEM((t, S5_STRIP), F32),
                        pltpu.VMEM((1, S5_STRIP), F32), pltpu.VMEM((1, S5_STRIP), F32)],
        compiler_params=_params(),
    )(dy, du_skip, u, s_re, s_im, s_re, s_im, bb_re, bb_im, cc_re, cc_imn, a_re, a_im_neg)


def _disc_math(lr, li, ldt, br, bi):
    dt = jnp.exp(ldt)
    mag = jnp.exp(lr * dt)
    a_re, a_im = mag * jnp.cos(li * dt), mag * jnp.sin(li * dt)
    den = lr * lr + li * li
    e_re, e_im = a_re - 1.0, a_im
    f_re = (e_re * lr + e_im * li) / den
    f_im = (e_im * lr - e_re * li) / den
    return a_re, a_im, f_re * br - f_im * bi, f_re * bi + f_im * br


def _disc_fwd(lr, li, ldt, br, bi):
    def body(lr_ref, li_ref, ldt_ref, br_ref, bi_ref, are_ref, aim_ref, bbr_ref, bbi_ref):
        a_re, a_im, bb_re, bb_im = _disc_math(lr_ref[...], li_ref[...], ldt_ref[...], br_ref[...], bi_ref[...])
        are_ref[...] = a_re
        aim_ref[...] = a_im
        bbr_ref[...] = bb_re
        bbi_ref[...] = bb_im

    col = jax.ShapeDtypeStruct(lr.shape, F32)
    mat = jax.ShapeDtypeStruct(br.shape, F32)
    return pl.pallas_call(body, name="s5_disc_fwd", out_shape=[col, col, mat, mat], compiler_params=_params())(
        lr, li, ldt, br, bi)


def _disc_bwd(lr, li, ldt, br, bi, da_re, da_im, dbb_re, dbb_im):
    def body(lr_ref, li_ref, ldt_ref, br_ref, bi_ref, g0, g1, g2, g3, o0, o1, o2, o3, o4):
        _, vjp = jax.vjp(_disc_math, lr_ref[...], li_ref[...], ldt_ref[...], br_ref[...], bi_ref[...])
        grads = vjp((g0[...], g1[...], g2[...], g3[...]))
        for o, g in zip((o0, o1, o2, o3, o4), grads):
            o[...] = g

    col = jax.ShapeDtypeStruct(lr.shape, F32)
    mat = jax.ShapeDtypeStruct(br.shape, F32)
    return pl.pallas_call(body, name="s5_disc_bwd", out_shape=[col, col, col, mat, mat], compiler_params=_params())(
        lr, li, ldt, br, bi, da_re, da_im, dbb_re, dbb_im)


N_CHIP = 4


def _place():
    x, y, c = lax.axis_index("x"), lax.axis_index("y"), lax.axis_index("c")
    return (x, y, c), (x, y, 1 - c), [(1 - x, y), (x, 1 - y), (1 - x, 1 - y)]


def _lin(px, py, pc):
    return 4 * px + 2 * py + pc


def _remote(src, dst, sems, k, dev):
    return pltpu.make_async_remote_copy(src_ref=src, dst_ref=dst, send_sem=sems[0].at[k], recv_sem=sems[1].at[k],
                                        device_id=dev, device_id_type=pl.DeviceIdType.MESH)


def _hbm_call(body, name, srcs, out_shapes, nsem):
    n = len(srcs)
    hbm = pl.BlockSpec(memory_space=pltpu.HBM)

    def wrapped(*refs):
        body(refs[:n], refs[n:2 * n], (refs[2 * n], refs[2 * n + 1]), refs[2 * n + 2])

    return pl.pallas_call(
        wrapped, name=name, in_specs=[hbm] * n, out_specs=[hbm] * n, out_shape=out_shapes,
        scratch_shapes=[pltpu.SemaphoreType.DMA((nsem * n,)), pltpu.SemaphoreType.DMA((nsem * n,)),
                        pltpu.SemaphoreType.DMA((n,))],
        compiler_params=pltpu.CompilerParams(has_side_effects=True),
    )(*srcs)


GATHER_SEMS = 7
CHIP_SEMS = 3


def _gather_copies(s_refs, o_refs, sems, loc_sems):
    me, sib, chips = _place()
    c = me[2]
    out = []
    for g, (s_ref, o_ref) in enumerate(zip(s_refs, o_refs)):
        slot = lambda dev, o_ref=o_ref: o_ref.at[_lin(*dev)]
        k0 = GATHER_SEMS * g
        mine = pltpu.make_async_copy(s_ref, slot(me), loc_sems.at[g])
        first = [_remote(s_ref, slot(me), sems, k0, sib)]
        first += [_remote(s_ref, slot(me), sems, k0 + 1 + j, (*chip, c)) for j, chip in enumerate(chips)]
        passed = [_remote(slot((*chip, c)), slot((*chip, c)), sems, k0 + 4 + j, sib) for j, chip in enumerate(chips)]
        arrive = [_remote(s_ref, slot(sib), sems, k0, me)]
        arrive += [_remote(s_ref, slot((*chip, c)), sems, k0 + 1 + j, me) for j, chip in enumerate(chips)]
        arrive += [_remote(s_ref, slot((*chip, 1 - c)), sems, k0 + 4 + j, me) for j, chip in enumerate(chips)]
        out.append((mine, first, passed, arrive))
    return out


def _gather_start(*refs):
    for mine, first, _, _ in _gather_copies(*refs):
        mine.start()
        for cp in first:
            cp.start()


def _gather_relay(*refs):
    for _, _, passed, arrive in _gather_copies(*refs):
        for j, cp in enumerate(passed):
            arrive[1 + j].wait_recv()
            cp.start()


def _gather_finish(*refs):
    for mine, first, passed, arrive in _gather_copies(*refs):
        arrive[0].wait_recv()
        for cp in arrive[4:]:
            cp.wait_recv()
        for cp in first + passed:
            cp.wait_send()
        mine.wait()


def _gather_shapes(srcs):
    return [jax.ShapeDtypeStruct((N_DEV,) + s.shape, s.dtype) for s in srcs]


def _gather_all(name, srcs):
    def body(*refs):
        _gather_start(*refs)
        _gather_relay(*refs)
        _gather_finish(*refs)

    return _hbm_call(body, name, srcs, _gather_shapes(srcs), GATHER_SEMS)


def _pair_exchange(name, sends):
    def body(s_refs, o_refs, sems, loc_sems):
        me, sib, _ = _place()
        c = me[2]
        copies = [_remote(s_ref.at[2 * q + (1 - c)], o_ref.at[q], sems, N_CHIP * g + q, sib)
                  for g, (s_ref, o_ref) in enumerate(zip(s_refs, o_refs)) for q in range(N_CHIP)]
        for cp in copies:
            cp.start()
        for cp in copies:
            cp.wait()

    return _hbm_call(body, name, sends, [jax.ShapeDtypeStruct((N_CHIP,) + s.shape[1:], s.dtype) for s in sends],
                     N_CHIP)


def _row_block(r, cap):
    best = 8
    for m in range(8, min(r, cap) + 1, 8):
        if r % m == 0:
            best = m
    return best


def _pair_sum(name, send, got):
    _, r, cols = send.shape
    bm = _row_block(r, max(8, 256 * 1024 // cols))
    core = lax.axis_index("c").astype(jnp.int32).reshape(1)

    def body(core_ref, a_ref, b_ref, o_ref):
        o_ref[...] = a_ref[...] + b_ref[...]

    grid_spec = pltpu.PrefetchScalarGridSpec(
        num_scalar_prefetch=1, grid=(N_CHIP, r // bm),
        in_specs=[pl.BlockSpec((1, bm, cols), lambda q, i, cr: (2 * q + cr[0], i, 0)),
                  pl.BlockSpec((1, bm, cols), lambda q, i, cr: (q, i, 0))],
        out_specs=pl.BlockSpec((1, bm, cols), lambda q, i, cr: (q, i, 0)))
    return pl.pallas_call(body, name=name, grid_spec=grid_spec,
                          out_shape=jax.ShapeDtypeStruct((N_CHIP, r, cols), F32), compiler_params=_params())(
        core, send, got)


def _chip_exchange(name, parts):
    def body(*refs):
        _chip_start(*refs)
        _chip_finish(*refs)

    return _hbm_call(body, name, parts, _chip_shapes(parts), CHIP_SEMS)


def _chip_copies(p_refs, o_refs, sems, loc_sems):
    me, _, chips = _place()
    c = me[2]
    chip_id = lambda chip: 2 * chip[0] + chip[1]
    my_chip = chip_id(me)
    out = []
    for g, (p_ref, o_ref) in enumerate(zip(p_refs, o_refs)):
        mine = pltpu.make_async_copy(p_ref.at[my_chip], o_ref.at[my_chip], loc_sems.at[g])
        send = [_remote(p_ref.at[chip_id(chip)], o_ref.at[my_chip], sems, CHIP_SEMS * g + j, (*chip, c))
                for j, chip in enumerate(chips)]
        arrive = [_remote(p_ref.at[my_chip], o_ref.at[chip_id(chip)], sems, CHIP_SEMS * g + j, me)
                  for j, chip in enumerate(chips)]
        out.append((mine, send, arrive))
    return out


def _chip_start(*refs):
    for mine, send, _ in _chip_copies(*refs):
        mine.start()
        for cp in send:
            cp.start()


def _chip_finish(*refs):
    for mine, send, arrive in _chip_copies(*refs):
        for cp in arrive:
            cp.wait_recv()
        for cp in send:
            cp.wait_send()
        mine.wait()


def _chip_shapes(parts):
    return [jax.ShapeDtypeStruct(p.shape, p.dtype) for p in parts]


def _adamw(name, rcv, w, m, v):
    r, c = w.shape
    nslot = rcv.shape[0]
    bm = _row_block(r, max(8, 256 * 1024 // c))

    def body(rcv_ref, w_ref, m_ref, v_ref, g_ref, d_ref, m2_ref, v2_ref):
        g = rcv_ref[0]
        for s in range(1, nslot):
            g = g + rcv_ref[s]
        m2 = ADAM_B1 * m_ref[...] + (1.0 - ADAM_B1) * g
        v2 = ADAM_B2 * v_ref[...] + (1.0 - ADAM_B2) * (g * g)
        m_hat = m2 / (1.0 - ADAM_B1 ** ADAM_STEP)
        v_hat = v2 / (1.0 - ADAM_B2 ** ADAM_STEP)
        g_ref[...] = g
        d_ref[...] = -ADAM_LR * (m_hat / (jnp.sqrt(v_hat) + ADAM_EPS) + ADAM_WD * w_ref[...])
        m2_ref[...] = m2
        v2_ref[...] = v2

    blk = pl.BlockSpec((bm, c), lambda i: (i, 0))
    out = jax.ShapeDtypeStruct((r, c), F32)
    return pl.pallas_call(
        body, name=name, grid=(r // bm,),
        in_specs=[pl.BlockSpec((nslot, bm, c), lambda i: (0, i, 0)), blk, blk, blk], out_specs=[blk] * 4,
        out_shape=[out] * 4, compiler_params=_params(),
    )(rcv, w, m, v)


IN_SHARD = IN_WIDTH // N_DEV
UP_SHARD = 2 * D_FF // N_DEV
GROUPS = (
    ("g128a", LANES, (("w_q_b", Q_LORA, D_QK), ("w_kv_b", KV_LORA, D_NOPE + D_V))),
    ("g128b", LANES, (("w_o_mla", MLA_HEADS * D_V, LANES), ("w_o_ssm", SSM_WIDTH, LANES),
                      ("w_o_cross", X_WIDTH, LANES))),
    ("g512", SSM_WIDTH, (("w_glu", SSM_WIDTH // N_DEV, SSM_WIDTH),)),
    ("g1024", D_MODEL, (("w_mem_kv", D_MODEL // N_DEV, D_MODEL), ("w_out", D_MODEL // N_DEV, D_MODEL),
                        ("w_down", D_FF // N_DEV, D_MODEL))),
    ("g640", 640, (("w_in", D_MODEL, IN_SHARD),)),
    ("g768", 768, (("w_up", D_MODEL, UP_SHARD),)),
    ("gconv", 768, (("conv_w", 3, UP_SHARD),)),
)
EARLY = ("g640", "g128a")
EARLY_GROUPS = tuple(g for g in GROUPS if g[0] in EARLY)
LATE_GROUPS = tuple(g for g in GROUPS if g[0] not in EARLY)
REP_W = 1024


def _rows8(a):
    return -(-a // 8) * 8


def _group_rows(params):
    off, r = {}, 0
    for n, a, _ in params:
        off[n] = r
        r += _rows8(a)
    return off


def _group_local(width, params, vals, li, dtype):
    return jnp.concatenate([jnp.pad(vals[n][li].astype(dtype), ((0, _rows8(a) - a), (0, width - b)))
                            for n, a, b in params], axis=0)


def _rep_rows(flat):
    n = flat.shape[-1]
    per = REP_W * 64
    tot = -(-n // per) * per
    flat = jnp.pad(flat, [(0, 0)] * (flat.ndim - 1) + [(0, tot - n)])
    return flat.reshape(flat.shape[:-1] + (tot // REP_W, REP_W))


IN_SEGS = (
    (0, Q_LORA + KV_LORA, "wqkv", 0),
    (Q_LORA + KV_LORA, Q_LORA + KV_LORA + D_ROPE, "wqkv", Q_LORA + KV_LORA + KR_LO),
    (Q_LORA + KV_LORA + D_ROPE, Q_LORA + KV_LORA + D_ROPE + SSM_WIDTH, "w_u", 0),
    (Q_LORA + KV_LORA + D_ROPE + SSM_WIDTH, Q_LORA + KV_LORA + D_ROPE + SSM_WIDTH + X_WIDTH, "w_xq", 0),
    (Q_LORA + KV_LORA + D_ROPE + SSM_WIDTH + X_WIDTH, IN_WIDTH, "w_g", 0),
)
IN_PARTS = (("wqkv", QKV_W), ("w_u", SSM_WIDTH), ("w_xq", X_WIDTH), ("w_g", 3 * D_MODEL))


def _in_pieces():
    out = []
    for d in range(N_DEV):
        for lo, hi, part, dst in IN_SEGS:
            s, e = max(lo, d * IN_SHARD), min(hi, (d + 1) * IN_SHARD)
            if s < e:
                out.append((d, s - d * IN_SHARD, e - s, [p for p, _ in IN_PARTS].index(part), dst + s - lo))
    return out


def _unpack_w_in(gathered, row0, bm=256):
    def body(x_ref, *o_refs):
        o_refs[0][...] = jnp.zeros_like(o_refs[0])
        for d, src, n, part, dst in _in_pieces():
            o_refs[part][:, dst:dst + n] = x_ref[d, :, src:src + n]

    return pl.pallas_call(
        body, name="unpack_w_in", grid=(D_MODEL // bm,),
        in_specs=[pl.BlockSpec((N_DEV, bm, gathered.shape[2]), lambda i: (0, row0 // bm + i, 0))],
        out_specs=[pl.BlockSpec((bm, w), lambda i: (i, 0)) for _, w in IN_PARTS],
        out_shape=[jax.ShapeDtypeStruct((D_MODEL, w), gathered.dtype) for _, w in IN_PARTS], compiler_params=_params(),
    )(gathered)


def _pack_w_in(parts, bm=256):
    def body(*refs):
        o_ref = refs[-1]
        o_ref[...] = jnp.zeros_like(o_ref)
        for d, src, n, part, dst in _in_pieces():
            o_ref[d, :, src:src + n] = refs[part][:, dst:dst + n]

    return pl.pallas_call(
        body, name="pack_w_in", grid=(D_MODEL // bm,),
        in_specs=[pl.BlockSpec((bm, w), lambda i: (i, 0)) for _, w in IN_PARTS],
        out_specs=pl.BlockSpec((N_DEV, bm, 640), lambda i: (0, i, 0)),
        out_shape=jax.ShapeDtypeStruct((N_DEV, D_MODEL, 640), F32), compiler_params=_params(),
    )(*parts)


def _unpack_w_up(gathered, row0, bm=256):
    def body(x_ref, o_ref):
        for d in range(N_DEV):
            o_ref[:, d * UP_SHARD:(d + 1) * UP_SHARD] = x_ref[d, :, :UP_SHARD]

    return pl.pallas_call(
        body, name="unpack_w_up", grid=(D_MODEL // bm,),
        in_specs=[pl.BlockSpec((N_DEV, bm, gathered.shape[2]), lambda i: (0, row0 // bm + i, 0))],
        out_specs=pl.BlockSpec((bm, 2 * D_FF), lambda i: (i, 0)),
        out_shape=jax.ShapeDtypeStruct((D_MODEL, 2 * D_FF), gathered.dtype), compiler_params=_params(),
    )(gathered)


def _pack_w_up(dw_g, dw_v, bm=256):
    half = N_DEV // 2

    def body(g_ref, v_ref, o_ref):
        o_ref[...] = jnp.zeros_like(o_ref)
        for d in range(N_DEV):
            src = g_ref if d < half else v_ref
            c0 = (d % half) * UP_SHARD
            o_ref[d, :, :UP_SHARD] = src[:, c0:c0 + UP_SHARD]

    blk = pl.BlockSpec((bm, D_FF), lambda i: (i, 0))
    return pl.pallas_call(
        body, name="pack_w_up", grid=(D_MODEL // bm,), in_specs=[blk, blk],
        out_specs=pl.BlockSpec((N_DEV, bm, 768), lambda i: (0, i, 0)),
        out_shape=jax.ShapeDtypeStruct((N_DEV, D_MODEL, 768), F32), compiler_params=_params(),
    )(dw_g, dw_v)


def _cols_to_rows(full):
    a, nb = full.shape
    return full.reshape(a, N_DEV, nb // N_DEV).transpose(1, 0, 2)


def _rows_to_cols(blocks):
    n, a, b = blocks.shape
    return blocks.transpose(1, 0, 2).reshape(a, n * b)


def _block_diag_in(bb):
    b3 = bb.reshape(SSM_GROUPS, SSM_STATE, SSM_GROUP_CH).transpose(0, 2, 1)
    eye = jnp.eye(SSM_GROUPS, dtype=bb.dtype)
    return (b3[:, :, None, :] * eye[:, None, :, None]).reshape(SSM_WIDTH, SSM_N)


def _block_diag_out(cc):
    c3 = cc.transpose(0, 2, 1)
    eye = jnp.eye(SSM_GROUPS, dtype=cc.dtype)
    return (c3[:, :, None, :] * eye[:, None, :, None]).reshape(SSM_N, SSM_WIDTH)


def _diag_blocks(mats, rows_per, cols_per):
    nw = mats.shape[0]
    per = SSM_GROUPS // nw
    m5 = mats.reshape(nw, per, rows_per, per, cols_per)
    eye = jnp.eye(per, dtype=mats.dtype)
    return jnp.sum(m5 * eye[None, :, None, :, None], axis=3).reshape(SSM_GROUPS, rows_per, cols_per)


def _blk(gath, offs, grp, n, a):
    r0 = offs[grp][n]
    return gath[grp][:, r0:r0 + a, :]


def _early_weights(gath, offs):
    o = dict(zip([p for p, _ in IN_PARTS], _unpack_w_in(gath["g640"], offs["g640"]["w_in"])))
    o["wqb"] = _rows_to_cols(_blk(gath, offs, "g128a", "w_q_b", Q_LORA))
    wkv = _rows_to_cols(_blk(gath, offs, "g128a", "w_kv_b", KV_LORA)).reshape(KV_LORA, MLA_HEADS, D_NOPE + D_V)
    o["wk"] = jnp.pad(wkv[:, :, :D_NOPE], ((0, 0), (0, 0), (0, LANES - D_NOPE))).reshape(KV_LORA, MLA_HEADS * LANES)
    o["wv"] = jnp.pad(wkv[:, :, D_NOPE:], ((0, 0), (0, 0), (0, LANES - D_V))).reshape(KV_LORA, MLA_HEADS * LANES)
    o["wv_t"] = o["wv"].T
    return o


def _late_weights(gath, offs):
    def blk(grp, n, a):
        return _blk(gath, offs, grp, n, a)

    o = {}
    wo = _rows_to_cols(blk("g128b", "w_o_mla", MLA_HEADS * D_V)).reshape(MLA_HEADS, D_V, D_MODEL)
    o["wo_mla"] = jnp.pad(wo, ((0, 0), (0, LANES - D_V), (0, 0))).reshape(MLA_HEADS * LANES, D_MODEL)
    o["w_o_ssm"] = _rows_to_cols(blk("g128b", "w_o_ssm", SSM_WIDTH))
    o["w_o_cross"] = _rows_to_cols(blk("g128b", "w_o_cross", X_WIDTH))
    o["w_glu"] = blk("g512", "w_glu", SSM_WIDTH // N_DEV).reshape(SSM_WIDTH, SSM_WIDTH)
    o["w_mem_kv"] = blk("g1024", "w_mem_kv", D_MODEL // N_DEV).reshape(D_MODEL, 2 * X_WIDTH)
    o["w_out"] = blk("g1024", "w_out", D_MODEL // N_DEV).reshape(D_MODEL, D_MODEL)
    o["w_down"] = blk("g1024", "w_down", D_FF // N_DEV).reshape(D_FF, D_MODEL)
    o["w_up"] = _unpack_w_up(gath["g768"], offs["g768"]["w_up"])
    o["w_up_g"], o["w_up_v"] = o["w_up"][:, :D_FF], o["w_up"][:, D_FF:]
    o["conv_w"] = _rows_to_cols(blk("gconv", "conv_w", 3)[:, :, :UP_SHARD])
    return o


def _row(v):
    return v.reshape(1, -1).astype(F32)


def _pad_lanes(v, n=LANES):
    return jnp.pad(v, (0, n - v.shape[0])).reshape(1, n).astype(F32)


def _layer_fwd(x, mem, tabs, w, p, side, late):
    l = x.shape[0]
    rope_c, rope_sa, rope_sb = tabs
    s = {"x": x}
    g_mix, g_qa, g_kva = _row(p["norm_mix_g"]), _row(p["q_a_norm_g"]), _row(p["kv_a_norm_g"])
    g_q, g_k = _pad_lanes(p["q_norm_g"]), _pad_lanes(p["k_norm_g"])

    (h,) = _rowwise("rms_mix", lambda i, r, c: (_rms_f(r[0][...], c[0][...], D_MODEL),), l, 1024,
                    [(x, D_MODEL, 0, "row")], [g_mix], [(D_MODEL, BF16)])
    pqkv = _mm("proj_qkv", [(h, w["wqkv"])])
    u = _mm("proj_u", [(h, w["w_u"])])
    xq = _mm("proj_xq", [(h, w["w_xq"])])
    gl = _mm("proj_gate", [(h, w["w_g"])], bm=1024, bn_cap=1024)
    s.update(h=h, pqkv=pqkv, u=u, xq=xq, gl=gl)

    def prep_a(i, r, c):
        return (_rms_f(r[0][:, :Q_LORA], c[0][...], Q_LORA),
                _rms_f(r[0][:, Q_LORA:Q_LORA + KV_LORA], c[1][...], KV_LORA))

    nq, nkv = _rowwise("mla_prep_a", prep_a, l, 1024, [(pqkv, QKV_W, 0, "row")], [g_qa, g_kva],
                       [(Q_LORA, BF16), (KV_LORA, BF16)])
    q_raw = _mm("mla_q_b", [(nq, w["wqb"])], bn_cap=1024)
    k_raw = _mm("mla_k_b", [(nkv, w["wk"])], bn_cap=1024)
    v_mla = _mm("mla_v_b", [(nkv, w["wv"])], out_dtype=BF16, bn_cap=1024)
    vt_mla = _mm("mla_vt_b", [(w["wv_t"], nkv)], trans_b=True, out_dtype=BF16, bn_cap=1024)

    def prep_b(i, r, c):
        q_ref, k_ref, kr_ref, c_ref, sa_ref, sb_ref = r
        rc, sa, sb, kr = c_ref[...], sa_ref[...], sb_ref[...], kr_ref[...]
        qs, ks = [], []
        for hd in range(MLA_HEADS):
            cols = slice(hd * LANES, (hd + 1) * LANES)
            qs.append(_rope_f(_rms_f(q_ref[:, cols], c[0][...], D_QK), rc, sa, sb))
            ks.append(_rope_f(_rms_f(k_ref[:, cols] + kr, c[1][...], D_QK), rc, sa, sb))
        return jnp.concatenate(qs, axis=1), jnp.concatenate(ks, axis=1)

    hw = MLA_HEADS * LANES
    kr_blk = (Q_LORA + KV_LORA) // LANES
    tab_ins = [(rope_c, LANES, 0, "row"), (rope_sa, LANES, 0, "row"), (rope_sb, LANES, 0, "row")]
    q, k = _rowwise("mla_prep_b", prep_b, l, 512,
                    [(q_raw, hw, 0, "row"), (k_raw, hw, 0, "row"), (pqkv, LANES, kr_blk, "row")] + tab_ins,
                    [g_q, g_k], [(hw, BF16), (hw, BF16)])
    o_a, lse_a, rode = _attn_fwd("mla_attn_fwd", q, k, vt_mla, qoff=0, koff=0, voff=0, heads=MLA_HEADS, causal=True,
                                 scale=D_QK ** -0.5, bq=1024, bk=1024, side=side)
    w_late, rode = late(rode)
    w = {**w, **w_late}
    ya = _mm("mla_o", [(o_a, w["wo_mla"])], bn_cap=1024)
    s.update(nq=nq, nkv=nkv, q_raw=q_raw, k_raw=k_raw, v_mla=v_mla, q=q, k=k, o_a=o_a, lse_a=lse_a, ya=ya)

    lr = p["ssm_lambda_re"].reshape(SSM_N, 1)
    li = p["ssm_lambda_im"].reshape(SSM_N, 1)
    ldt = jnp.repeat(p["ssm_log_dt"], SSM_STATE).reshape(SSM_N, 1)
    br = p["ssm_b_re"].reshape(SSM_N, SSM_GROUP_CH)
    bi = p["ssm_b_im"].reshape(SSM_N, SSM_GROUP_CH)
    a_re, a_im, bb_re, bb_im = _disc_fwd(lr, li, ldt, br, bi)
    bb_re_d, bb_im_d = _block_diag_in(bb_re).astype(BF16), _block_diag_in(bb_im).astype(BF16)
    cc_re_d = _block_diag_out(p["ssm_c_re"]).astype(BF16)
    cc_imn_d = _block_diag_out(-p["ssm_c_im"]).astype(BF16)
    a_re_row, a_im_row = a_re.reshape(1, SSM_N), a_im.reshape(1, SSM_N)
    d_row = _row(p["ssm_d"])
    b_glu = _row(p["b_glu"])
    s_re, s_im, ypre = _s5_fwd("s5_fwd", u, bb_re_d, bb_im_d, cc_re_d, cc_imn_d, a_re_row, a_im_row)

    def ssm_y(i, r, c):
        return (_gelu(r[0][...] + c[0][...] * r[1][...]),)

    (y_b,) = _rowwise("s5_gelu", ssm_y, l, 1024, [(ypre, SSM_WIDTH, 0, "row"), (u, SSM_WIDTH, 0, "row")], [d_row],
                      [(SSM_WIDTH, BF16)])
    z = _mm("s5_glu", [(y_b, w["w_glu"])])

    def ssm_out(i, r, c):
        y = _gelu(r[0][...] + c[0][...] * r[1][...])
        return (y * jax.nn.sigmoid(r[2][...] + c[1][...]),)

    (out_b,) = _rowwise("s5_glu_out", ssm_out, l, 1024,
                        [(ypre, SSM_WIDTH, 0, "row"), (u, SSM_WIDTH, 0, "row"), (z, SSM_WIDTH, 0, "row")],
                        [d_row, b_glu], [(SSM_WIDTH, BF16)])
    yb = _mm("s5_o", [(out_b, w["w_o_ssm"])], bn_cap=1024)
    s.update(disc=(lr, li, ldt, br, bi), a_rows=(a_re_row, a_im_row), bb_d=(bb_re_d, bb_im_d),
             cc_d=(cc_re_d, cc_imn_d), s_re=s_re, s_im=s_im, ypre=ypre, y_b=y_b, z=z, out_b=out_b, yb=yb)

    g_mem, g_xq, g_xk = _row(p["mem_norm_g"]), _row(p["xq_norm_g"]), _row(p["xk_norm_g"])
    ml = mem.shape[0]
    (memn,) = _rowwise("rms_mem", lambda i, r, c: (_rms_f(r[0][...], c[0][...], D_MODEL),), ml, 256,
                       [(mem, D_MODEL, 0, "row")], [g_mem], [(D_MODEL, BF16)])
    kvm = _mm("cross_kv", [(memn, w["w_mem_kv"])], bn_cap=1024)

    def headnorm(i, r, c):
        return (jnp.concatenate([_rms_f(r[0][:, hd * LANES:(hd + 1) * LANES], c[0][...], X_HEAD_DIM)
                                 for hd in range(X_HEADS)], axis=1),)

    (xk,) = _rowwise("cross_k_norm", headnorm, ml, 256, [(kvm, X_WIDTH, 0, "row")], [g_xk], [(X_WIDTH, BF16)])
    (xqn,) = _rowwise("cross_q_norm", headnorm, l, 512, [(xq, X_WIDTH, 0, "row")], [g_xq], [(X_WIDTH, BF16)])
    xvt = kvm[:, X_WIDTH:].T.astype(BF16)
    o_c, lse_c, _ = _attn_fwd("cross_attn_fwd", xqn, xk, xvt, qoff=0, koff=0, voff=0, heads=X_HEADS,
                              causal=False, scale=X_HEAD_DIM ** -0.5, bq=1024, bk=256)
    yc = _mm("cross_o", [(o_c, w["w_o_cross"])], bn_cap=1024)
    s.update(memn=memn, kvm=kvm, xk=xk, xqn=xqn, o_c=o_c, lse_c=lse_c, yc=yc)

    b_gate = _row(p["b_gate"])

    def merge(i, r, c):
        acc = None
        for br_ in range(3):
            g = jax.nn.sigmoid(r[br_][...] + c[0][:, br_ * D_MODEL:(br_ + 1) * D_MODEL])
            t = g * r[3 + br_][...]
            acc = t if acc is None else acc + t
        return (acc,)

    gate_ins = [(gl, D_MODEL, b_, "row") for b_ in range(3)]
    (merged,) = _rowwise("merge", merge, l, 512,
                         gate_ins + [(ya, D_MODEL, 0, "row"), (yb, D_MODEL, 0, "row"), (yc, D_MODEL, 0, "row")],
                         [b_gate], [(D_MODEL, BF16)])
    x1 = _mm("mix_out", [(merged, w["w_out"])], add=x, bn_cap=1024)
    s.update(merged=merged, x1=x1)

    g_ffn = _row(p["norm_ffn_g"])
    (h2,) = _rowwise("rms_ffn", lambda i, r, c: (_rms_f(r[0][...], c[0][...], D_MODEL),), l, 1024,
                     [(x1, D_MODEL, 0, "row")], [g_ffn], [(D_MODEL, BF16)])
    up = _mm("ffn_up", [(h2, w["w_up"])], bm=1024, bn_cap=1408)
    conv_w = w["conv_w"]
    conv_b = _row(p["conv_b"])

    def conv_glu(i, r, c):
        cg = _conv(r[0], r[2], i, c[0], c[1], 0)
        cv = _conv(r[1], r[3], i, c[0], c[1], D_FF)
        return (cg * jax.nn.sigmoid(cg) * cv,)

    up_ins = [(up, D_FF, 0, "row"), (up, D_FF, 1, "row"), (up, D_FF, 0, "prev"), (up, D_FF, 1, "prev")]
    (act,) = _rowwise("ffn_conv_glu", conv_glu, l, 256, up_ins, [conv_w, conv_b], [(D_FF, BF16)])
    x2 = _mm("ffn_down", [(act, w["w_down"])], add=x1, bm=1024, bn_cap=1024)
    s.update(h2=h2, up=up, act=act, conv_w=conv_w, conv_b=conv_b)
    return x2, s, w, rode


def _conv(x_ref, halo_ref, i, w_ref, b_ref, col0):
    x = x_ref[...]
    cols = slice(col0, col0 + D_FF)
    return (w_ref[0:1, cols] * _shift_down(x, halo_ref, i, 2) + w_ref[1:2, cols] * _shift_down(x, halo_ref, i, 1)
            + w_ref[2:3, cols] * x + b_ref[:, cols])


def _layer_bwd(dx2, dx2_b, s, mem, tabs, w, p, make_side):
    l = dx2.shape[0]
    rope_c, rope_sa, rope_sb = tabs
    x, x1 = s["x"], s["x1"]
    g = {}

    dact = _mm("ffn_down_dx", [(dx2_b, w["w_down"])], trans_b=True, bn_cap=1408)
    g["w_down"] = _mm_tn("ffn_down_dw", s["act"], dx2_b, bm_cap=1408).reshape(N_DEV, D_FF // N_DEV, D_MODEL)
    up = s["up"]
    nblk_c = l // min(256, l)

    bm_c = min(256, l)

    def conv_bwd(i, r, c):
        da_ref, xg_ref, xv_ref, hg_ref, hv_ref, dan_ref, ng_ref, nv_ref = r
        row8 = _row_ids(8)
        live_next = (i < nblk_c - 1).astype(F32)

        def conv_rows(x, xm1, xm2, cols):
            return c[0][0:1, cols] * xm2 + c[0][1:2, cols] * xm1 + c[0][2:3, cols] * x + c[1][:, cols]

        def tail_shift(x8, x_ref, k):
            out = pltpu.roll(x8, k, 0)
            for q in range(k):
                e = (row8 == q).astype(F32)
                out = out * (1.0 - e) + e * x_ref[bm_c - k + q:bm_c - k + q + 1, :]
            return out

        def glu_grads(da, cg, cv):
            sig = jax.nn.sigmoid(cg)
            return da * cv * (sig * (1.0 + cg * (1.0 - sig))), da * (cg * sig)

        def up_shift(d, d8, k):
            out = pltpu.roll(d, bm_c - k, 0)
            last = out[bm_c - 8:]
            for q in range(k):
                e = (row8 == 8 - k + q).astype(F32)
                nxt = jnp.sum(d8 * (row8 == q).astype(F32), axis=0, keepdims=True)
                last = last * (1.0 - e) + e * nxt
            return jnp.concatenate([out[:bm_c - 8], last], axis=0)

        halves = []
        for half, (x_ref, h_ref, n_ref) in enumerate(((xg_ref, hg_ref, ng_ref), (xv_ref, hv_ref, nv_ref))):
            cols = slice(half * D_FF, (half + 1) * D_FF)
            x, x8 = x_ref[...], n_ref[...]
            xm1, xm2 = _shift_down(x, h_ref, i, 1), _shift_down(x, h_ref, i, 2)
            halves.append((conv_rows(x, xm1, xm2, cols), x, xm1, xm2,
                           conv_rows(x8, tail_shift(x8, x_ref, 1), tail_shift(x8, x_ref, 2), cols)))
        (cg, xg, xg1, xg2, cg8), (cvv, xv, xv1, xv2, cv8) = halves
        dcg, dcv = glu_grads(da_ref[...], cg, cvv)
        dcg8, dcv8 = glu_grads(dan_ref[...] * live_next, cg8, cv8)
        outs, accs = [], []
        for half, (d, d8, (x0, xa, xb)) in enumerate(((dcg, dcg8, (xg, xg1, xg2)), (dcv, dcv8, (xv, xv1, xv2)))):
            cols = slice(half * D_FF, (half + 1) * D_FF)
            accs += [jnp.sum(d, axis=0, keepdims=True), jnp.sum(d * xb, axis=0, keepdims=True),
                     jnp.sum(d * xa, axis=0, keepdims=True), jnp.sum(d * x0, axis=0, keepdims=True)]
            outs.append(c[0][2:3, cols] * d + c[0][1:2, cols] * up_shift(d, d8, 1) + c[0][0:1, cols] * up_shift(d, d8, 2))
        return (*outs, *accs)

    conv_ins = [(dact, D_FF, 0, "row"), (up, D_FF, 0, "row"), (up, D_FF, 1, "row"), (up, D_FF, 0, "prev"),
                (up, D_FF, 1, "prev"), (dact, D_FF, 0, "next"), (up, D_FF, 0, "next"), (up, D_FF, 1, "next")]
    res = _rowwise("ffn_conv_glu_bwd", conv_bwd, l, bm_c, conv_ins, [s["conv_w"], s["conv_b"]],
                   [(D_FF, BF16), (D_FF, BF16)], [(1, D_FF)] * 8)
    dup_g, dup_v = res[0], res[1]
    db_g, dw0_g, dw1_g, dw2_g, db_v, dw0_v, dw1_v, dw2_v = res[2:]
    g["conv_b"] = jnp.concatenate([db_g, db_v], axis=1)[0]
    g["conv_w"] = _cols_to_rows(jnp.concatenate(
        [jnp.concatenate([dw0_g, dw0_v], axis=1), jnp.concatenate([dw1_g, dw1_v], axis=1),
         jnp.concatenate([dw2_g, dw2_v], axis=1)], axis=0))

    dh2 = _mm("ffn_up_dx", [(dup_g, w["w_up_g"]), (dup_v, w["w_up_v"])], trans_b=True, bm=512, bn_cap=1024)
    g["w_up"] = _pack_w_up(_mm_tn("ffn_up_dw_g", s["h2"], dup_g), _mm_tn("ffn_up_dw_v", s["h2"], dup_v))

    def rms_bwd_res(i, r, c):
        dx, dg = _rms_b(r[0][...], c[0][...], r[1][...], D_MODEL)
        dx = dx + r[2][...]
        return (dx, dx, dg)

    dx1, dx1_b, g["norm_ffn_g"] = _rowwise(
        "rms_ffn_bwd", rms_bwd_res, l, 1024, [(x1, D_MODEL, 0, "row"), (dh2, D_MODEL, 0, "row"), (dx2, D_MODEL, 0, "row")],
        [_row(p["norm_ffn_g"])], [(D_MODEL, F32), (D_MODEL, BF16)], [(1, D_MODEL)])

    dmerged = _mm("mix_out_dx", [(dx1_b, w["w_out"])], trans_b=True, bn_cap=1024)
    g["w_out"] = _mm_tn("mix_out_dw", s["merged"], dx1_b).reshape(N_DEV, D_MODEL // N_DEV, D_MODEL)
    gl, ya, yb, yc = s["gl"], s["ya"], s["yb"], s["yc"]

    def merge_bwd(i, r, c):
        dm = r[0][...]
        dys, dgs = [], []
        for b_ in range(3):
            gate = jax.nn.sigmoid(r[1 + b_][...] + c[0][:, b_ * D_MODEL:(b_ + 1) * D_MODEL])
            dys.append(dm * gate)
            dgs.append(dm * r[4 + b_][...] * (gate * (1.0 - gate)))
        dgl = jnp.concatenate(dgs, axis=1)
        return (*dys, dgl, jnp.sum(dgl, axis=0, keepdims=True))

    gate_ins = [(gl, D_MODEL, b_, "row") for b_ in range(3)]
    dya, dyb, dyc, dgl, db_gate = _rowwise(
        "merge_bwd", merge_bwd, l, 256,
        [(dmerged, D_MODEL, 0, "row")] + gate_ins + [(ya, D_MODEL, 0, "row"), (yb, D_MODEL, 0, "row"),
                                                     (yc, D_MODEL, 0, "row")],
        [_row(p["b_gate"])], [(D_MODEL, BF16)] * 3 + [(3 * D_MODEL, BF16)], [(1, 3 * D_MODEL)])
    g["b_gate"] = db_gate[0]

    do_c = _mm("cross_o_dx", [(dyc, w["w_o_cross"])], trans_b=True, out_dtype=BF16)
    g["w_o_cross"] = _cols_to_rows(_mm_tn("cross_o_dw", s["o_c"], dyc))
    kvm = s["kvm"]
    delta_c = _attn_delta("cross_attn_delta", s["o_c"], do_c, heads=X_HEADS, bq=2048)
    dxqn, dxk, dxv, _ = _attn_bwd("cross_attn_bwd", s["xqn"], s["xk"], s["xk"].T, kvm, do_c, s["lse_c"], delta_c,
                                  qoff=0, koff=0, voff=X_HEADS, heads=X_HEADS, causal=False,
                                  scale=X_HEAD_DIM ** -0.5, bq=1024, bk=256)
    ml = mem.shape[0]

    def headnorm_bwd(i, r, c):
        dxs, dg = [], None
        for hd in range(X_HEADS):
            cols = slice(hd * LANES, (hd + 1) * LANES)
            dx_h, dg_h = _rms_b(r[0][:, cols], c[0][...], r[1][:, cols], X_HEAD_DIM)
            dxs.append(dx_h)
            dg = dg_h if dg is None else dg + dg_h
        return (jnp.concatenate(dxs, axis=1), dg)

    dxq, dg_xq = _rowwise("cross_q_norm_bwd", headnorm_bwd, l, 512,
                          [(s["xq"], X_WIDTH, 0, "row"), (dxqn, X_WIDTH, 0, "row")], [_row(p["xq_norm_g"])],
                          [(X_WIDTH, BF16)], [(1, X_HEAD_DIM)])
    dkvm_k, dg_xk = _rowwise("cross_k_norm_bwd", headnorm_bwd, ml, 256,
                             [(kvm, X_WIDTH, 0, "row"), (dxk, X_WIDTH, 0, "row")], [_row(p["xk_norm_g"])],
                             [(X_WIDTH, F32)], [(1, X_HEAD_DIM)])
    g["xq_norm_g"], g["xk_norm_g"] = dg_xq[0], dg_xk[0]
    dkvm = jnp.concatenate([dkvm_k, dxv], axis=1)
    g["w_mem_kv"] = _mm_tn("cross_kv_dw", s["memn"], dkvm).reshape(N_DEV, D_MODEL // N_DEV, 2 * X_WIDTH)
    dmemn = _mm("cross_kv_dx", [(dkvm, w["w_mem_kv"])], trans_b=True, bn_cap=1024)

    def rms_bwd_gain_only(i, r, c):
        return (_rms_b(r[0][...], c[0][...], r[1][...], D_MODEL)[1],)

    (dg_mem,) = _rowwise("rms_mem_bwd", rms_bwd_gain_only, ml, 256,
                         [(mem, D_MODEL, 0, "row"), (dmemn, D_MODEL, 0, "row")], [_row(p["mem_norm_g"])], [],
                         [(1, D_MODEL)])
    g["mem_norm_g"] = dg_mem[0]

    dout_b = _mm("s5_o_dx", [(dyb, w["w_o_ssm"])], trans_b=True)
    g["w_o_ssm"] = _cols_to_rows(_mm_tn("s5_o_dw", s["out_b"], dyb))
    ypre, u, z = s["ypre"], s["u"], s["z"]
    d_row, b_glu = _row(p["ssm_d"]), _row(p["b_glu"])
    yuz = [(ypre, SSM_WIDTH, 0, "row"), (u, SSM_WIDTH, 0, "row"), (z, SSM_WIDTH, 0, "row")]

    def glu_bwd_z(i, r, c):
        y = _gelu(r[1][...] + c[0][...] * r[2][...])
        sg = jax.nn.sigmoid(r[3][...] + c[1][...])
        dz = r[0][...] * y * (sg * (1.0 - sg))
        return (dz, jnp.sum(dz, axis=0, keepdims=True))

    dz, db_glu = _rowwise("s5_glu_bwd_z", glu_bwd_z, l, 1024, [(dout_b, SSM_WIDTH, 0, "row")] + yuz, [d_row, b_glu],
                          [(SSM_WIDTH, BF16)], [(1, SSM_WIDTH)])
    g["b_glu"] = db_glu[0]
    g["w_glu"] = _mm_tn("s5_glu_dw", s["y_b"], dz).reshape(N_DEV, SSM_WIDTH // N_DEV, SSM_WIDTH)
    dy2 = _mm("s5_glu_dx", [(dz, w["w_glu"])], trans_b=True)

    def gelu_bwd(i, r, c):
        t = r[2][...] + c[0][...] * r[3][...]
        sg = jax.nn.sigmoid(r[4][...] + c[1][...])
        dt = (r[0][...] * sg + r[1][...]) * _gelu_grad(t)
        return (dt, c[0][...] * dt, jnp.sum(dt * r[3][...], axis=0, keepdims=True))

    dypre, du_skip, dd = _rowwise(
        "s5_gelu_bwd", gelu_bwd, l, 1024, [(dout_b, SSM_WIDTH, 0, "row"), (dy2, SSM_WIDTH, 0, "row")] + yuz,
        [d_row, b_glu], [(SSM_WIDTH, BF16), (SSM_WIDTH, F32)], [(1, SSM_WIDTH)])
    g["ssm_d"] = dd.reshape(SSM_GROUPS, SSM_GROUP_CH)
    cc_re_d, cc_imn_d = s["cc_d"]
    bb_re_d, bb_im_d = s["bb_d"]
    a_re_row, a_im_row = s["a_rows"]
    s_re, s_im = s["s_re"], s["s_im"]
    du, da_re, da_im, dbb_re_d, dbb_im_d, dcc_re, dcc_imn = _s5_bwd(
        "s5_bwd", dypre, du_skip, u, s_re, s_im, bb_re_d, bb_im_d, cc_re_d, cc_imn_d, a_re_row, -a_im_row)
    g["ssm_c_re"] = _diag_blocks(dcc_re, SSM_STATE, SSM_GROUP_CH).transpose(0, 2, 1)
    g["ssm_c_im"] = -_diag_blocks(dcc_imn, SSM_STATE, SSM_GROUP_CH).transpose(0, 2, 1)
    dbb_re = _diag_blocks(dbb_re_d, SSM_GROUP_CH, SSM_STATE).transpose(0, 2, 1).reshape(SSM_N, SSM_GROUP_CH)
    dbb_im = _diag_blocks(dbb_im_d, SSM_GROUP_CH, SSM_STATE).transpose(0, 2, 1).reshape(SSM_N, SSM_GROUP_CH)
    dlr, dli, dldt, dbr, dbi = _disc_bwd(*s["disc"], da_re.reshape(SSM_N, 1), da_im.reshape(SSM_N, 1), dbb_re, dbb_im)
    g["ssm_lambda_re"] = dlr.reshape(SSM_GROUPS, SSM_STATE)
    g["ssm_lambda_im"] = dli.reshape(SSM_GROUPS, SSM_STATE)
    g["ssm_log_dt"] = dldt.reshape(SSM_GROUPS, SSM_STATE).sum(axis=1)
    g["ssm_b_re"] = dbr.reshape(SSM_GROUPS, SSM_STATE, SSM_GROUP_CH)
    g["ssm_b_im"] = dbi.reshape(SSM_GROUPS, SSM_STATE, SSM_GROUP_CH)

    do_a = _mm("mla_o_dx", [(dya, w["wo_mla"])], trans_b=True, out_dtype=BF16, bn_cap=1024)
    dwo = _mm_tn("mla_o_dw", s["o_a"], dya)
    g["w_o_mla"] = _cols_to_rows(dwo.reshape(MLA_HEADS, LANES, D_MODEL)[:, :D_V].reshape(MLA_HEADS * D_V, D_MODEL))
    delta_a = _attn_delta("mla_attn_delta", s["o_a"], do_a, heads=MLA_HEADS, bq=2048)
    dq, dk, dv, rode = _attn_bwd("mla_attn_bwd", s["q"], s["k"], s["k"].T, s["v_mla"], do_a, s["lse_a"], delta_a,
                                 qoff=0, koff=0, voff=0, heads=MLA_HEADS, causal=True, scale=D_QK ** -0.5, bq=1024,
                                 bk=1024, side=make_side(g))
    hw = MLA_HEADS * LANES
    kr_blk = (Q_LORA + KV_LORA) // LANES
    pqkv = s["pqkv"]
    g_q, g_k = _pad_lanes(p["q_norm_g"]), _pad_lanes(p["k_norm_g"])
    lane = lax.broadcasted_iota(jnp.int32, (1, LANES), 1)
    kr_mask = jnp.logical_and(lane >= KR_LO, lane < KR_LO + D_ROPE).astype(F32)

    def prep_b_bwd(i, r, c):
        dq_ref, dk_ref, q_ref, k_ref, kr_ref, c_ref, sa_ref, sb_ref = r
        rc, sa, sb, kr = c_ref[...], sa_ref[...], sb_ref[...], kr_ref[...]
        dqs, dks, dkr, dgq, dgk = [], [], None, None, None
        for hd in range(MLA_HEADS):
            cols = slice(hd * LANES, (hd + 1) * LANES)
            dxq, dgq_h = _rms_b(q_ref[:, cols], c[0][...], _rope_b(dq_ref[:, cols], rc, sa, sb), D_QK)
            dxk, dgk_h = _rms_b(k_ref[:, cols] + kr, c[1][...], _rope_b(dk_ref[:, cols], rc, sa, sb), D_QK)
            dqs.append(dxq)
            dks.append(dxk)
            dkr = dxk if dkr is None else dkr + dxk
            dgq = dgq_h if dgq is None else dgq + dgq_h
            dgk = dgk_h if dgk is None else dgk + dgk_h
        return (jnp.concatenate(dqs, axis=1), jnp.concatenate(dks, axis=1), dkr * c[2][...], dgq, dgk)

    tab_ins = [(rope_c, LANES, 0, "row"), (rope_sa, LANES, 0, "row"), (rope_sb, LANES, 0, "row")]
    dq_raw, dk_raw, dkr, dg_q, dg_k = _rowwise(
        "mla_prep_b_bwd", prep_b_bwd, l, 256,
        [(dq, hw, 0, "row"), (dk, hw, 0, "row"), (s["q_raw"], hw, 0, "row"), (s["k_raw"], hw, 0, "row"),
         (pqkv, LANES, kr_blk, "row")] + tab_ins, [g_q, g_k, kr_mask],
        [(hw, BF16), (hw, BF16), (LANES, F32)], [(1, LANES), (1, LANES)])
    g["q_norm_g"], g["k_norm_g"] = dg_q[0, :D_QK], dg_k[0, :D_QK]
    dnq = _mm("mla_q_b_dx", [(dq_raw, w["wqb"])], trans_b=True)
    dnkv = _mm("mla_kv_b_dx", [(dk_raw, w["wk"]), (dv, w["wv"])], trans_b=True)
    dwqb = _mm_tn("mla_q_b_dw", s["nq"], dq_raw)
    g["w_q_b"] = _cols_to_rows(dwqb)
    dwk = _mm_tn("mla_k_b_dw", s["nkv"], dk_raw).reshape(KV_LORA, MLA_HEADS, LANES)[:, :, :D_NOPE]
    dwv = _mm_tn("mla_v_b_dw", s["nkv"], dv).reshape(KV_LORA, MLA_HEADS, LANES)[:, :, :D_V]
    g["w_kv_b"] = jnp.concatenate([dwk, dwv], axis=2).transpose(1, 0, 2)

    def prep_a_bwd(i, r, c):
        dcq, dgqa = _rms_b(r[0][:, :Q_LORA], c[0][...], r[1][...], Q_LORA)
        dckv, dgkva = _rms_b(r[0][:, Q_LORA:Q_LORA + KV_LORA], c[1][...], r[2][...], KV_LORA)
        return (jnp.concatenate([dcq, dckv, r[3][...]], axis=1), dgqa, dgkva)

    dpqkv, dg_qa, dg_kva = _rowwise(
        "mla_prep_a_bwd", prep_a_bwd, l, 1024,
        [(pqkv, QKV_W, 0, "row"), (dnq, Q_LORA, 0, "row"), (dnkv, KV_LORA, 0, "row"), (dkr, LANES, 0, "row")],
        [_row(p["q_a_norm_g"]), _row(p["kv_a_norm_g"])], [(QKV_W, BF16)], [(1, Q_LORA), (1, KV_LORA)])
    g["q_a_norm_g"], g["kv_a_norm_g"] = dg_qa[0], dg_kva[0]

    h = s["h"]
    dh = _mm("proj_dx", [(dpqkv, w["wqkv"]), (du, w["w_u"]), (dxq, w["w_xq"]), (dgl, w["w_g"])], trans_b=True,
             bm=512, bn_cap=1024)
    dwqkv = _mm_tn("proj_qkv_dw", h, dpqkv)
    g["w_in"] = _pack_w_in([dwqkv, _mm_tn("proj_u_dw", h, du), _mm_tn("proj_xq_dw", h, dxq),
                            _mm_tn("proj_gate_dw", h, dgl)])
    dx, dx_b, dg_mix = _rowwise(
        "rms_mix_bwd", rms_bwd_res, l, 1024, [(x, D_MODEL, 0, "row"), (dh, D_MODEL, 0, "row"), (dx1, D_MODEL, 0, "row")],
        [_row(p["norm_mix_g"])], [(D_MODEL, F32), (D_MODEL, BF16)], [(1, D_MODEL)])
    g["norm_mix_g"] = dg_mix[0]
    g["norm_ffn_g"] = g["norm_ffn_g"][0]
    return dx, dx_b, g, rode


def _rope_tables(positions):
    inv_freq = ROPE_THETA ** (-jnp.arange(0, D_ROPE, 2, dtype=F32) / D_ROPE)
    ang = positions.astype(F32)[:, None] * inv_freq
    cos, sin = jnp.cos(ang), jnp.sin(ang)
    l = positions.shape[0]
    one, zero = jnp.ones((l, D_NOPE), F32), lambda n: jnp.zeros((l, n), F32)
    pad = LANES - D_QK
    rope_c = jnp.concatenate([one, cos, cos, zero(pad)], axis=1)
    rope_sa = jnp.concatenate([zero(D_NOPE), -sin, zero(16), zero(pad)], axis=1)
    rope_sb = jnp.concatenate([zero(D_NOPE + 16), sin, zero(pad)], axis=1)
    return rope_c, rope_sa, rope_sb


def kernel(x, mem, positions, norm_mix_g, w_in, q_a_norm_g, w_q_b, kv_a_norm_g, w_kv_b, q_norm_g, k_norm_g, w_o_mla, ssm_lambda_re, ssm_lambda_im, ssm_log_dt, ssm_b_re, ssm_b_im, ssm_c_re, ssm_c_im, ssm_d, w_glu, b_glu, w_o_ssm, mem_norm_g, w_mem_kv, xq_norm_g, xk_norm_g, w_o_cross, b_gate, w_out, norm_ffn_g, w_up, conv_w, conv_b, w_down, loss_target, m_norm_mix_g, m_w_in, m_q_a_norm_g, m_w_q_b, m_kv_a_norm_g, m_w_kv_b, m_q_norm_g, m_k_norm_g, m_w_o_mla, m_ssm_lambda_re, m_ssm_lambda_im, m_ssm_log_dt, m_ssm_b_re, m_ssm_b_im, m_ssm_c_re, m_ssm_c_im, m_ssm_d, m_w_glu, m_b_glu, m_w_o_ssm, m_mem_norm_g, m_w_mem_kv, m_xq_norm_g, m_xk_norm_g, m_w_o_cross, m_b_gate, m_w_out, m_norm_ffn_g, m_w_up, m_conv_w, m_conv_b, m_w_down, v_norm_mix_g, v_w_in, v_q_a_norm_g, v_w_q_b, v_kv_a_norm_g, v_w_kv_b, v_q_norm_g, v_k_norm_g, v_w_o_mla, v_ssm_lambda_re, v_ssm_lambda_im, v_ssm_log_dt, v_ssm_b_re, v_ssm_b_im, v_ssm_c_re, v_ssm_c_im, v_ssm_d, v_w_glu, v_b_glu, v_w_o_ssm, v_mem_norm_g, v_w_mem_kv, v_xq_norm_g, v_xk_norm_g, v_w_o_cross, v_b_gate, v_w_out, v_norm_ffn_g, v_w_up, v_conv_w, v_conv_b, v_w_down):
    a = dict(locals())
    wts = {n: a[n] for n in WEIGHT_ORDER}
    m_in = {n: a["m_" + n] for n in WEIGHT_ORDER}
    v_in = {n: a["v_" + n] for n in WEIGHT_ORDER}
    depth = norm_mix_g.shape[0]
    x0, mem0, pos0, tgt = x[0], mem[0], positions[0], loss_target[0]
    l = x0.shape[0]
    offs = {grp: _group_rows(params) for grp, _, params in GROUPS}
    early_names, late_names = [g[0] for g in EARLY_GROUPS], [g[0] for g in LATE_GROUPS]

    def srcs(i, groups):
        return [_group_local(width, params, wts, i, F32 if grp == "gconv" else BF16) for grp, width, params in groups]

    tabs = _rope_tables(pos0)
    layer_p = [{n: wts[n][i] for n in REPLICATED} for i in range(depth)]
    gath = dict(zip(early_names, _gather_all("gather_weights", srcs(0, EARLY_GROUPS))))

    def late(rode):
        return _late_weights(dict(zip(late_names, rode)), offs), rode[len(late_names):]

    saved, layer_w = [], []
    xc = x0
    for i in range(depth):
        ride = srcs(i, LATE_GROUPS) + (srcs(i + 1, EARLY_GROUPS) if i + 1 < depth else [])
        side = _Side(ride, _gather_shapes(ride), GATHER_SEMS, (_gather_start, _gather_relay, _gather_finish))
        xc, s, w_all, rode = _layer_fwd(xc, mem0, tabs, _early_weights(gath, offs), layer_p[i], side, late)
        gath = dict(zip(early_names, rode))
        layer_w.append(w_all)
        saved.append(s)

    def loss_fn(i, r, c):
        d = r[0][...] - r[1][...]
        dy = d * (1.0 / D_MODEL)
        return (dy, dy, jnp.sum(d * d, axis=0, keepdims=True))

    dy, dy_b, sq = _rowwise("loss", loss_fn, l, 1024, [(xc, D_MODEL, 0, "row"), (tgt, D_MODEL, 0, "row")], [],
                            [(D_MODEL, F32), (D_MODEL, BF16)], [(1, D_MODEL)])
    loss = lax.psum(0.5 * jnp.sum(sq) / D_MODEL, ("x", "y", "c"))

    def core_sums(g, tag, groups, extra=()):
        sends = []
        for _, width, params in groups:
            blocks = [jnp.pad(g[n], ((0, 0), (0, _rows8(rows) - rows), (0, width - g[n].shape[2])))
                      for n, rows, _ in params]
            sends.append(jnp.concatenate(blocks, axis=1))
        sends += list(extra)
        got = _pair_exchange("exchange_grads_core_" + tag, sends)
        return [_pair_sum("grad_pair_sum_%s_%d" % (tag, k), s_, g_) for k, (s_, g_) in enumerate(zip(sends, got))]

    grads, rcvs = [None] * depth, {}
    dxc, dxc_b = dy, dy_b
    waiting = []
    for i in reversed(range(depth)):
        def make_side(g, i=i, waiting=waiting):
            parts = core_sums(g, "l%d_late" % i, LATE_GROUPS) + waiting
            return _Side(parts, _chip_shapes(parts), CHIP_SEMS, (_chip_start, _chip_finish))

        dxc, dxc_b, grads[i], rode = _layer_bwd(dxc, dxc_b, saved[i], mem0, tabs, layer_w[i], layer_p[i], make_side)
        rcvs.update({(i, nm): r for nm, r in zip(late_names, rode)})
        rcvs.update({(i + 1, nm): r for nm, r in zip(early_names, rode[len(late_names):])})
        waiting = core_sums(grads[i], "l%d_early" % i, EARLY_GROUPS) if i > 0 else []
    grad_x = dxc[None]
    rep_flat = jnp.concatenate([jnp.stack([grads[i][n] for i in range(depth)]).reshape(-1) for n in REPLICATED])
    rep_send = _rep_rows(jnp.broadcast_to(rep_flat[None], (N_DEV, rep_flat.shape[0])))
    last = _chip_exchange("exchange_grads_chip", core_sums(grads[0], "l0_early", EARLY_GROUPS, [rep_send]))
    rcvs.update({(0, nm): r for nm, r in zip(early_names, last)})
    rcv_rep = last[-1]

    per_layer = {}
    for i in range(depth):
        for grp, width, params in GROUPS:
            local = [_group_local(width, params, d, i, F32) for d in (wts, m_in, v_in)]
            res = _adamw("adamw_l%d_%s" % (i, grp), rcvs[(i, grp)], *local)
            for tag, arr in zip(("grad", "delta", "m", "v"), res):
                for n, rows, cols in params:
                    per_layer[(tag, n, i)] = arr[offs[grp][n]:offs[grp][n] + rows, :cols]
    outs = {(tag, n): jnp.stack([per_layer[(tag, n, i)] for i in range(depth)])
            for tag in ("grad", "delta", "m", "v") for _, _, params in GROUPS for n, _, _ in params}
    rep_local = [_rep_rows(jnp.concatenate([d[n].astype(F32).reshape(-1) for n in REPLICATED])) for d in (wts, m_in, v_in)]
    res = _adamw("adamw_rep", rcv_rep, *rep_local)
    for tag, arr in zip(("grad", "delta", "m", "v"), res):
        flat, off = arr.reshape(-1), 0
        for n in REPLICATED:
            cnt = wts[n].size
            outs[(tag, n)] = flat[off:off + cnt].reshape(wts[n].shape)
            off += cnt
    result = [loss, grad_x]
    for tag in ("grad", "delta", "m", "v"):
        result += [outs[(tag, n)] for n in WEIGHT_ORDER]
    return tuple(result)
```
